```python
import jax, jax.numpy as jnp
from jax import lax
import numpy as np

D_MODEL = 1024
BATCH = 8
SEQ = 2048
DEPTH = 1

CONV_GROUPS = 8
CONV_GROUP_DIM = 64
D_CONV = CONV_GROUPS * CONV_GROUP_DIM
CONV_WIDTH = 3
RWKV_HEADS = 8
RWKV_HEAD = 64
D_RWKV = RWKV_HEADS * RWKV_HEAD
D_MIX = D_CONV + D_RWKV
DECAY_LORA = 64
AAA_LORA = 64
GATE_LORA = 128
D_RWKV_PROJ = 3 * D_RWKV + DECAY_LORA + AAA_LORA + GATE_LORA
D_IN = 3 * D_CONV + D_RWKV_PROJ
N_GROUPS = 4
EXPERTS_PER_GROUP = 8
N_EXPERTS = N_GROUPS * EXPERTS_PER_GROUP
TOP_K_IN_GROUP = 2
D_EXPERT = D_MODEL // 4
RMS_EPS = 1e-6
LN_X_EPS = 64e-5
L2_EPS = 1e-12

kernel_name = "hybrid_conv_rwkv7_hmoe_block"


def rms_norm(x, g):
    xf = x.astype(jnp.float32)
    y = xf * lax.rsqrt(jnp.mean(xf * xf, axis=-1, keepdims=True) + RMS_EPS)
    return y.astype(x.dtype) * g


def token_shift(z):
    return jnp.pad(z, ((0, 0), (1, 0), (0, 0)))[:, :-1]


def causal_depthwise_conv(u, w):
    return lax.conv_general_dilated(
        u, w[:, None, :].astype(u.dtype), window_strides=(1,), padding=[(CONV_WIDTH - 1, 0)],
        dimension_numbers=("NWC", "WIO", "NWC"), feature_group_count=u.shape[-1])


def short_conv_mixer(z, conv_w):
    b_gate, c_gate, h = jnp.split(z, 3, axis=-1)
    return b_gate * causal_depthwise_conv(c_gate * h, conv_w)


def rwkv7_recurrence(r, decay, k, v, kk, a):
    bsz = r.shape[0]
    to_time = lambda t: jnp.moveaxis(t.astype(jnp.float32), 1, 0)

    def step(S, inp):
        r_t, w_t, k_t, v_t, kk_t, a_t = inp
        s_kk = jnp.einsum("bhvk,bhk->bhv", S, -kk_t)
        S = (S * w_t[:, :, None, :]
             + s_kk[..., None] * (kk_t * a_t)[:, :, None, :]
             + v_t[..., None] * k_t[:, :, None, :])
        o_t = jnp.einsum("bhvk,bhk->bhv", S, r_t)
        return S, o_t

    S0 = jnp.zeros((bsz, RWKV_HEADS, RWKV_HEAD, RWKV_HEAD), jnp.float32)
    _, o = lax.scan(step, S0, (to_time(r), to_time(decay), to_time(k),
                               to_time(v), to_time(kk), to_time(a)))
    return jnp.moveaxis(o, 0, 1)


def rwkv7_mixer(z, mu, decay_up, decay_base, aaa_up, aaa_base, gate_up,
                k_k, k_a, r_k, ln_w, ln_b):
    bsz, seq, _ = z.shape
    zs = z + (token_shift(z) - z) * mu
    cuts = [D_RWKV, 2 * D_RWKV, 3 * D_RWKV, 3 * D_RWKV + DECAY_LORA,
            3 * D_RWKV + DECAY_LORA + AAA_LORA]
    r, k, v, w_lo, a_lo, g_lo = jnp.split(zs, cuts, axis=-1)
    w = -jax.nn.softplus(-(decay_base + jnp.tanh(w_lo) @ decay_up)) - 0.5
    decay = jnp.exp(-jnp.exp(w.astype(jnp.float32)))
    a = jax.nn.sigmoid(aaa_base + a_lo @ aaa_up)
    g = jax.nn.sigmoid(g_lo) @ gate_up
    heads = lambda t: t.reshape(bsz, seq, RWKV_HEADS, RWKV_HEAD)
    kk = heads(k * k_k).astype(jnp.float32)
    kk = kk / jnp.maximum(jnp.linalg.norm(kk, axis=-1, keepdims=True), L2_EPS)
    k = k * (1.0 + (a - 1.0) * k_a)
    rh, kh, vh, ah = heads(r), heads(k), heads(v), heads(a)
    o = rwkv7_recurrence(rh, heads(decay), kh, vh, kk, ah)
    mean = jnp.mean(o, axis=-1, keepdims=True)
    var = jnp.mean(jnp.square(o - mean), axis=-1, keepdims=True)
    o = ((o - mean) * lax.rsqrt(var + LN_X_EPS)).astype(z.dtype)
    o = o * ln_w.reshape(RWKV_HEADS, RWKV_HEAD) + ln_b.reshape(RWKV_HEADS, RWKV_HEAD)
    bonus = jnp.sum(rh * kh * r_k, axis=-1, keepdims=True) * vh
    return (o + bonus).reshape(bsz, seq, D_RWKV) * g


def hierarchical_moe(h, group_w, group_b, expert_w, expert_b, w_gate, w_up, w_down):
    shape = h.shape
    t = h.reshape(-1, shape[-1])
    n_tok = t.shape[0]
    group_prob = jax.nn.softmax((t @ group_w + group_b).astype(jnp.float32), axis=-1)
    g_prob, g_idx = lax.top_k(group_prob, 1)
    e_logits = (t @ expert_w + expert_b).astype(jnp.float32)
    e_logits = e_logits.reshape(n_tok, N_GROUPS, EXPERTS_PER_GROUP)
    in_group = e_logits[jnp.arange(n_tok), g_idx[:, 0]]
    top_logit, top_idx = lax.top_k(in_group, TOP_K_IN_GROUP)
    gates = g_prob * jax.nn.softmax(top_logit, axis=-1)
    expert_idx = g_idx * EXPERTS_PER_GROUP + top_idx
    combine = jnp.sum(jax.nn.one_hot(expert_idx, N_EXPERTS, dtype=jnp.float32)
                      * gates[..., None], axis=1).astype(h.dtype)
    out = jnp.zeros_like(t)
    for e in range(N_EXPERTS):
        hid = jax.nn.silu(t @ w_gate[e]) * (t @ w_up[e])
        out = out + combine[:, e:e + 1] * (hid @ w_down[e])
    return out.reshape(shape)


def setup_inputs(seed: int = 0) -> dict:
    key = jax.random.key(seed)
    ks = jax.random.split(key, 26)
    nrm = lambda k, s, sc: jax.random.normal(k, s, jnp.float32) * sc
    L = DEPTH
    return {
        "x": nrm(ks[0], (BATCH, SEQ, D_MODEL), 1.0),
        "norm_mix_g": 1.0 + nrm(ks[1], (L, D_MODEL), 0.02),
        "w_in": nrm(ks[2], (L, D_MODEL, D_IN), D_MODEL ** -0.5),
        "rwkv_mu": jax.random.uniform(ks[3], (L, D_RWKV_PROJ), jnp.float32),
        "conv_w": nrm(ks[4], (L, CONV_WIDTH, D_CONV), CONV_WIDTH ** -0.5),
        "decay_up": nrm(ks[5], (L, DECAY_LORA, D_RWKV), 0.5 * DECAY_LORA ** -0.5),
        "decay_base": jax.random.uniform(ks[6], (L, D_RWKV), jnp.float32, -6.0, 1.0),
        "aaa_up": nrm(ks[7], (L, AAA_LORA, D_RWKV), 0.5 * AAA_LORA ** -0.5),
        "aaa_base": nrm(ks[8], (L, D_RWKV), 0.1),
        "gate_up": nrm(ks[9], (L, GATE_LORA, D_RWKV), GATE_LORA ** -0.5),
        "k_k": 0.85 + nrm(ks[10], (L, D_RWKV), 0.05),
        "k_a": 1.0 + nrm(ks[11], (L, D_RWKV), 0.05),
        "r_k": nrm(ks[12], (L, RWKV_HEADS, RWKV_HEAD), 0.1),
        "ln_x_w": 1.0 + nrm(ks[13], (L, D_RWKV), 0.02),
        "ln_x_b": nrm(ks[14], (L, D_RWKV), 0.02),
        "w_out": nrm(ks[15], (L, D_MIX, D_MODEL), D_MIX ** -0.5),
        "norm_ffn_g": 1.0 + nrm(ks[16], (L, D_MODEL), 0.02),
        "router_group_w": nrm(ks[17], (L, D_MODEL, N_GROUPS), D_MODEL ** -0.5),
        "router_group_b": nrm(ks[18], (L, N_GROUPS), 0.01),
        "router_expert_w": nrm(ks[19], (L, D_MODEL, N_EXPERTS), D_MODEL ** -0.5),
        "router_expert_b": nrm(ks[20], (L, N_EXPERTS), 0.01),
        "expert_w_gate": nrm(ks[21], (L, N_EXPERTS, D_MODEL, D_EXPERT), D_MODEL ** -0.5),
        "expert_w_up": nrm(ks[22], (L, N_EXPERTS, D_MODEL, D_EXPERT), D_MODEL ** -0.5),
        "expert_w_down": nrm(ks[23], (L, N_EXPERTS, D_EXPERT, D_MODEL), D_EXPERT ** -0.5),
        "norm_final_g": 1.0 + nrm(ks[24], (D_MODEL,), 0.02),
    }


def reference(x, norm_mix_g, w_in, rwkv_mu, conv_w, decay_up, decay_base, aaa_up,
              aaa_base, gate_up, k_k, k_a, r_k, ln_x_w, ln_x_b, w_out, norm_ffn_g,
              router_group_w, router_group_b, router_expert_w, router_expert_b,
              expert_w_gate, expert_w_up, expert_w_down, norm_final_g):
    h = x
    for l in range(DEPTH):
        u = rms_norm(h, norm_mix_g[l])
        z = u @ w_in[l]
        y_conv = short_conv_mixer(z[..., :3 * D_CONV], conv_w[l])
        y_rwkv = rwkv7_mixer(z[..., 3 * D_CONV:], rwkv_mu[l], decay_up[l], decay_base[l],
                             aaa_up[l], aaa_base[l], gate_up[l], k_k[l], k_a[l], r_k[l],
                             ln_x_w[l], ln_x_b[l])
        y = jnp.concatenate([y_conv, y_rwkv], axis=-1)
        h = h + y @ w_out[l]
        u = rms_norm(h, norm_ffn_g[l])
        h = h + hierarchical_moe(u, router_group_w[l], router_group_b[l],
                                 router_expert_w[l], router_expert_b[l],
                                 expert_w_gate[l], expert_w_up[l], expert_w_down[l])
    return rms_norm(h, norm_final_g)
```

```python
import functools

import jax
import jax.numpy as jnp
from jax import lax
from jax.experimental import pallas as pl
from jax.experimental.pallas import tpu as pltpu

F32 = jnp.float32
BF16 = jnp.bfloat16

D_MODEL = 1024
D_CONV = 512
CONV_WIDTH = 3
N_HEADS = 8
HEAD = 64
D_RWKV = N_HEADS * HEAD
LORA_WA = 128
GATE_LORA = 128
D_RWKV_PROJ = 3 * D_RWKV + LORA_WA + GATE_LORA
D_IN = 3 * D_CONV + D_RWKV_PROJ
N_GROUPS = 4
EXPERTS_PER_GROUP = 8
N_EXPERTS = N_GROUPS * EXPERTS_PER_GROUP
D_EXPERT = D_MODEL // 4
RMS_EPS = 1e-6
LN_X_EPS = 64e-5
L2_EPS = 1e-12

LANES = 128
CHUNK = 64
QUAD = 4 * HEAD
ROUTER_LANES = 128

VMEM_LIMIT = 48 * 1024 * 1024


def _bf(x):
    return x.astype(BF16)


def _mm(a, b):
    return jnp.dot(_bf(a), _bf(b), preferred_element_type=F32)


def _mm_nt(a, b):
    return lax.dot_general(_bf(a), _bf(b), (((1,), (1,)), ((), ())), preferred_element_type=F32)


def _mm_exact_lhs(lhs_bf16, x, passes):
    acc = None
    rem = x
    for _ in range(passes):
        piece = _bf(rem)
        part = jnp.dot(lhs_bf16, piece, preferred_element_type=F32)
        acc = part if acc is None else acc + part
        rem = rem - piece.astype(F32)
    return acc


def _mm_exact_rhs(x, rhs_bf16, passes):
    acc = None
    rem = x
    for _ in range(passes):
        piece = _bf(rem)
        part = jnp.dot(piece, rhs_bf16, preferred_element_type=F32)
        acc = part if acc is None else acc + part
        rem = rem - piece.astype(F32)
    return acc


def _rms_norm(x, g):
    return x * lax.rsqrt(jnp.mean(x * x, axis=-1, keepdims=True) + RMS_EPS) * g


def _shift_rows(cur, prev_rows, k):
    rolled = pltpu.roll(cur, k, 0)
    prev_rolled = pltpu.roll(prev_rows, k, 0)
    n = cur.shape[0]
    head = jnp.concatenate([prev_rolled, rolled[8:]], axis=0) if n > 8 else prev_rolled
    row = lax.broadcasted_iota(jnp.int32, cur.shape, 0)
    return jnp.where(row < k, head, rolled)


def _in_kernel(x_ref, g_ref, w_ref, mu_ref, convw_ref, dup_ref, dbase_ref, aup_ref, abase_ref,
               gup_ref, kk_ref, ka_ref, rk_ref, ones_ref,
               yconv_ref, r_ref, lw_ref, k_ref, v_ref, kkn_ref, kka_ref, gate_ref, bonus_ref,
               carry_ref):
    @pl.when(pl.program_id(1) == 0)
    def _():
        carry_ref[...] = jnp.zeros_like(carry_ref)

    x = x_ref[0]
    u = _rms_norm(x, g_ref[...])
    z = jnp.dot(_bf(u), w_ref[...], preferred_element_type=F32)

    b_gate = z[:, :D_CONV]
    ch = z[:, D_CONV:2 * D_CONV] * z[:, 2 * D_CONV:3 * D_CONV]
    prev_ch = carry_ref[:, :D_CONV]
    conv = (convw_ref[2:3, :] * ch
            + convw_ref[1:2, :] * _shift_rows(ch, prev_ch, 1)
            + convw_ref[0:1, :] * _shift_rows(ch, prev_ch, 2))
    yconv_ref[0] = b_gate * conv

    zr = z[:, 3 * D_CONV:]
    prev_zr = carry_ref[:, D_CONV:]
    zs = zr + (_shift_rows(zr, prev_zr, 1) - zr) * mu_ref[...]
    carry_ref[:, :D_CONV] = ch[-8:]
    carry_ref[:, D_CONV:] = zr[-8:]

    r = zs[:, :D_RWKV]
    k = zs[:, D_RWKV:2 * D_RWKV]
    v = zs[:, 2 * D_RWKV:3 * D_RWKV]
    wa_lo = zs[:, 3 * D_RWKV:3 * D_RWKV + LORA_WA]
    g_lo = zs[:, 3 * D_RWKV + LORA_WA:]

    dec_in = -(dbase_ref[...] + jnp.dot(_bf(jnp.tanh(wa_lo)), dup_ref[...], preferred_element_type=F32))
    softplus = jnp.maximum(dec_in, 0.0) + jnp.log(1.0 + jnp.exp(-jnp.abs(dec_in)))
    w = -softplus - 0.5
    lw_ref[0] = -jnp.exp(w)
    a = jax.nn.sigmoid(abase_ref[...] + jnp.dot(_bf(wa_lo), aup_ref[...], preferred_element_type=F32))
    gate_ref[0] = jnp.dot(_bf(jax.nn.sigmoid(g_lo)), gup_ref[...], preferred_element_type=F32)

    ones_bd = ones_ref[...]
    kk = k * kk_ref[...]
    norm = jnp.sqrt(_mm_exact_rhs(kk * kk, ones_bd, 2))
    kk = kk / jnp.maximum(norm, L2_EPS)
    k2 = k * (1.0 + (a - 1.0) * ka_ref[...])
    r_ref[0] = r
    k_ref[0] = k2
    v_ref[0] = v
    kkn_ref[0] = kk
    kka_ref[0] = kk * a
    bonus_ref[0] = _mm_exact_rhs(r * k2 * rk_ref[...], ones_bd, 2) * v


def _in_call(x, g, w_in, mu, conv_w, dup, dbase, aup, abase, gup, k_k, k_a, r_k, ones_bd, tm):
    bsz, seq, _ = x.shape
    full = lambda arr: pl.BlockSpec(arr.shape, lambda b, t: (0,) * arr.ndim)
    out_spec = pl.BlockSpec((1, tm, D_RWKV), lambda b, t: (b, t, 0))
    out_shape = jax.ShapeDtypeStruct((bsz, seq, D_RWKV), F32)
    params = (g, w_in, mu, conv_w, dup, dbase, aup, abase, gup, k_k, k_a, r_k, ones_bd)
    return pl.pallas_call(
        _in_kernel,
        grid=(bsz, seq // tm),
        in_specs=[pl.BlockSpec((1, tm, D_MODEL), lambda b, t: (b, t, 0))] + [full(p) for p in params],
        out_specs=[out_spec] * 9,
        out_shape=[out_shape] * 9,
        scratch_shapes=[pltpu.VMEM((8, D_CONV + D_RWKV_PROJ), F32)],
        compiler_params=pltpu.CompilerParams(
            dimension_semantics=("arbitrary", "arbitrary"), vmem_limit_bytes=VMEM_LIMIT),
        name="in_proj",
    )(x, *params)


def _block_diag(y, bd_mask):
    return jnp.where(bd_mask, jnp.concatenate([y] * (QUAD // CHUNK), axis=0), 0.0)


def _unit_lower_inverse(a_strict, t_idx, s_idx, bd_mask):
    bdmm = lambda x, y: _mm(x, _block_diag(y, bd_mask))
    eye = (t_idx == s_idx).astype(F32)
    a8 = jnp.where((t_idx // 8) == (s_idx // 8), a_strict, 0.0)
    a8_2 = bdmm(a8, a8)
    a8_3 = bdmm(a8, a8_2)
    a8_4 = bdmm(a8_2, a8_2)
    inv = eye + a8 + a8_2 + a8_3
    inv = inv + bdmm(inv, a8_4)
    size = 16
    while size <= CHUNK:
        off = ((t_idx // size) == (s_idx // size)) & ((t_idx // (size // 2)) != (s_idx // (size // 2)))
        cross = bdmm(jnp.where(off, a_strict, 0.0), inv)
        inv = inv + bdmm(inv, cross)
        size *= 2
    return inv


def _chunk_local(r, lw, k, v, kk, kka, tri, t_idx, s_idx, bd_mask):
    bdmm = lambda x, y: _mm(x, _block_diag(y, bd_mask))
    cum = _mm_exact_lhs(tri, lw, 3)
    cum_last = cum[CHUNK - 1:CHUNK, :]
    p_incl = jnp.exp(cum)
    p_excl = jnp.exp(cum - lw)
    p_inv = jnp.exp(-cum)
    to_end = jnp.exp(cum_last - cum)
    a_t = -kk * p_excl
    r_t = r * p_incl
    b_t = kka * p_inv
    k_t = k * p_inv
    b_end = kka * to_end
    k_end = k * to_end

    ar = jnp.concatenate([a_t, r_t], axis=0)
    row2 = lax.broadcasted_iota(jnp.int32, (2 * CHUNK, QUAD), 0)
    col2 = lax.broadcasted_iota(jnp.int32, (2 * CHUNK, QUAD), 1) % CHUNK
    causal = col2 < (row2 % CHUNK) + (row2 // CHUNK)
    with_b = jnp.where(causal, _mm_nt(ar, _block_diag(b_t, bd_mask)), 0.0)
    with_k = jnp.where(causal, _mm_nt(ar, _block_diag(k_t, bd_mask)), 0.0)
    a_ab, a_rb = with_b[:CHUNK], with_b[CHUNK:]

    inv = _unit_lower_inverse(a_ab, t_idx, s_idx, bd_mask)
    kv = bdmm(with_k, v)
    w_til = bdmm(inv, a_t)
    u_til = bdmm(inv, kv[:CHUNK])
    r_hat = r_t + bdmm(a_rb, w_til)
    o_loc = bdmm(a_rb, u_til) + kv[CHUNK:]

    trans = jnp.where(bd_mask, _mm(w_til.T, b_end), 0.0)
    d_bd = jnp.where(bd_mask, _mm(u_til.T, b_end) + _mm(v.T, k_end), 0.0)
    d_ls = d_bd[:CHUNK]
    for h in range(1, QUAD // CHUNK):
        d_ls = d_ls + d_bd[h * CHUNK:(h + 1) * CHUNK]
    return r_hat, o_loc, trans, d_ls, jnp.exp(cum_last)


def _rwkv_kernel(r_ref, lw_ref, k_ref, v_ref, kk_ref, kka_ref, tri_ref, o_ref, state_ref, *, n_sub):
    @pl.when(pl.program_id(1) == 0)
    def _():
        state_ref[...] = jnp.zeros_like(state_ref)

    t_idx = lax.broadcasted_iota(jnp.int32, (CHUNK, QUAD), 0)
    s_idx = lax.broadcasted_iota(jnp.int32, (CHUNK, QUAD), 1) % CHUNK
    bd_mask = (lax.broadcasted_iota(jnp.int32, (QUAD, QUAD), 0) // CHUNK
               == lax.broadcasted_iota(jnp.int32, (QUAD, QUAD), 1) // CHUNK)
    tri = tri_ref[...]

    for q in range(D_RWKV // QUAD):
        lanes = slice(q * QUAD, (q + 1) * QUAD)
        local = []
        for c in range(n_sub):
            rows = slice(c * CHUNK, (c + 1) * CHUNK)
            local.append(_chunk_local(
                r_ref[0, rows, lanes], lw_ref[0, rows, lanes], k_ref[0, rows, lanes],
                v_ref[0, rows, lanes], kk_ref[0, rows, lanes], kka_ref[0, rows, lanes],
                tri, t_idx, s_idx, bd_mask))
        state = state_ref[q]
        for c in range(n_sub):
            r_hat, o_loc, trans, d_ls, p_end = local[c]
            rows = slice(c * CHUNK, (c + 1) * CHUNK)
            o_ref[0, rows, lanes] = _mm_nt(r_hat, _block_diag(state, bd_mask)) + o_loc
            state = state * p_end + _mm(state, trans) + d_ls
        state_ref[q] = state


def _rwkv_call(r, lw, k, v, kk, kka, tri, tb):
    bsz, seq, _ = r.shape
    spec = pl.BlockSpec((1, tb, D_RWKV), lambda b, t: (b, t, 0))
    return pl.pallas_call(
        functools.partial(_rwkv_kernel, n_sub=tb // CHUNK),
        grid=(bsz, seq // tb),
        in_specs=[spec] * 6 + [pl.BlockSpec(tri.shape, lambda b, t: (0, 0))],
        out_specs=spec,
        out_shape=jax.ShapeDtypeStruct((bsz, seq, D_RWKV), F32),
        scratch_shapes=[pltpu.VMEM((D_RWKV // QUAD, HEAD, QUAD), F32)],
        compiler_params=pltpu.CompilerParams(
            dimension_semantics=("arbitrary", "arbitrary"), vmem_limit_bytes=VMEM_LIMIT),
        name="rwkv_chunk",
    )(r, lw, k, v, kk, kka, tri)


def _route(logits):
    lane = lax.broadcasted_iota(jnp.int32, logits.shape, 1)
    neg = -jnp.inf
    big = ROUTER_LANES
    is_group = (lane >= N_EXPERTS) & (lane < N_EXPERTS + N_GROUPS)
    gl = jnp.where(is_group, logits, neg)
    ge = jnp.exp(gl - jnp.max(gl, axis=-1, keepdims=True))
    gprob = ge / jnp.sum(ge, axis=-1, keepdims=True)
    g_top = jnp.max(gprob, axis=-1, keepdims=True)
    g_idx = jnp.min(jnp.where(is_group & (gprob == g_top), lane - N_EXPERTS, big), axis=-1, keepdims=True)

    in_group = (lane < N_EXPERTS) & ((lane // EXPERTS_PER_GROUP) == g_idx)
    el = jnp.where(in_group, logits, neg)
    top1 = jnp.max(el, axis=-1, keepdims=True)
    idx1 = jnp.min(jnp.where(in_group & (el == top1), lane, big), axis=-1, keepdims=True)
    el2 = jnp.where(lane == idx1, neg, el)
    top2 = jnp.max(el2, axis=-1, keepdims=True)
    idx2 = jnp.min(jnp.where(in_group & (lane != idx1) & (el2 == top2), lane, big), axis=-1, keepdims=True)
    e2 = jnp.exp(top2 - top1)
    denom = 1.0 + e2
    gate1 = g_top * (1.0 / denom)
    gate2 = g_top * (e2 / denom)
    return jnp.where(lane == idx1, gate1, 0.0) + jnp.where(lane == idx2, gate2, 0.0)


def _out_kernel(o_ref, bonus_ref, gate_ref, yconv_ref, x_ref, ones_ref, lnw_ref, lnb_ref,
                wout_ref, gffn_ref, rw_ref, rb_ref, h_ref, u_ref, comb_ref):
    ones_bd = ones_ref[...]
    o = o_ref[...]
    inv_n = 1.0 / HEAD
    mean = _mm_exact_rhs(o, ones_bd, 2) * inv_n
    cen = o - mean
    var = _mm_exact_rhs(cen * cen, ones_bd, 2) * inv_n
    on = cen * lax.rsqrt(var + LN_X_EPS) * lnw_ref[...] + lnb_ref[...]
    y_rwkv = (on + bonus_ref[...]) * gate_ref[...]
    mixed = (jnp.dot(_bf(yconv_ref[...]), wout_ref[:D_CONV, :], preferred_element_type=F32)
             + jnp.dot(_bf(y_rwkv), wout_ref[D_CONV:, :], preferred_element_type=F32))
    h = x_ref[...] + mixed
    h_ref[...] = h
    u = _rms_norm(h, gffn_ref[...])
    u_ref[...] = _bf(u)
    logits = jnp.dot(u, rw_ref[...], preferred_element_type=F32, precision=lax.Precision.HIGHEST) + rb_ref[...]
    comb_ref[...] = _route(logits)


def _out_call(o, bonus, gate, yconv, x, ones_bd, ln_w, ln_b, w_out, g_ffn, router_w, router_b, tm):
    n_tok = x.shape[0]
    row = lambda width: pl.BlockSpec((tm, width), lambda i: (i, 0))
    full = lambda arr: pl.BlockSpec(arr.shape, lambda i: (0,) * arr.ndim)
    params = (ones_bd, ln_w, ln_b, w_out, g_ffn, router_w, router_b)
    return pl.pallas_call(
        _out_kernel,
        grid=(n_tok // tm,),
        in_specs=[row(D_RWKV)] * 4 + [row(D_MODEL)] + [full(p) for p in params],
        out_specs=[row(D_MODEL), row(D_MODEL), row(ROUTER_LANES)],
        out_shape=[jax.ShapeDtypeStruct((n_tok, D_MODEL), F32),
                   jax.ShapeDtypeStruct((n_tok, D_MODEL), BF16),
                   jax.ShapeDtypeStruct((n_tok, ROUTER_LANES), F32)],
        compiler_params=pltpu.CompilerParams(
            dimension_semantics=("arbitrary",), vmem_limit_bytes=VMEM_LIMIT),
        name="out_proj_route",
    )(o, bonus, gate, yconv, x, *params)


def _moe_kernel(u_ref, comb_ref, h_ref, wg_ref, wu_ref, wd_ref, gfin_ref, out_ref, acc_ref, *, eb):
    step = pl.program_id(1)

    @pl.when(step == 0)
    def _():
        acc_ref[...] = jnp.zeros_like(acc_ref)

    u = u_ref[...]
    comb = comb_ref[...]
    lane = lax.broadcasted_iota(jnp.int32, comb.shape, 1)
    total = acc_ref[...]
    for j in range(eb):
        weight = jnp.sum(jnp.where(lane == step * eb + j, comb, 0.0), axis=-1, keepdims=True)
        gate = jnp.dot(u, wg_ref[j], preferred_element_type=F32)
        up = jnp.dot(u, wu_ref[j], preferred_element_type=F32)
        hid = gate * jax.nn.sigmoid(gate) * up * weight
        total = total + jnp.dot(_bf(hid), wd_ref[j], preferred_element_type=F32)
    acc_ref[...] = total

    @pl.when(step == pl.num_programs(1) - 1)
    def _():
        out_ref[...] = _rms_norm(h_ref[...] + total, gfin_ref[...])


def _moe_call(u, comb, h, w_gate, w_up, w_down, g_final, tm, eb):
    n_tok = u.shape[0]
    row = lambda width: pl.BlockSpec((tm, width), lambda i, e: (i, 0))
    return pl.pallas_call(
        functools.partial(_moe_kernel, eb=eb),
        grid=(n_tok // tm, N_EXPERTS // eb),
        in_specs=[row(D_MODEL), row(ROUTER_LANES), row(D_MODEL),
                  pl.BlockSpec((eb, D_MODEL, D_EXPERT), lambda i, e: (e, 0, 0)),
                  pl.BlockSpec((eb, D_MODEL, D_EXPERT), lambda i, e: (e, 0, 0)),
                  pl.BlockSpec((eb, D_EXPERT, D_MODEL), lambda i, e: (e, 0, 0)),
                  pl.BlockSpec((1, D_MODEL), lambda i, e: (0, 0))],
        out_specs=row(D_MODEL),
        out_shape=jax.ShapeDtypeStruct((n_tok, D_MODEL), F32),
        scratch_shapes=[pltpu.VMEM((tm, D_MODEL), F32)],
        compiler_params=pltpu.CompilerParams(
            dimension_semantics=("arbitrary", "arbitrary"), vmem_limit_bytes=VMEM_LIMIT),
        name="moe_dense",
    )(u, comb, h, w_gate, w_up, w_down, g_final)


def _block(x, norm_mix_g, w_in, rwkv_mu, conv_w, decay_up, decay_base, aaa_up, aaa_base, gate_up,
           k_k, k_a, r_k, ln_x_w, ln_x_b, w_out, norm_ffn_g, router_group_w, router_group_b,
           router_expert_w, router_expert_b, expert_w_gate, expert_w_up, expert_w_down, norm_final_g,
           *, tm_in, tb_rwkv, tm_out, tm_moe, eb_moe):
    bsz, seq, d_model = x.shape
    n_tok = bsz * seq
    row = lambda p: p.reshape(1, -1)
    half = LORA_WA // 2
    zeros = jnp.zeros((half, D_RWKV), F32)
    dup = _bf(jnp.concatenate([decay_up, zeros], axis=0))
    aup = _bf(jnp.concatenate([zeros, aaa_up], axis=0))
    head_of = jnp.arange(D_RWKV) // HEAD
    ones_bd = (head_of[:, None] == head_of[None, :]).astype(BF16)
    tri = (jnp.arange(CHUNK)[:, None] >= jnp.arange(CHUNK)[None, :]).astype(BF16)

    yconv, r, lw, k2, v, kk, kka, gate, bonus = _in_call(
        x, row(norm_mix_g), _bf(w_in), row(rwkv_mu), conv_w, dup, row(decay_base), aup, row(aaa_base),
        _bf(gate_up), row(k_k), row(k_a), row(r_k), ones_bd, tm_in)
    o = _rwkv_call(r, lw, k2, v, kk, kka, tri, tb_rwkv)

    pad = ROUTER_LANES - N_EXPERTS - N_GROUPS
    router_w = jnp.concatenate([router_expert_w, router_group_w, jnp.zeros((d_model, pad), F32)], axis=1)
    router_b = jnp.concatenate([router_expert_b, router_group_b, jnp.zeros((pad,), F32)]).reshape(1, -1)
    flat = lambda t: t.reshape(n_tok, t.shape[-1])
    h, u, comb = _out_call(flat(o), flat(bonus), flat(gate), flat(yconv), flat(x), ones_bd,
                           row(ln_x_w), row(ln_x_b), _bf(w_out), row(norm_ffn_g), router_w, router_b, tm_out)
    out = _moe_call(u, comb, h, _bf(expert_w_gate), _bf(expert_w_up), _bf(expert_w_down),
                    row(norm_final_g), tm_moe, eb_moe)
    return out.reshape(bsz, seq, d_model)


def kernel(x, norm_mix_g, w_in, rwkv_mu, conv_w, decay_up, decay_base, aaa_up, aaa_base, gate_up, k_k, k_a, r_k, ln_x_w, ln_x_b, w_out, norm_ffn_g, router_group_w, router_group_b, router_expert_w, router_expert_b, expert_w_gate, expert_w_up, expert_w_down, norm_final_g):
    return _block(x, norm_mix_g[0], w_in[0], rwkv_mu[0], conv_w[0], decay_up[0], decay_base[0],
                  aaa_up[0], aaa_base[0], gate_up[0], k_k[0], k_a[0], r_k[0].reshape(-1), ln_x_w[0],
                  ln_x_b[0], w_out[0], norm_ffn_g[0], router_group_w[0], router_group_b[0],
                  router_expert_w[0], router_expert_b[0], expert_w_gate[0], expert_w_up[0],
                  expert_w_down[0], norm_final_g,
                  tm_in=256, tb_rwkv=128, tm_out=256, tm_moe=1024, eb_moe=4)
```

```python
import functools

import jax
import jax.numpy as jnp
from jax import lax
from jax.experimental import pallas as pl
from jax.experimental.pallas import tpu as pltpu

F32 = jnp.float32
BF16 = jnp.bfloat16

D_MODEL = 1024
D_CONV = 512
CONV_WIDTH = 3
N_HEADS = 8
HEAD = 64
D_RWKV = N_HEADS * HEAD
LORA_WA = 128
GATE_LORA = 128
D_RWKV_PROJ = 3 * D_RWKV + LORA_WA + GATE_LORA
D_IN = 3 * D_CONV + D_RWKV_PROJ
N_GROUPS = 4
EXPERTS_PER_GROUP = 8
N_EXPERTS = N_GROUPS * EXPERTS_PER_GROUP
D_EXPERT = D_MODEL // 4
RMS_EPS = 1e-6
LN_X_EPS = 64e-5
L2_EPS = 1e-12

LANES = 128
CHUNK = 64
QUAD = 4 * HEAD
ROUTER_LANES = 128

VMEM_LIMIT = 56 * 1024 * 1024


def _bf(x):
    return x.astype(BF16)


def _mm(a, b):
    return jnp.dot(_bf(a), _bf(b), preferred_element_type=F32)


def _mm_nt(a, b):
    return lax.dot_general(_bf(a), _bf(b), (((1,), (1,)), ((), ())), preferred_element_type=F32)


def _mm_exact_lhs(lhs_bf16, x, passes):
    acc = None
    rem = x
    for _ in range(passes):
        piece = _bf(rem)
        part = jnp.dot(lhs_bf16, piece, preferred_element_type=F32)
        acc = part if acc is None else acc + part
        rem = rem - piece.astype(F32)
    return acc


def _head_sum(x, ones_quad):
    xb = _bf(x)
    return jnp.concatenate(
        [jnp.dot(xb[:, q * QUAD:(q + 1) * QUAD], ones_quad, preferred_element_type=F32)
         for q in range(x.shape[1] // QUAD)], axis=1)


def _rms_norm(x, g):
    return x * lax.rsqrt(jnp.mean(x * x, axis=-1, keepdims=True) + RMS_EPS) * g


def _shift_rows(cur, prev_rows, k):
    rolled = pltpu.roll(cur, k, 0)
    prev_rolled = pltpu.roll(prev_rows, k, 0)
    n = cur.shape[0]
    head = jnp.concatenate([prev_rolled, rolled[8:]], axis=0) if n > 8 else prev_rolled
    row = lax.broadcasted_iota(jnp.int32, cur.shape, 0)
    return jnp.where(row < k, head, rolled)


def _in_kernel(x_ref, g_ref, w_ref, mu_ref, convw_ref, dup_ref, dbase_ref, aup_ref, abase_ref,
               gup_ref, kk_ref, ka_ref, rk_ref, ones_ref,
               yconv_ref, r_ref, lw_ref, k_ref, v_ref, kkn_ref, kka_ref, gate_ref, bonus_ref,
               carry_ref):
    @pl.when(pl.program_id(1) == 0)
    def _():
        carry_ref[...] = jnp.zeros_like(carry_ref)

    x = x_ref[0]
    u = _rms_norm(x, g_ref[...])
    z = jnp.dot(_bf(u), w_ref[...], preferred_element_type=F32)

    b_gate = z[:, :D_CONV]
    ch = z[:, D_CONV:2 * D_CONV] * z[:, 2 * D_CONV:3 * D_CONV]
    prev_ch = carry_ref[:, :D_CONV]
    conv = (convw_ref[2:3, :] * ch
            + convw_ref[1:2, :] * _shift_rows(ch, prev_ch, 1)
            + convw_ref[0:1, :] * _shift_rows(ch, prev_ch, 2))
    yconv_ref[0] = _bf(b_gate * conv)

    zr = z[:, 3 * D_CONV:]
    prev_zr = carry_ref[:, D_CONV:]
    zs = zr + (_shift_rows(zr, prev_zr, 1) - zr) * mu_ref[...]
    carry_ref[:, :D_CONV] = ch[-8:]
    carry_ref[:, D_CONV:] = zr[-8:]

    r = zs[:, :D_RWKV]
    k = zs[:, D_RWKV:2 * D_RWKV]
    v = zs[:, 2 * D_RWKV:3 * D_RWKV]
    wa_lo = zs[:, 3 * D_RWKV:3 * D_RWKV + LORA_WA]
    g_lo = zs[:, 3 * D_RWKV + LORA_WA:]

    dec_in = -(dbase_ref[...] + jnp.dot(_bf(jnp.tanh(wa_lo)), dup_ref[...], preferred_element_type=F32))
    softplus = jnp.maximum(dec_in, 0.0) + jnp.log(1.0 + jnp.exp(-jnp.abs(dec_in)))
    w = -softplus - 0.5
    lw_ref[0] = -jnp.exp(w)
    a = jax.nn.sigmoid(abase_ref[...] + jnp.dot(_bf(wa_lo), aup_ref[...], preferred_element_type=F32))
    gate_ref[0] = jnp.dot(_bf(jax.nn.sigmoid(g_lo)), gup_ref[...], preferred_element_type=F32)

    ones_quad = ones_ref[...]
    kk = k * kk_ref[...]
    norm = jnp.sqrt(_head_sum(kk * kk, ones_quad))
    kk = kk / jnp.maximum(norm, L2_EPS)
    k2 = k * (1.0 + (a - 1.0) * ka_ref[...])
    r_ref[0] = _bf(r)
    k_ref[0] = _bf(k2)
    v_ref[0] = _bf(v)
    kkn_ref[0] = _bf(kk)
    kka_ref[0] = _bf(kk * a)
    bonus_ref[0] = _head_sum(r * k2 * rk_ref[...], ones_quad) * v


def _in_call(x, g, w_in, mu, conv_w, dup, dbase, aup, abase, gup, k_k, k_a, r_k, ones_quad, tm):
    bsz, seq, _ = x.shape
    full = lambda arr: pl.BlockSpec(arr.shape, lambda b, t: (0,) * arr.ndim, pipeline_mode=pl.Buffered(1))
    out_spec = pl.BlockSpec((1, tm, D_RWKV), lambda b, t: (b, t, 0))
    out_dtypes = (BF16, BF16, F32, BF16, BF16, BF16, BF16, F32, F32)
    params = (g, w_in, mu, conv_w, dup, dbase, aup, abase, gup, k_k, k_a, r_k, ones_quad)
    return pl.pallas_call(
        _in_kernel,
        grid=(bsz, seq // tm),
        in_specs=[pl.BlockSpec((1, tm, D_MODEL), lambda b, t: (b, t, 0))] + [full(p) for p in params],
        out_specs=[out_spec] * 9,
        out_shape=[jax.ShapeDtypeStruct((bsz, seq, D_RWKV), dt) for dt in out_dtypes],
        scratch_shapes=[pltpu.VMEM((8, D_CONV + D_RWKV_PROJ), F32)],
        compiler_params=pltpu.CompilerParams(
            dimension_semantics=("arbitrary", "arbitrary"), vmem_limit_bytes=VMEM_LIMIT),
        name="in_proj",
    )(x, *params)


def _block_diag(y, bd_mask):
    return jnp.where(bd_mask, jnp.concatenate([y] * (QUAD // CHUNK), axis=0), 0.0)


def _each(fn, *lists):
    return [fn(*args) for args in zip(*lists)]


def _unit_lower_inverse(a_strict, t_idx, s_idx, bdmm):
    eye = (t_idx == s_idx).astype(F32)
    same8 = (t_idx // 8) == (s_idx // 8)
    a8 = _each(lambda a: jnp.where(same8, a, 0.0), a_strict)
    a8_2 = bdmm(a8, a8)
    a8_3 = bdmm(a8, a8_2)
    a8_4 = bdmm(a8_2, a8_2)
    inv = _each(lambda a, b, c: eye + a + b + c, a8, a8_2, a8_3)
    inv = _each(jnp.add, inv, bdmm(inv, a8_4))
    size = 16
    while size <= CHUNK:
        off = ((t_idx // size) == (s_idx // size)) & ((t_idx // (size // 2)) != (s_idx // (size // 2)))
        cross = bdmm(_each(lambda a: jnp.where(off, a, 0.0), a_strict), inv)
        inv = _each(jnp.add, inv, bdmm(inv, cross))
        size *= 2
    return inv


def _chunk_local(r, lw, k, v, kk, kka, tri, t_idx, s_idx, bd_mask):
    bd = lambda y: _block_diag(y, bd_mask)
    bdmm = lambda xs, ys: _each(lambda x, y: _mm(x, bd(y)), xs, ys)
    cum = _each(lambda x: _mm_exact_lhs(tri, x, 3), lw)
    cum_last = _each(lambda c: c[CHUNK - 1:CHUNK, :], cum)
    p_incl = _each(jnp.exp, cum)
    p_excl = _each(lambda c, x: jnp.exp(c - x), cum, lw)
    p_inv = _each(lambda c: jnp.exp(-c), cum)
    to_end = _each(lambda cl, c: jnp.exp(cl - c), cum_last, cum)
    a_t = _each(lambda x, p: -x * p, kk, p_excl)
    r_t = _each(jnp.multiply, r, p_incl)
    b_t = _each(jnp.multiply, kka, p_inv)
    k_t = _each(jnp.multiply, k, p_inv)
    b_end = _each(jnp.multiply, kka, to_end)
    k_end = _each(jnp.multiply, k, to_end)

    ar = _each(lambda a, b: jnp.concatenate([a, b], axis=0), a_t, r_t)
    row2 = lax.broadcasted_iota(jnp.int32, (2 * CHUNK, QUAD), 0)
    col2 = lax.broadcasted_iota(jnp.int32, (2 * CHUNK, QUAD), 1) % CHUNK
    causal = col2 < (row2 % CHUNK) + (row2 // CHUNK)
    with_b = _each(lambda x, y: jnp.where(causal, _mm_nt(x, bd(y)), 0.0), ar, b_t)
    with_k = _each(lambda x, y: jnp.where(causal, _mm_nt(x, bd(y)), 0.0), ar, k_t)
    a_ab = _each(lambda x: x[:CHUNK], with_b)
    a_rb = _each(lambda x: x[CHUNK:], with_b)

    inv = _unit_lower_inverse(a_ab, t_idx, s_idx, bdmm)
    kv = bdmm(with_k, v)
    w_til = bdmm(inv, a_t)
    u_til = bdmm(inv, _each(lambda x: x[:CHUNK], kv))
    r_hat = _each(jnp.add, r_t, bdmm(a_rb, w_til))
    o_loc = _each(lambda x, y: x + y[CHUNK:], bdmm(a_rb, u_til), kv)

    trans = _each(lambda w, b: jnp.where(bd_mask, _mm(w.T, b), 0.0), w_til, b_end)
    d_bd = _each(lambda u, b, x, y: jnp.where(bd_mask, _mm(u.T, b) + _mm(x.T, y), 0.0), u_til, b_end, v, k_end)
    d_ls = _each(lambda d: sum(d[h * CHUNK:(h + 1) * CHUNK] for h in range(1, QUAD // CHUNK)) + d[:CHUNK], d_bd)
    p_end = _each(jnp.exp, cum_last)
    return r_hat, o_loc, trans, d_ls, p_end


def _rwkv_kernel(r_ref, lw_ref, k_ref, v_ref, kk_ref, kka_ref, tri_ref, o_ref, state_ref, *, n_sub):
    @pl.when(pl.program_id(1) == 0)
    def _():
        state_ref[...] = jnp.zeros_like(state_ref)

    t_idx = lax.broadcasted_iota(jnp.int32, (CHUNK, QUAD), 0)
    s_idx = lax.broadcasted_iota(jnp.int32, (CHUNK, QUAD), 1) % CHUNK
    bd_mask = (lax.broadcasted_iota(jnp.int32, (QUAD, QUAD), 0) // CHUNK
               == lax.broadcasted_iota(jnp.int32, (QUAD, QUAD), 1) // CHUNK)
    n_quad = D_RWKV // QUAD
    where = [(slice(c * CHUNK, (c + 1) * CHUNK), slice(q * QUAD, (q + 1) * QUAD))
             for c in range(n_sub) for q in range(n_quad)]
    load = lambda ref: [ref[0, rows, lanes].astype(F32) for rows, lanes in where]
    r_hat, o_loc, trans, d_ls, p_end = _chunk_local(
        load(r_ref), load(lw_ref), load(k_ref), load(v_ref), load(kk_ref), load(kka_ref),
        tri_ref[...], t_idx, s_idx, bd_mask)

    state = [state_ref[q] for q in range(n_quad)]
    for c in range(n_sub):
        chains = range(c * n_quad, (c + 1) * n_quad)
        out = [_mm_nt(r_hat[i], _block_diag(state[q], bd_mask)) + o_loc[i] for q, i in enumerate(chains)]
        for q, i in enumerate(chains):
            rows, lanes = where[i]
            o_ref[0, rows, lanes] = out[q]
        state = [state[q] * p_end[i] + _mm(state[q], trans[i]) + d_ls[i] for q, i in enumerate(chains)]
    for q in range(n_quad):
        state_ref[q] = state[q]


def _rwkv_call(r, lw, k, v, kk, kka, tri, tb):
    bsz, seq, _ = r.shape
    spec = pl.BlockSpec((1, tb, D_RWKV), lambda b, t: (b, t, 0))
    return pl.pallas_call(
        functools.partial(_rwkv_kernel, n_sub=tb // CHUNK),
        grid=(bsz, seq // tb),
        in_specs=[spec] * 6 + [pl.BlockSpec(tri.shape, lambda b, t: (0, 0))],
        out_specs=spec,
        out_shape=jax.ShapeDtypeStruct((bsz, seq, D_RWKV), F32),
        scratch_shapes=[pltpu.VMEM((D_RWKV // QUAD, HEAD, QUAD), F32)],
        compiler_params=pltpu.CompilerParams(
            dimension_semantics=("arbitrary", "arbitrary"), vmem_limit_bytes=VMEM_LIMIT),
        name="rwkv_chunk",
    )(r, lw, k, v, kk, kka, tri)


def _route(logits):
    lane = lax.broadcasted_iota(jnp.int32, logits.shape, 1)
    neg = -jnp.inf
    big = ROUTER_LANES
    is_group = (lane >= N_EXPERTS) & (lane < N_EXPERTS + N_GROUPS)
    gl = jnp.where(is_group, logits, neg)
    ge = jnp.exp(gl - jnp.max(gl, axis=-1, keepdims=True))
    gprob = ge / jnp.sum(ge, axis=-1, keepdims=True)
    g_top = jnp.max(gprob, axis=-1, keepdims=True)
    g_idx = jnp.min(jnp.where(is_group & (gprob == g_top), lane - N_EXPERTS, big), axis=-1, keepdims=True)

    in_group = (lane < N_EXPERTS) & ((lane // EXPERTS_PER_GROUP) == g_idx)
    el = jnp.where(in_group, logits, neg)
    top1 = jnp.max(el, axis=-1, keepdims=True)
    idx1 = jnp.min(jnp.where(in_group & (el == top1), lane, big), axis=-1, keepdims=True)
    el2 = jnp.where(lane == idx1, neg, el)
    top2 = jnp.max(el2, axis=-1, keepdims=True)
    idx2 = jnp.min(jnp.where(in_group & (lane != idx1) & (el2 == top2), lane, big), axis=-1, keepdims=True)
    e2 = jnp.exp(top2 - top1)
    denom = 1.0 + e2
    gate1 = g_top * (1.0 / denom)
    gate2 = g_top * (e2 / denom)
    return jnp.where(lane == idx1, gate1, 0.0) + jnp.where(lane == idx2, gate2, 0.0)


def _out_kernel(o_ref, bonus_ref, gate_ref, yconv_ref, x_ref, ones_ref, lnw_ref, lnb_ref,
                wout_ref, gffn_ref, rw_ref, rb_ref, h_ref, u_ref, comb_ref):
    ones_quad = ones_ref[...]
    o = o_ref[...]
    inv_n = 1.0 / HEAD
    mean = _head_sum(o, ones_quad) * inv_n
    cen = o - mean
    var = _head_sum(cen * cen, ones_quad) * inv_n
    on = cen * lax.rsqrt(var + LN_X_EPS) * lnw_ref[...] + lnb_ref[...]
    y_rwkv = (on + bonus_ref[...]) * gate_ref[...]
    mixed = (jnp.dot(yconv_ref[...], wout_ref[:D_CONV, :], preferred_element_type=F32)
             + jnp.dot(_bf(y_rwkv), wout_ref[D_CONV:, :], preferred_element_type=F32))
    h = x_ref[...] + mixed
    h_ref[...] = h
    u = _rms_norm(h, gffn_ref[...])
    u_hi = _bf(u)
    u_ref[...] = u_hi
    u_lo = _bf(u - u_hi.astype(F32))
    by_hi = jnp.dot(u_hi, rw_ref[...], preferred_element_type=F32)
    by_lo = jnp.dot(u_lo, rw_ref[:, :ROUTER_LANES], preferred_element_type=F32)
    logits = by_hi[:, :ROUTER_LANES] + by_hi[:, ROUTER_LANES:] + by_lo + rb_ref[...]
    comb_ref[...] = _route(logits)


def _out_call(o, bonus, gate, yconv, x, ones_bd, ln_w, ln_b, w_out, g_ffn, router_w, router_b, tm):
    n_tok = x.shape[0]
    row = lambda width: pl.BlockSpec((tm, width), lambda i: (i, 0))
    full = lambda arr: pl.BlockSpec(arr.shape, lambda i: (0,) * arr.ndim, pipeline_mode=pl.Buffered(1))
    params = (ones_bd, ln_w, ln_b, w_out, g_ffn, router_w, router_b)
    return pl.pallas_call(
        _out_kernel,
        grid=(n_tok // tm,),
        in_specs=[row(D_RWKV)] * 4 + [row(D_MODEL)] + [full(p) for p in params],
        out_specs=[row(D_MODEL), row(D_MODEL), row(ROUTER_LANES)],
        out_shape=[jax.ShapeDtypeStruct((n_tok, D_MODEL), F32),
                   jax.ShapeDtypeStruct((n_tok, D_MODEL), BF16),
                   jax.ShapeDtypeStruct((n_tok, ROUTER_LANES), F32)],
        compiler_params=pltpu.CompilerParams(
            dimension_semantics=("arbitrary",), vmem_limit_bytes=VMEM_LIMIT),
        name="out_proj_route",
    )(o, bonus, gate, yconv, x, *params)


def _moe_kernel(u_ref, comb_ref, h_ref, wg_ref, wu_ref, wd_ref, gfin_ref, out_ref, acc_ref, *, eb):
    step = pl.program_id(1)

    @pl.when(step == 0)
    def _():
        acc_ref[...] = jnp.zeros_like(acc_ref)

    u = u_ref[...]
    comb = comb_ref[...]
    lane = lax.broadcasted_iota(jnp.int32, comb.shape, 1)
    total = acc_ref[...]
    for j in range(eb):
        weight = jnp.sum(jnp.where(lane == step * eb + j, comb, 0.0), axis=-1, keepdims=True)
        gate = jnp.dot(u, wg_ref[j], preferred_element_type=F32)
        up = jnp.dot(u, wu_ref[j], preferred_element_type=F32)
        hid = gate * jax.nn.sigmoid(gate) * up * weight
        total = total + jnp.dot(_bf(hid), wd_ref[j], preferred_element_type=F32)
    acc_ref[...] = total

    @pl.when(step == pl.num_programs(1) - 1)
    def _():
        out_ref[...] = _rms_norm(h_ref[...] + total, gfin_ref[...])


def _moe_call(u, comb, h, w_gate, w_up, w_down, g_final, tm, eb):
    n_tok = u.shape[0]
    row = lambda width: pl.BlockSpec((tm, width), lambda i, e: (i, 0))
    return pl.pallas_call(
        functools.partial(_moe_kernel, eb=eb),
        grid=(n_tok // tm, N_EXPERTS // eb),
        in_specs=[row(D_MODEL), row(ROUTER_LANES), row(D_MODEL),
                  pl.BlockSpec((eb, D_MODEL, D_EXPERT), lambda i, e: (e, 0, 0)),
                  pl.BlockSpec((eb, D_MODEL, D_EXPERT), lambda i, e: (e, 0, 0)),
                  pl.BlockSpec((eb, D_EXPERT, D_MODEL), lambda i, e: (e, 0, 0)),
                  pl.BlockSpec((1, D_MODEL), lambda i, e: (0, 0))],
        out_specs=row(D_MODEL),
        out_shape=jax.ShapeDtypeStruct((n_tok, D_MODEL), F32),
        scratch_shapes=[pltpu.VMEM((tm, D_MODEL), F32)],
        compiler_params=pltpu.CompilerParams(
            dimension_semantics=("arbitrary", "arbitrary"), vmem_limit_bytes=VMEM_LIMIT),
        name="moe_dense",
    )(u, comb, h, w_gate, w_up, w_down, g_final)


def _block(x, norm_mix_g, w_in, rwkv_mu, conv_w, decay_up, decay_base, aaa_up, aaa_base, gate_up,
           k_k, k_a, r_k, ln_x_w, ln_x_b, w_out, norm_ffn_g, router_group_w, router_group_b,
           router_expert_w, router_expert_b, expert_w_gate, expert_w_up, expert_w_down, norm_final_g,
           *, tm_in, tb_rwkv, tm_out, tm_moe, eb_moe):
    bsz, seq, d_model = x.shape
    n_tok = bsz * seq
    row = lambda p: p.reshape(1, -1)
    half = LORA_WA // 2
    zeros = jnp.zeros((half, D_RWKV), F32)
    dup = _bf(jnp.concatenate([decay_up, zeros], axis=0))
    aup = _bf(jnp.concatenate([zeros, aaa_up], axis=0))
    head_of = jnp.arange(QUAD) // HEAD
    ones_quad = (head_of[:, None] == head_of[None, :]).astype(BF16)
    tri = (jnp.arange(CHUNK)[:, None] >= jnp.arange(CHUNK)[None, :]).astype(BF16)

    yconv, r, lw, k2, v, kk, kka, gate, bonus = _in_call(
        x, row(norm_mix_g), _bf(w_in), row(rwkv_mu), conv_w, dup, row(decay_base), aup, row(aaa_base),
        _bf(gate_up), row(k_k), row(k_a), row(r_k), ones_quad, tm_in)
    o = _rwkv_call(r, lw, k2, v, kk, kka, tri, tb_rwkv)

    pad = ROUTER_LANES - N_EXPERTS - N_GROUPS
    router_w = jnp.concatenate([router_expert_w, router_group_w, jnp.zeros((d_model, pad), F32)], axis=1)
    router_hi = _bf(router_w)
    router_split = jnp.concatenate([router_hi, _bf(router_w - router_hi.astype(F32))], axis=1)
    router_b = jnp.concatenate([router_expert_b, router_group_b, jnp.zeros((pad,), F32)]).reshape(1, -1)
    flat = lambda t: t.reshape(n_tok, t.shape[-1])
    h, u, comb = _out_call(flat(o), flat(bonus), flat(gate), flat(yconv), flat(x), ones_quad,
                           row(ln_x_w), row(ln_x_b), _bf(w_out), row(norm_ffn_g), router_split, router_b, tm_out)
    out = _moe_call(u, comb, h, _bf(expert_w_gate), _bf(expert_w_up), _bf(expert_w_down),
                    row(norm_final_g), tm_moe, eb_moe)
    return out.reshape(bsz, seq, d_model)


def kernel(x, norm_mix_g, w_in, rwkv_mu, conv_w, decay_up, decay_base, aaa_up, aaa_base, gate_up, k_k, k_a, r_k, ln_x_w, ln_x_b, w_out, norm_ffn_g, router_group_w, router_group_b, router_expert_w, router_expert_b, expert_w_gate, expert_w_up, expert_w_down, norm_final_g):
    return _block(x, norm_mix_g[0], w_in[0], rwkv_mu[0], conv_w[0], decay_up[0], decay_base[0],
                  aaa_up[0], aaa_base[0], gate_up[0], k_k[0], k_a[0], r_k[0].reshape(-1), ln_x_w[0],
                  ln_x_b[0], w_out[0], norm_ffn_g[0], router_group_w[0], router_group_b[0],
                  router_expert_w[0], router_expert_b[0], expert_w_gate[0], expert_w_up[0],
                  expert_w_down[0], norm_final_g,
                  tm_in=512, tb_rwkv=256, tm_out=512, tm_moe=1024, eb_moe=4)
```

```python
import functools

import jax
import jax.numpy as jnp
from jax import lax
from jax.experimental import pallas as pl
from jax.experimental.pallas import tpu as pltpu
from jax.experimental.pallas import tpu_sc as plsc

F32 = jnp.float32
BF16 = jnp.bfloat16

D_MODEL = 1024
D_CONV = 512
CONV_WIDTH = 3
N_HEADS = 8
HEAD = 64
D_RWKV = N_HEADS * HEAD
LORA_WA = 128
GATE_LORA = 128
D_RWKV_PROJ = 3 * D_RWKV + LORA_WA + GATE_LORA
D_IN = 3 * D_CONV + D_RWKV_PROJ
N_GROUPS = 4
EXPERTS_PER_GROUP = 8
N_EXPERTS = N_GROUPS * EXPERTS_PER_GROUP
D_EXPERT = D_MODEL // 4
RMS_EPS = 1e-6
LN_X_EPS = 64e-5
L2_EPS = 1e-12

LANES = 128
CHUNK = 64
QUAD = 4 * HEAD
ROUTER_LANES = 128

VMEM_LIMIT = 56 * 1024 * 1024


def _bf(x):
    return x.astype(BF16)


def _mm(a, b):
    return jnp.dot(_bf(a), _bf(b), preferred_element_type=F32)


def _mm_nt(a, b):
    return lax.dot_general(_bf(a), _bf(b), (((1,), (1,)), ((), ())), preferred_element_type=F32)


def _mm_exact_lhs(lhs_bf16, x, passes):
    acc = None
    rem = x
    for _ in range(passes):
        piece = _bf(rem)
        part = jnp.dot(lhs_bf16, piece, preferred_element_type=F32)
        acc = part if acc is None else acc + part
        rem = rem - piece.astype(F32)
    return acc


def _head_sum(x, ones_quad):
    xb = _bf(x)
    return jnp.concatenate(
        [jnp.dot(xb[:, q * QUAD:(q + 1) * QUAD], ones_quad, preferred_element_type=F32)
         for q in range(x.shape[1] // QUAD)], axis=1)


def _rms_norm(x, g):
    return x * lax.rsqrt(jnp.mean(x * x, axis=-1, keepdims=True) + RMS_EPS) * g


def _shift_rows(cur, prev_rows, k):
    rolled = pltpu.roll(cur, k, 0)
    prev_rolled = pltpu.roll(prev_rows, k, 0)
    n = cur.shape[0]
    head = jnp.concatenate([prev_rolled, rolled[8:]], axis=0) if n > 8 else prev_rolled
    row = lax.broadcasted_iota(jnp.int32, cur.shape, 0)
    return jnp.where(row < k, head, rolled)


def _in_kernel(x_ref, g_ref, w_ref, mu_ref, convw_ref, dup_ref, dbase_ref, aup_ref, abase_ref,
               gup_ref, kk_ref, ka_ref, rk_ref, ones_ref,
               yconv_ref, r_ref, lw_ref, k_ref, v_ref, kkn_ref, kka_ref, gate_ref, bonus_ref,
               carry_ref):
    @pl.when(pl.program_id(1) == 0)
    def _():
        carry_ref[...] = jnp.zeros_like(carry_ref)

    x = x_ref[0]
    u = _rms_norm(x, g_ref[...])
    z = jnp.dot(_bf(u), w_ref[...], preferred_element_type=F32)

    b_gate = z[:, :D_CONV]
    ch = z[:, D_CONV:2 * D_CONV] * z[:, 2 * D_CONV:3 * D_CONV]
    prev_ch = carry_ref[:, :D_CONV]
    conv = (convw_ref[2:3, :] * ch
            + convw_ref[1:2, :] * _shift_rows(ch, prev_ch, 1)
            + convw_ref[0:1, :] * _shift_rows(ch, prev_ch, 2))
    yconv_ref[0] = _bf(b_gate * conv)

    zr = z[:, 3 * D_CONV:]
    prev_zr = carry_ref[:, D_CONV:]
    zs = zr + (_shift_rows(zr, prev_zr, 1) - zr) * mu_ref[...]
    carry_ref[:, :D_CONV] = ch[-8:]
    carry_ref[:, D_CONV:] = zr[-8:]

    r = zs[:, :D_RWKV]
    k = zs[:, D_RWKV:2 * D_RWKV]
    v = zs[:, 2 * D_RWKV:3 * D_RWKV]
    wa_lo = zs[:, 3 * D_RWKV:3 * D_RWKV + LORA_WA]
    g_lo = zs[:, 3 * D_RWKV + LORA_WA:]

    dec_in = -(dbase_ref[...] + jnp.dot(_bf(jnp.tanh(wa_lo)), dup_ref[...], preferred_element_type=F32))
    softplus = jnp.maximum(dec_in, 0.0) + jnp.log(1.0 + jnp.exp(-jnp.abs(dec_in)))
    w = -softplus - 0.5
    lw_ref[0] = -jnp.exp(w)
    a = jax.nn.sigmoid(abase_ref[...] + jnp.dot(_bf(wa_lo), aup_ref[...], preferred_element_type=F32))
    gate_ref[0] = jnp.dot(_bf(jax.nn.sigmoid(g_lo)), gup_ref[...], preferred_element_type=F32)

    ones_quad = ones_ref[...]
    kk = k * kk_ref[...]
    norm = jnp.sqrt(_head_sum(kk * kk, ones_quad))
    kk = kk / jnp.maximum(norm, L2_EPS)
    k2 = k * (1.0 + (a - 1.0) * ka_ref[...])
    r_ref[0] = _bf(r)
    k_ref[0] = _bf(k2)
    v_ref[0] = _bf(v)
    kkn_ref[0] = _bf(kk)
    kka_ref[0] = _bf(kk * a)
    bonus_ref[0] = _head_sum(r * k2 * rk_ref[...], ones_quad) * v


def _in_call(x, g, w_in, mu, conv_w, dup, dbase, aup, abase, gup, k_k, k_a, r_k, ones_quad, tm):
    bsz, seq, _ = x.shape
    full = lambda arr: pl.BlockSpec(arr.shape, lambda b, t: (0,) * arr.ndim, pipeline_mode=pl.Buffered(1))
    out_spec = pl.BlockSpec((1, tm, D_RWKV), lambda b, t: (b, t, 0))
    out_dtypes = (BF16, BF16, F32, BF16, BF16, BF16, BF16, F32, F32)
    params = (g, w_in, mu, conv_w, dup, dbase, aup, abase, gup, k_k, k_a, r_k, ones_quad)
    return pl.pallas_call(
        _in_kernel,
        grid=(bsz, seq // tm),
        in_specs=[pl.BlockSpec((1, tm, D_MODEL), lambda b, t: (b, t, 0))] + [full(p) for p in params],
        out_specs=[out_spec] * 9,
        out_shape=[jax.ShapeDtypeStruct((bsz, seq, D_RWKV), dt) for dt in out_dtypes],
        scratch_shapes=[pltpu.VMEM((8, D_CONV + D_RWKV_PROJ), F32)],
        compiler_params=pltpu.CompilerParams(
            dimension_semantics=("arbitrary", "arbitrary"), vmem_limit_bytes=VMEM_LIMIT),
        name="in_proj",
    )(x, *params)


def _block_diag(y, bd_mask):
    return jnp.where(bd_mask, jnp.concatenate([y] * (QUAD // CHUNK), axis=0), 0.0)


def _each(fn, *lists):
    return [fn(*args) for args in zip(*lists)]


def _unit_lower_inverse(a_strict, t_idx, s_idx, bdmm):
    eye = (t_idx == s_idx).astype(F32)
    same8 = (t_idx // 8) == (s_idx // 8)
    a8 = _each(lambda a: jnp.where(same8, a, 0.0), a_strict)
    a8_2 = bdmm(a8, a8)
    a8_3 = bdmm(a8, a8_2)
    a8_4 = bdmm(a8_2, a8_2)
    inv = _each(lambda a, b, c: eye + a + b + c, a8, a8_2, a8_3)
    inv = _each(jnp.add, inv, bdmm(inv, a8_4))
    size = 16
    while size <= CHUNK:
        off = ((t_idx // size) == (s_idx // size)) & ((t_idx // (size // 2)) != (s_idx // (size // 2)))
        cross = bdmm(_each(lambda a: jnp.where(off, a, 0.0), a_strict), inv)
        inv = _each(jnp.add, inv, bdmm(inv, cross))
        size *= 2
    return inv


def _chunk_local(r, lw, k, v, kk, kka, tri, t_idx, s_idx, bd_mask):
    bd = lambda y: _block_diag(y, bd_mask)
    bdmm = lambda xs, ys: _each(lambda x, y: _mm(x, bd(y)), xs, ys)
    cum = _each(lambda x: _mm_exact_lhs(tri, x, 3), lw)
    cum_last = _each(lambda c: c[CHUNK - 1:CHUNK, :], cum)
    p_incl = _each(jnp.exp, cum)
    p_excl = _each(lambda c, x: jnp.exp(c - x), cum, lw)
    p_inv = _each(lambda c: jnp.exp(-c), cum)
    to_end = _each(lambda cl, c: jnp.exp(cl - c), cum_last, cum)
    a_t = _each(lambda x, p: -x * p, kk, p_excl)
    r_t = _each(jnp.multiply, r, p_incl)
    b_t = _each(jnp.multiply, kka, p_inv)
    k_t = _each(jnp.multiply, k, p_inv)
    b_end = _each(jnp.multiply, kka, to_end)
    k_end = _each(jnp.multiply, k, to_end)

    ar = _each(lambda a, b: jnp.concatenate([a, b], axis=0), a_t, r_t)
    row2 = lax.broadcasted_iota(jnp.int32, (2 * CHUNK, QUAD), 0)
    col2 = lax.broadcasted_iota(jnp.int32, (2 * CHUNK, QUAD), 1) % CHUNK
    causal = col2 < (row2 % CHUNK) + (row2 // CHUNK)
    with_b = _each(lambda x, y: jnp.where(causal, _mm_nt(x, bd(y)), 0.0), ar, b_t)
    with_k = _each(lambda x, y: jnp.where(causal, _mm_nt(x, bd(y)), 0.0), ar, k_t)
    a_ab = _each(lambda x: x[:CHUNK], with_b)
    a_rb = _each(lambda x: x[CHUNK:], with_b)

    inv = _unit_lower_inverse(a_ab, t_idx, s_idx, bdmm)
    kv = bdmm(with_k, v)
    w_til = bdmm(inv, a_t)
    u_til = bdmm(inv, _each(lambda x: x[:CHUNK], kv))
    r_hat = _each(jnp.add, r_t, bdmm(a_rb, w_til))
    o_loc = _each(lambda x, y: x + y[CHUNK:], bdmm(a_rb, u_til), kv)

    trans = _each(lambda w, b: jnp.where(bd_mask, _mm(w.T, b), 0.0), w_til, b_end)
    d_bd = _each(lambda u, b, x, y: jnp.where(bd_mask, _mm(u.T, b) + _mm(x.T, y), 0.0), u_til, b_end, v, k_end)
    d_ls = _each(lambda d: sum(d[h * CHUNK:(h + 1) * CHUNK] for h in range(1, QUAD // CHUNK)) + d[:CHUNK], d_bd)
    p_end = _each(jnp.exp, cum_last)
    return r_hat, o_loc, trans, d_ls, p_end


def _rwkv_kernel(r_ref, lw_ref, k_ref, v_ref, kk_ref, kka_ref, tri_ref, o_ref, state_ref, *, n_sub):
    @pl.when(pl.program_id(1) == 0)
    def _():
        state_ref[...] = jnp.zeros_like(state_ref)

    t_idx = lax.broadcasted_iota(jnp.int32, (CHUNK, QUAD), 0)
    s_idx = lax.broadcasted_iota(jnp.int32, (CHUNK, QUAD), 1) % CHUNK
    bd_mask = (lax.broadcasted_iota(jnp.int32, (QUAD, QUAD), 0) // CHUNK
               == lax.broadcasted_iota(jnp.int32, (QUAD, QUAD), 1) // CHUNK)
    n_quad = D_RWKV // QUAD
    where = [(slice(c * CHUNK, (c + 1) * CHUNK), slice(q * QUAD, (q + 1) * QUAD))
             for c in range(n_sub) for q in range(n_quad)]
    load = lambda ref: [ref[0, rows, lanes].astype(F32) for rows, lanes in where]
    r_hat, o_loc, trans, d_ls, p_end = _chunk_local(
        load(r_ref), load(lw_ref), load(k_ref), load(v_ref), load(kk_ref), load(kka_ref),
        tri_ref[...], t_idx, s_idx, bd_mask)

    state = [state_ref[q] for q in range(n_quad)]
    for c in range(n_sub):
        chains = range(c * n_quad, (c + 1) * n_quad)
        out = [_mm_nt(r_hat[i], _block_diag(state[q], bd_mask)) + o_loc[i] for q, i in enumerate(chains)]
        for q, i in enumerate(chains):
            rows, lanes = where[i]
            o_ref[0, rows, lanes] = out[q]
        state = [state[q] * p_end[i] + _mm(state[q], trans[i]) + d_ls[i] for q, i in enumerate(chains)]
    for q in range(n_quad):
        state_ref[q] = state[q]


def _rwkv_call(r, lw, k, v, kk, kka, tri, tb):
    bsz, seq, _ = r.shape
    spec = pl.BlockSpec((1, tb, D_RWKV), lambda b, t: (b, t, 0))
    return pl.pallas_call(
        functools.partial(_rwkv_kernel, n_sub=tb // CHUNK),
        grid=(bsz, seq // tb),
        in_specs=[spec] * 6 + [pl.BlockSpec(tri.shape, lambda b, t: (0, 0))],
        out_specs=spec,
        out_shape=jax.ShapeDtypeStruct((bsz, seq, D_RWKV), F32),
        scratch_shapes=[pltpu.VMEM((D_RWKV // QUAD, HEAD, QUAD), F32)],
        compiler_params=pltpu.CompilerParams(
            dimension_semantics=("arbitrary", "arbitrary"), vmem_limit_bytes=VMEM_LIMIT),
        name="rwkv_chunk",
    )(r, lw, k, v, kk, kka, tri)


def _route(logits):
    lane = lax.broadcasted_iota(jnp.int32, logits.shape, 1)
    neg = -jnp.inf
    big = ROUTER_LANES
    is_group = (lane >= N_EXPERTS) & (lane < N_EXPERTS + N_GROUPS)
    gl = jnp.where(is_group, logits, neg)
    ge = jnp.exp(gl - jnp.max(gl, axis=-1, keepdims=True))
    gprob = ge / jnp.sum(ge, axis=-1, keepdims=True)
    g_top = jnp.max(gprob, axis=-1, keepdims=True)
    g_idx = jnp.min(jnp.where(is_group & (gprob == g_top), lane - N_EXPERTS, big), axis=-1, keepdims=True)

    in_group = (lane < N_EXPERTS) & ((lane // EXPERTS_PER_GROUP) == g_idx)
    el = jnp.where(in_group, logits, neg)
    top1 = jnp.max(el, axis=-1, keepdims=True)
    idx1 = jnp.min(jnp.where(in_group & (el == top1), lane, big), axis=-1, keepdims=True)
    el2 = jnp.where(lane == idx1, neg, el)
    top2 = jnp.max(el2, axis=-1, keepdims=True)
    idx2 = jnp.min(jnp.where(in_group & (lane != idx1) & (el2 == top2), lane, big), axis=-1, keepdims=True)
    e2 = jnp.exp(top2 - top1)
    denom = 1.0 + e2
    return idx1, idx2, g_top * (1.0 / denom), g_top * (e2 / denom)


def _pack_bf16_pairs(x):
    bits = lax.bitcast_convert_type(_bf(x).astype(F32), jnp.uint32)
    k = x.shape[1] // 2
    return lax.bitcast_convert_type((bits[:, :k] >> 16) | bits[:, k:], jnp.int32)


def _unpack_bf16_pairs(words):
    bits = lax.bitcast_convert_type(words, jnp.uint32)
    lo = lax.bitcast_convert_type(bits << 16, F32)
    hi = lax.bitcast_convert_type(bits & jnp.uint32(0xFFFF0000), F32)
    return _bf(jnp.concatenate([lo, hi], axis=1))


LANE_IDX1, LANE_IDX2, LANE_RANK1, LANE_RANK2, LANE_GATE1, LANE_GATE2 = range(6)


def _out_kernel(o_ref, bonus_ref, gate_ref, yconv_ref, x_ref, ones_ref, lnw_ref, lnb_ref,
                wout_ref, gffn_ref, rw_ref, rb_ref, tri_ref, h_ref, u_ref, route_ref, count_ref, seen_ref):
    @pl.when(pl.program_id(0) == 0)
    def _():
        seen_ref[...] = jnp.zeros_like(seen_ref)

    ones_quad = ones_ref[...]
    o = o_ref[...]
    inv_n = 1.0 / HEAD
    mean = _head_sum(o, ones_quad) * inv_n
    cen = o - mean
    var = _head_sum(cen * cen, ones_quad) * inv_n
    on = cen * lax.rsqrt(var + LN_X_EPS) * lnw_ref[...] + lnb_ref[...]
    y_rwkv = (on + bonus_ref[...]) * gate_ref[...]
    mixed = (jnp.dot(yconv_ref[...], wout_ref[:D_CONV, :], preferred_element_type=F32)
             + jnp.dot(_bf(y_rwkv), wout_ref[D_CONV:, :], preferred_element_type=F32))
    h = x_ref[...] + mixed
    h_ref[...] = h
    u = _rms_norm(h, gffn_ref[...])
    u_hi = _bf(u)
    u_ref[...] = _pack_bf16_pairs(u)
    u_lo = _bf(u - u_hi.astype(F32))
    by_hi = jnp.dot(u_hi, rw_ref[...], preferred_element_type=F32)
    by_lo = jnp.dot(u_lo, rw_ref[:, :ROUTER_LANES], preferred_element_type=F32)
    logits = by_hi[:, :ROUTER_LANES] + by_hi[:, ROUTER_LANES:] + by_lo + rb_ref[...]
    idx1, idx2, gate1, gate2 = _route(logits)

    lane = lax.broadcasted_iota(jnp.int32, logits.shape, 1)
    hit1 = lane == idx1
    hit2 = lane == idx2
    both = hit1.astype(F32) + hit2.astype(F32)
    before = jnp.dot(tri_ref[...], _bf(both), preferred_element_type=F32) + seen_ref[...]
    rank1 = jnp.sum(jnp.where(hit1, before, 0.0), axis=-1, keepdims=True)
    rank2 = jnp.sum(jnp.where(hit2, before, 0.0), axis=-1, keepdims=True)
    seen = seen_ref[...] + jnp.sum(both, axis=0, keepdims=True)
    seen_ref[...] = seen
    count_ref[...] = jnp.broadcast_to(seen, count_ref.shape)

    route = jnp.zeros(logits.shape, F32)
    for lane_id, col in ((LANE_IDX1, idx1.astype(F32)), (LANE_IDX2, idx2.astype(F32)), (LANE_RANK1, rank1),
                         (LANE_RANK2, rank2), (LANE_GATE1, gate1), (LANE_GATE2, gate2)):
        route = jnp.where(lane == lane_id, col, route)
    route_ref[...] = route


def _out_call(o, bonus, gate, yconv, x, ones_bd, ln_w, ln_b, w_out, g_ffn, router_w, router_b, tm):
    n_tok = x.shape[0]
    row = lambda width: pl.BlockSpec((tm, width), lambda i: (i, 0))
    full = lambda arr: pl.BlockSpec(arr.shape, lambda i: (0,) * arr.ndim, pipeline_mode=pl.Buffered(1))
    tri_strict = (jnp.arange(tm)[:, None] > jnp.arange(tm)[None, :]).astype(BF16)
    params = (ones_bd, ln_w, ln_b, w_out, g_ffn, router_w, router_b, tri_strict)
    return pl.pallas_call(
        _out_kernel,
        grid=(n_tok // tm,),
        in_specs=[row(D_RWKV)] * 4 + [row(D_MODEL)] + [full(p) for p in params],
        out_specs=[row(D_MODEL), row(D_MODEL // 2), row(ROUTER_LANES),
                   pl.BlockSpec((8, ROUTER_LANES), lambda i: (0, 0))],
        out_shape=[jax.ShapeDtypeStruct((n_tok, D_MODEL), F32),
                   jax.ShapeDtypeStruct((n_tok, D_MODEL // 2), jnp.int32),
                   jax.ShapeDtypeStruct((n_tok, ROUTER_LANES), F32),
                   jax.ShapeDtypeStruct((8, ROUTER_LANES), F32)],
        scratch_shapes=[pltpu.VMEM((1, ROUTER_LANES), F32)],
        compiler_params=pltpu.CompilerParams(
            dimension_semantics=("arbitrary",), vmem_limit_bytes=VMEM_LIMIT),
        name="out_proj_route",
    )(o, bonus, gate, yconv, x, *params)


SC_CORES = 2
SC_SUBCORES = 16
SC_ROWS = 64


def _sc_gather(table, idx):
    n_rows = idx.shape[0]
    width = table.shape[1]
    n_chunks = n_rows // (SC_CORES * SC_SUBCORES * SC_ROWS)
    mesh = plsc.VectorSubcoreMesh(core_axis_name="c", subcore_axis_name="s",
                                  num_cores=SC_CORES, num_subcores=SC_SUBCORES)

    def body(table_hbm, idx_hbm, out_hbm, idx_v, rows_v, sem):
        worker = lax.axis_index("s") * SC_CORES + lax.axis_index("c")

        @pl.loop(0, n_chunks)
        def _(j):
            chunk = worker * n_chunks + j
            pltpu.sync_copy(idx_hbm.at[chunk], idx_v)
            pltpu.async_copy(table_hbm.at[idx_v], rows_v, sem).wait()
            pltpu.sync_copy(rows_v, out_hbm.at[pl.ds(pl.multiple_of(chunk * SC_ROWS, SC_ROWS), SC_ROWS)])

    return pl.kernel(
        body,
        out_type=jax.ShapeDtypeStruct((n_rows, width), table.dtype),
        mesh=mesh,
        scratch_types=[pltpu.VMEM((SC_ROWS,), jnp.int32), pltpu.VMEM((SC_ROWS, width), table.dtype),
                       pltpu.SemaphoreType.DMA],
        name="sc_row_gather",
    )(table, idx.reshape(n_rows // SC_ROWS, SC_ROWS))


def _expert_kernel(tile_expert_ref, n_valid_ref, x_ref, wg_ref, wu_ref, wd_ref, y_ref):
    del tile_expert_ref

    @pl.when(pl.program_id(0) < n_valid_ref[0])
    def _():
        x = _unpack_bf16_pairs(x_ref[...])
        gate = jnp.dot(x, wg_ref[0], preferred_element_type=F32)
        up = jnp.dot(x, wu_ref[0], preferred_element_type=F32)
        hid = gate * jax.nn.sigmoid(gate) * up
        y_ref[...] = _pack_bf16_pairs(jnp.dot(_bf(hid), wd_ref[0], preferred_element_type=F32))


def _expert_call(tile_expert, n_valid, x_sorted, w_gate, w_up, w_down, tm):
    n_rows = x_sorted.shape[0]
    rows = pl.BlockSpec((tm, D_MODEL // 2), lambda i, te, nv: (jnp.minimum(i, nv[0] - 1), 0))
    return pl.pallas_call(
        _expert_kernel,
        grid_spec=pltpu.PrefetchScalarGridSpec(
            num_scalar_prefetch=2,
            grid=(n_rows // tm,),
            in_specs=[rows,
                      pl.BlockSpec((1, D_MODEL, D_EXPERT), lambda i, te, nv: (te[i], 0, 0)),
                      pl.BlockSpec((1, D_MODEL, D_EXPERT), lambda i, te, nv: (te[i], 0, 0)),
                      pl.BlockSpec((1, D_EXPERT, D_MODEL), lambda i, te, nv: (te[i], 0, 0))],
            out_specs=rows),
        out_shape=jax.ShapeDtypeStruct((n_rows, D_MODEL // 2), jnp.int32),
        compiler_params=pltpu.CompilerParams(
            dimension_semantics=("arbitrary",), vmem_limit_bytes=VMEM_LIMIT),
        name="moe_experts",
    )(tile_expert, n_valid, x_sorted, w_gate, w_up, w_down)


def _final_kernel(h_ref, y1_ref, y2_ref, route_ref, gfin_ref, out_ref):
    route = route_ref[...]
    gate1 = route[:, LANE_GATE1:LANE_GATE1 + 1]
    gate2 = route[:, LANE_GATE2:LANE_GATE2 + 1]
    moe = gate1 * _unpack_bf16_pairs(y1_ref[...]).astype(F32) + gate2 * _unpack_bf16_pairs(y2_ref[...]).astype(F32)
    out_ref[...] = _rms_norm(h_ref[...] + moe, gfin_ref[...])


def _final_call(h, y_pairs, route, g_final, tm):
    n_tok = h.shape[0]
    n_blocks = n_tok // tm
    return pl.pallas_call(
        _final_kernel,
        grid=(n_blocks,),
        in_specs=[pl.BlockSpec((tm, D_MODEL), lambda i: (i, 0)),
                  pl.BlockSpec((tm, D_MODEL // 2), lambda i: (i, 0)),
                  pl.BlockSpec((tm, D_MODEL // 2), lambda i: (i + n_blocks, 0)),
                  pl.BlockSpec((tm, ROUTER_LANES), lambda i: (i, 0)),
                  pl.BlockSpec((1, D_MODEL), lambda i: (0, 0))],
        out_specs=pl.BlockSpec((tm, D_MODEL), lambda i: (i, 0)),
        out_shape=jax.ShapeDtypeStruct((n_tok, D_MODEL), F32),
        compiler_params=pltpu.CompilerParams(
            dimension_semantics=("arbitrary",), vmem_limit_bytes=VMEM_LIMIT),
        name="moe_combine_norm",
    )(h, y_pairs, y_pairs, route, g_final)


def _dispatch_plan(route, counts, tm):
    n_tok = route.shape[0]
    n_tiles = (2 * n_tok) // tm + N_EXPERTS
    counts = counts[0, :N_EXPERTS].astype(jnp.int32)
    tiles_per = (counts + tm - 1) // tm
    tile_end = jnp.cumsum(tiles_per)
    row_start = (tile_end - tiles_per) * tm
    col = lambda lane_id: route[:, lane_id].astype(jnp.int32)
    pos = jnp.concatenate([row_start[col(LANE_IDX1)] + col(LANE_RANK1),
                           row_start[col(LANE_IDX2)] + col(LANE_RANK2)])
    tok = jnp.arange(n_tok, dtype=jnp.int32)
    src = jnp.zeros((n_tiles * tm,), jnp.int32).at[pos].set(jnp.concatenate([tok, tok]))
    n_valid = tile_end[-1:]
    tile = jnp.minimum(jnp.arange(n_tiles, dtype=jnp.int32), n_valid - 1)
    tile_expert = jnp.searchsorted(tile_end, tile, side="right").astype(jnp.int32)
    return pos, src, tile_expert, n_valid.astype(jnp.int32)


def _block(x, norm_mix_g, w_in, rwkv_mu, conv_w, decay_up, decay_base, aaa_up, aaa_base, gate_up,
           k_k, k_a, r_k, ln_x_w, ln_x_b, w_out, norm_ffn_g, router_group_w, router_group_b,
           router_expert_w, router_expert_b, expert_w_gate, expert_w_up, expert_w_down, norm_final_g,
           *, tm_in, tb_rwkv, tm_out, tm_expert, tm_final):
    bsz, seq, d_model = x.shape
    n_tok = bsz * seq
    row = lambda p: p.reshape(1, -1)
    half = LORA_WA // 2
    zeros = jnp.zeros((half, D_RWKV), F32)
    dup = _bf(jnp.concatenate([decay_up, zeros], axis=0))
    aup = _bf(jnp.concatenate([zeros, aaa_up], axis=0))
    head_of = jnp.arange(QUAD) // HEAD
    ones_quad = (head_of[:, None] == head_of[None, :]).astype(BF16)
    tri = (jnp.arange(CHUNK)[:, None] >= jnp.arange(CHUNK)[None, :]).astype(BF16)

    yconv, r, lw, k2, v, kk, kka, gate, bonus = _in_call(
        x, row(norm_mix_g), _bf(w_in), row(rwkv_mu), conv_w, dup, row(decay_base), aup, row(aaa_base),
        _bf(gate_up), row(k_k), row(k_a), row(r_k), ones_quad, tm_in)
    o = _rwkv_call(r, lw, k2, v, kk, kka, tri, tb_rwkv)

    pad = ROUTER_LANES - N_EXPERTS - N_GROUPS
    router_w = jnp.concatenate([router_expert_w, router_group_w, jnp.zeros((d_model, pad), F32)], axis=1)
    router_hi = _bf(router_w)
    router_split = jnp.concatenate([router_hi, _bf(router_w - router_hi.astype(F32))], axis=1)
    router_b = jnp.concatenate([router_expert_b, router_group_b, jnp.zeros((pad,), F32)]).reshape(1, -1)
    flat = lambda t: t.reshape(n_tok, t.shape[-1])
    h, u_pairs, route, counts = _out_call(
        flat(o), flat(bonus), flat(gate), flat(yconv), flat(x), ones_quad,
        row(ln_x_w), row(ln_x_b), _bf(w_out), row(norm_ffn_g), router_split, router_b, tm_out)

    pos, src, tile_expert, n_valid = _dispatch_plan(route, counts, tm_expert)
    x_sorted = _sc_gather(u_pairs, src)
    y_sorted = _expert_call(tile_expert, n_valid, x_sorted, _bf(expert_w_gate), _bf(expert_w_up),
                            _bf(expert_w_down), tm_expert)
    y_pairs = _sc_gather(y_sorted, pos)
    out = _final_call(h, y_pairs, route, row(norm_final_g), tm_final)
    return out.reshape(bsz, seq, d_model)


def kernel(x, norm_mix_g, w_in, rwkv_mu, conv_w, decay_up, decay_base, aaa_up, aaa_base, gate_up, k_k, k_a, r_k, ln_x_w, ln_x_b, w_out, norm_ffn_g, router_group_w, router_group_b, router_expert_w, router_expert_b, expert_w_gate, expert_w_up, expert_w_down, norm_final_g):
    return _block(x, norm_mix_g[0], w_in[0], rwkv_mu[0], conv_w[0], decay_up[0], decay_base[0],
                  aaa_up[0], aaa_base[0], gate_up[0], k_k[0], k_a[0], r_k[0].reshape(-1), ln_x_w[0],
                  ln_x_b[0], w_out[0], norm_ffn_g[0], router_group_w[0], router_group_b[0],
                  router_expert_w[0], router_expert_b[0], expert_w_gate[0], expert_w_up[0],
                  expert_w_down[0], norm_final_g,
                  tm_in=512, tb_rwkv=256, tm_out=512, tm_expert=256, tm_final=512)
```

```python
import functools

import jax
import jax.numpy as jnp
from jax import lax
from jax.experimental import pallas as pl
from jax.experimental.pallas import tpu as pltpu
from jax.experimental.pallas import tpu_sc as plsc

F32 = jnp.float32
BF16 = jnp.bfloat16

D_MODEL = 1024
D_CONV = 512
CONV_WIDTH = 3
N_HEADS = 8
HEAD = 64
D_RWKV = N_HEADS * HEAD
LORA_WA = 128
GATE_LORA = 128
D_RWKV_PROJ = 3 * D_RWKV + LORA_WA + GATE_LORA
D_IN = 3 * D_CONV + D_RWKV_PROJ
N_GROUPS = 4
EXPERTS_PER_GROUP = 8
N_EXPERTS = N_GROUPS * EXPERTS_PER_GROUP
D_EXPERT = D_MODEL // 4
RMS_EPS = 1e-6
LN_X_EPS = 64e-5
L2_EPS = 1e-12

LANES = 128
CHUNK = 64
QUAD = 4 * HEAD
ROUTER_LANES = 128

VMEM_LIMIT = 56 * 1024 * 1024


def _bf(x):
    return x.astype(BF16)


def _mm(a, b):
    return jnp.dot(_bf(a), _bf(b), preferred_element_type=F32)


def _mm_nt(a, b):
    return lax.dot_general(_bf(a), _bf(b), (((1,), (1,)), ((), ())), preferred_element_type=F32)


def _mm_exact_lhs(lhs_bf16, x, passes):
    acc = None
    rem = x
    for _ in range(passes):
        piece = _bf(rem)
        part = jnp.dot(lhs_bf16, piece, preferred_element_type=F32)
        acc = part if acc is None else acc + part
        rem = rem - piece.astype(F32)
    return acc


def _head_sum(x, ones_quad):
    xb = _bf(x)
    return jnp.concatenate(
        [jnp.dot(xb[:, q * QUAD:(q + 1) * QUAD], ones_quad, preferred_element_type=F32)
         for q in range(x.shape[1] // QUAD)], axis=1)


def _rms_norm(x, g):
    return x * lax.rsqrt(jnp.mean(x * x, axis=-1, keepdims=True) + RMS_EPS) * g


def _shift_rows(cur, prev_rows, k):
    rolled = pltpu.roll(cur, k, 0)
    prev_rolled = pltpu.roll(prev_rows, k, 0)
    n = cur.shape[0]
    head = jnp.concatenate([prev_rolled, rolled[8:]], axis=0) if n > 8 else prev_rolled
    row = lax.broadcasted_iota(jnp.int32, cur.shape, 0)
    return jnp.where(row < k, head, rolled)


def _in_kernel(x_ref, g_ref, w_ref, mu_ref, convw_ref, dup_ref, dbase_ref, aup_ref, abase_ref,
               gup_ref, kk_ref, ka_ref, rk_ref, ones_ref,
               yconv_ref, r_ref, lw_ref, k_ref, v_ref, kkn_ref, kka_ref, gate_ref, bonus_ref,
               carry_ref):
    @pl.when(pl.program_id(1) == 0)
    def _():
        carry_ref[...] = jnp.zeros_like(carry_ref)

    x = x_ref[0]
    u = _rms_norm(x, g_ref[...])
    z = jnp.dot(_bf(u), w_ref[...], preferred_element_type=F32)

    b_gate = z[:, :D_CONV]
    ch = z[:, D_CONV:2 * D_CONV] * z[:, 2 * D_CONV:3 * D_CONV]
    prev_ch = carry_ref[:, :D_CONV]
    conv = (convw_ref[2:3, :] * ch
            + convw_ref[1:2, :] * _shift_rows(ch, prev_ch, 1)
            + convw_ref[0:1, :] * _shift_rows(ch, prev_ch, 2))
    yconv_ref[0] = _bf(b_gate * conv)

    zr = z[:, 3 * D_CONV:]
    prev_zr = carry_ref[:, D_CONV:]
    zs = zr + (_shift_rows(zr, prev_zr, 1) - zr) * mu_ref[...]
    carry_ref[:, :D_CONV] = ch[-8:]
    carry_ref[:, D_CONV:] = zr[-8:]

    r = zs[:, :D_RWKV]
    k = zs[:, D_RWKV:2 * D_RWKV]
    v = zs[:, 2 * D_RWKV:3 * D_RWKV]
    wa_lo = zs[:, 3 * D_RWKV:3 * D_RWKV + LORA_WA]
    g_lo = zs[:, 3 * D_RWKV + LORA_WA:]

    dec_in = -(dbase_ref[...] + jnp.dot(_bf(jnp.tanh(wa_lo)), dup_ref[...], preferred_element_type=F32))
    softplus = jnp.maximum(dec_in, 0.0) + jnp.log(1.0 + jnp.exp(-jnp.abs(dec_in)))
    w = -softplus - 0.5
    lw_ref[0] = -jnp.exp(w)
    a = jax.nn.sigmoid(abase_ref[...] + jnp.dot(_bf(wa_lo), aup_ref[...], preferred_element_type=F32))
    gate_ref[0] = jnp.dot(_bf(jax.nn.sigmoid(g_lo)), gup_ref[...], preferred_element_type=F32)

    ones_quad = ones_ref[...]
    kk = k * kk_ref[...]
    norm = jnp.sqrt(_head_sum(kk * kk, ones_quad))
    kk = kk / jnp.maximum(norm, L2_EPS)
    k2 = k * (1.0 + (a - 1.0) * ka_ref[...])
    r_ref[0] = _bf(r)
    k_ref[0] = _bf(k2)
    v_ref[0] = _bf(v)
    kkn_ref[0] = _bf(kk)
    kka_ref[0] = _bf(kk * a)
    bonus_ref[0] = _head_sum(r * k2 * rk_ref[...], ones_quad) * v


def _in_call(x, g, w_in, mu, conv_w, dup, dbase, aup, abase, gup, k_k, k_a, r_k, ones_quad, tm):
    bsz, seq, _ = x.shape
    full = lambda arr: pl.BlockSpec(arr.shape, lambda b, t: (0,) * arr.ndim, pipeline_mode=pl.Buffered(1))
    out_spec = pl.BlockSpec((1, tm, D_RWKV), lambda b, t: (b, t, 0))
    out_dtypes = (BF16, BF16, F32, BF16, BF16, BF16, BF16, F32, F32)
    params = (g, w_in, mu, conv_w, dup, dbase, aup, abase, gup, k_k, k_a, r_k, ones_quad)
    return pl.pallas_call(
        _in_kernel,
        grid=(bsz, seq // tm),
        in_specs=[pl.BlockSpec((1, tm, D_MODEL), lambda b, t: (b, t, 0))] + [full(p) for p in params],
        out_specs=[out_spec] * 9,
        out_shape=[jax.ShapeDtypeStruct((bsz, seq, D_RWKV), dt) for dt in out_dtypes],
        scratch_shapes=[pltpu.VMEM((8, D_CONV + D_RWKV_PROJ), F32)],
        compiler_params=pltpu.CompilerParams(
            dimension_semantics=("arbitrary", "arbitrary"), vmem_limit_bytes=VMEM_LIMIT),
        name="in_proj",
    )(x, *params)


def _block_diag(y, bd_mask):
    return jnp.where(bd_mask, jnp.concatenate([y] * (QUAD // CHUNK), axis=0), 0.0)


def _each(fn, *lists):
    return [fn(*args) for args in zip(*lists)]


def _unit_lower_inverse(a_strict, t_idx, s_idx, bdmm):
    eye = (t_idx == s_idx).astype(F32)
    same8 = (t_idx // 8) == (s_idx // 8)
    a8 = _each(lambda a: jnp.where(same8, a, 0.0), a_strict)
    a8_2 = bdmm(a8, a8)
    a8_3 = bdmm(a8, a8_2)
    a8_4 = bdmm(a8_2, a8_2)
    inv = _each(lambda a, b, c: eye + a + b + c, a8, a8_2, a8_3)
    inv = _each(jnp.add, inv, bdmm(inv, a8_4))
    size = 16
    while size <= CHUNK:
        off = ((t_idx // size) == (s_idx // size)) & ((t_idx // (size // 2)) != (s_idx // (size // 2)))
        cross = bdmm(_each(lambda a: jnp.where(off, a, 0.0), a_strict), inv)
        inv = _each(jnp.add, inv, bdmm(inv, cross))
        size *= 2
    return inv


def _chunk_local(r, lw, k, v, kk, kka, tri, t_idx, s_idx, bd_mask):
    bd = lambda y: _block_diag(y, bd_mask)
    bdmm = lambda xs, ys: _each(lambda x, y: _mm(x, bd(y)), xs, ys)
    cum = _each(lambda x: _mm_exact_lhs(tri, x, 3), lw)
    cum_last = _each(lambda c: c[CHUNK - 1:CHUNK, :], cum)
    p_incl = _each(jnp.exp, cum)
    p_excl = _each(lambda c, x: jnp.exp(c - x), cum, lw)
    p_inv = _each(lambda c: jnp.exp(-c), cum)
    to_end = _each(lambda cl, c: jnp.exp(cl - c), cum_last, cum)
    a_t = _each(lambda x, p: -x * p, kk, p_excl)
    r_t = _each(jnp.multiply, r, p_incl)
    b_t = _each(jnp.multiply, kka, p_inv)
    k_t = _each(jnp.multiply, k, p_inv)
    b_end = _each(jnp.multiply, kka, to_end)
    k_end = _each(jnp.multiply, k, to_end)

    ar = _each(lambda a, b: jnp.concatenate([a, b], axis=0), a_t, r_t)
    row2 = lax.broadcasted_iota(jnp.int32, (2 * CHUNK, QUAD), 0)
    col2 = lax.broadcasted_iota(jnp.int32, (2 * CHUNK, QUAD), 1) % CHUNK
    causal = col2 < (row2 % CHUNK) + (row2 // CHUNK)
    with_b = _each(lambda x, y: jnp.where(causal, _mm_nt(x, bd(y)), 0.0), ar, b_t)
    with_k = _each(lambda x, y: jnp.where(causal, _mm_nt(x, bd(y)), 0.0), ar, k_t)
    a_ab = _each(lambda x: x[:CHUNK], with_b)
    a_rb = _each(lambda x: x[CHUNK:], with_b)

    inv = _unit_lower_inverse(a_ab, t_idx, s_idx, bdmm)
    kv = bdmm(with_k, v)
    w_til = bdmm(inv, a_t)
    u_til = bdmm(inv, _each(lambda x: x[:CHUNK], kv))
    r_hat = _each(jnp.add, r_t, bdmm(a_rb, w_til))
    o_loc = _each(lambda x, y: x + y[CHUNK:], bdmm(a_rb, u_til), kv)

    trans = _each(lambda w, b: jnp.where(bd_mask, _mm(w.T, b), 0.0), w_til, b_end)
    d_bd = _each(lambda u, b, x, y: jnp.where(bd_mask, _mm(u.T, b) + _mm(x.T, y), 0.0), u_til, b_end, v, k_end)
    d_ls = _each(lambda d: sum(d[h * CHUNK:(h + 1) * CHUNK] for h in range(1, QUAD // CHUNK)) + d[:CHUNK], d_bd)
    p_end = _each(jnp.exp, cum_last)
    return r_hat, o_loc, trans, d_ls, p_end


def _rwkv_kernel(r_ref, lw_ref, k_ref, v_ref, kk_ref, kka_ref, tri_ref, o_ref, state_ref, *, n_sub):
    @pl.when(pl.program_id(1) == 0)
    def _():
        state_ref[...] = jnp.zeros_like(state_ref)

    t_idx = lax.broadcasted_iota(jnp.int32, (CHUNK, QUAD), 0)
    s_idx = lax.broadcasted_iota(jnp.int32, (CHUNK, QUAD), 1) % CHUNK
    bd_mask = (lax.broadcasted_iota(jnp.int32, (QUAD, QUAD), 0) // CHUNK
               == lax.broadcasted_iota(jnp.int32, (QUAD, QUAD), 1) // CHUNK)
    n_quad = D_RWKV // QUAD
    where = [(slice(c * CHUNK, (c + 1) * CHUNK), slice(q * QUAD, (q + 1) * QUAD))
             for c in range(n_sub) for q in range(n_quad)]
    load = lambda ref: [ref[0, rows, lanes].astype(F32) for rows, lanes in where]
    r_hat, o_loc, trans, d_ls, p_end = _chunk_local(
        load(r_ref), load(lw_ref), load(k_ref), load(v_ref), load(kk_ref), load(kka_ref),
        tri_ref[...], t_idx, s_idx, bd_mask)

    state = [state_ref[q] for q in range(n_quad)]
    for c in range(n_sub):
        chains = range(c * n_quad, (c + 1) * n_quad)
        out = [_mm_nt(r_hat[i], _block_diag(state[q], bd_mask)) + o_loc[i] for q, i in enumerate(chains)]
        for q, i in enumerate(chains):
            rows, lanes = where[i]
            o_ref[0, rows, lanes] = out[q]
        state = [state[q] * p_end[i] + _mm(state[q], trans[i]) + d_ls[i] for q, i in enumerate(chains)]
    for q in range(n_quad):
        state_ref[q] = state[q]


def _rwkv_call(r, lw, k, v, kk, kka, tri, tb):
    bsz, seq, _ = r.shape
    spec = pl.BlockSpec((1, tb, D_RWKV), lambda b, t: (b, t, 0))
    return pl.pallas_call(
        functools.partial(_rwkv_kernel, n_sub=tb // CHUNK),
        grid=(bsz, seq // tb),
        in_specs=[spec] * 6 + [pl.BlockSpec(tri.shape, lambda b, t: (0, 0))],
        out_specs=spec,
        out_shape=jax.ShapeDtypeStruct((bsz, seq, D_RWKV), F32),
        scratch_shapes=[pltpu.VMEM((D_RWKV // QUAD, HEAD, QUAD), F32)],
        compiler_params=pltpu.CompilerParams(
            dimension_semantics=("arbitrary", "arbitrary"), vmem_limit_bytes=VMEM_LIMIT),
        name="rwkv_chunk",
    )(r, lw, k, v, kk, kka, tri)


def _route(logits):
    lane = lax.broadcasted_iota(jnp.int32, logits.shape, 1)
    neg = -jnp.inf
    big = ROUTER_LANES
    is_group = (lane >= N_EXPERTS) & (lane < N_EXPERTS + N_GROUPS)
    gl = jnp.where(is_group, logits, neg)
    ge = jnp.exp(gl - jnp.max(gl, axis=-1, keepdims=True))
    gprob = ge / jnp.sum(ge, axis=-1, keepdims=True)
    g_top = jnp.max(gprob, axis=-1, keepdims=True)
    g_idx = jnp.min(jnp.where(is_group & (gprob == g_top), lane - N_EXPERTS, big), axis=-1, keepdims=True)

    in_group = (lane < N_EXPERTS) & ((lane // EXPERTS_PER_GROUP) == g_idx)
    el = jnp.where(in_group, logits, neg)
    top1 = jnp.max(el, axis=-1, keepdims=True)
    idx1 = jnp.min(jnp.where(in_group & (el == top1), lane, big), axis=-1, keepdims=True)
    el2 = jnp.where(lane == idx1, neg, el)
    top2 = jnp.max(el2, axis=-1, keepdims=True)
    idx2 = jnp.min(jnp.where(in_group & (lane != idx1) & (el2 == top2), lane, big), axis=-1, keepdims=True)
    e2 = jnp.exp(top2 - top1)
    denom = 1.0 + e2
    return idx1, idx2, g_top * (1.0 / denom), g_top * (e2 / denom)


def _pack_bf16_pairs(x):
    bits = lax.bitcast_convert_type(_bf(x).astype(F32), jnp.uint32)
    k = x.shape[1] // 2
    return lax.bitcast_convert_type((bits[:, :k] >> 16) | bits[:, k:], jnp.int32)


def _unpack_bf16_pairs(words):
    bits = lax.bitcast_convert_type(words, jnp.uint32)
    lo = lax.bitcast_convert_type(bits << 16, F32)
    hi = lax.bitcast_convert_type(bits & jnp.uint32(0xFFFF0000), F32)
    return _bf(jnp.concatenate([lo, hi], axis=1))


LANE_IDX1, LANE_IDX2, LANE_RANK1, LANE_RANK2, LANE_GATE1, LANE_GATE2 = range(6)


def _out_kernel(o_ref, bonus_ref, gate_ref, yconv_ref, x_ref, ones_ref, lnw_ref, lnb_ref,
                wout_ref, gffn_ref, rw_ref, rb_ref, tri_ref, h_ref, u_ref, route_ref, count_ref, seen_ref):
    @pl.when(pl.program_id(0) == 0)
    def _():
        seen_ref[...] = jnp.zeros_like(seen_ref)

    ones_quad = ones_ref[...]
    o = o_ref[...]
    inv_n = 1.0 / HEAD
    mean = _head_sum(o, ones_quad) * inv_n
    cen = o - mean
    var = _head_sum(cen * cen, ones_quad) * inv_n
    on = cen * lax.rsqrt(var + LN_X_EPS) * lnw_ref[...] + lnb_ref[...]
    y_rwkv = (on + bonus_ref[...]) * gate_ref[...]
    mixed = (jnp.dot(yconv_ref[...], wout_ref[:D_CONV, :], preferred_element_type=F32)
             + jnp.dot(_bf(y_rwkv), wout_ref[D_CONV:, :], preferred_element_type=F32))
    h = x_ref[...] + mixed
    h_ref[...] = h
    u = _rms_norm(h, gffn_ref[...])
    u_hi = _bf(u)
    u_ref[...] = _pack_bf16_pairs(u)
    u_lo = _bf(u - u_hi.astype(F32))
    by_hi = jnp.dot(u_hi, rw_ref[...], preferred_element_type=F32)
    by_lo = jnp.dot(u_lo, rw_ref[:, :ROUTER_LANES], preferred_element_type=F32)
    logits = by_hi[:, :ROUTER_LANES] + by_hi[:, ROUTER_LANES:] + by_lo + rb_ref[...]
    idx1, idx2, gate1, gate2 = _route(logits)

    lane = lax.broadcasted_iota(jnp.int32, logits.shape, 1)
    hit1 = lane == idx1
    hit2 = lane == idx2
    both = hit1.astype(F32) + hit2.astype(F32)
    before = jnp.dot(tri_ref[...], _bf(both), preferred_element_type=F32) + seen_ref[...]
    rank1 = jnp.sum(jnp.where(hit1, before, 0.0), axis=-1, keepdims=True)
    rank2 = jnp.sum(jnp.where(hit2, before, 0.0), axis=-1, keepdims=True)
    seen = seen_ref[...] + jnp.sum(both, axis=0, keepdims=True)
    seen_ref[...] = seen
    count_ref[...] = jnp.broadcast_to(seen, count_ref.shape)

    route = jnp.zeros(logits.shape, F32)
    for lane_id, col in ((LANE_IDX1, idx1.astype(F32)), (LANE_IDX2, idx2.astype(F32)), (LANE_RANK1, rank1),
                         (LANE_RANK2, rank2), (LANE_GATE1, gate1), (LANE_GATE2, gate2)):
        route = jnp.where(lane == lane_id, col, route)
    route_ref[...] = route


def _out_call(o, bonus, gate, yconv, x, ones_bd, ln_w, ln_b, w_out, g_ffn, router_w, router_b, tm):
    n_tok = x.shape[0]
    row = lambda width: pl.BlockSpec((tm, width), lambda i: (i, 0))
    full = lambda arr: pl.BlockSpec(arr.shape, lambda i: (0,) * arr.ndim, pipeline_mode=pl.Buffered(1))
    tri_strict = (jnp.arange(tm)[:, None] > jnp.arange(tm)[None, :]).astype(BF16)
    params = (ones_bd, ln_w, ln_b, w_out, g_ffn, router_w, router_b, tri_strict)
    return pl.pallas_call(
        _out_kernel,
        grid=(n_tok // tm,),
        in_specs=[row(D_RWKV)] * 4 + [row(D_MODEL)] + [full(p) for p in params],
        out_specs=[row(D_MODEL), row(D_MODEL // 2), row(ROUTER_LANES),
                   pl.BlockSpec((8, ROUTER_LANES), lambda i: (0, 0))],
        out_shape=[jax.ShapeDtypeStruct((n_tok, D_MODEL), F32),
                   jax.ShapeDtypeStruct((n_tok, D_MODEL // 2), jnp.int32),
                   jax.ShapeDtypeStruct((n_tok, ROUTER_LANES), F32),
                   jax.ShapeDtypeStruct((8, ROUTER_LANES), F32)],
        scratch_shapes=[pltpu.VMEM((1, ROUTER_LANES), F32)],
        compiler_params=pltpu.CompilerParams(
            dimension_semantics=("arbitrary",), vmem_limit_bytes=VMEM_LIMIT),
        name="out_proj_route",
    )(o, bonus, gate, yconv, x, *params)


SC_CORES = 2
SC_SUBCORES = 16
SC_ROWS = 64


def _sc_gather(table, idx):
    n_rows = idx.shape[0]
    width = table.shape[1]
    n_chunks = n_rows // (SC_CORES * SC_SUBCORES * SC_ROWS)
    mesh = plsc.VectorSubcoreMesh(core_axis_name="c", subcore_axis_name="s",
                                  num_cores=SC_CORES, num_subcores=SC_SUBCORES)

    def body(table_hbm, idx_hbm, out_hbm, idx_v, rows_v, sem):
        worker = lax.axis_index("s") * SC_CORES + lax.axis_index("c")

        @pl.loop(0, n_chunks)
        def _(j):
            chunk = worker * n_chunks + j
            pltpu.sync_copy(idx_hbm.at[chunk], idx_v)
            pltpu.async_copy(table_hbm.at[idx_v], rows_v, sem).wait()
            pltpu.sync_copy(rows_v, out_hbm.at[pl.ds(pl.multiple_of(chunk * SC_ROWS, SC_ROWS), SC_ROWS)])

    return pl.kernel(
        body,
        out_type=jax.ShapeDtypeStruct((n_rows, width), table.dtype),
        mesh=mesh,
        scratch_types=[pltpu.VMEM((SC_ROWS,), jnp.int32), pltpu.VMEM((SC_ROWS, width), table.dtype),
                       pltpu.SemaphoreType.DMA],
        name="sc_row_gather",
    )(table, idx.reshape(n_rows // SC_ROWS, SC_ROWS))


def _sc_scatter(rows, pos, n_out):
    n_rows, width = rows.shape
    n_slots = pos.shape[0] // n_rows
    slot_chunks = n_rows // SC_ROWS
    n_chunks = slot_chunks // (SC_CORES * SC_SUBCORES)
    mesh = plsc.VectorSubcoreMesh(core_axis_name="c", subcore_axis_name="s",
                                  num_cores=SC_CORES, num_subcores=SC_SUBCORES)

    def body(rows_hbm, pos_hbm, out_hbm, idx_v, rows_v, sem):
        worker = lax.axis_index("s") * SC_CORES + lax.axis_index("c")

        @pl.loop(0, n_chunks)
        def _(j):
            chunk = worker * n_chunks + j
            pltpu.sync_copy(rows_hbm.at[pl.ds(pl.multiple_of(chunk * SC_ROWS, SC_ROWS), SC_ROWS)], rows_v)
            for s in range(n_slots):
                pltpu.sync_copy(pos_hbm.at[s * slot_chunks + chunk], idx_v.at[s])
            copies = [pltpu.async_copy(rows_v, out_hbm.at[idx_v.at[s]], sem) for s in range(n_slots)]
            for copy in copies:
                copy.wait()

    return pl.kernel(
        body,
        out_type=jax.ShapeDtypeStruct((n_out, width), rows.dtype),
        mesh=mesh,
        scratch_types=[pltpu.VMEM((n_slots, SC_ROWS), jnp.int32), pltpu.VMEM((SC_ROWS, width), rows.dtype),
                       pltpu.SemaphoreType.DMA],
        name="sc_row_scatter",
    )(rows, pos.reshape(n_slots * slot_chunks, SC_ROWS))


def _expert_kernel(tile_expert_ref, n_valid_ref, x_ref, wg_ref, wu_ref, wd_ref, y_ref):
    del tile_expert_ref

    @pl.when(pl.program_id(0) < n_valid_ref[0])
    def _():
        x = _unpack_bf16_pairs(x_ref[...])
        gate = jnp.dot(x, wg_ref[0], preferred_element_type=F32)
        up = jnp.dot(x, wu_ref[0], preferred_element_type=F32)
        hid = gate * jax.nn.sigmoid(gate) * up
        y_ref[...] = _pack_bf16_pairs(jnp.dot(_bf(hid), wd_ref[0], preferred_element_type=F32))


def _expert_call(tile_expert, n_valid, x_sorted, w_gate, w_up, w_down, tm):
    n_rows = x_sorted.shape[0]
    rows = pl.BlockSpec((tm, D_MODEL // 2), lambda i, te, nv: (jnp.minimum(i, nv[0] - 1), 0))
    return pl.pallas_call(
        _expert_kernel,
        grid_spec=pltpu.PrefetchScalarGridSpec(
            num_scalar_prefetch=2,
            grid=(n_rows // tm,),
            in_specs=[rows,
                      pl.BlockSpec((1, D_MODEL, D_EXPERT), lambda i, te, nv: (te[i], 0, 0)),
                      pl.BlockSpec((1, D_MODEL, D_EXPERT), lambda i, te, nv: (te[i], 0, 0)),
                      pl.BlockSpec((1, D_EXPERT, D_MODEL), lambda i, te, nv: (te[i], 0, 0))],
            out_specs=rows),
        out_shape=jax.ShapeDtypeStruct((n_rows, D_MODEL // 2), jnp.int32),
        compiler_params=pltpu.CompilerParams(
            dimension_semantics=("arbitrary",), vmem_limit_bytes=VMEM_LIMIT),
        name="moe_experts",
    )(tile_expert, n_valid, x_sorted, w_gate, w_up, w_down)


def _final_kernel(h_ref, y1_ref, y2_ref, route_ref, gfin_ref, out_ref):
    route = route_ref[...]
    gate1 = route[:, LANE_GATE1:LANE_GATE1 + 1]
    gate2 = route[:, LANE_GATE2:LANE_GATE2 + 1]
    moe = gate1 * _unpack_bf16_pairs(y1_ref[...]).astype(F32) + gate2 * _unpack_bf16_pairs(y2_ref[...]).astype(F32)
    out_ref[...] = _rms_norm(h_ref[...] + moe, gfin_ref[...])


def _final_call(h, y_pairs, route, g_final, tm):
    n_tok = h.shape[0]
    n_blocks = n_tok // tm
    return pl.pallas_call(
        _final_kernel,
        grid=(n_blocks,),
        in_specs=[pl.BlockSpec((tm, D_MODEL), lambda i: (i, 0)),
                  pl.BlockSpec((tm, D_MODEL // 2), lambda i: (i, 0)),
                  pl.BlockSpec((tm, D_MODEL // 2), lambda i: (i + n_blocks, 0)),
                  pl.BlockSpec((tm, ROUTER_LANES), lambda i: (i, 0)),
                  pl.BlockSpec((1, D_MODEL), lambda i: (0, 0))],
        out_specs=pl.BlockSpec((tm, D_MODEL), lambda i: (i, 0)),
        out_shape=jax.ShapeDtypeStruct((n_tok, D_MODEL), F32),
        compiler_params=pltpu.CompilerParams(
            dimension_semantics=("arbitrary",), vmem_limit_bytes=VMEM_LIMIT),
        name="moe_combine_norm",
    )(h, y_pairs, y_pairs, route, g_final)


def _dispatch_plan(route, counts, tm):
    n_tok = route.shape[0]
    n_tiles = (2 * n_tok) // tm + N_EXPERTS
    counts = counts[0, :N_EXPERTS].astype(jnp.int32)
    tiles_per = (counts + tm - 1) // tm
    tile_end = jnp.cumsum(tiles_per)
    row_start = (tile_end - tiles_per) * tm
    experts = jnp.arange(N_EXPERTS, dtype=jnp.int32)

    def position(idx_lane, rank_lane):
        idx = route[:, idx_lane].astype(jnp.int32)
        start = jnp.sum(jnp.where(idx[:, None] == experts[None, :], row_start[None, :], 0), axis=1)
        return start + route[:, rank_lane].astype(jnp.int32)

    pos = jnp.concatenate([position(LANE_IDX1, LANE_RANK1), position(LANE_IDX2, LANE_RANK2)])
    n_valid = tile_end[-1:]
    tile = jnp.minimum(jnp.arange(n_tiles, dtype=jnp.int32), n_valid - 1)
    tile_expert = jnp.sum((tile_end[None, :] <= tile[:, None]).astype(jnp.int32), axis=1)
    return pos, n_tiles, tile_expert, n_valid


def _block(x, norm_mix_g, w_in, rwkv_mu, conv_w, decay_up, decay_base, aaa_up, aaa_base, gate_up,
           k_k, k_a, r_k, ln_x_w, ln_x_b, w_out, norm_ffn_g, router_group_w, router_group_b,
           router_expert_w, router_expert_b, expert_w_gate, expert_w_up, expert_w_down, norm_final_g,
           *, tm_in, tb_rwkv, tm_out, tm_expert, tm_final):
    bsz, seq, d_model = x.shape
    n_tok = bsz * seq
    row = lambda p: p.reshape(1, -1)
    half = LORA_WA // 2
    zeros = jnp.zeros((half, D_RWKV), F32)
    dup = _bf(jnp.concatenate([decay_up, zeros], axis=0))
    aup = _bf(jnp.concatenate([zeros, aaa_up], axis=0))
    head_of = jnp.arange(QUAD) // HEAD
    ones_quad = (head_of[:, None] == head_of[None, :]).astype(BF16)
    tri = (jnp.arange(CHUNK)[:, None] >= jnp.arange(CHUNK)[None, :]).astype(BF16)

    yconv, r, lw, k2, v, kk, kka, gate, bonus = _in_call(
        x, row(norm_mix_g), _bf(w_in), row(rwkv_mu), conv_w, dup, row(decay_base), aup, row(aaa_base),
        _bf(gate_up), row(k_k), row(k_a), row(r_k), ones_quad, tm_in)
    o = _rwkv_call(r, lw, k2, v, kk, kka, tri, tb_rwkv)

    pad = ROUTER_LANES - N_EXPERTS - N_GROUPS
    router_w = jnp.concatenate([router_expert_w, router_group_w, jnp.zeros((d_model, pad), F32)], axis=1)
    router_hi = _bf(router_w)
    router_split = jnp.concatenate([router_hi, _bf(router_w - router_hi.astype(F32))], axis=1)
    router_b = jnp.concatenate([router_expert_b, router_group_b, jnp.zeros((pad,), F32)]).reshape(1, -1)
    flat = lambda t: t.reshape(n_tok, t.shape[-1])
    h, u_pairs, route, counts = _out_call(
        flat(o), flat(bonus), flat(gate), flat(yconv), flat(x), ones_quad,
        row(ln_x_w), row(ln_x_b), _bf(w_out), row(norm_ffn_g), router_split, router_b, tm_out)

    pos, n_tiles, tile_expert, n_valid = _dispatch_plan(route, counts, tm_expert)
    x_sorted = _sc_scatter(u_pairs, pos, n_tiles * tm_expert)
    y_sorted = _expert_call(tile_expert, n_valid, x_sorted, _bf(expert_w_gate), _bf(expert_w_up),
                            _bf(expert_w_down), tm_expert)
    y_pairs = _sc_gather(y_sorted, pos)
    out = _final_call(h, y_pairs, route, row(norm_final_g), tm_final)
    return out.reshape(bsz, seq, d_model)


def kernel(x, norm_mix_g, w_in, rwkv_mu, conv_w, decay_up, decay_base, aaa_up, aaa_base, gate_up, k_k, k_a, r_k, ln_x_w, ln_x_b, w_out, norm_ffn_g, router_group_w, router_group_b, router_expert_w, router_expert_b, expert_w_gate, expert_w_up, expert_w_down, norm_final_g):
    return _block(x, norm_mix_g[0], w_in[0], rwkv_mu[0], conv_w[0], decay_up[0], decay_base[0],
                  aaa_up[0], aaa_base[0], gate_up[0], k_k[0], k_a[0], r_k[0].reshape(-1), ln_x_w[0],
                  ln_x_b[0], w_out[0], norm_ffn_g[0], router_group_w[0], router_group_b[0],
                  router_expert_w[0], router_expert_b[0], expert_w_gate[0], expert_w_up[0],
                  expert_w_down[0], norm_final_g,
                  tm_in=512, tb_rwkv=256, tm_out=512, tm_expert=512, tm_final=512)
```

```python
import functools

import jax
import jax.numpy as jnp
from jax import lax
from jax.experimental import pallas as pl
from jax.experimental.pallas import tpu as pltpu
from jax.experimental.pallas import tpu_sc as plsc

F32 = jnp.float32
BF16 = jnp.bfloat16

D_MODEL = 1024
D_CONV = 512
CONV_WIDTH = 3
N_HEADS = 8
HEAD = 64
D_RWKV = N_HEADS * HEAD
LORA_WA = 128
GATE_LORA = 128
D_RWKV_PROJ = 3 * D_RWKV + LORA_WA + GATE_LORA
D_IN = 3 * D_CONV + D_RWKV_PROJ
N_GROUPS = 4
EXPERTS_PER_GROUP = 8
N_EXPERTS = N_GROUPS * EXPERTS_PER_GROUP
D_EXPERT = D_MODEL // 4
RMS_EPS = 1e-6
LN_X_EPS = 64e-5
L2_EPS = 1e-12

LANES = 128
CHUNK = 64
QUAD = 4 * HEAD
GROUP = 2 * HEAD
IN_SPLIT = 4
OUT_SPLIT = 4
ROUTER_LANES = 128

VMEM_LIMIT = 56 * 1024 * 1024


def _bf(x):
    return x.astype(BF16)


def _mm(a, b):
    return jnp.dot(_bf(a), _bf(b), preferred_element_type=F32)


def _mm_nt(a, b):
    return lax.dot_general(_bf(a), _bf(b), (((1,), (1,)), ((), ())), preferred_element_type=F32)


def _mm_exact_lhs(lhs_bf16, x, passes):
    acc = None
    rem = x
    for _ in range(passes):
        piece = _bf(rem)
        part = jnp.dot(lhs_bf16, piece, preferred_element_type=F32)
        acc = part if acc is None else acc + part
        rem = rem - piece.astype(F32)
    return acc


def _head_sum(x, ones_quad):
    xb = _bf(x)
    return jnp.concatenate(
        [jnp.dot(xb[:, q * QUAD:(q + 1) * QUAD], ones_quad, preferred_element_type=F32)
         for q in range(x.shape[1] // QUAD)], axis=1)


def _rms_norm(x, g):
    return x * lax.rsqrt(jnp.mean(x * x, axis=-1, keepdims=True) + RMS_EPS) * g


def _shift_rows(cur, prev_rows, k):
    rolled = pltpu.roll(cur, k, 0)
    prev_rolled = pltpu.roll(prev_rows, k, 0)
    n = cur.shape[0]
    head = jnp.concatenate([prev_rolled, rolled[8:]], axis=0) if n > 8 else prev_rolled
    row = lax.broadcasted_iota(jnp.int32, cur.shape, 0)
    return jnp.where(row < k, head, rolled)


def _in_kernel(x_ref, g_ref, w_ref, mu_ref, convw_ref, dup_ref, dbase_ref, aup_ref, abase_ref,
               gup_ref, kk_ref, ka_ref, rk_ref, ones_ref,
               yconv_ref, r_ref, lw_ref, k_ref, v_ref, kkn_ref, kka_ref, gate_ref, bonus_ref,
               carry_ref, *, n_split):
    @pl.when(pl.program_id(1) == 0)
    def _():
        carry_ref[...] = jnp.zeros_like(carry_ref)

    sub = x_ref.shape[1] // n_split
    z_parts = []
    for s in range(n_split):
        u = _rms_norm(x_ref[0, s * sub:(s + 1) * sub, :], g_ref[...])
        z_parts.append(jnp.dot(_bf(u), w_ref[...], preferred_element_type=F32))
    prev_ch = carry_ref[:, :D_CONV]
    prev_zr = carry_ref[:, D_CONV:]
    outs = (yconv_ref, r_ref, lw_ref, k_ref, v_ref, kkn_ref, kka_ref, gate_ref, bonus_ref)
    for s, z in enumerate(z_parts):
        rows = slice(s * sub, (s + 1) * sub)
        prev_ch, prev_zr = _in_post(z, prev_ch, prev_zr, mu_ref, convw_ref, dup_ref, dbase_ref, aup_ref,
                                    abase_ref, gup_ref, kk_ref, ka_ref, rk_ref, ones_ref,
                                    [ref.at[0, rows, :] for ref in outs])
    carry_ref[:, :D_CONV] = prev_ch
    carry_ref[:, D_CONV:] = prev_zr


def _in_post(z, prev_ch, prev_zr, mu_ref, convw_ref, dup_ref, dbase_ref, aup_ref, abase_ref,
             gup_ref, kk_ref, ka_ref, rk_ref, ones_ref, outs):
    yconv_ref, r_ref, lw_ref, k_ref, v_ref, kkn_ref, kka_ref, gate_ref, bonus_ref = outs
    b_gate = z[:, :D_CONV]
    ch = z[:, D_CONV:2 * D_CONV] * z[:, 2 * D_CONV:3 * D_CONV]
    conv = (convw_ref[2:3, :] * ch
            + convw_ref[1:2, :] * _shift_rows(ch, prev_ch, 1)
            + convw_ref[0:1, :] * _shift_rows(ch, prev_ch, 2))
    yconv_ref[...] = _bf(b_gate * conv)

    zr = z[:, 3 * D_CONV:]
    zs = zr + (_shift_rows(zr, prev_zr, 1) - zr) * mu_ref[...]

    r = zs[:, :D_RWKV]
    k = zs[:, D_RWKV:2 * D_RWKV]
    v = zs[:, 2 * D_RWKV:3 * D_RWKV]
    wa_lo = zs[:, 3 * D_RWKV:3 * D_RWKV + LORA_WA]
    g_lo = zs[:, 3 * D_RWKV + LORA_WA:]

    dec_in = -(dbase_ref[...] + jnp.dot(_bf(jnp.tanh(wa_lo)), dup_ref[...], preferred_element_type=F32))
    softplus = jnp.maximum(dec_in, 0.0) + jnp.log(1.0 + jnp.exp(-jnp.abs(dec_in)))
    w = -softplus - 0.5
    lw_ref[...] = -jnp.exp(w)
    a = jax.nn.sigmoid(abase_ref[...] + jnp.dot(_bf(wa_lo), aup_ref[...], preferred_element_type=F32))
    gate_ref[...] = jnp.dot(_bf(jax.nn.sigmoid(g_lo)), gup_ref[...], preferred_element_type=F32)

    ones_quad = ones_ref[...]
    kk = k * kk_ref[...]
    norm = jnp.sqrt(_head_sum(kk * kk, ones_quad))
    kk = kk / jnp.maximum(norm, L2_EPS)
    k2 = k * (1.0 + (a - 1.0) * ka_ref[...])
    r_ref[...] = _bf(r)
    k_ref[...] = _bf(k2)
    v_ref[...] = _bf(v)
    kkn_ref[...] = _bf(kk)
    kka_ref[...] = _bf(kk * a)
    bonus_ref[...] = _head_sum(r * k2 * rk_ref[...], ones_quad) * v
    return ch[-8:], zr[-8:]


def _in_call(x, g, w_in, mu, conv_w, dup, dbase, aup, abase, gup, k_k, k_a, r_k, ones_quad, tm, n_split):
    bsz, seq, _ = x.shape
    full = lambda arr: pl.BlockSpec(arr.shape, lambda b, t: (0,) * arr.ndim, pipeline_mode=pl.Buffered(1))
    out_spec = pl.BlockSpec((1, tm, D_RWKV), lambda b, t: (b, t, 0))
    out_dtypes = (BF16, BF16, F32, BF16, BF16, BF16, BF16, F32, F32)
    params = (g, w_in, mu, conv_w, dup, dbase, aup, abase, gup, k_k, k_a, r_k, ones_quad)
    return pl.pallas_call(
        functools.partial(_in_kernel, n_split=n_split),
        grid=(bsz, seq // tm),
        in_specs=[pl.BlockSpec((1, tm, D_MODEL), lambda b, t: (b, t, 0))] + [full(p) for p in params],
        out_specs=[out_spec] * 9,
        out_shape=[jax.ShapeDtypeStruct((bsz, seq, D_RWKV), dt) for dt in out_dtypes],
        scratch_shapes=[pltpu.VMEM((8, D_CONV + D_RWKV_PROJ), F32)],
        compiler_params=pltpu.CompilerParams(
            dimension_semantics=("arbitrary", "arbitrary"), vmem_limit_bytes=VMEM_LIMIT),
        name="in_proj",
    )(x, *params)


def _block_diag(y, bd_mask):
    return jnp.where(bd_mask, jnp.concatenate([y] * (GROUP // CHUNK), axis=0), 0.0)


def _each(fn, *lists):
    return [fn(*args) for args in zip(*lists)]


def _unit_lower_inverse(a_strict, t_idx, s_idx, bd):
    bdmm = lambda xs, ys: _each(lambda x, y: _mm(x, bd(y)), xs, ys)
    eye = (t_idx == s_idx).astype(F32)
    same8 = (t_idx // 8) == (s_idx // 8)
    a8 = _each(lambda a: jnp.where(same8, a, 0.0), a_strict)
    a8_2 = bdmm(a8, a8)
    a8_34 = bdmm(_each(lambda a, b: jnp.concatenate([a, b], axis=0), a8, a8_2), a8_2)
    inv = _each(lambda a, b, c: eye + a + b + c[:CHUNK], a8, a8_2, a8_34)
    inv = _each(jnp.add, inv, bdmm(inv, _each(lambda c: c[CHUNK:], a8_34)))
    size = 16
    while size <= CHUNK:
        off = ((t_idx // size) == (s_idx // size)) & ((t_idx // (size // 2)) != (s_idx // (size // 2)))
        cross = bdmm(_each(lambda a: jnp.where(off, a, 0.0), a_strict), inv)
        inv = _each(jnp.add, inv, bdmm(inv, cross))
        size *= 2
    return inv


def _chunk_local(r, lw, k, v, kk, kka, tri, t_idx, s_idx, bd_mask):
    bd = lambda y: _block_diag(y, bd_mask)
    bdmm = lambda xs, ys: _each(lambda x, y: _mm(x, bd(y)), xs, ys)
    bdmm2 = lambda xs, ys, zs: _each(lambda x, y, z: _mm(x, jnp.concatenate([bd(y), bd(z)], axis=1)), xs, ys, zs)
    left = lambda xs: _each(lambda x: x[:, :GROUP], xs)
    right = lambda xs: _each(lambda x: x[:, GROUP:], xs)
    top = lambda xs: _each(lambda x: x[:CHUNK], xs)
    bottom = lambda xs: _each(lambda x: x[CHUNK:], xs)

    cum = _each(lambda x: _mm_exact_lhs(tri, x, 3), lw)
    cum_last = _each(lambda c: c[CHUNK - 1:CHUNK, :], cum)
    p_incl = _each(jnp.exp, cum)
    p_excl = _each(lambda c, x: jnp.exp(c - x), cum, lw)
    p_inv = _each(lambda c: jnp.exp(-c), cum)
    to_end = _each(lambda cl, c: jnp.exp(cl - c), cum_last, cum)
    a_t = _each(lambda x, p: -x * p, kk, p_excl)
    r_t = _each(jnp.multiply, r, p_incl)
    b_t = _each(jnp.multiply, kka, p_inv)
    k_t = _each(jnp.multiply, k, p_inv)
    b_end = _each(jnp.multiply, kka, to_end)
    k_end = _each(jnp.multiply, k, to_end)

    ar = _each(lambda a, b: jnp.concatenate([a, b], axis=0), a_t, r_t)
    row2 = lax.broadcasted_iota(jnp.int32, (2 * CHUNK, 2 * GROUP), 0)
    col2 = lax.broadcasted_iota(jnp.int32, (2 * CHUNK, 2 * GROUP), 1) % CHUNK
    causal = col2 < (row2 % CHUNK) + (row2 // CHUNK)
    scores = _each(lambda x, y, z: jnp.where(causal, _mm_nt(x, jnp.concatenate([bd(y), bd(z)], axis=0)), 0.0),
                   ar, b_t, k_t)
    a_ab = top(left(scores))
    a_rb = bottom(left(scores))

    inv = _unit_lower_inverse(a_ab, t_idx, s_idx, bd)
    kv = bdmm(right(scores), v)
    wu = bdmm2(inv, a_t, top(kv))
    ro = bdmm2(a_rb, left(wu), right(wu))
    r_hat = _each(jnp.add, r_t, left(ro))
    o_loc = _each(jnp.add, right(ro), bottom(kv))

    wu_b = _each(lambda x, b: _mm(x.T, b), wu, b_end)
    trans = _each(lambda x: jnp.where(bd_mask, x[:GROUP], 0.0), wu_b)
    d_bd = _each(lambda x, y, z: jnp.where(bd_mask, x[GROUP:] + _mm(y.T, z), 0.0), wu_b, v, k_end)
    d_ls = _each(lambda d: sum(d[h * CHUNK:(h + 1) * CHUNK] for h in range(1, GROUP // CHUNK)) + d[:CHUNK], d_bd)
    p_end = _each(jnp.exp, cum_last)
    return r_hat, o_loc, trans, d_ls, p_end


def _rwkv_kernel(r_ref, lw_ref, k_ref, v_ref, kk_ref, kka_ref, tri_ref, o_ref, state_ref, *, n_sub):
    @pl.when(pl.program_id(1) == 0)
    def _():
        state_ref[...] = jnp.zeros_like(state_ref)

    t_idx = lax.broadcasted_iota(jnp.int32, (CHUNK, GROUP), 0)
    s_idx = lax.broadcasted_iota(jnp.int32, (CHUNK, GROUP), 1) % CHUNK
    bd_mask = (lax.broadcasted_iota(jnp.int32, (GROUP, GROUP), 0) // CHUNK
               == lax.broadcasted_iota(jnp.int32, (GROUP, GROUP), 1) // CHUNK)
    n_group = D_RWKV // GROUP
    where = [(slice(c * CHUNK, (c + 1) * CHUNK), slice(g * GROUP, (g + 1) * GROUP))
             for c in range(n_sub) for g in range(n_group)]
    load = lambda ref: [ref[0, rows, lanes].astype(F32) for rows, lanes in where]
    r_hat, o_loc, trans, d_ls, p_end = _chunk_local(
        load(r_ref), load(lw_ref), load(k_ref), load(v_ref), load(kk_ref), load(kka_ref),
        tri_ref[...], t_idx, s_idx, bd_mask)

    state = [state_ref[g] for g in range(n_group)]
    for c in range(n_sub):
        chains = range(c * n_group, (c + 1) * n_group)
        out = [_mm_nt(r_hat[i], _block_diag(state[g], bd_mask)) + o_loc[i] for g, i in enumerate(chains)]
        for g, i in enumerate(chains):
            rows, lanes = where[i]
            o_ref[0, rows, lanes] = out[g]
        state = [state[g] * p_end[i] + _mm(state[g], trans[i]) + d_ls[i] for g, i in enumerate(chains)]
    for g in range(n_group):
        state_ref[g] = state[g]


def _rwkv_call(r, lw, k, v, kk, kka, tri, tb):
    bsz, seq, _ = r.shape
    spec = pl.BlockSpec((1, tb, D_RWKV), lambda b, t: (b, t, 0))
    return pl.pallas_call(
        functools.partial(_rwkv_kernel, n_sub=tb // CHUNK),
        grid=(bsz, seq // tb),
        in_specs=[spec] * 6 + [pl.BlockSpec(tri.shape, lambda b, t: (0, 0))],
        out_specs=spec,
        out_shape=jax.ShapeDtypeStruct((bsz, seq, D_RWKV), F32),
        scratch_shapes=[pltpu.VMEM((D_RWKV // GROUP, HEAD, GROUP), F32)],
        compiler_params=pltpu.CompilerParams(
            dimension_semantics=("arbitrary", "arbitrary"), vmem_limit_bytes=VMEM_LIMIT),
        name="rwkv_chunk",
    )(r, lw, k, v, kk, kka, tri)


def _route(logits):
    lane = lax.broadcasted_iota(jnp.int32, logits.shape, 1)
    neg = -jnp.inf
    big = ROUTER_LANES
    is_group = (lane >= N_EXPERTS) & (lane < N_EXPERTS + N_GROUPS)
    gl = jnp.where(is_group, logits, neg)
    ge = jnp.exp(gl - jnp.max(gl, axis=-1, keepdims=True))
    gprob = ge / jnp.sum(ge, axis=-1, keepdims=True)
    g_top = jnp.max(gprob, axis=-1, keepdims=True)
    g_idx = jnp.min(jnp.where(is_group & (gprob == g_top), lane - N_EXPERTS, big), axis=-1, keepdims=True)

    in_group = (lane < N_EXPERTS) & ((lane // EXPERTS_PER_GROUP) == g_idx)
    el = jnp.where(in_group, logits, neg)
    top1 = jnp.max(el, axis=-1, keepdims=True)
    idx1 = jnp.min(jnp.where(in_group & (el == top1), lane, big), axis=-1, keepdims=True)
    el2 = jnp.where(lane == idx1, neg, el)
    top2 = jnp.max(el2, axis=-1, keepdims=True)
    idx2 = jnp.min(jnp.where(in_group & (lane != idx1) & (el2 == top2), lane, big), axis=-1, keepdims=True)
    e2 = jnp.exp(top2 - top1)
    denom = 1.0 + e2
    return idx1, idx2, g_top * (1.0 / denom), g_top * (e2 / denom)


def _pack_bf16_pairs(x):
    bits = lax.bitcast_convert_type(_bf(x).astype(F32), jnp.uint32)
    k = x.shape[1] // 2
    return lax.bitcast_convert_type((bits[:, :k] >> 16) | bits[:, k:], jnp.int32)


def _unpack_bf16_pairs(words):
    bits = lax.bitcast_convert_type(words, jnp.uint32)
    lo = lax.bitcast_convert_type(bits << 16, F32)
    hi = lax.bitcast_convert_type(bits & jnp.uint32(0xFFFF0000), F32)
    return _bf(jnp.concatenate([lo, hi], axis=1))


LANE_IDX1, LANE_IDX2, LANE_RANK1, LANE_RANK2, LANE_GATE1, LANE_GATE2 = range(6)


def _out_kernel(o_ref, bonus_ref, gate_ref, yconv_ref, x_ref, ones_ref, lnw_ref, lnb_ref,
                wout_ref, gffn_ref, rw_ref, rb_ref, tri_ref, h_ref, u_ref, route_ref, route_t_ref, count_ref,
                seen_ref, *, n_split):
    @pl.when(pl.program_id(0) == 0)
    def _():
        seen_ref[...] = jnp.zeros_like(seen_ref)

    sub = o_ref.shape[0] // n_split
    parts = [slice(s * sub, (s + 1) * sub) for s in range(n_split)]
    read = lambda ref: [ref[p, :] for p in parts]
    ones_quad = ones_ref[...]
    inv_n = 1.0 / HEAD
    o = read(o_ref)
    mean = _each(lambda v: _head_sum(v, ones_quad) * inv_n, o)
    cen = _each(jnp.subtract, o, mean)
    var = _each(lambda c: _head_sum(c * c, ones_quad) * inv_n, cen)
    on = _each(lambda c, v: c * lax.rsqrt(v + LN_X_EPS) * lnw_ref[...] + lnb_ref[...], cen, var)
    y_rwkv = _each(lambda a, b, g: (a + b) * g, on, read(bonus_ref), read(gate_ref))
    mixed = _each(lambda yc, yr: (jnp.dot(yc, wout_ref[:D_CONV, :], preferred_element_type=F32)
                                  + jnp.dot(_bf(yr), wout_ref[D_CONV:, :], preferred_element_type=F32)),
                  read(yconv_ref), y_rwkv)
    h = _each(jnp.add, read(x_ref), mixed)
    u = _each(lambda v: _rms_norm(v, gffn_ref[...]), h)
    u_hi = _each(_bf, u)
    u_lo = _each(lambda a, b: _bf(a - b.astype(F32)), u, u_hi)
    by_hi = _each(lambda a: jnp.dot(a, rw_ref[...], preferred_element_type=F32), u_hi)
    by_lo = _each(lambda a: jnp.dot(a, rw_ref[:, :ROUTER_LANES], preferred_element_type=F32), u_lo)
    logits = _each(lambda a, b: a[:, :ROUTER_LANES] + a[:, ROUTER_LANES:] + b + rb_ref[...], by_hi, by_lo)
    routed = _each(_route, logits)

    lane = lax.broadcasted_iota(jnp.int32, (sub, ROUTER_LANES), 1)
    hit1 = _each(lambda rt: lane == rt[0], routed)
    hit2 = _each(lambda rt: lane == rt[1], routed)
    both = _each(lambda a, b: a.astype(F32) + b.astype(F32), hit1, hit2)
    inside = _each(lambda b: jnp.dot(tri_ref[...], _bf(b), preferred_element_type=F32), both)
    seen = seen_ref[...]
    for s, p in enumerate(parts):
        idx1, idx2, gate1, gate2 = routed[s]
        before = inside[s] + seen
        rank1 = jnp.sum(jnp.where(hit1[s], before, 0.0), axis=-1, keepdims=True)
        rank2 = jnp.sum(jnp.where(hit2[s], before, 0.0), axis=-1, keepdims=True)
        seen = seen + jnp.sum(both[s], axis=0, keepdims=True)
        route = jnp.zeros((sub, ROUTER_LANES), F32)
        for lane_id, col in ((LANE_IDX1, idx1.astype(F32)), (LANE_IDX2, idx2.astype(F32)), (LANE_RANK1, rank1),
                             (LANE_RANK2, rank2), (LANE_GATE1, gate1), (LANE_GATE2, gate2)):
            route = jnp.where(lane == lane_id, col, route)
        route_ref[p, :] = route
        route_t_ref[:, p] = route.T[:8, :]
        h_ref[p, :] = h[s]
        u_ref[p, :] = _pack_bf16_pairs(u[s])
    seen_ref[...] = seen
    count_ref[...] = jnp.broadcast_to(seen, count_ref.shape)


def _out_call(o, bonus, gate, yconv, x, ones_bd, ln_w, ln_b, w_out, g_ffn, router_w, router_b, tm, n_split):
    n_tok = x.shape[0]
    row = lambda width: pl.BlockSpec((tm, width), lambda i: (i, 0))
    full = lambda arr: pl.BlockSpec(arr.shape, lambda i: (0,) * arr.ndim, pipeline_mode=pl.Buffered(1))
    sub = tm // n_split
    tri_strict = (jnp.arange(sub)[:, None] > jnp.arange(sub)[None, :]).astype(BF16)
    params = (ones_bd, ln_w, ln_b, w_out, g_ffn, router_w, router_b, tri_strict)
    return pl.pallas_call(
        functools.partial(_out_kernel, n_split=n_split),
        grid=(n_tok // tm,),
        in_specs=[row(D_RWKV)] * 4 + [row(D_MODEL)] + [full(p) for p in params],
        out_specs=[row(D_MODEL), row(D_MODEL // 2), row(ROUTER_LANES),
                   pl.BlockSpec((8, tm), lambda i: (0, i)),
                   pl.BlockSpec((8, ROUTER_LANES), lambda i: (0, 0))],
        out_shape=[jax.ShapeDtypeStruct((n_tok, D_MODEL), F32),
                   jax.ShapeDtypeStruct((n_tok, D_MODEL // 2), jnp.int32),
                   jax.ShapeDtypeStruct((n_tok, ROUTER_LANES), F32),
                   jax.ShapeDtypeStruct((8, n_tok), F32),
                   jax.ShapeDtypeStruct((8, ROUTER_LANES), F32)],
        scratch_shapes=[pltpu.VMEM((1, ROUTER_LANES), F32)],
        compiler_params=pltpu.CompilerParams(
            dimension_semantics=("arbitrary",), vmem_limit_bytes=VMEM_LIMIT),
        name="out_proj_route",
    )(o, bonus, gate, yconv, x, *params)


SC_CORES = 2
SC_SUBCORES = 16
SC_ROWS = 64


def _sc_gather(table, idx):
    n_rows = idx.shape[0]
    width = table.shape[1]
    n_chunks = n_rows // (SC_CORES * SC_SUBCORES * SC_ROWS)
    mesh = plsc.VectorSubcoreMesh(core_axis_name="c", subcore_axis_name="s",
                                  num_cores=SC_CORES, num_subcores=SC_SUBCORES)

    def body(table_hbm, idx_hbm, out_hbm, idx_v, rows_v, sem):
        worker = lax.axis_index("s") * SC_CORES + lax.axis_index("c")

        @pl.loop(0, n_chunks)
        def _(j):
            chunk = worker * n_chunks + j
            pltpu.sync_copy(idx_hbm.at[chunk], idx_v)
            pltpu.async_copy(table_hbm.at[idx_v], rows_v, sem).wait()
            pltpu.sync_copy(rows_v, out_hbm.at[pl.ds(pl.multiple_of(chunk * SC_ROWS, SC_ROWS), SC_ROWS)])

    return pl.kernel(
        body,
        out_type=jax.ShapeDtypeStruct((n_rows, width), table.dtype),
        mesh=mesh,
        scratch_types=[pltpu.VMEM((SC_ROWS,), jnp.int32), pltpu.VMEM((SC_ROWS, width), table.dtype),
                       pltpu.SemaphoreType.DMA],
        name="sc_row_gather",
    )(table, idx.reshape(n_rows // SC_ROWS, SC_ROWS))


def _sc_scatter(rows, pos, n_out):
    n_rows, width = rows.shape
    n_slots = pos.shape[0] // n_rows
    slot_chunks = n_rows // SC_ROWS
    n_chunks = slot_chunks // (SC_CORES * SC_SUBCORES)
    mesh = plsc.VectorSubcoreMesh(core_axis_name="c", subcore_axis_name="s",
                                  num_cores=SC_CORES, num_subcores=SC_SUBCORES)

    def body(rows_hbm, pos_hbm, out_hbm, idx_v, rows_v, sem):
        worker = lax.axis_index("s") * SC_CORES + lax.axis_index("c")

        @pl.loop(0, n_chunks)
        def _(j):
            chunk = worker * n_chunks + j
            pltpu.sync_copy(rows_hbm.at[pl.ds(pl.multiple_of(chunk * SC_ROWS, SC_ROWS), SC_ROWS)], rows_v)
            for s in range(n_slots):
                pltpu.sync_copy(pos_hbm.at[s * slot_chunks + chunk], idx_v.at[s])
            copies = [pltpu.async_copy(rows_v, out_hbm.at[idx_v.at[s]], sem) for s in range(n_slots)]
            for copy in copies:
                copy.wait()

    return pl.kernel(
        body,
        out_type=jax.ShapeDtypeStruct((n_out, width), rows.dtype),
        mesh=mesh,
        scratch_types=[pltpu.VMEM((n_slots, SC_ROWS), jnp.int32), pltpu.VMEM((SC_ROWS, width), rows.dtype),
                       pltpu.SemaphoreType.DMA],
        name="sc_row_scatter",
    )(rows, pos.reshape(n_slots * slot_chunks, SC_ROWS))


def _expert_kernel(tile_expert_ref, n_valid_ref, x_ref, wg_ref, wu_ref, wd_ref, y_ref):
    del tile_expert_ref

    @pl.when(pl.program_id(0) < n_valid_ref[0])
    def _():
        x = _unpack_bf16_pairs(x_ref[...])
        gate = jnp.dot(x, wg_ref[0], preferred_element_type=F32)
        up = jnp.dot(x, wu_ref[0], preferred_element_type=F32)
        hid = gate * jax.nn.sigmoid(gate) * up
        y_ref[...] = _pack_bf16_pairs(jnp.dot(_bf(hid), wd_ref[0], preferred_element_type=F32))


def _expert_call(tile_expert, n_valid, x_sorted, w_gate, w_up, w_down, tm):
    n_rows = x_sorted.shape[0]
    rows = pl.BlockSpec((tm, D_MODEL // 2), lambda i, te, nv: (jnp.minimum(i, nv[0] - 1), 0))
    return pl.pallas_call(
        _expert_kernel,
        grid_spec=pltpu.PrefetchScalarGridSpec(
            num_scalar_prefetch=2,
            grid=(n_rows // tm,),
            in_specs=[rows,
                      pl.BlockSpec((1, D_MODEL, D_EXPERT), lambda i, te, nv: (te[i], 0, 0)),
                      pl.BlockSpec((1, D_MODEL, D_EXPERT), lambda i, te, nv: (te[i], 0, 0)),
                      pl.BlockSpec((1, D_EXPERT, D_MODEL), lambda i, te, nv: (te[i], 0, 0))],
            out_specs=rows),
        out_shape=jax.ShapeDtypeStruct((n_rows, D_MODEL // 2), jnp.int32),
        compiler_params=pltpu.CompilerParams(
            dimension_semantics=("arbitrary",), vmem_limit_bytes=VMEM_LIMIT),
        name="moe_experts",
    )(tile_expert, n_valid, x_sorted, w_gate, w_up, w_down)


def _final_kernel(h_ref, y1_ref, y2_ref, route_ref, gfin_ref, out_ref):
    route = route_ref[...]
    gate1 = route[:, LANE_GATE1:LANE_GATE1 + 1]
    gate2 = route[:, LANE_GATE2:LANE_GATE2 + 1]
    moe = gate1 * _unpack_bf16_pairs(y1_ref[...]).astype(F32) + gate2 * _unpack_bf16_pairs(y2_ref[...]).astype(F32)
    out_ref[...] = _rms_norm(h_ref[...] + moe, gfin_ref[...])


def _final_call(h, y_pairs, route, g_final, tm):
    n_tok = h.shape[0]
    n_blocks = n_tok // tm
    return pl.pallas_call(
        _final_kernel,
        grid=(n_blocks,),
        in_specs=[pl.BlockSpec((tm, D_MODEL), lambda i: (i, 0)),
                  pl.BlockSpec((tm, D_MODEL // 2), lambda i: (i, 0)),
                  pl.BlockSpec((tm, D_MODEL // 2), lambda i: (i + n_blocks, 0)),
                  pl.BlockSpec((tm, ROUTER_LANES), lambda i: (i, 0)),
                  pl.BlockSpec((1, D_MODEL), lambda i: (0, 0))],
        out_specs=pl.BlockSpec((tm, D_MODEL), lambda i: (i, 0)),
        out_shape=jax.ShapeDtypeStruct((n_tok, D_MODEL), F32),
        compiler_params=pltpu.CompilerParams(
            dimension_semantics=("arbitrary",), vmem_limit_bytes=VMEM_LIMIT),
        name="moe_combine_norm",
    )(h, y_pairs, y_pairs, route, g_final)


def _dispatch_plan(route_t, counts, tm):
    n_tok = route_t.shape[1]
    n_tiles = (2 * n_tok) // tm + N_EXPERTS
    counts = counts[0, :N_EXPERTS].astype(jnp.int32)
    tiles_per = (counts + tm - 1) // tm
    tile_end = jnp.cumsum(tiles_per)
    row_start = (tile_end - tiles_per) * tm
    experts = jnp.arange(N_EXPERTS, dtype=jnp.int32)

    def position(idx_lane, rank_lane):
        idx = route_t[idx_lane].astype(jnp.int32)
        start = jnp.sum(jnp.where(idx[None, :] == experts[:, None], row_start[:, None], 0), axis=0)
        return start + route_t[rank_lane].astype(jnp.int32)

    pos = jnp.concatenate([position(LANE_IDX1, LANE_RANK1), position(LANE_IDX2, LANE_RANK2)])
    n_valid = tile_end[-1:]
    tile = jnp.minimum(jnp.arange(n_tiles, dtype=jnp.int32), n_valid - 1)
    tile_expert = jnp.sum((tile_end[None, :] <= tile[:, None]).astype(jnp.int32), axis=1)
    return pos, n_tiles, tile_expert, n_valid


def _block(x, norm_mix_g, w_in, rwkv_mu, conv_w, decay_up, decay_base, aaa_up, aaa_base, gate_up,
           k_k, k_a, r_k, ln_x_w, ln_x_b, w_out, norm_ffn_g, router_group_w, router_group_b,
           router_expert_w, router_expert_b, expert_w_gate, expert_w_up, expert_w_down, norm_final_g,
           *, tm_in, tb_rwkv, tm_out, tm_expert, tm_final):
    bsz, seq, d_model = x.shape
    n_tok = bsz * seq
    row = lambda p: p.reshape(1, -1)
    half = LORA_WA // 2
    zeros = jnp.zeros((half, D_RWKV), F32)
    dup = _bf(jnp.concatenate([decay_up, zeros], axis=0))
    aup = _bf(jnp.concatenate([zeros, aaa_up], axis=0))
    head_of = jnp.arange(QUAD) // HEAD
    ones_quad = (head_of[:, None] == head_of[None, :]).astype(BF16)
    tri = (jnp.arange(CHUNK)[:, None] >= jnp.arange(CHUNK)[None, :]).astype(BF16)

    yconv, r, lw, k2, v, kk, kka, gate, bonus = _in_call(
        x, row(norm_mix_g), _bf(w_in), row(rwkv_mu), conv_w, dup, row(decay_base), aup, row(aaa_base),
        _bf(gate_up), row(k_k), row(k_a), row(r_k), ones_quad, tm_in, IN_SPLIT)
    o = _rwkv_call(r, lw, k2, v, kk, kka, tri, tb_rwkv)

    pad = ROUTER_LANES - N_EXPERTS - N_GROUPS
    router_w = jnp.concatenate([router_expert_w, router_group_w, jnp.zeros((d_model, pad), F32)], axis=1)
    router_hi = _bf(router_w)
    router_split = jnp.concatenate([router_hi, _bf(router_w - router_hi.astype(F32))], axis=1)
    router_b = jnp.concatenate([router_expert_b, router_group_b, jnp.zeros((pad,), F32)]).reshape(1, -1)
    flat = lambda t: t.reshape(n_tok, t.shape[-1])
    h, u_pairs, route, route_t, counts = _out_call(
        flat(o), flat(bonus), flat(gate), flat(yconv), flat(x), ones_quad,
        row(ln_x_w), row(ln_x_b), _bf(w_out), row(norm_ffn_g), router_split, router_b, tm_out, OUT_SPLIT)

    pos, n_tiles, tile_expert, n_valid = _dispatch_plan(route_t, counts, tm_expert)
    x_sorted = _sc_scatter(u_pairs, pos, n_tiles * tm_expert)
    y_sorted = _expert_call(tile_expert, n_valid, x_sorted, _bf(expert_w_gate), _bf(expert_w_up),
                            _bf(expert_w_down), tm_expert)
    y_pairs = _sc_gather(y_sorted, pos)
    out = _final_call(h, y_pairs, route, row(norm_final_g), tm_final)
    return out.reshape(bsz, seq, d_model)


def kernel(x, norm_mix_g, w_in, rwkv_mu, conv_w, decay_up, decay_base, aaa_up, aaa_base, gate_up, k_k, k_a, r_k, ln_x_w, ln_x_b, w_out, norm_ffn_g, router_group_w, router_group_b, router_expert_w, router_expert_b, expert_w_gate, expert_w_up, expert_w_down, norm_final_g):
    return _block(x, norm_mix_g[0], w_in[0], rwkv_mu[0], conv_w[0], decay_up[0], decay_base[0],
                  aaa_up[0], aaa_base[0], gate_up[0], k_k[0], k_a[0], r_k[0].reshape(-1), ln_x_w[0],
                  ln_x_b[0], w_out[0], norm_ffn_g[0], router_group_w[0], router_group_b[0],
                  router_expert_w[0], router_expert_b[0], expert_w_gate[0], expert_w_up[0],
                  expert_w_down[0], norm_final_g,
                  tm_in=512, tb_rwkv=256, tm_out=512, tm_expert=512, tm_final=512)
```

```python
import functools

import jax
import jax.numpy as jnp
from jax import lax
from jax.experimental import pallas as pl
from jax.experimental.pallas import tpu as pltpu
from jax.experimental.pallas import tpu_sc as plsc

F32 = jnp.float32
BF16 = jnp.bfloat16

D_MODEL = 1024
D_CONV = 512
CONV_WIDTH = 3
N_HEADS = 8
HEAD = 64
D_RWKV = N_HEADS * HEAD
LORA_WA = 128
GATE_LORA = 128
D_RWKV_PROJ = 3 * D_RWKV + LORA_WA + GATE_LORA
D_IN = 3 * D_CONV + D_RWKV_PROJ
N_GROUPS = 4
EXPERTS_PER_GROUP = 8
N_EXPERTS = N_GROUPS * EXPERTS_PER_GROUP
D_EXPERT = D_MODEL // 4
RMS_EPS = 1e-6
LN_X_EPS = 64e-5
L2_EPS = 1e-12

LANES = 128
CHUNK = 64
QUAD = 4 * HEAD
GROUP = 2 * HEAD
IN_SPLIT = 4
OUT_SPLIT = 4
COMBINE_CHUNKS = 2
ROUTER_LANES = 128

VMEM_LIMIT = 56 * 1024 * 1024


def _bf(x):
    return x.astype(BF16)


def _mm(a, b):
    return jnp.dot(_bf(a), _bf(b), preferred_element_type=F32)


def _mm_nt(a, b):
    return lax.dot_general(_bf(a), _bf(b), (((1,), (1,)), ((), ())), preferred_element_type=F32)


def _mm_exact_lhs(lhs_bf16, x, passes):
    acc = None
    rem = x
    for _ in range(passes):
        piece = _bf(rem)
        part = jnp.dot(lhs_bf16, piece, preferred_element_type=F32)
        acc = part if acc is None else acc + part
        rem = rem - piece.astype(F32)
    return acc


def _head_sum(x, ones_quad):
    xb = _bf(x)
    return jnp.concatenate(
        [jnp.dot(xb[:, q * QUAD:(q + 1) * QUAD], ones_quad, preferred_element_type=F32)
         for q in range(x.shape[1] // QUAD)], axis=1)


def _rms_norm(x, g):
    return x * lax.rsqrt(jnp.mean(x * x, axis=-1, keepdims=True) + RMS_EPS) * g


def _shift_rows(cur, prev_rows, k):
    rolled = pltpu.roll(cur, k, 0)
    prev_rolled = pltpu.roll(prev_rows, k, 0)
    n = cur.shape[0]
    head = jnp.concatenate([prev_rolled, rolled[8:]], axis=0) if n > 8 else prev_rolled
    row = lax.broadcasted_iota(jnp.int32, cur.shape, 0)
    return jnp.where(row < k, head, rolled)


def _in_kernel(x_ref, g_ref, w_ref, mu_ref, convw_ref, dup_ref, dbase_ref, aup_ref, abase_ref,
               gup_ref, kk_ref, ka_ref, rk_ref, ones_ref,
               yconv_ref, r_ref, lw_ref, k_ref, v_ref, kkn_ref, kka_ref, gate_ref, bonus_ref,
               carry_ref, wbf_ref, *, n_split):
    @pl.when((pl.program_id(0) == 0) & (pl.program_id(1) == 0))
    def _():
        wbf_ref[...] = _bf(w_ref[...])

    @pl.when(pl.program_id(1) == 0)
    def _():
        carry_ref[...] = jnp.zeros_like(carry_ref)

    sub = x_ref.shape[1] // n_split
    z_parts = []
    for s in range(n_split):
        u = _rms_norm(x_ref[0, s * sub:(s + 1) * sub, :], g_ref[...])
        z_parts.append(jnp.dot(_bf(u), wbf_ref[...], preferred_element_type=F32))
    prev_ch = carry_ref[:, :D_CONV]
    prev_zr = carry_ref[:, D_CONV:]
    outs = (yconv_ref, r_ref, lw_ref, k_ref, v_ref, kkn_ref, kka_ref, gate_ref, bonus_ref)
    for s, z in enumerate(z_parts):
        rows = slice(s * sub, (s + 1) * sub)
        prev_ch, prev_zr = _in_post(z, prev_ch, prev_zr, mu_ref, convw_ref, dup_ref, dbase_ref, aup_ref,
                                    abase_ref, gup_ref, kk_ref, ka_ref, rk_ref, ones_ref,
                                    [ref.at[0, rows, :] for ref in outs])
    carry_ref[:, :D_CONV] = prev_ch
    carry_ref[:, D_CONV:] = prev_zr


def _in_post(z, prev_ch, prev_zr, mu_ref, convw_ref, dup_ref, dbase_ref, aup_ref, abase_ref,
             gup_ref, kk_ref, ka_ref, rk_ref, ones_ref, outs):
    yconv_ref, r_ref, lw_ref, k_ref, v_ref, kkn_ref, kka_ref, gate_ref, bonus_ref = outs
    b_gate = z[:, :D_CONV]
    ch = z[:, D_CONV:2 * D_CONV] * z[:, 2 * D_CONV:3 * D_CONV]
    conv = (convw_ref[2:3, :] * ch
            + convw_ref[1:2, :] * _shift_rows(ch, prev_ch, 1)
            + convw_ref[0:1, :] * _shift_rows(ch, prev_ch, 2))
    yconv_ref[...] = _bf(b_gate * conv)

    zr = z[:, 3 * D_CONV:]
    zs = zr + (_shift_rows(zr, prev_zr, 1) - zr) * mu_ref[...]

    r = zs[:, :D_RWKV]
    k = zs[:, D_RWKV:2 * D_RWKV]
    v = zs[:, 2 * D_RWKV:3 * D_RWKV]
    wa_lo = zs[:, 3 * D_RWKV:3 * D_RWKV + LORA_WA]
    g_lo = zs[:, 3 * D_RWKV + LORA_WA:]

    dec_in = -(dbase_ref[...] + jnp.dot(_bf(jnp.tanh(wa_lo)), dup_ref[...], preferred_element_type=F32))
    softplus = jnp.maximum(dec_in, 0.0) + jnp.log(1.0 + jnp.exp(-jnp.abs(dec_in)))
    w = -softplus - 0.5
    lw_ref[...] = -jnp.exp(w)
    a = jax.nn.sigmoid(abase_ref[...] + jnp.dot(_bf(wa_lo), aup_ref[...], preferred_element_type=F32))
    gate_ref[...] = jnp.dot(_bf(jax.nn.sigmoid(g_lo)), gup_ref[...], preferred_element_type=F32)

    ones_quad = ones_ref[...]
    kk = k * kk_ref[...]
    norm = jnp.sqrt(_head_sum(kk * kk, ones_quad))
    kk = kk / jnp.maximum(norm, L2_EPS)
    k2 = k * (1.0 + (a - 1.0) * ka_ref[...])
    r_ref[...] = _bf(r)
    k_ref[...] = _bf(k2)
    v_ref[...] = _bf(v)
    kkn_ref[...] = _bf(kk)
    kka_ref[...] = _bf(kk * a)
    bonus_ref[...] = _head_sum(r * k2 * rk_ref[...], ones_quad) * v
    return ch[-8:], zr[-8:]


def _in_call(x, g, w_in, mu, conv_w, dup, dbase, aup, abase, gup, k_k, k_a, r_k, ones_quad, tm, n_split):
    bsz, seq, _ = x.shape
    full = lambda arr: pl.BlockSpec(arr.shape, lambda b, t: (0,) * arr.ndim, pipeline_mode=pl.Buffered(1))
    out_spec = pl.BlockSpec((1, tm, D_RWKV), lambda b, t: (b, t, 0))
    out_dtypes = (BF16, BF16, F32, BF16, BF16, BF16, BF16, F32, F32)
    params = (g, w_in, mu, conv_w, dup, dbase, aup, abase, gup, k_k, k_a, r_k, ones_quad)
    return pl.pallas_call(
        functools.partial(_in_kernel, n_split=n_split),
        grid=(bsz, seq // tm),
        in_specs=[pl.BlockSpec((1, tm, D_MODEL), lambda b, t: (b, t, 0))] + [full(p) for p in params],
        out_specs=[out_spec] * 9,
        out_shape=[jax.ShapeDtypeStruct((bsz, seq, D_RWKV), dt) for dt in out_dtypes],
        scratch_shapes=[pltpu.VMEM((8, D_CONV + D_RWKV_PROJ), F32), pltpu.VMEM(w_in.shape, BF16)],
        compiler_params=pltpu.CompilerParams(
            dimension_semantics=("arbitrary", "arbitrary"), vmem_limit_bytes=VMEM_LIMIT),
        name="in_proj",
    )(x, *params)


def _block_diag(y, bd_mask):
    return jnp.where(bd_mask, jnp.concatenate([y] * (GROUP // CHUNK), axis=0), 0.0)


def _each(fn, *lists):
    return [fn(*args) for args in zip(*lists)]


def _unit_lower_inverse(a_strict, t_idx, s_idx, bd):
    bdmm = lambda xs, ys: _each(lambda x, y: _mm(x, bd(y)), xs, ys)
    eye = (t_idx == s_idx).astype(F32)
    same8 = (t_idx // 8) == (s_idx // 8)
    a8 = _each(lambda a: jnp.where(same8, a, 0.0), a_strict)
    a8_2 = bdmm(a8, a8)
    a8_34 = bdmm(_each(lambda a, b: jnp.concatenate([a, b], axis=0), a8, a8_2), a8_2)
    inv = _each(lambda a, b, c: eye + a + b + c[:CHUNK], a8, a8_2, a8_34)
    inv = _each(jnp.add, inv, bdmm(inv, _each(lambda c: c[CHUNK:], a8_34)))
    size = 16
    while size <= CHUNK:
        off = ((t_idx // size) == (s_idx // size)) & ((t_idx // (size // 2)) != (s_idx // (size // 2)))
        cross = bdmm(_each(lambda a: jnp.where(off, a, 0.0), a_strict), inv)
        inv = _each(jnp.add, inv, bdmm(inv, cross))
        size *= 2
    return inv


def _chunk_local(r, lw, k, v, kk, kka, tri, t_idx, s_idx, bd_mask):
    bd = lambda y: _block_diag(y, bd_mask)
    bdmm = lambda xs, ys: _each(lambda x, y: _mm(x, bd(y)), xs, ys)
    bdmm2 = lambda xs, ys, zs: _each(lambda x, y, z: _mm(x, jnp.concatenate([bd(y), bd(z)], axis=1)), xs, ys, zs)
    left = lambda xs: _each(lambda x: x[:, :GROUP], xs)
    right = lambda xs: _each(lambda x: x[:, GROUP:], xs)
    top = lambda xs: _each(lambda x: x[:CHUNK], xs)
    bottom = lambda xs: _each(lambda x: x[CHUNK:], xs)

    cum = _each(lambda x: _mm_exact_lhs(tri, x, 3), lw)
    cum_last = _each(lambda c: c[CHUNK - 1:CHUNK, :], cum)
    p_incl = _each(jnp.exp, cum)
    p_excl = _each(lambda c, x: jnp.exp(c - x), cum, lw)
    p_inv = _each(lambda c: jnp.exp(-c), cum)
    to_end = _each(lambda cl, c: jnp.exp(cl - c), cum_last, cum)
    a_t = _each(lambda x, p: -x * p, kk, p_excl)
    r_t = _each(jnp.multiply, r, p_incl)
    b_t = _each(jnp.multiply, kka, p_inv)
    k_t = _each(jnp.multiply, k, p_inv)
    b_end = _each(jnp.multiply, kka, to_end)
    k_end = _each(jnp.multiply, k, to_end)

    ar = _each(lambda a, b: jnp.concatenate([a, b], axis=0), a_t, r_t)
    row2 = lax.broadcasted_iota(jnp.int32, (2 * CHUNK, 2 * GROUP), 0)
    col2 = lax.broadcasted_iota(jnp.int32, (2 * CHUNK, 2 * GROUP), 1) % CHUNK
    causal = col2 < (row2 % CHUNK) + (row2 // CHUNK)
    scores = _each(lambda x, y, z: jnp.where(causal, _mm_nt(x, jnp.concatenate([bd(y), bd(z)], axis=0)), 0.0),
                   ar, b_t, k_t)
    a_ab = top(left(scores))
    a_rb = bottom(left(scores))

    inv = _unit_lower_inverse(a_ab, t_idx, s_idx, bd)
    kv = bdmm(right(scores), v)
    wu = bdmm2(inv, a_t, top(kv))
    ro = bdmm2(a_rb, left(wu), right(wu))
    r_hat = _each(jnp.add, r_t, left(ro))
    o_loc = _each(jnp.add, right(ro), bottom(kv))

    wu_b = _each(lambda x, b: _mm(x.T, b), wu, b_end)
    trans = _each(lambda x: jnp.where(bd_mask, x[:GROUP], 0.0), wu_b)
    d_bd = _each(lambda x, y, z: jnp.where(bd_mask, x[GROUP:] + _mm(y.T, z), 0.0), wu_b, v, k_end)
    d_ls = _each(lambda d: sum(d[h * CHUNK:(h + 1) * CHUNK] for h in range(1, GROUP // CHUNK)) + d[:CHUNK], d_bd)
    p_end = _each(jnp.exp, cum_last)
    return r_hat, o_loc, trans, d_ls, p_end


def _rwkv_kernel(r_ref, lw_ref, k_ref, v_ref, kk_ref, kka_ref, tri_ref, o_ref, state_ref, *, n_sub):
    @pl.when(pl.program_id(1) == 0)
    def _():
        state_ref[...] = jnp.zeros_like(state_ref)

    t_idx = lax.broadcasted_iota(jnp.int32, (CHUNK, GROUP), 0)
    s_idx = lax.broadcasted_iota(jnp.int32, (CHUNK, GROUP), 1) % CHUNK
    bd_mask = (lax.broadcasted_iota(jnp.int32, (GROUP, GROUP), 0) // CHUNK
               == lax.broadcasted_iota(jnp.int32, (GROUP, GROUP), 1) // CHUNK)
    n_group = D_RWKV // GROUP
    where = [(slice(c * CHUNK, (c + 1) * CHUNK), slice(g * GROUP, (g + 1) * GROUP))
             for c in range(n_sub) for g in range(n_group)]
    load = lambda ref: [ref[0, rows, lanes].astype(F32) for rows, lanes in where]
    r_hat, o_loc, trans, d_ls, p_end = _chunk_local(
        load(r_ref), load(lw_ref), load(k_ref), load(v_ref), load(kk_ref), load(kka_ref),
        tri_ref[...], t_idx, s_idx, bd_mask)

    state = [state_ref[g] for g in range(n_group)]
    for c in range(n_sub):
        chains = range(c * n_group, (c + 1) * n_group)
        out = [_mm_nt(r_hat[i], _block_diag(state[g], bd_mask)) + o_loc[i] for g, i in enumerate(chains)]
        for g, i in enumerate(chains):
            rows, lanes = where[i]
            o_ref[0, rows, lanes] = out[g]
        state = [state[g] * p_end[i] + _mm(state[g], trans[i]) + d_ls[i] for g, i in enumerate(chains)]
    for g in range(n_group):
        state_ref[g] = state[g]


def _rwkv_call(r, lw, k, v, kk, kka, tri, tb):
    bsz, seq, _ = r.shape
    spec = pl.BlockSpec((1, tb, D_RWKV), lambda b, t: (b, t, 0))
    return pl.pallas_call(
        functools.partial(_rwkv_kernel, n_sub=tb // CHUNK),
        grid=(bsz, seq // tb),
        in_specs=[spec] * 6 + [pl.BlockSpec(tri.shape, lambda b, t: (0, 0))],
        out_specs=spec,
        out_shape=jax.ShapeDtypeStruct((bsz, seq, D_RWKV), F32),
        scratch_shapes=[pltpu.VMEM((D_RWKV // GROUP, HEAD, GROUP), F32)],
        compiler_params=pltpu.CompilerParams(
            dimension_semantics=("arbitrary", "arbitrary"), vmem_limit_bytes=VMEM_LIMIT),
        name="rwkv_chunk",
    )(r, lw, k, v, kk, kka, tri)


def _route(logits):
    lane = lax.broadcasted_iota(jnp.int32, logits.shape, 1)
    neg = -jnp.inf
    big = ROUTER_LANES
    is_group = (lane >= N_EXPERTS) & (lane < N_EXPERTS + N_GROUPS)
    gl = jnp.where(is_group, logits, neg)
    ge = jnp.exp(gl - jnp.max(gl, axis=-1, keepdims=True))
    gprob = ge / jnp.sum(ge, axis=-1, keepdims=True)
    g_top = jnp.max(gprob, axis=-1, keepdims=True)
    g_idx = jnp.min(jnp.where(is_group & (gprob == g_top), lane - N_EXPERTS, big), axis=-1, keepdims=True)

    in_group = (lane < N_EXPERTS) & ((lane // EXPERTS_PER_GROUP) == g_idx)
    el = jnp.where(in_group, logits, neg)
    top1 = jnp.max(el, axis=-1, keepdims=True)
    idx1 = jnp.min(jnp.where(in_group & (el == top1), lane, big), axis=-1, keepdims=True)
    el2 = jnp.where(lane == idx1, neg, el)
    top2 = jnp.max(el2, axis=-1, keepdims=True)
    idx2 = jnp.min(jnp.where(in_group & (lane != idx1) & (el2 == top2), lane, big), axis=-1, keepdims=True)
    e2 = jnp.exp(top2 - top1)
    denom = 1.0 + e2
    return idx1, idx2, g_top * (1.0 / denom), g_top * (e2 / denom)


def _pack_bf16_pairs(x):
    bits = lax.bitcast_convert_type(_bf(x).astype(F32), jnp.uint32)
    k = x.shape[1] // 2
    return lax.bitcast_convert_type((bits[:, :k] >> 16) | bits[:, k:], jnp.int32)


def _unpack_bf16_pairs(words):
    bits = lax.bitcast_convert_type(words, jnp.uint32)
    lo = lax.bitcast_convert_type(bits << 16, F32)
    hi = lax.bitcast_convert_type(bits & jnp.uint32(0xFFFF0000), F32)
    return _bf(jnp.concatenate([lo, hi], axis=1))


LANE_IDX1, LANE_IDX2, LANE_RANK1, LANE_RANK2, LANE_GATE1, LANE_GATE2 = range(6)


def _out_kernel(o_ref, bonus_ref, gate_ref, yconv_ref, x_ref, ones_ref, lnw_ref, lnb_ref,
                wout_ref, gffn_ref, rw_ref, rb_ref, tri_ref, h_ref, u_ref, route_ref, route_t_ref, count_ref,
                seen_ref, wbf_ref, *, n_split):
    @pl.when(pl.program_id(0) == 0)
    def _():
        seen_ref[...] = jnp.zeros_like(seen_ref)
        wbf_ref[...] = _bf(wout_ref[...])

    sub = o_ref.shape[0] // n_split
    parts = [slice(s * sub, (s + 1) * sub) for s in range(n_split)]
    read = lambda ref: [ref[p, :] for p in parts]
    ones_quad = ones_ref[...]
    inv_n = 1.0 / HEAD
    o = read(o_ref)
    mean = _each(lambda v: _head_sum(v, ones_quad) * inv_n, o)
    cen = _each(jnp.subtract, o, mean)
    var = _each(lambda c: _head_sum(c * c, ones_quad) * inv_n, cen)
    on = _each(lambda c, v: c * lax.rsqrt(v + LN_X_EPS) * lnw_ref[...] + lnb_ref[...], cen, var)
    y_rwkv = _each(lambda a, b, g: (a + b) * g, on, read(bonus_ref), read(gate_ref))
    mixed = _each(lambda yc, yr: (jnp.dot(yc, wbf_ref[:D_CONV, :], preferred_element_type=F32)
                                  + jnp.dot(_bf(yr), wbf_ref[D_CONV:, :], preferred_element_type=F32)),
                  read(yconv_ref), y_rwkv)
    h = _each(jnp.add, read(x_ref), mixed)
    u = _each(lambda v: _rms_norm(v, gffn_ref[...]), h)
    u_hi = _each(_bf, u)
    u_lo = _each(lambda a, b: _bf(a - b.astype(F32)), u, u_hi)
    by_hi = _each(lambda a: jnp.dot(a, rw_ref[...], preferred_element_type=F32), u_hi)
    by_lo = _each(lambda a: jnp.dot(a, rw_ref[:, :ROUTER_LANES], preferred_element_type=F32), u_lo)
    logits = _each(lambda a, b: a[:, :ROUTER_LANES] + a[:, ROUTER_LANES:] + b + rb_ref[...], by_hi, by_lo)
    routed = _each(_route, logits)

    lane = lax.broadcasted_iota(jnp.int32, (sub, ROUTER_LANES), 1)
    hit1 = _each(lambda rt: lane == rt[0], routed)
    hit2 = _each(lambda rt: lane == rt[1], routed)
    both = _each(lambda a, b: a.astype(F32) + b.astype(F32), hit1, hit2)
    inside = _each(lambda b: jnp.dot(tri_ref[...], _bf(b), preferred_element_type=F32), both)
    seen = seen_ref[...]
    for s, p in enumerate(parts):
        idx1, idx2, gate1, gate2 = routed[s]
        before = inside[s] + seen
        rank1 = jnp.sum(jnp.where(hit1[s], before, 0.0), axis=-1, keepdims=True)
        rank2 = jnp.sum(jnp.where(hit2[s], before, 0.0), axis=-1, keepdims=True)
        seen = seen + jnp.sum(both[s], axis=0, keepdims=True)
        route = jnp.zeros((sub, ROUTER_LANES), F32)
        for lane_id, col in ((LANE_IDX1, idx1.astype(F32)), (LANE_IDX2, idx2.astype(F32)), (LANE_RANK1, rank1),
                             (LANE_RANK2, rank2), (LANE_GATE1, gate1), (LANE_GATE2, gate2)):
            route = jnp.where(lane == lane_id, col, route)
        route_ref[p, :] = route
        route_t_ref[:, p] = route.T[:8, :]
        h_ref[p, :] = _bf(h[s])
        u_ref[p, :] = _pack_bf16_pairs(u[s])
    seen_ref[...] = seen
    count_ref[...] = jnp.broadcast_to(seen, count_ref.shape)


def _out_call(o, bonus, gate, yconv, x, ones_bd, ln_w, ln_b, w_out, g_ffn, router_w, router_b, tm, n_split):
    n_tok = x.shape[0]
    row = lambda width: pl.BlockSpec((tm, width), lambda i: (i, 0))
    full = lambda arr: pl.BlockSpec(arr.shape, lambda i: (0,) * arr.ndim, pipeline_mode=pl.Buffered(1))
    sub = tm // n_split
    tri_strict = (jnp.arange(sub)[:, None] > jnp.arange(sub)[None, :]).astype(BF16)
    params = (ones_bd, ln_w, ln_b, w_out, g_ffn, router_w, router_b, tri_strict)
    return pl.pallas_call(
        functools.partial(_out_kernel, n_split=n_split),
        grid=(n_tok // tm,),
        in_specs=[row(D_RWKV)] * 4 + [row(D_MODEL)] + [full(p) for p in params],
        out_specs=[row(D_MODEL), row(D_MODEL // 2), row(ROUTER_LANES),
                   pl.BlockSpec((8, tm), lambda i: (0, i)),
                   pl.BlockSpec((8, ROUTER_LANES), lambda i: (0, 0))],
        out_shape=[jax.ShapeDtypeStruct((n_tok, D_MODEL), BF16),
                   jax.ShapeDtypeStruct((n_tok, D_MODEL // 2), jnp.int32),
                   jax.ShapeDtypeStruct((n_tok, ROUTER_LANES), F32),
                   jax.ShapeDtypeStruct((8, n_tok), F32),
                   jax.ShapeDtypeStruct((8, ROUTER_LANES), F32)],
        scratch_shapes=[pltpu.VMEM((1, ROUTER_LANES), F32), pltpu.VMEM(w_out.shape, BF16)],
        compiler_params=pltpu.CompilerParams(
            dimension_semantics=("arbitrary",), vmem_limit_bytes=VMEM_LIMIT),
        name="out_proj_route",
    )(o, bonus, gate, yconv, x, *params)


SC_CORES = 2
SC_SUBCORES = 16
SC_ROWS = 64


def _sc_gather(table, idx):
    n_rows = idx.shape[0]
    width = table.shape[1]
    n_chunks = n_rows // (SC_CORES * SC_SUBCORES * SC_ROWS)
    mesh = plsc.VectorSubcoreMesh(core_axis_name="c", subcore_axis_name="s",
                                  num_cores=SC_CORES, num_subcores=SC_SUBCORES)

    def body(table_hbm, idx_hbm, out_hbm, idx_v, rows_v, sem):
        worker = lax.axis_index("s") * SC_CORES + lax.axis_index("c")

        @pl.loop(0, n_chunks)
        def _(j):
            chunk = worker * n_chunks + j
            pltpu.sync_copy(idx_hbm.at[chunk], idx_v)
            pltpu.async_copy(table_hbm.at[idx_v], rows_v, sem).wait()
            pltpu.sync_copy(rows_v, out_hbm.at[pl.ds(pl.multiple_of(chunk * SC_ROWS, SC_ROWS), SC_ROWS)])

    return pl.kernel(
        body,
        out_type=jax.ShapeDtypeStruct((n_rows, width), table.dtype),
        mesh=mesh,
        scratch_types=[pltpu.VMEM((SC_ROWS,), jnp.int32), pltpu.VMEM((SC_ROWS, width), table.dtype),
                       pltpu.SemaphoreType.DMA],
        name="sc_row_gather",
    )(table, idx.reshape(n_rows // SC_ROWS, SC_ROWS))


def _sc_scatter(rows, pos, n_out):
    n_rows, width = rows.shape
    n_slots = pos.shape[0] // n_rows
    slot_chunks = n_rows // SC_ROWS
    n_chunks = slot_chunks // (SC_CORES * SC_SUBCORES)
    mesh = plsc.VectorSubcoreMesh(core_axis_name="c", subcore_axis_name="s",
                                  num_cores=SC_CORES, num_subcores=SC_SUBCORES)

    def body(rows_hbm, pos_hbm, out_hbm, idx_v, rows_v, sem):
        worker = lax.axis_index("s") * SC_CORES + lax.axis_index("c")

        @pl.loop(0, n_chunks)
        def _(j):
            chunk = worker * n_chunks + j
            pltpu.sync_copy(rows_hbm.at[pl.ds(pl.multiple_of(chunk * SC_ROWS, SC_ROWS), SC_ROWS)], rows_v)
            for s in range(n_slots):
                pltpu.sync_copy(pos_hbm.at[s * slot_chunks + chunk], idx_v.at[s])
            copies = [pltpu.async_copy(rows_v, out_hbm.at[idx_v.at[s]], sem) for s in range(n_slots)]
            for copy in copies:
                copy.wait()

    return pl.kernel(
        body,
        out_type=jax.ShapeDtypeStruct((n_out, width), rows.dtype),
        mesh=mesh,
        scratch_types=[pltpu.VMEM((n_slots, SC_ROWS), jnp.int32), pltpu.VMEM((SC_ROWS, width), rows.dtype),
                       pltpu.SemaphoreType.DMA],
        name="sc_row_scatter",
    )(rows, pos.reshape(n_slots * slot_chunks, SC_ROWS))


def _expert_kernel(tile_expert_ref, n_valid_ref, x_ref, wg_ref, wu_ref, wd_ref, y_ref):
    del tile_expert_ref

    @pl.when(pl.program_id(0) < n_valid_ref[0])
    def _():
        x = _unpack_bf16_pairs(x_ref[...])
        gate = jnp.dot(x, wg_ref[0], preferred_element_type=F32)
        up = jnp.dot(x, wu_ref[0], preferred_element_type=F32)
        hid = gate * jax.nn.sigmoid(gate) * up
        y_ref[...] = _pack_bf16_pairs(jnp.dot(_bf(hid), wd_ref[0], preferred_element_type=F32))


def _expert_call(tile_expert, n_valid, x_sorted, w_gate, w_up, w_down, tm):
    n_rows = x_sorted.shape[0]
    rows = pl.BlockSpec((tm, D_MODEL // 2), lambda i, te, nv: (jnp.minimum(i, nv[0] - 1), 0))
    return pl.pallas_call(
        _expert_kernel,
        grid_spec=pltpu.PrefetchScalarGridSpec(
            num_scalar_prefetch=2,
            grid=(n_rows // tm,),
            in_specs=[rows,
                      pl.BlockSpec((1, D_MODEL, D_EXPERT), lambda i, te, nv: (te[i], 0, 0)),
                      pl.BlockSpec((1, D_MODEL, D_EXPERT), lambda i, te, nv: (te[i], 0, 0)),
                      pl.BlockSpec((1, D_EXPERT, D_MODEL), lambda i, te, nv: (te[i], 0, 0))],
            out_specs=rows),
        out_shape=jax.ShapeDtypeStruct((n_rows, D_MODEL // 2), jnp.int32),
        compiler_params=pltpu.CompilerParams(
            dimension_semantics=("arbitrary",), vmem_limit_bytes=VMEM_LIMIT),
        name="moe_experts",
    )(tile_expert, n_valid, x_sorted, w_gate, w_up, w_down)


def _final_kernel(h_ref, y1_ref, y2_ref, route_ref, gfin_ref, *rest):
    out_ref = rest[-1]
    route = route_ref[...]
    gate1 = route[:, LANE_GATE1:LANE_GATE1 + 1]
    gate2 = route[:, LANE_GATE2:LANE_GATE2 + 1]
    moe = gate1 * _unpack_bf16_pairs(y1_ref[...]).astype(F32) + gate2 * _unpack_bf16_pairs(y2_ref[...]).astype(F32)
    out_ref[...] = _rms_norm(h_ref[...] + moe, gfin_ref[...])


def _final_call(h, y_pairs, route, g_final, earlier, chunk, n_chunks, tm):
    n_tok = h.shape[0]
    n_blocks = n_tok // n_chunks // tm
    first = chunk * n_blocks
    in_specs = [pl.BlockSpec((tm, D_MODEL), lambda i: (first + i, 0)),
                pl.BlockSpec((tm, D_MODEL // 2), lambda i: (i, 0)),
                pl.BlockSpec((tm, D_MODEL // 2), lambda i: (i + n_blocks, 0)),
                pl.BlockSpec((tm, ROUTER_LANES), lambda i: (first + i, 0)),
                pl.BlockSpec((1, D_MODEL), lambda i: (0, 0))]
    args = [h, y_pairs, y_pairs, route, g_final]
    aliases = {}
    if earlier is not None:
        in_specs.append(pl.BlockSpec(memory_space=pl.ANY))
        args.append(earlier)
        aliases = {len(args) - 1: 0}
    return pl.pallas_call(
        _final_kernel,
        grid=(n_blocks,),
        in_specs=in_specs,
        out_specs=pl.BlockSpec((tm, D_MODEL), lambda i: (first + i, 0)),
        out_shape=jax.ShapeDtypeStruct((n_tok, D_MODEL), F32),
        input_output_aliases=aliases,
        compiler_params=pltpu.CompilerParams(
            dimension_semantics=("arbitrary",), vmem_limit_bytes=VMEM_LIMIT),
        name="moe_combine_norm",
    )(*args)


def _dispatch_plan(route_t, counts, tm):
    n_tok = route_t.shape[1]
    n_tiles = (2 * n_tok) // tm + N_EXPERTS
    counts = counts[0, :N_EXPERTS].astype(jnp.int32)
    tiles_per = (counts + tm - 1) // tm
    tile_end = jnp.cumsum(tiles_per)
    row_start = (tile_end - tiles_per) * tm
    experts = jnp.arange(N_EXPERTS, dtype=jnp.int32)

    def position(idx_lane, rank_lane):
        idx = route_t[idx_lane].astype(jnp.int32)
        start = jnp.sum(jnp.where(idx[None, :] == experts[:, None], row_start[:, None], 0), axis=0)
        return start + route_t[rank_lane].astype(jnp.int32)

    pos = jnp.concatenate([position(LANE_IDX1, LANE_RANK1), position(LANE_IDX2, LANE_RANK2)])
    n_valid = tile_end[-1:]
    tile = jnp.minimum(jnp.arange(n_tiles, dtype=jnp.int32), n_valid - 1)
    tile_expert = jnp.sum((tile_end[None, :] <= tile[:, None]).astype(jnp.int32), axis=1)
    return pos, n_tiles, tile_expert, n_valid


def _block(x, norm_mix_g, w_in, rwkv_mu, conv_w, decay_up, decay_base, aaa_up, aaa_base, gate_up,
           k_k, k_a, r_k, ln_x_w, ln_x_b, w_out, norm_ffn_g, router_group_w, router_group_b,
           router_expert_w, router_expert_b, expert_w_gate, expert_w_up, expert_w_down, norm_final_g,
           *, tm_in, tb_rwkv, tm_out, tm_expert, tm_final):
    bsz, seq, d_model = x.shape
    n_tok = bsz * seq
    row = lambda p: p.reshape(1, -1)
    half = LORA_WA // 2
    zeros = jnp.zeros((half, D_RWKV), F32)
    dup = _bf(jnp.concatenate([decay_up, zeros], axis=0))
    aup = _bf(jnp.concatenate([zeros, aaa_up], axis=0))
    head_of = jnp.arange(QUAD) // HEAD
    ones_quad = (head_of[:, None] == head_of[None, :]).astype(BF16)
    tri = (jnp.arange(CHUNK)[:, None] >= jnp.arange(CHUNK)[None, :]).astype(BF16)

    yconv, r, lw, k2, v, kk, kka, gate, bonus = _in_call(
        x, row(norm_mix_g), w_in, row(rwkv_mu), conv_w, dup, row(decay_base), aup, row(aaa_base),
        _bf(gate_up), row(k_k), row(k_a), row(r_k), ones_quad, tm_in, IN_SPLIT)
    o = _rwkv_call(r, lw, k2, v, kk, kka, tri, tb_rwkv)

    pad = ROUTER_LANES - N_EXPERTS - N_GROUPS
    router_w = jnp.concatenate([router_expert_w, router_group_w, jnp.zeros((d_model, pad), F32)], axis=1)
    router_hi = _bf(router_w)
    router_split = jnp.concatenate([router_hi, _bf(router_w - router_hi.astype(F32))], axis=1)
    router_b = jnp.concatenate([router_expert_b, router_group_b, jnp.zeros((pad,), F32)]).reshape(1, -1)
    flat = lambda t: t.reshape(n_tok, t.shape[-1])
    h, u_pairs, route, route_t, counts = _out_call(
        flat(o), flat(bonus), flat(gate), flat(yconv), flat(x), ones_quad,
        row(ln_x_w), row(ln_x_b), w_out, row(norm_ffn_g), router_split, router_b, tm_out, OUT_SPLIT)

    pos, n_tiles, tile_expert, n_valid = _dispatch_plan(route_t, counts, tm_expert)
    x_sorted = _sc_scatter(u_pairs, pos, n_tiles * tm_expert)
    y_sorted = _expert_call(tile_expert, n_valid, x_sorted, _bf(expert_w_gate), _bf(expert_w_up),
                            _bf(expert_w_down), tm_expert)
    per_chunk = n_tok // COMBINE_CHUNKS
    out = None
    for c in range(COMBINE_CHUNKS):
        lo = c * per_chunk
        pos_c = jnp.concatenate([pos[lo:lo + per_chunk], pos[n_tok + lo:n_tok + lo + per_chunk]])
        out = _final_call(h, _sc_gather(y_sorted, pos_c), route, row(norm_final_g), out, c, COMBINE_CHUNKS, tm_final)
    return out.reshape(bsz, seq, d_model)


def kernel(x, norm_mix_g, w_in, rwkv_mu, conv_w, decay_up, decay_base, aaa_up, aaa_base, gate_up, k_k, k_a, r_k, ln_x_w, ln_x_b, w_out, norm_ffn_g, router_group_w, router_group_b, router_expert_w, router_expert_b, expert_w_gate, expert_w_up, expert_w_down, norm_final_g):
    return _block(x, norm_mix_g[0], w_in[0], rwkv_mu[0], conv_w[0], decay_up[0], decay_base[0],
                  aaa_up[0], aaa_base[0], gate_up[0], k_k[0], k_a[0], r_k[0].reshape(-1), ln_x_w[0],
                  ln_x_b[0], w_out[0], norm_ffn_g[0], router_group_w[0], router_group_b[0],
                  router_expert_w[0], router_expert_b[0], expert_w_gate[0], expert_w_up[0],
                  expert_w_down[0], norm_final_g,
                  tm_in=512, tb_rwkv=256, tm_out=512, tm_expert=512, tm_final=512)
```

```python
import functools

import jax
import jax.numpy as jnp
from jax import lax
from jax.experimental import pallas as pl
from jax.experimental.pallas import tpu as pltpu
from jax.experimental.pallas import tpu_sc as plsc

F32 = jnp.float32
BF16 = jnp.bfloat16

D_MODEL = 1024
D_CONV = 512
CONV_WIDTH = 3
N_HEADS = 8
HEAD = 64
D_RWKV = N_HEADS * HEAD
LORA_WA = 128
GATE_LORA = 128
D_RWKV_PROJ = 3 * D_RWKV + LORA_WA + GATE_LORA
D_IN = 3 * D_CONV + D_RWKV_PROJ
N_GROUPS = 4
EXPERTS_PER_GROUP = 8
N_EXPERTS = N_GROUPS * EXPERTS_PER_GROUP
D_EXPERT = D_MODEL // 4
RMS_EPS = 1e-6
LN_X_EPS = 64e-5
L2_EPS = 1e-12

LANES = 128
CHUNK = 64
QUAD = 4 * HEAD
GROUP = 2 * HEAD
IN_SPLIT = 4
OUT_SPLIT = 4
COMBINE_CHUNKS = 2
ROUTER_LANES = 128

VMEM_LIMIT = 56 * 1024 * 1024


def _bf(x):
    return x.astype(BF16)


def _mm(a, b):
    return jnp.dot(_bf(a), _bf(b), preferred_element_type=F32)


def _mm_nt(a, b):
    return lax.dot_general(_bf(a), _bf(b), (((1,), (1,)), ((), ())), preferred_element_type=F32)


def _mm_exact_lhs(lhs_bf16, x, passes):
    acc = None
    rem = x
    for _ in range(passes):
        piece = _bf(rem)
        part = jnp.dot(lhs_bf16, piece, preferred_element_type=F32)
        acc = part if acc is None else acc + part
        rem = rem - piece.astype(F32)
    return acc


def _head_sum(x, ones_quad):
    xb = _bf(x)
    return jnp.concatenate(
        [jnp.dot(xb[:, q * QUAD:(q + 1) * QUAD], ones_quad, preferred_element_type=F32)
         for q in range(x.shape[1] // QUAD)], axis=1)


def _rms_norm(x, g):
    return x * lax.rsqrt(jnp.mean(x * x, axis=-1, keepdims=True) + RMS_EPS) * g


def _shift_rows(cur, prev_rows, k):
    rolled = pltpu.roll(cur, k, 0)
    prev_rolled = pltpu.roll(prev_rows, k, 0)
    n = cur.shape[0]
    head = jnp.concatenate([prev_rolled, rolled[8:]], axis=0) if n > 8 else prev_rolled
    row = lax.broadcasted_iota(jnp.int32, cur.shape, 0)
    return jnp.where(row < k, head, rolled)


def _in_kernel(x_ref, g_ref, w_ref, mu_ref, convw_ref, dup_ref, dbase_ref, aup_ref, abase_ref,
               gup_ref, kk_ref, ka_ref, rk_ref, ones_ref,
               yconv_ref, r_ref, lw_ref, k_ref, v_ref, kkn_ref, kka_ref, gate_ref, bonus_ref,
               carry_ref, wbf_ref, *, n_split):
    @pl.when((pl.program_id(0) == 0) & (pl.program_id(1) == 0))
    def _():
        wbf_ref[...] = _bf(w_ref[...])

    @pl.when(pl.program_id(1) == 0)
    def _():
        carry_ref[...] = jnp.zeros_like(carry_ref)

    sub = x_ref.shape[1] // n_split
    z_parts = []
    for s in range(n_split):
        u = _rms_norm(x_ref[0, s * sub:(s + 1) * sub, :], g_ref[...])
        z_parts.append(jnp.dot(_bf(u), wbf_ref[...], preferred_element_type=F32))
    prev_ch = carry_ref[:, :D_CONV]
    prev_zr = carry_ref[:, D_CONV:]
    outs = (yconv_ref, r_ref, lw_ref, k_ref, v_ref, kkn_ref, kka_ref, gate_ref, bonus_ref)
    for s, z in enumerate(z_parts):
        rows = slice(s * sub, (s + 1) * sub)
        prev_ch, prev_zr = _in_post(z, prev_ch, prev_zr, mu_ref, convw_ref, dup_ref, dbase_ref, aup_ref,
                                    abase_ref, gup_ref, kk_ref, ka_ref, rk_ref, ones_ref,
                                    [ref.at[0, rows, :] for ref in outs])
    carry_ref[:, :D_CONV] = prev_ch
    carry_ref[:, D_CONV:] = prev_zr


def _in_post(z, prev_ch, prev_zr, mu_ref, convw_ref, dup_ref, dbase_ref, aup_ref, abase_ref,
             gup_ref, kk_ref, ka_ref, rk_ref, ones_ref, outs):
    yconv_ref, r_ref, lw_ref, k_ref, v_ref, kkn_ref, kka_ref, gate_ref, bonus_ref = outs
    b_gate = z[:, :D_CONV]
    ch = z[:, D_CONV:2 * D_CONV] * z[:, 2 * D_CONV:3 * D_CONV]
    conv = (convw_ref[2:3, :] * ch
            + convw_ref[1:2, :] * _shift_rows(ch, prev_ch, 1)
            + convw_ref[0:1, :] * _shift_rows(ch, prev_ch, 2))
    yconv_ref[...] = _bf(b_gate * conv)

    zr = z[:, 3 * D_CONV:]
    zs = zr + (_shift_rows(zr, prev_zr, 1) - zr) * mu_ref[...]

    r = zs[:, :D_RWKV]
    k = zs[:, D_RWKV:2 * D_RWKV]
    v = zs[:, 2 * D_RWKV:3 * D_RWKV]
    wa_lo = zs[:, 3 * D_RWKV:3 * D_RWKV + LORA_WA]
    g_lo = zs[:, 3 * D_RWKV + LORA_WA:]

    dec_in = -(dbase_ref[...] + jnp.dot(_bf(jnp.tanh(wa_lo)), dup_ref[...], preferred_element_type=F32))
    softplus = jnp.maximum(dec_in, 0.0) + jnp.log(1.0 + jnp.exp(-jnp.abs(dec_in)))
    w = -softplus - 0.5
    lw_ref[...] = -jnp.exp(w)
    a = jax.nn.sigmoid(abase_ref[...] + jnp.dot(_bf(wa_lo), aup_ref[...], preferred_element_type=F32))
    gate_ref[...] = jnp.dot(_bf(jax.nn.sigmoid(g_lo)), gup_ref[...], preferred_element_type=F32)

    ones_quad = ones_ref[...]
    kk = k * kk_ref[...]
    norm = jnp.sqrt(_head_sum(kk * kk, ones_quad))
    kk = kk / jnp.maximum(norm, L2_EPS)
    k2 = k * (1.0 + (a - 1.0) * ka_ref[...])
    r_ref[...] = _bf(r)
    k_ref[...] = _bf(k2)
    v_ref[...] = _bf(v)
    kkn_ref[...] = _bf(kk)
    kka_ref[...] = _bf(kk * a)
    bonus_ref[...] = _head_sum(r * k2 * rk_ref[...], ones_quad) * v
    return ch[-8:], zr[-8:]


def _in_call(x, g, w_in, mu, conv_w, dup, dbase, aup, abase, gup, k_k, k_a, r_k, ones_quad, tm, n_split):
    bsz, seq, _ = x.shape
    full = lambda arr: pl.BlockSpec(arr.shape, lambda b, t: (0,) * arr.ndim, pipeline_mode=pl.Buffered(1))
    out_spec = pl.BlockSpec((1, tm, D_RWKV), lambda b, t: (b, t, 0))
    out_dtypes = (BF16, BF16, F32, BF16, BF16, BF16, BF16, F32, F32)
    params = (g, w_in, mu, conv_w, dup, dbase, aup, abase, gup, k_k, k_a, r_k, ones_quad)
    return pl.pallas_call(
        functools.partial(_in_kernel, n_split=n_split),
        grid=(bsz, seq // tm),
        in_specs=[pl.BlockSpec((1, tm, D_MODEL), lambda b, t: (b, t, 0))] + [full(p) for p in params],
        out_specs=[out_spec] * 9,
        out_shape=[jax.ShapeDtypeStruct((bsz, seq, D_RWKV), dt) for dt in out_dtypes],
        scratch_shapes=[pltpu.VMEM((8, D_CONV + D_RWKV_PROJ), F32), pltpu.VMEM(w_in.shape, BF16)],
        compiler_params=pltpu.CompilerParams(
            dimension_semantics=("arbitrary", "arbitrary"), vmem_limit_bytes=VMEM_LIMIT),
        name="in_proj",
    )(x, *params)


def _block_diag(y, bd_mask):
    return jnp.where(bd_mask, jnp.concatenate([y] * (GROUP // CHUNK), axis=0), 0.0)


def _each(fn, *lists):
    return [fn(*args) for args in zip(*lists)]


def _unit_lower_inverse(a_strict, t_idx, s_idx, bd):
    bdmm = lambda xs, ys: _each(lambda x, y: _mm(x, bd(y)), xs, ys)
    eye = (t_idx == s_idx).astype(F32)
    same8 = (t_idx // 8) == (s_idx // 8)
    a8 = _each(lambda a: jnp.where(same8, a, 0.0), a_strict)
    a8_2 = bdmm(a8, a8)
    a8_34 = bdmm(_each(lambda a, b: jnp.concatenate([a, b], axis=0), a8, a8_2), a8_2)
    inv = _each(lambda a, b, c: eye + a + b + c[:CHUNK], a8, a8_2, a8_34)
    inv = _each(jnp.add, inv, bdmm(inv, _each(lambda c: c[CHUNK:], a8_34)))
    size = 16
    while size <= CHUNK:
        off = ((t_idx // size) == (s_idx // size)) & ((t_idx // (size // 2)) != (s_idx // (size // 2)))
        cross = bdmm(_each(lambda a: jnp.where(off, a, 0.0), a_strict), inv)
        inv = _each(jnp.add, inv, bdmm(inv, cross))
        size *= 2
    return inv


def _chunk_local(r, lw, k, v, kk, kka, tri, t_idx, s_idx, bd_mask):
    bd = lambda y: _block_diag(y, bd_mask)
    bdmm = lambda xs, ys: _each(lambda x, y: _mm(x, bd(y)), xs, ys)
    bdmm2 = lambda xs, ys, zs: _each(lambda x, y, z: _mm(x, jnp.concatenate([bd(y), bd(z)], axis=1)), xs, ys, zs)
    left = lambda xs: _each(lambda x: x[:, :GROUP], xs)
    right = lambda xs: _each(lambda x: x[:, GROUP:], xs)
    top = lambda xs: _each(lambda x: x[:CHUNK], xs)
    bottom = lambda xs: _each(lambda x: x[CHUNK:], xs)

    cum = _each(lambda x: _mm_exact_lhs(tri, x, 3), lw)
    cum_last = _each(lambda c: c[CHUNK - 1:CHUNK, :], cum)
    p_incl = _each(jnp.exp, cum)
    p_excl = _each(lambda c, x: jnp.exp(c - x), cum, lw)
    p_inv = _each(lambda c: jnp.exp(-c), cum)
    to_end = _each(lambda cl, c: jnp.exp(cl - c), cum_last, cum)
    a_t = _each(lambda x, p: -x * p, kk, p_excl)
    r_t = _each(jnp.multiply, r, p_incl)
    b_t = _each(jnp.multiply, kka, p_inv)
    k_t = _each(jnp.multiply, k, p_inv)
    b_end = _each(jnp.multiply, kka, to_end)
    k_end = _each(jnp.multiply, k, to_end)

    ar = _each(lambda a, b: jnp.concatenate([a, b], axis=0), a_t, r_t)
    row2 = lax.broadcasted_iota(jnp.int32, (2 * CHUNK, 2 * GROUP), 0)
    col2 = lax.broadcasted_iota(jnp.int32, (2 * CHUNK, 2 * GROUP), 1) % CHUNK
    causal = col2 < (row2 % CHUNK) + (row2 // CHUNK)
    scores = _each(lambda x, y, z: jnp.where(causal, _mm_nt(x, jnp.concatenate([bd(y), bd(z)], axis=0)), 0.0),
                   ar, b_t, k_t)
    a_ab = top(left(scores))
    a_rb = bottom(left(scores))

    inv = _unit_lower_inverse(a_ab, t_idx, s_idx, bd)
    kv = bdmm(right(scores), v)
    wu = bdmm2(inv, a_t, top(kv))
    ro = bdmm2(a_rb, left(wu), right(wu))
    r_hat = _each(jnp.add, r_t, left(ro))
    o_loc = _each(jnp.add, right(ro), bottom(kv))

    wu_b = _each(lambda x, b: _mm(x.T, b), wu, b_end)
    trans = _each(lambda x: jnp.where(bd_mask, x[:GROUP], 0.0), wu_b)
    d_bd = _each(lambda x, y, z: jnp.where(bd_mask, x[GROUP:] + _mm(y.T, z), 0.0), wu_b, v, k_end)
    d_ls = _each(lambda d: sum(d[h * CHUNK:(h + 1) * CHUNK] for h in range(1, GROUP // CHUNK)) + d[:CHUNK], d_bd)
    p_end = _each(jnp.exp, cum_last)
    return r_hat, o_loc, trans, d_ls, p_end


def _rwkv_kernel(r_ref, lw_ref, k_ref, v_ref, kk_ref, kka_ref, tri_ref, o_ref, state_ref, *, n_sub):
    @pl.when(pl.program_id(1) == 0)
    def _():
        state_ref[...] = jnp.zeros_like(state_ref)

    t_idx = lax.broadcasted_iota(jnp.int32, (CHUNK, GROUP), 0)
    s_idx = lax.broadcasted_iota(jnp.int32, (CHUNK, GROUP), 1) % CHUNK
    bd_mask = (lax.broadcasted_iota(jnp.int32, (GROUP, GROUP), 0) // CHUNK
               == lax.broadcasted_iota(jnp.int32, (GROUP, GROUP), 1) // CHUNK)
    n_group = D_RWKV // GROUP
    where = [(slice(c * CHUNK, (c + 1) * CHUNK), slice(g * GROUP, (g + 1) * GROUP))
             for c in range(n_sub) for g in range(n_group)]
    load = lambda ref: [ref[0, rows, lanes].astype(F32) for rows, lanes in where]
    r_hat, o_loc, trans, d_ls, p_end = _chunk_local(
        load(r_ref), load(lw_ref), load(k_ref), load(v_ref), load(kk_ref), load(kka_ref),
        tri_ref[...], t_idx, s_idx, bd_mask)

    state = [state_ref[g] for g in range(n_group)]
    for c in range(n_sub):
        chains = range(c * n_group, (c + 1) * n_group)
        out = [_mm_nt(r_hat[i], _block_diag(state[g], bd_mask)) + o_loc[i] for g, i in enumerate(chains)]
        for g, i in enumerate(chains):
            rows, lanes = where[i]
            o_ref[0, rows, lanes] = out[g]
        state = [state[g] * p_end[i] + _mm(state[g], trans[i]) + d_ls[i] for g, i in enumerate(chains)]
    for g in range(n_group):
        state_ref[g] = state[g]


def _rwkv_call(r, lw, k, v, kk, kka, tri, tb):
    bsz, seq, _ = r.shape
    spec = pl.BlockSpec((1, tb, D_RWKV), lambda b, t: (b, t, 0))
    return pl.pallas_call(
        functools.partial(_rwkv_kernel, n_sub=tb // CHUNK),
        grid=(bsz, seq // tb),
        in_specs=[spec] * 6 + [pl.BlockSpec(tri.shape, lambda b, t: (0, 0))],
        out_specs=spec,
        out_shape=jax.ShapeDtypeStruct((bsz, seq, D_RWKV), F32),
        scratch_shapes=[pltpu.VMEM((D_RWKV // GROUP, HEAD, GROUP), F32)],
        compiler_params=pltpu.CompilerParams(
            dimension_semantics=("arbitrary", "arbitrary"), vmem_limit_bytes=VMEM_LIMIT),
        name="rwkv_chunk",
    )(r, lw, k, v, kk, kka, tri)


def _route(logits):
    lane = lax.broadcasted_iota(jnp.int32, logits.shape, 1)
    neg = -jnp.inf
    big = ROUTER_LANES
    is_group = (lane >= N_EXPERTS) & (lane < N_EXPERTS + N_GROUPS)
    gl = jnp.where(is_group, logits, neg)
    ge = jnp.exp(gl - jnp.max(gl, axis=-1, keepdims=True))
    gprob = ge / jnp.sum(ge, axis=-1, keepdims=True)
    g_top = jnp.max(gprob, axis=-1, keepdims=True)
    g_idx = jnp.min(jnp.where(is_group & (gprob == g_top), lane - N_EXPERTS, big), axis=-1, keepdims=True)

    in_group = (lane < N_EXPERTS) & ((lane // EXPERTS_PER_GROUP) == g_idx)
    el = jnp.where(in_group, logits, neg)
    top1 = jnp.max(el, axis=-1, keepdims=True)
    idx1 = jnp.min(jnp.where(in_group & (el == top1), lane, big), axis=-1, keepdims=True)
    el2 = jnp.where(lane == idx1, neg, el)
    top2 = jnp.max(el2, axis=-1, keepdims=True)
    idx2 = jnp.min(jnp.where(in_group & (lane != idx1) & (el2 == top2), lane, big), axis=-1, keepdims=True)
    e2 = jnp.exp(top2 - top1)
    denom = 1.0 + e2
    return idx1, idx2, g_top * (1.0 / denom), g_top * (e2 / denom)


def _pack_bf16_pairs(x):
    bits = lax.bitcast_convert_type(_bf(x).astype(F32), jnp.uint32)
    k = x.shape[1] // 2
    return lax.bitcast_convert_type((bits[:, :k] >> 16) | bits[:, k:], jnp.int32)


def _unpack_bf16_pairs(words):
    bits = lax.bitcast_convert_type(words, jnp.uint32)
    lo = lax.bitcast_convert_type(bits << 16, F32)
    hi = lax.bitcast_convert_type(bits & jnp.uint32(0xFFFF0000), F32)
    return _bf(jnp.concatenate([lo, hi], axis=1))


LANE_IDX1, LANE_IDX2, LANE_RANK1, LANE_RANK2, LANE_GATE1, LANE_GATE2 = range(6)


def _out_kernel(o_ref, bonus_ref, gate_ref, yconv_ref, x_ref, ones_ref, lnw_ref, lnb_ref,
                wout_ref, gffn_ref, rw_ref, rb_ref, tri_ref, h_ref, u_ref, route_ref, route_t_ref, count_ref,
                seen_ref, wbf_ref, *, n_split):
    @pl.when(pl.program_id(0) == 0)
    def _():
        seen_ref[...] = jnp.zeros_like(seen_ref)
        wbf_ref[...] = _bf(wout_ref[...])

    sub = o_ref.shape[0] // n_split
    parts = [slice(s * sub, (s + 1) * sub) for s in range(n_split)]
    read = lambda ref: [ref[p, :] for p in parts]
    ones_quad = ones_ref[...]
    inv_n = 1.0 / HEAD
    o = read(o_ref)
    mean = _each(lambda v: _head_sum(v, ones_quad) * inv_n, o)
    cen = _each(jnp.subtract, o, mean)
    var = _each(lambda c: _head_sum(c * c, ones_quad) * inv_n, cen)
    on = _each(lambda c, v: c * lax.rsqrt(v + LN_X_EPS) * lnw_ref[...] + lnb_ref[...], cen, var)
    y_rwkv = _each(lambda a, b, g: (a + b) * g, on, read(bonus_ref), read(gate_ref))
    mixed = _each(lambda yc, yr: (jnp.dot(yc, wbf_ref[:D_CONV, :], preferred_element_type=F32)
                                  + jnp.dot(_bf(yr), wbf_ref[D_CONV:, :], preferred_element_type=F32)),
                  read(yconv_ref), y_rwkv)
    h = _each(jnp.add, read(x_ref), mixed)
    u = _each(lambda v: _rms_norm(v, gffn_ref[...]), h)
    u_hi = _each(_bf, u)
    u_lo = _each(lambda a, b: _bf(a - b.astype(F32)), u, u_hi)
    by_hi = _each(lambda a: jnp.dot(a, rw_ref[...], preferred_element_type=F32), u_hi)
    by_lo = _each(lambda a: jnp.dot(a, rw_ref[:, :ROUTER_LANES], preferred_element_type=F32), u_lo)
    logits = _each(lambda a, b: a[:, :ROUTER_LANES] + a[:, ROUTER_LANES:] + b + rb_ref[...], by_hi, by_lo)
    routed = _each(_route, logits)

    lane = lax.broadcasted_iota(jnp.int32, (sub, ROUTER_LANES), 1)
    hit1 = _each(lambda rt: lane == rt[0], routed)
    hit2 = _each(lambda rt: lane == rt[1], routed)
    both = _each(lambda a, b: a.astype(F32) + b.astype(F32), hit1, hit2)
    inside = _each(lambda b: jnp.dot(tri_ref[...], _bf(b), preferred_element_type=F32), both)
    seen = seen_ref[...]
    for s, p in enumerate(parts):
        idx1, idx2, gate1, gate2 = routed[s]
        before = inside[s] + seen
        rank1 = jnp.sum(jnp.where(hit1[s], before, 0.0), axis=-1, keepdims=True)
        rank2 = jnp.sum(jnp.where(hit2[s], before, 0.0), axis=-1, keepdims=True)
        seen = seen + jnp.sum(both[s], axis=0, keepdims=True)
        route = jnp.zeros((sub, ROUTER_LANES), F32)
        for lane_id, col in ((LANE_IDX1, idx1.astype(F32)), (LANE_IDX2, idx2.astype(F32)), (LANE_RANK1, rank1),
                             (LANE_RANK2, rank2), (LANE_GATE1, gate1), (LANE_GATE2, gate2)):
            route = jnp.where(lane == lane_id, col, route)
        route_ref[p, :] = route
        route_t_ref[:, p] = route.T[:8, :]
        h_ref[p, :] = _bf(h[s])
        u_ref[p, :] = _pack_bf16_pairs(u[s])
    seen_ref[...] = seen
    count_ref[...] = jnp.broadcast_to(seen, count_ref.shape)


def _out_call(o, bonus, gate, yconv, x, ones_bd, ln_w, ln_b, w_out, g_ffn, router_w, router_b, tm, n_split):
    n_tok = x.shape[0]
    row = lambda width: pl.BlockSpec((tm, width), lambda i: (i, 0))
    full = lambda arr: pl.BlockSpec(arr.shape, lambda i: (0,) * arr.ndim, pipeline_mode=pl.Buffered(1))
    sub = tm // n_split
    tri_strict = (jnp.arange(sub)[:, None] > jnp.arange(sub)[None, :]).astype(BF16)
    params = (ones_bd, ln_w, ln_b, w_out, g_ffn, router_w, router_b, tri_strict)
    return pl.pallas_call(
        functools.partial(_out_kernel, n_split=n_split),
        grid=(n_tok // tm,),
        in_specs=[row(D_RWKV)] * 4 + [row(D_MODEL)] + [full(p) for p in params],
        out_specs=[row(D_MODEL), row(D_MODEL // 2), row(ROUTER_LANES),
                   pl.BlockSpec((8, tm), lambda i: (0, i)),
                   pl.BlockSpec((8, ROUTER_LANES), lambda i: (0, 0))],
        out_shape=[jax.ShapeDtypeStruct((n_tok, D_MODEL), BF16),
                   jax.ShapeDtypeStruct((n_tok, D_MODEL // 2), jnp.int32),
                   jax.ShapeDtypeStruct((n_tok, ROUTER_LANES), F32),
                   jax.ShapeDtypeStruct((8, n_tok), F32),
                   jax.ShapeDtypeStruct((8, ROUTER_LANES), F32)],
        scratch_shapes=[pltpu.VMEM((1, ROUTER_LANES), F32), pltpu.VMEM(w_out.shape, BF16)],
        compiler_params=pltpu.CompilerParams(
            dimension_semantics=("arbitrary",), vmem_limit_bytes=VMEM_LIMIT),
        name="out_proj_route",
    )(o, bonus, gate, yconv, x, *params)


SC_CORES = 2
SC_SUBCORES = 16
SC_ROWS = 64


def _sc_mesh():
    return plsc.VectorSubcoreMesh(core_axis_name="c", subcore_axis_name="s",
                                  num_cores=SC_CORES, num_subcores=SC_SUBCORES)


def _sc_worker():
    return lax.axis_index("s") * SC_CORES + lax.axis_index("c")


def _sc_gather(table, idx):
    n_rows = idx.shape[0]
    width = table.shape[1]
    n_chunks = n_rows // (SC_CORES * SC_SUBCORES * SC_ROWS)

    def body(table_hbm, idx_hbm, out_hbm, idx_v, rows_v, gather_sem, write_sem):
        first = _sc_worker() * n_chunks
        pltpu.sync_copy(idx_hbm.at[pl.ds(first, n_chunks)], idx_v)
        gather = lambda j: pltpu.async_copy(table_hbm.at[idx_v.at[j]], rows_v.at[j % 2], gather_sem.at[j % 2])
        gathers = [gather(0)]
        writes = []
        for j in range(n_chunks):
            gathers[j].wait()
            if j + 1 < n_chunks:
                if j >= 1:
                    writes[j - 1].wait()
                gathers.append(gather(j + 1))
            dst = out_hbm.at[pl.ds(pl.multiple_of((first + j) * SC_ROWS, SC_ROWS), SC_ROWS)]
            writes.append(pltpu.async_copy(rows_v.at[j % 2], dst, write_sem.at[j % 2]))
        for j in range(max(n_chunks - 2, 0), n_chunks):
            writes[j].wait()

    return pl.kernel(
        body,
        out_type=jax.ShapeDtypeStruct((n_rows, width), table.dtype),
        mesh=_sc_mesh(),
        scratch_types=[pltpu.VMEM((n_chunks, SC_ROWS), jnp.int32), pltpu.VMEM((2, SC_ROWS, width), table.dtype),
                       pltpu.SemaphoreType.DMA((2,)), pltpu.SemaphoreType.DMA((2,))],
        name="sc_row_gather",
    )(table, idx.reshape(n_rows // SC_ROWS, SC_ROWS))


def _sc_scatter(rows, pos, n_out):
    n_rows, width = rows.shape
    n_slots = pos.shape[0] // n_rows
    slot_chunks = n_rows // SC_ROWS
    n_chunks = slot_chunks // (SC_CORES * SC_SUBCORES)

    def body(rows_hbm, pos_hbm, out_hbm, idx_v, rows_v, read_sem, scatter_sem):
        first = _sc_worker() * n_chunks
        for s in range(n_slots):
            pltpu.sync_copy(pos_hbm.at[pl.ds(s * slot_chunks + first, n_chunks)], idx_v.at[s])
        read = lambda j: pltpu.async_copy(
            rows_hbm.at[pl.ds(pl.multiple_of((first + j) * SC_ROWS, SC_ROWS), SC_ROWS)],
            rows_v.at[j % 2], read_sem.at[j % 2])
        reads = [read(0)]
        scatters = []
        for j in range(n_chunks):
            reads[j].wait()
            if j + 1 < n_chunks:
                if j >= 1:
                    for copy in scatters[j - 1]:
                        copy.wait()
                reads.append(read(j + 1))
            scatters.append([pltpu.async_copy(rows_v.at[j % 2], out_hbm.at[idx_v.at[s, j]], scatter_sem.at[j % 2])
                             for s in range(n_slots)])
        for j in range(max(n_chunks - 2, 0), n_chunks):
            for copy in scatters[j]:
                copy.wait()

    return pl.kernel(
        body,
        out_type=jax.ShapeDtypeStruct((n_out, width), rows.dtype),
        mesh=_sc_mesh(),
        scratch_types=[pltpu.VMEM((n_slots, n_chunks, SC_ROWS), jnp.int32),
                       pltpu.VMEM((2, SC_ROWS, width), rows.dtype),
                       pltpu.SemaphoreType.DMA((2,)), pltpu.SemaphoreType.DMA((2,))],
        name="sc_row_scatter",
    )(rows, pos.reshape(n_slots * slot_chunks, SC_ROWS))


def _expert_kernel(tile_expert_ref, n_valid_ref, x_ref, wg_ref, wu_ref, wd_ref, y_ref):
    del tile_expert_ref

    @pl.when(pl.program_id(0) < n_valid_ref[0])
    def _():
        x = _unpack_bf16_pairs(x_ref[...])
        gate = jnp.dot(x, _bf(wg_ref[0]), preferred_element_type=F32)
        up = jnp.dot(x, _bf(wu_ref[0]), preferred_element_type=F32)
        hid = gate * jax.nn.sigmoid(gate) * up
        y_ref[...] = _pack_bf16_pairs(jnp.dot(_bf(hid), _bf(wd_ref[0]), preferred_element_type=F32))


def _expert_call(tile_expert, n_valid, x_sorted, w_gate, w_up, w_down, tm):
    n_rows = x_sorted.shape[0]
    rows = pl.BlockSpec((tm, D_MODEL // 2), lambda i, te, nv: (jnp.minimum(i, nv[0] - 1), 0))
    return pl.pallas_call(
        _expert_kernel,
        grid_spec=pltpu.PrefetchScalarGridSpec(
            num_scalar_prefetch=2,
            grid=(n_rows // tm,),
            in_specs=[rows,
                      pl.BlockSpec((1, D_MODEL, D_EXPERT), lambda i, te, nv: (te[i], 0, 0)),
                      pl.BlockSpec((1, D_MODEL, D_EXPERT), lambda i, te, nv: (te[i], 0, 0)),
                      pl.BlockSpec((1, D_EXPERT, D_MODEL), lambda i, te, nv: (te[i], 0, 0))],
            out_specs=rows),
        out_shape=jax.ShapeDtypeStruct((n_rows, D_MODEL // 2), jnp.int32),
        compiler_params=pltpu.CompilerParams(
            dimension_semantics=("arbitrary",), vmem_limit_bytes=VMEM_LIMIT),
        name="moe_experts",
    )(tile_expert, n_valid, x_sorted, w_gate, w_up, w_down)


def _final_kernel(h_ref, y1_ref, y2_ref, route_ref, gfin_ref, *rest):
    out_ref = rest[-1]
    route = route_ref[...]
    gate1 = route[:, LANE_GATE1:LANE_GATE1 + 1]
    gate2 = route[:, LANE_GATE2:LANE_GATE2 + 1]
    moe = gate1 * _unpack_bf16_pairs(y1_ref[...]).astype(F32) + gate2 * _unpack_bf16_pairs(y2_ref[...]).astype(F32)
    out_ref[...] = _rms_norm(h_ref[...] + moe, gfin_ref[...])


def _final_call(h, y_pairs, route, g_final, earlier, chunk, n_chunks, tm):
    n_tok = h.shape[0]
    n_blocks = n_tok // n_chunks // tm
    first = chunk * n_blocks
    in_specs = [pl.BlockSpec((tm, D_MODEL), lambda i: (first + i, 0)),
                pl.BlockSpec((tm, D_MODEL // 2), lambda i: (i, 0)),
                pl.BlockSpec((tm, D_MODEL // 2), lambda i: (i + n_blocks, 0)),
                pl.BlockSpec((tm, ROUTER_LANES), lambda i: (first + i, 0)),
                pl.BlockSpec((1, D_MODEL), lambda i: (0, 0))]
    args = [h, y_pairs, y_pairs, route, g_final]
    aliases = {}
    if earlier is not None:
        in_specs.append(pl.BlockSpec(memory_space=pl.ANY))
        args.append(earlier)
        aliases = {len(args) - 1: 0}
    return pl.pallas_call(
        _final_kernel,
        grid=(n_blocks,),
        in_specs=in_specs,
        out_specs=pl.BlockSpec((tm, D_MODEL), lambda i: (first + i, 0)),
        out_shape=jax.ShapeDtypeStruct((n_tok, D_MODEL), F32),
        input_output_aliases=aliases,
        compiler_params=pltpu.CompilerParams(
            dimension_semantics=("arbitrary",), vmem_limit_bytes=VMEM_LIMIT),
        name="moe_combine_norm",
    )(*args)


def _dispatch_plan(route_t, counts, tm):
    n_tok = route_t.shape[1]
    n_tiles = (2 * n_tok) // tm + N_EXPERTS
    counts = counts[0, :N_EXPERTS].astype(jnp.int32)
    tiles_per = (counts + tm - 1) // tm
    tile_end = jnp.cumsum(tiles_per)
    row_start = (tile_end - tiles_per) * tm
    experts = jnp.arange(N_EXPERTS, dtype=jnp.int32)

    def position(idx_lane, rank_lane):
        idx = route_t[idx_lane].astype(jnp.int32)
        start = jnp.sum(jnp.where(idx[None, :] == experts[:, None], row_start[:, None], 0), axis=0)
        return start + route_t[rank_lane].astype(jnp.int32)

    pos = jnp.concatenate([position(LANE_IDX1, LANE_RANK1), position(LANE_IDX2, LANE_RANK2)])
    n_valid = tile_end[-1:]
    tile = jnp.minimum(jnp.arange(n_tiles, dtype=jnp.int32), n_valid - 1)
    tile_expert = jnp.sum((tile_end[None, :] <= tile[:, None]).astype(jnp.int32), axis=1)
    return pos, n_tiles, tile_expert, n_valid


def _block(x, norm_mix_g, w_in, rwkv_mu, conv_w, decay_up, decay_base, aaa_up, aaa_base, gate_up,
           k_k, k_a, r_k, ln_x_w, ln_x_b, w_out, norm_ffn_g, router_group_w, router_group_b,
           router_expert_w, router_expert_b, expert_w_gate, expert_w_up, expert_w_down, norm_final_g,
           *, tm_in, tb_rwkv, tm_out, tm_expert, tm_final):
    bsz, seq, d_model = x.shape
    n_tok = bsz * seq
    row = lambda p: p.reshape(1, -1)
    half = LORA_WA // 2
    zeros = jnp.zeros((half, D_RWKV), F32)
    dup = _bf(jnp.concatenate([decay_up, zeros], axis=0))
    aup = _bf(jnp.concatenate([zeros, aaa_up], axis=0))
    head_of = jnp.arange(QUAD) // HEAD
    ones_quad = (head_of[:, None] == head_of[None, :]).astype(BF16)
    tri = (jnp.arange(CHUNK)[:, None] >= jnp.arange(CHUNK)[None, :]).astype(BF16)

    yconv, r, lw, k2, v, kk, kka, gate, bonus = _in_call(
        x, row(norm_mix_g), w_in, row(rwkv_mu), conv_w, dup, row(decay_base), aup, row(aaa_base),
        _bf(gate_up), row(k_k), row(k_a), row(r_k), ones_quad, tm_in, IN_SPLIT)
    o = _rwkv_call(r, lw, k2, v, kk, kka, tri, tb_rwkv)

    pad = ROUTER_LANES - N_EXPERTS - N_GROUPS
    router_w = jnp.concatenate([router_expert_w, router_group_w, jnp.zeros((d_model, pad), F32)], axis=1)
    router_hi = _bf(router_w)
    router_split = jnp.concatenate([router_hi, _bf(router_w - router_hi.astype(F32))], axis=1)
    router_b = jnp.concatenate([router_expert_b, router_group_b, jnp.zeros((pad,), F32)]).reshape(1, -1)
    flat = lambda t: t.reshape(n_tok, t.shape[-1])
    h, u_pairs, route, route_t, counts = _out_call(
        flat(o), flat(bonus), flat(gate), flat(yconv), flat(x), ones_quad,
        row(ln_x_w), row(ln_x_b), w_out, row(norm_ffn_g), router_split, router_b, tm_out, OUT_SPLIT)

    pos, n_tiles, tile_expert, n_valid = _dispatch_plan(route_t, counts, tm_expert)
    x_sorted = _sc_scatter(u_pairs, pos, n_tiles * tm_expert)
    y_sorted = _expert_call(tile_expert, n_valid, x_sorted, expert_w_gate, expert_w_up, expert_w_down, tm_expert)
    per_chunk = n_tok // COMBINE_CHUNKS
    out = None
    for c in range(COMBINE_CHUNKS):
        lo = c * per_chunk
        pos_c = jnp.concatenate([pos[lo:lo + per_chunk], pos[n_tok + lo:n_tok + lo + per_chunk]])
        out = _final_call(h, _sc_gather(y_sorted, pos_c), route, row(norm_final_g), out, c, COMBINE_CHUNKS, tm_final)
    return out.reshape(bsz, seq, d_model)


def kernel(x, norm_mix_g, w_in, rwkv_mu, conv_w, decay_up, decay_base, aaa_up, aaa_base, gate_up, k_k, k_a, r_k, ln_x_w, ln_x_b, w_out, norm_ffn_g, router_group_w, router_group_b, router_expert_w, router_expert_b, expert_w_gate, expert_w_up, expert_w_down, norm_final_g):
    return _block(x, norm_mix_g[0], w_in[0], rwkv_mu[0], conv_w[0], decay_up[0], decay_base[0],
                  aaa_up[0], aaa_base[0], gate_up[0], k_k[0], k_a[0], r_k[0].reshape(-1), ln_x_w[0],
                  ln_x_b[0], w_out[0], norm_ffn_g[0], router_group_w[0], router_group_b[0],
                  router_expert_w[0], router_expert_b[0], expert_w_gate[0], expert_w_up[0],
                  expert_w_down[0], norm_final_g,
                  tm_in=512, tb_rwkv=256, tm_out=512, tm_expert=512, tm_final=512)
```

```python
import functools

import jax
import jax.numpy as jnp
from jax import lax
from jax.experimental import pallas as pl
from jax.experimental.pallas import tpu as pltpu
from jax.experimental.pallas import tpu_sc as plsc

F32 = jnp.float32
BF16 = jnp.bfloat16

D_MODEL = 1024
D_CONV = 512
CONV_WIDTH = 3
N_HEADS = 8
HEAD = 64
D_RWKV = N_HEADS * HEAD
LORA_WA = 128
GATE_LORA = 128
D_RWKV_PROJ = 3 * D_RWKV + LORA_WA + GATE_LORA
D_IN = 3 * D_CONV + D_RWKV_PROJ
N_GROUPS = 4
EXPERTS_PER_GROUP = 8
N_EXPERTS = N_GROUPS * EXPERTS_PER_GROUP
D_EXPERT = D_MODEL // 4
RMS_EPS = 1e-6
LN_X_EPS = 64e-5
L2_EPS = 1e-12

LANES = 128
CHUNK = 64
QUAD = 4 * HEAD
GROUP = 2 * HEAD
OUT_SPLIT = 4
COMBINE_CHUNKS = 4
ROUTER_LANES = 128

VMEM_LIMIT = 56 * 1024 * 1024


def _bf(x):
    return x.astype(BF16)


def _mm(a, b):
    return jnp.dot(_bf(a), _bf(b), preferred_element_type=F32)


def _mm_nt(a, b):
    return lax.dot_general(_bf(a), _bf(b), (((1,), (1,)), ((), ())), preferred_element_type=F32)


def _mm_exact_lhs(lhs_bf16, x, passes):
    acc = None
    rem = x
    for _ in range(passes):
        piece = _bf(rem)
        part = jnp.dot(lhs_bf16, piece, preferred_element_type=F32)
        acc = part if acc is None else acc + part
        rem = rem - piece.astype(F32)
    return acc


def _head_sum(x, ones_quad):
    xb = _bf(x)
    return jnp.concatenate(
        [jnp.dot(xb[:, q * QUAD:(q + 1) * QUAD], ones_quad, preferred_element_type=F32)
         for q in range(x.shape[1] // QUAD)], axis=1)


def _rms_norm(x, g):
    return x * lax.rsqrt(jnp.mean(x * x, axis=-1, keepdims=True) + RMS_EPS) * g


def _shift_rows(cur, prev_rows, k):
    rolled = pltpu.roll(cur, k, 0)
    prev_rolled = pltpu.roll(prev_rows, k, 0)
    n = cur.shape[0]
    head = jnp.concatenate([prev_rolled, rolled[8:]], axis=0) if n > 8 else prev_rolled
    row = lax.broadcasted_iota(jnp.int32, cur.shape, 0)
    return jnp.where(row < k, head, rolled)


def _in_kernel(x_ref, g_ref, w_ref, mu_ref, convw_ref, dup_ref, dbase_ref, aup_ref, abase_ref,
               gup_ref, kk_ref, ka_ref, rk_ref, ones_ref,
               yconv_ref, r_ref, lw_ref, k_ref, v_ref, kkn_ref, kka_ref, gate_ref, bonus_ref,
               carry_ref, wbf_ref):
    @pl.when((pl.program_id(0) == 0) & (pl.program_id(1) == 0))
    def _():
        wbf_ref[...] = _bf(w_ref[...])

    @pl.when(pl.program_id(1) == 0)
    def _():
        carry_ref[...] = jnp.zeros_like(carry_ref)

    u = _bf(_rms_norm(x_ref[0], g_ref[...]))
    n_conv = 3 * D_CONV
    rk0 = n_conv
    bounds = dict(conv=(0, n_conv), lora=(rk0 + 3 * D_RWKV, D_IN), k=(rk0 + D_RWKV, rk0 + 2 * D_RWKV),
                  r=(rk0, rk0 + D_RWKV), v=(rk0 + 2 * D_RWKV, rk0 + 3 * D_RWKV))
    z = {name: jnp.dot(u, wbf_ref[:, lo:hi], preferred_element_type=F32) for name, (lo, hi) in bounds.items()}

    def lerp(name):
        lo, hi = bounds[name]
        cur = z[name]
        prev = carry_ref[:, D_CONV + lo - n_conv:D_CONV + hi - n_conv]
        mixed = cur + (_shift_rows(cur, prev, 1) - cur) * mu_ref[:, lo - n_conv:hi - n_conv]
        carry_ref[:, D_CONV + lo - n_conv:D_CONV + hi - n_conv] = cur[-8:]
        return mixed

    b_gate = z["conv"][:, :D_CONV]
    ch = z["conv"][:, D_CONV:2 * D_CONV] * z["conv"][:, 2 * D_CONV:]
    prev_ch = carry_ref[:, :D_CONV]
    conv = (convw_ref[2:3, :] * ch
            + convw_ref[1:2, :] * _shift_rows(ch, prev_ch, 1)
            + convw_ref[0:1, :] * _shift_rows(ch, prev_ch, 2))
    yconv_ref[0] = _bf(b_gate * conv)
    carry_ref[:, :D_CONV] = ch[-8:]

    lora = lerp("lora")
    wa_lo = lora[:, :LORA_WA]
    g_lo = lora[:, LORA_WA:]
    dec_in = -(dbase_ref[...] + jnp.dot(_bf(jnp.tanh(wa_lo)), dup_ref[...], preferred_element_type=F32))
    softplus = jnp.maximum(dec_in, 0.0) + jnp.log(1.0 + jnp.exp(-jnp.abs(dec_in)))
    w = -softplus - 0.5
    lw_ref[0] = -jnp.exp(w)
    a = jax.nn.sigmoid(abase_ref[...] + jnp.dot(_bf(wa_lo), aup_ref[...], preferred_element_type=F32))
    gate_ref[0] = jnp.dot(_bf(jax.nn.sigmoid(g_lo)), gup_ref[...], preferred_element_type=F32)

    ones_quad = ones_ref[...]
    k = lerp("k")
    kk = k * kk_ref[...]
    norm = jnp.sqrt(_head_sum(kk * kk, ones_quad))
    kk = kk / jnp.maximum(norm, L2_EPS)
    k2 = k * (1.0 + (a - 1.0) * ka_ref[...])
    k_ref[0] = _bf(k2)
    kkn_ref[0] = _bf(kk)
    kka_ref[0] = _bf(kk * a)
    r = lerp("r")
    r_ref[0] = _bf(r)
    v = lerp("v")
    v_ref[0] = _bf(v)
    bonus_ref[0] = _head_sum(r * k2 * rk_ref[...], ones_quad) * v


def _in_call(x, g, w_in, mu, conv_w, dup, dbase, aup, abase, gup, k_k, k_a, r_k, ones_quad, tm):
    bsz, seq, _ = x.shape
    full = lambda arr: pl.BlockSpec(arr.shape, lambda b, t: (0,) * arr.ndim, pipeline_mode=pl.Buffered(1))
    out_spec = pl.BlockSpec((1, tm, D_RWKV), lambda b, t: (b, t, 0))
    out_dtypes = (BF16, BF16, F32, BF16, BF16, BF16, BF16, F32, F32)
    params = (g, w_in, mu, conv_w, dup, dbase, aup, abase, gup, k_k, k_a, r_k, ones_quad)
    return pl.pallas_call(
        _in_kernel,
        grid=(bsz, seq // tm),
        in_specs=[pl.BlockSpec((1, tm, D_MODEL), lambda b, t: (b, t, 0))] + [full(p) for p in params],
        out_specs=[out_spec] * 9,
        out_shape=[jax.ShapeDtypeStruct((bsz, seq, D_RWKV), dt) for dt in out_dtypes],
        scratch_shapes=[pltpu.VMEM((8, D_CONV + D_RWKV_PROJ), F32), pltpu.VMEM(w_in.shape, BF16)],
        compiler_params=pltpu.CompilerParams(
            dimension_semantics=("arbitrary", "arbitrary"), vmem_limit_bytes=VMEM_LIMIT),
        name="in_proj",
    )(x, *params)


def _block_diag(y, bd_mask):
    return jnp.where(bd_mask, jnp.concatenate([y] * (GROUP // CHUNK), axis=0), 0.0)


def _each(fn, *lists):
    return [fn(*args) for args in zip(*lists)]


def _unit_lower_inverse(a_strict, t_idx, s_idx, bd):
    bdmm = lambda xs, ys: _each(lambda x, y: _mm(x, bd(y)), xs, ys)
    eye = (t_idx == s_idx).astype(F32)
    same8 = (t_idx // 8) == (s_idx // 8)
    a8 = _each(lambda a: jnp.where(same8, a, 0.0), a_strict)
    a8_2 = bdmm(a8, a8)
    a8_34 = bdmm(_each(lambda a, b: jnp.concatenate([a, b], axis=0), a8, a8_2), a8_2)
    inv = _each(lambda a, b, c: eye + a + b + c[:CHUNK], a8, a8_2, a8_34)
    inv = _each(jnp.add, inv, bdmm(inv, _each(lambda c: c[CHUNK:], a8_34)))
    size = 16
    while size <= CHUNK:
        off = ((t_idx // size) == (s_idx // size)) & ((t_idx // (size // 2)) != (s_idx // (size // 2)))
        cross = bdmm(_each(lambda a: jnp.where(off, a, 0.0), a_strict), inv)
        inv = _each(jnp.add, inv, bdmm(inv, cross))
        size *= 2
    return inv


def _chunk_local(r, lw, k, v, kk, kka, tri, t_idx, s_idx, bd_mask):
    bd = lambda y: _block_diag(y, bd_mask)
    bdmm = lambda xs, ys: _each(lambda x, y: _mm(x, bd(y)), xs, ys)
    bdmm2 = lambda xs, ys, zs: _each(lambda x, y, z: _mm(x, jnp.concatenate([bd(y), bd(z)], axis=1)), xs, ys, zs)
    left = lambda xs: _each(lambda x: x[:, :GROUP], xs)
    right = lambda xs: _each(lambda x: x[:, GROUP:], xs)
    top = lambda xs: _each(lambda x: x[:CHUNK], xs)
    bottom = lambda xs: _each(lambda x: x[CHUNK:], xs)

    cum = _each(lambda x: _mm_exact_lhs(tri, x, 3), lw)
    cum_last = _each(lambda c: c[CHUNK - 1:CHUNK, :], cum)
    p_incl = _each(jnp.exp, cum)
    p_excl = _each(lambda c, x: jnp.exp(c - x), cum, lw)
    p_inv = _each(lambda c: jnp.exp(-c), cum)
    to_end = _each(lambda cl, c: jnp.exp(cl - c), cum_last, cum)
    a_t = _each(lambda x, p: -x * p, kk, p_excl)
    r_t = _each(jnp.multiply, r, p_incl)
    b_t = _each(jnp.multiply, kka, p_inv)
    k_t = _each(jnp.multiply, k, p_inv)
    b_end = _each(jnp.multiply, kka, to_end)
    k_end = _each(jnp.multiply, k, to_end)

    ar = _each(lambda a, b: jnp.concatenate([a, b], axis=0), a_t, r_t)
    row2 = lax.broadcasted_iota(jnp.int32, (2 * CHUNK, 2 * GROUP), 0)
    col2 = lax.broadcasted_iota(jnp.int32, (2 * CHUNK, 2 * GROUP), 1) % CHUNK
    causal = col2 < (row2 % CHUNK) + (row2 // CHUNK)
    scores = _each(lambda x, y, z: jnp.where(causal, _mm_nt(x, jnp.concatenate([bd(y), bd(z)], axis=0)), 0.0),
                   ar, b_t, k_t)
    a_ab = top(left(scores))
    a_rb = bottom(left(scores))

    inv = _unit_lower_inverse(a_ab, t_idx, s_idx, bd)
    kv = bdmm(right(scores), v)
    wu = bdmm2(inv, a_t, top(kv))
    ro = bdmm2(a_rb, left(wu), right(wu))
    r_hat = _each(jnp.add, r_t, left(ro))
    o_loc = _each(jnp.add, right(ro), bottom(kv))

    wu_b = _each(lambda x, b: _mm(x.T, b), wu, b_end)
    trans = _each(lambda x: jnp.where(bd_mask, x[:GROUP], 0.0), wu_b)
    d_bd = _each(lambda x, y, z: jnp.where(bd_mask, x[GROUP:] + _mm(y.T, z), 0.0), wu_b, v, k_end)
    d_ls = _each(lambda d: sum(d[h * CHUNK:(h + 1) * CHUNK] for h in range(1, GROUP // CHUNK)) + d[:CHUNK], d_bd)
    p_end = _each(jnp.exp, cum_last)
    return r_hat, o_loc, trans, d_ls, p_end


def _rwkv_kernel(r_ref, lw_ref, k_ref, v_ref, kk_ref, kka_ref, tri_ref, o_ref, state_ref, *, n_sub):
    @pl.when(pl.program_id(1) == 0)
    def _():
        state_ref[...] = jnp.zeros_like(state_ref)

    t_idx = lax.broadcasted_iota(jnp.int32, (CHUNK, GROUP), 0)
    s_idx = lax.broadcasted_iota(jnp.int32, (CHUNK, GROUP), 1) % CHUNK
    bd_mask = (lax.broadcasted_iota(jnp.int32, (GROUP, GROUP), 0) // CHUNK
               == lax.broadcasted_iota(jnp.int32, (GROUP, GROUP), 1) // CHUNK)
    n_group = D_RWKV // GROUP
    where = [(slice(c * CHUNK, (c + 1) * CHUNK), slice(g * GROUP, (g + 1) * GROUP))
             for c in range(n_sub) for g in range(n_group)]
    load = lambda ref: [ref[0, rows, lanes].astype(F32) for rows, lanes in where]
    r_hat, o_loc, trans, d_ls, p_end = _chunk_local(
        load(r_ref), load(lw_ref), load(k_ref), load(v_ref), load(kk_ref), load(kka_ref),
        tri_ref[...], t_idx, s_idx, bd_mask)

    state = [state_ref[g] for g in range(n_group)]
    for c in range(n_sub):
        chains = range(c * n_group, (c + 1) * n_group)
        out = [_mm_nt(r_hat[i], _block_diag(state[g], bd_mask)) + o_loc[i] for g, i in enumerate(chains)]
        for g, i in enumerate(chains):
            rows, lanes = where[i]
            o_ref[0, rows, lanes] = out[g]
        state = [state[g] * p_end[i] + _mm(state[g], trans[i]) + d_ls[i] for g, i in enumerate(chains)]
    for g in range(n_group):
        state_ref[g] = state[g]


def _rwkv_call(r, lw, k, v, kk, kka, tri, tb):
    bsz, seq, _ = r.shape
    spec = pl.BlockSpec((1, tb, D_RWKV), lambda b, t: (b, t, 0))
    return pl.pallas_call(
        functools.partial(_rwkv_kernel, n_sub=tb // CHUNK),
        grid=(bsz, seq // tb),
        in_specs=[spec] * 6 + [pl.BlockSpec(tri.shape, lambda b, t: (0, 0))],
        out_specs=spec,
        out_shape=jax.ShapeDtypeStruct((bsz, seq, D_RWKV), F32),
        scratch_shapes=[pltpu.VMEM((D_RWKV // GROUP, HEAD, GROUP), F32)],
        compiler_params=pltpu.CompilerParams(
            dimension_semantics=("arbitrary", "arbitrary"), vmem_limit_bytes=VMEM_LIMIT),
        name="rwkv_chunk",
    )(r, lw, k, v, kk, kka, tri)


def _route(logits):
    lane = lax.broadcasted_iota(jnp.int32, logits.shape, 1)
    neg = -jnp.inf
    big = ROUTER_LANES
    is_group = (lane >= N_EXPERTS) & (lane < N_EXPERTS + N_GROUPS)
    gl = jnp.where(is_group, logits, neg)
    ge = jnp.exp(gl - jnp.max(gl, axis=-1, keepdims=True))
    gprob = ge / jnp.sum(ge, axis=-1, keepdims=True)
    g_top = jnp.max(gprob, axis=-1, keepdims=True)
    g_idx = jnp.min(jnp.where(is_group & (gprob == g_top), lane - N_EXPERTS, big), axis=-1, keepdims=True)

    in_group = (lane < N_EXPERTS) & ((lane // EXPERTS_PER_GROUP) == g_idx)
    el = jnp.where(in_group, logits, neg)
    top1 = jnp.max(el, axis=-1, keepdims=True)
    idx1 = jnp.min(jnp.where(in_group & (el == top1), lane, big), axis=-1, keepdims=True)
    el2 = jnp.where(lane == idx1, neg, el)
    top2 = jnp.max(el2, axis=-1, keepdims=True)
    idx2 = jnp.min(jnp.where(in_group & (lane != idx1) & (el2 == top2), lane, big), axis=-1, keepdims=True)
    e2 = jnp.exp(top2 - top1)
    denom = 1.0 + e2
    return idx1, idx2, g_top * (1.0 / denom), g_top * (e2 / denom)


def _pack_bf16_pairs(x):
    bits = lax.bitcast_convert_type(_bf(x).astype(F32), jnp.uint32)
    k = x.shape[1] // 2
    return lax.bitcast_convert_type((bits[:, :k] >> 16) | bits[:, k:], jnp.int32)


def _unpack_bf16_pairs(words):
    bits = lax.bitcast_convert_type(words, jnp.uint32)
    lo = lax.bitcast_convert_type(bits << 16, F32)
    hi = lax.bitcast_convert_type(bits & jnp.uint32(0xFFFF0000), F32)
    return _bf(jnp.concatenate([lo, hi], axis=1))


LANE_IDX1, LANE_IDX2, LANE_RANK1, LANE_RANK2, LANE_GATE1, LANE_GATE2 = range(6)


def _out_kernel(o_ref, bonus_ref, gate_ref, yconv_ref, x_ref, ones_ref, lnw_ref, lnb_ref,
                wout_ref, gffn_ref, rw_ref, rb_ref, tri_ref, h_ref, u_ref, route_ref, route_t_ref, count_ref,
                seen_ref, wbf_ref, *, n_split):
    @pl.when(pl.program_id(0) == 0)
    def _():
        seen_ref[...] = jnp.zeros_like(seen_ref)
        wbf_ref[...] = _bf(wout_ref[...])

    sub = o_ref.shape[0] // n_split
    parts = [slice(s * sub, (s + 1) * sub) for s in range(n_split)]
    read = lambda ref: [ref[p, :] for p in parts]
    ones_quad = ones_ref[...]
    inv_n = 1.0 / HEAD
    o = read(o_ref)
    mean = _each(lambda v: _head_sum(v, ones_quad) * inv_n, o)
    cen = _each(jnp.subtract, o, mean)
    var = _each(lambda c: _head_sum(c * c, ones_quad) * inv_n, cen)
    on = _each(lambda c, v: c * lax.rsqrt(v + LN_X_EPS) * lnw_ref[...] + lnb_ref[...], cen, var)
    y_rwkv = _each(lambda a, b, g: (a + b) * g, on, read(bonus_ref), read(gate_ref))
    mixed = _each(lambda yc, yr: (jnp.dot(yc, wbf_ref[:D_CONV, :], preferred_element_type=F32)
                                  + jnp.dot(_bf(yr), wbf_ref[D_CONV:, :], preferred_element_type=F32)),
                  read(yconv_ref), y_rwkv)
    h = _each(jnp.add, read(x_ref), mixed)
    u = _each(lambda v: _rms_norm(v, gffn_ref[...]), h)
    u_hi = _each(_bf, u)
    u_lo = _each(lambda a, b: _bf(a - b.astype(F32)), u, u_hi)
    by_hi = _each(lambda a: jnp.dot(a, rw_ref[...], preferred_element_type=F32), u_hi)
    by_lo = _each(lambda a: jnp.dot(a, rw_ref[:, :ROUTER_LANES], preferred_element_type=F32), u_lo)
    logits = _each(lambda a, b: a[:, :ROUTER_LANES] + a[:, ROUTER_LANES:] + b + rb_ref[...], by_hi, by_lo)
    routed = _each(_route, logits)

    lane = lax.broadcasted_iota(jnp.int32, (sub, ROUTER_LANES), 1)
    hit1 = _each(lambda rt: lane == rt[0], routed)
    hit2 = _each(lambda rt: lane == rt[1], routed)
    both = _each(lambda a, b: a.astype(F32) + b.astype(F32), hit1, hit2)
    inside = _each(lambda b: jnp.dot(tri_ref[...], _bf(b), preferred_element_type=F32), both)
    seen = seen_ref[...]
    for s, p in enumerate(parts):
        idx1, idx2, gate1, gate2 = routed[s]
        before = inside[s] + seen
        rank1 = jnp.sum(jnp.where(hit1[s], before, 0.0), axis=-1, keepdims=True)
        rank2 = jnp.sum(jnp.where(hit2[s], before, 0.0), axis=-1, keepdims=True)
        seen = seen + jnp.sum(both[s], axis=0, keepdims=True)
        route = jnp.zeros((sub, ROUTER_LANES), F32)
        for lane_id, col in ((LANE_IDX1, idx1.astype(F32)), (LANE_IDX2, idx2.astype(F32)), (LANE_RANK1, rank1),
                             (LANE_RANK2, rank2), (LANE_GATE1, gate1), (LANE_GATE2, gate2)):
            route = jnp.where(lane == lane_id, col, route)
        route_ref[p, :] = route
        route_t_ref[:, p] = route.T[:8, :]
        h_ref[p, :] = _bf(h[s])
        u_ref[p, :] = _pack_bf16_pairs(u[s])
    seen_ref[...] = seen
    count_ref[...] = jnp.broadcast_to(seen, count_ref.shape)


def _out_call(o, bonus, gate, yconv, x, ones_bd, ln_w, ln_b, w_out, g_ffn, router_w, router_b, tm, n_split):
    n_tok = x.shape[0]
    row = lambda width: pl.BlockSpec((tm, width), lambda i: (i, 0))
    full = lambda arr: pl.BlockSpec(arr.shape, lambda i: (0,) * arr.ndim, pipeline_mode=pl.Buffered(1))
    sub = tm // n_split
    tri_strict = (jnp.arange(sub)[:, None] > jnp.arange(sub)[None, :]).astype(BF16)
    params = (ones_bd, ln_w, ln_b, w_out, g_ffn, router_w, router_b, tri_strict)
    return pl.pallas_call(
        functools.partial(_out_kernel, n_split=n_split),
        grid=(n_tok // tm,),
        in_specs=[row(D_RWKV)] * 4 + [row(D_MODEL)] + [full(p) for p in params],
        out_specs=[row(D_MODEL), row(D_MODEL // 2), row(ROUTER_LANES),
                   pl.BlockSpec((8, tm), lambda i: (0, i)),
                   pl.BlockSpec((8, ROUTER_LANES), lambda i: (0, 0))],
        out_shape=[jax.ShapeDtypeStruct((n_tok, D_MODEL), BF16),
                   jax.ShapeDtypeStruct((n_tok, D_MODEL // 2), jnp.int32),
                   jax.ShapeDtypeStruct((n_tok, ROUTER_LANES), F32),
                   jax.ShapeDtypeStruct((8, n_tok), F32),
                   jax.ShapeDtypeStruct((8, ROUTER_LANES), F32)],
        scratch_shapes=[pltpu.VMEM((1, ROUTER_LANES), F32), pltpu.VMEM(w_out.shape, BF16)],
        compiler_params=pltpu.CompilerParams(
            dimension_semantics=("arbitrary",), vmem_limit_bytes=VMEM_LIMIT),
        name="out_proj_route",
    )(o, bonus, gate, yconv, x, *params)


SC_CORES = 2
SC_SUBCORES = 16
SC_ROWS = 64


def _sc_mesh():
    return plsc.VectorSubcoreMesh(core_axis_name="c", subcore_axis_name="s",
                                  num_cores=SC_CORES, num_subcores=SC_SUBCORES)


def _sc_worker():
    return lax.axis_index("s") * SC_CORES + lax.axis_index("c")


def _sc_gather(table, idx):
    n_rows = idx.shape[0]
    width = table.shape[1]
    n_chunks = n_rows // (SC_CORES * SC_SUBCORES * SC_ROWS)

    def body(table_hbm, idx_hbm, out_hbm, idx_v, rows_v, gather_sem, write_sem):
        first = _sc_worker() * n_chunks
        pltpu.sync_copy(idx_hbm.at[pl.ds(first, n_chunks)], idx_v)
        gather = lambda j: pltpu.async_copy(table_hbm.at[idx_v.at[j]], rows_v.at[j % 2], gather_sem.at[j % 2])
        gathers = [gather(0)]
        writes = []
        for j in range(n_chunks):
            gathers[j].wait()
            if j + 1 < n_chunks:
                if j >= 1:
                    writes[j - 1].wait()
                gathers.append(gather(j + 1))
            dst = out_hbm.at[pl.ds(pl.multiple_of((first + j) * SC_ROWS, SC_ROWS), SC_ROWS)]
            writes.append(pltpu.async_copy(rows_v.at[j % 2], dst, write_sem.at[j % 2]))
        for j in range(max(n_chunks - 2, 0), n_chunks):
            writes[j].wait()

    return pl.kernel(
        body,
        out_type=jax.ShapeDtypeStruct((n_rows, width), table.dtype),
        mesh=_sc_mesh(),
        scratch_types=[pltpu.VMEM((n_chunks, SC_ROWS), jnp.int32), pltpu.VMEM((2, SC_ROWS, width), table.dtype),
                       pltpu.SemaphoreType.DMA((2,)), pltpu.SemaphoreType.DMA((2,))],
        name="sc_row_gather",
    )(table, idx.reshape(n_rows // SC_ROWS, SC_ROWS))


def _sc_scatter(rows, pos, n_out):
    n_rows, width = rows.shape
    n_slots = pos.shape[0] // n_rows
    slot_chunks = n_rows // SC_ROWS
    n_chunks = slot_chunks // (SC_CORES * SC_SUBCORES)

    def body(rows_hbm, pos_hbm, out_hbm, idx_v, rows_v, read_sem, scatter_sem):
        first = _sc_worker() * n_chunks
        for s in range(n_slots):
            pltpu.sync_copy(pos_hbm.at[pl.ds(s * slot_chunks + first, n_chunks)], idx_v.at[s])
        read = lambda j: pltpu.async_copy(
            rows_hbm.at[pl.ds(pl.multiple_of((first + j) * SC_ROWS, SC_ROWS), SC_ROWS)],
            rows_v.at[j % 2], read_sem.at[j % 2])
        reads = [read(0)]
        scatters = []
        for j in range(n_chunks):
            reads[j].wait()
            if j + 1 < n_chunks:
                if j >= 1:
                    for copy in scatters[j - 1]:
                        copy.wait()
                reads.append(read(j + 1))
            scatters.append([pltpu.async_copy(rows_v.at[j % 2], out_hbm.at[idx_v.at[s, j]], scatter_sem.at[j % 2])
                             for s in range(n_slots)])
        for j in range(max(n_chunks - 2, 0), n_chunks):
            for copy in scatters[j]:
                copy.wait()

    return pl.kernel(
        body,
        out_type=jax.ShapeDtypeStruct((n_out, width), rows.dtype),
        mesh=_sc_mesh(),
        scratch_types=[pltpu.VMEM((n_slots, n_chunks, SC_ROWS), jnp.int32),
                       pltpu.VMEM((2, SC_ROWS, width), rows.dtype),
                       pltpu.SemaphoreType.DMA((2,)), pltpu.SemaphoreType.DMA((2,))],
        name="sc_row_scatter",
    )(rows, pos.reshape(n_slots * slot_chunks, SC_ROWS))


def _expert_kernel(tile_expert_ref, n_valid_ref, x_ref, wg_ref, wu_ref, wd_ref, y_ref):
    del tile_expert_ref

    @pl.when(pl.program_id(0) < n_valid_ref[0])
    def _():
        x = _unpack_bf16_pairs(x_ref[...])
        gate = jnp.dot(x, _bf(wg_ref[0]), preferred_element_type=F32)
        up = jnp.dot(x, _bf(wu_ref[0]), preferred_element_type=F32)
        hid = gate * jax.nn.sigmoid(gate) * up
        y_ref[...] = _pack_bf16_pairs(jnp.dot(_bf(hid), _bf(wd_ref[0]), preferred_element_type=F32))


def _expert_call(tile_expert, n_valid, x_sorted, w_gate, w_up, w_down, tm):
    n_rows = x_sorted.shape[0]
    rows = pl.BlockSpec((tm, D_MODEL // 2), lambda i, te, nv: (jnp.minimum(i, nv[0] - 1), 0))
    by_expert = lambda i, te, nv: (te[i], 0, 0)
    return pl.pallas_call(
        _expert_kernel,
        grid_spec=pltpu.PrefetchScalarGridSpec(
            num_scalar_prefetch=2,
            grid=(n_rows // tm,),
            in_specs=[rows,
                      pl.BlockSpec((1, D_MODEL, D_EXPERT), by_expert),
                      pl.BlockSpec((1, D_MODEL, D_EXPERT), by_expert),
                      pl.BlockSpec((1, D_EXPERT, D_MODEL), by_expert)],
            out_specs=rows),
        out_shape=jax.ShapeDtypeStruct((n_rows, D_MODEL // 2), jnp.int32),
        compiler_params=pltpu.CompilerParams(
            dimension_semantics=("arbitrary",), vmem_limit_bytes=VMEM_LIMIT),
        name="moe_experts",
    )(tile_expert, n_valid, x_sorted, w_gate, w_up, w_down)


def _final_kernel(h_ref, y1_ref, y2_ref, route_ref, gfin_ref, *rest):
    out_ref = rest[-1]
    route = route_ref[...]
    gate1 = route[:, LANE_GATE1:LANE_GATE1 + 1]
    gate2 = route[:, LANE_GATE2:LANE_GATE2 + 1]
    moe = gate1 * _unpack_bf16_pairs(y1_ref[...]).astype(F32) + gate2 * _unpack_bf16_pairs(y2_ref[...]).astype(F32)
    out_ref[...] = _rms_norm(h_ref[...] + moe, gfin_ref[...])


def _final_call(h, y_pairs, route, g_final, earlier, chunk, n_chunks, tm):
    n_tok = h.shape[0]
    n_blocks = n_tok // n_chunks // tm
    first = chunk * n_blocks
    in_specs = [pl.BlockSpec((tm, D_MODEL), lambda i: (first + i, 0)),
                pl.BlockSpec((tm, D_MODEL // 2), lambda i: (i, 0)),
                pl.BlockSpec((tm, D_MODEL // 2), lambda i: (i + n_blocks, 0)),
                pl.BlockSpec((tm, ROUTER_LANES), lambda i: (first + i, 0)),
                pl.BlockSpec((1, D_MODEL), lambda i: (0, 0))]
    args = [h, y_pairs, y_pairs, route, g_final]
    aliases = {}
    if earlier is not None:
        in_specs.append(pl.BlockSpec(memory_space=pl.ANY))
        args.append(earlier)
        aliases = {len(args) - 1: 0}
    return pl.pallas_call(
        _final_kernel,
        grid=(n_blocks,),
        in_specs=in_specs,
        out_specs=pl.BlockSpec((tm, D_MODEL), lambda i: (first + i, 0)),
        out_shape=jax.ShapeDtypeStruct((n_tok, D_MODEL), F32),
        input_output_aliases=aliases,
        compiler_params=pltpu.CompilerParams(
            dimension_semantics=("arbitrary",), vmem_limit_bytes=VMEM_LIMIT),
        name="moe_combine_norm",
    )(*args)


def _dispatch_plan(route_t, counts, tm):
    n_tok = route_t.shape[1]
    n_tiles = (2 * n_tok) // tm + N_EXPERTS
    counts = counts[0, :N_EXPERTS].astype(jnp.int32)
    tiles_per = (counts + tm - 1) // tm
    tile_end = jnp.cumsum(tiles_per)
    row_start = (tile_end - tiles_per) * tm
    experts = jnp.arange(N_EXPERTS, dtype=jnp.int32)

    def position(idx_lane, rank_lane):
        idx = route_t[idx_lane].astype(jnp.int32)
        start = jnp.sum(jnp.where(idx[None, :] == experts[:, None], row_start[:, None], 0), axis=0)
        return start + route_t[rank_lane].astype(jnp.int32)

    pos = jnp.concatenate([position(LANE_IDX1, LANE_RANK1), position(LANE_IDX2, LANE_RANK2)])
    n_valid = tile_end[-1:]
    tile = jnp.minimum(jnp.arange(n_tiles, dtype=jnp.int32), n_valid - 1)
    tile_expert = jnp.sum((tile_end[None, :] <= tile[:, None]).astype(jnp.int32), axis=1)
    return pos, n_tiles, tile_expert, n_valid


def _block(x, norm_mix_g, w_in, rwkv_mu, conv_w, decay_up, decay_base, aaa_up, aaa_base, gate_up,
           k_k, k_a, r_k, ln_x_w, ln_x_b, w_out, norm_ffn_g, router_group_w, router_group_b,
           router_expert_w, router_expert_b, expert_w_gate, expert_w_up, expert_w_down, norm_final_g,
           *, tm_in, tb_rwkv, tm_out, tm_expert, tm_final):
    bsz, seq, d_model = x.shape
    n_tok = bsz * seq
    row = lambda p: p.reshape(1, -1)
    half = LORA_WA // 2
    zeros = jnp.zeros((half, D_RWKV), F32)
    dup = _bf(jnp.concatenate([decay_up, zeros], axis=0))
    aup = _bf(jnp.concatenate([zeros, aaa_up], axis=0))
    head_of = jnp.arange(QUAD) // HEAD
    ones_quad = (head_of[:, None] == head_of[None, :]).astype(BF16)
    tri = (jnp.arange(CHUNK)[:, None] >= jnp.arange(CHUNK)[None, :]).astype(BF16)

    yconv, r, lw, k2, v, kk, kka, gate, bonus = _in_call(
        x, row(norm_mix_g), w_in, row(rwkv_mu), conv_w, dup, row(decay_base), aup, row(aaa_base),
        _bf(gate_up), row(k_k), row(k_a), row(r_k), ones_quad, tm_in)
    o = _rwkv_call(r, lw, k2, v, kk, kka, tri, tb_rwkv)

    pad = ROUTER_LANES - N_EXPERTS - N_GROUPS
    router_w = jnp.concatenate([router_expert_w, router_group_w, jnp.zeros((d_model, pad), F32)], axis=1)
    router_hi = _bf(router_w)
    router_split = jnp.concatenate([router_hi, _bf(router_w - router_hi.astype(F32))], axis=1)
    router_b = jnp.concatenate([router_expert_b, router_group_b, jnp.zeros((pad,), F32)]).reshape(1, -1)
    flat = lambda t: t.reshape(n_tok, t.shape[-1])
    h, u_pairs, route, route_t, counts = _out_call(
        flat(o), flat(bonus), flat(gate), flat(yconv), flat(x), ones_quad,
        row(ln_x_w), row(ln_x_b), w_out, row(norm_ffn_g), router_split, router_b, tm_out, OUT_SPLIT)

    pos, n_tiles, tile_expert, n_valid = _dispatch_plan(route_t, counts, tm_expert)
    x_sorted = _sc_scatter(u_pairs, pos, n_tiles * tm_expert)
    y_sorted = _expert_call(tile_expert, n_valid, x_sorted, expert_w_gate, expert_w_up, expert_w_down, tm_expert)
    per_chunk = n_tok // COMBINE_CHUNKS
    out = None
    for c in range(COMBINE_CHUNKS):
        lo = c * per_chunk
        pos_c = jnp.concatenate([pos[lo:lo + per_chunk], pos[n_tok + lo:n_tok + lo + per_chunk]])
        out = _final_call(h, _sc_gather(y_sorted, pos_c), route, row(norm_final_g), out, c, COMBINE_CHUNKS, tm_final)
    return out.reshape(bsz, seq, d_model)


def kernel(x, norm_mix_g, w_in, rwkv_mu, conv_w, decay_up, decay_base, aaa_up, aaa_base, gate_up, k_k, k_a, r_k, ln_x_w, ln_x_b, w_out, norm_ffn_g, router_group_w, router_group_b, router_expert_w, router_expert_b, expert_w_gate, expert_w_up, expert_w_down, norm_final_g):
    return _block(x, norm_mix_g[0], w_in[0], rwkv_mu[0], conv_w[0], decay_up[0], decay_base[0],
                  aaa_up[0], aaa_base[0], gate_up[0], k_k[0], k_a[0], r_k[0].reshape(-1), ln_x_w[0],
                  ln_x_b[0], w_out[0], norm_ffn_g[0], router_group_w[0], router_group_b[0],
                  router_expert_w[0], router_expert_b[0], expert_w_gate[0], expert_w_up[0],
                  expert_w_down[0], norm_final_g,
                  tm_in=512, tb_rwkv=256, tm_out=512, tm_expert=512, tm_final=512)
```

```python
import functools

import jax
import jax.numpy as jnp
from jax import lax
from jax.experimental import pallas as pl
from jax.experimental.pallas import tpu as pltpu
from jax.experimental.pallas import tpu_sc as plsc

F32 = jnp.float32
BF16 = jnp.bfloat16

D_MODEL = 1024
D_CONV = 512
CONV_WIDTH = 3
N_HEADS = 8
HEAD = 64
D_RWKV = N_HEADS * HEAD
LORA_WA = 128
GATE_LORA = 128
D_RWKV_PROJ = 3 * D_RWKV + LORA_WA + GATE_LORA
D_IN = 3 * D_CONV + D_RWKV_PROJ
N_GROUPS = 4
EXPERTS_PER_GROUP = 8
N_EXPERTS = N_GROUPS * EXPERTS_PER_GROUP
D_EXPERT = D_MODEL // 4
RMS_EPS = 1e-6
LN_X_EPS = 64e-5
L2_EPS = 1e-12

LANES = 128
CHUNK = 64
QUAD = 4 * HEAD
GROUP = 2 * HEAD
OUT_SPLIT = 4
COMBINE_CHUNKS = 4
ROUTER_LANES = 128

VMEM_LIMIT = 56 * 1024 * 1024


def _bf(x):
    return x.astype(BF16)


def _mm(a, b):
    return jnp.dot(_bf(a), _bf(b), preferred_element_type=F32)


def _mm_nt(a, b):
    return lax.dot_general(_bf(a), _bf(b), (((1,), (1,)), ((), ())), preferred_element_type=F32)


def _mm_exact_lhs(lhs_bf16, x, passes):
    acc = None
    rem = x
    for _ in range(passes):
        piece = _bf(rem)
        part = jnp.dot(lhs_bf16, piece, preferred_element_type=F32)
        acc = part if acc is None else acc + part
        rem = rem - piece.astype(F32)
    return acc


def _head_sum(x, ones_quad):
    xb = _bf(x)
    return jnp.concatenate(
        [jnp.dot(xb[:, q * QUAD:(q + 1) * QUAD], ones_quad, preferred_element_type=F32)
         for q in range(x.shape[1] // QUAD)], axis=1)


def _rms_norm(x, g):
    return x * lax.rsqrt(jnp.mean(x * x, axis=-1, keepdims=True) + RMS_EPS) * g


def _shift_rows(cur, prev_rows, k):
    rolled = pltpu.roll(cur, k, 0)
    prev_rolled = pltpu.roll(prev_rows, k, 0)
    n = cur.shape[0]
    head = jnp.concatenate([prev_rolled, rolled[8:]], axis=0) if n > 8 else prev_rolled
    row = lax.broadcasted_iota(jnp.int32, cur.shape, 0)
    return jnp.where(row < k, head, rolled)


def _in_kernel(x_ref, g_ref, w_ref, mu_ref, convw_ref, dup_ref, dbase_ref, aup_ref, abase_ref,
               gup_ref, kk_ref, ka_ref, rk_ref, ones_ref,
               yconv_ref, r_ref, lw_ref, k_ref, v_ref, kkn_ref, kka_ref, gate_ref, bonus_ref,
               carry_ref, wbf_ref):
    @pl.when((pl.program_id(0) == 0) & (pl.program_id(1) == 0))
    def _():
        wbf_ref[...] = _bf(w_ref[...])

    @pl.when(pl.program_id(1) == 0)
    def _():
        carry_ref[...] = jnp.zeros_like(carry_ref)

    u = _bf(_rms_norm(x_ref[0], g_ref[...]))
    n_conv = 3 * D_CONV
    rk0 = n_conv
    bounds = dict(conv=(0, n_conv), lora=(rk0 + 3 * D_RWKV, D_IN), k=(rk0 + D_RWKV, rk0 + 2 * D_RWKV),
                  r=(rk0, rk0 + D_RWKV), v=(rk0 + 2 * D_RWKV, rk0 + 3 * D_RWKV))
    z = {name: jnp.dot(u, wbf_ref[:, lo:hi], preferred_element_type=F32) for name, (lo, hi) in bounds.items()}

    def lerp(name):
        lo, hi = bounds[name]
        cur = z[name]
        prev = carry_ref[:, D_CONV + lo - n_conv:D_CONV + hi - n_conv]
        mixed = cur + (_shift_rows(cur, prev, 1) - cur) * mu_ref[:, lo - n_conv:hi - n_conv]
        carry_ref[:, D_CONV + lo - n_conv:D_CONV + hi - n_conv] = cur[-8:]
        return mixed

    b_gate = z["conv"][:, :D_CONV]
    ch = z["conv"][:, D_CONV:2 * D_CONV] * z["conv"][:, 2 * D_CONV:]
    prev_ch = carry_ref[:, :D_CONV]
    conv = (convw_ref[2:3, :] * ch
            + convw_ref[1:2, :] * _shift_rows(ch, prev_ch, 1)
            + convw_ref[0:1, :] * _shift_rows(ch, prev_ch, 2))
    yconv_ref[0] = _bf(b_gate * conv)
    carry_ref[:, :D_CONV] = ch[-8:]

    lora = lerp("lora")
    wa_lo = lora[:, :LORA_WA]
    g_lo = lora[:, LORA_WA:]
    dec_in = -(dbase_ref[...] + jnp.dot(_bf(jnp.tanh(wa_lo)), dup_ref[...], preferred_element_type=F32))
    softplus = jnp.maximum(dec_in, 0.0) + jnp.log(1.0 + jnp.exp(-jnp.abs(dec_in)))
    w = -softplus - 0.5
    lw_ref[0] = -jnp.exp(w)
    a = jax.nn.sigmoid(abase_ref[...] + jnp.dot(_bf(wa_lo), aup_ref[...], preferred_element_type=F32))
    gate_ref[0] = jnp.dot(_bf(jax.nn.sigmoid(g_lo)), gup_ref[...], preferred_element_type=F32)

    ones_quad = ones_ref[...]
    k = lerp("k")
    kk = k * kk_ref[...]
    norm = jnp.sqrt(_head_sum(kk * kk, ones_quad))
    kk = kk / jnp.maximum(norm, L2_EPS)
    k2 = k * (1.0 + (a - 1.0) * ka_ref[...])
    k_ref[0] = _bf(k2)
    kkn_ref[0] = _bf(kk)
    kka_ref[0] = _bf(kk * a)
    r = lerp("r")
    r_ref[0] = _bf(r)
    v = lerp("v")
    v_ref[0] = _bf(v)
    bonus_ref[0] = _head_sum(r * k2 * rk_ref[...], ones_quad) * v


def _in_call(x, g, w_in, mu, conv_w, dup, dbase, aup, abase, gup, k_k, k_a, r_k, ones_quad, tm):
    bsz, seq, _ = x.shape
    full = lambda arr: pl.BlockSpec(arr.shape, lambda b, t: (0,) * arr.ndim, pipeline_mode=pl.Buffered(1))
    out_spec = pl.BlockSpec((1, tm, D_RWKV), lambda b, t: (b, t, 0))
    out_dtypes = (BF16, BF16, F32, BF16, BF16, BF16, BF16, F32, F32)
    params = (g, w_in, mu, conv_w, dup, dbase, aup, abase, gup, k_k, k_a, r_k, ones_quad)
    return pl.pallas_call(
        _in_kernel,
        grid=(bsz, seq // tm),
        in_specs=[pl.BlockSpec((1, tm, D_MODEL), lambda b, t: (b, t, 0))] + [full(p) for p in params],
        out_specs=[out_spec] * 9,
        out_shape=[jax.ShapeDtypeStruct((bsz, seq, D_RWKV), dt) for dt in out_dtypes],
        scratch_shapes=[pltpu.VMEM((8, D_CONV + D_RWKV_PROJ), F32), pltpu.VMEM(w_in.shape, BF16)],
        compiler_params=pltpu.CompilerParams(
            dimension_semantics=("arbitrary", "arbitrary"), vmem_limit_bytes=VMEM_LIMIT),
        name="in_proj",
    )(x, *params)


def _block_diag(y, bd_mask):
    return jnp.where(bd_mask, jnp.concatenate([y] * (GROUP // CHUNK), axis=0), 0.0)


def _each(fn, *lists):
    return [fn(*args) for args in zip(*lists)]


def _unit_lower_inverse(a_strict, t_idx, s_idx, bd):
    bdmm = lambda xs, ys: _each(lambda x, y: _mm(x, bd(y)), xs, ys)
    eye = (t_idx == s_idx).astype(F32)
    same8 = (t_idx // 8) == (s_idx // 8)
    a8 = _each(lambda a: jnp.where(same8, a, 0.0), a_strict)
    a8_2 = bdmm(a8, a8)
    a8_34 = bdmm(_each(lambda a, b: jnp.concatenate([a, b], axis=0), a8, a8_2), a8_2)
    inv = _each(lambda a, b, c: eye + a + b + c[:CHUNK], a8, a8_2, a8_34)
    inv = _each(jnp.add, inv, bdmm(inv, _each(lambda c: c[CHUNK:], a8_34)))
    size = 16
    while size <= CHUNK:
        off = ((t_idx // size) == (s_idx // size)) & ((t_idx // (size // 2)) != (s_idx // (size // 2)))
        cross = bdmm(_each(lambda a: jnp.where(off, a, 0.0), a_strict), inv)
        inv = _each(jnp.add, inv, bdmm(inv, cross))
        size *= 2
    return inv


def _chunk_local(r, lw, k, v, kk, kka, tri, t_idx, s_idx, bd_mask):
    bd = lambda y: _block_diag(y, bd_mask)
    bdmm = lambda xs, ys: _each(lambda x, y: _mm(x, bd(y)), xs, ys)
    bdmm2 = lambda xs, ys, zs: _each(lambda x, y, z: _mm(x, jnp.concatenate([bd(y), bd(z)], axis=1)), xs, ys, zs)
    left = lambda xs: _each(lambda x: x[:, :GROUP], xs)
    right = lambda xs: _each(lambda x: x[:, GROUP:], xs)
    top = lambda xs: _each(lambda x: x[:CHUNK], xs)
    bottom = lambda xs: _each(lambda x: x[CHUNK:], xs)

    cum = _each(lambda x: _mm_exact_lhs(tri, x, 3), lw)
    cum_last = _each(lambda c: c[CHUNK - 1:CHUNK, :], cum)
    p_incl = _each(jnp.exp, cum)
    p_excl = _each(lambda c, x: jnp.exp(c - x), cum, lw)
    p_inv = _each(lambda c: jnp.exp(-c), cum)
    to_end = _each(lambda cl, c: jnp.exp(cl - c), cum_last, cum)
    a_t = _each(lambda x, p: -x * p, kk, p_excl)
    r_t = _each(jnp.multiply, r, p_incl)
    b_t = _each(jnp.multiply, kka, p_inv)
    k_t = _each(jnp.multiply, k, p_inv)
    b_end = _each(jnp.multiply, kka, to_end)
    k_end = _each(jnp.multiply, k, to_end)

    ar = _each(lambda a, b: jnp.concatenate([a, b], axis=0), a_t, r_t)
    row2 = lax.broadcasted_iota(jnp.int32, (2 * CHUNK, 2 * GROUP), 0)
    col2 = lax.broadcasted_iota(jnp.int32, (2 * CHUNK, 2 * GROUP), 1) % CHUNK
    causal = col2 < (row2 % CHUNK) + (row2 // CHUNK)
    scores = _each(lambda x, y, z: jnp.where(causal, _mm_nt(x, jnp.concatenate([bd(y), bd(z)], axis=0)), 0.0),
                   ar, b_t, k_t)
    a_ab = top(left(scores))
    a_rb = bottom(left(scores))

    inv = _unit_lower_inverse(a_ab, t_idx, s_idx, bd)
    kv = bdmm(right(scores), v)
    wu = bdmm2(inv, a_t, top(kv))
    ro = bdmm2(a_rb, left(wu), right(wu))
    r_hat = _each(jnp.add, r_t, left(ro))
    o_loc = _each(jnp.add, right(ro), bottom(kv))

    wu_b = _each(lambda x, b: _mm(x.T, b), wu, b_end)
    trans = _each(lambda x: jnp.where(bd_mask, x[:GROUP], 0.0), wu_b)
    d_bd = _each(lambda x, y, z: jnp.where(bd_mask, x[GROUP:] + _mm(y.T, z), 0.0), wu_b, v, k_end)
    d_ls = _each(lambda d: sum(d[h * CHUNK:(h + 1) * CHUNK] for h in range(1, GROUP // CHUNK)) + d[:CHUNK], d_bd)
    p_end = _each(jnp.exp, cum_last)
    return r_hat, o_loc, trans, d_ls, p_end


def _rwkv_kernel(r_ref, lw_ref, k_ref, v_ref, kk_ref, kka_ref, tri_ref, o_ref, state_ref, *, n_sub):
    @pl.when(pl.program_id(1) == 0)
    def _():
        state_ref[...] = jnp.zeros_like(state_ref)

    t_idx = lax.broadcasted_iota(jnp.int32, (CHUNK, GROUP), 0)
    s_idx = lax.broadcasted_iota(jnp.int32, (CHUNK, GROUP), 1) % CHUNK
    bd_mask = (lax.broadcasted_iota(jnp.int32, (GROUP, GROUP), 0) // CHUNK
               == lax.broadcasted_iota(jnp.int32, (GROUP, GROUP), 1) // CHUNK)
    n_group = D_RWKV // GROUP
    where = [(slice(c * CHUNK, (c + 1) * CHUNK), slice(g * GROUP, (g + 1) * GROUP))
             for c in range(n_sub) for g in range(n_group)]
    load = lambda ref: [ref[0, rows, lanes].astype(F32) for rows, lanes in where]
    r_hat, o_loc, trans, d_ls, p_end = _chunk_local(
        load(r_ref), load(lw_ref), load(k_ref), load(v_ref), load(kk_ref), load(kka_ref),
        tri_ref[...], t_idx, s_idx, bd_mask)

    state = [state_ref[g] for g in range(n_group)]
    for c in range(n_sub):
        chains = range(c * n_group, (c + 1) * n_group)
        out = [_mm_nt(r_hat[i], _block_diag(state[g], bd_mask)) + o_loc[i] for g, i in enumerate(chains)]
        for g, i in enumerate(chains):
            rows, lanes = where[i]
            o_ref[0, rows, lanes] = out[g]
        state = [state[g] * p_end[i] + _mm(state[g], trans[i]) + d_ls[i] for g, i in enumerate(chains)]
    for g in range(n_group):
        state_ref[g] = state[g]


def _rwkv_call(r, lw, k, v, kk, kka, tri, tb):
    bsz, seq, _ = r.shape
    spec = pl.BlockSpec((1, tb, D_RWKV), lambda b, t: (b, t, 0))
    return pl.pallas_call(
        functools.partial(_rwkv_kernel, n_sub=tb // CHUNK),
        grid=(bsz, seq // tb),
        in_specs=[spec] * 6 + [pl.BlockSpec(tri.shape, lambda b, t: (0, 0))],
        out_specs=spec,
        out_shape=jax.ShapeDtypeStruct((bsz, seq, D_RWKV), F32),
        scratch_shapes=[pltpu.VMEM((D_RWKV // GROUP, HEAD, GROUP), F32)],
        compiler_params=pltpu.CompilerParams(
            dimension_semantics=("arbitrary", "arbitrary"), vmem_limit_bytes=VMEM_LIMIT),
        name="rwkv_chunk",
    )(r, lw, k, v, kk, kka, tri)


def _route(logits):
    lane = lax.broadcasted_iota(jnp.int32, logits.shape, 1)
    neg = -jnp.inf
    big = ROUTER_LANES
    is_group = (lane >= N_EXPERTS) & (lane < N_EXPERTS + N_GROUPS)
    gl = jnp.where(is_group, logits, neg)
    ge = jnp.exp(gl - jnp.max(gl, axis=-1, keepdims=True))
    gprob = ge / jnp.sum(ge, axis=-1, keepdims=True)
    g_top = jnp.max(gprob, axis=-1, keepdims=True)
    g_idx = jnp.min(jnp.where(is_group & (gprob == g_top), lane - N_EXPERTS, big), axis=-1, keepdims=True)

    in_group = (lane < N_EXPERTS) & ((lane // EXPERTS_PER_GROUP) == g_idx)
    el = jnp.where(in_group, logits, neg)
    top1 = jnp.max(el, axis=-1, keepdims=True)
    idx1 = jnp.min(jnp.where(in_group & (el == top1), lane, big), axis=-1, keepdims=True)
    el2 = jnp.where(lane == idx1, neg, el)
    top2 = jnp.max(el2, axis=-1, keepdims=True)
    idx2 = jnp.min(jnp.where(in_group & (lane != idx1) & (el2 == top2), lane, big), axis=-1, keepdims=True)
    e2 = jnp.exp(top2 - top1)
    denom = 1.0 + e2
    return idx1, idx2, g_top * (1.0 / denom), g_top * (e2 / denom)


def _pack_bf16_pairs(x):
    bits = lax.bitcast_convert_type(_bf(x).astype(F32), jnp.uint32)
    k = x.shape[1] // 2
    return lax.bitcast_convert_type((bits[:, :k] >> 16) | bits[:, k:], jnp.int32)


def _unpack_bf16_pairs(words):
    bits = lax.bitcast_convert_type(words, jnp.uint32)
    lo = lax.bitcast_convert_type(bits << 16, F32)
    hi = lax.bitcast_convert_type(bits & jnp.uint32(0xFFFF0000), F32)
    return _bf(jnp.concatenate([lo, hi], axis=1))


LANE_IDX1, LANE_IDX2, LANE_RANK1, LANE_RANK2, LANE_GATE1, LANE_GATE2 = range(6)


def _out_kernel(o_ref, bonus_ref, gate_ref, yconv_ref, x_ref, ones_ref, lnw_ref, lnb_ref,
                wout_ref, gffn_ref, rw_ref, rb_ref, tri_ref, h_ref, u_ref, route_ref, route_t_ref, count_ref,
                seen_ref, wbf_ref, *, n_split):
    @pl.when(pl.program_id(0) == 0)
    def _():
        seen_ref[...] = jnp.zeros_like(seen_ref)
        wbf_ref[...] = _bf(wout_ref[...])

    sub = o_ref.shape[0] // n_split
    parts = [slice(s * sub, (s + 1) * sub) for s in range(n_split)]
    read = lambda ref: [ref[p, :] for p in parts]
    ones_quad = ones_ref[...]
    inv_n = 1.0 / HEAD
    o = read(o_ref)
    mean = _each(lambda v: _head_sum(v, ones_quad) * inv_n, o)
    cen = _each(jnp.subtract, o, mean)
    var = _each(lambda c: _head_sum(c * c, ones_quad) * inv_n, cen)
    on = _each(lambda c, v: c * lax.rsqrt(v + LN_X_EPS) * lnw_ref[...] + lnb_ref[...], cen, var)
    y_rwkv = _each(lambda a, b, g: (a + b) * g, on, read(bonus_ref), read(gate_ref))
    mixed = _each(lambda yc, yr: (jnp.dot(yc, wbf_ref[:D_CONV, :], preferred_element_type=F32)
                                  + jnp.dot(_bf(yr), wbf_ref[D_CONV:, :], preferred_element_type=F32)),
                  read(yconv_ref), y_rwkv)
    h = _each(jnp.add, read(x_ref), mixed)
    u = _each(lambda v: _rms_norm(v, gffn_ref[...]), h)
    u_hi = _each(_bf, u)
    u_lo = _each(lambda a, b: _bf(a - b.astype(F32)), u, u_hi)
    by_hi = _each(lambda a: jnp.dot(a, rw_ref[...], preferred_element_type=F32), u_hi)
    by_lo = _each(lambda a: jnp.dot(a, rw_ref[:, :ROUTER_LANES], preferred_element_type=F32), u_lo)
    logits = _each(lambda a, b: a[:, :ROUTER_LANES] + a[:, ROUTER_LANES:] + b + rb_ref[...], by_hi, by_lo)
    routed = _each(_route, logits)

    lane = lax.broadcasted_iota(jnp.int32, (sub, ROUTER_LANES), 1)
    hit1 = _each(lambda rt: lane == rt[0], routed)
    hit2 = _each(lambda rt: lane == rt[1], routed)
    both = _each(lambda a, b: a.astype(F32) + b.astype(F32), hit1, hit2)
    inside = _each(lambda b: jnp.dot(tri_ref[...], _bf(b), preferred_element_type=F32), both)
    seen = seen_ref[...]
    for s, p in enumerate(parts):
        idx1, idx2, gate1, gate2 = routed[s]
        before = inside[s] + seen
        rank1 = jnp.sum(jnp.where(hit1[s], before, 0.0), axis=-1, keepdims=True)
        rank2 = jnp.sum(jnp.where(hit2[s], before, 0.0), axis=-1, keepdims=True)
        seen = seen + jnp.sum(both[s], axis=0, keepdims=True)
        route = jnp.zeros((sub, ROUTER_LANES), F32)
        for lane_id, col in ((LANE_IDX1, idx1.astype(F32)), (LANE_IDX2, idx2.astype(F32)), (LANE_RANK1, rank1),
                             (LANE_RANK2, rank2), (LANE_GATE1, gate1), (LANE_GATE2, gate2)):
            route = jnp.where(lane == lane_id, col, route)
        route_ref[p, :] = route
        route_t_ref[:, p] = route.T[:8, :]
        h_ref[p, :] = _bf(h[s])
        u_ref[p, :] = _pack_bf16_pairs(u[s])
    seen_ref[...] = seen
    count_ref[...] = jnp.broadcast_to(seen, count_ref.shape)


def _out_call(o, bonus, gate, yconv, x, ones_bd, ln_w, ln_b, w_out, g_ffn, router_w, router_b, tm, n_split):
    n_tok = x.shape[0]
    row = lambda width: pl.BlockSpec((tm, width), lambda i: (i, 0))
    full = lambda arr: pl.BlockSpec(arr.shape, lambda i: (0,) * arr.ndim, pipeline_mode=pl.Buffered(1))
    sub = tm // n_split
    tri_strict = (jnp.arange(sub)[:, None] > jnp.arange(sub)[None, :]).astype(BF16)
    params = (ones_bd, ln_w, ln_b, w_out, g_ffn, router_w, router_b, tri_strict)
    return pl.pallas_call(
        functools.partial(_out_kernel, n_split=n_split),
        grid=(n_tok // tm,),
        in_specs=[row(D_RWKV)] * 4 + [row(D_MODEL)] + [full(p) for p in params],
        out_specs=[row(D_MODEL), row(D_MODEL // 2), row(ROUTER_LANES),
                   pl.BlockSpec((8, tm), lambda i: (0, i)),
                   pl.BlockSpec((8, ROUTER_LANES), lambda i: (0, 0))],
        out_shape=[jax.ShapeDtypeStruct((n_tok, D_MODEL), BF16),
                   jax.ShapeDtypeStruct((n_tok, D_MODEL // 2), jnp.int32),
                   jax.ShapeDtypeStruct((n_tok, ROUTER_LANES), F32),
                   jax.ShapeDtypeStruct((8, n_tok), F32),
                   jax.ShapeDtypeStruct((8, ROUTER_LANES), F32)],
        scratch_shapes=[pltpu.VMEM((1, ROUTER_LANES), F32), pltpu.VMEM(w_out.shape, BF16)],
        compiler_params=pltpu.CompilerParams(
            dimension_semantics=("arbitrary",), vmem_limit_bytes=VMEM_LIMIT),
        name="out_proj_route",
    )(o, bonus, gate, yconv, x, *params)


SC_CORES = 2
SC_SUBCORES = 16
SC_ROWS = 64


def _sc_mesh():
    return plsc.VectorSubcoreMesh(core_axis_name="c", subcore_axis_name="s",
                                  num_cores=SC_CORES, num_subcores=SC_SUBCORES)


def _sc_worker():
    return lax.axis_index("s") * SC_CORES + lax.axis_index("c")


def _sc_gather(table, idx):
    n_rows = idx.shape[0]
    width = table.shape[1]
    n_chunks = n_rows // (SC_CORES * SC_SUBCORES * SC_ROWS)

    def body(table_hbm, idx_hbm, out_hbm, idx_v, rows_v, gather_sem, write_sem):
        first = _sc_worker() * n_chunks
        pltpu.sync_copy(idx_hbm.at[pl.ds(first, n_chunks)], idx_v)
        gather = lambda j: pltpu.async_copy(table_hbm.at[idx_v.at[j]], rows_v.at[j % 2], gather_sem.at[j % 2])
        gathers = [gather(0)]
        writes = []
        for j in range(n_chunks):
            gathers[j].wait()
            if j + 1 < n_chunks:
                if j >= 1:
                    writes[j - 1].wait()
                gathers.append(gather(j + 1))
            dst = out_hbm.at[pl.ds(pl.multiple_of((first + j) * SC_ROWS, SC_ROWS), SC_ROWS)]
            writes.append(pltpu.async_copy(rows_v.at[j % 2], dst, write_sem.at[j % 2]))
        for j in range(max(n_chunks - 2, 0), n_chunks):
            writes[j].wait()

    return pl.kernel(
        body,
        out_type=jax.ShapeDtypeStruct((n_rows, width), table.dtype),
        mesh=_sc_mesh(),
        scratch_types=[pltpu.VMEM((n_chunks, SC_ROWS), jnp.int32), pltpu.VMEM((2, SC_ROWS, width), table.dtype),
                       pltpu.SemaphoreType.DMA((2,)), pltpu.SemaphoreType.DMA((2,))],
        name="sc_row_gather",
    )(table, idx.reshape(n_rows // SC_ROWS, SC_ROWS))


def _sc_scatter(rows, pos, n_out):
    n_rows, width = rows.shape
    n_slots = pos.shape[0] // n_rows
    slot_chunks = n_rows // SC_ROWS
    n_chunks = slot_chunks // (SC_CORES * SC_SUBCORES)

    def body(rows_hbm, pos_hbm, out_hbm, idx_v, rows_v, read_sem, scatter_sem):
        first = _sc_worker() * n_chunks
        for s in range(n_slots):
            pltpu.sync_copy(pos_hbm.at[pl.ds(s * slot_chunks + first, n_chunks)], idx_v.at[s])
        read = lambda j: pltpu.async_copy(
            rows_hbm.at[pl.ds(pl.multiple_of((first + j) * SC_ROWS, SC_ROWS), SC_ROWS)],
            rows_v.at[j % 2], read_sem.at[j % 2])
        reads = [read(0)]
        scatters = []
        for j in range(n_chunks):
            reads[j].wait()
            if j + 1 < n_chunks:
                if j >= 1:
                    for copy in scatters[j - 1]:
                        copy.wait()
                reads.append(read(j + 1))
            scatters.append([pltpu.async_copy(rows_v.at[j % 2], out_hbm.at[idx_v.at[s, j]], scatter_sem.at[j % 2])
                             for s in range(n_slots)])
        for j in range(max(n_chunks - 2, 0), n_chunks):
            for copy in scatters[j]:
                copy.wait()

    return pl.kernel(
        body,
        out_type=jax.ShapeDtypeStruct((n_out, width), rows.dtype),
        mesh=_sc_mesh(),
        scratch_types=[pltpu.VMEM((n_slots, n_chunks, SC_ROWS), jnp.int32),
                       pltpu.VMEM((2, SC_ROWS, width), rows.dtype),
                       pltpu.SemaphoreType.DMA((2,)), pltpu.SemaphoreType.DMA((2,))],
        name="sc_row_scatter",
    )(rows, pos.reshape(n_slots * slot_chunks, SC_ROWS))


def _expert_kernel(tile_expert_ref, n_valid_ref, tile_rows_ref, x_ref, wg_ref, wu_ref, wd_ref, y_ref):
    del tile_expert_ref
    step = pl.program_id(0)
    half = x_ref.shape[0] // 2

    @pl.when(step < n_valid_ref[0])
    def _():
        w_gate, w_up, w_down = _bf(wg_ref[0]), _bf(wu_ref[0]), _bf(wd_ref[0])

        def run(rows):
            x = _unpack_bf16_pairs(x_ref[rows, :])
            gate = jnp.dot(x, w_gate, preferred_element_type=F32)
            up = jnp.dot(x, w_up, preferred_element_type=F32)
            hid = gate * jax.nn.sigmoid(gate) * up
            y_ref[rows, :] = _pack_bf16_pairs(jnp.dot(_bf(hid), w_down, preferred_element_type=F32))

        run(slice(0, half))

        @pl.when(tile_rows_ref[step] > half)
        def _():
            run(slice(half, 2 * half))


def _expert_call(tile_expert, n_valid, tile_rows, x_sorted, w_gate, w_up, w_down, tm):
    n_rows = x_sorted.shape[0]
    rows = pl.BlockSpec((tm, D_MODEL // 2), lambda i, te, nv, tr: (jnp.minimum(i, nv[0] - 1), 0))
    by_expert = lambda i, te, nv, tr: (te[i], 0, 0)
    return pl.pallas_call(
        _expert_kernel,
        grid_spec=pltpu.PrefetchScalarGridSpec(
            num_scalar_prefetch=3,
            grid=(n_rows // tm,),
            in_specs=[rows,
                      pl.BlockSpec((1, D_MODEL, D_EXPERT), by_expert),
                      pl.BlockSpec((1, D_MODEL, D_EXPERT), by_expert),
                      pl.BlockSpec((1, D_EXPERT, D_MODEL), by_expert)],
            out_specs=rows),
        out_shape=jax.ShapeDtypeStruct((n_rows, D_MODEL // 2), jnp.int32),
        compiler_params=pltpu.CompilerParams(
            dimension_semantics=("arbitrary",), vmem_limit_bytes=VMEM_LIMIT),
        name="moe_experts",
    )(tile_expert, n_valid, tile_rows, x_sorted, w_gate, w_up, w_down)


def _final_kernel(h_ref, y1_ref, y2_ref, route_ref, gfin_ref, *rest):
    out_ref = rest[-1]
    route = route_ref[...]
    gate1 = route[:, LANE_GATE1:LANE_GATE1 + 1]
    gate2 = route[:, LANE_GATE2:LANE_GATE2 + 1]
    moe = gate1 * _unpack_bf16_pairs(y1_ref[...]).astype(F32) + gate2 * _unpack_bf16_pairs(y2_ref[...]).astype(F32)
    out_ref[...] = _rms_norm(h_ref[...] + moe, gfin_ref[...])


def _final_call(h, y_pairs, route, g_final, earlier, chunk, n_chunks, tm):
    n_tok = h.shape[0]
    n_blocks = n_tok // n_chunks // tm
    first = chunk * n_blocks
    in_specs = [pl.BlockSpec((tm, D_MODEL), lambda i: (first + i, 0)),
                pl.BlockSpec((tm, D_MODEL // 2), lambda i: (i, 0)),
                pl.BlockSpec((tm, D_MODEL // 2), lambda i: (i + n_blocks, 0)),
                pl.BlockSpec((tm, ROUTER_LANES), lambda i: (first + i, 0)),
                pl.BlockSpec((1, D_MODEL), lambda i: (0, 0))]
    args = [h, y_pairs, y_pairs, route, g_final]
    aliases = {}
    if earlier is not None:
        in_specs.append(pl.BlockSpec(memory_space=pl.ANY))
        args.append(earlier)
        aliases = {len(args) - 1: 0}
    return pl.pallas_call(
        _final_kernel,
        grid=(n_blocks,),
        in_specs=in_specs,
        out_specs=pl.BlockSpec((tm, D_MODEL), lambda i: (first + i, 0)),
        out_shape=jax.ShapeDtypeStruct((n_tok, D_MODEL), F32),
        input_output_aliases=aliases,
        compiler_params=pltpu.CompilerParams(
            dimension_semantics=("arbitrary",), vmem_limit_bytes=VMEM_LIMIT),
        name="moe_combine_norm",
    )(*args)


def _dispatch_plan(route_t, counts, tm):
    n_tok = route_t.shape[1]
    n_tiles = (2 * n_tok) // tm + N_EXPERTS
    counts = counts[0, :N_EXPERTS].astype(jnp.int32)
    tiles_per = (counts + tm - 1) // tm
    tile_end = jnp.cumsum(tiles_per)
    row_start = (tile_end - tiles_per) * tm
    experts = jnp.arange(N_EXPERTS, dtype=jnp.int32)

    def position(idx_lane, rank_lane):
        idx = route_t[idx_lane].astype(jnp.int32)
        start = jnp.sum(jnp.where(idx[None, :] == experts[:, None], row_start[:, None], 0), axis=0)
        return start + route_t[rank_lane].astype(jnp.int32)

    pos = jnp.concatenate([position(LANE_IDX1, LANE_RANK1), position(LANE_IDX2, LANE_RANK2)])
    n_valid = tile_end[-1:]
    tile = jnp.minimum(jnp.arange(n_tiles, dtype=jnp.int32), n_valid - 1)
    tile_expert = jnp.sum((tile_end[None, :] <= tile[:, None]).astype(jnp.int32), axis=1)
    of_tile = lambda per_expert: jnp.sum(jnp.where(tile_expert[:, None] == experts[None, :], per_expert[None, :], 0), axis=1)
    tile_rows = jnp.clip(of_tile(counts) - (tile - of_tile(tile_end - tiles_per)) * tm, 0, tm)
    return pos, n_tiles, tile_expert, n_valid, tile_rows


def _block(x, norm_mix_g, w_in, rwkv_mu, conv_w, decay_up, decay_base, aaa_up, aaa_base, gate_up,
           k_k, k_a, r_k, ln_x_w, ln_x_b, w_out, norm_ffn_g, router_group_w, router_group_b,
           router_expert_w, router_expert_b, expert_w_gate, expert_w_up, expert_w_down, norm_final_g,
           *, tm_in, tb_rwkv, tm_out, tm_expert, tm_final):
    bsz, seq, d_model = x.shape
    n_tok = bsz * seq
    row = lambda p: p.reshape(1, -1)
    half = LORA_WA // 2
    zeros = jnp.zeros((half, D_RWKV), F32)
    dup = _bf(jnp.concatenate([decay_up, zeros], axis=0))
    aup = _bf(jnp.concatenate([zeros, aaa_up], axis=0))
    head_of = jnp.arange(QUAD) // HEAD
    ones_quad = (head_of[:, None] == head_of[None, :]).astype(BF16)
    tri = (jnp.arange(CHUNK)[:, None] >= jnp.arange(CHUNK)[None, :]).astype(BF16)

    yconv, r, lw, k2, v, kk, kka, gate, bonus = _in_call(
        x, row(norm_mix_g), w_in, row(rwkv_mu), conv_w, dup, row(decay_base), aup, row(aaa_base),
        _bf(gate_up), row(k_k), row(k_a), row(r_k), ones_quad, tm_in)
    o = _rwkv_call(r, lw, k2, v, kk, kka, tri, tb_rwkv)

    pad = ROUTER_LANES - N_EXPERTS - N_GROUPS
    router_w = jnp.concatenate([router_expert_w, router_group_w, jnp.zeros((d_model, pad), F32)], axis=1)
    router_hi = _bf(router_w)
    router_split = jnp.concatenate([router_hi, _bf(router_w - router_hi.astype(F32))], axis=1)
    router_b = jnp.concatenate([router_expert_b, router_group_b, jnp.zeros((pad,), F32)]).reshape(1, -1)
    flat = lambda t: t.reshape(n_tok, t.shape[-1])
    h, u_pairs, route, route_t, counts = _out_call(
        flat(o), flat(bonus), flat(gate), flat(yconv), flat(x), ones_quad,
        row(ln_x_w), row(ln_x_b), w_out, row(norm_ffn_g), router_split, router_b, tm_out, OUT_SPLIT)

    pos, n_tiles, tile_expert, n_valid, tile_rows = _dispatch_plan(route_t, counts, tm_expert)
    x_sorted = _sc_scatter(u_pairs, pos, n_tiles * tm_expert)
    y_sorted = _expert_call(tile_expert, n_valid, tile_rows, x_sorted, expert_w_gate, expert_w_up, expert_w_down,
                            tm_expert)
    per_chunk = n_tok // COMBINE_CHUNKS
    out = None
    for c in range(COMBINE_CHUNKS):
        lo = c * per_chunk
        pos_c = jnp.concatenate([pos[lo:lo + per_chunk], pos[n_tok + lo:n_tok + lo + per_chunk]])
        out = _final_call(h, _sc_gather(y_sorted, pos_c), route, row(norm_final_g), out, c, COMBINE_CHUNKS, tm_final)
    return out.reshape(bsz, seq, d_model)


def kernel(x, norm_mix_g, w_in, rwkv_mu, conv_w, decay_up, decay_base, aaa_up, aaa_base, gate_up, k_k, k_a, r_k, ln_x_w, ln_x_b, w_out, norm_ffn_g, router_group_w, router_group_b, router_expert_w, router_expert_b, expert_w_gate, expert_w_up, expert_w_down, norm_final_g):
    return _block(x, norm_mix_g[0], w_in[0], rwkv_mu[0], conv_w[0], decay_up[0], decay_base[0],
                  aaa_up[0], aaa_base[0], gate_up[0], k_k[0], k_a[0], r_k[0].reshape(-1), ln_x_w[0],
                  ln_x_b[0], w_out[0], norm_ffn_g[0], router_group_w[0], router_group_b[0],
                  router_expert_w[0], router_expert_b[0], expert_w_gate[0], expert_w_up[0],
                  expert_w_down[0], norm_final_g,
                  tm_in=512, tb_rwkv=512, tm_out=512, tm_expert=1024, tm_final=512)
```

```python
import functools

import jax
import jax.numpy as jnp
from jax import lax
from jax.experimental import pallas as pl
from jax.experimental.pallas import tpu as pltpu
from jax.experimental.pallas import tpu_sc as plsc

F32 = jnp.float32
BF16 = jnp.bfloat16

D_MODEL = 1024
D_CONV = 512
CONV_WIDTH = 3
N_HEADS = 8
HEAD = 64
D_RWKV = N_HEADS * HEAD
LORA_WA = 128
GATE_LORA = 128
D_RWKV_PROJ = 3 * D_RWKV + LORA_WA + GATE_LORA
D_IN = 3 * D_CONV + D_RWKV_PROJ
N_GROUPS = 4
EXPERTS_PER_GROUP = 8
N_EXPERTS = N_GROUPS * EXPERTS_PER_GROUP
D_EXPERT = D_MODEL // 4
RMS_EPS = 1e-6
LN_X_EPS = 64e-5
L2_EPS = 1e-12

LANES = 128
CHUNK = 64
QUAD = 4 * HEAD
GROUP = 2 * HEAD
COMPACT_FROM = 64
OUT_SPLIT = 4
COMBINE_CHUNKS = 4
ROUTER_LANES = 128

VMEM_LIMIT = 56 * 1024 * 1024


def _bf(x):
    return x.astype(BF16)


def _mm(a, b):
    return jnp.dot(_bf(a), _bf(b), preferred_element_type=F32)


def _mm_nt(a, b):
    return lax.dot_general(_bf(a), _bf(b), (((1,), (1,)), ((), ())), preferred_element_type=F32)


def _mm_exact_lhs(lhs_bf16, x, passes):
    acc = None
    rem = x
    for _ in range(passes):
        piece = _bf(rem)
        part = jnp.dot(lhs_bf16, piece, preferred_element_type=F32)
        acc = part if acc is None else acc + part
        rem = rem - piece.astype(F32)
    return acc


def _head_sum(x, ones_quad):
    xb = _bf(x)
    return jnp.concatenate(
        [jnp.dot(xb[:, q * QUAD:(q + 1) * QUAD], ones_quad, preferred_element_type=F32)
         for q in range(x.shape[1] // QUAD)], axis=1)


def _rms_norm(x, g):
    return x * lax.rsqrt(jnp.mean(x * x, axis=-1, keepdims=True) + RMS_EPS) * g


def _shift_rows(cur, prev_rows, k):
    rolled = pltpu.roll(cur, k, 0)
    prev_rolled = pltpu.roll(prev_rows, k, 0)
    n = cur.shape[0]
    head = jnp.concatenate([prev_rolled, rolled[8:]], axis=0) if n > 8 else prev_rolled
    row = lax.broadcasted_iota(jnp.int32, cur.shape, 0)
    return jnp.where(row < k, head, rolled)


def _in_kernel(x_ref, g_ref, w_ref, mu_ref, convw_ref, dup_ref, dbase_ref, aup_ref, abase_ref,
               gup_ref, kk_ref, ka_ref, rk_ref, ones_ref,
               yconv_ref, r_ref, lw_ref, k_ref, v_ref, kkn_ref, kka_ref, gate_ref, bonus_ref,
               carry_ref, wbf_ref):
    @pl.when((pl.program_id(0) == 0) & (pl.program_id(1) == 0))
    def _():
        wbf_ref[...] = _bf(w_ref[...])

    @pl.when(pl.program_id(1) == 0)
    def _():
        carry_ref[...] = jnp.zeros_like(carry_ref)

    u = _bf(_rms_norm(x_ref[0], g_ref[...]))
    n_conv = 3 * D_CONV
    rk0 = n_conv
    bounds = dict(conv=(0, n_conv), lora=(rk0 + 3 * D_RWKV, D_IN), k=(rk0 + D_RWKV, rk0 + 2 * D_RWKV),
                  r=(rk0, rk0 + D_RWKV), v=(rk0 + 2 * D_RWKV, rk0 + 3 * D_RWKV))
    z = {name: jnp.dot(u, wbf_ref[:, lo:hi], preferred_element_type=F32) for name, (lo, hi) in bounds.items()}

    def lerp(name):
        lo, hi = bounds[name]
        cur = z[name]
        prev = carry_ref[:, D_CONV + lo - n_conv:D_CONV + hi - n_conv]
        mixed = cur + (_shift_rows(cur, prev, 1) - cur) * mu_ref[:, lo - n_conv:hi - n_conv]
        carry_ref[:, D_CONV + lo - n_conv:D_CONV + hi - n_conv] = cur[-8:]
        return mixed

    b_gate = z["conv"][:, :D_CONV]
    ch = z["conv"][:, D_CONV:2 * D_CONV] * z["conv"][:, 2 * D_CONV:]
    prev_ch = carry_ref[:, :D_CONV]
    conv = (convw_ref[2:3, :] * ch
            + convw_ref[1:2, :] * _shift_rows(ch, prev_ch, 1)
            + convw_ref[0:1, :] * _shift_rows(ch, prev_ch, 2))
    yconv_ref[0] = _bf(b_gate * conv)
    carry_ref[:, :D_CONV] = ch[-8:]

    lora = lerp("lora")
    wa_lo = lora[:, :LORA_WA]
    g_lo = lora[:, LORA_WA:]
    dec_in = -(dbase_ref[...] + jnp.dot(_bf(jnp.tanh(wa_lo)), dup_ref[...], preferred_element_type=F32))
    softplus = jnp.maximum(dec_in, 0.0) + jnp.log(1.0 + jnp.exp(-jnp.abs(dec_in)))
    w = -softplus - 0.5
    lw_ref[0] = -jnp.exp(w)
    a = jax.nn.sigmoid(abase_ref[...] + jnp.dot(_bf(wa_lo), aup_ref[...], preferred_element_type=F32))
    gate_ref[0] = jnp.dot(_bf(jax.nn.sigmoid(g_lo)), gup_ref[...], preferred_element_type=F32)

    ones_quad = ones_ref[...]
    k = lerp("k")
    kk = k * kk_ref[...]
    norm = jnp.sqrt(_head_sum(kk * kk, ones_quad))
    kk = kk / jnp.maximum(norm, L2_EPS)
    k2 = k * (1.0 + (a - 1.0) * ka_ref[...])
    k_ref[0] = _bf(k2)
    kkn_ref[0] = _bf(kk)
    kka_ref[0] = _bf(kk * a)
    r = lerp("r")
    r_ref[0] = _bf(r)
    v = lerp("v")
    v_ref[0] = _bf(v)
    bonus_ref[0] = _head_sum(r * k2 * rk_ref[...], ones_quad) * v


def _in_call(x, g, w_in, mu, conv_w, dup, dbase, aup, abase, gup, k_k, k_a, r_k, ones_quad, tm):
    bsz, seq, _ = x.shape
    full = lambda arr: pl.BlockSpec(arr.shape, lambda b, t: (0,) * arr.ndim, pipeline_mode=pl.Buffered(1))
    out_spec = pl.BlockSpec((1, tm, D_RWKV), lambda b, t: (b, t, 0))
    out_dtypes = (BF16, BF16, F32, BF16, BF16, BF16, BF16, F32, F32)
    params = (g, w_in, mu, conv_w, dup, dbase, aup, abase, gup, k_k, k_a, r_k, ones_quad)
    return pl.pallas_call(
        _in_kernel,
        grid=(bsz, seq // tm),
        in_specs=[pl.BlockSpec((1, tm, D_MODEL), lambda b, t: (b, t, 0))] + [full(p) for p in params],
        out_specs=[out_spec] * 9,
        out_shape=[jax.ShapeDtypeStruct((bsz, seq, D_RWKV), dt) for dt in out_dtypes],
        scratch_shapes=[pltpu.VMEM((8, D_CONV + D_RWKV_PROJ), F32), pltpu.VMEM(w_in.shape, BF16)],
        compiler_params=pltpu.CompilerParams(
            dimension_semantics=("arbitrary", "arbitrary"), vmem_limit_bytes=VMEM_LIMIT),
        name="in_proj",
    )(x, *params)


def _block_diag(y, bd_mask):
    return jnp.where(bd_mask, jnp.concatenate([y] * (GROUP // CHUNK), axis=0), 0.0)


def _each(fn, *lists):
    return [fn(*args) for args in zip(*lists)]


def _unit_lower_inverse(a_strict, t_idx, s_idx, bd):
    bdmm = lambda xs, ys: _each(lambda x, y: _mm(x, bd(y)), xs, ys)
    eye = (t_idx == s_idx).astype(F32)
    same8 = (t_idx // 8) == (s_idx // 8)
    a8 = _each(lambda a: jnp.where(same8, a, 0.0), a_strict)
    a8_2 = bdmm(a8, a8)
    a8_34 = bdmm(_each(lambda a, b: jnp.concatenate([a, b], axis=0), a8, a8_2), a8_2)
    inv = _each(lambda a, b, c: eye + a + b + c[:CHUNK], a8, a8_2, a8_34)
    inv = _each(jnp.add, inv, bdmm(inv, _each(lambda c: c[CHUNK:], a8_34)))
    size = 16
    while size < COMPACT_FROM:
        off = ((t_idx // size) == (s_idx // size)) & ((t_idx // (size // 2)) != (s_idx // (size // 2)))
        cross = bdmm(_each(lambda a: jnp.where(off, a, 0.0), a_strict), inv)
        inv = _each(jnp.add, inv, bdmm(inv, cross))
        size *= 2
    while size <= CHUNK:
        inv = _compact_level(a_strict, inv, size)
        size *= 2
    return inv


def _compact_level(a_strict, inv, size):
    half = size // 2
    heads = GROUP // CHUNK
    blocks = CHUNK // size
    lane_starts = [h * CHUNK + b * size for h in range(heads) for b in range(blocks)]
    row_starts = [b * size for b in range(blocks)]
    rows_first = lambda x: jnp.concatenate([x[r:r + half] for r in row_starts], axis=0)
    rows_second = lambda x: jnp.concatenate([x[r + half:r + size] for r in row_starts], axis=0)
    lanes_first = lambda x: jnp.concatenate([x[:, c:c + half] for c in lane_starts], axis=1)
    lanes_second = lambda x: jnp.concatenate([x[:, c + half:c + size] for c in lane_starts], axis=1)
    n_rows, n_lanes = CHUNK // 2, GROUP // 2
    own = (lax.broadcasted_iota(jnp.int32, (n_rows, n_lanes), 0) // half
           == (lax.broadcasted_iota(jnp.int32, (n_rows, n_lanes), 1) // half) % blocks)
    diag = (lax.broadcasted_iota(jnp.int32, (n_lanes, n_lanes), 0) // half
            == lax.broadcasted_iota(jnp.int32, (n_lanes, n_lanes), 1) // half)
    bd_half = lambda y: jnp.where(diag, jnp.concatenate([y] * heads, axis=0), 0.0)
    a21 = _each(lambda a: jnp.where(own, lanes_first(rows_second(a)), 0.0), a_strict)
    t11 = _each(lambda t: lanes_first(rows_first(t)), inv)
    t22 = _each(lambda t: lanes_second(rows_second(t)), inv)
    a21_t11 = _each(lambda x, y: _mm(x, bd_half(y)), a21, t11)
    new = _each(lambda x, y: _mm(x, bd_half(y)), t22, a21_t11)

    def placed(x):
        zero_lanes = jnp.zeros((n_rows, half), F32)
        wide = jnp.concatenate(sum(([x[:, i * half:(i + 1) * half], zero_lanes] for i in range(heads * blocks)), []),
                               axis=1)
        zero_rows = jnp.zeros((half, GROUP), F32)
        return jnp.concatenate(sum(([zero_rows, wide[b * half:(b + 1) * half]] for b in range(blocks)), []), axis=0)

    return _each(lambda t, x: t + placed(x), inv, new)


def _chunk_local(r, lw, k, v, kk, kka, tri, t_idx, s_idx, bd_mask):
    bd = lambda y: _block_diag(y, bd_mask)
    bdmm = lambda xs, ys: _each(lambda x, y: _mm(x, bd(y)), xs, ys)
    bdmm2 = lambda xs, ys, zs: _each(lambda x, y, z: _mm(x, jnp.concatenate([bd(y), bd(z)], axis=1)), xs, ys, zs)
    left = lambda xs: _each(lambda x: x[:, :GROUP], xs)
    right = lambda xs: _each(lambda x: x[:, GROUP:], xs)
    top = lambda xs: _each(lambda x: x[:CHUNK], xs)
    bottom = lambda xs: _each(lambda x: x[CHUNK:], xs)

    cum = _each(lambda x: _mm_exact_lhs(tri, x, 3), lw)
    cum_last = _each(lambda c: c[CHUNK - 1:CHUNK, :], cum)
    p_incl = _each(jnp.exp, cum)
    p_excl = _each(lambda c, x: jnp.exp(c - x), cum, lw)
    p_inv = _each(lambda c: jnp.exp(-c), cum)
    to_end = _each(lambda cl, c: jnp.exp(cl - c), cum_last, cum)
    a_t = _each(lambda x, p: -x * p, kk, p_excl)
    r_t = _each(jnp.multiply, r, p_incl)
    b_t = _each(jnp.multiply, kka, p_inv)
    k_t = _each(jnp.multiply, k, p_inv)
    b_end = _each(jnp.multiply, kka, to_end)
    k_end = _each(jnp.multiply, k, to_end)

    ar = _each(lambda a, b: jnp.concatenate([a, b], axis=0), a_t, r_t)
    row2 = lax.broadcasted_iota(jnp.int32, (2 * CHUNK, 2 * GROUP), 0)
    col2 = lax.broadcasted_iota(jnp.int32, (2 * CHUNK, 2 * GROUP), 1) % CHUNK
    causal = col2 < (row2 % CHUNK) + (row2 // CHUNK)
    scores = _each(lambda x, y, z: jnp.where(causal, _mm_nt(x, jnp.concatenate([bd(y), bd(z)], axis=0)), 0.0),
                   ar, b_t, k_t)
    a_ab = top(left(scores))
    a_rb = bottom(left(scores))

    inv = _unit_lower_inverse(a_ab, t_idx, s_idx, bd)
    kv = bdmm(right(scores), v)
    wu = bdmm2(inv, a_t, top(kv))
    ro = bdmm2(a_rb, left(wu), right(wu))
    r_hat = _each(jnp.add, r_t, left(ro))
    o_loc = _each(jnp.add, right(ro), bottom(kv))

    wu_b = _each(lambda x, b: _mm(x.T, b), wu, b_end)
    trans = _each(lambda x: jnp.where(bd_mask, x[:GROUP], 0.0), wu_b)
    d_bd = _each(lambda x, y, z: jnp.where(bd_mask, x[GROUP:] + _mm(y.T, z), 0.0), wu_b, v, k_end)
    d_ls = _each(lambda d: sum(d[h * CHUNK:(h + 1) * CHUNK] for h in range(1, GROUP // CHUNK)) + d[:CHUNK], d_bd)
    p_end = _each(jnp.exp, cum_last)
    return r_hat, o_loc, trans, d_ls, p_end


def _rwkv_kernel(r_ref, lw_ref, k_ref, v_ref, kk_ref, kka_ref, tri_ref, o_ref, state_ref, *, n_sub):
    @pl.when(pl.program_id(1) == 0)
    def _():
        state_ref[...] = jnp.zeros_like(state_ref)

    t_idx = lax.broadcasted_iota(jnp.int32, (CHUNK, GROUP), 0)
    s_idx = lax.broadcasted_iota(jnp.int32, (CHUNK, GROUP), 1) % CHUNK
    bd_mask = (lax.broadcasted_iota(jnp.int32, (GROUP, GROUP), 0) // CHUNK
               == lax.broadcasted_iota(jnp.int32, (GROUP, GROUP), 1) // CHUNK)
    n_group = D_RWKV // GROUP
    where = [(slice(c * CHUNK, (c + 1) * CHUNK), slice(g * GROUP, (g + 1) * GROUP))
             for c in range(n_sub) for g in range(n_group)]
    load = lambda ref: [ref[0, rows, lanes].astype(F32) for rows, lanes in where]
    r_hat, o_loc, trans, d_ls, p_end = _chunk_local(
        load(r_ref), load(lw_ref), load(k_ref), load(v_ref), load(kk_ref), load(kka_ref),
        tri_ref[...], t_idx, s_idx, bd_mask)

    state = [state_ref[g] for g in range(n_group)]
    for c in range(n_sub):
        chains = range(c * n_group, (c + 1) * n_group)
        out = [_mm_nt(r_hat[i], _block_diag(state[g], bd_mask)) + o_loc[i] for g, i in enumerate(chains)]
        for g, i in enumerate(chains):
            rows, lanes = where[i]
            o_ref[0, rows, lanes] = out[g]
        state = [state[g] * p_end[i] + _mm(state[g], trans[i]) + d_ls[i] for g, i in enumerate(chains)]
    for g in range(n_group):
        state_ref[g] = state[g]


def _rwkv_call(r, lw, k, v, kk, kka, tri, tb):
    bsz, seq, _ = r.shape
    spec = pl.BlockSpec((1, tb, D_RWKV), lambda b, t: (b, t, 0))
    return pl.pallas_call(
        functools.partial(_rwkv_kernel, n_sub=tb // CHUNK),
        grid=(bsz, seq // tb),
        in_specs=[spec] * 6 + [pl.BlockSpec(tri.shape, lambda b, t: (0, 0))],
        out_specs=spec,
        out_shape=jax.ShapeDtypeStruct((bsz, seq, D_RWKV), F32),
        scratch_shapes=[pltpu.VMEM((D_RWKV // GROUP, HEAD, GROUP), F32)],
        compiler_params=pltpu.CompilerParams(
            dimension_semantics=("arbitrary", "arbitrary"), vmem_limit_bytes=VMEM_LIMIT),
        name="rwkv_chunk",
    )(r, lw, k, v, kk, kka, tri)


def _route(logits):
    lane_i = lax.broadcasted_iota(jnp.int32, logits.shape, 1)
    lane = lane_i.astype(F32)
    lane_group = (lane_i // EXPERTS_PER_GROUP).astype(F32)
    neg = -jnp.inf
    big = float(ROUTER_LANES)
    is_group = (lane_i >= N_EXPERTS) & (lane_i < N_EXPERTS + N_GROUPS)
    gl = jnp.where(is_group, logits, neg)
    ge = jnp.exp(gl - jnp.max(gl, axis=-1, keepdims=True))
    gprob = ge / jnp.sum(ge, axis=-1, keepdims=True)
    g_top = jnp.max(gprob, axis=-1, keepdims=True)
    g_idx = jnp.min(jnp.where(is_group & (gprob == g_top), lane - N_EXPERTS, big), axis=-1, keepdims=True)

    in_group = (lane_i < N_EXPERTS) & (lane_group == g_idx)
    el = jnp.where(in_group, logits, neg)
    top1 = jnp.max(el, axis=-1, keepdims=True)
    idx1 = jnp.min(jnp.where(in_group & (el == top1), lane, big), axis=-1, keepdims=True)
    el2 = jnp.where(lane == idx1, neg, el)
    top2 = jnp.max(el2, axis=-1, keepdims=True)
    idx2 = jnp.min(jnp.where(in_group & (lane != idx1) & (el2 == top2), lane, big), axis=-1, keepdims=True)
    e2 = jnp.exp(top2 - top1)
    denom = 1.0 + e2
    return idx1, idx2, g_top * (1.0 / denom), g_top * (e2 / denom)


def _pack_bf16_pairs(x):
    bits = lax.bitcast_convert_type(_bf(x).astype(F32), jnp.uint32)
    k = x.shape[1] // 2
    return lax.bitcast_convert_type((bits[:, :k] >> 16) | bits[:, k:], jnp.int32)


def _unpack_bf16_pairs(words):
    bits = lax.bitcast_convert_type(words, jnp.uint32)
    lo = lax.bitcast_convert_type(bits << 16, F32)
    hi = lax.bitcast_convert_type(bits & jnp.uint32(0xFFFF0000), F32)
    return _bf(jnp.concatenate([lo, hi], axis=1))


LANE_IDX1, LANE_IDX2, LANE_RANK1, LANE_RANK2, LANE_GATE1, LANE_GATE2 = range(6)


def _out_kernel(o_ref, bonus_ref, gate_ref, yconv_ref, x_ref, ones_ref, lnw_ref, lnb_ref,
                wout_ref, gffn_ref, rw_ref, rb_ref, tri_ref, h_ref, u_ref, route_ref, route_t_ref, count_ref,
                seen_ref, wbf_ref, *, n_split):
    @pl.when(pl.program_id(0) == 0)
    def _():
        seen_ref[...] = jnp.zeros_like(seen_ref)
        wbf_ref[...] = _bf(wout_ref[...])

    sub = o_ref.shape[0] // n_split
    parts = [slice(s * sub, (s + 1) * sub) for s in range(n_split)]
    read = lambda ref: [ref[p, :] for p in parts]
    ones_quad = ones_ref[...]
    inv_n = 1.0 / HEAD
    o = read(o_ref)
    mean = _each(lambda v: _head_sum(v, ones_quad) * inv_n, o)
    cen = _each(jnp.subtract, o, mean)
    var = _each(lambda c: _head_sum(c * c, ones_quad) * inv_n, cen)
    on = _each(lambda c, v: c * lax.rsqrt(v + LN_X_EPS) * lnw_ref[...] + lnb_ref[...], cen, var)
    y_rwkv = _each(lambda a, b, g: (a + b) * g, on, read(bonus_ref), read(gate_ref))
    mixed = _each(lambda yc, yr: (jnp.dot(yc, wbf_ref[:D_CONV, :], preferred_element_type=F32)
                                  + jnp.dot(_bf(yr), wbf_ref[D_CONV:, :], preferred_element_type=F32)),
                  read(yconv_ref), y_rwkv)
    h = _each(jnp.add, read(x_ref), mixed)
    u = _each(lambda v: _rms_norm(v, gffn_ref[...]), h)
    u_hi = _each(_bf, u)
    u_lo = _each(lambda a, b: _bf(a - b.astype(F32)), u, u_hi)
    by_hi = _each(lambda a: jnp.dot(a, rw_ref[...], preferred_element_type=F32), u_hi)
    by_lo = _each(lambda a: jnp.dot(a, rw_ref[:, :ROUTER_LANES], preferred_element_type=F32), u_lo)
    logits = _each(lambda a, b: a[:, :ROUTER_LANES] + a[:, ROUTER_LANES:] + b + rb_ref[...], by_hi, by_lo)
    routed = _each(_route, logits)

    lane_i = lax.broadcasted_iota(jnp.int32, (sub, ROUTER_LANES), 1)
    lane = lane_i.astype(F32)
    hit1 = _each(lambda rt: lane == rt[0], routed)
    hit2 = _each(lambda rt: lane == rt[1], routed)
    both = _each(lambda a, b: a.astype(F32) + b.astype(F32), hit1, hit2)
    inside = _each(lambda b: jnp.dot(tri_ref[...], _bf(b), preferred_element_type=F32), both)
    seen = seen_ref[...]
    for s, p in enumerate(parts):
        idx1, idx2, gate1, gate2 = routed[s]
        before = inside[s] + seen
        rank1 = jnp.sum(jnp.where(hit1[s], before, 0.0), axis=-1, keepdims=True)
        rank2 = jnp.sum(jnp.where(hit2[s], before, 0.0), axis=-1, keepdims=True)
        seen = seen + jnp.sum(both[s], axis=0, keepdims=True)
        route = jnp.zeros((sub, ROUTER_LANES), F32)
        for lane_id, col in ((LANE_IDX1, idx1), (LANE_IDX2, idx2), (LANE_RANK1, rank1),
                             (LANE_RANK2, rank2), (LANE_GATE1, gate1), (LANE_GATE2, gate2)):
            route = jnp.where(lane_i == lane_id, col, route)
        route_ref[p, :] = route
        route_t_ref[:, p] = route.T[:8, :]
        h_ref[p, :] = _bf(h[s])
        u_ref[p, :] = _pack_bf16_pairs(u[s])
    seen_ref[...] = seen
    count_ref[...] = jnp.broadcast_to(seen, count_ref.shape)


def _out_call(o, bonus, gate, yconv, x, ones_bd, ln_w, ln_b, w_out, g_ffn, router_w, router_b, tm, n_split):
    n_tok = x.shape[0]
    row = lambda width: pl.BlockSpec((tm, width), lambda i: (i, 0))
    full = lambda arr: pl.BlockSpec(arr.shape, lambda i: (0,) * arr.ndim, pipeline_mode=pl.Buffered(1))
    sub = tm // n_split
    tri_strict = (jnp.arange(sub)[:, None] > jnp.arange(sub)[None, :]).astype(BF16)
    params = (ones_bd, ln_w, ln_b, w_out, g_ffn, router_w, router_b, tri_strict)
    return pl.pallas_call(
        functools.partial(_out_kernel, n_split=n_split),
        grid=(n_tok // tm,),
        in_specs=[row(D_RWKV)] * 4 + [row(D_MODEL)] + [full(p) for p in params],
        out_specs=[row(D_MODEL), row(D_MODEL // 2), row(ROUTER_LANES),
                   pl.BlockSpec((8, tm), lambda i: (0, i)),
                   pl.BlockSpec((8, ROUTER_LANES), lambda i: (0, 0))],
        out_shape=[jax.ShapeDtypeStruct((n_tok, D_MODEL), BF16),
                   jax.ShapeDtypeStruct((n_tok, D_MODEL // 2), jnp.int32),
                   jax.ShapeDtypeStruct((n_tok, ROUTER_LANES), F32),
                   jax.ShapeDtypeStruct((8, n_tok), F32),
                   jax.ShapeDtypeStruct((8, ROUTER_LANES), F32)],
        scratch_shapes=[pltpu.VMEM((1, ROUTER_LANES), F32), pltpu.VMEM(w_out.shape, BF16)],
        compiler_params=pltpu.CompilerParams(
            dimension_semantics=("arbitrary",), vmem_limit_bytes=VMEM_LIMIT),
        name="out_proj_route",
    )(o, bonus, gate, yconv, x, *params)


SC_CORES = 2
SC_SUBCORES = 16
SC_ROWS = 64


def _sc_mesh():
    return plsc.VectorSubcoreMesh(core_axis_name="c", subcore_axis_name="s",
                                  num_cores=SC_CORES, num_subcores=SC_SUBCORES)


def _sc_worker():
    return lax.axis_index("s") * SC_CORES + lax.axis_index("c")


def _sc_gather(table, idx):
    n_rows = idx.shape[0]
    width = table.shape[1]
    n_chunks = n_rows // (SC_CORES * SC_SUBCORES * SC_ROWS)

    def body(table_hbm, idx_hbm, out_hbm, idx_v, rows_v, gather_sem, write_sem):
        first = _sc_worker() * n_chunks
        pltpu.sync_copy(idx_hbm.at[pl.ds(first, n_chunks)], idx_v)
        gather = lambda j: pltpu.async_copy(table_hbm.at[idx_v.at[j]], rows_v.at[j % 2], gather_sem.at[j % 2])
        gathers = [gather(0)]
        writes = []
        for j in range(n_chunks):
            gathers[j].wait()
            if j + 1 < n_chunks:
                if j >= 1:
                    writes[j - 1].wait()
                gathers.append(gather(j + 1))
            dst = out_hbm.at[pl.ds(pl.multiple_of((first + j) * SC_ROWS, SC_ROWS), SC_ROWS)]
            writes.append(pltpu.async_copy(rows_v.at[j % 2], dst, write_sem.at[j % 2]))
        for j in range(max(n_chunks - 2, 0), n_chunks):
            writes[j].wait()

    return pl.kernel(
        body,
        out_type=jax.ShapeDtypeStruct((n_rows, width), table.dtype),
        mesh=_sc_mesh(),
        scratch_types=[pltpu.VMEM((n_chunks, SC_ROWS), jnp.int32), pltpu.VMEM((2, SC_ROWS, width), table.dtype),
                       pltpu.SemaphoreType.DMA((2,)), pltpu.SemaphoreType.DMA((2,))],
        name="sc_row_gather",
    )(table, idx.reshape(n_rows // SC_ROWS, SC_ROWS))


def _sc_scatter(rows, pos, n_out):
    n_rows, width = rows.shape
    n_slots = pos.shape[0] // n_rows
    slot_chunks = n_rows // SC_ROWS
    n_chunks = slot_chunks // (SC_CORES * SC_SUBCORES)

    def body(rows_hbm, pos_hbm, out_hbm, idx_v, rows_v, read_sem, scatter_sem):
        first = _sc_worker() * n_chunks
        for s in range(n_slots):
            pltpu.sync_copy(pos_hbm.at[pl.ds(s * slot_chunks + first, n_chunks)], idx_v.at[s])
        read = lambda j: pltpu.async_copy(
            rows_hbm.at[pl.ds(pl.multiple_of((first + j) * SC_ROWS, SC_ROWS), SC_ROWS)],
            rows_v.at[j % 2], read_sem.at[j % 2])
        reads = [read(0)]
        scatters = []
        for j in range(n_chunks):
            reads[j].wait()
            if j + 1 < n_chunks:
                if j >= 1:
                    for copy in scatters[j - 1]:
                        copy.wait()
                reads.append(read(j + 1))
            scatters.append([pltpu.async_copy(rows_v.at[j % 2], out_hbm.at[idx_v.at[s, j]], scatter_sem.at[j % 2])
                             for s in range(n_slots)])
        for j in range(max(n_chunks - 2, 0), n_chunks):
            for copy in scatters[j]:
                copy.wait()

    return pl.kernel(
        body,
        out_type=jax.ShapeDtypeStruct((n_out, width), rows.dtype),
        mesh=_sc_mesh(),
        scratch_types=[pltpu.VMEM((n_slots, n_chunks, SC_ROWS), jnp.int32),
                       pltpu.VMEM((2, SC_ROWS, width), rows.dtype),
                       pltpu.SemaphoreType.DMA((2,)), pltpu.SemaphoreType.DMA((2,))],
        name="sc_row_scatter",
    )(rows, pos.reshape(n_slots * slot_chunks, SC_ROWS))


def _expert_kernel(tile_expert_ref, n_valid_ref, tile_rows_ref, x_ref, wg_ref, wu_ref, wd_ref, y_ref):
    del tile_expert_ref
    step = pl.program_id(0)
    half = x_ref.shape[0] // 2

    @pl.when(step < n_valid_ref[0])
    def _():
        w_gate, w_up, w_down = _bf(wg_ref[0]), _bf(wu_ref[0]), _bf(wd_ref[0])

        def run(rows):
            x = _unpack_bf16_pairs(x_ref[rows, :])
            gate = jnp.dot(x, w_gate, preferred_element_type=F32)
            up = jnp.dot(x, w_up, preferred_element_type=F32)
            hid = gate * jax.nn.sigmoid(gate) * up
            y_ref[rows, :] = _pack_bf16_pairs(jnp.dot(_bf(hid), w_down, preferred_element_type=F32))

        run(slice(0, half))

        @pl.when(tile_rows_ref[step] > half)
        def _():
            run(slice(half, 2 * half))


def _expert_call(tile_expert, n_valid, tile_rows, x_sorted, w_gate, w_up, w_down, tm):
    n_rows = x_sorted.shape[0]
    rows = pl.BlockSpec((tm, D_MODEL // 2), lambda i, te, nv, tr: (jnp.minimum(i, nv[0] - 1), 0))
    by_expert = lambda i, te, nv, tr: (te[i], 0, 0)
    return pl.pallas_call(
        _expert_kernel,
        grid_spec=pltpu.PrefetchScalarGridSpec(
            num_scalar_prefetch=3,
            grid=(n_rows // tm,),
            in_specs=[rows,
                      pl.BlockSpec((1, D_MODEL, D_EXPERT), by_expert),
                      pl.BlockSpec((1, D_MODEL, D_EXPERT), by_expert),
                      pl.BlockSpec((1, D_EXPERT, D_MODEL), by_expert)],
            out_specs=rows),
        out_shape=jax.ShapeDtypeStruct((n_rows, D_MODEL // 2), jnp.int32),
        compiler_params=pltpu.CompilerParams(
            dimension_semantics=("arbitrary",), vmem_limit_bytes=VMEM_LIMIT),
        name="moe_experts",
    )(tile_expert, n_valid, tile_rows, x_sorted, w_gate, w_up, w_down)


def _final_kernel(h_ref, y1_ref, y2_ref, route_ref, gfin_ref, *rest):
    out_ref = rest[-1]
    route = route_ref[...]
    gate1 = route[:, LANE_GATE1:LANE_GATE1 + 1]
    gate2 = route[:, LANE_GATE2:LANE_GATE2 + 1]
    moe = gate1 * _unpack_bf16_pairs(y1_ref[...]).astype(F32) + gate2 * _unpack_bf16_pairs(y2_ref[...]).astype(F32)
    out_ref[...] = _rms_norm(h_ref[...] + moe, gfin_ref[...])


def _final_call(h, y_pairs, route, g_final, earlier, chunk, n_chunks, tm):
    n_tok = h.shape[0]
    n_blocks = n_tok // n_chunks // tm
    first = chunk * n_blocks
    in_specs = [pl.BlockSpec((tm, D_MODEL), lambda i: (first + i, 0)),
                pl.BlockSpec((tm, D_MODEL // 2), lambda i: (i, 0)),
                pl.BlockSpec((tm, D_MODEL // 2), lambda i: (i + n_blocks, 0)),
                pl.BlockSpec((tm, ROUTER_LANES), lambda i: (first + i, 0)),
                pl.BlockSpec((1, D_MODEL), lambda i: (0, 0))]
    args = [h, y_pairs, y_pairs, route, g_final]
    aliases = {}
    if earlier is not None:
        in_specs.append(pl.BlockSpec(memory_space=pl.ANY))
        args.append(earlier)
        aliases = {len(args) - 1: 0}
    return pl.pallas_call(
        _final_kernel,
        grid=(n_blocks,),
        in_specs=in_specs,
        out_specs=pl.BlockSpec((tm, D_MODEL), lambda i: (first + i, 0)),
        out_shape=jax.ShapeDtypeStruct((n_tok, D_MODEL), F32),
        input_output_aliases=aliases,
        compiler_params=pltpu.CompilerParams(
            dimension_semantics=("arbitrary",), vmem_limit_bytes=VMEM_LIMIT),
        name="moe_combine_norm",
    )(*args)


def _dispatch_plan(route_t, counts, tm):
    n_tok = route_t.shape[1]
    n_tiles = (2 * n_tok) // tm + N_EXPERTS
    counts = counts[0, :N_EXPERTS].astype(jnp.int32)
    tiles_per = (counts + tm - 1) // tm
    tile_end = jnp.cumsum(tiles_per)
    row_start = (tile_end - tiles_per) * tm
    experts = jnp.arange(N_EXPERTS, dtype=jnp.int32)

    def position(idx_lane, rank_lane):
        idx = route_t[idx_lane].astype(jnp.int32)
        start = jnp.sum(jnp.where(idx[None, :] == experts[:, None], row_start[:, None], 0), axis=0)
        return start + route_t[rank_lane].astype(jnp.int32)

    pos = jnp.concatenate([position(LANE_IDX1, LANE_RANK1), position(LANE_IDX2, LANE_RANK2)])
    n_valid = tile_end[-1:]
    tile = jnp.minimum(jnp.arange(n_tiles, dtype=jnp.int32), n_valid - 1)
    tile_expert = jnp.sum((tile_end[None, :] <= tile[:, None]).astype(jnp.int32), axis=1)
    of_tile = lambda per_expert: jnp.sum(jnp.where(tile_expert[:, None] == experts[None, :], per_expert[None, :], 0), axis=1)
    tile_rows = jnp.clip(of_tile(counts) - (tile - of_tile(tile_end - tiles_per)) * tm, 0, tm)
    return pos, n_tiles, tile_expert, n_valid, tile_rows


def _block(x, norm_mix_g, w_in, rwkv_mu, conv_w, decay_up, decay_base, aaa_up, aaa_base, gate_up,
           k_k, k_a, r_k, ln_x_w, ln_x_b, w_out, norm_ffn_g, router_group_w, router_group_b,
           router_expert_w, router_expert_b, expert_w_gate, expert_w_up, expert_w_down, norm_final_g,
           *, tm_in, tb_rwkv, tm_out, tm_expert, tm_final):
    bsz, seq, d_model = x.shape
    n_tok = bsz * seq
    row = lambda p: p.reshape(1, -1)
    half = LORA_WA // 2
    zeros = jnp.zeros((half, D_RWKV), F32)
    dup = _bf(jnp.concatenate([decay_up, zeros], axis=0))
    aup = _bf(jnp.concatenate([zeros, aaa_up], axis=0))
    head_of = jnp.arange(QUAD) // HEAD
    ones_quad = (head_of[:, None] == head_of[None, :]).astype(BF16)
    tri = (jnp.arange(CHUNK)[:, None] >= jnp.arange(CHUNK)[None, :]).astype(BF16)

    yconv, r, lw, k2, v, kk, kka, gate, bonus = _in_call(
        x, row(norm_mix_g), w_in, row(rwkv_mu), conv_w, dup, row(decay_base), aup, row(aaa_base),
        _bf(gate_up), row(k_k), row(k_a), row(r_k), ones_quad, tm_in)
    o = _rwkv_call(r, lw, k2, v, kk, kka, tri, tb_rwkv)

    pad = ROUTER_LANES - N_EXPERTS - N_GROUPS
    router_w = jnp.concatenate([router_expert_w, router_group_w, jnp.zeros((d_model, pad), F32)], axis=1)
    router_hi = _bf(router_w)
    router_split = jnp.concatenate([router_hi, _bf(router_w - router_hi.astype(F32))], axis=1)
    router_b = jnp.concatenate([router_expert_b, router_group_b, jnp.zeros((pad,), F32)]).reshape(1, -1)
    flat = lambda t: t.reshape(n_tok, t.shape[-1])
    h, u_pairs, route, route_t, counts = _out_call(
        flat(o), flat(bonus), flat(gate), flat(yconv), flat(x), ones_quad,
        row(ln_x_w), row(ln_x_b), w_out, row(norm_ffn_g), router_split, router_b, tm_out, OUT_SPLIT)

    pos, n_tiles, tile_expert, n_valid, tile_rows = _dispatch_plan(route_t, counts, tm_expert)
    x_sorted = _sc_scatter(u_pairs, pos, n_tiles * tm_expert)
    y_sorted = _expert_call(tile_expert, n_valid, tile_rows, x_sorted, expert_w_gate, expert_w_up, expert_w_down,
                            tm_expert)
    per_chunk = n_tok // COMBINE_CHUNKS
    out = None
    for c in range(COMBINE_CHUNKS):
        lo = c * per_chunk
        pos_c = jnp.concatenate([pos[lo:lo + per_chunk], pos[n_tok + lo:n_tok + lo + per_chunk]])
        out = _final_call(h, _sc_gather(y_sorted, pos_c), route, row(norm_final_g), out, c, COMBINE_CHUNKS, tm_final)
    return out.reshape(bsz, seq, d_model)


def kernel(x, norm_mix_g, w_in, rwkv_mu, conv_w, decay_up, decay_base, aaa_up, aaa_base, gate_up, k_k, k_a, r_k, ln_x_w, ln_x_b, w_out, norm_ffn_g, router_group_w, router_group_b, router_expert_w, router_expert_b, expert_w_gate, expert_w_up, expert_w_down, norm_final_g):
    return _block(x, norm_mix_g[0], w_in[0], rwkv_mu[0], conv_w[0], decay_up[0], decay_base[0],
                  aaa_up[0], aaa_base[0], gate_up[0], k_k[0], k_a[0], r_k[0].reshape(-1), ln_x_w[0],
                  ln_x_b[0], w_out[0], norm_ffn_g[0], router_group_w[0], router_group_b[0],
                  router_expert_w[0], router_expert_b[0], expert_w_gate[0], expert_w_up[0],
                  expert_w_down[0], norm_final_g,
                  tm_in=512, tb_rwkv=512, tm_out=512, tm_expert=1024, tm_final=512)
```

```python
import functools

import jax
import jax.numpy as jnp
from jax import lax
from jax.experimental import pallas as pl
from jax.experimental.pallas import tpu as pltpu
from jax.experimental.pallas import tpu_sc as plsc

F32 = jnp.float32
BF16 = jnp.bfloat16

D_MODEL = 1024
D_CONV = 512
CONV_WIDTH = 3
N_HEADS = 8
HEAD = 64
D_RWKV = N_HEADS * HEAD
LORA_WA = 128
GATE_LORA = 128
D_RWKV_PROJ = 3 * D_RWKV + LORA_WA + GATE_LORA
D_IN = 3 * D_CONV + D_RWKV_PROJ
N_GROUPS = 4
EXPERTS_PER_GROUP = 8
N_EXPERTS = N_GROUPS * EXPERTS_PER_GROUP
D_EXPERT = D_MODEL // 4
RMS_EPS = 1e-6
LN_X_EPS = 64e-5
L2_EPS = 1e-12

LANES = 128
CHUNK = 64
QUAD = 4 * HEAD
GROUP = 2 * HEAD
COMPACT_FROM = 64
OUT_SPLIT = 4
COMBINE_CHUNKS = 4
ROUTER_LANES = 128

VMEM_LIMIT = 56 * 1024 * 1024


def _bf(x):
    return x.astype(BF16)


def _mm(a, b):
    return jnp.dot(_bf(a), _bf(b), preferred_element_type=F32)


def _mm_nt(a, b):
    return lax.dot_general(_bf(a), _bf(b), (((1,), (1,)), ((), ())), preferred_element_type=F32)


def _mm_exact_lhs(lhs_bf16, x, passes):
    acc = None
    rem = x
    for _ in range(passes):
        piece = _bf(rem)
        part = jnp.dot(lhs_bf16, piece, preferred_element_type=F32)
        acc = part if acc is None else acc + part
        rem = rem - piece.astype(F32)
    return acc


def _head_sum(x, ones_quad):
    xb = _bf(x)
    return jnp.concatenate(
        [jnp.dot(xb[:, q * QUAD:(q + 1) * QUAD], ones_quad, preferred_element_type=F32)
         for q in range(x.shape[1] // QUAD)], axis=1)


def _rms_norm(x, g):
    return x * lax.rsqrt(jnp.mean(x * x, axis=-1, keepdims=True) + RMS_EPS) * g


def _shift_rows(cur, prev_rows, k):
    rolled = pltpu.roll(cur, k, 0)
    prev_rolled = pltpu.roll(prev_rows, k, 0)
    n = cur.shape[0]
    head = jnp.concatenate([prev_rolled, rolled[8:]], axis=0) if n > 8 else prev_rolled
    row = lax.broadcasted_iota(jnp.int32, cur.shape, 0)
    return jnp.where(row < k, head, rolled)


def _in_kernel(x_ref, g_ref, w_ref, mu_ref, convw_ref, dup_ref, dbase_ref, aup_ref, abase_ref,
               gup_ref, kk_ref, ka_ref, rk_ref, ones_ref,
               yconv_ref, r_ref, lw_ref, k_ref, v_ref, kkn_ref, kka_ref, gate_ref, bonus_ref,
               carry_ref, wbf_ref):
    @pl.when((pl.program_id(0) == 0) & (pl.program_id(1) == 0))
    def _():
        wbf_ref[...] = _bf(w_ref[...])

    @pl.when(pl.program_id(1) == 0)
    def _():
        carry_ref[...] = jnp.zeros_like(carry_ref)

    u = _bf(_rms_norm(x_ref[0], g_ref[...]))
    n_conv = 3 * D_CONV
    rk0 = n_conv
    bounds = dict(conv=(0, n_conv), lora=(rk0 + 3 * D_RWKV, D_IN), k=(rk0 + D_RWKV, rk0 + 2 * D_RWKV),
                  r=(rk0, rk0 + D_RWKV), v=(rk0 + 2 * D_RWKV, rk0 + 3 * D_RWKV))
    z = {name: jnp.dot(u, wbf_ref[:, lo:hi], preferred_element_type=F32) for name, (lo, hi) in bounds.items()}

    def lerp(name):
        lo, hi = bounds[name]
        cur = z[name]
        prev = carry_ref[:, D_CONV + lo - n_conv:D_CONV + hi - n_conv]
        mixed = cur + (_shift_rows(cur, prev, 1) - cur) * mu_ref[:, lo - n_conv:hi - n_conv]
        carry_ref[:, D_CONV + lo - n_conv:D_CONV + hi - n_conv] = cur[-8:]
        return mixed

    b_gate = z["conv"][:, :D_CONV]
    ch = z["conv"][:, D_CONV:2 * D_CONV] * z["conv"][:, 2 * D_CONV:]
    prev_ch = carry_ref[:, :D_CONV]
    conv = (convw_ref[2:3, :] * ch
            + convw_ref[1:2, :] * _shift_rows(ch, prev_ch, 1)
            + convw_ref[0:1, :] * _shift_rows(ch, prev_ch, 2))
    yconv_ref[0] = _bf(b_gate * conv)
    carry_ref[:, :D_CONV] = ch[-8:]

    lora = lerp("lora")
    wa_lo = lora[:, :LORA_WA]
    g_lo = lora[:, LORA_WA:]
    dec_in = -(dbase_ref[...] + jnp.dot(_bf(jnp.tanh(wa_lo)), dup_ref[...], preferred_element_type=F32))
    softplus = jnp.maximum(dec_in, 0.0) + jnp.log(1.0 + jnp.exp(-jnp.abs(dec_in)))
    w = -softplus - 0.5
    lw_ref[0] = -jnp.exp(w)
    a = jax.nn.sigmoid(abase_ref[...] + jnp.dot(_bf(wa_lo), aup_ref[...], preferred_element_type=F32))
    gate_ref[0] = jnp.dot(_bf(jax.nn.sigmoid(g_lo)), gup_ref[...], preferred_element_type=F32)

    ones_quad = ones_ref[...]
    k = lerp("k")
    kk = k * kk_ref[...]
    norm = jnp.sqrt(_head_sum(kk * kk, ones_quad))
    kk = kk / jnp.maximum(norm, L2_EPS)
    k2 = k * (1.0 + (a - 1.0) * ka_ref[...])
    k_ref[0] = _bf(k2)
    kkn_ref[0] = _bf(kk)
    kka_ref[0] = _bf(kk * a)
    r = lerp("r")
    r_ref[0] = _bf(r)
    v = lerp("v")
    v_ref[0] = _bf(v)
    bonus_ref[0] = _head_sum(r * k2 * rk_ref[...], ones_quad) * v


def _in_call(x, g, w_in, mu, conv_w, dup, dbase, aup, abase, gup, k_k, k_a, r_k, ones_quad, tm):
    bsz, seq, _ = x.shape
    full = lambda arr: pl.BlockSpec(arr.shape, lambda b, t: (0,) * arr.ndim, pipeline_mode=pl.Buffered(1))
    out_spec = pl.BlockSpec((1, tm, D_RWKV), lambda b, t: (b, t, 0))
    out_dtypes = (BF16, BF16, F32, BF16, BF16, BF16, BF16, F32, F32)
    params = (g, w_in, mu, conv_w, dup, dbase, aup, abase, gup, k_k, k_a, r_k, ones_quad)
    return pl.pallas_call(
        _in_kernel,
        grid=(bsz, seq // tm),
        in_specs=[pl.BlockSpec((1, tm, D_MODEL), lambda b, t: (b, t, 0))] + [full(p) for p in params],
        out_specs=[out_spec] * 9,
        out_shape=[jax.ShapeDtypeStruct((bsz, seq, D_RWKV), dt) for dt in out_dtypes],
        scratch_shapes=[pltpu.VMEM((8, D_CONV + D_RWKV_PROJ), F32), pltpu.VMEM(w_in.shape, BF16)],
        compiler_params=pltpu.CompilerParams(
            dimension_semantics=("arbitrary", "arbitrary"), vmem_limit_bytes=VMEM_LIMIT),
        name="in_proj",
    )(x, *params)


def _block_diag(y, bd_mask):
    return jnp.where(bd_mask, jnp.concatenate([y] * (GROUP // CHUNK), axis=0), 0.0)


def _each(fn, *lists):
    return [fn(*args) for args in zip(*lists)]


def _unit_lower_inverse(a_strict, t_idx, s_idx, bd):
    bdmm = lambda xs, ys: _each(lambda x, y: _mm(x, bd(y)), xs, ys)
    eye = (t_idx == s_idx).astype(F32)
    same8 = (t_idx // 8) == (s_idx // 8)
    a8 = _each(lambda a: jnp.where(same8, a, 0.0), a_strict)
    a8_2 = bdmm(a8, a8)
    a8_34 = bdmm(_each(lambda a, b: jnp.concatenate([a, b], axis=0), a8, a8_2), a8_2)
    inv = _each(lambda a, b, c: eye + a + b + c[:CHUNK], a8, a8_2, a8_34)
    inv = _each(jnp.add, inv, bdmm(inv, _each(lambda c: c[CHUNK:], a8_34)))
    size = 16
    while size < COMPACT_FROM:
        off = ((t_idx // size) == (s_idx // size)) & ((t_idx // (size // 2)) != (s_idx // (size // 2)))
        cross = bdmm(_each(lambda a: jnp.where(off, a, 0.0), a_strict), inv)
        inv = _each(jnp.add, inv, bdmm(inv, cross))
        size *= 2
    while size <= CHUNK:
        inv = _compact_level(a_strict, inv, size)
        size *= 2
    return inv


def _compact_level(a_strict, inv, size):
    half = size // 2
    heads = GROUP // CHUNK
    blocks = CHUNK // size
    lane_starts = [h * CHUNK + b * size for h in range(heads) for b in range(blocks)]
    row_starts = [b * size for b in range(blocks)]
    rows_first = lambda x: jnp.concatenate([x[r:r + half] for r in row_starts], axis=0)
    rows_second = lambda x: jnp.concatenate([x[r + half:r + size] for r in row_starts], axis=0)
    lanes_first = lambda x: jnp.concatenate([x[:, c:c + half] for c in lane_starts], axis=1)
    lanes_second = lambda x: jnp.concatenate([x[:, c + half:c + size] for c in lane_starts], axis=1)
    n_rows, n_lanes = CHUNK // 2, GROUP // 2
    own = (lax.broadcasted_iota(jnp.int32, (n_rows, n_lanes), 0) // half
           == (lax.broadcasted_iota(jnp.int32, (n_rows, n_lanes), 1) // half) % blocks)
    diag = (lax.broadcasted_iota(jnp.int32, (n_lanes, n_lanes), 0) // half
            == lax.broadcasted_iota(jnp.int32, (n_lanes, n_lanes), 1) // half)
    bd_half = lambda y: jnp.where(diag, jnp.concatenate([y] * heads, axis=0), 0.0)
    a21 = _each(lambda a: jnp.where(own, lanes_first(rows_second(a)), 0.0), a_strict)
    t11 = _each(lambda t: lanes_first(rows_first(t)), inv)
    t22 = _each(lambda t: lanes_second(rows_second(t)), inv)
    a21_t11 = _each(lambda x, y: _mm(x, bd_half(y)), a21, t11)
    new = _each(lambda x, y: _mm(x, bd_half(y)), t22, a21_t11)

    def placed(x):
        zero_lanes = jnp.zeros((n_rows, half), F32)
        wide = jnp.concatenate(sum(([x[:, i * half:(i + 1) * half], zero_lanes] for i in range(heads * blocks)), []),
                               axis=1)
        zero_rows = jnp.zeros((half, GROUP), F32)
        return jnp.concatenate(sum(([zero_rows, wide[b * half:(b + 1) * half]] for b in range(blocks)), []), axis=0)

    return _each(lambda t, x: t + placed(x), inv, new)


def _chunk_local(r, lw, k, v, kk, kka, tri, t_idx, s_idx, bd_mask):
    bd = lambda y: _block_diag(y, bd_mask)
    bdmm = lambda xs, ys: _each(lambda x, y: _mm(x, bd(y)), xs, ys)
    bdmm2 = lambda xs, ys, zs: _each(lambda x, y, z: _mm(x, jnp.concatenate([bd(y), bd(z)], axis=1)), xs, ys, zs)
    left = lambda xs: _each(lambda x: x[:, :GROUP], xs)
    right = lambda xs: _each(lambda x: x[:, GROUP:], xs)
    top = lambda xs: _each(lambda x: x[:CHUNK], xs)
    bottom = lambda xs: _each(lambda x: x[CHUNK:], xs)

    cum = _each(lambda x: _mm_exact_lhs(tri, x, 3), lw)
    cum_last = _each(lambda c: c[CHUNK - 1:CHUNK, :], cum)
    p_incl = _each(jnp.exp, cum)
    p_excl = _each(lambda c, x: jnp.exp(c - x), cum, lw)
    p_inv = _each(lambda c: jnp.exp(-c), cum)
    to_end = _each(lambda cl, c: jnp.exp(cl - c), cum_last, cum)
    a_t = _each(lambda x, p: -x * p, kk, p_excl)
    r_t = _each(jnp.multiply, r, p_incl)
    b_t = _each(jnp.multiply, kka, p_inv)
    k_t = _each(jnp.multiply, k, p_inv)
    b_end = _each(jnp.multiply, kka, to_end)
    k_end = _each(jnp.multiply, k, to_end)

    ar = _each(lambda a, b: jnp.concatenate([a, b], axis=0), a_t, r_t)
    row2 = lax.broadcasted_iota(jnp.int32, (2 * CHUNK, 2 * GROUP), 0)
    col2 = lax.broadcasted_iota(jnp.int32, (2 * CHUNK, 2 * GROUP), 1) % CHUNK
    causal = col2 < (row2 % CHUNK) + (row2 // CHUNK)
    scores = _each(lambda x, y, z: jnp.where(causal, _mm_nt(x, jnp.concatenate([bd(y), bd(z)], axis=0)), 0.0),
                   ar, b_t, k_t)
    a_ab = top(left(scores))
    a_rb = bottom(left(scores))

    inv = _unit_lower_inverse(a_ab, t_idx, s_idx, bd)
    kv = bdmm(right(scores), v)
    wu = bdmm2(inv, a_t, top(kv))
    ro = bdmm2(a_rb, left(wu), right(wu))
    r_hat = _each(jnp.add, r_t, left(ro))
    o_loc = _each(jnp.add, right(ro), bottom(kv))

    wu_b = _each(lambda x, b: _mm(x.T, b), wu, b_end)
    trans = _each(lambda x: jnp.where(bd_mask, x[:GROUP], 0.0), wu_b)
    d_bd = _each(lambda x, y, z: jnp.where(bd_mask, x[GROUP:] + _mm(y.T, z), 0.0), wu_b, v, k_end)
    d_ls = _each(lambda d: sum(d[h * CHUNK:(h + 1) * CHUNK] for h in range(1, GROUP // CHUNK)) + d[:CHUNK], d_bd)
    p_end = _each(jnp.exp, cum_last)
    return r_hat, o_loc, trans, d_ls, p_end


def _rwkv_kernel(r_ref, lw_ref, k_ref, v_ref, kk_ref, kka_ref, tri_ref, wg_ref, wu_ref, wd_ref,
                 o_ref, wg_bf_ref, wu_bf_ref, wd_bf_ref, state_ref, *, n_sub):
    @pl.when(pl.program_id(1) == 0)
    def _():
        state_ref[...] = jnp.zeros_like(state_ref)

    wg_bf_ref[...] = _bf(wg_ref[...])
    wu_bf_ref[...] = _bf(wu_ref[...])
    wd_bf_ref[...] = _bf(wd_ref[...])

    t_idx = lax.broadcasted_iota(jnp.int32, (CHUNK, GROUP), 0)
    s_idx = lax.broadcasted_iota(jnp.int32, (CHUNK, GROUP), 1) % CHUNK
    bd_mask = (lax.broadcasted_iota(jnp.int32, (GROUP, GROUP), 0) // CHUNK
               == lax.broadcasted_iota(jnp.int32, (GROUP, GROUP), 1) // CHUNK)
    n_group = D_RWKV // GROUP
    where = [(slice(c * CHUNK, (c + 1) * CHUNK), slice(g * GROUP, (g + 1) * GROUP))
             for c in range(n_sub) for g in range(n_group)]
    load = lambda ref: [ref[0, rows, lanes].astype(F32) for rows, lanes in where]
    r_hat, o_loc, trans, d_ls, p_end = _chunk_local(
        load(r_ref), load(lw_ref), load(k_ref), load(v_ref), load(kk_ref), load(kka_ref),
        tri_ref[...], t_idx, s_idx, bd_mask)

    state = [state_ref[g] for g in range(n_group)]
    for c in range(n_sub):
        chains = range(c * n_group, (c + 1) * n_group)
        out = [_mm_nt(r_hat[i], _block_diag(state[g], bd_mask)) + o_loc[i] for g, i in enumerate(chains)]
        for g, i in enumerate(chains):
            rows, lanes = where[i]
            o_ref[0, rows, lanes] = out[g]
        state = [state[g] * p_end[i] + _mm(state[g], trans[i]) + d_ls[i] for g, i in enumerate(chains)]
    for g in range(n_group):
        state_ref[g] = state[g]


def _rwkv_call(r, lw, k, v, kk, kka, tri, expert_weights, tb):
    bsz, seq, _ = r.shape
    n_t = seq // tb
    per_step = N_EXPERTS // (bsz * n_t)
    assert per_step * bsz * n_t == N_EXPERTS, "grid steps must divide the expert count"
    spec = pl.BlockSpec((1, tb, D_RWKV), lambda b, t: (b, t, 0))
    w_specs = [pl.BlockSpec((per_step,) + w.shape[1:], lambda b, t: (b * n_t + t, 0, 0)) for w in expert_weights]
    return pl.pallas_call(
        functools.partial(_rwkv_kernel, n_sub=tb // CHUNK),
        grid=(bsz, n_t),
        in_specs=[spec] * 6 + [pl.BlockSpec(tri.shape, lambda b, t: (0, 0))] + w_specs,
        out_specs=[spec] + w_specs,
        out_shape=[jax.ShapeDtypeStruct((bsz, seq, D_RWKV), F32)]
        + [jax.ShapeDtypeStruct(w.shape, BF16) for w in expert_weights],
        scratch_shapes=[pltpu.VMEM((D_RWKV // GROUP, HEAD, GROUP), F32)],
        compiler_params=pltpu.CompilerParams(
            dimension_semantics=("arbitrary", "arbitrary"), vmem_limit_bytes=VMEM_LIMIT),
        name="rwkv_chunk",
    )(r, lw, k, v, kk, kka, tri, *expert_weights)


def _route(logits):
    lane_i = lax.broadcasted_iota(jnp.int32, logits.shape, 1)
    lane = lane_i.astype(F32)
    lane_group = (lane_i // EXPERTS_PER_GROUP).astype(F32)
    neg = -jnp.inf
    big = float(ROUTER_LANES)
    is_group = (lane_i >= N_EXPERTS) & (lane_i < N_EXPERTS + N_GROUPS)
    gl = jnp.where(is_group, logits, neg)
    ge = jnp.exp(gl - jnp.max(gl, axis=-1, keepdims=True))
    gprob = ge / jnp.sum(ge, axis=-1, keepdims=True)
    g_top = jnp.max(gprob, axis=-1, keepdims=True)
    g_idx = jnp.min(jnp.where(is_group & (gprob == g_top), lane - N_EXPERTS, big), axis=-1, keepdims=True)

    in_group = (lane_i < N_EXPERTS) & (lane_group == g_idx)
    el = jnp.where(in_group, logits, neg)
    top1 = jnp.max(el, axis=-1, keepdims=True)
    idx1 = jnp.min(jnp.where(in_group & (el == top1), lane, big), axis=-1, keepdims=True)
    el2 = jnp.where(lane == idx1, neg, el)
    top2 = jnp.max(el2, axis=-1, keepdims=True)
    idx2 = jnp.min(jnp.where(in_group & (lane != idx1) & (el2 == top2), lane, big), axis=-1, keepdims=True)
    e2 = jnp.exp(top2 - top1)
    denom = 1.0 + e2
    return idx1, idx2, g_top * (1.0 / denom), g_top * (e2 / denom)


def _pack_bf16_pairs(x):
    bits = lax.bitcast_convert_type(_bf(x).astype(F32), jnp.uint32)
    k = x.shape[1] // 2
    return lax.bitcast_convert_type((bits[:, :k] >> 16) | bits[:, k:], jnp.int32)


def _unpack_bf16_pairs(words):
    bits = lax.bitcast_convert_type(words, jnp.uint32)
    lo = lax.bitcast_convert_type(bits << 16, F32)
    hi = lax.bitcast_convert_type(bits & jnp.uint32(0xFFFF0000), F32)
    return _bf(jnp.concatenate([lo, hi], axis=1))


LANE_IDX1, LANE_IDX2, LANE_RANK1, LANE_RANK2, LANE_GATE1, LANE_GATE2 = range(6)


def _out_kernel(o_ref, bonus_ref, gate_ref, yconv_ref, x_ref, ones_ref, lnw_ref, lnb_ref,
                wout_ref, gffn_ref, rw_ref, rb_ref, tri_ref, h_ref, u_ref, route_ref, route_t_ref, count_ref,
                seen_ref, wbf_ref, *, n_split):
    @pl.when(pl.program_id(0) == 0)
    def _():
        seen_ref[...] = jnp.zeros_like(seen_ref)
        wbf_ref[...] = _bf(wout_ref[...])

    sub = o_ref.shape[0] // n_split
    parts = [slice(s * sub, (s + 1) * sub) for s in range(n_split)]
    read = lambda ref: [ref[p, :] for p in parts]
    ones_quad = ones_ref[...]
    inv_n = 1.0 / HEAD
    o = read(o_ref)
    mean = _each(lambda v: _head_sum(v, ones_quad) * inv_n, o)
    cen = _each(jnp.subtract, o, mean)
    var = _each(lambda c: _head_sum(c * c, ones_quad) * inv_n, cen)
    on = _each(lambda c, v: c * lax.rsqrt(v + LN_X_EPS) * lnw_ref[...] + lnb_ref[...], cen, var)
    y_rwkv = _each(lambda a, b, g: (a + b) * g, on, read(bonus_ref), read(gate_ref))
    mixed = _each(lambda yc, yr: (jnp.dot(yc, wbf_ref[:D_CONV, :], preferred_element_type=F32)
                                  + jnp.dot(_bf(yr), wbf_ref[D_CONV:, :], preferred_element_type=F32)),
                  read(yconv_ref), y_rwkv)
    h = _each(jnp.add, read(x_ref), mixed)
    u = _each(lambda v: _rms_norm(v, gffn_ref[...]), h)
    u_hi = _each(_bf, u)
    u_lo = _each(lambda a, b: _bf(a - b.astype(F32)), u, u_hi)
    by_hi = _each(lambda a: jnp.dot(a, rw_ref[...], preferred_element_type=F32), u_hi)
    by_lo = _each(lambda a: jnp.dot(a, rw_ref[:, :ROUTER_LANES], preferred_element_type=F32), u_lo)
    logits = _each(lambda a, b: a[:, :ROUTER_LANES] + a[:, ROUTER_LANES:] + b + rb_ref[...], by_hi, by_lo)
    routed = _each(_route, logits)

    lane_i = lax.broadcasted_iota(jnp.int32, (sub, ROUTER_LANES), 1)
    lane = lane_i.astype(F32)
    hit1 = _each(lambda rt: lane == rt[0], routed)
    hit2 = _each(lambda rt: lane == rt[1], routed)
    both = _each(lambda a, b: a.astype(F32) + b.astype(F32), hit1, hit2)
    inside = _each(lambda b: jnp.dot(tri_ref[...], _bf(b), preferred_element_type=F32), both)
    seen = seen_ref[...]
    for s, p in enumerate(parts):
        idx1, idx2, gate1, gate2 = routed[s]
        before = inside[s] + seen
        rank1 = jnp.sum(jnp.where(hit1[s], before, 0.0), axis=-1, keepdims=True)
        rank2 = jnp.sum(jnp.where(hit2[s], before, 0.0), axis=-1, keepdims=True)
        seen = seen + jnp.sum(both[s], axis=0, keepdims=True)
        route = jnp.zeros((sub, ROUTER_LANES), F32)
        for lane_id, col in ((LANE_IDX1, idx1), (LANE_IDX2, idx2), (LANE_RANK1, rank1),
                             (LANE_RANK2, rank2), (LANE_GATE1, gate1), (LANE_GATE2, gate2)):
            route = jnp.where(lane_i == lane_id, col, route)
        route_ref[p, :] = route
        route_t_ref[:, p] = route.T[:8, :]
        h_ref[p, :] = _bf(h[s])
        u_ref[p, :] = _pack_bf16_pairs(u[s])
    seen_ref[...] = seen
    count_ref[...] = jnp.broadcast_to(seen, count_ref.shape)


def _out_call(o, bonus, gate, yconv, x, ones_bd, ln_w, ln_b, w_out, g_ffn, router_w, router_b, tm, n_split):
    n_tok = x.shape[0]
    row = lambda width: pl.BlockSpec((tm, width), lambda i: (i, 0))
    full = lambda arr: pl.BlockSpec(arr.shape, lambda i: (0,) * arr.ndim, pipeline_mode=pl.Buffered(1))
    sub = tm // n_split
    tri_strict = (jnp.arange(sub)[:, None] > jnp.arange(sub)[None, :]).astype(BF16)
    params = (ones_bd, ln_w, ln_b, w_out, g_ffn, router_w, router_b, tri_strict)
    return pl.pallas_call(
        functools.partial(_out_kernel, n_split=n_split),
        grid=(n_tok // tm,),
        in_specs=[row(D_RWKV)] * 4 + [row(D_MODEL)] + [full(p) for p in params],
        out_specs=[row(D_MODEL), row(D_MODEL // 2), row(ROUTER_LANES),
                   pl.BlockSpec((8, tm), lambda i: (0, i)),
                   pl.BlockSpec((8, ROUTER_LANES), lambda i: (0, 0))],
        out_shape=[jax.ShapeDtypeStruct((n_tok, D_MODEL), BF16),
                   jax.ShapeDtypeStruct((n_tok, D_MODEL // 2), jnp.int32),
                   jax.ShapeDtypeStruct((n_tok, ROUTER_LANES), F32),
                   jax.ShapeDtypeStruct((8, n_tok), F32),
                   jax.ShapeDtypeStruct((8, ROUTER_LANES), F32)],
        scratch_shapes=[pltpu.VMEM((1, ROUTER_LANES), F32), pltpu.VMEM(w_out.shape, BF16)],
        compiler_params=pltpu.CompilerParams(
            dimension_semantics=("arbitrary",), vmem_limit_bytes=VMEM_LIMIT),
        name="out_proj_route",
    )(o, bonus, gate, yconv, x, *params)


SC_CORES = 2
SC_SUBCORES = 16
SC_ROWS = 64


def _sc_mesh():
    return plsc.VectorSubcoreMesh(core_axis_name="c", subcore_axis_name="s",
                                  num_cores=SC_CORES, num_subcores=SC_SUBCORES)


def _sc_worker():
    return lax.axis_index("s") * SC_CORES + lax.axis_index("c")


def _sc_gather(table, idx):
    n_rows = idx.shape[0]
    width = table.shape[1]
    n_chunks = n_rows // (SC_CORES * SC_SUBCORES * SC_ROWS)

    def body(table_hbm, idx_hbm, out_hbm, idx_v, rows_v, gather_sem, write_sem):
        first = _sc_worker() * n_chunks
        pltpu.sync_copy(idx_hbm.at[pl.ds(first, n_chunks)], idx_v)
        gather = lambda j: pltpu.async_copy(table_hbm.at[idx_v.at[j]], rows_v.at[j % 2], gather_sem.at[j % 2])
        gathers = [gather(0)]
        writes = []
        for j in range(n_chunks):
            gathers[j].wait()
            if j + 1 < n_chunks:
                if j >= 1:
                    writes[j - 1].wait()
                gathers.append(gather(j + 1))
            dst = out_hbm.at[pl.ds(pl.multiple_of((first + j) * SC_ROWS, SC_ROWS), SC_ROWS)]
            writes.append(pltpu.async_copy(rows_v.at[j % 2], dst, write_sem.at[j % 2]))
        for j in range(max(n_chunks - 2, 0), n_chunks):
            writes[j].wait()

    return pl.kernel(
        body,
        out_type=jax.ShapeDtypeStruct((n_rows, width), table.dtype),
        mesh=_sc_mesh(),
        scratch_types=[pltpu.VMEM((n_chunks, SC_ROWS), jnp.int32), pltpu.VMEM((2, SC_ROWS, width), table.dtype),
                       pltpu.SemaphoreType.DMA((2,)), pltpu.SemaphoreType.DMA((2,))],
        name="sc_row_gather",
    )(table, idx.reshape(n_rows // SC_ROWS, SC_ROWS))


def _sc_scatter(rows, pos, n_out):
    n_rows, width = rows.shape
    n_slots = pos.shape[0] // n_rows
    slot_chunks = n_rows // SC_ROWS
    n_chunks = slot_chunks // (SC_CORES * SC_SUBCORES)

    def body(rows_hbm, pos_hbm, out_hbm, idx_v, rows_v, read_sem, scatter_sem):
        first = _sc_worker() * n_chunks
        for s in range(n_slots):
            pltpu.sync_copy(pos_hbm.at[pl.ds(s * slot_chunks + first, n_chunks)], idx_v.at[s])
        read = lambda j: pltpu.async_copy(
            rows_hbm.at[pl.ds(pl.multiple_of((first + j) * SC_ROWS, SC_ROWS), SC_ROWS)],
            rows_v.at[j % 2], read_sem.at[j % 2])
        reads = [read(0)]
        scatters = []
        for j in range(n_chunks):
            reads[j].wait()
            if j + 1 < n_chunks:
                if j >= 1:
                    for copy in scatters[j - 1]:
                        copy.wait()
                reads.append(read(j + 1))
            scatters.append([pltpu.async_copy(rows_v.at[j % 2], out_hbm.at[idx_v.at[s, j]], scatter_sem.at[j % 2])
                             for s in range(n_slots)])
        for j in range(max(n_chunks - 2, 0), n_chunks):
            for copy in scatters[j]:
                copy.wait()

    return pl.kernel(
        body,
        out_type=jax.ShapeDtypeStruct((n_out, width), rows.dtype),
        mesh=_sc_mesh(),
        scratch_types=[pltpu.VMEM((n_slots, n_chunks, SC_ROWS), jnp.int32),
                       pltpu.VMEM((2, SC_ROWS, width), rows.dtype),
                       pltpu.SemaphoreType.DMA((2,)), pltpu.SemaphoreType.DMA((2,))],
        name="sc_row_scatter",
    )(rows, pos.reshape(n_slots * slot_chunks, SC_ROWS))


def _expert_kernel(tile_expert_ref, n_valid_ref, tile_rows_ref, x_ref, wg_ref, wu_ref, wd_ref, y_ref):
    del tile_expert_ref
    step = pl.program_id(0)
    half = x_ref.shape[0] // 2

    @pl.when(step < n_valid_ref[0])
    def _():
        w_gate, w_up, w_down = wg_ref[0], wu_ref[0], wd_ref[0]

        def run(rows):
            x = _unpack_bf16_pairs(x_ref[rows, :])
            gate = jnp.dot(x, w_gate, preferred_element_type=F32)
            up = jnp.dot(x, w_up, preferred_element_type=F32)
            hid = gate * jax.nn.sigmoid(gate) * up
            y_ref[rows, :] = _pack_bf16_pairs(jnp.dot(_bf(hid), w_down, preferred_element_type=F32))

        run(slice(0, half))

        @pl.when(tile_rows_ref[step] > half)
        def _():
            run(slice(half, 2 * half))


def _expert_call(tile_expert, n_valid, tile_rows, x_sorted, w_gate, w_up, w_down, tm):
    n_rows = x_sorted.shape[0]
    rows = pl.BlockSpec((tm, D_MODEL // 2), lambda i, te, nv, tr: (jnp.minimum(i, nv[0] - 1), 0))
    by_expert = lambda i, te, nv, tr: (te[i], 0, 0)
    return pl.pallas_call(
        _expert_kernel,
        grid_spec=pltpu.PrefetchScalarGridSpec(
            num_scalar_prefetch=3,
            grid=(n_rows // tm,),
            in_specs=[rows,
                      pl.BlockSpec((1, D_MODEL, D_EXPERT), by_expert),
                      pl.BlockSpec((1, D_MODEL, D_EXPERT), by_expert),
                      pl.BlockSpec((1, D_EXPERT, D_MODEL), by_expert)],
            out_specs=rows),
        out_shape=jax.ShapeDtypeStruct((n_rows, D_MODEL // 2), jnp.int32),
        compiler_params=pltpu.CompilerParams(
            dimension_semantics=("arbitrary",), vmem_limit_bytes=VMEM_LIMIT),
        name="moe_experts",
    )(tile_expert, n_valid, tile_rows, x_sorted, w_gate, w_up, w_down)


def _final_kernel(h_ref, y1_ref, y2_ref, route_ref, gfin_ref, *rest):
    out_ref = rest[-1]
    route = route_ref[...]
    gate1 = route[:, LANE_GATE1:LANE_GATE1 + 1]
    gate2 = route[:, LANE_GATE2:LANE_GATE2 + 1]
    moe = gate1 * _unpack_bf16_pairs(y1_ref[...]).astype(F32) + gate2 * _unpack_bf16_pairs(y2_ref[...]).astype(F32)
    out_ref[...] = _rms_norm(h_ref[...] + moe, gfin_ref[...])


def _final_call(h, y_pairs, route, g_final, earlier, chunk, n_chunks, tm):
    n_tok = h.shape[0]
    n_blocks = n_tok // n_chunks // tm
    first = chunk * n_blocks
    in_specs = [pl.BlockSpec((tm, D_MODEL), lambda i: (first + i, 0)),
                pl.BlockSpec((tm, D_MODEL // 2), lambda i: (i, 0)),
                pl.BlockSpec((tm, D_MODEL // 2), lambda i: (i + n_blocks, 0)),
                pl.BlockSpec((tm, ROUTER_LANES), lambda i: (first + i, 0)),
                pl.BlockSpec((1, D_MODEL), lambda i: (0, 0))]
    args = [h, y_pairs, y_pairs, route, g_final]
    aliases = {}
    if earlier is not None:
        in_specs.append(pl.BlockSpec(memory_space=pl.ANY))
        args.append(earlier)
        aliases = {len(args) - 1: 0}
    return pl.pallas_call(
        _final_kernel,
        grid=(n_blocks,),
        in_specs=in_specs,
        out_specs=pl.BlockSpec((tm, D_MODEL), lambda i: (first + i, 0)),
        out_shape=jax.ShapeDtypeStruct((n_tok, D_MODEL), F32),
        input_output_aliases=aliases,
        compiler_params=pltpu.CompilerParams(
            dimension_semantics=("arbitrary",), vmem_limit_bytes=VMEM_LIMIT),
        name="moe_combine_norm",
    )(*args)


def _dispatch_plan(route_t, counts, tm):
    n_tok = route_t.shape[1]
    n_tiles = (2 * n_tok) // tm + N_EXPERTS
    counts = counts[0, :N_EXPERTS].astype(jnp.int32)
    tiles_per = (counts + tm - 1) // tm
    tile_end = jnp.cumsum(tiles_per)
    row_start = (tile_end - tiles_per) * tm
    experts = jnp.arange(N_EXPERTS, dtype=jnp.int32)

    def position(idx_lane, rank_lane):
        idx = route_t[idx_lane].astype(jnp.int32)
        start = jnp.sum(jnp.where(idx[None, :] == experts[:, None], row_start[:, None], 0), axis=0)
        return start + route_t[rank_lane].astype(jnp.int32)

    pos = jnp.concatenate([position(LANE_IDX1, LANE_RANK1), position(LANE_IDX2, LANE_RANK2)])
    n_valid = tile_end[-1:]
    tile = jnp.minimum(jnp.arange(n_tiles, dtype=jnp.int32), n_valid - 1)
    tile_expert = jnp.sum((tile_end[None, :] <= tile[:, None]).astype(jnp.int32), axis=1)
    of_tile = lambda per_expert: jnp.sum(jnp.where(tile_expert[:, None] == experts[None, :], per_expert[None, :], 0), axis=1)
    tile_rows = jnp.clip(of_tile(counts) - (tile - of_tile(tile_end - tiles_per)) * tm, 0, tm)
    return pos, n_tiles, tile_expert, n_valid, tile_rows


def _block(x, norm_mix_g, w_in, rwkv_mu, conv_w, decay_up, decay_base, aaa_up, aaa_base, gate_up,
           k_k, k_a, r_k, ln_x_w, ln_x_b, w_out, norm_ffn_g, router_group_w, router_group_b,
           router_expert_w, router_expert_b, expert_w_gate, expert_w_up, expert_w_down, norm_final_g,
           *, tm_in, tb_rwkv, tm_out, tm_expert, tm_final):
    bsz, seq, d_model = x.shape
    n_tok = bsz * seq
    row = lambda p: p.reshape(1, -1)
    half = LORA_WA // 2
    zeros = jnp.zeros((half, D_RWKV), F32)
    dup = _bf(jnp.concatenate([decay_up, zeros], axis=0))
    aup = _bf(jnp.concatenate([zeros, aaa_up], axis=0))
    head_of = jnp.arange(QUAD) // HEAD
    ones_quad = (head_of[:, None] == head_of[None, :]).astype(BF16)
    tri = (jnp.arange(CHUNK)[:, None] >= jnp.arange(CHUNK)[None, :]).astype(BF16)

    yconv, r, lw, k2, v, kk, kka, gate, bonus = _in_call(
        x, row(norm_mix_g), w_in, row(rwkv_mu), conv_w, dup, row(decay_base), aup, row(aaa_base),
        _bf(gate_up), row(k_k), row(k_a), row(r_k), ones_quad, tm_in)
    o, w_gate_bf, w_up_bf, w_down_bf = _rwkv_call(r, lw, k2, v, kk, kka, tri,
                                                  (expert_w_gate, expert_w_up, expert_w_down), tb_rwkv)

    pad = ROUTER_LANES - N_EXPERTS - N_GROUPS
    router_w = jnp.concatenate([router_expert_w, router_group_w, jnp.zeros((d_model, pad), F32)], axis=1)
    router_hi = _bf(router_w)
    router_split = jnp.concatenate([router_hi, _bf(router_w - router_hi.astype(F32))], axis=1)
    router_b = jnp.concatenate([router_expert_b, router_group_b, jnp.zeros((pad,), F32)]).reshape(1, -1)
    flat = lambda t: t.reshape(n_tok, t.shape[-1])
    h, u_pairs, route, route_t, counts = _out_call(
        flat(o), flat(bonus), flat(gate), flat(yconv), flat(x), ones_quad,
        row(ln_x_w), row(ln_x_b), w_out, row(norm_ffn_g), router_split, router_b, tm_out, OUT_SPLIT)

    pos, n_tiles, tile_expert, n_valid, tile_rows = _dispatch_plan(route_t, counts, tm_expert)
    x_sorted = _sc_scatter(u_pairs, pos, n_tiles * tm_expert)
    y_sorted = _expert_call(tile_expert, n_valid, tile_rows, x_sorted, w_gate_bf, w_up_bf, w_down_bf, tm_expert)
    per_chunk = n_tok // COMBINE_CHUNKS
    out = None
    for c in range(COMBINE_CHUNKS):
        lo = c * per_chunk
        pos_c = jnp.concatenate([pos[lo:lo + per_chunk], pos[n_tok + lo:n_tok + lo + per_chunk]])
        out = _final_call(h, _sc_gather(y_sorted, pos_c), route, row(norm_final_g), out, c, COMBINE_CHUNKS, tm_final)
    return out.reshape(bsz, seq, d_model)


def kernel(x, norm_mix_g, w_in, rwkv_mu, conv_w, decay_up, decay_base, aaa_up, aaa_base, gate_up, k_k, k_a, r_k, ln_x_w, ln_x_b, w_out, norm_ffn_g, router_group_w, router_group_b, router_expert_w, router_expert_b, expert_w_gate, expert_w_up, expert_w_down, norm_final_g):
    return _block(x, norm_mix_g[0], w_in[0], rwkv_mu[0], conv_w[0], decay_up[0], decay_base[0],
                  aaa_up[0], aaa_base[0], gate_up[0], k_k[0], k_a[0], r_k[0].reshape(-1), ln_x_w[0],
                  ln_x_b[0], w_out[0], norm_ffn_g[0], router_group_w[0], router_group_b[0],
                  router_expert_w[0], router_expert_b[0], expert_w_gate[0], expert_w_up[0],
                  expert_w_down[0], norm_final_g,
                  tm_in=512, tb_rwkv=512, tm_out=512, tm_expert=1024, tm_final=512)
```

```python
import functools

import jax
import jax.numpy as jnp
from jax import lax
from jax.experimental import pallas as pl
from jax.experimental.pallas import tpu as pltpu
from jax.experimental.pallas import tpu_sc as plsc

F32 = jnp.float32
BF16 = jnp.bfloat16

D_MODEL = 1024
D_CONV = 512
CONV_WIDTH = 3
N_HEADS = 8
HEAD = 64
D_RWKV = N_HEADS * HEAD
LORA_WA = 128
GATE_LORA = 128
D_RWKV_PROJ = 3 * D_RWKV + LORA_WA + GATE_LORA
D_IN = 3 * D_CONV + D_RWKV_PROJ
N_GROUPS = 4
EXPERTS_PER_GROUP = 8
N_EXPERTS = N_GROUPS * EXPERTS_PER_GROUP
D_EXPERT = D_MODEL // 4
RMS_EPS = 1e-6
LN_X_EPS = 64e-5
L2_EPS = 1e-12

LANES = 128
CHUNK = 64
QUAD = 4 * HEAD
GROUP = 2 * HEAD
COMPACT_FROM = 64
OUT_SPLIT = 4
COMBINE_CHUNKS = 4
ROUTER_LANES = 128

VMEM_LIMIT = 56 * 1024 * 1024


def _bf(x):
    return x.astype(BF16)


def _mm(a, b):
    return jnp.dot(_bf(a), _bf(b), preferred_element_type=F32)


def _mm_nt(a, b):
    return lax.dot_general(_bf(a), _bf(b), (((1,), (1,)), ((), ())), preferred_element_type=F32)


def _mm_exact_lhs(lhs_bf16, x, passes):
    acc = None
    rem = x
    for _ in range(passes):
        piece = _bf(rem)
        part = jnp.dot(lhs_bf16, piece, preferred_element_type=F32)
        acc = part if acc is None else acc + part
        rem = rem - piece.astype(F32)
    return acc


def _head_sum(x, ones_quad):
    xb = _bf(x)
    return jnp.concatenate(
        [jnp.dot(xb[:, q * QUAD:(q + 1) * QUAD], ones_quad, preferred_element_type=F32)
         for q in range(x.shape[1] // QUAD)], axis=1)


def _rms_norm(x, g):
    return x * lax.rsqrt(jnp.mean(x * x, axis=-1, keepdims=True) + RMS_EPS) * g


def _shift_rows(cur, prev_rows, k):
    rolled = pltpu.roll(cur, k, 0)
    prev_rolled = pltpu.roll(prev_rows, k, 0)
    n = cur.shape[0]
    head = jnp.concatenate([prev_rolled, rolled[8:]], axis=0) if n > 8 else prev_rolled
    row = lax.broadcasted_iota(jnp.int32, cur.shape, 0)
    return jnp.where(row < k, head, rolled)


def _in_kernel(x_ref, g_ref, w_ref, mu_ref, convw_ref, dup_ref, dbase_ref, aup_ref, abase_ref,
               gup_ref, kk_ref, ka_ref, rk_ref, ones_ref,
               yconv_ref, r_ref, lw_ref, k_ref, v_ref, kkn_ref, kka_ref, gate_ref, bonus_ref,
               carry_ref, wbf_ref):
    @pl.when((pl.program_id(0) == 0) & (pl.program_id(1) == 0))
    def _():
        wbf_ref[...] = _bf(w_ref[...])

    @pl.when(pl.program_id(1) == 0)
    def _():
        carry_ref[...] = jnp.zeros_like(carry_ref)

    u = _bf(_rms_norm(x_ref[0], g_ref[...]))
    n_conv = 3 * D_CONV
    rk0 = n_conv
    bounds = dict(conv=(0, n_conv), lora=(rk0 + 3 * D_RWKV, D_IN), k=(rk0 + D_RWKV, rk0 + 2 * D_RWKV),
                  r=(rk0, rk0 + D_RWKV), v=(rk0 + 2 * D_RWKV, rk0 + 3 * D_RWKV))
    z = {name: jnp.dot(u, wbf_ref[:, lo:hi], preferred_element_type=F32) for name, (lo, hi) in bounds.items()}

    def lerp(name):
        lo, hi = bounds[name]
        cur = z[name]
        prev = carry_ref[:, D_CONV + lo - n_conv:D_CONV + hi - n_conv]
        mixed = cur + (_shift_rows(cur, prev, 1) - cur) * mu_ref[:, lo - n_conv:hi - n_conv]
        carry_ref[:, D_CONV + lo - n_conv:D_CONV + hi - n_conv] = cur[-8:]
        return mixed

    b_gate = z["conv"][:, :D_CONV]
    ch = z["conv"][:, D_CONV:2 * D_CONV] * z["conv"][:, 2 * D_CONV:]
    prev_ch = carry_ref[:, :D_CONV]
    conv = (convw_ref[2:3, :] * ch
            + convw_ref[1:2, :] * _shift_rows(ch, prev_ch, 1)
            + convw_ref[0:1, :] * _shift_rows(ch, prev_ch, 2))
    yconv_ref[0] = _bf(b_gate * conv)
    carry_ref[:, :D_CONV] = ch[-8:]

    lora = lerp("lora")
    wa_lo = lora[:, :LORA_WA]
    g_lo = lora[:, LORA_WA:]
    dec_in = -(dbase_ref[...] + jnp.dot(_bf(jnp.tanh(wa_lo)), dup_ref[...], preferred_element_type=F32))
    softplus = jnp.maximum(dec_in, 0.0) + jnp.log(1.0 + jnp.exp(-jnp.abs(dec_in)))
    w = -softplus - 0.5
    lw_ref[0] = -jnp.exp(w)
    a = jax.nn.sigmoid(abase_ref[...] + jnp.dot(_bf(wa_lo), aup_ref[...], preferred_element_type=F32))
    gate_ref[0] = _bf(jnp.dot(_bf(jax.nn.sigmoid(g_lo)), gup_ref[...], preferred_element_type=F32))

    ones_quad = ones_ref[...]
    k = lerp("k")
    kk = k * kk_ref[...]
    norm = jnp.sqrt(_head_sum(kk * kk, ones_quad))
    kk = kk / jnp.maximum(norm, L2_EPS)
    k2 = k * (1.0 + (a - 1.0) * ka_ref[...])
    k_ref[0] = _bf(k2)
    kkn_ref[0] = _bf(kk)
    kka_ref[0] = _bf(kk * a)
    r = lerp("r")
    r_ref[0] = _bf(r)
    v = lerp("v")
    v_ref[0] = _bf(v)
    bonus_ref[0] = _bf(_head_sum(r * k2 * rk_ref[...], ones_quad) * v)


def _in_call(x, g, w_in, mu, conv_w, dup, dbase, aup, abase, gup, k_k, k_a, r_k, ones_quad, tm):
    bsz, seq, _ = x.shape
    full = lambda arr: pl.BlockSpec(arr.shape, lambda b, t: (0,) * arr.ndim, pipeline_mode=pl.Buffered(1))
    out_spec = pl.BlockSpec((1, tm, D_RWKV), lambda b, t: (b, t, 0))
    out_dtypes = (BF16, BF16, F32, BF16, BF16, BF16, BF16, BF16, BF16)
    params = (g, w_in, mu, conv_w, dup, dbase, aup, abase, gup, k_k, k_a, r_k, ones_quad)
    return pl.pallas_call(
        _in_kernel,
        grid=(bsz, seq // tm),
        in_specs=[pl.BlockSpec((1, tm, D_MODEL), lambda b, t: (b, t, 0))] + [full(p) for p in params],
        out_specs=[out_spec] * 9,
        out_shape=[jax.ShapeDtypeStruct((bsz, seq, D_RWKV), dt) for dt in out_dtypes],
        scratch_shapes=[pltpu.VMEM((8, D_CONV + D_RWKV_PROJ), F32), pltpu.VMEM(w_in.shape, BF16)],
        compiler_params=pltpu.CompilerParams(
            dimension_semantics=("arbitrary", "arbitrary"), vmem_limit_bytes=VMEM_LIMIT),
        name="in_proj",
    )(x, *params)


def _block_diag(y, bd_mask):
    return jnp.where(bd_mask, jnp.concatenate([y] * (GROUP // CHUNK), axis=0), 0.0)


def _each(fn, *lists):
    return [fn(*args) for args in zip(*lists)]


def _unit_lower_inverse(a_strict, t_idx, s_idx, bd):
    bdmm = lambda xs, ys: _each(lambda x, y: _mm(x, bd(y)), xs, ys)
    eye = (t_idx == s_idx).astype(F32)
    same8 = (t_idx // 8) == (s_idx // 8)
    a8 = _each(lambda a: jnp.where(same8, a, 0.0), a_strict)
    a8_2 = bdmm(a8, a8)
    a8_34 = bdmm(_each(lambda a, b: jnp.concatenate([a, b], axis=0), a8, a8_2), a8_2)
    inv = _each(lambda a, b, c: eye + a + b + c[:CHUNK], a8, a8_2, a8_34)
    inv = _each(jnp.add, inv, bdmm(inv, _each(lambda c: c[CHUNK:], a8_34)))
    size = 16
    while size < COMPACT_FROM:
        off = ((t_idx // size) == (s_idx // size)) & ((t_idx // (size // 2)) != (s_idx // (size // 2)))
        cross = bdmm(_each(lambda a: jnp.where(off, a, 0.0), a_strict), inv)
        inv = _each(jnp.add, inv, bdmm(inv, cross))
        size *= 2
    while size <= CHUNK:
        inv = _compact_level(a_strict, inv, size)
        size *= 2
    return inv


def _compact_level(a_strict, inv, size):
    half = size // 2
    heads = GROUP // CHUNK
    blocks = CHUNK // size
    lane_starts = [h * CHUNK + b * size for h in range(heads) for b in range(blocks)]
    row_starts = [b * size for b in range(blocks)]
    rows_first = lambda x: jnp.concatenate([x[r:r + half] for r in row_starts], axis=0)
    rows_second = lambda x: jnp.concatenate([x[r + half:r + size] for r in row_starts], axis=0)
    lanes_first = lambda x: jnp.concatenate([x[:, c:c + half] for c in lane_starts], axis=1)
    lanes_second = lambda x: jnp.concatenate([x[:, c + half:c + size] for c in lane_starts], axis=1)
    n_rows, n_lanes = CHUNK // 2, GROUP // 2
    own = (lax.broadcasted_iota(jnp.int32, (n_rows, n_lanes), 0) // half
           == (lax.broadcasted_iota(jnp.int32, (n_rows, n_lanes), 1) // half) % blocks)
    diag = (lax.broadcasted_iota(jnp.int32, (n_lanes, n_lanes), 0) // half
            == lax.broadcasted_iota(jnp.int32, (n_lanes, n_lanes), 1) // half)
    bd_half = lambda y: jnp.where(diag, jnp.concatenate([y] * heads, axis=0), 0.0)
    a21 = _each(lambda a: jnp.where(own, lanes_first(rows_second(a)), 0.0), a_strict)
    t11 = _each(lambda t: lanes_first(rows_first(t)), inv)
    t22 = _each(lambda t: lanes_second(rows_second(t)), inv)
    a21_t11 = _each(lambda x, y: _mm(x, bd_half(y)), a21, t11)
    new = _each(lambda x, y: _mm(x, bd_half(y)), t22, a21_t11)

    def placed(x):
        zero_lanes = jnp.zeros((n_rows, half), F32)
        wide = jnp.concatenate(sum(([x[:, i * half:(i + 1) * half], zero_lanes] for i in range(heads * blocks)), []),
                               axis=1)
        zero_rows = jnp.zeros((half, GROUP), F32)
        return jnp.concatenate(sum(([zero_rows, wide[b * half:(b + 1) * half]] for b in range(blocks)), []), axis=0)

    return _each(lambda t, x: t + placed(x), inv, new)


def _chunk_local(r, lw, k, v, kk, kka, tri, t_idx, s_idx, bd_mask):
    bd = lambda y: _block_diag(y, bd_mask)
    bdmm = lambda xs, ys: _each(lambda x, y: _mm(x, bd(y)), xs, ys)
    bdmm2 = lambda xs, ys, zs: _each(lambda x, y, z: _mm(x, jnp.concatenate([bd(y), bd(z)], axis=1)), xs, ys, zs)
    left = lambda xs: _each(lambda x: x[:, :GROUP], xs)
    right = lambda xs: _each(lambda x: x[:, GROUP:], xs)
    top = lambda xs: _each(lambda x: x[:CHUNK], xs)
    bottom = lambda xs: _each(lambda x: x[CHUNK:], xs)

    cum = _each(lambda x: _mm_exact_lhs(tri, x, 3), lw)
    cum_last = _each(lambda c: c[CHUNK - 1:CHUNK, :], cum)
    p_incl = _each(jnp.exp, cum)
    p_excl = _each(lambda c, x: jnp.exp(c - x), cum, lw)
    p_inv = _each(lambda c: jnp.exp(-c), cum)
    to_end = _each(lambda cl, c: jnp.exp(cl - c), cum_last, cum)
    a_t = _each(lambda x, p: -x * p, kk, p_excl)
    r_t = _each(jnp.multiply, r, p_incl)
    b_t = _each(jnp.multiply, kka, p_inv)
    k_t = _each(jnp.multiply, k, p_inv)
    b_end = _each(jnp.multiply, kka, to_end)
    k_end = _each(jnp.multiply, k, to_end)

    ar = _each(lambda a, b: jnp.concatenate([a, b], axis=0), a_t, r_t)
    row2 = lax.broadcasted_iota(jnp.int32, (2 * CHUNK, 2 * GROUP), 0)
    col2 = lax.broadcasted_iota(jnp.int32, (2 * CHUNK, 2 * GROUP), 1) % CHUNK
    causal = col2 < (row2 % CHUNK) + (row2 // CHUNK)
    scores = _each(lambda x, y, z: jnp.where(causal, _mm_nt(x, jnp.concatenate([bd(y), bd(z)], axis=0)), 0.0),
                   ar, b_t, k_t)
    a_ab = top(left(scores))
    a_rb = bottom(left(scores))

    inv = _unit_lower_inverse(a_ab, t_idx, s_idx, bd)
    kv = bdmm(right(scores), v)
    wu = bdmm2(inv, a_t, top(kv))
    ro = bdmm2(a_rb, left(wu), right(wu))
    r_hat = _each(jnp.add, r_t, left(ro))
    o_loc = _each(jnp.add, right(ro), bottom(kv))

    wu_b = _each(lambda x, b: _mm(x.T, b), wu, b_end)
    trans = _each(lambda x: jnp.where(bd_mask, x[:GROUP], 0.0), wu_b)
    d_bd = _each(lambda x, y, z: jnp.where(bd_mask, x[GROUP:] + _mm(y.T, z), 0.0), wu_b, v, k_end)
    d_ls = _each(lambda d: sum(d[h * CHUNK:(h + 1) * CHUNK] for h in range(1, GROUP // CHUNK)) + d[:CHUNK], d_bd)
    p_end = _each(jnp.exp, cum_last)
    return r_hat, o_loc, trans, d_ls, p_end


def _rwkv_kernel(r_ref, lw_ref, k_ref, v_ref, kk_ref, kka_ref, tri_ref, wg_ref, wu_ref, wd_ref,
                 o_ref, wg_bf_ref, wu_bf_ref, wd_bf_ref, state_ref, *, n_sub):
    @pl.when(pl.program_id(1) == 0)
    def _():
        state_ref[...] = jnp.zeros_like(state_ref)

    wg_bf_ref[...] = _bf(wg_ref[...])
    wu_bf_ref[...] = _bf(wu_ref[...])
    wd_bf_ref[...] = _bf(wd_ref[...])

    t_idx = lax.broadcasted_iota(jnp.int32, (CHUNK, GROUP), 0)
    s_idx = lax.broadcasted_iota(jnp.int32, (CHUNK, GROUP), 1) % CHUNK
    bd_mask = (lax.broadcasted_iota(jnp.int32, (GROUP, GROUP), 0) // CHUNK
               == lax.broadcasted_iota(jnp.int32, (GROUP, GROUP), 1) // CHUNK)
    n_group = D_RWKV // GROUP
    where = [(slice(c * CHUNK, (c + 1) * CHUNK), slice(g * GROUP, (g + 1) * GROUP))
             for c in range(n_sub) for g in range(n_group)]
    load = lambda ref: [ref[0, rows, lanes].astype(F32) for rows, lanes in where]
    r_hat, o_loc, trans, d_ls, p_end = _chunk_local(
        load(r_ref), load(lw_ref), load(k_ref), load(v_ref), load(kk_ref), load(kka_ref),
        tri_ref[...], t_idx, s_idx, bd_mask)

    state = [state_ref[g] for g in range(n_group)]
    for c in range(n_sub):
        chains = range(c * n_group, (c + 1) * n_group)
        out = [_mm_nt(r_hat[i], _block_diag(state[g], bd_mask)) + o_loc[i] for g, i in enumerate(chains)]
        for g, i in enumerate(chains):
            rows, lanes = where[i]
            o_ref[0, rows, lanes] = _bf(out[g])
        state = [state[g] * p_end[i] + _mm(state[g], trans[i]) + d_ls[i] for g, i in enumerate(chains)]
    for g in range(n_group):
        state_ref[g] = state[g]


def _rwkv_call(r, lw, k, v, kk, kka, tri, expert_weights, tb):
    bsz, seq, _ = r.shape
    n_t = seq // tb
    per_step = N_EXPERTS // (bsz * n_t)
    assert per_step * bsz * n_t == N_EXPERTS, "grid steps must divide the expert count"
    spec = pl.BlockSpec((1, tb, D_RWKV), lambda b, t: (b, t, 0))
    w_specs = [pl.BlockSpec((per_step,) + w.shape[1:], lambda b, t: (b * n_t + t, 0, 0)) for w in expert_weights]
    return pl.pallas_call(
        functools.partial(_rwkv_kernel, n_sub=tb // CHUNK),
        grid=(bsz, n_t),
        in_specs=[spec] * 6 + [pl.BlockSpec(tri.shape, lambda b, t: (0, 0))] + w_specs,
        out_specs=[spec] + w_specs,
        out_shape=[jax.ShapeDtypeStruct((bsz, seq, D_RWKV), BF16)]
        + [jax.ShapeDtypeStruct(w.shape, BF16) for w in expert_weights],
        scratch_shapes=[pltpu.VMEM((D_RWKV // GROUP, HEAD, GROUP), F32)],
        compiler_params=pltpu.CompilerParams(
            dimension_semantics=("arbitrary", "arbitrary"), vmem_limit_bytes=VMEM_LIMIT),
        name="rwkv_chunk",
    )(r, lw, k, v, kk, kka, tri, *expert_weights)


def _route(logits):
    lane_i = lax.broadcasted_iota(jnp.int32, logits.shape, 1)
    lane = lane_i.astype(F32)
    lane_group = (lane_i // EXPERTS_PER_GROUP).astype(F32)
    neg = -jnp.inf
    big = float(ROUTER_LANES)
    is_group = (lane_i >= N_EXPERTS) & (lane_i < N_EXPERTS + N_GROUPS)
    gl = jnp.where(is_group, logits, neg)
    ge = jnp.exp(gl - jnp.max(gl, axis=-1, keepdims=True))
    gprob = ge / jnp.sum(ge, axis=-1, keepdims=True)
    g_top = jnp.max(gprob, axis=-1, keepdims=True)
    g_idx = jnp.min(jnp.where(is_group & (gprob == g_top), lane - N_EXPERTS, big), axis=-1, keepdims=True)

    in_group = (lane_i < N_EXPERTS) & (lane_group == g_idx)
    el = jnp.where(in_group, logits, neg)
    top1 = jnp.max(el, axis=-1, keepdims=True)
    idx1 = jnp.min(jnp.where(in_group & (el == top1), lane, big), axis=-1, keepdims=True)
    el2 = jnp.where(lane == idx1, neg, el)
    top2 = jnp.max(el2, axis=-1, keepdims=True)
    idx2 = jnp.min(jnp.where(in_group & (lane != idx1) & (el2 == top2), lane, big), axis=-1, keepdims=True)
    e2 = jnp.exp(top2 - top1)
    denom = 1.0 + e2
    return idx1, idx2, g_top * (1.0 / denom), g_top * (e2 / denom)


def _pack_bf16_pairs(x):
    bits = lax.bitcast_convert_type(_bf(x).astype(F32), jnp.uint32)
    k = x.shape[1] // 2
    return lax.bitcast_convert_type((bits[:, :k] >> 16) | bits[:, k:], jnp.int32)


def _unpack_bf16_pairs(words):
    bits = lax.bitcast_convert_type(words, jnp.uint32)
    lo = lax.bitcast_convert_type(bits << 16, F32)
    hi = lax.bitcast_convert_type(bits & jnp.uint32(0xFFFF0000), F32)
    return _bf(jnp.concatenate([lo, hi], axis=1))


LANE_IDX1, LANE_IDX2, LANE_RANK1, LANE_RANK2, LANE_GATE1, LANE_GATE2 = range(6)


def _out_kernel(o_ref, bonus_ref, gate_ref, yconv_ref, x_ref, ones_ref, lnw_ref, lnb_ref,
                wout_ref, gffn_ref, rw_ref, rb_ref, tri_ref, h_ref, u_ref, route_ref, route_t_ref, count_ref,
                seen_ref, wbf_ref, *, n_split):
    @pl.when(pl.program_id(0) == 0)
    def _():
        seen_ref[...] = jnp.zeros_like(seen_ref)
        wbf_ref[...] = _bf(wout_ref[...])

    sub = o_ref.shape[0] // n_split
    parts = [slice(s * sub, (s + 1) * sub) for s in range(n_split)]
    read = lambda ref: [ref[p, :] for p in parts]
    ones_quad = ones_ref[...]
    inv_n = 1.0 / HEAD
    o = _each(lambda v: v.astype(F32), read(o_ref))
    mean = _each(lambda v: _head_sum(v, ones_quad) * inv_n, o)
    cen = _each(jnp.subtract, o, mean)
    var = _each(lambda c: _head_sum(c * c, ones_quad) * inv_n, cen)
    on = _each(lambda c, v: c * lax.rsqrt(v + LN_X_EPS) * lnw_ref[...] + lnb_ref[...], cen, var)
    y_rwkv = _each(lambda a, b, g: (a + b) * g, on, read(bonus_ref), read(gate_ref))
    mixed = _each(lambda yc, yr: (jnp.dot(yc, wbf_ref[:D_CONV, :], preferred_element_type=F32)
                                  + jnp.dot(_bf(yr), wbf_ref[D_CONV:, :], preferred_element_type=F32)),
                  read(yconv_ref), y_rwkv)
    h = _each(jnp.add, read(x_ref), mixed)
    u = _each(lambda v: _rms_norm(v, gffn_ref[...]), h)
    u_hi = _each(_bf, u)
    u_lo = _each(lambda a, b: _bf(a - b.astype(F32)), u, u_hi)
    by_hi = _each(lambda a: jnp.dot(a, rw_ref[...], preferred_element_type=F32), u_hi)
    by_lo = _each(lambda a: jnp.dot(a, rw_ref[:, :ROUTER_LANES], preferred_element_type=F32), u_lo)
    logits = _each(lambda a, b: a[:, :ROUTER_LANES] + a[:, ROUTER_LANES:] + b + rb_ref[...], by_hi, by_lo)
    routed = _each(_route, logits)

    lane_i = lax.broadcasted_iota(jnp.int32, (sub, ROUTER_LANES), 1)
    lane = lane_i.astype(F32)
    hit1 = _each(lambda rt: lane == rt[0], routed)
    hit2 = _each(lambda rt: lane == rt[1], routed)
    both = _each(lambda a, b: a.astype(F32) + b.astype(F32), hit1, hit2)
    inside = _each(lambda b: jnp.dot(tri_ref[...], _bf(b), preferred_element_type=F32), both)
    seen = seen_ref[...]
    for s, p in enumerate(parts):
        idx1, idx2, gate1, gate2 = routed[s]
        before = inside[s] + seen
        rank1 = jnp.sum(jnp.where(hit1[s], before, 0.0), axis=-1, keepdims=True)
        rank2 = jnp.sum(jnp.where(hit2[s], before, 0.0), axis=-1, keepdims=True)
        seen = seen + jnp.sum(both[s], axis=0, keepdims=True)
        route = jnp.zeros((sub, ROUTER_LANES), F32)
        for lane_id, col in ((LANE_IDX1, idx1), (LANE_IDX2, idx2), (LANE_RANK1, rank1),
                             (LANE_RANK2, rank2), (LANE_GATE1, gate1), (LANE_GATE2, gate2)):
            route = jnp.where(lane_i == lane_id, col, route)
        route_ref[p, :] = route
        route_t_ref[:, p] = route.T[:8, :]
        h_ref[p, :] = _bf(h[s])
        u_ref[p, :] = _pack_bf16_pairs(u[s])
    seen_ref[...] = seen
    count_ref[...] = jnp.broadcast_to(seen, count_ref.shape)


def _out_call(o, bonus, gate, yconv, x, ones_bd, ln_w, ln_b, w_out, g_ffn, router_w, router_b, tm, n_split):
    n_tok = x.shape[0]
    row = lambda width: pl.BlockSpec((tm, width), lambda i: (i, 0))
    full = lambda arr: pl.BlockSpec(arr.shape, lambda i: (0,) * arr.ndim, pipeline_mode=pl.Buffered(1))
    sub = tm // n_split
    tri_strict = (jnp.arange(sub)[:, None] > jnp.arange(sub)[None, :]).astype(BF16)
    params = (ones_bd, ln_w, ln_b, w_out, g_ffn, router_w, router_b, tri_strict)
    return pl.pallas_call(
        functools.partial(_out_kernel, n_split=n_split),
        grid=(n_tok // tm,),
        in_specs=[row(D_RWKV)] * 4 + [row(D_MODEL)] + [full(p) for p in params],
        out_specs=[row(D_MODEL), row(D_MODEL // 2), row(ROUTER_LANES),
                   pl.BlockSpec((8, tm), lambda i: (0, i)),
                   pl.BlockSpec((8, ROUTER_LANES), lambda i: (0, 0))],
        out_shape=[jax.ShapeDtypeStruct((n_tok, D_MODEL), BF16),
                   jax.ShapeDtypeStruct((n_tok, D_MODEL // 2), jnp.int32),
                   jax.ShapeDtypeStruct((n_tok, ROUTER_LANES), F32),
                   jax.ShapeDtypeStruct((8, n_tok), F32),
                   jax.ShapeDtypeStruct((8, ROUTER_LANES), F32)],
        scratch_shapes=[pltpu.VMEM((1, ROUTER_LANES), F32), pltpu.VMEM(w_out.shape, BF16)],
        compiler_params=pltpu.CompilerParams(
            dimension_semantics=("arbitrary",), vmem_limit_bytes=VMEM_LIMIT),
        name="out_proj_route",
    )(o, bonus, gate, yconv, x, *params)


SC_CORES = 2
SC_SUBCORES = 16
SC_ROWS = 64


def _sc_mesh():
    return plsc.VectorSubcoreMesh(core_axis_name="c", subcore_axis_name="s",
                                  num_cores=SC_CORES, num_subcores=SC_SUBCORES)


def _sc_worker():
    return lax.axis_index("s") * SC_CORES + lax.axis_index("c")


def _sc_gather(table, idx):
    n_rows = idx.shape[0]
    width = table.shape[1]
    n_chunks = n_rows // (SC_CORES * SC_SUBCORES * SC_ROWS)

    def body(table_hbm, idx_hbm, out_hbm, idx_v, rows_v, gather_sem, write_sem):
        first = _sc_worker() * n_chunks
        pltpu.sync_copy(idx_hbm.at[pl.ds(first, n_chunks)], idx_v)
        gather = lambda j: pltpu.async_copy(table_hbm.at[idx_v.at[j]], rows_v.at[j % 2], gather_sem.at[j % 2])
        gathers = [gather(0)]
        writes = []
        for j in range(n_chunks):
            gathers[j].wait()
            if j + 1 < n_chunks:
                if j >= 1:
                    writes[j - 1].wait()
                gathers.append(gather(j + 1))
            dst = out_hbm.at[pl.ds(pl.multiple_of((first + j) * SC_ROWS, SC_ROWS), SC_ROWS)]
            writes.append(pltpu.async_copy(rows_v.at[j % 2], dst, write_sem.at[j % 2]))
        for j in range(max(n_chunks - 2, 0), n_chunks):
            writes[j].wait()

    return pl.kernel(
        body,
        out_type=jax.ShapeDtypeStruct((n_rows, width), table.dtype),
        mesh=_sc_mesh(),
        scratch_types=[pltpu.VMEM((n_chunks, SC_ROWS), jnp.int32), pltpu.VMEM((2, SC_ROWS, width), table.dtype),
                       pltpu.SemaphoreType.DMA((2,)), pltpu.SemaphoreType.DMA((2,))],
        name="sc_row_gather",
    )(table, idx.reshape(n_rows // SC_ROWS, SC_ROWS))


def _sc_scatter(rows, pos, n_out):
    n_rows, width = rows.shape
    n_slots = pos.shape[0] // n_rows
    slot_chunks = n_rows // SC_ROWS
    n_chunks = slot_chunks // (SC_CORES * SC_SUBCORES)

    def body(rows_hbm, pos_hbm, out_hbm, idx_v, rows_v, read_sem, scatter_sem):
        first = _sc_worker() * n_chunks
        for s in range(n_slots):
            pltpu.sync_copy(pos_hbm.at[pl.ds(s * slot_chunks + first, n_chunks)], idx_v.at[s])
        read = lambda j: pltpu.async_copy(
            rows_hbm.at[pl.ds(pl.multiple_of((first + j) * SC_ROWS, SC_ROWS), SC_ROWS)],
            rows_v.at[j % 2], read_sem.at[j % 2])
        reads = [read(0)]
        scatters = []
        for j in range(n_chunks):
            reads[j].wait()
            if j + 1 < n_chunks:
                if j >= 1:
                    for copy in scatters[j - 1]:
                        copy.wait()
                reads.append(read(j + 1))
            scatters.append([pltpu.async_copy(rows_v.at[j % 2], out_hbm.at[idx_v.at[s, j]], scatter_sem.at[j % 2])
                             for s in range(n_slots)])
        for j in range(max(n_chunks - 2, 0), n_chunks):
            for copy in scatters[j]:
                copy.wait()

    return pl.kernel(
        body,
        out_type=jax.ShapeDtypeStruct((n_out, width), rows.dtype),
        mesh=_sc_mesh(),
        scratch_types=[pltpu.VMEM((n_slots, n_chunks, SC_ROWS), jnp.int32),
                       pltpu.VMEM((2, SC_ROWS, width), rows.dtype),
                       pltpu.SemaphoreType.DMA((2,)), pltpu.SemaphoreType.DMA((2,))],
        name="sc_row_scatter",
    )(rows, pos.reshape(n_slots * slot_chunks, SC_ROWS))


def _expert_kernel(tile_expert_ref, n_valid_ref, tile_rows_ref, x_ref, wg_ref, wu_ref, wd_ref, y_ref):
    del tile_expert_ref
    step = pl.program_id(0)
    half = x_ref.shape[0] // 2

    @pl.when(step < n_valid_ref[0])
    def _():
        w_gate, w_up, w_down = wg_ref[0], wu_ref[0], wd_ref[0]

        def run(rows):
            x = _unpack_bf16_pairs(x_ref[rows, :])
            gate = jnp.dot(x, w_gate, preferred_element_type=F32)
            up = jnp.dot(x, w_up, preferred_element_type=F32)
            hid = gate * jax.nn.sigmoid(gate) * up
            y_ref[rows, :] = _pack_bf16_pairs(jnp.dot(_bf(hid), w_down, preferred_element_type=F32))

        run(slice(0, half))

        @pl.when(tile_rows_ref[step] > half)
        def _():
            run(slice(half, 2 * half))


def _expert_call(tile_expert, n_valid, tile_rows, x_sorted, w_gate, w_up, w_down, tm):
    n_rows = x_sorted.shape[0]
    rows = pl.BlockSpec((tm, D_MODEL // 2), lambda i, te, nv, tr: (jnp.minimum(i, nv[0] - 1), 0))
    by_expert = lambda i, te, nv, tr: (te[i], 0, 0)
    return pl.pallas_call(
        _expert_kernel,
        grid_spec=pltpu.PrefetchScalarGridSpec(
            num_scalar_prefetch=3,
            grid=(n_rows // tm,),
            in_specs=[rows,
                      pl.BlockSpec((1, D_MODEL, D_EXPERT), by_expert),
                      pl.BlockSpec((1, D_MODEL, D_EXPERT), by_expert),
                      pl.BlockSpec((1, D_EXPERT, D_MODEL), by_expert)],
            out_specs=pl.BlockSpec((tm, D_MODEL // 2), lambda i, te, nv, tr: (i, 0))),
        out_shape=jax.ShapeDtypeStruct((n_rows, D_MODEL // 2), jnp.int32),
        compiler_params=pltpu.CompilerParams(
            dimension_semantics=("arbitrary",), vmem_limit_bytes=VMEM_LIMIT),
        name="moe_experts",
    )(tile_expert, n_valid, tile_rows, x_sorted, w_gate, w_up, w_down)


def _final_kernel(h_ref, y1_ref, y2_ref, route_ref, gfin_ref, *rest):
    out_ref = rest[-1]
    route = route_ref[...]
    gate1 = route[:, LANE_GATE1:LANE_GATE1 + 1]
    gate2 = route[:, LANE_GATE2:LANE_GATE2 + 1]
    moe = gate1 * _unpack_bf16_pairs(y1_ref[...]).astype(F32) + gate2 * _unpack_bf16_pairs(y2_ref[...]).astype(F32)
    out_ref[...] = _rms_norm(h_ref[...] + moe, gfin_ref[...])


def _final_call(h, y_pairs, route, g_final, earlier, chunk, n_chunks, tm):
    n_tok = h.shape[0]
    n_blocks = n_tok // n_chunks // tm
    first = chunk * n_blocks
    in_specs = [pl.BlockSpec((tm, D_MODEL), lambda i: (first + i, 0)),
                pl.BlockSpec((tm, D_MODEL // 2), lambda i: (i, 0)),
                pl.BlockSpec((tm, D_MODEL // 2), lambda i: (i + n_blocks, 0)),
                pl.BlockSpec((tm, ROUTER_LANES), lambda i: (first + i, 0)),
                pl.BlockSpec((1, D_MODEL), lambda i: (0, 0))]
    args = [h, y_pairs, y_pairs, route, g_final]
    aliases = {}
    if earlier is not None:
        in_specs.append(pl.BlockSpec(memory_space=pl.ANY))
        args.append(earlier)
        aliases = {len(args) - 1: 0}
    return pl.pallas_call(
        _final_kernel,
        grid=(n_blocks,),
        in_specs=in_specs,
        out_specs=pl.BlockSpec((tm, D_MODEL), lambda i: (first + i, 0)),
        out_shape=jax.ShapeDtypeStruct((n_tok, D_MODEL), F32),
        input_output_aliases=aliases,
        compiler_params=pltpu.CompilerParams(
            dimension_semantics=("arbitrary",), vmem_limit_bytes=VMEM_LIMIT),
        name="moe_combine_norm",
    )(*args)


def _dispatch_plan(route_t, counts, tm):
    n_tok = route_t.shape[1]
    n_tiles = (2 * n_tok) // tm + N_EXPERTS
    counts = counts[0, :N_EXPERTS].astype(jnp.int32)
    tiles_per = (counts + tm - 1) // tm
    tile_end = jnp.cumsum(tiles_per)
    row_start = (tile_end - tiles_per) * tm
    experts = jnp.arange(N_EXPERTS, dtype=jnp.int32)

    def position(idx_lane, rank_lane):
        idx = route_t[idx_lane].astype(jnp.int32)
        start = jnp.sum(jnp.where(idx[None, :] == experts[:, None], row_start[:, None], 0), axis=0)
        return start + route_t[rank_lane].astype(jnp.int32)

    pos = jnp.concatenate([position(LANE_IDX1, LANE_RANK1), position(LANE_IDX2, LANE_RANK2)])
    n_valid = tile_end[-1:]
    tile = jnp.minimum(jnp.arange(n_tiles, dtype=jnp.int32), n_valid - 1)
    tile_expert = jnp.sum((tile_end[None, :] <= tile[:, None]).astype(jnp.int32), axis=1)
    of_tile = lambda per_expert: jnp.sum(jnp.where(tile_expert[:, None] == experts[None, :], per_expert[None, :], 0), axis=1)
    tile_rows = jnp.clip(of_tile(counts) - (tile - of_tile(tile_end - tiles_per)) * tm, 0, tm)
    return pos, n_tiles, tile_expert, n_valid, tile_rows


def _block(x, norm_mix_g, w_in, rwkv_mu, conv_w, decay_up, decay_base, aaa_up, aaa_base, gate_up,
           k_k, k_a, r_k, ln_x_w, ln_x_b, w_out, norm_ffn_g, router_group_w, router_group_b,
           router_expert_w, router_expert_b, expert_w_gate, expert_w_up, expert_w_down, norm_final_g,
           *, tm_in, tb_rwkv, tm_out, tm_expert, tm_final):
    bsz, seq, d_model = x.shape
    n_tok = bsz * seq
    row = lambda p: p.reshape(1, -1)
    half = LORA_WA // 2
    zeros = jnp.zeros((half, D_RWKV), F32)
    dup = _bf(jnp.concatenate([decay_up, zeros], axis=0))
    aup = _bf(jnp.concatenate([zeros, aaa_up], axis=0))
    head_of = jnp.arange(QUAD) // HEAD
    ones_quad = (head_of[:, None] == head_of[None, :]).astype(BF16)
    tri = (jnp.arange(CHUNK)[:, None] >= jnp.arange(CHUNK)[None, :]).astype(BF16)

    yconv, r, lw, k2, v, kk, kka, gate, bonus = _in_call(
        x, row(norm_mix_g), w_in, row(rwkv_mu), conv_w, dup, row(decay_base), aup, row(aaa_base),
        _bf(gate_up), row(k_k), row(k_a), row(r_k), ones_quad, tm_in)
    o, w_gate_bf, w_up_bf, w_down_bf = _rwkv_call(r, lw, k2, v, kk, kka, tri,
                                                  (expert_w_gate, expert_w_up, expert_w_down), tb_rwkv)

    pad = ROUTER_LANES - N_EXPERTS - N_GROUPS
    router_w = jnp.concatenate([router_expert_w, router_group_w, jnp.zeros((d_model, pad), F32)], axis=1)
    router_hi = _bf(router_w)
    router_split = jnp.concatenate([router_hi, _bf(router_w - router_hi.astype(F32))], axis=1)
    router_b = jnp.concatenate([router_expert_b, router_group_b, jnp.zeros((pad,), F32)]).reshape(1, -1)
    flat = lambda t: t.reshape(n_tok, t.shape[-1])
    h, u_pairs, route, route_t, counts = _out_call(
        flat(o), flat(bonus), flat(gate), flat(yconv), flat(x), ones_quad,
        row(ln_x_w), row(ln_x_b), w_out, row(norm_ffn_g), router_split, router_b, tm_out, OUT_SPLIT)

    pos, n_tiles, tile_expert, n_valid, tile_rows = _dispatch_plan(route_t, counts, tm_expert)
    x_sorted = _sc_scatter(u_pairs, pos, n_tiles * tm_expert)
    y_sorted = _expert_call(tile_expert, n_valid, tile_rows, x_sorted, w_gate_bf, w_up_bf, w_down_bf, tm_expert)
    per_chunk = n_tok // COMBINE_CHUNKS
    out = None
    for c in range(COMBINE_CHUNKS):
        lo = c * per_chunk
        pos_c = jnp.concatenate([pos[lo:lo + per_chunk], pos[n_tok + lo:n_tok + lo + per_chunk]])
        out = _final_call(h, _sc_gather(y_sorted, pos_c), route, row(norm_final_g), out, c, COMBINE_CHUNKS, tm_final)
    return out.reshape(bsz, seq, d_model)


def kernel(x, norm_mix_g, w_in, rwkv_mu, conv_w, decay_up, decay_base, aaa_up, aaa_base, gate_up, k_k, k_a, r_k, ln_x_w, ln_x_b, w_out, norm_ffn_g, router_group_w, router_group_b, router_expert_w, router_expert_b, expert_w_gate, expert_w_up, expert_w_down, norm_final_g):
    return _block(x, norm_mix_g[0], w_in[0], rwkv_mu[0], conv_w[0], decay_up[0], decay_base[0],
                  aaa_up[0], aaa_base[0], gate_up[0], k_k[0], k_a[0], r_k[0].reshape(-1), ln_x_w[0],
                  ln_x_b[0], w_out[0], norm_ffn_g[0], router_group_w[0], router_group_b[0],
                  router_expert_w[0], router_expert_b[0], expert_w_gate[0], expert_w_up[0],
                  expert_w_down[0], norm_final_g,
                  tm_in=512, tb_rwkv=512, tm_out=512, tm_expert=1024, tm_final=512)
```

```python
import functools

import jax
import jax.numpy as jnp
from jax import lax
from jax.experimental import pallas as pl
from jax.experimental.pallas import tpu as pltpu
from jax.experimental.pallas import tpu_sc as plsc

F32 = jnp.float32
BF16 = jnp.bfloat16

D_MODEL = 1024
D_CONV = 512
CONV_WIDTH = 3
N_HEADS = 8
HEAD = 64
D_RWKV = N_HEADS * HEAD
LORA_WA = 128
GATE_LORA = 128
D_RWKV_PROJ = 3 * D_RWKV + LORA_WA + GATE_LORA
D_IN = 3 * D_CONV + D_RWKV_PROJ
N_GROUPS = 4
EXPERTS_PER_GROUP = 8
N_EXPERTS = N_GROUPS * EXPERTS_PER_GROUP
D_EXPERT = D_MODEL // 4
RMS_EPS = 1e-6
LN_X_EPS = 64e-5
L2_EPS = 1e-12

LANES = 128
CHUNK = 64
QUAD = 4 * HEAD
GROUP = 2 * HEAD
COMPACT_FROM = 64
OUT_SPLIT = 4
COMBINE_CHUNKS = 4
ROUTER_LANES = 128

VMEM_LIMIT = 56 * 1024 * 1024


def _bf(x):
    return x.astype(BF16)


def _mm(a, b):
    return jnp.dot(_bf(a), _bf(b), preferred_element_type=F32)


def _mm_nt(a, b):
    return lax.dot_general(_bf(a), _bf(b), (((1,), (1,)), ((), ())), preferred_element_type=F32)


def _mm_exact_lhs(lhs_bf16, x, passes):
    acc = None
    rem = x
    for _ in range(passes):
        piece = _bf(rem)
        part = jnp.dot(lhs_bf16, piece, preferred_element_type=F32)
        acc = part if acc is None else acc + part
        rem = rem - piece.astype(F32)
    return acc


def _head_sum(x, ones_quad):
    xb = _bf(x)
    return jnp.concatenate(
        [jnp.dot(xb[:, q * QUAD:(q + 1) * QUAD], ones_quad, preferred_element_type=F32)
         for q in range(x.shape[1] // QUAD)], axis=1)


def _rms_norm(x, g):
    return x * lax.rsqrt(jnp.mean(x * x, axis=-1, keepdims=True) + RMS_EPS) * g


def _shift_rows(cur, prev_rows, k):
    rolled = pltpu.roll(cur, k, 0)
    prev_rolled = pltpu.roll(prev_rows, k, 0)
    n = cur.shape[0]
    head = jnp.concatenate([prev_rolled, rolled[8:]], axis=0) if n > 8 else prev_rolled
    row = lax.broadcasted_iota(jnp.int32, cur.shape, 0)
    return jnp.where(row < k, head, rolled)


def _in_kernel(x_ref, g_ref, w_ref, mu_ref, convw_ref, dup_ref, dbase_ref, aup_ref, abase_ref,
               gup_ref, kk_ref, ka_ref, rk_ref, ones_ref,
               yconv_ref, r_ref, lw_ref, k_ref, v_ref, kkn_ref, kka_ref, gate_ref, bonus_ref,
               carry_ref, wbf_ref):
    @pl.when((pl.program_id(0) == 0) & (pl.program_id(1) == 0))
    def _():
        wbf_ref[...] = _bf(w_ref[...])

    @pl.when(pl.program_id(1) == 0)
    def _():
        carry_ref[...] = jnp.zeros_like(carry_ref)

    u = _bf(_rms_norm(x_ref[0], g_ref[...]))
    n_conv = 3 * D_CONV
    rk0 = n_conv
    bounds = dict(conv=(0, n_conv), lora=(rk0 + 3 * D_RWKV, D_IN), k=(rk0 + D_RWKV, rk0 + 2 * D_RWKV),
                  r=(rk0, rk0 + D_RWKV), v=(rk0 + 2 * D_RWKV, rk0 + 3 * D_RWKV))
    z = {name: jnp.dot(u, wbf_ref[:, lo:hi], preferred_element_type=F32) for name, (lo, hi) in bounds.items()}

    def lerp(name):
        lo, hi = bounds[name]
        cur = z[name]
        prev = carry_ref[:, D_CONV + lo - n_conv:D_CONV + hi - n_conv]
        mixed = cur + (_shift_rows(cur, prev, 1) - cur) * mu_ref[:, lo - n_conv:hi - n_conv]
        carry_ref[:, D_CONV + lo - n_conv:D_CONV + hi - n_conv] = cur[-8:]
        return mixed

    b_gate = z["conv"][:, :D_CONV]
    ch = z["conv"][:, D_CONV:2 * D_CONV] * z["conv"][:, 2 * D_CONV:]
    prev_ch = carry_ref[:, :D_CONV]
    conv = (convw_ref[2:3, :] * ch
            + convw_ref[1:2, :] * _shift_rows(ch, prev_ch, 1)
            + convw_ref[0:1, :] * _shift_rows(ch, prev_ch, 2))
    yconv_ref[0] = _bf(b_gate * conv)
    carry_ref[:, :D_CONV] = ch[-8:]

    lora = lerp("lora")
    wa_lo = lora[:, :LORA_WA]
    g_lo = lora[:, LORA_WA:]
    dec_in = -(dbase_ref[...] + jnp.dot(_bf(jnp.tanh(wa_lo)), dup_ref[...], preferred_element_type=F32))
    softplus = jnp.maximum(dec_in, 0.0) + jnp.log(1.0 + jnp.exp(-jnp.abs(dec_in)))
    w = -softplus - 0.5
    lw_ref[0] = -jnp.exp(w)
    a = jax.nn.sigmoid(abase_ref[...] + jnp.dot(_bf(wa_lo), aup_ref[...], preferred_element_type=F32))
    gate_ref[0] = _bf(jnp.dot(_bf(jax.nn.sigmoid(g_lo)), gup_ref[...], preferred_element_type=F32))

    ones_quad = ones_ref[...]
    k = lerp("k")
    kk = k * kk_ref[...]
    norm = jnp.sqrt(_head_sum(kk * kk, ones_quad))
    kk = kk / jnp.maximum(norm, L2_EPS)
    k2 = k * (1.0 + (a - 1.0) * ka_ref[...])
    k_ref[0] = _bf(k2)
    kkn_ref[0] = _bf(kk)
    kka_ref[0] = _bf(kk * a)
    r = lerp("r")
    r_ref[0] = _bf(r)
    v = lerp("v")
    v_ref[0] = _bf(v)
    bonus_ref[0] = _bf(_head_sum(r * k2 * rk_ref[...], ones_quad) * v)


def _in_call(x, g, w_in, mu, conv_w, dup, dbase, aup, abase, gup, k_k, k_a, r_k, ones_quad, tm):
    bsz, seq, _ = x.shape
    full = lambda arr: pl.BlockSpec(arr.shape, lambda b, t: (0,) * arr.ndim, pipeline_mode=pl.Buffered(1))
    out_spec = pl.BlockSpec((1, tm, D_RWKV), lambda b, t: (b, t, 0))
    out_dtypes = (BF16, BF16, F32, BF16, BF16, BF16, BF16, BF16, BF16)
    params = (g, w_in, mu, conv_w, dup, dbase, aup, abase, gup, k_k, k_a, r_k, ones_quad)
    return pl.pallas_call(
        _in_kernel,
        grid=(bsz, seq // tm),
        in_specs=[pl.BlockSpec((1, tm, D_MODEL), lambda b, t: (b, t, 0))] + [full(p) for p in params],
        out_specs=[out_spec] * 9,
        out_shape=[jax.ShapeDtypeStruct((bsz, seq, D_RWKV), dt) for dt in out_dtypes],
        scratch_shapes=[pltpu.VMEM((8, D_CONV + D_RWKV_PROJ), F32), pltpu.VMEM(w_in.shape, BF16)],
        compiler_params=pltpu.CompilerParams(
            dimension_semantics=("arbitrary", "arbitrary"), vmem_limit_bytes=VMEM_LIMIT),
        name="in_proj",
    )(x, *params)


def _block_diag(y, bd_mask):
    return jnp.where(bd_mask, jnp.concatenate([y] * (GROUP // CHUNK), axis=0), 0.0)


def _each(fn, *lists):
    return [fn(*args) for args in zip(*lists)]


def _unit_lower_inverse(a_strict, t_idx, s_idx, bd):
    bdmm = lambda xs, ys: _each(lambda x, y: _mm(x, bd(y)), xs, ys)
    eye = (t_idx == s_idx).astype(F32)
    same8 = (t_idx // 8) == (s_idx // 8)
    a8 = _each(lambda a: jnp.where(same8, a, 0.0), a_strict)
    a8_2 = bdmm(a8, a8)
    a8_34 = bdmm(_each(lambda a, b: jnp.concatenate([a, b], axis=0), a8, a8_2), a8_2)
    inv = _each(lambda a, b, c: eye + a + b + c[:CHUNK], a8, a8_2, a8_34)
    inv = _each(jnp.add, inv, bdmm(inv, _each(lambda c: c[CHUNK:], a8_34)))
    size = 16
    while size < COMPACT_FROM:
        off = ((t_idx // size) == (s_idx // size)) & ((t_idx // (size // 2)) != (s_idx // (size // 2)))
        cross = bdmm(_each(lambda a: jnp.where(off, a, 0.0), a_strict), inv)
        inv = _each(jnp.add, inv, bdmm(inv, cross))
        size *= 2
    while size <= CHUNK:
        inv = _compact_level(a_strict, inv, size)
        size *= 2
    return inv


def _compact_level(a_strict, inv, size):
    half = size // 2
    heads = GROUP // CHUNK
    blocks = CHUNK // size
    lane_starts = [h * CHUNK + b * size for h in range(heads) for b in range(blocks)]
    row_starts = [b * size for b in range(blocks)]
    rows_first = lambda x: jnp.concatenate([x[r:r + half] for r in row_starts], axis=0)
    rows_second = lambda x: jnp.concatenate([x[r + half:r + size] for r in row_starts], axis=0)
    lanes_first = lambda x: jnp.concatenate([x[:, c:c + half] for c in lane_starts], axis=1)
    lanes_second = lambda x: jnp.concatenate([x[:, c + half:c + size] for c in lane_starts], axis=1)
    n_rows, n_lanes = CHUNK // 2, GROUP // 2
    own = (lax.broadcasted_iota(jnp.int32, (n_rows, n_lanes), 0) // half
           == (lax.broadcasted_iota(jnp.int32, (n_rows, n_lanes), 1) // half) % blocks)
    diag = (lax.broadcasted_iota(jnp.int32, (n_lanes, n_lanes), 0) // half
            == lax.broadcasted_iota(jnp.int32, (n_lanes, n_lanes), 1) // half)
    bd_half = lambda y: jnp.where(diag, jnp.concatenate([y] * heads, axis=0), 0.0)
    a21 = _each(lambda a: jnp.where(own, lanes_first(rows_second(a)), 0.0), a_strict)
    t11 = _each(lambda t: lanes_first(rows_first(t)), inv)
    t22 = _each(lambda t: lanes_second(rows_second(t)), inv)
    a21_t11 = _each(lambda x, y: _mm(x, bd_half(y)), a21, t11)
    new = _each(lambda x, y: _mm(x, bd_half(y)), t22, a21_t11)

    def placed(x):
        zero_lanes = jnp.zeros((n_rows, half), F32)
        wide = jnp.concatenate(sum(([x[:, i * half:(i + 1) * half], zero_lanes] for i in range(heads * blocks)), []),
                               axis=1)
        zero_rows = jnp.zeros((half, GROUP), F32)
        return jnp.concatenate(sum(([zero_rows, wide[b * half:(b + 1) * half]] for b in range(blocks)), []), axis=0)

    return _each(lambda t, x: t + placed(x), inv, new)


def _chunk_local(r, lw, k, v, kk, kka, tri, t_idx, s_idx, bd_mask):
    bd = lambda y: _block_diag(y, bd_mask)
    bdmm = lambda xs, ys: _each(lambda x, y: _mm(x, bd(y)), xs, ys)
    bdmm2 = lambda xs, ys, zs: _each(lambda x, y, z: _mm(x, jnp.concatenate([bd(y), bd(z)], axis=1)), xs, ys, zs)
    left = lambda xs: _each(lambda x: x[:, :GROUP], xs)
    right = lambda xs: _each(lambda x: x[:, GROUP:], xs)
    top = lambda xs: _each(lambda x: x[:CHUNK], xs)
    bottom = lambda xs: _each(lambda x: x[CHUNK:], xs)

    cum = _each(lambda x: _mm_exact_lhs(tri, x, 3), lw)
    cum_last = _each(lambda c: c[CHUNK - 1:CHUNK, :], cum)
    p_incl = _each(jnp.exp, cum)
    p_excl = _each(lambda c, x: jnp.exp(c - x), cum, lw)
    p_inv = _each(lambda c: jnp.exp(-c), cum)
    to_end = _each(lambda cl, c: jnp.exp(cl - c), cum_last, cum)
    a_t = _each(lambda x, p: -x * p, kk, p_excl)
    r_t = _each(jnp.multiply, r, p_incl)
    b_t = _each(jnp.multiply, kka, p_inv)
    k_t = _each(jnp.multiply, k, p_inv)
    b_end = _each(jnp.multiply, kka, to_end)
    k_end = _each(jnp.multiply, k, to_end)

    ar = _each(lambda a, b: jnp.concatenate([a, b], axis=0), a_t, r_t)
    row2 = lax.broadcasted_iota(jnp.int32, (2 * CHUNK, 2 * GROUP), 0)
    col2 = lax.broadcasted_iota(jnp.int32, (2 * CHUNK, 2 * GROUP), 1) % CHUNK
    causal = col2 < (row2 % CHUNK) + (row2 // CHUNK)
    scores = _each(lambda x, y, z: jnp.where(causal, _mm_nt(x, jnp.concatenate([bd(y), bd(z)], axis=0)), 0.0),
                   ar, b_t, k_t)
    a_ab = top(left(scores))
    a_rb = bottom(left(scores))

    inv = _unit_lower_inverse(a_ab, t_idx, s_idx, bd)
    kv = bdmm(right(scores), v)
    wu = bdmm2(inv, a_t, top(kv))
    ro = bdmm2(a_rb, left(wu), right(wu))
    r_hat = _each(jnp.add, r_t, left(ro))
    o_loc = _each(jnp.add, right(ro), bottom(kv))

    wu_b = _each(lambda x, b: _mm(x.T, b), wu, b_end)
    trans = _each(lambda x: jnp.where(bd_mask, x[:GROUP], 0.0), wu_b)
    d_bd = _each(lambda x, y, z: jnp.where(bd_mask, x[GROUP:] + _mm(y.T, z), 0.0), wu_b, v, k_end)
    d_ls = _each(lambda d: sum(d[h * CHUNK:(h + 1) * CHUNK] for h in range(1, GROUP // CHUNK)) + d[:CHUNK], d_bd)
    p_end = _each(jnp.exp, cum_last)
    return r_hat, o_loc, trans, d_ls, p_end


def _rwkv_kernel(r_ref, lw_ref, k_ref, v_ref, kk_ref, kka_ref, tri_ref, wg_ref, wu_ref, wd_ref,
                 o_ref, wg_bf_ref, wu_bf_ref, wd_bf_ref, state_ref, *, n_sub):
    @pl.when(pl.program_id(1) == 0)
    def _():
        state_ref[...] = jnp.zeros_like(state_ref)

    wg_bf_ref[...] = _bf(wg_ref[...])
    wu_bf_ref[...] = _bf(wu_ref[...])
    wd_bf_ref[...] = _bf(wd_ref[...])

    t_idx = lax.broadcasted_iota(jnp.int32, (CHUNK, GROUP), 0)
    s_idx = lax.broadcasted_iota(jnp.int32, (CHUNK, GROUP), 1) % CHUNK
    bd_mask = (lax.broadcasted_iota(jnp.int32, (GROUP, GROUP), 0) // CHUNK
               == lax.broadcasted_iota(jnp.int32, (GROUP, GROUP), 1) // CHUNK)
    n_group = D_RWKV // GROUP
    where = [(slice(c * CHUNK, (c + 1) * CHUNK), slice(g * GROUP, (g + 1) * GROUP))
             for c in range(n_sub) for g in range(n_group)]
    load = lambda ref: [ref[0, rows, lanes].astype(F32) for rows, lanes in where]
    r_hat, o_loc, trans, d_ls, p_end = _chunk_local(
        load(r_ref), load(lw_ref), load(k_ref), load(v_ref), load(kk_ref), load(kka_ref),
        tri_ref[...], t_idx, s_idx, bd_mask)

    state = [state_ref[g] for g in range(n_group)]
    for c in range(n_sub):
        chains = range(c * n_group, (c + 1) * n_group)
        out = [_mm_nt(r_hat[i], _block_diag(state[g], bd_mask)) + o_loc[i] for g, i in enumerate(chains)]
        for g, i in enumerate(chains):
            rows, lanes = where[i]
            o_ref[0, rows, lanes] = _bf(out[g])
        state = [state[g] * p_end[i] + _mm(state[g], trans[i]) + d_ls[i] for g, i in enumerate(chains)]
    for g in range(n_group):
        state_ref[g] = state[g]


def _rwkv_call(r, lw, k, v, kk, kka, tri, expert_weights, tb):
    bsz, seq, _ = r.shape
    n_t = seq // tb
    per_step = N_EXPERTS // (bsz * n_t)
    assert per_step * bsz * n_t == N_EXPERTS, "grid steps must divide the expert count"
    spec = pl.BlockSpec((1, tb, D_RWKV), lambda b, t: (b, t, 0))
    w_specs = [pl.BlockSpec((per_step,) + w.shape[1:], lambda b, t: (b * n_t + t, 0, 0)) for w in expert_weights]
    return pl.pallas_call(
        functools.partial(_rwkv_kernel, n_sub=tb // CHUNK),
        grid=(bsz, n_t),
        in_specs=[spec] * 6 + [pl.BlockSpec(tri.shape, lambda b, t: (0, 0))] + w_specs,
        out_specs=[spec] + w_specs,
        out_shape=[jax.ShapeDtypeStruct((bsz, seq, D_RWKV), BF16)]
        + [jax.ShapeDtypeStruct(w.shape, BF16) for w in expert_weights],
        scratch_shapes=[pltpu.VMEM((D_RWKV // GROUP, HEAD, GROUP), F32)],
        compiler_params=pltpu.CompilerParams(
            dimension_semantics=("arbitrary", "arbitrary"), vmem_limit_bytes=VMEM_LIMIT),
        name="rwkv_chunk",
    )(r, lw, k, v, kk, kka, tri, *expert_weights)


def _route(logits):
    lane_i = lax.broadcasted_iota(jnp.int32, logits.shape, 1)
    lane = lane_i.astype(F32)
    lane_group = (lane_i // EXPERTS_PER_GROUP).astype(F32)
    neg = -jnp.inf
    big = float(ROUTER_LANES)
    is_group = (lane_i >= N_EXPERTS) & (lane_i < N_EXPERTS + N_GROUPS)
    gl = jnp.where(is_group, logits, neg)
    ge = jnp.exp(gl - jnp.max(gl, axis=-1, keepdims=True))
    gprob = ge / jnp.sum(ge, axis=-1, keepdims=True)
    g_top = jnp.max(gprob, axis=-1, keepdims=True)
    g_idx = jnp.min(jnp.where(is_group & (gprob == g_top), lane - N_EXPERTS, big), axis=-1, keepdims=True)

    in_group = (lane_i < N_EXPERTS) & (lane_group == g_idx)
    el = jnp.where(in_group, logits, neg)
    top1 = jnp.max(el, axis=-1, keepdims=True)
    idx1 = jnp.min(jnp.where(in_group & (el == top1), lane, big), axis=-1, keepdims=True)
    el2 = jnp.where(lane == idx1, neg, el)
    top2 = jnp.max(el2, axis=-1, keepdims=True)
    idx2 = jnp.min(jnp.where(in_group & (lane != idx1) & (el2 == top2), lane, big), axis=-1, keepdims=True)
    e2 = jnp.exp(top2 - top1)
    denom = 1.0 + e2
    return idx1, idx2, g_top * (1.0 / denom), g_top * (e2 / denom)


def _pack_bf16_pairs(x):
    bits = lax.bitcast_convert_type(_bf(x).astype(F32), jnp.uint32)
    k = x.shape[1] // 2
    return lax.bitcast_convert_type((bits[:, :k] >> 16) | bits[:, k:], jnp.int32)


def _unpack_bf16_pairs(words):
    bits = lax.bitcast_convert_type(words, jnp.uint32)
    lo = lax.bitcast_convert_type(bits << 16, F32)
    hi = lax.bitcast_convert_type(bits & jnp.uint32(0xFFFF0000), F32)
    return _bf(jnp.concatenate([lo, hi], axis=1))


LANE_IDX1, LANE_IDX2, LANE_RANK1, LANE_RANK2, LANE_GATE1, LANE_GATE2 = range(6)


def _out_kernel(o_ref, bonus_ref, gate_ref, yconv_ref, x_ref, ones_ref, lnw_ref, lnb_ref,
                wout_ref, gffn_ref, rw_ref, rb_ref, tri_ref, h_ref, u_ref, route_ref, route_t_ref, count_ref,
                seen_ref, wbf_ref, *, n_split):
    @pl.when(pl.program_id(0) == 0)
    def _():
        seen_ref[...] = jnp.zeros_like(seen_ref)
        wbf_ref[...] = _bf(wout_ref[...])

    sub = o_ref.shape[0] // n_split
    parts = [slice(s * sub, (s + 1) * sub) for s in range(n_split)]
    read = lambda ref: [ref[p, :] for p in parts]
    ones_quad = ones_ref[...]
    inv_n = 1.0 / HEAD
    o = _each(lambda v: v.astype(F32), read(o_ref))
    mean = _each(lambda v: _head_sum(v, ones_quad) * inv_n, o)
    cen = _each(jnp.subtract, o, mean)
    var = _each(lambda c: _head_sum(c * c, ones_quad) * inv_n, cen)
    on = _each(lambda c, v: c * lax.rsqrt(v + LN_X_EPS) * lnw_ref[...] + lnb_ref[...], cen, var)
    y_rwkv = _each(lambda a, b, g: (a + b) * g, on, read(bonus_ref), read(gate_ref))
    mixed = _each(lambda yc, yr: (jnp.dot(yc, wbf_ref[:D_CONV, :], preferred_element_type=F32)
                                  + jnp.dot(_bf(yr), wbf_ref[D_CONV:, :], preferred_element_type=F32)),
                  read(yconv_ref), y_rwkv)
    h = _each(jnp.add, read(x_ref), mixed)
    u = _each(lambda v: _rms_norm(v, gffn_ref[...]), h)
    u_hi = _each(_bf, u)
    u_lo = _each(lambda a, b: _bf(a - b.astype(F32)), u, u_hi)
    by_hi = _each(lambda a: jnp.dot(a, rw_ref[...], preferred_element_type=F32), u_hi)
    by_lo = _each(lambda a: jnp.dot(a, rw_ref[:, :ROUTER_LANES], preferred_element_type=F32), u_lo)
    logits = _each(lambda a, b: a[:, :ROUTER_LANES] + a[:, ROUTER_LANES:] + b + rb_ref[...], by_hi, by_lo)
    routed = _each(_route, logits)

    lane_i = lax.broadcasted_iota(jnp.int32, (sub, ROUTER_LANES), 1)
    lane = lane_i.astype(F32)
    hit1 = _each(lambda rt: lane == rt[0], routed)
    hit2 = _each(lambda rt: lane == rt[1], routed)
    both = _each(lambda a, b: a.astype(F32) + b.astype(F32), hit1, hit2)
    inside = _each(lambda b: jnp.dot(tri_ref[...], _bf(b), preferred_element_type=F32), both)
    seen = seen_ref[...]
    for s, p in enumerate(parts):
        idx1, idx2, gate1, gate2 = routed[s]
        before = inside[s] + seen
        rank1 = jnp.sum(jnp.where(hit1[s], before, 0.0), axis=-1, keepdims=True)
        rank2 = jnp.sum(jnp.where(hit2[s], before, 0.0), axis=-1, keepdims=True)
        seen = seen + jnp.sum(both[s], axis=0, keepdims=True)
        route = jnp.zeros((sub, ROUTER_LANES), F32)
        for lane_id, col in ((LANE_IDX1, idx1), (LANE_IDX2, idx2), (LANE_RANK1, rank1),
                             (LANE_RANK2, rank2), (LANE_GATE1, gate1), (LANE_GATE2, gate2)):
            route = jnp.where(lane_i == lane_id, col, route)
        route_ref[p, :] = route
        route_t_ref[:, p] = route.T[:8, :]
        h_ref[p, :] = _bf(h[s])
        u_ref[p, :] = _pack_bf16_pairs(u[s])
    seen_ref[...] = seen
    count_ref[...] = jnp.broadcast_to(seen, count_ref.shape)


def _out_call(o, bonus, gate, yconv, x, ones_bd, ln_w, ln_b, w_out, g_ffn, router_w, router_b, tm, n_split):
    n_tok = x.shape[0]
    row = lambda width: pl.BlockSpec((tm, width), lambda i: (i, 0))
    full = lambda arr: pl.BlockSpec(arr.shape, lambda i: (0,) * arr.ndim, pipeline_mode=pl.Buffered(1))
    sub = tm // n_split
    tri_strict = (jnp.arange(sub)[:, None] > jnp.arange(sub)[None, :]).astype(BF16)
    params = (ones_bd, ln_w, ln_b, w_out, g_ffn, router_w, router_b, tri_strict)
    return pl.pallas_call(
        functools.partial(_out_kernel, n_split=n_split),
        grid=(n_tok // tm,),
        in_specs=[row(D_RWKV)] * 4 + [row(D_MODEL)] + [full(p) for p in params],
        out_specs=[row(D_MODEL), row(D_MODEL // 2), row(ROUTER_LANES),
                   pl.BlockSpec((8, tm), lambda i: (0, i)),
                   pl.BlockSpec((8, ROUTER_LANES), lambda i: (0, 0))],
        out_shape=[jax.ShapeDtypeStruct((n_tok, D_MODEL), BF16),
                   jax.ShapeDtypeStruct((n_tok, D_MODEL // 2), jnp.int32),
                   jax.ShapeDtypeStruct((n_tok, ROUTER_LANES), F32),
                   jax.ShapeDtypeStruct((8, n_tok), F32),
                   jax.ShapeDtypeStruct((8, ROUTER_LANES), F32)],
        scratch_shapes=[pltpu.VMEM((1, ROUTER_LANES), F32), pltpu.VMEM(w_out.shape, BF16)],
        compiler_params=pltpu.CompilerParams(
            dimension_semantics=("arbitrary",), vmem_limit_bytes=VMEM_LIMIT),
        name="out_proj_route",
    )(o, bonus, gate, yconv, x, *params)


SC_CORES = 2
SC_SUBCORES = 16
SC_ROWS = 64


def _sc_mesh():
    return plsc.VectorSubcoreMesh(core_axis_name="c", subcore_axis_name="s",
                                  num_cores=SC_CORES, num_subcores=SC_SUBCORES)


def _sc_worker():
    return lax.axis_index("s") * SC_CORES + lax.axis_index("c")


def _sc_gather(table, idx):
    n_rows = idx.shape[0]
    width = table.shape[1]
    n_chunks = n_rows // (SC_CORES * SC_SUBCORES * SC_ROWS)

    def body(table_hbm, idx_hbm, out_hbm, idx_v, rows_v, gather_sem, write_sem):
        first = _sc_worker() * n_chunks
        pltpu.sync_copy(idx_hbm.at[pl.ds(first, n_chunks)], idx_v)
        gather = lambda j: pltpu.async_copy(table_hbm.at[idx_v.at[j]], rows_v.at[j % 2], gather_sem.at[j % 2])
        gathers = [gather(0)]
        writes = []
        for j in range(n_chunks):
            gathers[j].wait()
            if j + 1 < n_chunks:
                if j >= 1:
                    writes[j - 1].wait()
                gathers.append(gather(j + 1))
            dst = out_hbm.at[pl.ds(pl.multiple_of((first + j) * SC_ROWS, SC_ROWS), SC_ROWS)]
            writes.append(pltpu.async_copy(rows_v.at[j % 2], dst, write_sem.at[j % 2]))
        for j in range(max(n_chunks - 2, 0), n_chunks):
            writes[j].wait()

    return pl.kernel(
        body,
        out_type=jax.ShapeDtypeStruct((n_rows, width), table.dtype),
        mesh=_sc_mesh(),
        scratch_types=[pltpu.VMEM((n_chunks, SC_ROWS), jnp.int32), pltpu.VMEM((2, SC_ROWS, width), table.dtype),
                       pltpu.SemaphoreType.DMA((2,)), pltpu.SemaphoreType.DMA((2,))],
        name="sc_row_gather",
    )(table, idx.reshape(n_rows // SC_ROWS, SC_ROWS))


def _sc_scatter(rows, pos, n_out):
    n_rows, width = rows.shape
    n_slots = pos.shape[0] // n_rows
    slot_chunks = n_rows // SC_ROWS
    n_chunks = slot_chunks // (SC_CORES * SC_SUBCORES)

    def body(rows_hbm, pos_hbm, out_hbm, idx_v, rows_v, read_sem, scatter_sem):
        first = _sc_worker() * n_chunks
        for s in range(n_slots):
            pltpu.sync_copy(pos_hbm.at[pl.ds(s * slot_chunks + first, n_chunks)], idx_v.at[s])
        read = lambda j: pltpu.async_copy(
            rows_hbm.at[pl.ds(pl.multiple_of((first + j) * SC_ROWS, SC_ROWS), SC_ROWS)],
            rows_v.at[j % 2], read_sem.at[j % 2])
        reads = [read(0)]
        scatters = []
        for j in range(n_chunks):
            reads[j].wait()
            if j + 1 < n_chunks:
                if j >= 1:
                    for copy in scatters[j - 1]:
                        copy.wait()
                reads.append(read(j + 1))
            scatters.append([pltpu.async_copy(rows_v.at[j % 2], out_hbm.at[idx_v.at[s, j]], scatter_sem.at[j % 2])
                             for s in range(n_slots)])
        for j in range(max(n_chunks - 2, 0), n_chunks):
            for copy in scatters[j]:
                copy.wait()

    return pl.kernel(
        body,
        out_type=jax.ShapeDtypeStruct((n_out, width), rows.dtype),
        mesh=_sc_mesh(),
        scratch_types=[pltpu.VMEM((n_slots, n_chunks, SC_ROWS), jnp.int32),
                       pltpu.VMEM((2, SC_ROWS, width), rows.dtype),
                       pltpu.SemaphoreType.DMA((2,)), pltpu.SemaphoreType.DMA((2,))],
        name="sc_row_scatter",
    )(rows, pos.reshape(n_slots * slot_chunks, SC_ROWS))


def _expert_kernel(tile_expert_ref, n_valid_ref, tile_rows_ref, x_ref, wg_ref, wu_ref, wd_ref, y_ref):
    del tile_expert_ref
    step = pl.program_id(0)
    half = x_ref.shape[0] // 2

    @pl.when(step < n_valid_ref[0])
    def _():
        w_gate, w_up, w_down = wg_ref[0], wu_ref[0], wd_ref[0]

        def run(rows):
            x = _unpack_bf16_pairs(x_ref[rows, :])
            gate = jnp.dot(x, w_gate, preferred_element_type=F32)
            up = jnp.dot(x, w_up, preferred_element_type=F32)
            hid = gate * jax.nn.sigmoid(gate) * up
            y_ref[rows, :] = _pack_bf16_pairs(jnp.dot(_bf(hid), w_down, preferred_element_type=F32))

        run(slice(0, half))

        @pl.when(tile_rows_ref[step] > half)
        def _():
            run(slice(half, 2 * half))


def _expert_call(tile_expert, n_valid, tile_rows, x_sorted, w_gate, w_up, w_down, tm):
    n_rows = x_sorted.shape[0]
    rows = pl.BlockSpec((tm, D_MODEL // 2), lambda i, te, nv, tr: (jnp.minimum(i, nv[0] - 1), 0))
    by_expert = lambda i, te, nv, tr: (te[i], 0, 0)
    return pl.pallas_call(
        _expert_kernel,
        grid_spec=pltpu.PrefetchScalarGridSpec(
            num_scalar_prefetch=3,
            grid=(n_rows // tm,),
            in_specs=[rows,
                      pl.BlockSpec((1, D_MODEL, D_EXPERT), by_expert),
                      pl.BlockSpec((1, D_MODEL, D_EXPERT), by_expert),
                      pl.BlockSpec((1, D_EXPERT, D_MODEL), by_expert)],
            out_specs=rows),
        out_shape=jax.ShapeDtypeStruct((n_rows, D_MODEL // 2), jnp.int32),
        compiler_params=pltpu.CompilerParams(
            dimension_semantics=("arbitrary",), vmem_limit_bytes=VMEM_LIMIT),
        name="moe_experts",
    )(tile_expert, n_valid, tile_rows, x_sorted, w_gate, w_up, w_down)


def _final_kernel(h_ref, y1_ref, y2_ref, route_ref, gfin_ref, *rest):
    out_ref = rest[-1]
    route = route_ref[...]
    gate1 = route[:, LANE_GATE1:LANE_GATE1 + 1]
    gate2 = route[:, LANE_GATE2:LANE_GATE2 + 1]
    moe = gate1 * _unpack_bf16_pairs(y1_ref[...]).astype(F32) + gate2 * _unpack_bf16_pairs(y2_ref[...]).astype(F32)
    out_ref[...] = _rms_norm(h_ref[...] + moe, gfin_ref[...])


def _final_call(h, y_pairs, route, g_final, earlier, chunk, n_chunks, tm):
    n_tok = h.shape[0]
    n_blocks = n_tok // n_chunks // tm
    first = chunk * n_blocks
    in_specs = [pl.BlockSpec((tm, D_MODEL), lambda i: (first + i, 0)),
                pl.BlockSpec((tm, D_MODEL // 2), lambda i: (i, 0)),
                pl.BlockSpec((tm, D_MODEL // 2), lambda i: (i + n_blocks, 0)),
                pl.BlockSpec((tm, ROUTER_LANES), lambda i: (first + i, 0)),
                pl.BlockSpec((1, D_MODEL), lambda i: (0, 0))]
    args = [h, y_pairs, y_pairs, route, g_final]
    aliases = {}
    if earlier is not None:
        in_specs.append(pl.BlockSpec(memory_space=pl.ANY))
        args.append(earlier)
        aliases = {len(args) - 1: 0}
    return pl.pallas_call(
        _final_kernel,
        grid=(n_blocks,),
        in_specs=in_specs,
        out_specs=pl.BlockSpec((tm, D_MODEL), lambda i: (first + i, 0)),
        out_shape=jax.ShapeDtypeStruct((n_tok, D_MODEL), F32),
        input_output_aliases=aliases,
        compiler_params=pltpu.CompilerParams(
            dimension_semantics=("arbitrary",), vmem_limit_bytes=VMEM_LIMIT),
        name="moe_combine_norm",
    )(*args)


def _dispatch_plan(route_t, counts, tm):
    n_tok = route_t.shape[1]
    n_tiles = (2 * n_tok) // tm + N_EXPERTS
    counts = counts[0, :N_EXPERTS].astype(jnp.int32)
    tiles_per = (counts + tm - 1) // tm
    tile_end = jnp.cumsum(tiles_per)
    row_start = (tile_end - tiles_per) * tm
    experts = jnp.arange(N_EXPERTS, dtype=jnp.int32)

    def position(idx_lane, rank_lane):
        idx = route_t[idx_lane].astype(jnp.int32)
        start = jnp.sum(jnp.where(idx[None, :] == experts[:, None], row_start[:, None], 0), axis=0)
        return start + route_t[rank_lane].astype(jnp.int32)

    pos = jnp.concatenate([position(LANE_IDX1, LANE_RANK1), position(LANE_IDX2, LANE_RANK2)])
    n_valid = tile_end[-1:]
    tile = jnp.minimum(jnp.arange(n_tiles, dtype=jnp.int32), n_valid - 1)
    tile_expert = jnp.sum((tile_end[None, :] <= tile[:, None]).astype(jnp.int32), axis=1)
    of_tile = lambda per_expert: jnp.sum(jnp.where(tile_expert[:, None] == experts[None, :], per_expert[None, :], 0), axis=1)
    tile_rows = jnp.clip(of_tile(counts) - (tile - of_tile(tile_end - tiles_per)) * tm, 0, tm)
    return pos, n_tiles, tile_expert, n_valid, tile_rows


def _block(x, norm_mix_g, w_in, rwkv_mu, conv_w, decay_up, decay_base, aaa_up, aaa_base, gate_up,
           k_k, k_a, r_k, ln_x_w, ln_x_b, w_out, norm_ffn_g, router_group_w, router_group_b,
           router_expert_w, router_expert_b, expert_w_gate, expert_w_up, expert_w_down, norm_final_g,
           *, tm_in, tb_rwkv, tm_out, tm_expert, tm_final):
    bsz, seq, d_model = x.shape
    n_tok = bsz * seq
    row = lambda p: p.reshape(1, -1)
    half = LORA_WA // 2
    zeros = jnp.zeros((half, D_RWKV), F32)
    dup = _bf(jnp.concatenate([decay_up, zeros], axis=0))
    aup = _bf(jnp.concatenate([zeros, aaa_up], axis=0))
    head_of = jnp.arange(QUAD) // HEAD
    ones_quad = (head_of[:, None] == head_of[None, :]).astype(BF16)
    tri = (jnp.arange(CHUNK)[:, None] >= jnp.arange(CHUNK)[None, :]).astype(BF16)

    yconv, r, lw, k2, v, kk, kka, gate, bonus = _in_call(
        x, row(norm_mix_g), w_in, row(rwkv_mu), conv_w, dup, row(decay_base), aup, row(aaa_base),
        _bf(gate_up), row(k_k), row(k_a), row(r_k), ones_quad, tm_in)
    o, w_gate_bf, w_up_bf, w_down_bf = _rwkv_call(r, lw, k2, v, kk, kka, tri,
                                                  (expert_w_gate, expert_w_up, expert_w_down), tb_rwkv)

    pad = ROUTER_LANES - N_EXPERTS - N_GROUPS
    router_w = jnp.concatenate([router_expert_w, router_group_w, jnp.zeros((d_model, pad), F32)], axis=1)
    router_hi = _bf(router_w)
    router_split = jnp.concatenate([router_hi, _bf(router_w - router_hi.astype(F32))], axis=1)
    router_b = jnp.concatenate([router_expert_b, router_group_b, jnp.zeros((pad,), F32)]).reshape(1, -1)
    flat = lambda t: t.reshape(n_tok, t.shape[-1])
    h, u_pairs, route, route_t, counts = _out_call(
        flat(o), flat(bonus), flat(gate), flat(yconv), flat(x), ones_quad,
        row(ln_x_w), row(ln_x_b), w_out, row(norm_ffn_g), router_split, router_b, tm_out, OUT_SPLIT)

    pos, n_tiles, tile_expert, n_valid, tile_rows = _dispatch_plan(route_t, counts, tm_expert)
    x_sorted = _sc_scatter(u_pairs, pos, n_tiles * tm_expert)
    y_sorted = _expert_call(tile_expert, n_valid, tile_rows, x_sorted, w_gate_bf, w_up_bf, w_down_bf, tm_expert)
    per_chunk = n_tok // COMBINE_CHUNKS
    out = None
    for c in range(COMBINE_CHUNKS):
        lo = c * per_chunk
        pos_c = jnp.concatenate([pos[lo:lo + per_chunk], pos[n_tok + lo:n_tok + lo + per_chunk]])
        out = _final_call(h, _sc_gather(y_sorted, pos_c), route, row(norm_final_g), out, c, COMBINE_CHUNKS, tm_final)
    return out.reshape(bsz, seq, d_model)


def kernel(x, norm_mix_g, w_in, rwkv_mu, conv_w, decay_up, decay_base, aaa_up, aaa_base, gate_up, k_k, k_a, r_k, ln_x_w, ln_x_b, w_out, norm_ffn_g, router_group_w, router_group_b, router_expert_w, router_expert_b, expert_w_gate, expert_w_up, expert_w_down, norm_final_g):
    return _block(x, norm_mix_g[0], w_in[0], rwkv_mu[0], conv_w[0], decay_up[0], decay_base[0],
                  aaa_up[0], aaa_base[0], gate_up[0], k_k[0], k_a[0], r_k[0].reshape(-1), ln_x_w[0],
                  ln_x_b[0], w_out[0], norm_ffn_g[0], router_group_w[0], router_group_b[0],
                  router_expert_w[0], router_expert_b[0], expert_w_gate[0], expert_w_up[0],
                  expert_w_down[0], norm_final_g,
                  tm_in=512, tb_rwkv=512, tm_out=512, tm_expert=1024, tm_final=512)
```

```python
import functools

import jax
import jax.numpy as jnp
from jax import lax
from jax.experimental import pallas as pl
from jax.experimental.pallas import tpu as pltpu
from jax.experimental.pallas import tpu_sc as plsc

F32 = jnp.float32
BF16 = jnp.bfloat16

D_MODEL = 1024
D_CONV = 512
CONV_WIDTH = 3
N_HEADS = 8
HEAD = 64
D_RWKV = N_HEADS * HEAD
LORA_WA = 128
GATE_LORA = 128
D_RWKV_PROJ = 3 * D_RWKV + LORA_WA + GATE_LORA
D_IN = 3 * D_CONV + D_RWKV_PROJ
N_GROUPS = 4
EXPERTS_PER_GROUP = 8
N_EXPERTS = N_GROUPS * EXPERTS_PER_GROUP
D_EXPERT = D_MODEL // 4
RMS_EPS = 1e-6
LN_X_EPS = 64e-5
L2_EPS = 1e-12

SUBLANES = 8
CHUNK = 64
QUAD = 4 * HEAD
GROUP = 2 * HEAD
INV_BASE = 8
COMPACT_FROM = 64
OUT_SPLIT = 4
COMBINE_SPLIT = (8, 8, 4, 2)
ROUTER_LANES = 128

VMEM_LIMIT = 56 * 1024 * 1024


def _bf(x):
    return x.astype(BF16)


def _mm(a, b):
    return jnp.dot(_bf(a), _bf(b), preferred_element_type=F32)


def _mm_nt(a, b):
    return lax.dot_general(_bf(a), _bf(b), (((1,), (1,)), ((), ())), preferred_element_type=F32)


def _mm_exact_lhs(lhs_bf16, x, passes):
    acc = None
    rem = x
    for _ in range(passes):
        piece = _bf(rem)
        part = jnp.dot(lhs_bf16, piece, preferred_element_type=F32)
        acc = part if acc is None else acc + part
        rem = rem - piece.astype(F32)
    return acc


def _head_sum(x, ones_quad):
    xb = _bf(x)
    return jnp.concatenate(
        [jnp.dot(xb[:, q * QUAD:(q + 1) * QUAD], ones_quad, preferred_element_type=F32)
         for q in range(x.shape[1] // QUAD)], axis=1)


def _rms_norm(x, g):
    return x * lax.rsqrt(jnp.mean(x * x, axis=-1, keepdims=True) + RMS_EPS) * g


def _shift_rows(cur, prev_rows, k):
    rolled = pltpu.roll(cur, k, 0)
    prev_rolled = pltpu.roll(prev_rows, k, 0)
    n = cur.shape[0]
    head = jnp.concatenate([prev_rolled, rolled[SUBLANES:]], axis=0) if n > SUBLANES else prev_rolled
    row = lax.broadcasted_iota(jnp.int32, cur.shape, 0)
    return jnp.where(row < k, head, rolled)


def _in_kernel(x_ref, g_ref, w_ref, mu_ref, convw_ref, dup_ref, dbase_ref, aup_ref, abase_ref,
               gup_ref, kk_ref, ka_ref, rk_ref, ones_ref,
               yconv_ref, r_ref, lw_ref, k_ref, v_ref, kkn_ref, kka_ref, gate_ref, bonus_ref,
               carry_ref, wbf_ref):
    @pl.when((pl.program_id(0) == 0) & (pl.program_id(1) == 0))
    def _():
        wbf_ref[...] = _bf(w_ref[...])

    @pl.when(pl.program_id(1) == 0)
    def _():
        carry_ref[...] = jnp.zeros_like(carry_ref)

    u = _bf(_rms_norm(x_ref[0], g_ref[...]))
    n_conv = 3 * D_CONV
    rk0 = n_conv
    bounds = dict(conv=(0, n_conv), lora=(rk0 + 3 * D_RWKV, D_IN), k=(rk0 + D_RWKV, rk0 + 2 * D_RWKV),
                  r=(rk0, rk0 + D_RWKV), v=(rk0 + 2 * D_RWKV, rk0 + 3 * D_RWKV))
    z = {name: jnp.dot(u, wbf_ref[:, lo:hi], preferred_element_type=F32) for name, (lo, hi) in bounds.items()}

    def lerp(name):
        lo, hi = bounds[name]
        cur = z[name]
        prev = carry_ref[:, D_CONV + lo - n_conv:D_CONV + hi - n_conv]
        mixed = cur + (_shift_rows(cur, prev, 1) - cur) * mu_ref[:, lo - n_conv:hi - n_conv]
        carry_ref[:, D_CONV + lo - n_conv:D_CONV + hi - n_conv] = cur[-SUBLANES:]
        return mixed

    b_gate = z["conv"][:, :D_CONV]
    ch = z["conv"][:, D_CONV:2 * D_CONV] * z["conv"][:, 2 * D_CONV:]
    prev_ch = carry_ref[:, :D_CONV]
    conv = convw_ref[CONV_WIDTH - 1:CONV_WIDTH, :] * ch
    for delay in range(1, CONV_WIDTH):
        tap = CONV_WIDTH - 1 - delay
        conv = conv + convw_ref[tap:tap + 1, :] * _shift_rows(ch, prev_ch, delay)
    yconv_ref[0] = _bf(b_gate * conv)
    carry_ref[:, :D_CONV] = ch[-SUBLANES:]

    lora = lerp("lora")
    wa_lo = lora[:, :LORA_WA]
    g_lo = lora[:, LORA_WA:]
    dec_in = -(dbase_ref[...] + jnp.dot(_bf(jnp.tanh(wa_lo)), dup_ref[...], preferred_element_type=F32))
    softplus = jnp.maximum(dec_in, 0.0) + jnp.log(1.0 + jnp.exp(-jnp.abs(dec_in)))
    w = -softplus - 0.5
    lw_ref[0] = -jnp.exp(w)
    a = jax.nn.sigmoid(abase_ref[...] + jnp.dot(_bf(wa_lo), aup_ref[...], preferred_element_type=F32))
    gate_ref[0] = _bf(jnp.dot(_bf(jax.nn.sigmoid(g_lo)), gup_ref[...], preferred_element_type=F32))

    ones_quad = ones_ref[...]
    k = lerp("k")
    kk = k * kk_ref[...]
    norm = jnp.sqrt(_head_sum(kk * kk, ones_quad))
    kk = kk / jnp.maximum(norm, L2_EPS)
    k2 = k * (1.0 + (a - 1.0) * ka_ref[...])
    k_ref[0] = _bf(k2)
    kkn_ref[0] = _bf(kk)
    kka_ref[0] = _bf(kk * a)
    r = lerp("r")
    r_ref[0] = _bf(r)
    v = lerp("v")
    v_ref[0] = _bf(v)
    bonus_ref[0] = _bf(_head_sum(r * k2 * rk_ref[...], ones_quad) * v)


def _in_call(x, g, w_in, mu, conv_w, dup, dbase, aup, abase, gup, k_k, k_a, r_k, ones_quad, tm):
    bsz, seq, _ = x.shape
    full = lambda arr: pl.BlockSpec(arr.shape, lambda b, t: (0,) * arr.ndim, pipeline_mode=pl.Buffered(1))
    out_spec = pl.BlockSpec((1, tm, D_RWKV), lambda b, t: (b, t, 0))
    out_dtypes = (BF16, BF16, F32, BF16, BF16, BF16, BF16, BF16, BF16)
    params = (g, w_in, mu, conv_w, dup, dbase, aup, abase, gup, k_k, k_a, r_k, ones_quad)
    return pl.pallas_call(
        _in_kernel,
        grid=(bsz, seq // tm),
        in_specs=[pl.BlockSpec((1, tm, D_MODEL), lambda b, t: (b, t, 0))] + [full(p) for p in params],
        out_specs=[out_spec] * 9,
        out_shape=[jax.ShapeDtypeStruct((bsz, seq, D_RWKV), dt) for dt in out_dtypes],
        scratch_shapes=[pltpu.VMEM((SUBLANES, D_CONV + D_RWKV_PROJ), F32), pltpu.VMEM(w_in.shape, BF16)],
        compiler_params=pltpu.CompilerParams(
            dimension_semantics=("arbitrary", "arbitrary"), vmem_limit_bytes=VMEM_LIMIT),
        name="in_proj",
    )(x, *params)


def _block_diag(y, bd_mask):
    return jnp.where(bd_mask, jnp.concatenate([y] * (GROUP // CHUNK), axis=0), 0.0)


def _each(fn, *lists):
    return [fn(*args) for args in zip(*lists)]


def _unit_lower_inverse(a_strict, t_idx, s_idx, bd):
    bdmm = lambda xs, ys: _each(lambda x, y: _mm(x, bd(y)), xs, ys)
    eye = (t_idx == s_idx).astype(F32)
    same8 = (t_idx // INV_BASE) == (s_idx // INV_BASE)
    a8 = _each(lambda a: jnp.where(same8, a, 0.0), a_strict)
    a8_2 = bdmm(a8, a8)
    a8_34 = bdmm(_each(lambda a, b: jnp.concatenate([a, b], axis=0), a8, a8_2), a8_2)
    inv = _each(lambda a, b, c: eye + a + b + c[:CHUNK], a8, a8_2, a8_34)
    inv = _each(jnp.add, inv, bdmm(inv, _each(lambda c: c[CHUNK:], a8_34)))
    size = 2 * INV_BASE
    while size < COMPACT_FROM:
        off = ((t_idx // size) == (s_idx // size)) & ((t_idx // (size // 2)) != (s_idx // (size // 2)))
        cross = bdmm(_each(lambda a: jnp.where(off, a, 0.0), a_strict), inv)
        inv = _each(jnp.add, inv, bdmm(inv, cross))
        size *= 2
    while size <= CHUNK:
        inv = _compact_level(a_strict, inv, size)
        size *= 2
    return inv


def _compact_level(a_strict, inv, size):
    half = size // 2
    heads = GROUP // CHUNK
    blocks = CHUNK // size
    lane_starts = [h * CHUNK + b * size for h in range(heads) for b in range(blocks)]
    row_starts = [b * size for b in range(blocks)]
    rows_first = lambda x: jnp.concatenate([x[r:r + half] for r in row_starts], axis=0)
    rows_second = lambda x: jnp.concatenate([x[r + half:r + size] for r in row_starts], axis=0)
    lanes_first = lambda x: jnp.concatenate([x[:, c:c + half] for c in lane_starts], axis=1)
    lanes_second = lambda x: jnp.concatenate([x[:, c + half:c + size] for c in lane_starts], axis=1)
    n_rows, n_lanes = CHUNK // 2, GROUP // 2
    own = (lax.broadcasted_iota(jnp.int32, (n_rows, n_lanes), 0) // half
           == (lax.broadcasted_iota(jnp.int32, (n_rows, n_lanes), 1) // half) % blocks)
    diag = (lax.broadcasted_iota(jnp.int32, (n_lanes, n_lanes), 0) // half
            == lax.broadcasted_iota(jnp.int32, (n_lanes, n_lanes), 1) // half)
    bd_half = lambda y: jnp.where(diag, jnp.concatenate([y] * heads, axis=0), 0.0)
    a21 = _each(lambda a: jnp.where(own, lanes_first(rows_second(a)), 0.0), a_strict)
    t11 = _each(lambda t: lanes_first(rows_first(t)), inv)
    t22 = _each(lambda t: lanes_second(rows_second(t)), inv)
    a21_t11 = _each(lambda x, y: _mm(x, bd_half(y)), a21, t11)
    new = _each(lambda x, y: _mm(x, bd_half(y)), t22, a21_t11)

    def placed(x):
        zero_lanes = jnp.zeros((n_rows, half), F32)
        wide = jnp.concatenate(sum(([x[:, i * half:(i + 1) * half], zero_lanes] for i in range(heads * blocks)), []),
                               axis=1)
        zero_rows = jnp.zeros((half, GROUP), F32)
        return jnp.concatenate(sum(([zero_rows, wide[b * half:(b + 1) * half]] for b in range(blocks)), []), axis=0)

    return _each(lambda t, x: t + placed(x), inv, new)


def _chunk_local(r, lw, k, v, kk, kka, tri, t_idx, s_idx, bd_mask):
    bd = lambda y: _block_diag(y, bd_mask)
    bdmm = lambda xs, ys: _each(lambda x, y: _mm(x, bd(y)), xs, ys)
    bdmm2 = lambda xs, ys, zs: _each(lambda x, y, z: _mm(x, jnp.concatenate([bd(y), bd(z)], axis=1)), xs, ys, zs)
    left = lambda xs: _each(lambda x: x[:, :GROUP], xs)
    right = lambda xs: _each(lambda x: x[:, GROUP:], xs)
    top = lambda xs: _each(lambda x: x[:CHUNK], xs)
    bottom = lambda xs: _each(lambda x: x[CHUNK:], xs)

    cum = _each(lambda x: _mm_exact_lhs(tri, x, 3), lw)
    cum_last = _each(lambda c: c[CHUNK - 1:CHUNK, :], cum)
    p_incl = _each(jnp.exp, cum)
    p_excl = _each(lambda c, x: jnp.exp(c - x), cum, lw)
    p_inv = _each(lambda c: jnp.exp(-c), cum)
    to_end = _each(lambda cl, c: jnp.exp(cl - c), cum_last, cum)
    a_t = _each(lambda x, p: -x * p, kk, p_excl)
    r_t = _each(jnp.multiply, r, p_incl)
    b_t = _each(jnp.multiply, kka, p_inv)
    k_t = _each(jnp.multiply, k, p_inv)
    b_end = _each(jnp.multiply, kka, to_end)
    k_end = _each(jnp.multiply, k, to_end)

    ar = _each(lambda a, b: jnp.concatenate([a, b], axis=0), a_t, r_t)
    row2 = lax.broadcasted_iota(jnp.int32, (2 * CHUNK, 2 * GROUP), 0)
    col2 = lax.broadcasted_iota(jnp.int32, (2 * CHUNK, 2 * GROUP), 1) % CHUNK
    causal = col2 < (row2 % CHUNK) + (row2 // CHUNK)
    scores = _each(lambda x, y, z: jnp.where(causal, _mm_nt(x, jnp.concatenate([bd(y), bd(z)], axis=0)), 0.0),
                   ar, b_t, k_t)
    a_ab = top(left(scores))
    a_rb = bottom(left(scores))

    inv = _unit_lower_inverse(a_ab, t_idx, s_idx, bd)
    kv = bdmm(right(scores), v)
    wu = bdmm2(inv, a_t, top(kv))
    ro = bdmm2(a_rb, left(wu), right(wu))
    r_hat = _each(jnp.add, r_t, left(ro))
    o_loc = _each(jnp.add, right(ro), bottom(kv))

    wu_b = _each(lambda x, b: _mm(x.T, b), wu, b_end)
    trans = _each(lambda x: jnp.where(bd_mask, x[:GROUP], 0.0), wu_b)
    d_bd = _each(lambda x, y, z: jnp.where(bd_mask, x[GROUP:] + _mm(y.T, z), 0.0), wu_b, v, k_end)
    d_ls = _each(lambda d: sum(d[h * CHUNK:(h + 1) * CHUNK] for h in range(1, GROUP // CHUNK)) + d[:CHUNK], d_bd)
    p_end = _each(jnp.exp, cum_last)
    return r_hat, o_loc, trans, d_ls, p_end


def _rwkv_kernel(r_ref, lw_ref, k_ref, v_ref, kk_ref, kka_ref, tri_ref, wg_ref, wu_ref, wd_ref,
                 o_ref, wg_bf_ref, wu_bf_ref, wd_bf_ref, state_ref, *, n_sub):
    @pl.when(pl.program_id(1) == 0)
    def _():
        state_ref[...] = jnp.zeros_like(state_ref)

    wg_bf_ref[...] = _bf(wg_ref[...])
    wu_bf_ref[...] = _bf(wu_ref[...])
    wd_bf_ref[...] = _bf(wd_ref[...])

    t_idx = lax.broadcasted_iota(jnp.int32, (CHUNK, GROUP), 0)
    s_idx = lax.broadcasted_iota(jnp.int32, (CHUNK, GROUP), 1) % CHUNK
    bd_mask = (lax.broadcasted_iota(jnp.int32, (GROUP, GROUP), 0) // CHUNK
               == lax.broadcasted_iota(jnp.int32, (GROUP, GROUP), 1) // CHUNK)
    n_group = D_RWKV // GROUP
    where = [(slice(c * CHUNK, (c + 1) * CHUNK), slice(g * GROUP, (g + 1) * GROUP))
             for c in range(n_sub) for g in range(n_group)]
    load = lambda ref: [ref[0, rows, lanes].astype(F32) for rows, lanes in where]
    r_hat, o_loc, trans, d_ls, p_end = _chunk_local(
        load(r_ref), load(lw_ref), load(k_ref), load(v_ref), load(kk_ref), load(kka_ref),
        tri_ref[...], t_idx, s_idx, bd_mask)

    state = [state_ref[g] for g in range(n_group)]
    for c in range(n_sub):
        chains = range(c * n_group, (c + 1) * n_group)
        out = [_mm_nt(r_hat[i], _block_diag(state[g], bd_mask)) + o_loc[i] for g, i in enumerate(chains)]
        for g, i in enumerate(chains):
            rows, lanes = where[i]
            o_ref[0, rows, lanes] = _bf(out[g])
        state = [state[g] * p_end[i] + _mm(state[g], trans[i]) + d_ls[i] for g, i in enumerate(chains)]
    for g in range(n_group):
        state_ref[g] = state[g]


def _rwkv_call(r, lw, k, v, kk, kka, tri, expert_weights, tb):
    bsz, seq, _ = r.shape
    n_t = seq // tb
    per_step = N_EXPERTS // (bsz * n_t)
    assert per_step * bsz * n_t == N_EXPERTS, "grid steps must divide the expert count"
    spec = pl.BlockSpec((1, tb, D_RWKV), lambda b, t: (b, t, 0))
    w_specs = [pl.BlockSpec((per_step,) + w.shape[1:], lambda b, t: (b * n_t + t, 0, 0)) for w in expert_weights]
    return pl.pallas_call(
        functools.partial(_rwkv_kernel, n_sub=tb // CHUNK),
        grid=(bsz, n_t),
        in_specs=[spec] * 6 + [pl.BlockSpec(tri.shape, lambda b, t: (0, 0))] + w_specs,
        out_specs=[spec] + w_specs,
        out_shape=[jax.ShapeDtypeStruct((bsz, seq, D_RWKV), BF16)]
        + [jax.ShapeDtypeStruct(w.shape, BF16) for w in expert_weights],
        scratch_shapes=[pltpu.VMEM((D_RWKV // GROUP, HEAD, GROUP), F32)],
        compiler_params=pltpu.CompilerParams(
            dimension_semantics=("arbitrary", "arbitrary"), vmem_limit_bytes=VMEM_LIMIT),
        name="rwkv_chunk",
    )(r, lw, k, v, kk, kka, tri, *expert_weights)


def _route(logits):
    lane_i = lax.broadcasted_iota(jnp.int32, logits.shape, 1)
    lane = lane_i.astype(F32)
    lane_group = (lane_i // EXPERTS_PER_GROUP).astype(F32)
    neg = -jnp.inf
    big = float(ROUTER_LANES)
    is_group = (lane_i >= N_EXPERTS) & (lane_i < N_EXPERTS + N_GROUPS)
    gl = jnp.where(is_group, logits, neg)
    ge = jnp.exp(gl - jnp.max(gl, axis=-1, keepdims=True))
    gprob = ge / jnp.sum(ge, axis=-1, keepdims=True)
    g_top = jnp.max(gprob, axis=-1, keepdims=True)
    g_idx = jnp.min(jnp.where(is_group & (gprob == g_top), lane - N_EXPERTS, big), axis=-1, keepdims=True)

    in_group = (lane_i < N_EXPERTS) & (lane_group == g_idx)
    el = jnp.where(in_group, logits, neg)
    top1 = jnp.max(el, axis=-1, keepdims=True)
    idx1 = jnp.min(jnp.where(in_group & (el == top1), lane, big), axis=-1, keepdims=True)
    el2 = jnp.where(lane == idx1, neg, el)
    top2 = jnp.max(el2, axis=-1, keepdims=True)
    idx2 = jnp.min(jnp.where(in_group & (lane != idx1) & (el2 == top2), lane, big), axis=-1, keepdims=True)
    e2 = jnp.exp(top2 - top1)
    denom = 1.0 + e2
    return idx1, idx2, g_top * (1.0 / denom), g_top * (e2 / denom)


def _pack_bf16_pairs(x):
    bits = lax.bitcast_convert_type(_bf(x).astype(F32), jnp.uint32)
    k = x.shape[1] // 2
    return lax.bitcast_convert_type((bits[:, :k] >> 16) | bits[:, k:], jnp.int32)


def _unpack_bf16_pairs(words):
    bits = lax.bitcast_convert_type(words, jnp.uint32)
    lo = lax.bitcast_convert_type(bits << 16, F32)
    hi = lax.bitcast_convert_type(bits & jnp.uint32(0xFFFF0000), F32)
    return _bf(jnp.concatenate([lo, hi], axis=1))


LANE_IDX1, LANE_IDX2, LANE_RANK1, LANE_RANK2, LANE_GATE1, LANE_GATE2 = range(6)


def _out_kernel(o_ref, bonus_ref, gate_ref, yconv_ref, x_ref, ones_ref, lnw_ref, lnb_ref,
                wout_ref, gffn_ref, rw_ref, rb_ref, tri_ref, h_ref, u_ref, route_ref, route_t_ref, count_ref,
                seen_ref, wbf_ref, *, n_split):
    @pl.when(pl.program_id(0) == 0)
    def _():
        seen_ref[...] = jnp.zeros_like(seen_ref)
        wbf_ref[...] = _bf(wout_ref[...])

    sub = o_ref.shape[0] // n_split
    parts = [slice(s * sub, (s + 1) * sub) for s in range(n_split)]
    read = lambda ref: [ref[p, :] for p in parts]
    ones_quad = ones_ref[...]
    inv_n = 1.0 / HEAD
    o = _each(lambda v: v.astype(F32), read(o_ref))
    mean = _each(lambda v: _head_sum(v, ones_quad) * inv_n, o)
    cen = _each(jnp.subtract, o, mean)
    var = _each(lambda c: _head_sum(c * c, ones_quad) * inv_n, cen)
    on = _each(lambda c, v: c * lax.rsqrt(v + LN_X_EPS) * lnw_ref[...] + lnb_ref[...], cen, var)
    y_rwkv = _each(lambda a, b, g: (a + b) * g, on, read(bonus_ref), read(gate_ref))
    mixed = _each(lambda yc, yr: (jnp.dot(yc, wbf_ref[:D_CONV, :], preferred_element_type=F32)
                                  + jnp.dot(_bf(yr), wbf_ref[D_CONV:, :], preferred_element_type=F32)),
                  read(yconv_ref), y_rwkv)
    h = _each(jnp.add, read(x_ref), mixed)
    u = _each(lambda v: _rms_norm(v, gffn_ref[...]), h)
    u_hi = _each(_bf, u)
    u_lo = _each(lambda a, b: _bf(a - b.astype(F32)), u, u_hi)
    by_hi = _each(lambda a: jnp.dot(a, rw_ref[...], preferred_element_type=F32), u_hi)
    by_lo = _each(lambda a: jnp.dot(a, rw_ref[:, :ROUTER_LANES], preferred_element_type=F32), u_lo)
    logits = _each(lambda a, b: a[:, :ROUTER_LANES] + a[:, ROUTER_LANES:] + b + rb_ref[...], by_hi, by_lo)
    routed = _each(_route, logits)

    lane_i = lax.broadcasted_iota(jnp.int32, (sub, ROUTER_LANES), 1)
    lane = lane_i.astype(F32)
    hit1 = _each(lambda rt: lane == rt[0], routed)
    hit2 = _each(lambda rt: lane == rt[1], routed)
    both = _each(lambda a, b: a.astype(F32) + b.astype(F32), hit1, hit2)
    inside = _each(lambda b: jnp.dot(tri_ref[...], _bf(b), preferred_element_type=F32), both)
    seen = seen_ref[...]
    for s, p in enumerate(parts):
        idx1, idx2, gate1, gate2 = routed[s]
        before = inside[s] + seen
        rank1 = jnp.sum(jnp.where(hit1[s], before, 0.0), axis=-1, keepdims=True)
        rank2 = jnp.sum(jnp.where(hit2[s], before, 0.0), axis=-1, keepdims=True)
        seen = seen + jnp.sum(both[s], axis=0, keepdims=True)
        route = jnp.zeros((sub, ROUTER_LANES), F32)
        for lane_id, col in ((LANE_IDX1, idx1), (LANE_IDX2, idx2), (LANE_RANK1, rank1),
                             (LANE_RANK2, rank2), (LANE_GATE1, gate1), (LANE_GATE2, gate2)):
            route = jnp.where(lane_i == lane_id, col, route)
        route_ref[p, :] = route
        route_t_ref[:, p] = route.T[:SUBLANES, :]
        h_ref[p, :] = _bf(h[s])
        u_ref[p, :] = _pack_bf16_pairs(u[s])
    seen_ref[...] = seen
    count_ref[...] = jnp.broadcast_to(seen, count_ref.shape)


def _out_call(o, bonus, gate, yconv, x, ones_bd, ln_w, ln_b, w_out, g_ffn, router_w, router_b, tm, n_split):
    n_tok = x.shape[0]
    row = lambda width: pl.BlockSpec((tm, width), lambda i: (i, 0))
    full = lambda arr: pl.BlockSpec(arr.shape, lambda i: (0,) * arr.ndim, pipeline_mode=pl.Buffered(1))
    sub = tm // n_split
    tri_strict = (jnp.arange(sub)[:, None] > jnp.arange(sub)[None, :]).astype(BF16)
    params = (ones_bd, ln_w, ln_b, w_out, g_ffn, router_w, router_b, tri_strict)
    return pl.pallas_call(
        functools.partial(_out_kernel, n_split=n_split),
        grid=(n_tok // tm,),
        in_specs=[row(D_RWKV)] * 4 + [row(D_MODEL)] + [full(p) for p in params],
        out_specs=[row(D_MODEL), row(D_MODEL // 2), row(ROUTER_LANES),
                   pl.BlockSpec((SUBLANES, tm), lambda i: (0, i)),
                   pl.BlockSpec((SUBLANES, ROUTER_LANES), lambda i: (0, 0))],
        out_shape=[jax.ShapeDtypeStruct((n_tok, D_MODEL), BF16),
                   jax.ShapeDtypeStruct((n_tok, D_MODEL // 2), jnp.int32),
                   jax.ShapeDtypeStruct((n_tok, ROUTER_LANES), F32),
                   jax.ShapeDtypeStruct((SUBLANES, n_tok), F32),
                   jax.ShapeDtypeStruct((SUBLANES, ROUTER_LANES), F32)],
        scratch_shapes=[pltpu.VMEM((1, ROUTER_LANES), F32), pltpu.VMEM(w_out.shape, BF16)],
        compiler_params=pltpu.CompilerParams(
            dimension_semantics=("arbitrary",), vmem_limit_bytes=VMEM_LIMIT),
        name="out_proj_route",
    )(o, bonus, gate, yconv, x, *params)


SC_CORES = 2
SC_SUBCORES = 16
SC_ROWS = 64


def _sc_mesh():
    return plsc.VectorSubcoreMesh(core_axis_name="c", subcore_axis_name="s",
                                  num_cores=SC_CORES, num_subcores=SC_SUBCORES)


def _sc_worker():
    return lax.axis_index("s") * SC_CORES + lax.axis_index("c")


def _sc_gather(table, idx):
    n_rows = idx.shape[0]
    width = table.shape[1]
    n_chunks = n_rows // (SC_CORES * SC_SUBCORES * SC_ROWS)

    def body(table_hbm, idx_hbm, out_hbm, idx_v, rows_v, gather_sem, write_sem):
        first = _sc_worker() * n_chunks
        pltpu.sync_copy(idx_hbm.at[pl.ds(first, n_chunks)], idx_v)
        gather = lambda j: pltpu.async_copy(table_hbm.at[idx_v.at[j]], rows_v.at[j % 2], gather_sem.at[j % 2])
        gathers = [gather(0)]
        writes = []
        for j in range(n_chunks):
            gathers[j].wait()
            if j + 1 < n_chunks:
                if j >= 1:
                    writes[j - 1].wait()
                gathers.append(gather(j + 1))
            dst = out_hbm.at[pl.ds(pl.multiple_of((first + j) * SC_ROWS, SC_ROWS), SC_ROWS)]
            writes.append(pltpu.async_copy(rows_v.at[j % 2], dst, write_sem.at[j % 2]))
        for j in range(max(n_chunks - 2, 0), n_chunks):
            writes[j].wait()

    return pl.kernel(
        body,
        out_type=jax.ShapeDtypeStruct((n_rows, width), table.dtype),
        mesh=_sc_mesh(),
        scratch_types=[pltpu.VMEM((n_chunks, SC_ROWS), jnp.int32), pltpu.VMEM((2, SC_ROWS, width), table.dtype),
                       pltpu.SemaphoreType.DMA((2,)), pltpu.SemaphoreType.DMA((2,))],
        name="sc_row_gather",
    )(table, idx.reshape(n_rows // SC_ROWS, SC_ROWS))


def _sc_scatter(rows, pos, n_out):
    n_rows, width = rows.shape
    n_slots = pos.shape[0] // n_rows
    slot_chunks = n_rows // SC_ROWS
    n_chunks = slot_chunks // (SC_CORES * SC_SUBCORES)

    def body(rows_hbm, pos_hbm, out_hbm, idx_v, rows_v, read_sem, scatter_sem):
        first = _sc_worker() * n_chunks
        for s in range(n_slots):
            pltpu.sync_copy(pos_hbm.at[pl.ds(s * slot_chunks + first, n_chunks)], idx_v.at[s])
        read = lambda j: pltpu.async_copy(
            rows_hbm.at[pl.ds(pl.multiple_of((first + j) * SC_ROWS, SC_ROWS), SC_ROWS)],
            rows_v.at[j % 2], read_sem.at[j % 2])
        reads = [read(0)]
        scatters = []
        for j in range(n_chunks):
            reads[j].wait()
            if j + 1 < n_chunks:
                if j >= 1:
                    for copy in scatters[j - 1]:
                        copy.wait()
                reads.append(read(j + 1))
            scatters.append([pltpu.async_copy(rows_v.at[j % 2], out_hbm.at[idx_v.at[s, j]], scatter_sem.at[j % 2])
                             for s in range(n_slots)])
        for j in range(max(n_chunks - 2, 0), n_chunks):
            for copy in scatters[j]:
                copy.wait()

    return pl.kernel(
        body,
        out_type=jax.ShapeDtypeStruct((n_out, width), rows.dtype),
        mesh=_sc_mesh(),
        scratch_types=[pltpu.VMEM((n_slots, n_chunks, SC_ROWS), jnp.int32),
                       pltpu.VMEM((2, SC_ROWS, width), rows.dtype),
                       pltpu.SemaphoreType.DMA((2,)), pltpu.SemaphoreType.DMA((2,))],
        name="sc_row_scatter",
    )(rows, pos.reshape(n_slots * slot_chunks, SC_ROWS))


def _expert_kernel(tile_expert_ref, n_valid_ref, x_ref, wg_ref, wu_ref, wd_ref, y_ref):
    del tile_expert_ref

    @pl.when(pl.program_id(0) < n_valid_ref[0])
    def _():
        x = _unpack_bf16_pairs(x_ref[...])
        gate = jnp.dot(x, wg_ref[0], preferred_element_type=F32)
        up = jnp.dot(x, wu_ref[0], preferred_element_type=F32)
        hid = gate * jax.nn.sigmoid(gate) * up
        y_ref[...] = _pack_bf16_pairs(jnp.dot(_bf(hid), wd_ref[0], preferred_element_type=F32))


def _expert_call(tile_expert, n_valid, x_sorted, w_gate, w_up, w_down, tm):
    n_rows = x_sorted.shape[0]
    rows = pl.BlockSpec((tm, D_MODEL // 2), lambda i, te, nv: (jnp.minimum(i, nv[0] - 1), 0))
    by_expert = lambda i, te, nv: (te[i], 0, 0)
    return pl.pallas_call(
        _expert_kernel,
        grid_spec=pltpu.PrefetchScalarGridSpec(
            num_scalar_prefetch=2,
            grid=(n_rows // tm,),
            in_specs=[rows,
                      pl.BlockSpec((1, D_MODEL, D_EXPERT), by_expert),
                      pl.BlockSpec((1, D_MODEL, D_EXPERT), by_expert),
                      pl.BlockSpec((1, D_EXPERT, D_MODEL), by_expert)],
            out_specs=rows),
        out_shape=jax.ShapeDtypeStruct((n_rows, D_MODEL // 2), jnp.int32),
        compiler_params=pltpu.CompilerParams(
            dimension_semantics=("arbitrary",), vmem_limit_bytes=VMEM_LIMIT),
        name="moe_experts",
    )(tile_expert, n_valid, x_sorted, w_gate, w_up, w_down)


def _final_kernel(h_ref, y1_ref, y2_ref, route_ref, gfin_ref, *rest):
    out_ref = rest[-1]
    route = route_ref[...]
    gate1 = route[:, LANE_GATE1:LANE_GATE1 + 1]
    gate2 = route[:, LANE_GATE2:LANE_GATE2 + 1]
    moe = gate1 * _unpack_bf16_pairs(y1_ref[...]).astype(F32) + gate2 * _unpack_bf16_pairs(y2_ref[...]).astype(F32)
    out_ref[...] = _rms_norm(h_ref[...] + moe, gfin_ref[...])


def _final_call(h, y_pairs, route, g_final, earlier, first_token, tm):
    n_tok = h.shape[0]
    n_blocks = y_pairs.shape[0] // 2 // tm
    assert n_blocks * tm * 2 == y_pairs.shape[0] and first_token % tm == 0, "chunks must be whole row blocks"
    first = first_token // tm
    in_specs = [pl.BlockSpec((tm, D_MODEL), lambda i: (first + i, 0)),
                pl.BlockSpec((tm, D_MODEL // 2), lambda i: (i, 0)),
                pl.BlockSpec((tm, D_MODEL // 2), lambda i: (i + n_blocks, 0)),
                pl.BlockSpec((tm, ROUTER_LANES), lambda i: (first + i, 0)),
                pl.BlockSpec((1, D_MODEL), lambda i: (0, 0))]
    args = [h, y_pairs, y_pairs, route, g_final]
    aliases = {}
    if earlier is not None:
        in_specs.append(pl.BlockSpec(memory_space=pl.ANY))
        args.append(earlier)
        aliases = {len(args) - 1: 0}
    return pl.pallas_call(
        _final_kernel,
        grid=(n_blocks,),
        in_specs=in_specs,
        out_specs=pl.BlockSpec((tm, D_MODEL), lambda i: (first + i, 0)),
        out_shape=jax.ShapeDtypeStruct((n_tok, D_MODEL), F32),
        input_output_aliases=aliases,
        compiler_params=pltpu.CompilerParams(
            dimension_semantics=("arbitrary",), vmem_limit_bytes=VMEM_LIMIT),
        name="moe_combine_norm",
    )(*args)


def _dispatch_plan(route_t, counts, tm):
    n_tok = route_t.shape[1]
    n_tiles = (2 * n_tok) // tm + N_EXPERTS
    counts = counts[0, :N_EXPERTS].astype(jnp.int32)
    tiles_per = (counts + tm - 1) // tm
    tile_end = jnp.cumsum(tiles_per)
    row_start = (tile_end - tiles_per) * tm
    experts = jnp.arange(N_EXPERTS, dtype=jnp.int32)

    def position(idx_lane, rank_lane):
        idx = route_t[idx_lane].astype(jnp.int32)
        start = jnp.sum(jnp.where(idx[None, :] == experts[:, None], row_start[:, None], 0), axis=0)
        return start + route_t[rank_lane].astype(jnp.int32)

    pos = jnp.concatenate([position(LANE_IDX1, LANE_RANK1), position(LANE_IDX2, LANE_RANK2)])
    n_valid = tile_end[-1:]
    tile = jnp.minimum(jnp.arange(n_tiles, dtype=jnp.int32), n_valid - 1)
    tile_expert = jnp.sum((tile_end[None, :] <= tile[:, None]).astype(jnp.int32), axis=1)
    return pos, n_tiles, tile_expert, n_valid


def _block(x, norm_mix_g, w_in, rwkv_mu, conv_w, decay_up, decay_base, aaa_up, aaa_base, gate_up,
           k_k, k_a, r_k, ln_x_w, ln_x_b, w_out, norm_ffn_g, router_group_w, router_group_b,
           router_expert_w, router_expert_b, expert_w_gate, expert_w_up, expert_w_down, norm_final_g,
           *, tm_in, tb_rwkv, tm_out, tm_expert, tm_final):
    bsz, seq, d_model = x.shape
    n_tok = bsz * seq
    row = lambda p: p.reshape(1, -1)
    half = LORA_WA // 2
    zeros = jnp.zeros((half, D_RWKV), F32)
    dup = _bf(jnp.concatenate([decay_up, zeros], axis=0))
    aup = _bf(jnp.concatenate([zeros, aaa_up], axis=0))
    head_of = jnp.arange(QUAD) // HEAD
    ones_quad = (head_of[:, None] == head_of[None, :]).astype(BF16)
    tri = (jnp.arange(CHUNK)[:, None] >= jnp.arange(CHUNK)[None, :]).astype(BF16)

    yconv, r, lw, k2, v, kk, kka, gate, bonus = _in_call(
        x, row(norm_mix_g), w_in, row(rwkv_mu), conv_w, dup, row(decay_base), aup, row(aaa_base),
        _bf(gate_up), row(k_k), row(k_a), row(r_k), ones_quad, tm_in)
    o, w_gate_bf, w_up_bf, w_down_bf = _rwkv_call(r, lw, k2, v, kk, kka, tri,
                                                  (expert_w_gate, expert_w_up, expert_w_down), tb_rwkv)

    pad = ROUTER_LANES - N_EXPERTS - N_GROUPS
    router_w = jnp.concatenate([router_expert_w, router_group_w, jnp.zeros((d_model, pad), F32)], axis=1)
    router_hi = _bf(router_w)
    router_split = jnp.concatenate([router_hi, _bf(router_w - router_hi.astype(F32))], axis=1)
    router_b = jnp.concatenate([router_expert_b, router_group_b, jnp.zeros((pad,), F32)]).reshape(1, -1)
    flat = lambda t: t.reshape(n_tok, t.shape[-1])
    h, u_pairs, route, route_t, counts = _out_call(
        flat(o), flat(bonus), flat(gate), flat(yconv), flat(x), ones_quad,
        row(ln_x_w), row(ln_x_b), w_out, row(norm_ffn_g), router_split, router_b, tm_out, OUT_SPLIT)

    pos, n_tiles, tile_expert, n_valid = _dispatch_plan(route_t, counts, tm_expert)
    x_sorted = _sc_scatter(u_pairs, pos, n_tiles * tm_expert)
    y_sorted = _expert_call(tile_expert, n_valid, x_sorted, w_gate_bf, w_up_bf, w_down_bf, tm_expert)
    out, lo = None, 0
    for fraction in COMBINE_SPLIT:
        size = n_tok // fraction
        pos_c = jnp.concatenate([pos[lo:lo + size], pos[n_tok + lo:n_tok + lo + size]])
        out = _final_call(h, _sc_gather(y_sorted, pos_c), route, row(norm_final_g), out, lo, tm_final)
        lo += size
    assert lo == n_tok, "COMBINE_SPLIT must cover all tokens"
    return out.reshape(bsz, seq, d_model)


def kernel(x, norm_mix_g, w_in, rwkv_mu, conv_w, decay_up, decay_base, aaa_up, aaa_base, gate_up, k_k, k_a, r_k, ln_x_w, ln_x_b, w_out, norm_ffn_g, router_group_w, router_group_b, router_expert_w, router_expert_b, expert_w_gate, expert_w_up, expert_w_down, norm_final_g):
    return _block(x, norm_mix_g[0], w_in[0], rwkv_mu[0], conv_w[0], decay_up[0], decay_base[0],
                  aaa_up[0], aaa_base[0], gate_up[0], k_k[0], k_a[0], r_k[0].reshape(-1), ln_x_w[0],
                  ln_x_b[0], w_out[0], norm_ffn_g[0], router_group_w[0], router_group_b[0],
                  router_expert_w[0], router_expert_b[0], expert_w_gate[0], expert_w_up[0],
                  expert_w_down[0], norm_final_g,
                  tm_in=512, tb_rwkv=512, tm_out=512, tm_expert=512, tm_final=512)
```

```python
import functools

import jax
import jax.numpy as jnp
from jax import lax
from jax.experimental import pallas as pl
from jax.experimental.pallas import tpu as pltpu
from jax.experimental.pallas import tpu_sc as plsc

F32 = jnp.float32
BF16 = jnp.bfloat16

D_MODEL = 1024
D_CONV = 512
CONV_WIDTH = 3
N_HEADS = 8
HEAD = 64
D_RWKV = N_HEADS * HEAD
LORA_WA = 128
GATE_LORA = 128
D_RWKV_PROJ = 3 * D_RWKV + LORA_WA + GATE_LORA
D_IN = 3 * D_CONV + D_RWKV_PROJ
N_GROUPS = 4
EXPERTS_PER_GROUP = 8
N_EXPERTS = N_GROUPS * EXPERTS_PER_GROUP
D_EXPERT = D_MODEL // 4
RMS_EPS = 1e-6
LN_X_EPS = 64e-5
L2_EPS = 1e-12

SUBLANES = 8
CHUNK = 64
QUAD = 4 * HEAD
GROUP = 2 * HEAD
INV_BASE = 8
COMPACT_FROM = 64
OUT_SPLIT = 4
COMBINE_SPLIT = (4, 4, 4, 4)
ROUTER_LANES = 128

VMEM_LIMIT = 56 * 1024 * 1024


def _bf(x):
    return x.astype(BF16)


def _mm(a, b):
    return jnp.dot(_bf(a), _bf(b), preferred_element_type=F32)


def _mm_nt(a, b):
    return lax.dot_general(_bf(a), _bf(b), (((1,), (1,)), ((), ())), preferred_element_type=F32)


def _mm_exact_lhs(lhs_bf16, x, passes):
    acc = None
    rem = x
    for _ in range(passes):
        piece = _bf(rem)
        part = jnp.dot(lhs_bf16, piece, preferred_element_type=F32)
        acc = part if acc is None else acc + part
        rem = rem - piece.astype(F32)
    return acc


def _head_sum(x, ones_quad):
    xb = _bf(x)
    return jnp.concatenate(
        [jnp.dot(xb[:, q * QUAD:(q + 1) * QUAD], ones_quad, preferred_element_type=F32)
         for q in range(x.shape[1] // QUAD)], axis=1)


def _rms_norm(x, g):
    return x * lax.rsqrt(jnp.mean(x * x, axis=-1, keepdims=True) + RMS_EPS) * g


def _shift_rows(cur, prev_rows, k):
    rolled = pltpu.roll(cur, k, 0)
    prev_rolled = pltpu.roll(prev_rows, k, 0)
    n = cur.shape[0]
    head = jnp.concatenate([prev_rolled, rolled[SUBLANES:]], axis=0) if n > SUBLANES else prev_rolled
    row = lax.broadcasted_iota(jnp.int32, cur.shape, 0)
    return jnp.where(row < k, head, rolled)


def _in_kernel(x_ref, g_ref, w_ref, mu_ref, convw_ref, dup_ref, dbase_ref, aup_ref, abase_ref,
               gup_ref, kk_ref, ka_ref, rk_ref, ones_ref,
               yconv_ref, r_ref, lw_ref, k_ref, v_ref, kkn_ref, kka_ref, gate_ref, bonus_ref,
               carry_ref, wbf_ref):
    @pl.when((pl.program_id(0) == 0) & (pl.program_id(1) == 0))
    def _():
        wbf_ref[...] = _bf(w_ref[...])

    @pl.when(pl.program_id(1) == 0)
    def _():
        carry_ref[...] = jnp.zeros_like(carry_ref)

    u = _bf(_rms_norm(x_ref[0], g_ref[...]))
    n_conv = 3 * D_CONV
    rk0 = n_conv
    bounds = dict(conv=(0, n_conv), lora=(rk0 + 3 * D_RWKV, D_IN), k=(rk0 + D_RWKV, rk0 + 2 * D_RWKV),
                  r=(rk0, rk0 + D_RWKV), v=(rk0 + 2 * D_RWKV, rk0 + 3 * D_RWKV))
    z = {name: jnp.dot(u, wbf_ref[:, lo:hi], preferred_element_type=F32) for name, (lo, hi) in bounds.items()}

    def lerp(name):
        lo, hi = bounds[name]
        cur = z[name]
        prev = carry_ref[:, D_CONV + lo - n_conv:D_CONV + hi - n_conv]
        mixed = cur + (_shift_rows(cur, prev, 1) - cur) * mu_ref[:, lo - n_conv:hi - n_conv]
        carry_ref[:, D_CONV + lo - n_conv:D_CONV + hi - n_conv] = cur[-SUBLANES:]
        return mixed

    b_gate = z["conv"][:, :D_CONV]
    ch = z["conv"][:, D_CONV:2 * D_CONV] * z["conv"][:, 2 * D_CONV:]
    prev_ch = carry_ref[:, :D_CONV]
    conv = convw_ref[CONV_WIDTH - 1:CONV_WIDTH, :] * ch
    for delay in range(1, CONV_WIDTH):
        tap = CONV_WIDTH - 1 - delay
        conv = conv + convw_ref[tap:tap + 1, :] * _shift_rows(ch, prev_ch, delay)
    yconv_ref[0] = _bf(b_gate * conv)
    carry_ref[:, :D_CONV] = ch[-SUBLANES:]

    lora = lerp("lora")
    wa_lo = lora[:, :LORA_WA]
    g_lo = lora[:, LORA_WA:]
    dec_in = -(dbase_ref[...] + jnp.dot(_bf(jnp.tanh(wa_lo)), dup_ref[...], preferred_element_type=F32))
    softplus = jnp.maximum(dec_in, 0.0) + jnp.log(1.0 + jnp.exp(-jnp.abs(dec_in)))
    w = -softplus - 0.5
    lw_ref[0] = -jnp.exp(w)
    a = jax.nn.sigmoid(abase_ref[...] + jnp.dot(_bf(wa_lo), aup_ref[...], preferred_element_type=F32))
    gate_ref[0] = _bf(jnp.dot(_bf(jax.nn.sigmoid(g_lo)), gup_ref[...], preferred_element_type=F32))

    ones_quad = ones_ref[...]
    k = lerp("k")
    kk = k * kk_ref[...]
    norm = jnp.sqrt(_head_sum(kk * kk, ones_quad))
    kk = kk / jnp.maximum(norm, L2_EPS)
    k2 = k * (1.0 + (a - 1.0) * ka_ref[...])
    k_ref[0] = _bf(k2)
    kkn_ref[0] = _bf(kk)
    kka_ref[0] = _bf(kk * a)
    r = lerp("r")
    r_ref[0] = _bf(r)
    v = lerp("v")
    v_ref[0] = _bf(v)
    bonus_ref[0] = _bf(_head_sum(r * k2 * rk_ref[...], ones_quad) * v)


def _in_call(x, g, w_in, mu, conv_w, dup, dbase, aup, abase, gup, k_k, k_a, r_k, ones_quad, tm):
    bsz, seq, _ = x.shape
    full = lambda arr: pl.BlockSpec(arr.shape, lambda b, t: (0,) * arr.ndim, pipeline_mode=pl.Buffered(1))
    out_spec = pl.BlockSpec((1, tm, D_RWKV), lambda b, t: (b, t, 0))
    out_dtypes = (BF16, BF16, F32, BF16, BF16, BF16, BF16, BF16, BF16)
    params = (g, w_in, mu, conv_w, dup, dbase, aup, abase, gup, k_k, k_a, r_k, ones_quad)
    return pl.pallas_call(
        _in_kernel,
        grid=(bsz, seq // tm),
        in_specs=[pl.BlockSpec((1, tm, D_MODEL), lambda b, t: (b, t, 0))] + [full(p) for p in params],
        out_specs=[out_spec] * 9,
        out_shape=[jax.ShapeDtypeStruct((bsz, seq, D_RWKV), dt) for dt in out_dtypes],
        scratch_shapes=[pltpu.VMEM((SUBLANES, D_CONV + D_RWKV_PROJ), F32), pltpu.VMEM(w_in.shape, BF16)],
        compiler_params=pltpu.CompilerParams(
            dimension_semantics=("arbitrary", "arbitrary"), vmem_limit_bytes=VMEM_LIMIT),
        name="in_proj",
    )(x, *params)


def _block_diag(y, bd_mask):
    return jnp.where(bd_mask, jnp.concatenate([y] * (GROUP // CHUNK), axis=0), 0.0)


def _each(fn, *lists):
    return [fn(*args) for args in zip(*lists)]


def _unit_lower_inverse(a_strict, t_idx, s_idx, bd):
    bdmm = lambda xs, ys: _each(lambda x, y: _mm(x, bd(y)), xs, ys)
    eye = (t_idx == s_idx).astype(F32)
    same8 = (t_idx // INV_BASE) == (s_idx // INV_BASE)
    a8 = _each(lambda a: jnp.where(same8, a, 0.0), a_strict)
    a8_2 = bdmm(a8, a8)
    a8_34 = bdmm(_each(lambda a, b: jnp.concatenate([a, b], axis=0), a8, a8_2), a8_2)
    inv = _each(lambda a, b, c: eye + a + b + c[:CHUNK], a8, a8_2, a8_34)
    inv = _each(jnp.add, inv, bdmm(inv, _each(lambda c: c[CHUNK:], a8_34)))
    size = 2 * INV_BASE
    while size < COMPACT_FROM:
        off = ((t_idx // size) == (s_idx // size)) & ((t_idx // (size // 2)) != (s_idx // (size // 2)))
        cross = bdmm(_each(lambda a: jnp.where(off, a, 0.0), a_strict), inv)
        inv = _each(jnp.add, inv, bdmm(inv, cross))
        size *= 2
    while size <= CHUNK:
        inv = _compact_level(a_strict, inv, size)
        size *= 2
    return inv


def _compact_level(a_strict, inv, size):
    half = size // 2
    heads = GROUP // CHUNK
    blocks = CHUNK // size
    lane_starts = [h * CHUNK + b * size for h in range(heads) for b in range(blocks)]
    row_starts = [b * size for b in range(blocks)]
    rows_first = lambda x: jnp.concatenate([x[r:r + half] for r in row_starts], axis=0)
    rows_second = lambda x: jnp.concatenate([x[r + half:r + size] for r in row_starts], axis=0)
    lanes_first = lambda x: jnp.concatenate([x[:, c:c + half] for c in lane_starts], axis=1)
    lanes_second = lambda x: jnp.concatenate([x[:, c + half:c + size] for c in lane_starts], axis=1)
    n_rows, n_lanes = CHUNK // 2, GROUP // 2
    own = (lax.broadcasted_iota(jnp.int32, (n_rows, n_lanes), 0) // half
           == (lax.broadcasted_iota(jnp.int32, (n_rows, n_lanes), 1) // half) % blocks)
    diag = (lax.broadcasted_iota(jnp.int32, (n_lanes, n_lanes), 0) // half
            == lax.broadcasted_iota(jnp.int32, (n_lanes, n_lanes), 1) // half)
    bd_half = lambda y: jnp.where(diag, jnp.concatenate([y] * heads, axis=0), 0.0)
    a21 = _each(lambda a: jnp.where(own, lanes_first(rows_second(a)), 0.0), a_strict)
    t11 = _each(lambda t: lanes_first(rows_first(t)), inv)
    t22 = _each(lambda t: lanes_second(rows_second(t)), inv)
    a21_t11 = _each(lambda x, y: _mm(x, bd_half(y)), a21, t11)
    new = _each(lambda x, y: _mm(x, bd_half(y)), t22, a21_t11)

    def placed(x):
        zero_lanes = jnp.zeros((n_rows, half), F32)
        wide = jnp.concatenate(sum(([x[:, i * half:(i + 1) * half], zero_lanes] for i in range(heads * blocks)), []),
                               axis=1)
        zero_rows = jnp.zeros((half, GROUP), F32)
        return jnp.concatenate(sum(([zero_rows, wide[b * half:(b + 1) * half]] for b in range(blocks)), []), axis=0)

    return _each(lambda t, x: t + placed(x), inv, new)


def _chunk_local(r, lw, k, v, kk, kka, tri, t_idx, s_idx, bd_mask):
    bd = lambda y: _block_diag(y, bd_mask)
    bdmm = lambda xs, ys: _each(lambda x, y: _mm(x, bd(y)), xs, ys)
    bdmm2 = lambda xs, ys, zs: _each(lambda x, y, z: _mm(x, jnp.concatenate([bd(y), bd(z)], axis=1)), xs, ys, zs)
    left = lambda xs: _each(lambda x: x[:, :GROUP], xs)
    right = lambda xs: _each(lambda x: x[:, GROUP:], xs)
    top = lambda xs: _each(lambda x: x[:CHUNK], xs)
    bottom = lambda xs: _each(lambda x: x[CHUNK:], xs)

    cum = _each(lambda x: _mm_exact_lhs(tri, x, 3), lw)
    cum_last = _each(lambda c: c[CHUNK - 1:CHUNK, :], cum)
    p_incl = _each(jnp.exp, cum)
    p_excl = _each(lambda c, x: jnp.exp(c - x), cum, lw)
    p_inv = _each(lambda c: jnp.exp(-c), cum)
    to_end = _each(lambda cl, c: jnp.exp(cl - c), cum_last, cum)
    a_t = _each(lambda x, p: -x * p, kk, p_excl)
    r_t = _each(jnp.multiply, r, p_incl)
    b_t = _each(jnp.multiply, kka, p_inv)
    k_t = _each(jnp.multiply, k, p_inv)
    b_end = _each(jnp.multiply, kka, to_end)
    k_end = _each(jnp.multiply, k, to_end)

    ar = _each(lambda a, b: jnp.concatenate([a, b], axis=0), a_t, r_t)
    row2 = lax.broadcasted_iota(jnp.int32, (2 * CHUNK, 2 * GROUP), 0)
    col2 = lax.broadcasted_iota(jnp.int32, (2 * CHUNK, 2 * GROUP), 1) % CHUNK
    causal = col2 < (row2 % CHUNK) + (row2 // CHUNK)
    scores = _each(lambda x, y, z: jnp.where(causal, _mm_nt(x, jnp.concatenate([bd(y), bd(z)], axis=0)), 0.0),
                   ar, b_t, k_t)
    a_ab = top(left(scores))
    a_rb = bottom(left(scores))

    inv = _unit_lower_inverse(a_ab, t_idx, s_idx, bd)
    kv = bdmm(right(scores), v)
    wu = bdmm2(inv, a_t, top(kv))
    ro = bdmm2(a_rb, left(wu), right(wu))
    r_hat = _each(jnp.add, r_t, left(ro))
    o_loc = _each(jnp.add, right(ro), bottom(kv))

    wu_b = _each(lambda x, b: _mm(x.T, b), wu, b_end)
    trans = _each(lambda x: jnp.where(bd_mask, x[:GROUP], 0.0), wu_b)
    d_bd = _each(lambda x, y, z: jnp.where(bd_mask, x[GROUP:] + _mm(y.T, z), 0.0), wu_b, v, k_end)
    d_ls = _each(lambda d: sum(d[h * CHUNK:(h + 1) * CHUNK] for h in range(1, GROUP // CHUNK)) + d[:CHUNK], d_bd)
    p_end = _each(jnp.exp, cum_last)
    return r_hat, o_loc, trans, d_ls, p_end


def _rwkv_kernel(r_ref, lw_ref, k_ref, v_ref, kk_ref, kka_ref, tri_ref, wg_ref, wu_ref, wd_ref,
                 o_ref, wg_bf_ref, wu_bf_ref, wd_bf_ref, state_ref, *, n_sub):
    @pl.when(pl.program_id(1) == 0)
    def _():
        state_ref[...] = jnp.zeros_like(state_ref)

    wg_bf_ref[...] = _bf(wg_ref[...])
    wu_bf_ref[...] = _bf(wu_ref[...])
    wd_bf_ref[...] = _bf(wd_ref[...])

    t_idx = lax.broadcasted_iota(jnp.int32, (CHUNK, GROUP), 0)
    s_idx = lax.broadcasted_iota(jnp.int32, (CHUNK, GROUP), 1) % CHUNK
    bd_mask = (lax.broadcasted_iota(jnp.int32, (GROUP, GROUP), 0) // CHUNK
               == lax.broadcasted_iota(jnp.int32, (GROUP, GROUP), 1) // CHUNK)
    n_group = D_RWKV // GROUP
    where = [(slice(c * CHUNK, (c + 1) * CHUNK), slice(g * GROUP, (g + 1) * GROUP))
             for c in range(n_sub) for g in range(n_group)]
    load = lambda ref: [ref[0, rows, lanes].astype(F32) for rows, lanes in where]
    r_hat, o_loc, trans, d_ls, p_end = _chunk_local(
        load(r_ref), load(lw_ref), load(k_ref), load(v_ref), load(kk_ref), load(kka_ref),
        tri_ref[...], t_idx, s_idx, bd_mask)

    state = [state_ref[g] for g in range(n_group)]
    for c in range(n_sub):
        chains = range(c * n_group, (c + 1) * n_group)
        out = [_mm_nt(r_hat[i], _block_diag(state[g], bd_mask)) + o_loc[i] for g, i in enumerate(chains)]
        for g, i in enumerate(chains):
            rows, lanes = where[i]
            o_ref[0, rows, lanes] = _bf(out[g])
        state = [state[g] * p_end[i] + _mm(state[g], trans[i]) + d_ls[i] for g, i in enumerate(chains)]
    for g in range(n_group):
        state_ref[g] = state[g]


def _rwkv_call(r, lw, k, v, kk, kka, tri, expert_weights, tb):
    bsz, seq, _ = r.shape
    n_t = seq // tb
    per_step = N_EXPERTS // (bsz * n_t)
    assert per_step * bsz * n_t == N_EXPERTS, "grid steps must divide the expert count"
    spec = pl.BlockSpec((1, tb, D_RWKV), lambda b, t: (b, t, 0))
    w_specs = [pl.BlockSpec((per_step,) + w.shape[1:], lambda b, t: (b * n_t + t, 0, 0)) for w in expert_weights]
    return pl.pallas_call(
        functools.partial(_rwkv_kernel, n_sub=tb // CHUNK),
        grid=(bsz, n_t),
        in_specs=[spec] * 6 + [pl.BlockSpec(tri.shape, lambda b, t: (0, 0))] + w_specs,
        out_specs=[spec] + w_specs,
        out_shape=[jax.ShapeDtypeStruct((bsz, seq, D_RWKV), BF16)]
        + [jax.ShapeDtypeStruct(w.shape, BF16) for w in expert_weights],
        scratch_shapes=[pltpu.VMEM((D_RWKV // GROUP, HEAD, GROUP), F32)],
        compiler_params=pltpu.CompilerParams(
            dimension_semantics=("arbitrary", "arbitrary"), vmem_limit_bytes=VMEM_LIMIT),
        name="rwkv_chunk",
    )(r, lw, k, v, kk, kka, tri, *expert_weights)


def _route(logits):
    lane_i = lax.broadcasted_iota(jnp.int32, logits.shape, 1)
    lane = lane_i.astype(F32)
    lane_group = (lane_i // EXPERTS_PER_GROUP).astype(F32)
    neg = -jnp.inf
    big = float(ROUTER_LANES)
    is_group = (lane_i >= N_EXPERTS) & (lane_i < N_EXPERTS + N_GROUPS)
    gl = jnp.where(is_group, logits, neg)
    ge = jnp.exp(gl - jnp.max(gl, axis=-1, keepdims=True))
    gprob = ge / jnp.sum(ge, axis=-1, keepdims=True)
    g_top = jnp.max(gprob, axis=-1, keepdims=True)
    g_idx = jnp.min(jnp.where(is_group & (gprob == g_top), lane - N_EXPERTS, big), axis=-1, keepdims=True)

    in_group = (lane_i < N_EXPERTS) & (lane_group == g_idx)
    el = jnp.where(in_group, logits, neg)
    top1 = jnp.max(el, axis=-1, keepdims=True)
    idx1 = jnp.min(jnp.where(in_group & (el == top1), lane, big), axis=-1, keepdims=True)
    el2 = jnp.where(lane == idx1, neg, el)
    top2 = jnp.max(el2, axis=-1, keepdims=True)
    idx2 = jnp.min(jnp.where(in_group & (lane != idx1) & (el2 == top2), lane, big), axis=-1, keepdims=True)
    e2 = jnp.exp(top2 - top1)
    denom = 1.0 + e2
    return idx1, idx2, g_top * (1.0 / denom), g_top * (e2 / denom)


def _pack_bf16_pairs(x):
    bits = lax.bitcast_convert_type(_bf(x).astype(F32), jnp.uint32)
    k = x.shape[1] // 2
    return lax.bitcast_convert_type((bits[:, :k] >> 16) | bits[:, k:], jnp.int32)


def _unpack_bf16_pairs(words):
    bits = lax.bitcast_convert_type(words, jnp.uint32)
    lo = lax.bitcast_convert_type(bits << 16, F32)
    hi = lax.bitcast_convert_type(bits & jnp.uint32(0xFFFF0000), F32)
    return _bf(jnp.concatenate([lo, hi], axis=1))


LANE_IDX1, LANE_IDX2, LANE_RANK1, LANE_RANK2, LANE_GATE1, LANE_GATE2 = range(6)


def _out_kernel(o_ref, bonus_ref, gate_ref, yconv_ref, x_ref, ones_ref, lnw_ref, lnb_ref,
                wout_ref, gffn_ref, rw_ref, rb_ref, tri_ref, h_ref, u_ref, route_ref, route_t_ref, count_ref,
                seen_ref, wbf_ref, *, n_split):
    @pl.when(pl.program_id(0) == 0)
    def _():
        seen_ref[...] = jnp.zeros_like(seen_ref)
        wbf_ref[...] = _bf(wout_ref[...])

    sub = o_ref.shape[0] // n_split
    parts = [slice(s * sub, (s + 1) * sub) for s in range(n_split)]
    read = lambda ref: [ref[p, :] for p in parts]
    ones_quad = ones_ref[...]
    inv_n = 1.0 / HEAD
    o = _each(lambda v: v.astype(F32), read(o_ref))
    mean = _each(lambda v: _head_sum(v, ones_quad) * inv_n, o)
    cen = _each(jnp.subtract, o, mean)
    var = _each(lambda c: _head_sum(c * c, ones_quad) * inv_n, cen)
    on = _each(lambda c, v: c * lax.rsqrt(v + LN_X_EPS) * lnw_ref[...] + lnb_ref[...], cen, var)
    y_rwkv = _each(lambda a, b, g: (a + b) * g, on, read(bonus_ref), read(gate_ref))
    mixed = _each(lambda yc, yr: (jnp.dot(yc, wbf_ref[:D_CONV, :], preferred_element_type=F32)
                                  + jnp.dot(_bf(yr), wbf_ref[D_CONV:, :], preferred_element_type=F32)),
                  read(yconv_ref), y_rwkv)
    h = _each(jnp.add, read(x_ref), mixed)
    u = _each(lambda v: _rms_norm(v, gffn_ref[...]), h)
    u_hi = _each(_bf, u)
    u_lo = _each(lambda a, b: _bf(a - b.astype(F32)), u, u_hi)
    by_hi = _each(lambda a: jnp.dot(a, rw_ref[...], preferred_element_type=F32), u_hi)
    by_lo = _each(lambda a: jnp.dot(a, rw_ref[:, :ROUTER_LANES], preferred_element_type=F32), u_lo)
    logits = _each(lambda a, b: a[:, :ROUTER_LANES] + a[:, ROUTER_LANES:] + b + rb_ref[...], by_hi, by_lo)
    routed = _each(_route, logits)

    lane_i = lax.broadcasted_iota(jnp.int32, (sub, ROUTER_LANES), 1)
    lane = lane_i.astype(F32)
    hit1 = _each(lambda rt: lane == rt[0], routed)
    hit2 = _each(lambda rt: lane == rt[1], routed)
    both = _each(lambda a, b: a.astype(F32) + b.astype(F32), hit1, hit2)
    inside = _each(lambda b: jnp.dot(tri_ref[...], _bf(b), preferred_element_type=F32), both)
    seen = seen_ref[...]
    for s, p in enumerate(parts):
        idx1, idx2, gate1, gate2 = routed[s]
        before = inside[s] + seen
        rank1 = jnp.sum(jnp.where(hit1[s], before, 0.0), axis=-1, keepdims=True)
        rank2 = jnp.sum(jnp.where(hit2[s], before, 0.0), axis=-1, keepdims=True)
        seen = seen + jnp.sum(both[s], axis=0, keepdims=True)
        route = jnp.zeros((sub, ROUTER_LANES), F32)
        for lane_id, col in ((LANE_IDX1, idx1), (LANE_IDX2, idx2), (LANE_RANK1, rank1),
                             (LANE_RANK2, rank2), (LANE_GATE1, gate1), (LANE_GATE2, gate2)):
            route = jnp.where(lane_i == lane_id, col, route)
        route_ref[p, :] = route
        route_t_ref[:, p] = route.T[:SUBLANES, :]
        h_ref[p, :] = _bf(h[s])
        u_ref[p, :] = _pack_bf16_pairs(u[s])
    seen_ref[...] = seen
    count_ref[...] = jnp.broadcast_to(seen, count_ref.shape)


def _out_call(o, bonus, gate, yconv, x, ones_bd, ln_w, ln_b, w_out, g_ffn, router_w, router_b, tm, n_split):
    n_tok = x.shape[0]
    row = lambda width: pl.BlockSpec((tm, width), lambda i: (i, 0))
    full = lambda arr: pl.BlockSpec(arr.shape, lambda i: (0,) * arr.ndim, pipeline_mode=pl.Buffered(1))
    sub = tm // n_split
    tri_strict = (jnp.arange(sub)[:, None] > jnp.arange(sub)[None, :]).astype(BF16)
    params = (ones_bd, ln_w, ln_b, w_out, g_ffn, router_w, router_b, tri_strict)
    return pl.pallas_call(
        functools.partial(_out_kernel, n_split=n_split),
        grid=(n_tok // tm,),
        in_specs=[row(D_RWKV)] * 4 + [row(D_MODEL)] + [full(p) for p in params],
        out_specs=[row(D_MODEL), row(D_MODEL // 2), row(ROUTER_LANES),
                   pl.BlockSpec((SUBLANES, tm), lambda i: (0, i)),
                   pl.BlockSpec((SUBLANES, ROUTER_LANES), lambda i: (0, 0))],
        out_shape=[jax.ShapeDtypeStruct((n_tok, D_MODEL), BF16),
                   jax.ShapeDtypeStruct((n_tok, D_MODEL // 2), jnp.int32),
                   jax.ShapeDtypeStruct((n_tok, ROUTER_LANES), F32),
                   jax.ShapeDtypeStruct((SUBLANES, n_tok), F32),
                   jax.ShapeDtypeStruct((SUBLANES, ROUTER_LANES), F32)],
        scratch_shapes=[pltpu.VMEM((1, ROUTER_LANES), F32), pltpu.VMEM(w_out.shape, BF16)],
        compiler_params=pltpu.CompilerParams(
            dimension_semantics=("arbitrary",), vmem_limit_bytes=VMEM_LIMIT),
        name="out_proj_route",
    )(o, bonus, gate, yconv, x, *params)


SC_CORES = 2
SC_SUBCORES = 16
SC_ROWS = 64


def _sc_mesh():
    return plsc.VectorSubcoreMesh(core_axis_name="c", subcore_axis_name="s",
                                  num_cores=SC_CORES, num_subcores=SC_SUBCORES)


def _sc_worker():
    return lax.axis_index("s") * SC_CORES + lax.axis_index("c")


def _sc_gather(table, idx):
    n_rows = idx.shape[0]
    width = table.shape[1]
    n_chunks = n_rows // (SC_CORES * SC_SUBCORES * SC_ROWS)

    def body(table_hbm, idx_hbm, out_hbm, idx_v, rows_v, gather_sem, write_sem):
        first = _sc_worker() * n_chunks
        pltpu.sync_copy(idx_hbm.at[pl.ds(first, n_chunks)], idx_v)
        gather = lambda j: pltpu.async_copy(table_hbm.at[idx_v.at[j]], rows_v.at[j % 2], gather_sem.at[j % 2])
        gathers = [gather(0)]
        writes = []
        for j in range(n_chunks):
            gathers[j].wait()
            if j + 1 < n_chunks:
                if j >= 1:
                    writes[j - 1].wait()
                gathers.append(gather(j + 1))
            dst = out_hbm.at[pl.ds(pl.multiple_of((first + j) * SC_ROWS, SC_ROWS), SC_ROWS)]
            writes.append(pltpu.async_copy(rows_v.at[j % 2], dst, write_sem.at[j % 2]))
        for j in range(max(n_chunks - 2, 0), n_chunks):
            writes[j].wait()

    return pl.kernel(
        body,
        out_type=jax.ShapeDtypeStruct((n_rows, width), table.dtype),
        mesh=_sc_mesh(),
        scratch_types=[pltpu.VMEM((n_chunks, SC_ROWS), jnp.int32), pltpu.VMEM((2, SC_ROWS, width), table.dtype),
                       pltpu.SemaphoreType.DMA((2,)), pltpu.SemaphoreType.DMA((2,))],
        name="sc_row_gather",
    )(table, idx.reshape(n_rows // SC_ROWS, SC_ROWS))


def _sc_scatter(rows, pos, n_out):
    n_rows, width = rows.shape
    n_slots = pos.shape[0] // n_rows
    slot_chunks = n_rows // SC_ROWS
    n_chunks = slot_chunks // (SC_CORES * SC_SUBCORES)

    def body(rows_hbm, pos_hbm, out_hbm, idx_v, rows_v, read_sem, scatter_sem):
        first = _sc_worker() * n_chunks
        for s in range(n_slots):
            pltpu.sync_copy(pos_hbm.at[pl.ds(s * slot_chunks + first, n_chunks)], idx_v.at[s])
        read = lambda j: pltpu.async_copy(
            rows_hbm.at[pl.ds(pl.multiple_of((first + j) * SC_ROWS, SC_ROWS), SC_ROWS)],
            rows_v.at[j % 2], read_sem.at[j % 2])
        reads = [read(0)]
        scatters = []
        for j in range(n_chunks):
            reads[j].wait()
            if j + 1 < n_chunks:
                if j >= 1:
                    for copy in scatters[j - 1]:
                        copy.wait()
                reads.append(read(j + 1))
            scatters.append([pltpu.async_copy(rows_v.at[j % 2], out_hbm.at[idx_v.at[s, j]], scatter_sem.at[j % 2])
                             for s in range(n_slots)])
        for j in range(max(n_chunks - 2, 0), n_chunks):
            for copy in scatters[j]:
                copy.wait()

    return pl.kernel(
        body,
        out_type=jax.ShapeDtypeStruct((n_out, width), rows.dtype),
        mesh=_sc_mesh(),
        scratch_types=[pltpu.VMEM((n_slots, n_chunks, SC_ROWS), jnp.int32),
                       pltpu.VMEM((2, SC_ROWS, width), rows.dtype),
                       pltpu.SemaphoreType.DMA((2,)), pltpu.SemaphoreType.DMA((2,))],
        name="sc_row_scatter",
    )(rows, pos.reshape(n_slots * slot_chunks, SC_ROWS))


def _expert_kernel(tile_expert_ref, n_valid_ref, x_ref, wg_ref, wu_ref, wd_ref, y_ref):
    del tile_expert_ref

    @pl.when(pl.program_id(0) < n_valid_ref[0])
    def _():
        x = _unpack_bf16_pairs(x_ref[...])
        gate = jnp.dot(x, wg_ref[0], preferred_element_type=F32)
        up = jnp.dot(x, wu_ref[0], preferred_element_type=F32)
        hid = gate * jax.nn.sigmoid(gate) * up
        y_ref[...] = _pack_bf16_pairs(jnp.dot(_bf(hid), wd_ref[0], preferred_element_type=F32))


def _expert_call(tile_expert, n_valid, x_sorted, w_gate, w_up, w_down, tm):
    n_rows = x_sorted.shape[0]
    rows = pl.BlockSpec((tm, D_MODEL // 2), lambda i, te, nv: (jnp.minimum(i, nv[0] - 1), 0))
    by_expert = lambda i, te, nv: (te[i], 0, 0)
    return pl.pallas_call(
        _expert_kernel,
        grid_spec=pltpu.PrefetchScalarGridSpec(
            num_scalar_prefetch=2,
            grid=(n_rows // tm,),
            in_specs=[rows,
                      pl.BlockSpec((1, D_MODEL, D_EXPERT), by_expert),
                      pl.BlockSpec((1, D_MODEL, D_EXPERT), by_expert),
                      pl.BlockSpec((1, D_EXPERT, D_MODEL), by_expert)],
            out_specs=rows),
        out_shape=jax.ShapeDtypeStruct((n_rows, D_MODEL // 2), jnp.int32),
        compiler_params=pltpu.CompilerParams(
            dimension_semantics=("arbitrary",), vmem_limit_bytes=VMEM_LIMIT),
        name="moe_experts",
    )(tile_expert, n_valid, x_sorted, w_gate, w_up, w_down)


def _final_kernel(h_ref, y1_ref, y2_ref, route_ref, gfin_ref, *rest):
    out_ref = rest[-1]
    route = route_ref[...]
    gate1 = route[:, LANE_GATE1:LANE_GATE1 + 1]
    gate2 = route[:, LANE_GATE2:LANE_GATE2 + 1]
    moe = gate1 * _unpack_bf16_pairs(y1_ref[...]).astype(F32) + gate2 * _unpack_bf16_pairs(y2_ref[...]).astype(F32)
    out_ref[...] = _rms_norm(h_ref[...] + moe, gfin_ref[...])


def _final_call(h, y_pairs, route, g_final, earlier, first_token, tm):
    n_tok = h.shape[0]
    n_blocks = y_pairs.shape[0] // 2 // tm
    assert n_blocks * tm * 2 == y_pairs.shape[0] and first_token % tm == 0, "chunks must be whole row blocks"
    first = first_token // tm
    in_specs = [pl.BlockSpec((tm, D_MODEL), lambda i: (first + i, 0)),
                pl.BlockSpec((tm, D_MODEL // 2), lambda i: (i, 0)),
                pl.BlockSpec((tm, D_MODEL // 2), lambda i: (i + n_blocks, 0)),
                pl.BlockSpec((tm, ROUTER_LANES), lambda i: (first + i, 0)),
                pl.BlockSpec((1, D_MODEL), lambda i: (0, 0))]
    args = [h, y_pairs, y_pairs, route, g_final]
    aliases = {}
    if earlier is not None:
        in_specs.append(pl.BlockSpec(memory_space=pl.ANY))
        args.append(earlier)
        aliases = {len(args) - 1: 0}
    return pl.pallas_call(
        _final_kernel,
        grid=(n_blocks,),
        in_specs=in_specs,
        out_specs=pl.BlockSpec((tm, D_MODEL), lambda i: (first + i, 0)),
        out_shape=jax.ShapeDtypeStruct((n_tok, D_MODEL), F32),
        input_output_aliases=aliases,
        compiler_params=pltpu.CompilerParams(
            dimension_semantics=("arbitrary",), vmem_limit_bytes=VMEM_LIMIT),
        name="moe_combine_norm",
    )(*args)


def _dispatch_plan(route_t, counts, tm):
    n_tok = route_t.shape[1]
    n_tiles = (2 * n_tok) // tm + N_EXPERTS
    counts = counts[0, :N_EXPERTS].astype(jnp.int32)
    tiles_per = (counts + tm - 1) // tm
    tile_end = jnp.cumsum(tiles_per)
    row_start = (tile_end - tiles_per) * tm
    experts = jnp.arange(N_EXPERTS, dtype=jnp.int32)

    def position(idx_lane, rank_lane):
        idx = route_t[idx_lane].astype(jnp.int32)
        start = jnp.sum(jnp.where(idx[None, :] == experts[:, None], row_start[:, None], 0), axis=0)
        return start + route_t[rank_lane].astype(jnp.int32)

    pos = jnp.concatenate([position(LANE_IDX1, LANE_RANK1), position(LANE_IDX2, LANE_RANK2)])
    n_valid = tile_end[-1:]
    tile = jnp.minimum(jnp.arange(n_tiles, dtype=jnp.int32), n_valid - 1)
    tile_expert = jnp.sum((tile_end[None, :] <= tile[:, None]).astype(jnp.int32), axis=1)
    return pos, n_tiles, tile_expert, n_valid


def _block(x, norm_mix_g, w_in, rwkv_mu, conv_w, decay_up, decay_base, aaa_up, aaa_base, gate_up,
           k_k, k_a, r_k, ln_x_w, ln_x_b, w_out, norm_ffn_g, router_group_w, router_group_b,
           router_expert_w, router_expert_b, expert_w_gate, expert_w_up, expert_w_down, norm_final_g,
           *, tm_in, tb_rwkv, tm_out, tm_expert, tm_final):
    bsz, seq, d_model = x.shape
    n_tok = bsz * seq
    row = lambda p: p.reshape(1, -1)
    half = LORA_WA // 2
    zeros = jnp.zeros((half, D_RWKV), F32)
    dup = _bf(jnp.concatenate([decay_up, zeros], axis=0))
    aup = _bf(jnp.concatenate([zeros, aaa_up], axis=0))
    head_of = jnp.arange(QUAD) // HEAD
    ones_quad = (head_of[:, None] == head_of[None, :]).astype(BF16)
    tri = (jnp.arange(CHUNK)[:, None] >= jnp.arange(CHUNK)[None, :]).astype(BF16)

    yconv, r, lw, k2, v, kk, kka, gate, bonus = _in_call(
        x, row(norm_mix_g), w_in, row(rwkv_mu), conv_w, dup, row(decay_base), aup, row(aaa_base),
        _bf(gate_up), row(k_k), row(k_a), row(r_k), ones_quad, tm_in)
    o, w_gate_bf, w_up_bf, w_down_bf = _rwkv_call(r, lw, k2, v, kk, kka, tri,
                                                  (expert_w_gate, expert_w_up, expert_w_down), tb_rwkv)

    pad = ROUTER_LANES - N_EXPERTS - N_GROUPS
    router_w = jnp.concatenate([router_expert_w, router_group_w, jnp.zeros((d_model, pad), F32)], axis=1)
    router_hi = _bf(router_w)
    router_split = jnp.concatenate([router_hi, _bf(router_w - router_hi.astype(F32))], axis=1)
    router_b = jnp.concatenate([router_expert_b, router_group_b, jnp.zeros((pad,), F32)]).reshape(1, -1)
    flat = lambda t: t.reshape(n_tok, t.shape[-1])
    h, u_pairs, route, route_t, counts = _out_call(
        flat(o), flat(bonus), flat(gate), flat(yconv), flat(x), ones_quad,
        row(ln_x_w), row(ln_x_b), w_out, row(norm_ffn_g), router_split, router_b, tm_out, OUT_SPLIT)

    pos, n_tiles, tile_expert, n_valid = _dispatch_plan(route_t, counts, tm_expert)
    x_sorted = _sc_scatter(u_pairs, pos, n_tiles * tm_expert)
    y_sorted = _expert_call(tile_expert, n_valid, x_sorted, w_gate_bf, w_up_bf, w_down_bf, tm_expert)
    out, lo = None, 0
    for fraction in COMBINE_SPLIT:
        size = n_tok // fraction
        pos_c = jnp.concatenate([pos[lo:lo + size], pos[n_tok + lo:n_tok + lo + size]])
        out = _final_call(h, _sc_gather(y_sorted, pos_c), route, row(norm_final_g), out, lo, tm_final)
        lo += size
    assert lo == n_tok, "COMBINE_SPLIT must cover all tokens"
    return out.reshape(bsz, seq, d_model)


def kernel(x, norm_mix_g, w_in, rwkv_mu, conv_w, decay_up, decay_base, aaa_up, aaa_base, gate_up, k_k, k_a, r_k, ln_x_w, ln_x_b, w_out, norm_ffn_g, router_group_w, router_group_b, router_expert_w, router_expert_b, expert_w_gate, expert_w_up, expert_w_down, norm_final_g):
    return _block(x, norm_mix_g[0], w_in[0], rwkv_mu[0], conv_w[0], decay_up[0], decay_base[0],
                  aaa_up[0], aaa_base[0], gate_up[0], k_k[0], k_a[0], r_k[0].reshape(-1), ln_x_w[0],
                  ln_x_b[0], w_out[0], norm_ffn_g[0], router_group_w[0], router_group_b[0],
                  router_expert_w[0], router_expert_b[0], expert_w_gate[0], expert_w_up[0],
                  expert_w_down[0], norm_final_g,
                  tm_in=512, tb_rwkv=512, tm_out=512, tm_expert=512, tm_final=1024)
```

```python
import functools

import jax
import jax.numpy as jnp
from jax import lax
from jax.experimental import pallas as pl
from jax.experimental.pallas import tpu as pltpu
from jax.experimental.pallas import tpu_sc as plsc

F32 = jnp.float32
BF16 = jnp.bfloat16

D_MODEL = 1024
D_CONV = 512
CONV_WIDTH = 3
N_HEADS = 8
HEAD = 64
D_RWKV = N_HEADS * HEAD
LORA_WA = 128
GATE_LORA = 128
D_RWKV_PROJ = 3 * D_RWKV + LORA_WA + GATE_LORA
D_IN = 3 * D_CONV + D_RWKV_PROJ
N_GROUPS = 4
EXPERTS_PER_GROUP = 8
N_EXPERTS = N_GROUPS * EXPERTS_PER_GROUP
D_EXPERT = D_MODEL // 4
RMS_EPS = 1e-6
LN_X_EPS = 64e-5
L2_EPS = 1e-12

SUBLANES = 8
CHUNK = 64
QUAD = 4 * HEAD
GROUP = 2 * HEAD
INV_BASE = 8
COMPACT_FROM = 64
OUT_SPLIT = 8
COMBINE_SPLIT = (4, 4, 4, 4)
ROUTER_LANES = 128

VMEM_LIMIT = 56 * 1024 * 1024


def _bf(x):
    return x.astype(BF16)


def _mm(a, b):
    return jnp.dot(_bf(a), _bf(b), preferred_element_type=F32)


def _mm_nt(a, b):
    return lax.dot_general(_bf(a), _bf(b), (((1,), (1,)), ((), ())), preferred_element_type=F32)


def _mm_exact_lhs(lhs_bf16, x, passes):
    acc = None
    rem = x
    for _ in range(passes):
        piece = _bf(rem)
        part = jnp.dot(lhs_bf16, piece, preferred_element_type=F32)
        acc = part if acc is None else acc + part
        rem = rem - piece.astype(F32)
    return acc


def _head_sum(x, ones_quad):
    xb = _bf(x)
    return jnp.concatenate(
        [jnp.dot(xb[:, q * QUAD:(q + 1) * QUAD], ones_quad, preferred_element_type=F32)
         for q in range(x.shape[1] // QUAD)], axis=1)


def _rms_norm(x, g):
    return x * lax.rsqrt(jnp.mean(x * x, axis=-1, keepdims=True) + RMS_EPS) * g


def _shift_rows(cur, prev_rows, k):
    rolled = pltpu.roll(cur, k, 0)
    prev_rolled = pltpu.roll(prev_rows, k, 0)
    n = cur.shape[0]
    head = jnp.concatenate([prev_rolled, rolled[SUBLANES:]], axis=0) if n > SUBLANES else prev_rolled
    row = lax.broadcasted_iota(jnp.int32, cur.shape, 0)
    return jnp.where(row < k, head, rolled)


def _in_kernel(x_ref, g_ref, w_ref, mu_ref, convw_ref, dup_ref, dbase_ref, aup_ref, abase_ref,
               gup_ref, kk_ref, ka_ref, rk_ref, ones_ref,
               yconv_ref, r_ref, lw_ref, k_ref, v_ref, kkn_ref, kka_ref, gate_ref, bonus_ref,
               carry_ref, wbf_ref):
    @pl.when((pl.program_id(0) == 0) & (pl.program_id(1) == 0))
    def _():
        wbf_ref[...] = _bf(w_ref[...])

    @pl.when(pl.program_id(1) == 0)
    def _():
        carry_ref[...] = jnp.zeros_like(carry_ref)

    u = _bf(_rms_norm(x_ref[0], g_ref[...]))
    n_conv = 3 * D_CONV
    rk0 = n_conv
    bounds = dict(conv=(0, n_conv), lora=(rk0 + 3 * D_RWKV, D_IN), k=(rk0 + D_RWKV, rk0 + 2 * D_RWKV),
                  r=(rk0, rk0 + D_RWKV), v=(rk0 + 2 * D_RWKV, rk0 + 3 * D_RWKV))
    z = {name: jnp.dot(u, wbf_ref[:, lo:hi], preferred_element_type=F32) for name, (lo, hi) in bounds.items()}

    def lerp(name):
        lo, hi = bounds[name]
        cur = z[name]
        prev = carry_ref[:, D_CONV + lo - n_conv:D_CONV + hi - n_conv]
        mixed = cur + (_shift_rows(cur, prev, 1) - cur) * mu_ref[:, lo - n_conv:hi - n_conv]
        carry_ref[:, D_CONV + lo - n_conv:D_CONV + hi - n_conv] = cur[-SUBLANES:]
        return mixed

    b_gate = z["conv"][:, :D_CONV]
    ch = z["conv"][:, D_CONV:2 * D_CONV] * z["conv"][:, 2 * D_CONV:]
    prev_ch = carry_ref[:, :D_CONV]
    conv = convw_ref[CONV_WIDTH - 1:CONV_WIDTH, :] * ch
    for delay in range(1, CONV_WIDTH):
        tap = CONV_WIDTH - 1 - delay
        conv = conv + convw_ref[tap:tap + 1, :] * _shift_rows(ch, prev_ch, delay)
    yconv_ref[0] = _bf(b_gate * conv)
    carry_ref[:, :D_CONV] = ch[-SUBLANES:]

    lora = lerp("lora")
    wa_lo = lora[:, :LORA_WA]
    g_lo = lora[:, LORA_WA:]
    dec_in = -(dbase_ref[...] + jnp.dot(_bf(jnp.tanh(wa_lo)), dup_ref[...], preferred_element_type=F32))
    softplus = jnp.maximum(dec_in, 0.0) + jnp.log(1.0 + jnp.exp(-jnp.abs(dec_in)))
    w = -softplus - 0.5
    lw_ref[0] = -jnp.exp(w)
    a = jax.nn.sigmoid(abase_ref[...] + jnp.dot(_bf(wa_lo), aup_ref[...], preferred_element_type=F32))
    gate_ref[0] = _bf(jnp.dot(_bf(jax.nn.sigmoid(g_lo)), gup_ref[...], preferred_element_type=F32))

    ones_quad = ones_ref[...]
    k = lerp("k")
    kk = k * kk_ref[...]
    norm = jnp.sqrt(_head_sum(kk * kk, ones_quad))
    kk = kk / jnp.maximum(norm, L2_EPS)
    k2 = k * (1.0 + (a - 1.0) * ka_ref[...])
    k_ref[0] = _bf(k2)
    kkn_ref[0] = _bf(kk)
    kka_ref[0] = _bf(kk * a)
    r = lerp("r")
    r_ref[0] = _bf(r)
    v = lerp("v")
    v_ref[0] = _bf(v)
    bonus_ref[0] = _bf(_head_sum(r * k2 * rk_ref[...], ones_quad) * v)


def _in_call(x, g, w_in, mu, conv_w, dup, dbase, aup, abase, gup, k_k, k_a, r_k, ones_quad, tm):
    bsz, seq, _ = x.shape
    full = lambda arr: pl.BlockSpec(arr.shape, lambda b, t: (0,) * arr.ndim, pipeline_mode=pl.Buffered(1))
    out_spec = pl.BlockSpec((1, tm, D_RWKV), lambda b, t: (b, t, 0))
    out_dtypes = (BF16, BF16, F32, BF16, BF16, BF16, BF16, BF16, BF16)
    params = (g, w_in, mu, conv_w, dup, dbase, aup, abase, gup, k_k, k_a, r_k, ones_quad)
    return pl.pallas_call(
        _in_kernel,
        grid=(bsz, seq // tm),
        in_specs=[pl.BlockSpec((1, tm, D_MODEL), lambda b, t: (b, t, 0))] + [full(p) for p in params],
        out_specs=[out_spec] * 9,
        out_shape=[jax.ShapeDtypeStruct((bsz, seq, D_RWKV), dt) for dt in out_dtypes],
        scratch_shapes=[pltpu.VMEM((SUBLANES, D_CONV + D_RWKV_PROJ), F32), pltpu.VMEM(w_in.shape, BF16)],
        compiler_params=pltpu.CompilerParams(
            dimension_semantics=("arbitrary", "arbitrary"), vmem_limit_bytes=VMEM_LIMIT),
        name="in_proj",
    )(x, *params)


def _block_diag(y, bd_mask):
    return jnp.where(bd_mask, jnp.concatenate([y] * (GROUP // CHUNK), axis=0), 0.0)


def _each(fn, *lists):
    return [fn(*args) for args in zip(*lists)]


def _unit_lower_inverse(a_strict, t_idx, s_idx, bd):
    bdmm = lambda xs, ys: _each(lambda x, y: _mm(x, bd(y)), xs, ys)
    eye = (t_idx == s_idx).astype(F32)
    same8 = (t_idx // INV_BASE) == (s_idx // INV_BASE)
    a8 = _each(lambda a: jnp.where(same8, a, 0.0), a_strict)
    a8_2 = bdmm(a8, a8)
    a8_34 = bdmm(_each(lambda a, b: jnp.concatenate([a, b], axis=0), a8, a8_2), a8_2)
    inv = _each(lambda a, b, c: eye + a + b + c[:CHUNK], a8, a8_2, a8_34)
    inv = _each(jnp.add, inv, bdmm(inv, _each(lambda c: c[CHUNK:], a8_34)))
    size = 2 * INV_BASE
    while size < COMPACT_FROM:
        off = ((t_idx // size) == (s_idx // size)) & ((t_idx // (size // 2)) != (s_idx // (size // 2)))
        cross = bdmm(_each(lambda a: jnp.where(off, a, 0.0), a_strict), inv)
        inv = _each(jnp.add, inv, bdmm(inv, cross))
        size *= 2
    while size <= CHUNK:
        inv = _compact_level(a_strict, inv, size)
        size *= 2
    return inv


def _compact_level(a_strict, inv, size):
    half = size // 2
    heads = GROUP // CHUNK
    blocks = CHUNK // size
    lane_starts = [h * CHUNK + b * size for h in range(heads) for b in range(blocks)]
    row_starts = [b * size for b in range(blocks)]
    rows_first = lambda x: jnp.concatenate([x[r:r + half] for r in row_starts], axis=0)
    rows_second = lambda x: jnp.concatenate([x[r + half:r + size] for r in row_starts], axis=0)
    lanes_first = lambda x: jnp.concatenate([x[:, c:c + half] for c in lane_starts], axis=1)
    lanes_second = lambda x: jnp.concatenate([x[:, c + half:c + size] for c in lane_starts], axis=1)
    n_rows, n_lanes = CHUNK // 2, GROUP // 2
    own = (lax.broadcasted_iota(jnp.int32, (n_rows, n_lanes), 0) // half
           == (lax.broadcasted_iota(jnp.int32, (n_rows, n_lanes), 1) // half) % blocks)
    diag = (lax.broadcasted_iota(jnp.int32, (n_lanes, n_lanes), 0) // half
            == lax.broadcasted_iota(jnp.int32, (n_lanes, n_lanes), 1) // half)
    bd_half = lambda y: jnp.where(diag, jnp.concatenate([y] * heads, axis=0), 0.0)
    a21 = _each(lambda a: jnp.where(own, lanes_first(rows_second(a)), 0.0), a_strict)
    t11 = _each(lambda t: lanes_first(rows_first(t)), inv)
    t22 = _each(lambda t: lanes_second(rows_second(t)), inv)
    a21_t11 = _each(lambda x, y: _mm(x, bd_half(y)), a21, t11)
    new = _each(lambda x, y: _mm(x, bd_half(y)), t22, a21_t11)

    def placed(x):
        zero_lanes = jnp.zeros((n_rows, half), F32)
        wide = jnp.concatenate(sum(([x[:, i * half:(i + 1) * half], zero_lanes] for i in range(heads * blocks)), []),
                               axis=1)
        zero_rows = jnp.zeros((half, GROUP), F32)
        return jnp.concatenate(sum(([zero_rows, wide[b * half:(b + 1) * half]] for b in range(blocks)), []), axis=0)

    return _each(lambda t, x: t + placed(x), inv, new)


def _chunk_local(r, lw, k, v, kk, kka, tri, t_idx, s_idx, bd_mask):
    bd = lambda y: _block_diag(y, bd_mask)
    bdmm = lambda xs, ys: _each(lambda x, y: _mm(x, bd(y)), xs, ys)
    bdmm2 = lambda xs, ys, zs: _each(lambda x, y, z: _mm(x, jnp.concatenate([bd(y), bd(z)], axis=1)), xs, ys, zs)
    left = lambda xs: _each(lambda x: x[:, :GROUP], xs)
    right = lambda xs: _each(lambda x: x[:, GROUP:], xs)
    top = lambda xs: _each(lambda x: x[:CHUNK], xs)
    bottom = lambda xs: _each(lambda x: x[CHUNK:], xs)

    cum = _each(lambda x: _mm_exact_lhs(tri, x, 3), lw)
    cum_last = _each(lambda c: c[CHUNK - 1:CHUNK, :], cum)
    p_incl = _each(jnp.exp, cum)
    p_excl = _each(lambda c, x: jnp.exp(c - x), cum, lw)
    p_inv = _each(lambda c: jnp.exp(-c), cum)
    to_end = _each(lambda cl, c: jnp.exp(cl - c), cum_last, cum)
    a_t = _each(lambda x, p: -x * p, kk, p_excl)
    r_t = _each(jnp.multiply, r, p_incl)
    b_t = _each(jnp.multiply, kka, p_inv)
    k_t = _each(jnp.multiply, k, p_inv)
    b_end = _each(jnp.multiply, kka, to_end)
    k_end = _each(jnp.multiply, k, to_end)

    ar = _each(lambda a, b: jnp.concatenate([a, b], axis=0), a_t, r_t)
    row2 = lax.broadcasted_iota(jnp.int32, (2 * CHUNK, 2 * GROUP), 0)
    col2 = lax.broadcasted_iota(jnp.int32, (2 * CHUNK, 2 * GROUP), 1) % CHUNK
    causal = col2 < (row2 % CHUNK) + (row2 // CHUNK)
    scores = _each(lambda x, y, z: jnp.where(causal, _mm_nt(x, jnp.concatenate([bd(y), bd(z)], axis=0)), 0.0),
                   ar, b_t, k_t)
    a_ab = top(left(scores))
    a_rb = bottom(left(scores))

    inv = _unit_lower_inverse(a_ab, t_idx, s_idx, bd)
    kv = bdmm(right(scores), v)
    wu = bdmm2(inv, a_t, top(kv))
    ro = bdmm2(a_rb, left(wu), right(wu))
    r_hat = _each(jnp.add, r_t, left(ro))
    o_loc = _each(jnp.add, right(ro), bottom(kv))

    wu_b = _each(lambda x, b: _mm(x.T, b), wu, b_end)
    trans = _each(lambda x: jnp.where(bd_mask, x[:GROUP], 0.0), wu_b)
    d_bd = _each(lambda x, y, z: jnp.where(bd_mask, x[GROUP:] + _mm(y.T, z), 0.0), wu_b, v, k_end)
    d_ls = _each(lambda d: sum(d[h * CHUNK:(h + 1) * CHUNK] for h in range(1, GROUP // CHUNK)) + d[:CHUNK], d_bd)
    p_end = _each(jnp.exp, cum_last)
    return r_hat, o_loc, trans, d_ls, p_end


def _rwkv_kernel(r_ref, lw_ref, k_ref, v_ref, kk_ref, kka_ref, tri_ref, wg_ref, wu_ref, wd_ref,
                 o_ref, wg_bf_ref, wu_bf_ref, wd_bf_ref, state_ref, *, n_sub):
    @pl.when(pl.program_id(1) == 0)
    def _():
        state_ref[...] = jnp.zeros_like(state_ref)

    wg_bf_ref[...] = _bf(wg_ref[...])
    wu_bf_ref[...] = _bf(wu_ref[...])
    wd_bf_ref[...] = _bf(wd_ref[...])

    t_idx = lax.broadcasted_iota(jnp.int32, (CHUNK, GROUP), 0)
    s_idx = lax.broadcasted_iota(jnp.int32, (CHUNK, GROUP), 1) % CHUNK
    bd_mask = (lax.broadcasted_iota(jnp.int32, (GROUP, GROUP), 0) // CHUNK
               == lax.broadcasted_iota(jnp.int32, (GROUP, GROUP), 1) // CHUNK)
    n_group = D_RWKV // GROUP
    where = [(slice(c * CHUNK, (c + 1) * CHUNK), slice(g * GROUP, (g + 1) * GROUP))
             for c in range(n_sub) for g in range(n_group)]
    load = lambda ref: [ref[0, rows, lanes].astype(F32) for rows, lanes in where]
    r_hat, o_loc, trans, d_ls, p_end = _chunk_local(
        load(r_ref), load(lw_ref), load(k_ref), load(v_ref), load(kk_ref), load(kka_ref),
        tri_ref[...], t_idx, s_idx, bd_mask)

    state = [state_ref[g] for g in range(n_group)]
    for c in range(n_sub):
        chains = range(c * n_group, (c + 1) * n_group)
        out = [_mm_nt(r_hat[i], _block_diag(state[g], bd_mask)) + o_loc[i] for g, i in enumerate(chains)]
        for g, i in enumerate(chains):
            rows, lanes = where[i]
            o_ref[0, rows, lanes] = _bf(out[g])
        state = [state[g] * p_end[i] + _mm(state[g], trans[i]) + d_ls[i] for g, i in enumerate(chains)]
    for g in range(n_group):
        state_ref[g] = state[g]


def _rwkv_call(r, lw, k, v, kk, kka, tri, expert_weights, tb):
    bsz, seq, _ = r.shape
    n_t = seq // tb
    per_step = N_EXPERTS // (bsz * n_t)
    assert per_step * bsz * n_t == N_EXPERTS, "grid steps must divide the expert count"
    spec = pl.BlockSpec((1, tb, D_RWKV), lambda b, t: (b, t, 0))
    w_specs = [pl.BlockSpec((per_step,) + w.shape[1:], lambda b, t: (b * n_t + t, 0, 0)) for w in expert_weights]
    return pl.pallas_call(
        functools.partial(_rwkv_kernel, n_sub=tb // CHUNK),
        grid=(bsz, n_t),
        in_specs=[spec] * 6 + [pl.BlockSpec(tri.shape, lambda b, t: (0, 0))] + w_specs,
        out_specs=[spec] + w_specs,
        out_shape=[jax.ShapeDtypeStruct((bsz, seq, D_RWKV), BF16)]
        + [jax.ShapeDtypeStruct(w.shape, BF16) for w in expert_weights],
        scratch_shapes=[pltpu.VMEM((D_RWKV // GROUP, HEAD, GROUP), F32)],
        compiler_params=pltpu.CompilerParams(
            dimension_semantics=("arbitrary", "arbitrary"), vmem_limit_bytes=VMEM_LIMIT),
        name="rwkv_chunk",
    )(r, lw, k, v, kk, kka, tri, *expert_weights)


def _route(logits):
    lane_i = lax.broadcasted_iota(jnp.int32, logits.shape, 1)
    lane = lane_i.astype(F32)
    lane_group = (lane_i // EXPERTS_PER_GROUP).astype(F32)
    neg = -jnp.inf
    big = float(ROUTER_LANES)
    is_group = (lane_i >= N_EXPERTS) & (lane_i < N_EXPERTS + N_GROUPS)
    gl = jnp.where(is_group, logits, neg)
    ge = jnp.exp(gl - jnp.max(gl, axis=-1, keepdims=True))
    gprob = ge / jnp.sum(ge, axis=-1, keepdims=True)
    g_top = jnp.max(gprob, axis=-1, keepdims=True)
    g_idx = jnp.min(jnp.where(is_group & (gprob == g_top), lane - N_EXPERTS, big), axis=-1, keepdims=True)

    in_group = (lane_i < N_EXPERTS) & (lane_group == g_idx)
    el = jnp.where(in_group, logits, neg)
    top1 = jnp.max(el, axis=-1, keepdims=True)
    idx1 = jnp.min(jnp.where(in_group & (el == top1), lane, big), axis=-1, keepdims=True)
    el2 = jnp.where(lane == idx1, neg, el)
    top2 = jnp.max(el2, axis=-1, keepdims=True)
    idx2 = jnp.min(jnp.where(in_group & (lane != idx1) & (el2 == top2), lane, big), axis=-1, keepdims=True)
    e2 = jnp.exp(top2 - top1)
    denom = 1.0 + e2
    return idx1, idx2, g_top * (1.0 / denom), g_top * (e2 / denom)


def _pack_bf16_pairs(x):
    bits = lax.bitcast_convert_type(_bf(x).astype(F32), jnp.uint32)
    k = x.shape[1] // 2
    return lax.bitcast_convert_type((bits[:, :k] >> 16) | bits[:, k:], jnp.int32)


def _unpack_bf16_pairs(words):
    bits = lax.bitcast_convert_type(words, jnp.uint32)
    lo = lax.bitcast_convert_type(bits << 16, F32)
    hi = lax.bitcast_convert_type(bits & jnp.uint32(0xFFFF0000), F32)
    return _bf(jnp.concatenate([lo, hi], axis=1))


LANE_IDX1, LANE_IDX2, LANE_RANK1, LANE_RANK2, LANE_GATE1, LANE_GATE2 = range(6)


def _out_kernel(o_ref, bonus_ref, gate_ref, yconv_ref, x_ref, ones_ref, lnw_ref, lnb_ref,
                wout_ref, gffn_ref, rw_ref, rb_ref, tri_ref, h_ref, u_ref, route_ref, route_t_ref, count_ref,
                seen_ref, wbf_ref, *, n_split):
    @pl.when(pl.program_id(0) == 0)
    def _():
        seen_ref[...] = jnp.zeros_like(seen_ref)
        wbf_ref[...] = _bf(wout_ref[...])

    sub = o_ref.shape[0] // n_split
    parts = [slice(s * sub, (s + 1) * sub) for s in range(n_split)]
    read = lambda ref: [ref[p, :] for p in parts]
    ones_quad = ones_ref[...]
    inv_n = 1.0 / HEAD
    o = _each(lambda v: v.astype(F32), read(o_ref))
    mean = _each(lambda v: _head_sum(v, ones_quad) * inv_n, o)
    cen = _each(jnp.subtract, o, mean)
    var = _each(lambda c: _head_sum(c * c, ones_quad) * inv_n, cen)
    on = _each(lambda c, v: c * lax.rsqrt(v + LN_X_EPS) * lnw_ref[...] + lnb_ref[...], cen, var)
    y_rwkv = _each(lambda a, b, g: (a + b) * g, on, read(bonus_ref), read(gate_ref))
    mixed = _each(lambda yc, yr: (jnp.dot(yc, wbf_ref[:D_CONV, :], preferred_element_type=F32)
                                  + jnp.dot(_bf(yr), wbf_ref[D_CONV:, :], preferred_element_type=F32)),
                  read(yconv_ref), y_rwkv)
    h = _each(jnp.add, read(x_ref), mixed)
    u = _each(lambda v: _rms_norm(v, gffn_ref[...]), h)
    u_hi = _each(_bf, u)
    u_lo = _each(lambda a, b: _bf(a - b.astype(F32)), u, u_hi)
    by_hi = _each(lambda a: jnp.dot(a, rw_ref[...], preferred_element_type=F32), u_hi)
    by_lo = _each(lambda a: jnp.dot(a, rw_ref[:, :ROUTER_LANES], preferred_element_type=F32), u_lo)
    logits = _each(lambda a, b: a[:, :ROUTER_LANES] + a[:, ROUTER_LANES:] + b + rb_ref[...], by_hi, by_lo)
    routed = _each(_route, logits)

    lane_i = lax.broadcasted_iota(jnp.int32, (sub, ROUTER_LANES), 1)
    lane = lane_i.astype(F32)
    hit1 = _each(lambda rt: lane == rt[0], routed)
    hit2 = _each(lambda rt: lane == rt[1], routed)
    both = _each(lambda a, b: a.astype(F32) + b.astype(F32), hit1, hit2)
    inside = _each(lambda b: jnp.dot(tri_ref[...], _bf(b), preferred_element_type=F32), both)
    seen = seen_ref[...]
    for s, p in enumerate(parts):
        idx1, idx2, gate1, gate2 = routed[s]
        before = inside[s] + seen
        rank1 = jnp.sum(jnp.where(hit1[s], before, 0.0), axis=-1, keepdims=True)
        rank2 = jnp.sum(jnp.where(hit2[s], before, 0.0), axis=-1, keepdims=True)
        seen = seen + jnp.sum(both[s], axis=0, keepdims=True)
        route = jnp.zeros((sub, ROUTER_LANES), F32)
        for lane_id, col in ((LANE_IDX1, idx1), (LANE_IDX2, idx2), (LANE_RANK1, rank1),
                             (LANE_RANK2, rank2), (LANE_GATE1, gate1), (LANE_GATE2, gate2)):
            route = jnp.where(lane_i == lane_id, col, route)
        route_ref[p, :] = route
        route_t_ref[:, p] = route.T[:SUBLANES, :]
        h_ref[p, :] = _bf(h[s])
        u_ref[p, :] = _pack_bf16_pairs(u[s])
    seen_ref[...] = seen
    count_ref[...] = jnp.broadcast_to(seen, count_ref.shape)


def _out_call(o, bonus, gate, yconv, x, ones_bd, ln_w, ln_b, w_out, g_ffn, router_w, router_b, tm, n_split):
    n_tok = x.shape[0]
    row = lambda width: pl.BlockSpec((tm, width), lambda i: (i, 0))
    full = lambda arr: pl.BlockSpec(arr.shape, lambda i: (0,) * arr.ndim, pipeline_mode=pl.Buffered(1))
    sub = tm // n_split
    tri_strict = (jnp.arange(sub)[:, None] > jnp.arange(sub)[None, :]).astype(BF16)
    params = (ones_bd, ln_w, ln_b, w_out, g_ffn, router_w, router_b, tri_strict)
    return pl.pallas_call(
        functools.partial(_out_kernel, n_split=n_split),
        grid=(n_tok // tm,),
        in_specs=[row(D_RWKV)] * 4 + [row(D_MODEL)] + [full(p) for p in params],
        out_specs=[row(D_MODEL), row(D_MODEL // 2), row(ROUTER_LANES),
                   pl.BlockSpec((SUBLANES, tm), lambda i: (0, i)),
                   pl.BlockSpec((SUBLANES, ROUTER_LANES), lambda i: (0, 0))],
        out_shape=[jax.ShapeDtypeStruct((n_tok, D_MODEL), BF16),
                   jax.ShapeDtypeStruct((n_tok, D_MODEL // 2), jnp.int32),
                   jax.ShapeDtypeStruct((n_tok, ROUTER_LANES), F32),
                   jax.ShapeDtypeStruct((SUBLANES, n_tok), F32),
                   jax.ShapeDtypeStruct((SUBLANES, ROUTER_LANES), F32)],
        scratch_shapes=[pltpu.VMEM((1, ROUTER_LANES), F32), pltpu.VMEM(w_out.shape, BF16)],
        compiler_params=pltpu.CompilerParams(
            dimension_semantics=("arbitrary",), vmem_limit_bytes=VMEM_LIMIT),
        name="out_proj_route",
    )(o, bonus, gate, yconv, x, *params)


SC_CORES = 2
SC_SUBCORES = 16
SC_ROWS = 64


def _sc_mesh():
    return plsc.VectorSubcoreMesh(core_axis_name="c", subcore_axis_name="s",
                                  num_cores=SC_CORES, num_subcores=SC_SUBCORES)


def _sc_worker():
    return lax.axis_index("s") * SC_CORES + lax.axis_index("c")


def _sc_gather(table, idx):
    n_rows = idx.shape[0]
    width = table.shape[1]
    n_chunks = n_rows // (SC_CORES * SC_SUBCORES * SC_ROWS)

    def body(table_hbm, idx_hbm, out_hbm, idx_v, rows_v, gather_sem, write_sem):
        first = _sc_worker() * n_chunks
        pltpu.sync_copy(idx_hbm.at[pl.ds(first, n_chunks)], idx_v)
        gather = lambda j: pltpu.async_copy(table_hbm.at[idx_v.at[j]], rows_v.at[j % 2], gather_sem.at[j % 2])
        gathers = [gather(0)]
        writes = []
        for j in range(n_chunks):
            gathers[j].wait()
            if j + 1 < n_chunks:
                if j >= 1:
                    writes[j - 1].wait()
                gathers.append(gather(j + 1))
            dst = out_hbm.at[pl.ds(pl.multiple_of((first + j) * SC_ROWS, SC_ROWS), SC_ROWS)]
            writes.append(pltpu.async_copy(rows_v.at[j % 2], dst, write_sem.at[j % 2]))
        for j in range(max(n_chunks - 2, 0), n_chunks):
            writes[j].wait()

    return pl.kernel(
        body,
        out_type=jax.ShapeDtypeStruct((n_rows, width), table.dtype),
        mesh=_sc_mesh(),
        scratch_types=[pltpu.VMEM((n_chunks, SC_ROWS), jnp.int32), pltpu.VMEM((2, SC_ROWS, width), table.dtype),
                       pltpu.SemaphoreType.DMA((2,)), pltpu.SemaphoreType.DMA((2,))],
        name="sc_row_gather",
    )(table, idx.reshape(n_rows // SC_ROWS, SC_ROWS))


def _sc_scatter(rows, pos, n_out):
    n_rows, width = rows.shape
    n_slots = pos.shape[0] // n_rows
    slot_chunks = n_rows // SC_ROWS
    n_chunks = slot_chunks // (SC_CORES * SC_SUBCORES)

    def body(rows_hbm, pos_hbm, out_hbm, idx_v, rows_v, read_sem, scatter_sem):
        first = _sc_worker() * n_chunks
        for s in range(n_slots):
            pltpu.sync_copy(pos_hbm.at[pl.ds(s * slot_chunks + first, n_chunks)], idx_v.at[s])
        read = lambda j: pltpu.async_copy(
            rows_hbm.at[pl.ds(pl.multiple_of((first + j) * SC_ROWS, SC_ROWS), SC_ROWS)],
            rows_v.at[j % 2], read_sem.at[j % 2])
        reads = [read(0)]
        scatters = []
        for j in range(n_chunks):
            reads[j].wait()
            if j + 1 < n_chunks:
                if j >= 1:
                    for copy in scatters[j - 1]:
                        copy.wait()
                reads.append(read(j + 1))
            scatters.append([pltpu.async_copy(rows_v.at[j % 2], out_hbm.at[idx_v.at[s, j]], scatter_sem.at[j % 2])
                             for s in range(n_slots)])
        for j in range(max(n_chunks - 2, 0), n_chunks):
            for copy in scatters[j]:
                copy.wait()

    return pl.kernel(
        body,
        out_type=jax.ShapeDtypeStruct((n_out, width), rows.dtype),
        mesh=_sc_mesh(),
        scratch_types=[pltpu.VMEM((n_slots, n_chunks, SC_ROWS), jnp.int32),
                       pltpu.VMEM((2, SC_ROWS, width), rows.dtype),
                       pltpu.SemaphoreType.DMA((2,)), pltpu.SemaphoreType.DMA((2,))],
        name="sc_row_scatter",
    )(rows, pos.reshape(n_slots * slot_chunks, SC_ROWS))


def _expert_kernel(tile_expert_ref, n_valid_ref, x_ref, wg_ref, wu_ref, wd_ref, y_ref):
    del tile_expert_ref

    @pl.when(pl.program_id(0) < n_valid_ref[0])
    def _():
        x = _unpack_bf16_pairs(x_ref[...])
        gate = jnp.dot(x, wg_ref[0], preferred_element_type=F32)
        up = jnp.dot(x, wu_ref[0], preferred_element_type=F32)
        hid = gate * jax.nn.sigmoid(gate) * up
        y_ref[...] = _pack_bf16_pairs(jnp.dot(_bf(hid), wd_ref[0], preferred_element_type=F32))


def _expert_call(tile_expert, n_valid, x_sorted, w_gate, w_up, w_down, tm):
    n_rows = x_sorted.shape[0]
    rows = pl.BlockSpec((tm, D_MODEL // 2), lambda i, te, nv: (jnp.minimum(i, nv[0] - 1), 0))
    by_expert = lambda i, te, nv: (te[i], 0, 0)
    return pl.pallas_call(
        _expert_kernel,
        grid_spec=pltpu.PrefetchScalarGridSpec(
            num_scalar_prefetch=2,
            grid=(n_rows // tm,),
            in_specs=[rows,
                      pl.BlockSpec((1, D_MODEL, D_EXPERT), by_expert),
                      pl.BlockSpec((1, D_MODEL, D_EXPERT), by_expert),
                      pl.BlockSpec((1, D_EXPERT, D_MODEL), by_expert)],
            out_specs=rows),
        out_shape=jax.ShapeDtypeStruct((n_rows, D_MODEL // 2), jnp.int32),
        compiler_params=pltpu.CompilerParams(
            dimension_semantics=("arbitrary",), vmem_limit_bytes=VMEM_LIMIT),
        name="moe_experts",
    )(tile_expert, n_valid, x_sorted, w_gate, w_up, w_down)


def _final_kernel(h_ref, y1_ref, y2_ref, route_ref, gfin_ref, *rest):
    out_ref = rest[-1]
    route = route_ref[...]
    gate1 = route[:, LANE_GATE1:LANE_GATE1 + 1]
    gate2 = route[:, LANE_GATE2:LANE_GATE2 + 1]
    moe = gate1 * _unpack_bf16_pairs(y1_ref[...]).astype(F32) + gate2 * _unpack_bf16_pairs(y2_ref[...]).astype(F32)
    out_ref[...] = _rms_norm(h_ref[...] + moe, gfin_ref[...])


def _final_call(h, y_pairs, route, g_final, earlier, first_token, tm):
    n_tok = h.shape[0]
    n_blocks = y_pairs.shape[0] // 2 // tm
    assert n_blocks * tm * 2 == y_pairs.shape[0] and first_token % tm == 0, "chunks must be whole row blocks"
    first = first_token // tm
    in_specs = [pl.BlockSpec((tm, D_MODEL), lambda i: (first + i, 0)),
                pl.BlockSpec((tm, D_MODEL // 2), lambda i: (i, 0)),
                pl.BlockSpec((tm, D_MODEL // 2), lambda i: (i + n_blocks, 0)),
                pl.BlockSpec((tm, ROUTER_LANES), lambda i: (first + i, 0)),
                pl.BlockSpec((1, D_MODEL), lambda i: (0, 0))]
    args = [h, y_pairs, y_pairs, route, g_final]
    aliases = {}
    if earlier is not None:
        in_specs.append(pl.BlockSpec(memory_space=pl.ANY))
        args.append(earlier)
        aliases = {len(args) - 1: 0}
    return pl.pallas_call(
        _final_kernel,
        grid=(n_blocks,),
        in_specs=in_specs,
        out_specs=pl.BlockSpec((tm, D_MODEL), lambda i: (first + i, 0)),
        out_shape=jax.ShapeDtypeStruct((n_tok, D_MODEL), F32),
        input_output_aliases=aliases,
        compiler_params=pltpu.CompilerParams(
            dimension_semantics=("arbitrary",), vmem_limit_bytes=VMEM_LIMIT),
        name="moe_combine_norm",
    )(*args)


def _dispatch_plan(route_t, counts, tm):
    n_tok = route_t.shape[1]
    n_tiles = (2 * n_tok) // tm + N_EXPERTS
    counts = counts[0, :N_EXPERTS].astype(jnp.int32)
    tiles_per = (counts + tm - 1) // tm
    tile_end = jnp.cumsum(tiles_per)
    row_start = (tile_end - tiles_per) * tm
    experts = jnp.arange(N_EXPERTS, dtype=jnp.int32)

    def position(idx_lane, rank_lane):
        idx = route_t[idx_lane].astype(jnp.int32)
        start = jnp.sum(jnp.where(idx[None, :] == experts[:, None], row_start[:, None], 0), axis=0)
        return start + route_t[rank_lane].astype(jnp.int32)

    pos = jnp.concatenate([position(LANE_IDX1, LANE_RANK1), position(LANE_IDX2, LANE_RANK2)])
    n_valid = tile_end[-1:]
    tile = jnp.minimum(jnp.arange(n_tiles, dtype=jnp.int32), n_valid - 1)
    tile_expert = jnp.sum((tile_end[None, :] <= tile[:, None]).astype(jnp.int32), axis=1)
    return pos, n_tiles, tile_expert, n_valid


def _block(x, norm_mix_g, w_in, rwkv_mu, conv_w, decay_up, decay_base, aaa_up, aaa_base, gate_up,
           k_k, k_a, r_k, ln_x_w, ln_x_b, w_out, norm_ffn_g, router_group_w, router_group_b,
           router_expert_w, router_expert_b, expert_w_gate, expert_w_up, expert_w_down, norm_final_g,
           *, tm_in, tb_rwkv, tm_out, tm_expert, tm_final):
    bsz, seq, d_model = x.shape
    n_tok = bsz * seq
    row = lambda p: p.reshape(1, -1)
    half = LORA_WA // 2
    zeros = jnp.zeros((half, D_RWKV), F32)
    dup = _bf(jnp.concatenate([decay_up, zeros], axis=0))
    aup = _bf(jnp.concatenate([zeros, aaa_up], axis=0))
    head_of = jnp.arange(QUAD) // HEAD
    ones_quad = (head_of[:, None] == head_of[None, :]).astype(BF16)
    tri = (jnp.arange(CHUNK)[:, None] >= jnp.arange(CHUNK)[None, :]).astype(BF16)

    yconv, r, lw, k2, v, kk, kka, gate, bonus = _in_call(
        x, row(norm_mix_g), w_in, row(rwkv_mu), conv_w, dup, row(decay_base), aup, row(aaa_base),
        _bf(gate_up), row(k_k), row(k_a), row(r_k), ones_quad, tm_in)
    o, w_gate_bf, w_up_bf, w_down_bf = _rwkv_call(r, lw, k2, v, kk, kka, tri,
                                                  (expert_w_gate, expert_w_up, expert_w_down), tb_rwkv)

    pad = ROUTER_LANES - N_EXPERTS - N_GROUPS
    router_w = jnp.concatenate([router_expert_w, router_group_w, jnp.zeros((d_model, pad), F32)], axis=1)
    router_hi = _bf(router_w)
    router_split = jnp.concatenate([router_hi, _bf(router_w - router_hi.astype(F32))], axis=1)
    router_b = jnp.concatenate([router_expert_b, router_group_b, jnp.zeros((pad,), F32)]).reshape(1, -1)
    flat = lambda t: t.reshape(n_tok, t.shape[-1])
    h, u_pairs, route, route_t, counts = _out_call(
        flat(o), flat(bonus), flat(gate), flat(yconv), flat(x), ones_quad,
        row(ln_x_w), row(ln_x_b), w_out, row(norm_ffn_g), router_split, router_b, tm_out, OUT_SPLIT)

    pos, n_tiles, tile_expert, n_valid = _dispatch_plan(route_t, counts, tm_expert)
    x_sorted = _sc_scatter(u_pairs, pos, n_tiles * tm_expert)
    y_sorted = _expert_call(tile_expert, n_valid, x_sorted, w_gate_bf, w_up_bf, w_down_bf, tm_expert)
    out, lo = None, 0
    for fraction in COMBINE_SPLIT:
        size = n_tok // fraction
        pos_c = jnp.concatenate([pos[lo:lo + size], pos[n_tok + lo:n_tok + lo + size]])
        out = _final_call(h, _sc_gather(y_sorted, pos_c), route, row(norm_final_g), out, lo, tm_final)
        lo += size
    assert lo == n_tok, "COMBINE_SPLIT must cover all tokens"
    return out.reshape(bsz, seq, d_model)


def kernel(x, norm_mix_g, w_in, rwkv_mu, conv_w, decay_up, decay_base, aaa_up, aaa_base, gate_up, k_k, k_a, r_k, ln_x_w, ln_x_b, w_out, norm_ffn_g, router_group_w, router_group_b, router_expert_w, router_expert_b, expert_w_gate, expert_w_up, expert_w_down, norm_final_g):
    return _block(x, norm_mix_g[0], w_in[0], rwkv_mu[0], conv_w[0], decay_up[0], decay_base[0],
                  aaa_up[0], aaa_base[0], gate_up[0], k_k[0], k_a[0], r_k[0].reshape(-1), ln_x_w[0],
                  ln_x_b[0], w_out[0], norm_ffn_g[0], router_group_w[0], router_group_b[0],
                  router_expert_w[0], router_expert_b[0], expert_w_gate[0], expert_w_up[0],
                  expert_w_down[0], norm_final_g,
                  tm_in=512, tb_rwkv=1024, tm_out=1024, tm_expert=512, tm_final=1024)
```

```python
import functools

import jax
import jax.numpy as jnp
from jax import lax
from jax.experimental import pallas as pl
from jax.experimental.pallas import tpu as pltpu
from jax.experimental.pallas import tpu_sc as plsc

F32 = jnp.float32
BF16 = jnp.bfloat16

D_MODEL = 1024
D_CONV = 512
CONV_WIDTH = 3
N_HEADS = 8
HEAD = 64
D_RWKV = N_HEADS * HEAD
LORA_WA = 128
GATE_LORA = 128
D_RWKV_PROJ = 3 * D_RWKV + LORA_WA + GATE_LORA
D_IN = 3 * D_CONV + D_RWKV_PROJ
N_GROUPS = 4
EXPERTS_PER_GROUP = 8
N_EXPERTS = N_GROUPS * EXPERTS_PER_GROUP
D_EXPERT = D_MODEL // 4
RMS_EPS = 1e-6
LN_X_EPS = 64e-5
L2_EPS = 1e-12

SUBLANES = 8
CHUNK = 64
QUAD = 4 * HEAD
GROUP = 2 * HEAD
INV_BASE = 8
COMPACT_FROM = 64
OUT_SPLIT = 8
COMBINE_SPLIT = (4, 4, 4, 4)
ROUTER_LANES = 128

VMEM_LIMIT = 56 * 1024 * 1024


def _bf(x):
    return x.astype(BF16)


def _mm(a, b):
    return jnp.dot(_bf(a), _bf(b), preferred_element_type=F32)


def _mm_nt(a, b):
    return lax.dot_general(_bf(a), _bf(b), (((1,), (1,)), ((), ())), preferred_element_type=F32)


def _mm_exact_lhs(lhs_bf16, x, passes):
    acc = None
    rem = x
    for _ in range(passes):
        piece = _bf(rem)
        part = jnp.dot(lhs_bf16, piece, preferred_element_type=F32)
        acc = part if acc is None else acc + part
        rem = rem - piece.astype(F32)
    return acc


def _head_ones():
    return (lax.broadcasted_iota(jnp.int32, (QUAD, QUAD), 0) // HEAD
            == lax.broadcasted_iota(jnp.int32, (QUAD, QUAD), 1) // HEAD).astype(BF16)


def _head_sum(x, ones_quad):
    xb = _bf(x)
    return jnp.concatenate(
        [jnp.dot(xb[:, q * QUAD:(q + 1) * QUAD], ones_quad, preferred_element_type=F32)
         for q in range(x.shape[1] // QUAD)], axis=1)


def _rms_norm(x, g):
    return x * lax.rsqrt(jnp.mean(x * x, axis=-1, keepdims=True) + RMS_EPS) * g


def _shift_rows(cur, prev_rows, k):
    rolled = pltpu.roll(cur, k, 0)
    prev_rolled = pltpu.roll(prev_rows, k, 0)
    n = cur.shape[0]
    head = jnp.concatenate([prev_rolled, rolled[SUBLANES:]], axis=0) if n > SUBLANES else prev_rolled
    row = lax.broadcasted_iota(jnp.int32, cur.shape, 0)
    return jnp.where(row < k, head, rolled)


def _in_kernel(x_ref, g_ref, w_ref, mu_ref, convw_ref, dup_ref, dbase_ref, aup_ref, abase_ref,
               gup_ref, kk_ref, ka_ref, rk_ref,
               yconv_ref, r_ref, lw_ref, k_ref, v_ref, kkn_ref, kka_ref, gate_ref, bonus_ref,
               carry_ref, wbf_ref):
    @pl.when((pl.program_id(0) == 0) & (pl.program_id(1) == 0))
    def _():
        wbf_ref[...] = _bf(w_ref[...])

    @pl.when(pl.program_id(1) == 0)
    def _():
        carry_ref[...] = jnp.zeros_like(carry_ref)

    u = _bf(_rms_norm(x_ref[0], g_ref[...]))
    n_conv = 3 * D_CONV
    rk0 = n_conv
    bounds = dict(conv=(0, n_conv), lora=(rk0 + 3 * D_RWKV, D_IN), k=(rk0 + D_RWKV, rk0 + 2 * D_RWKV),
                  r=(rk0, rk0 + D_RWKV), v=(rk0 + 2 * D_RWKV, rk0 + 3 * D_RWKV))
    z = {name: jnp.dot(u, wbf_ref[:, lo:hi], preferred_element_type=F32) for name, (lo, hi) in bounds.items()}

    def lerp(name):
        lo, hi = bounds[name]
        cur = z[name]
        prev = carry_ref[:, D_CONV + lo - n_conv:D_CONV + hi - n_conv]
        mixed = cur + (_shift_rows(cur, prev, 1) - cur) * mu_ref[:, lo - n_conv:hi - n_conv]
        carry_ref[:, D_CONV + lo - n_conv:D_CONV + hi - n_conv] = cur[-SUBLANES:]
        return mixed

    b_gate = z["conv"][:, :D_CONV]
    ch = z["conv"][:, D_CONV:2 * D_CONV] * z["conv"][:, 2 * D_CONV:]
    prev_ch = carry_ref[:, :D_CONV]
    conv = convw_ref[CONV_WIDTH - 1:CONV_WIDTH, :] * ch
    for delay in range(1, CONV_WIDTH):
        tap = CONV_WIDTH - 1 - delay
        conv = conv + convw_ref[tap:tap + 1, :] * _shift_rows(ch, prev_ch, delay)
    yconv_ref[0] = _bf(b_gate * conv)
    carry_ref[:, :D_CONV] = ch[-SUBLANES:]

    lora = lerp("lora")
    wa_lo = lora[:, :LORA_WA]
    g_lo = lora[:, LORA_WA:]
    no_rows = jnp.zeros_like(dup_ref[...])
    decay_up = _bf(jnp.concatenate([dup_ref[...], no_rows], axis=0))
    rate_up = _bf(jnp.concatenate([no_rows, aup_ref[...]], axis=0))
    dec_in = -(dbase_ref[...] + jnp.dot(_bf(jnp.tanh(wa_lo)), decay_up, preferred_element_type=F32))
    softplus = jnp.maximum(dec_in, 0.0) + jnp.log(1.0 + jnp.exp(-jnp.abs(dec_in)))
    w = -softplus - 0.5
    lw_ref[0] = -jnp.exp(w)
    a = jax.nn.sigmoid(abase_ref[...] + jnp.dot(_bf(wa_lo), rate_up, preferred_element_type=F32))
    gate_ref[0] = _bf(jnp.dot(_bf(jax.nn.sigmoid(g_lo)), _bf(gup_ref[...]), preferred_element_type=F32))

    ones_quad = _head_ones()
    k = lerp("k")
    kk = k * kk_ref[...]
    norm = jnp.sqrt(_head_sum(kk * kk, ones_quad))
    kk = kk / jnp.maximum(norm, L2_EPS)
    k2 = k * (1.0 + (a - 1.0) * ka_ref[...])
    k_ref[0] = _bf(k2)
    kkn_ref[0] = _bf(kk)
    kka_ref[0] = _bf(kk * a)
    r = lerp("r")
    r_ref[0] = _bf(r)
    v = lerp("v")
    v_ref[0] = _bf(v)
    bonus_ref[0] = _bf(_head_sum(r * k2 * rk_ref[...], ones_quad) * v)


def _in_call(x, g, w_in, mu, conv_w, dup, dbase, aup, abase, gup, k_k, k_a, r_k, tm):
    bsz, seq, _ = x.shape
    full = lambda arr: pl.BlockSpec(arr.shape, lambda b, t: (0,) * arr.ndim, pipeline_mode=pl.Buffered(1))
    out_spec = pl.BlockSpec((1, tm, D_RWKV), lambda b, t: (b, t, 0))
    out_dtypes = (BF16, BF16, F32, BF16, BF16, BF16, BF16, BF16, BF16)
    params = (g, w_in, mu, conv_w, dup, dbase, aup, abase, gup, k_k, k_a, r_k)
    return pl.pallas_call(
        _in_kernel,
        grid=(bsz, seq // tm),
        in_specs=[pl.BlockSpec((1, tm, D_MODEL), lambda b, t: (b, t, 0))] + [full(p) for p in params],
        out_specs=[out_spec] * 9,
        out_shape=[jax.ShapeDtypeStruct((bsz, seq, D_RWKV), dt) for dt in out_dtypes],
        scratch_shapes=[pltpu.VMEM((SUBLANES, D_CONV + D_RWKV_PROJ), F32), pltpu.VMEM(w_in.shape, BF16)],
        compiler_params=pltpu.CompilerParams(
            dimension_semantics=("arbitrary", "arbitrary"), vmem_limit_bytes=VMEM_LIMIT),
        name="in_proj",
    )(x, *params)


def _block_diag(y, bd_mask):
    return jnp.where(bd_mask, jnp.concatenate([y] * (GROUP // CHUNK), axis=0), 0.0)


def _each(fn, *lists):
    return [fn(*args) for args in zip(*lists)]


def _unit_lower_inverse(a_strict, t_idx, s_idx, bd):
    bdmm = lambda xs, ys: _each(lambda x, y: _mm(x, bd(y)), xs, ys)
    eye = (t_idx == s_idx).astype(F32)
    same8 = (t_idx // INV_BASE) == (s_idx // INV_BASE)
    a8 = _each(lambda a: jnp.where(same8, a, 0.0), a_strict)
    a8_2 = bdmm(a8, a8)
    a8_34 = bdmm(_each(lambda a, b: jnp.concatenate([a, b], axis=0), a8, a8_2), a8_2)
    inv = _each(lambda a, b, c: eye + a + b + c[:CHUNK], a8, a8_2, a8_34)
    inv = _each(jnp.add, inv, bdmm(inv, _each(lambda c: c[CHUNK:], a8_34)))
    size = 2 * INV_BASE
    while size < COMPACT_FROM:
        off = ((t_idx // size) == (s_idx // size)) & ((t_idx // (size // 2)) != (s_idx // (size // 2)))
        cross = bdmm(_each(lambda a: jnp.where(off, a, 0.0), a_strict), inv)
        inv = _each(jnp.add, inv, bdmm(inv, cross))
        size *= 2
    while size <= CHUNK:
        inv = _compact_level(a_strict, inv, size)
        size *= 2
    return inv


def _compact_level(a_strict, inv, size):
    half = size // 2
    heads = GROUP // CHUNK
    blocks = CHUNK // size
    lane_starts = [h * CHUNK + b * size for h in range(heads) for b in range(blocks)]
    row_starts = [b * size for b in range(blocks)]
    rows_first = lambda x: jnp.concatenate([x[r:r + half] for r in row_starts], axis=0)
    rows_second = lambda x: jnp.concatenate([x[r + half:r + size] for r in row_starts], axis=0)
    lanes_first = lambda x: jnp.concatenate([x[:, c:c + half] for c in lane_starts], axis=1)
    lanes_second = lambda x: jnp.concatenate([x[:, c + half:c + size] for c in lane_starts], axis=1)
    n_rows, n_lanes = CHUNK // 2, GROUP // 2
    own = (lax.broadcasted_iota(jnp.int32, (n_rows, n_lanes), 0) // half
           == (lax.broadcasted_iota(jnp.int32, (n_rows, n_lanes), 1) // half) % blocks)
    diag = (lax.broadcasted_iota(jnp.int32, (n_lanes, n_lanes), 0) // half
            == lax.broadcasted_iota(jnp.int32, (n_lanes, n_lanes), 1) // half)
    bd_half = lambda y: jnp.where(diag, jnp.concatenate([y] * heads, axis=0), 0.0)
    a21 = _each(lambda a: jnp.where(own, lanes_first(rows_second(a)), 0.0), a_strict)
    t11 = _each(lambda t: lanes_first(rows_first(t)), inv)
    t22 = _each(lambda t: lanes_second(rows_second(t)), inv)
    a21_t11 = _each(lambda x, y: _mm(x, bd_half(y)), a21, t11)
    new = _each(lambda x, y: _mm(x, bd_half(y)), t22, a21_t11)

    def placed(x):
        zero_lanes = jnp.zeros((n_rows, half), F32)
        wide = jnp.concatenate(sum(([x[:, i * half:(i + 1) * half], zero_lanes] for i in range(heads * blocks)), []),
                               axis=1)
        zero_rows = jnp.zeros((half, GROUP), F32)
        return jnp.concatenate(sum(([zero_rows, wide[b * half:(b + 1) * half]] for b in range(blocks)), []), axis=0)

    return _each(lambda t, x: t + placed(x), inv, new)


def _chunk_local(r, lw, k, v, kk, kka, tri, t_idx, s_idx, bd_mask):
    bd = lambda y: _block_diag(y, bd_mask)
    bdmm = lambda xs, ys: _each(lambda x, y: _mm(x, bd(y)), xs, ys)
    bdmm2 = lambda xs, ys, zs: _each(lambda x, y, z: _mm(x, jnp.concatenate([bd(y), bd(z)], axis=1)), xs, ys, zs)
    left = lambda xs: _each(lambda x: x[:, :GROUP], xs)
    right = lambda xs: _each(lambda x: x[:, GROUP:], xs)
    top = lambda xs: _each(lambda x: x[:CHUNK], xs)
    bottom = lambda xs: _each(lambda x: x[CHUNK:], xs)

    cum = _each(lambda x: _mm_exact_lhs(tri, x, 3), lw)
    cum_last = _each(lambda c: c[CHUNK - 1:CHUNK, :], cum)
    p_incl = _each(jnp.exp, cum)
    p_excl = _each(lambda c, x: jnp.exp(c - x), cum, lw)
    p_inv = _each(lambda c: jnp.exp(-c), cum)
    to_end = _each(lambda cl, c: jnp.exp(cl - c), cum_last, cum)
    a_t = _each(lambda x, p: -x * p, kk, p_excl)
    r_t = _each(jnp.multiply, r, p_incl)
    b_t = _each(jnp.multiply, kka, p_inv)
    k_t = _each(jnp.multiply, k, p_inv)
    b_end = _each(jnp.multiply, kka, to_end)
    k_end = _each(jnp.multiply, k, to_end)

    ar = _each(lambda a, b: jnp.concatenate([a, b], axis=0), a_t, r_t)
    row2 = lax.broadcasted_iota(jnp.int32, (2 * CHUNK, 2 * GROUP), 0)
    col2 = lax.broadcasted_iota(jnp.int32, (2 * CHUNK, 2 * GROUP), 1) % CHUNK
    causal = col2 < (row2 % CHUNK) + (row2 // CHUNK)
    scores = _each(lambda x, y, z: jnp.where(causal, _mm_nt(x, jnp.concatenate([bd(y), bd(z)], axis=0)), 0.0),
                   ar, b_t, k_t)
    a_ab = top(left(scores))
    a_rb = bottom(left(scores))

    inv = _unit_lower_inverse(a_ab, t_idx, s_idx, bd)
    kv = bdmm(right(scores), v)
    wu = bdmm2(inv, a_t, top(kv))
    ro = bdmm2(a_rb, left(wu), right(wu))
    r_hat = _each(jnp.add, r_t, left(ro))
    o_loc = _each(jnp.add, right(ro), bottom(kv))

    wu_b = _each(lambda x, b: _mm(x.T, b), wu, b_end)
    trans = _each(lambda x: jnp.where(bd_mask, x[:GROUP], 0.0), wu_b)
    d_bd = _each(lambda x, y, z: jnp.where(bd_mask, x[GROUP:] + _mm(y.T, z), 0.0), wu_b, v, k_end)
    d_ls = _each(lambda d: sum(d[h * CHUNK:(h + 1) * CHUNK] for h in range(1, GROUP // CHUNK)) + d[:CHUNK], d_bd)
    p_end = _each(jnp.exp, cum_last)
    return r_hat, o_loc, trans, d_ls, p_end


def _rwkv_kernel(r_ref, lw_ref, k_ref, v_ref, kk_ref, kka_ref, wg_ref, wu_ref, wd_ref,
                 o_ref, wg_bf_ref, wu_bf_ref, wd_bf_ref, state_ref, *, n_sub):
    @pl.when(pl.program_id(1) == 0)
    def _():
        state_ref[...] = jnp.zeros_like(state_ref)

    wg_bf_ref[...] = _bf(wg_ref[...])
    wu_bf_ref[...] = _bf(wu_ref[...])
    wd_bf_ref[...] = _bf(wd_ref[...])

    t_idx = lax.broadcasted_iota(jnp.int32, (CHUNK, GROUP), 0)
    s_idx = lax.broadcasted_iota(jnp.int32, (CHUNK, GROUP), 1) % CHUNK
    bd_mask = (lax.broadcasted_iota(jnp.int32, (GROUP, GROUP), 0) // CHUNK
               == lax.broadcasted_iota(jnp.int32, (GROUP, GROUP), 1) // CHUNK)
    chunk_row = lax.broadcasted_iota(jnp.int32, (CHUNK, CHUNK), 0)
    chunk_col = lax.broadcasted_iota(jnp.int32, (CHUNK, CHUNK), 1)
    n_group = D_RWKV // GROUP
    where = [(slice(c * CHUNK, (c + 1) * CHUNK), slice(g * GROUP, (g + 1) * GROUP))
             for c in range(n_sub) for g in range(n_group)]
    load = lambda ref: [ref[0, rows, lanes].astype(F32) for rows, lanes in where]
    r_hat, o_loc, trans, d_ls, p_end = _chunk_local(
        load(r_ref), load(lw_ref), load(k_ref), load(v_ref), load(kk_ref), load(kka_ref),
        (chunk_row >= chunk_col).astype(BF16), t_idx, s_idx, bd_mask)

    state = [state_ref[g] for g in range(n_group)]
    for c in range(n_sub):
        chains = range(c * n_group, (c + 1) * n_group)
        out = [_mm_nt(r_hat[i], _block_diag(state[g], bd_mask)) + o_loc[i] for g, i in enumerate(chains)]
        for g, i in enumerate(chains):
            rows, lanes = where[i]
            o_ref[0, rows, lanes] = _bf(out[g])
        state = [state[g] * p_end[i] + _mm(state[g], trans[i]) + d_ls[i] for g, i in enumerate(chains)]
    for g in range(n_group):
        state_ref[g] = state[g]


def _rwkv_call(r, lw, k, v, kk, kka, expert_weights, tb):
    bsz, seq, _ = r.shape
    n_t = seq // tb
    per_step = N_EXPERTS // (bsz * n_t)
    assert per_step * bsz * n_t == N_EXPERTS, "grid steps must divide the expert count"
    spec = pl.BlockSpec((1, tb, D_RWKV), lambda b, t: (b, t, 0))
    w_specs = [pl.BlockSpec((per_step,) + w.shape[1:], lambda b, t: (b * n_t + t, 0, 0)) for w in expert_weights]
    return pl.pallas_call(
        functools.partial(_rwkv_kernel, n_sub=tb // CHUNK),
        grid=(bsz, n_t),
        in_specs=[spec] * 6 + w_specs,
        out_specs=[spec] + w_specs,
        out_shape=[jax.ShapeDtypeStruct((bsz, seq, D_RWKV), BF16)]
        + [jax.ShapeDtypeStruct(w.shape, BF16) for w in expert_weights],
        scratch_shapes=[pltpu.VMEM((D_RWKV // GROUP, HEAD, GROUP), F32)],
        compiler_params=pltpu.CompilerParams(
            dimension_semantics=("arbitrary", "arbitrary"), vmem_limit_bytes=VMEM_LIMIT),
        name="rwkv_chunk",
    )(r, lw, k, v, kk, kka, *expert_weights)


def _route(logits):
    lane_i = lax.broadcasted_iota(jnp.int32, logits.shape, 1)
    lane = lane_i.astype(F32)
    lane_group = (lane_i // EXPERTS_PER_GROUP).astype(F32)
    neg = -jnp.inf
    big = float(ROUTER_LANES)
    is_group = (lane_i >= N_EXPERTS) & (lane_i < N_EXPERTS + N_GROUPS)
    gl = jnp.where(is_group, logits, neg)
    ge = jnp.exp(gl - jnp.max(gl, axis=-1, keepdims=True))
    gprob = ge / jnp.sum(ge, axis=-1, keepdims=True)
    g_top = jnp.max(gprob, axis=-1, keepdims=True)
    g_idx = jnp.min(jnp.where(is_group & (gprob == g_top), lane - N_EXPERTS, big), axis=-1, keepdims=True)

    in_group = (lane_i < N_EXPERTS) & (lane_group == g_idx)
    el = jnp.where(in_group, logits, neg)
    top1 = jnp.max(el, axis=-1, keepdims=True)
    idx1 = jnp.min(jnp.where(in_group & (el == top1), lane, big), axis=-1, keepdims=True)
    el2 = jnp.where(lane == idx1, neg, el)
    top2 = jnp.max(el2, axis=-1, keepdims=True)
    idx2 = jnp.min(jnp.where(in_group & (lane != idx1) & (el2 == top2), lane, big), axis=-1, keepdims=True)
    e2 = jnp.exp(top2 - top1)
    denom = 1.0 + e2
    return idx1, idx2, g_top * (1.0 / denom), g_top * (e2 / denom)


def _pack_bf16_pairs(x):
    bits = lax.bitcast_convert_type(_bf(x).astype(F32), jnp.uint32)
    k = x.shape[1] // 2
    return lax.bitcast_convert_type((bits[:, :k] >> 16) | bits[:, k:], jnp.int32)


def _unpack_bf16_pairs(words):
    bits = lax.bitcast_convert_type(words, jnp.uint32)
    lo = lax.bitcast_convert_type(bits << 16, F32)
    hi = lax.bitcast_convert_type(bits & jnp.uint32(0xFFFF0000), F32)
    return _bf(jnp.concatenate([lo, hi], axis=1))


LANE_IDX1, LANE_IDX2, LANE_RANK1, LANE_RANK2, LANE_GATE1, LANE_GATE2 = range(6)


def _out_kernel(o_ref, bonus_ref, gate_ref, yconv_ref, x_ref, lnw_ref, lnb_ref,
                wout_ref, gffn_ref, rw_ref, rb_ref, h_ref, u_ref, route_ref, route_t_ref, count_ref,
                seen_ref, wbf_ref, *, n_split):
    @pl.when(pl.program_id(0) == 0)
    def _():
        seen_ref[...] = jnp.zeros_like(seen_ref)
        wbf_ref[...] = _bf(wout_ref[...])

    sub = o_ref.shape[0] // n_split
    parts = [slice(s * sub, (s + 1) * sub) for s in range(n_split)]
    read = lambda ref: [ref[p, :] for p in parts]
    ones_quad = _head_ones()
    inv_n = 1.0 / HEAD
    o = _each(lambda v: v.astype(F32), read(o_ref))
    mean = _each(lambda v: _head_sum(v, ones_quad) * inv_n, o)
    cen = _each(jnp.subtract, o, mean)
    var = _each(lambda c: _head_sum(c * c, ones_quad) * inv_n, cen)
    on = _each(lambda c, v: c * lax.rsqrt(v + LN_X_EPS) * lnw_ref[...] + lnb_ref[...], cen, var)
    y_rwkv = _each(lambda a, b, g: (a + b) * g, on, read(bonus_ref), read(gate_ref))
    mixed = _each(lambda yc, yr: (jnp.dot(yc, wbf_ref[:D_CONV, :], preferred_element_type=F32)
                                  + jnp.dot(_bf(yr), wbf_ref[D_CONV:, :], preferred_element_type=F32)),
                  read(yconv_ref), y_rwkv)
    h = _each(jnp.add, read(x_ref), mixed)
    u = _each(lambda v: _rms_norm(v, gffn_ref[...]), h)
    u_hi = _each(_bf, u)
    u_lo = _each(lambda a, b: _bf(a - b.astype(F32)), u, u_hi)
    by_hi = _each(lambda a: jnp.dot(a, rw_ref[...], preferred_element_type=F32), u_hi)
    by_lo = _each(lambda a: jnp.dot(a, rw_ref[:, :ROUTER_LANES], preferred_element_type=F32), u_lo)
    logits = _each(lambda a, b: a[:, :ROUTER_LANES] + a[:, ROUTER_LANES:] + b + rb_ref[...], by_hi, by_lo)
    routed = _each(_route, logits)

    lane_i = lax.broadcasted_iota(jnp.int32, (sub, ROUTER_LANES), 1)
    lane = lane_i.astype(F32)
    hit1 = _each(lambda rt: lane == rt[0], routed)
    hit2 = _each(lambda rt: lane == rt[1], routed)
    both = _each(lambda a, b: a.astype(F32) + b.astype(F32), hit1, hit2)
    earlier = (lax.broadcasted_iota(jnp.int32, (sub, sub), 0) > lax.broadcasted_iota(jnp.int32, (sub, sub), 1))
    inside = _each(lambda b: jnp.dot(earlier.astype(BF16), _bf(b), preferred_element_type=F32), both)
    seen = seen_ref[...]
    for s, p in enumerate(parts):
        idx1, idx2, gate1, gate2 = routed[s]
        before = inside[s] + seen
        rank1 = jnp.sum(jnp.where(hit1[s], before, 0.0), axis=-1, keepdims=True)
        rank2 = jnp.sum(jnp.where(hit2[s], before, 0.0), axis=-1, keepdims=True)
        seen = seen + jnp.sum(both[s], axis=0, keepdims=True)
        route = jnp.zeros((sub, ROUTER_LANES), F32)
        for lane_id, col in ((LANE_IDX1, idx1), (LANE_IDX2, idx2), (LANE_RANK1, rank1),
                             (LANE_RANK2, rank2), (LANE_GATE1, gate1), (LANE_GATE2, gate2)):
            route = jnp.where(lane_i == lane_id, col, route)
        route_ref[p, :] = route
        route_t_ref[:, p] = route.T[:SUBLANES, :]
        h_ref[p, :] = _bf(h[s])
        u_ref[p, :] = _pack_bf16_pairs(u[s])
    seen_ref[...] = seen
    count_ref[...] = jnp.broadcast_to(seen, count_ref.shape)


def _out_call(o, bonus, gate, yconv, x, ln_w, ln_b, w_out, g_ffn, router_w, router_b, tm, n_split):
    n_tok = x.shape[0]
    row = lambda width: pl.BlockSpec((tm, width), lambda i: (i, 0))
    full = lambda arr: pl.BlockSpec(arr.shape, lambda i: (0,) * arr.ndim, pipeline_mode=pl.Buffered(1))
    params = (ln_w, ln_b, w_out, g_ffn, router_w, router_b)
    return pl.pallas_call(
        functools.partial(_out_kernel, n_split=n_split),
        grid=(n_tok // tm,),
        in_specs=[row(D_RWKV)] * 4 + [row(D_MODEL)] + [full(p) for p in params],
        out_specs=[row(D_MODEL), row(D_MODEL // 2), row(ROUTER_LANES),
                   pl.BlockSpec((SUBLANES, tm), lambda i: (0, i)),
                   pl.BlockSpec((SUBLANES, ROUTER_LANES), lambda i: (0, 0))],
        out_shape=[jax.ShapeDtypeStruct((n_tok, D_MODEL), BF16),
                   jax.ShapeDtypeStruct((n_tok, D_MODEL // 2), jnp.int32),
                   jax.ShapeDtypeStruct((n_tok, ROUTER_LANES), F32),
                   jax.ShapeDtypeStruct((SUBLANES, n_tok), F32),
                   jax.ShapeDtypeStruct((SUBLANES, ROUTER_LANES), F32)],
        scratch_shapes=[pltpu.VMEM((1, ROUTER_LANES), F32), pltpu.VMEM(w_out.shape, BF16)],
        compiler_params=pltpu.CompilerParams(
            dimension_semantics=("arbitrary",), vmem_limit_bytes=VMEM_LIMIT),
        name="out_proj_route",
    )(o, bonus, gate, yconv, x, *params)


SC_CORES = 2
SC_SUBCORES = 16
SC_ROWS = 64


def _sc_mesh():
    return plsc.VectorSubcoreMesh(core_axis_name="c", subcore_axis_name="s",
                                  num_cores=SC_CORES, num_subcores=SC_SUBCORES)


def _sc_worker():
    return lax.axis_index("s") * SC_CORES + lax.axis_index("c")


def _sc_gather(table, idx):
    n_rows = idx.shape[0]
    width = table.shape[1]
    n_chunks = n_rows // (SC_CORES * SC_SUBCORES * SC_ROWS)

    def body(table_hbm, idx_hbm, out_hbm, idx_v, rows_v, gather_sem, write_sem):
        first = _sc_worker() * n_chunks
        pltpu.sync_copy(idx_hbm.at[pl.ds(first, n_chunks)], idx_v)
        gather = lambda j: pltpu.async_copy(table_hbm.at[idx_v.at[j]], rows_v.at[j % 2], gather_sem.at[j % 2])
        gathers = [gather(0)]
        writes = []
        for j in range(n_chunks):
            gathers[j].wait()
            if j + 1 < n_chunks:
                if j >= 1:
                    writes[j - 1].wait()
                gathers.append(gather(j + 1))
            dst = out_hbm.at[pl.ds(pl.multiple_of((first + j) * SC_ROWS, SC_ROWS), SC_ROWS)]
            writes.append(pltpu.async_copy(rows_v.at[j % 2], dst, write_sem.at[j % 2]))
        for j in range(max(n_chunks - 2, 0), n_chunks):
            writes[j].wait()

    return pl.kernel(
        body,
        out_type=jax.ShapeDtypeStruct((n_rows, width), table.dtype),
        mesh=_sc_mesh(),
        scratch_types=[pltpu.VMEM((n_chunks, SC_ROWS), jnp.int32), pltpu.VMEM((2, SC_ROWS, width), table.dtype),
                       pltpu.SemaphoreType.DMA((2,)), pltpu.SemaphoreType.DMA((2,))],
        name="sc_row_gather",
    )(table, idx.reshape(n_rows // SC_ROWS, SC_ROWS))


def _sc_scatter(rows, pos, n_out):
    n_rows, width = rows.shape
    n_slots = pos.shape[0] // n_rows
    slot_chunks = n_rows // SC_ROWS
    n_chunks = slot_chunks // (SC_CORES * SC_SUBCORES)

    def body(rows_hbm, pos_hbm, out_hbm, idx_v, rows_v, read_sem, scatter_sem):
        first = _sc_worker() * n_chunks
        for s in range(n_slots):
            pltpu.sync_copy(pos_hbm.at[pl.ds(s * slot_chunks + first, n_chunks)], idx_v.at[s])
        read = lambda j: pltpu.async_copy(
            rows_hbm.at[pl.ds(pl.multiple_of((first + j) * SC_ROWS, SC_ROWS), SC_ROWS)],
            rows_v.at[j % 2], read_sem.at[j % 2])
        reads = [read(0)]
        scatters = []
        for j in range(n_chunks):
            reads[j].wait()
            if j + 1 < n_chunks:
                if j >= 1:
                    for copy in scatters[j - 1]:
                        copy.wait()
                reads.append(read(j + 1))
            scatters.append([pltpu.async_copy(rows_v.at[j % 2], out_hbm.at[idx_v.at[s, j]], scatter_sem.at[j % 2])
                             for s in range(n_slots)])
        for j in range(max(n_chunks - 2, 0), n_chunks):
            for copy in scatters[j]:
                copy.wait()

    return pl.kernel(
        body,
        out_type=jax.ShapeDtypeStruct((n_out, width), rows.dtype),
        mesh=_sc_mesh(),
        scratch_types=[pltpu.VMEM((n_slots, n_chunks, SC_ROWS), jnp.int32),
                       pltpu.VMEM((2, SC_ROWS, width), rows.dtype),
                       pltpu.SemaphoreType.DMA((2,)), pltpu.SemaphoreType.DMA((2,))],
        name="sc_row_scatter",
    )(rows, pos.reshape(n_slots * slot_chunks, SC_ROWS))


def _expert_kernel(tile_expert_ref, n_valid_ref, x_ref, wg_ref, wu_ref, wd_ref, y_ref):
    del tile_expert_ref

    @pl.when(pl.program_id(0) < n_valid_ref[0])
    def _():
        x = _unpack_bf16_pairs(x_ref[...])
        gate = jnp.dot(x, wg_ref[0], preferred_element_type=F32)
        up = jnp.dot(x, wu_ref[0], preferred_element_type=F32)
        hid = gate * jax.nn.sigmoid(gate) * up
        y_ref[...] = _pack_bf16_pairs(jnp.dot(_bf(hid), wd_ref[0], preferred_element_type=F32))


def _expert_call(tile_expert, n_valid, x_sorted, w_gate, w_up, w_down, tm):
    n_rows = x_sorted.shape[0]
    rows = pl.BlockSpec((tm, D_MODEL // 2), lambda i, te, nv: (jnp.minimum(i, nv[0] - 1), 0))
    by_expert = lambda i, te, nv: (te[i], 0, 0)
    return pl.pallas_call(
        _expert_kernel,
        grid_spec=pltpu.PrefetchScalarGridSpec(
            num_scalar_prefetch=2,
            grid=(n_rows // tm,),
            in_specs=[rows,
                      pl.BlockSpec((1, D_MODEL, D_EXPERT), by_expert),
                      pl.BlockSpec((1, D_MODEL, D_EXPERT), by_expert),
                      pl.BlockSpec((1, D_EXPERT, D_MODEL), by_expert)],
            out_specs=rows),
        out_shape=jax.ShapeDtypeStruct((n_rows, D_MODEL // 2), jnp.int32),
        compiler_params=pltpu.CompilerParams(
            dimension_semantics=("arbitrary",), vmem_limit_bytes=VMEM_LIMIT),
        name="moe_experts",
    )(tile_expert, n_valid, x_sorted, w_gate, w_up, w_down)


def _final_kernel(h_ref, y1_ref, y2_ref, route_ref, gfin_ref, *rest):
    out_ref = rest[-1]
    route = route_ref[...]
    gate1 = route[:, LANE_GATE1:LANE_GATE1 + 1]
    gate2 = route[:, LANE_GATE2:LANE_GATE2 + 1]
    moe = gate1 * _unpack_bf16_pairs(y1_ref[...]).astype(F32) + gate2 * _unpack_bf16_pairs(y2_ref[...]).astype(F32)
    out_ref[...] = _rms_norm(h_ref[...] + moe, gfin_ref[...])


def _final_call(h, y_pairs, route, g_final, earlier, first_token, tm):
    n_tok = h.shape[0]
    n_blocks = y_pairs.shape[0] // 2 // tm
    assert n_blocks * tm * 2 == y_pairs.shape[0] and first_token % tm == 0, "chunks must be whole row blocks"
    first = first_token // tm
    in_specs = [pl.BlockSpec((tm, D_MODEL), lambda i: (first + i, 0)),
                pl.BlockSpec((tm, D_MODEL // 2), lambda i: (i, 0)),
                pl.BlockSpec((tm, D_MODEL // 2), lambda i: (i + n_blocks, 0)),
                pl.BlockSpec((tm, ROUTER_LANES), lambda i: (first + i, 0)),
                pl.BlockSpec((1, D_MODEL), lambda i: (0, 0))]
    args = [h, y_pairs, y_pairs, route, g_final]
    aliases = {}
    if earlier is not None:
        in_specs.append(pl.BlockSpec(memory_space=pl.ANY))
        args.append(earlier)
        aliases = {len(args) - 1: 0}
    return pl.pallas_call(
        _final_kernel,
        grid=(n_blocks,),
        in_specs=in_specs,
        out_specs=pl.BlockSpec((tm, D_MODEL), lambda i: (first + i, 0)),
        out_shape=jax.ShapeDtypeStruct((n_tok, D_MODEL), F32),
        input_output_aliases=aliases,
        compiler_params=pltpu.CompilerParams(
            dimension_semantics=("arbitrary",), vmem_limit_bytes=VMEM_LIMIT),
        name="moe_combine_norm",
    )(*args)


def _dispatch_plan(route_t, counts, tm):
    n_tok = route_t.shape[1]
    n_tiles = (2 * n_tok) // tm + N_EXPERTS
    counts = counts[0, :N_EXPERTS].astype(jnp.int32)
    tiles_per = (counts + tm - 1) // tm
    tile_end = jnp.cumsum(tiles_per)
    row_start = (tile_end - tiles_per) * tm
    experts = jnp.arange(N_EXPERTS, dtype=jnp.int32)

    def position(idx_lane, rank_lane):
        idx = route_t[idx_lane].astype(jnp.int32)
        start = jnp.sum(jnp.where(idx[None, :] == experts[:, None], row_start[:, None], 0), axis=0)
        return start + route_t[rank_lane].astype(jnp.int32)

    pos = jnp.concatenate([position(LANE_IDX1, LANE_RANK1), position(LANE_IDX2, LANE_RANK2)])
    n_valid = tile_end[-1:]
    tile = jnp.minimum(jnp.arange(n_tiles, dtype=jnp.int32), n_valid - 1)
    tile_expert = jnp.sum((tile_end[None, :] <= tile[:, None]).astype(jnp.int32), axis=1)
    return pos, n_tiles, tile_expert, n_valid


def _block(x, norm_mix_g, w_in, rwkv_mu, conv_w, decay_up, decay_base, aaa_up, aaa_base, gate_up,
           k_k, k_a, r_k, ln_x_w, ln_x_b, w_out, norm_ffn_g, router_group_w, router_group_b,
           router_expert_w, router_expert_b, expert_w_gate, expert_w_up, expert_w_down, norm_final_g,
           *, tm_in, tb_rwkv, tm_out, tm_expert, tm_final):
    bsz, seq, d_model = x.shape
    n_tok = bsz * seq
    row = lambda p: p.reshape(1, -1)

    yconv, r, lw, k2, v, kk, kka, gate, bonus = _in_call(
        x, row(norm_mix_g), w_in, row(rwkv_mu), conv_w, decay_up, row(decay_base), aaa_up, row(aaa_base),
        gate_up, row(k_k), row(k_a), row(r_k), tm_in)
    o, w_gate_bf, w_up_bf, w_down_bf = _rwkv_call(r, lw, k2, v, kk, kka,
                                                  (expert_w_gate, expert_w_up, expert_w_down), tb_rwkv)

    pad = ROUTER_LANES - N_EXPERTS - N_GROUPS
    router_w = jnp.concatenate([router_expert_w, router_group_w, jnp.zeros((d_model, pad), F32)], axis=1)
    router_hi = _bf(router_w)
    router_split = jnp.concatenate([router_hi, _bf(router_w - router_hi.astype(F32))], axis=1)
    router_b = jnp.concatenate([router_expert_b, router_group_b, jnp.zeros((pad,), F32)]).reshape(1, -1)
    flat = lambda t: t.reshape(n_tok, t.shape[-1])
    h, u_pairs, route, route_t, counts = _out_call(
        flat(o), flat(bonus), flat(gate), flat(yconv), flat(x),
        row(ln_x_w), row(ln_x_b), w_out, row(norm_ffn_g), router_split, router_b, tm_out, OUT_SPLIT)

    pos, n_tiles, tile_expert, n_valid = _dispatch_plan(route_t, counts, tm_expert)
    x_sorted = _sc_scatter(u_pairs, pos, n_tiles * tm_expert)
    y_sorted = _expert_call(tile_expert, n_valid, x_sorted, w_gate_bf, w_up_bf, w_down_bf, tm_expert)
    out, lo = None, 0
    for fraction in COMBINE_SPLIT:
        size = n_tok // fraction
        pos_c = jnp.concatenate([pos[lo:lo + size], pos[n_tok + lo:n_tok + lo + size]])
        out = _final_call(h, _sc_gather(y_sorted, pos_c), route, row(norm_final_g), out, lo, tm_final)
        lo += size
    assert lo == n_tok, "COMBINE_SPLIT must cover all tokens"
    return out.reshape(bsz, seq, d_model)


def kernel(x, norm_mix_g, w_in, rwkv_mu, conv_w, decay_up, decay_base, aaa_up, aaa_base, gate_up, k_k, k_a, r_k, ln_x_w, ln_x_b, w_out, norm_ffn_g, router_group_w, router_group_b, router_expert_w, router_expert_b, expert_w_gate, expert_w_up, expert_w_down, norm_final_g):
    return _block(x, norm_mix_g[0], w_in[0], rwkv_mu[0], conv_w[0], decay_up[0], decay_base[0],
                  aaa_up[0], aaa_base[0], gate_up[0], k_k[0], k_a[0], r_k[0].reshape(-1), ln_x_w[0],
                  ln_x_b[0], w_out[0], norm_ffn_g[0], router_group_w[0], router_group_b[0],
                  router_expert_w[0], router_expert_b[0], expert_w_gate[0], expert_w_up[0],
                  expert_w_down[0], norm_final_g,
                  tm_in=512, tb_rwkv=1024, tm_out=1024, tm_expert=512, tm_final=1024)
```

```python
import functools

import jax
import jax.numpy as jnp
from jax import lax
from jax.experimental import pallas as pl
from jax.experimental.pallas import tpu as pltpu
from jax.experimental.pallas import tpu_sc as plsc

F32 = jnp.float32
BF16 = jnp.bfloat16

D_MODEL = 1024
D_CONV = 512
CONV_WIDTH = 3
N_HEADS = 8
HEAD = 64
D_RWKV = N_HEADS * HEAD
LORA_WA = 128
GATE_LORA = 128
D_RWKV_PROJ = 3 * D_RWKV + LORA_WA + GATE_LORA
D_IN = 3 * D_CONV + D_RWKV_PROJ
N_GROUPS = 4
EXPERTS_PER_GROUP = 8
N_EXPERTS = N_GROUPS * EXPERTS_PER_GROUP
D_EXPERT = D_MODEL // 4
RMS_EPS = 1e-6
LN_X_EPS = 64e-5
L2_EPS = 1e-12

SUBLANES = 8
CHUNK = 64
QUAD = 4 * HEAD
GROUP = 2 * HEAD
INV_BASE = 8
COMPACT_FROM = 64
OUT_SPLIT = 8
COMBINE_SPLIT = (4, 4, 4, 4)
ROUTER_LANES = 128

VMEM_LIMIT = 56 * 1024 * 1024


def _bf(x):
    return x.astype(BF16)


def _mm(a, b):
    return jnp.dot(_bf(a), _bf(b), preferred_element_type=F32)


def _mm_nt(a, b):
    return lax.dot_general(_bf(a), _bf(b), (((1,), (1,)), ((), ())), preferred_element_type=F32)


def _mm_exact_lhs(lhs_bf16, x, passes):
    acc = None
    rem = x
    for _ in range(passes):
        piece = _bf(rem)
        part = jnp.dot(lhs_bf16, piece, preferred_element_type=F32)
        acc = part if acc is None else acc + part
        rem = rem - piece.astype(F32)
    return acc


def _head_ones():
    return (lax.broadcasted_iota(jnp.int32, (QUAD, QUAD), 0) // HEAD
            == lax.broadcasted_iota(jnp.int32, (QUAD, QUAD), 1) // HEAD).astype(BF16)


def _head_sum(x, ones_quad):
    xb = _bf(x)
    return jnp.concatenate(
        [jnp.dot(xb[:, q * QUAD:(q + 1) * QUAD], ones_quad, preferred_element_type=F32)
         for q in range(x.shape[1] // QUAD)], axis=1)


def _rms_norm(x, g):
    return x * lax.rsqrt(jnp.mean(x * x, axis=-1, keepdims=True) + RMS_EPS) * g


def _shift_rows(cur, prev_rows, k):
    rolled = pltpu.roll(cur, k, 0)
    prev_rolled = pltpu.roll(prev_rows, k, 0)
    n = cur.shape[0]
    head = jnp.concatenate([prev_rolled, rolled[SUBLANES:]], axis=0) if n > SUBLANES else prev_rolled
    row = lax.broadcasted_iota(jnp.int32, cur.shape, 0)
    return jnp.where(row < k, head, rolled)


def _in_kernel(x_ref, g_ref, w_ref, mu_ref, convw_ref, dup_ref, dbase_ref, aup_ref, abase_ref,
               gup_ref, kk_ref, ka_ref, rk_ref,
               yconv_ref, r_ref, lw_ref, k_ref, v_ref, kkn_ref, kka_ref, gate_ref, bonus_ref,
               carry_ref, wbf_ref):
    @pl.when((pl.program_id(0) == 0) & (pl.program_id(1) == 0))
    def _():
        wbf_ref[...] = _bf(w_ref[...])

    @pl.when(pl.program_id(1) == 0)
    def _():
        carry_ref[...] = jnp.zeros_like(carry_ref)

    u = _bf(_rms_norm(x_ref[0], g_ref[...]))
    n_conv = 3 * D_CONV
    rk0 = n_conv
    bounds = dict(conv=(0, n_conv), lora=(rk0 + 3 * D_RWKV, D_IN), k=(rk0 + D_RWKV, rk0 + 2 * D_RWKV),
                  r=(rk0, rk0 + D_RWKV), v=(rk0 + 2 * D_RWKV, rk0 + 3 * D_RWKV))
    z = {name: jnp.dot(u, wbf_ref[:, lo:hi], preferred_element_type=F32) for name, (lo, hi) in bounds.items()}

    def lerp(name):
        lo, hi = bounds[name]
        cur = z[name]
        prev = carry_ref[:, D_CONV + lo - n_conv:D_CONV + hi - n_conv]
        mixed = cur + (_shift_rows(cur, prev, 1) - cur) * mu_ref[:, lo - n_conv:hi - n_conv]
        carry_ref[:, D_CONV + lo - n_conv:D_CONV + hi - n_conv] = cur[-SUBLANES:]
        return mixed

    b_gate = z["conv"][:, :D_CONV]
    ch = z["conv"][:, D_CONV:2 * D_CONV] * z["conv"][:, 2 * D_CONV:]
    prev_ch = carry_ref[:, :D_CONV]
    conv = convw_ref[CONV_WIDTH - 1:CONV_WIDTH, :] * ch
    for delay in range(1, CONV_WIDTH):
        tap = CONV_WIDTH - 1 - delay
        conv = conv + convw_ref[tap:tap + 1, :] * _shift_rows(ch, prev_ch, delay)
    yconv_ref[0] = _bf(b_gate * conv)
    carry_ref[:, :D_CONV] = ch[-SUBLANES:]

    lora = lerp("lora")
    wa_lo = lora[:, :LORA_WA]
    g_lo = lora[:, LORA_WA:]
    no_rows = jnp.zeros_like(dup_ref[...])
    decay_up = _bf(jnp.concatenate([dup_ref[...], no_rows], axis=0))
    rate_up = _bf(jnp.concatenate([no_rows, aup_ref[...]], axis=0))
    dec_in = -(dbase_ref[...] + jnp.dot(_bf(jnp.tanh(wa_lo)), decay_up, preferred_element_type=F32))
    softplus = jnp.maximum(dec_in, 0.0) + jnp.log(1.0 + jnp.exp(-jnp.abs(dec_in)))
    w = -softplus - 0.5
    lw_ref[0] = -jnp.exp(w)
    a = jax.nn.sigmoid(abase_ref[...] + jnp.dot(_bf(wa_lo), rate_up, preferred_element_type=F32))
    gate_ref[0] = _bf(jnp.dot(_bf(jax.nn.sigmoid(g_lo)), _bf(gup_ref[...]), preferred_element_type=F32))

    ones_quad = _head_ones()
    k = lerp("k")
    kk = k * kk_ref[...]
    norm = jnp.sqrt(_head_sum(kk * kk, ones_quad))
    kk = kk / jnp.maximum(norm, L2_EPS)
    k2 = k * (1.0 + (a - 1.0) * ka_ref[...])
    k_ref[0] = _bf(k2)
    kkn_ref[0] = _bf(kk)
    kka_ref[0] = _bf(kk * a)
    r = lerp("r")
    r_ref[0] = _bf(r)
    v = lerp("v")
    v_ref[0] = _bf(v)
    bonus_ref[0] = _bf(_head_sum(r * k2 * rk_ref[...], ones_quad) * v)


def _in_call(x, g, w_in, mu, conv_w, dup, dbase, aup, abase, gup, k_k, k_a, r_k, tm):
    bsz, seq, _ = x.shape
    full = lambda arr: pl.BlockSpec(arr.shape, lambda b, t: (0,) * arr.ndim, pipeline_mode=pl.Buffered(1))
    out_spec = pl.BlockSpec((1, tm, D_RWKV), lambda b, t: (b, t, 0))
    out_dtypes = (BF16, BF16, F32, BF16, BF16, BF16, BF16, BF16, BF16)
    params = (g, w_in, mu, conv_w, dup, dbase, aup, abase, gup, k_k, k_a, r_k)
    return pl.pallas_call(
        _in_kernel,
        grid=(bsz, seq // tm),
        in_specs=[pl.BlockSpec((1, tm, D_MODEL), lambda b, t: (b, t, 0))] + [full(p) for p in params],
        out_specs=[out_spec] * 9,
        out_shape=[jax.ShapeDtypeStruct((bsz, seq, D_RWKV), dt) for dt in out_dtypes],
        scratch_shapes=[pltpu.VMEM((SUBLANES, D_CONV + D_RWKV_PROJ), F32), pltpu.VMEM(w_in.shape, BF16)],
        compiler_params=pltpu.CompilerParams(
            dimension_semantics=("arbitrary", "arbitrary"), vmem_limit_bytes=VMEM_LIMIT),
        name="in_proj",
    )(x, *params)


def _block_diag(y, bd_mask):
    return jnp.where(bd_mask, jnp.concatenate([y] * (GROUP // CHUNK), axis=0), 0.0)


def _each(fn, *lists):
    return [fn(*args) for args in zip(*lists)]


def _unit_lower_inverse(a_strict, t_idx, s_idx, bd):
    bdmm = lambda xs, ys: _each(lambda x, y: _mm(x, bd(y)), xs, ys)
    eye = (t_idx == s_idx).astype(F32)
    same8 = (t_idx // INV_BASE) == (s_idx // INV_BASE)
    a8 = _each(lambda a: jnp.where(same8, a, 0.0), a_strict)
    a8_2 = bdmm(a8, a8)
    a8_34 = bdmm(_each(lambda a, b: jnp.concatenate([a, b], axis=0), a8, a8_2), a8_2)
    inv = _each(lambda a, b, c: eye + a + b + c[:CHUNK], a8, a8_2, a8_34)
    inv = _each(jnp.add, inv, bdmm(inv, _each(lambda c: c[CHUNK:], a8_34)))
    size = 2 * INV_BASE
    while size < COMPACT_FROM:
        off = ((t_idx // size) == (s_idx // size)) & ((t_idx // (size // 2)) != (s_idx // (size // 2)))
        cross = bdmm(_each(lambda a: jnp.where(off, a, 0.0), a_strict), inv)
        inv = _each(jnp.add, inv, bdmm(inv, cross))
        size *= 2
    while size <= CHUNK:
        inv = _compact_level(a_strict, inv, size)
        size *= 2
    return inv


def _compact_level(a_strict, inv, size):
    half = size // 2
    heads = GROUP // CHUNK
    blocks = CHUNK // size
    lane_starts = [h * CHUNK + b * size for h in range(heads) for b in range(blocks)]
    row_starts = [b * size for b in range(blocks)]
    rows_first = lambda x: jnp.concatenate([x[r:r + half] for r in row_starts], axis=0)
    rows_second = lambda x: jnp.concatenate([x[r + half:r + size] for r in row_starts], axis=0)
    lanes_first = lambda x: jnp.concatenate([x[:, c:c + half] for c in lane_starts], axis=1)
    lanes_second = lambda x: jnp.concatenate([x[:, c + half:c + size] for c in lane_starts], axis=1)
    n_rows, n_lanes = CHUNK // 2, GROUP // 2
    own = (lax.broadcasted_iota(jnp.int32, (n_rows, n_lanes), 0) // half
           == (lax.broadcasted_iota(jnp.int32, (n_rows, n_lanes), 1) // half) % blocks)
    diag = (lax.broadcasted_iota(jnp.int32, (n_lanes, n_lanes), 0) // half
            == lax.broadcasted_iota(jnp.int32, (n_lanes, n_lanes), 1) // half)
    bd_half = lambda y: jnp.where(diag, jnp.concatenate([y] * heads, axis=0), 0.0)
    a21 = _each(lambda a: jnp.where(own, lanes_first(rows_second(a)), 0.0), a_strict)
    t11 = _each(lambda t: lanes_first(rows_first(t)), inv)
    t22 = _each(lambda t: lanes_second(rows_second(t)), inv)
    a21_t11 = _each(lambda x, y: _mm(x, bd_half(y)), a21, t11)
    new = _each(lambda x, y: _mm(x, bd_half(y)), t22, a21_t11)

    def placed(x):
        zero_lanes = jnp.zeros((n_rows, half), F32)
        wide = jnp.concatenate(sum(([x[:, i * half:(i + 1) * half], zero_lanes] for i in range(heads * blocks)), []),
                               axis=1)
        zero_rows = jnp.zeros((half, GROUP), F32)
        return jnp.concatenate(sum(([zero_rows, wide[b * half:(b + 1) * half]] for b in range(blocks)), []), axis=0)

    return _each(lambda t, x: t + placed(x), inv, new)


def _chunk_local(r, lw, k, v, kk, kka, tri, t_idx, s_idx, bd_mask):
    bd = lambda y: _block_diag(y, bd_mask)
    bdmm = lambda xs, ys: _each(lambda x, y: _mm(x, bd(y)), xs, ys)
    bdmm2 = lambda xs, ys, zs: _each(lambda x, y, z: _mm(x, jnp.concatenate([bd(y), bd(z)], axis=1)), xs, ys, zs)
    left = lambda xs: _each(lambda x: x[:, :GROUP], xs)
    right = lambda xs: _each(lambda x: x[:, GROUP:], xs)
    top = lambda xs: _each(lambda x: x[:CHUNK], xs)
    bottom = lambda xs: _each(lambda x: x[CHUNK:], xs)

    cum = _each(lambda x: _mm_exact_lhs(tri, x, 3), lw)
    cum_last = _each(lambda c: c[CHUNK - 1:CHUNK, :], cum)
    p_incl = _each(jnp.exp, cum)
    p_excl = _each(lambda c, x: jnp.exp(c - x), cum, lw)
    p_inv = _each(lambda c: jnp.exp(-c), cum)
    to_end = _each(lambda cl, c: jnp.exp(cl - c), cum_last, cum)
    a_t = _each(lambda x, p: -x * p, kk, p_excl)
    r_t = _each(jnp.multiply, r, p_incl)
    b_t = _each(jnp.multiply, kka, p_inv)
    k_t = _each(jnp.multiply, k, p_inv)
    b_end = _each(jnp.multiply, kka, to_end)
    k_end = _each(jnp.multiply, k, to_end)

    ar = _each(lambda a, b: jnp.concatenate([a, b], axis=0), a_t, r_t)
    row2 = lax.broadcasted_iota(jnp.int32, (2 * CHUNK, 2 * GROUP), 0)
    col2 = lax.broadcasted_iota(jnp.int32, (2 * CHUNK, 2 * GROUP), 1) % CHUNK
    causal = col2 < (row2 % CHUNK) + (row2 // CHUNK)
    scores = _each(lambda x, y, z: jnp.where(causal, _mm_nt(x, jnp.concatenate([bd(y), bd(z)], axis=0)), 0.0),
                   ar, b_t, k_t)
    a_ab = top(left(scores))
    a_rb = bottom(left(scores))

    inv = _unit_lower_inverse(a_ab, t_idx, s_idx, bd)
    kv = bdmm(right(scores), v)
    wu = bdmm2(inv, a_t, top(kv))
    ro = bdmm2(a_rb, left(wu), right(wu))
    r_hat = _each(jnp.add, r_t, left(ro))
    o_loc = _each(jnp.add, right(ro), bottom(kv))

    wu_b = _each(lambda x, b: _mm(x.T, b), wu, b_end)
    trans = _each(lambda x: jnp.where(bd_mask, x[:GROUP], 0.0), wu_b)
    d_bd = _each(lambda x, y, z: jnp.where(bd_mask, x[GROUP:] + _mm(y.T, z), 0.0), wu_b, v, k_end)
    d_ls = _each(lambda d: sum(d[h * CHUNK:(h + 1) * CHUNK] for h in range(1, GROUP // CHUNK)) + d[:CHUNK], d_bd)
    p_end = _each(jnp.exp, cum_last)
    return r_hat, o_loc, trans, d_ls, p_end


def _rwkv_kernel(r_ref, lw_ref, k_ref, v_ref, kk_ref, kka_ref, wg_ref, wu_ref, wd_ref,
                 o_ref, wg_bf_ref, wu_bf_ref, wd_bf_ref, state_ref, *, n_sub):
    @pl.when(pl.program_id(1) == 0)
    def _():
        state_ref[...] = jnp.zeros_like(state_ref)

    wg_bf_ref[...] = _bf(wg_ref[...])
    wu_bf_ref[...] = _bf(wu_ref[...])
    wd_bf_ref[...] = _bf(wd_ref[...])

    t_idx = lax.broadcasted_iota(jnp.int32, (CHUNK, GROUP), 0)
    s_idx = lax.broadcasted_iota(jnp.int32, (CHUNK, GROUP), 1) % CHUNK
    bd_mask = (lax.broadcasted_iota(jnp.int32, (GROUP, GROUP), 0) // CHUNK
               == lax.broadcasted_iota(jnp.int32, (GROUP, GROUP), 1) // CHUNK)
    chunk_row = lax.broadcasted_iota(jnp.int32, (CHUNK, CHUNK), 0)
    chunk_col = lax.broadcasted_iota(jnp.int32, (CHUNK, CHUNK), 1)
    n_group = D_RWKV // GROUP
    where = [(slice(c * CHUNK, (c + 1) * CHUNK), slice(g * GROUP, (g + 1) * GROUP))
             for c in range(n_sub) for g in range(n_group)]
    load = lambda ref: [ref[0, rows, lanes].astype(F32) for rows, lanes in where]
    r_hat, o_loc, trans, d_ls, p_end = _chunk_local(
        load(r_ref), load(lw_ref), load(k_ref), load(v_ref), load(kk_ref), load(kka_ref),
        (chunk_row >= chunk_col).astype(BF16), t_idx, s_idx, bd_mask)

    state = [state_ref[g] for g in range(n_group)]
    for c in range(n_sub):
        chains = range(c * n_group, (c + 1) * n_group)
        out = [_mm_nt(r_hat[i], _block_diag(state[g], bd_mask)) + o_loc[i] for g, i in enumerate(chains)]
        for g, i in enumerate(chains):
            rows, lanes = where[i]
            o_ref[0, rows, lanes] = _bf(out[g])
        state = [state[g] * p_end[i] + _mm(state[g], trans[i]) + d_ls[i] for g, i in enumerate(chains)]
    for g in range(n_group):
        state_ref[g] = state[g]


def _rwkv_call(r, lw, k, v, kk, kka, expert_weights, tb):
    bsz, seq, _ = r.shape
    n_t = seq // tb
    per_step = N_EXPERTS // (bsz * n_t)
    assert per_step * bsz * n_t == N_EXPERTS, "grid steps must divide the expert count"
    spec = pl.BlockSpec((1, tb, D_RWKV), lambda b, t: (b, t, 0))
    w_specs = [pl.BlockSpec((per_step,) + w.shape[1:], lambda b, t: (b * n_t + t, 0, 0)) for w in expert_weights]
    return pl.pallas_call(
        functools.partial(_rwkv_kernel, n_sub=tb // CHUNK),
        grid=(bsz, n_t),
        in_specs=[spec] * 6 + w_specs,
        out_specs=[spec] + w_specs,
        out_shape=[jax.ShapeDtypeStruct((bsz, seq, D_RWKV), BF16)]
        + [jax.ShapeDtypeStruct(w.shape, BF16) for w in expert_weights],
        scratch_shapes=[pltpu.VMEM((D_RWKV // GROUP, HEAD, GROUP), F32)],
        compiler_params=pltpu.CompilerParams(
            dimension_semantics=("arbitrary", "arbitrary"), vmem_limit_bytes=VMEM_LIMIT),
        name="rwkv_chunk",
    )(r, lw, k, v, kk, kka, *expert_weights)


def _route(logits):
    lane_i = lax.broadcasted_iota(jnp.int32, logits.shape, 1)
    lane = lane_i.astype(F32)
    lane_group = (lane_i // EXPERTS_PER_GROUP).astype(F32)
    neg = -jnp.inf
    big = float(ROUTER_LANES)
    is_group = (lane_i >= N_EXPERTS) & (lane_i < N_EXPERTS + N_GROUPS)
    gl = jnp.where(is_group, logits, neg)
    ge = jnp.exp(gl - jnp.max(gl, axis=-1, keepdims=True))
    gprob = ge / jnp.sum(ge, axis=-1, keepdims=True)
    g_top = jnp.max(gprob, axis=-1, keepdims=True)
    g_idx = jnp.min(jnp.where(is_group & (gprob == g_top), lane - N_EXPERTS, big), axis=-1, keepdims=True)

    in_group = (lane_i < N_EXPERTS) & (lane_group == g_idx)
    el = jnp.where(in_group, logits, neg)
    top1 = jnp.max(el, axis=-1, keepdims=True)
    idx1 = jnp.min(jnp.where(in_group & (el == top1), lane, big), axis=-1, keepdims=True)
    el2 = jnp.where(lane == idx1, neg, el)
    top2 = jnp.max(el2, axis=-1, keepdims=True)
    idx2 = jnp.min(jnp.where(in_group & (lane != idx1) & (el2 == top2), lane, big), axis=-1, keepdims=True)
    e2 = jnp.exp(top2 - top1)
    denom = 1.0 + e2
    return idx1, idx2, g_top * (1.0 / denom), g_top * (e2 / denom)


def _pack_bf16_pairs(x):
    bits = lax.bitcast_convert_type(_bf(x).astype(F32), jnp.uint32)
    k = x.shape[1] // 2
    return lax.bitcast_convert_type((bits[:, :k] >> 16) | bits[:, k:], jnp.int32)


def _unpack_bf16_pairs(words):
    bits = lax.bitcast_convert_type(words, jnp.uint32)
    lo = lax.bitcast_convert_type(bits << 16, F32)
    hi = lax.bitcast_convert_type(bits & jnp.uint32(0xFFFF0000), F32)
    return _bf(jnp.concatenate([lo, hi], axis=1))


LANE_IDX1, LANE_IDX2, LANE_RANK1, LANE_RANK2, LANE_GATE1, LANE_GATE2 = range(6)


def _out_kernel(o_ref, bonus_ref, gate_ref, yconv_ref, x_ref, lnw_ref, lnb_ref,
                wout_ref, gffn_ref, rw_ref, rb_ref, h_ref, u_ref, route_ref, route_t_ref, count_ref,
                seen_ref, wbf_ref, *, n_split):
    @pl.when(pl.program_id(0) == 0)
    def _():
        seen_ref[...] = jnp.zeros_like(seen_ref)
        wbf_ref[...] = _bf(wout_ref[...])

    sub = o_ref.shape[0] // n_split
    parts = [slice(s * sub, (s + 1) * sub) for s in range(n_split)]
    read = lambda ref: [ref[p, :] for p in parts]
    ones_quad = _head_ones()
    inv_n = 1.0 / HEAD
    o = _each(lambda v: v.astype(F32), read(o_ref))
    mean = _each(lambda v: _head_sum(v, ones_quad) * inv_n, o)
    cen = _each(jnp.subtract, o, mean)
    var = _each(lambda c: _head_sum(c * c, ones_quad) * inv_n, cen)
    on = _each(lambda c, v: c * lax.rsqrt(v + LN_X_EPS) * lnw_ref[...] + lnb_ref[...], cen, var)
    y_rwkv = _each(lambda a, b, g: (a + b) * g, on, read(bonus_ref), read(gate_ref))
    mixed = _each(lambda yc, yr: (jnp.dot(yc, wbf_ref[:D_CONV, :], preferred_element_type=F32)
                                  + jnp.dot(_bf(yr), wbf_ref[D_CONV:, :], preferred_element_type=F32)),
                  read(yconv_ref), y_rwkv)
    h = _each(jnp.add, read(x_ref), mixed)
    u = _each(lambda v: _rms_norm(v, gffn_ref[...]), h)
    u_hi = _each(_bf, u)
    u_lo = _each(lambda a, b: _bf(a - b.astype(F32)), u, u_hi)
    by_hi = _each(lambda a: jnp.dot(a, rw_ref[...], preferred_element_type=F32), u_hi)
    by_lo = _each(lambda a: jnp.dot(a, rw_ref[:, :ROUTER_LANES], preferred_element_type=F32), u_lo)
    logits = _each(lambda a, b: a[:, :ROUTER_LANES] + a[:, ROUTER_LANES:] + b + rb_ref[...], by_hi, by_lo)
    routed = _each(_route, logits)

    lane_i = lax.broadcasted_iota(jnp.int32, (sub, ROUTER_LANES), 1)
    lane = lane_i.astype(F32)
    hit1 = _each(lambda rt: lane == rt[0], routed)
    hit2 = _each(lambda rt: lane == rt[1], routed)
    both = _each(lambda a, b: a.astype(F32) + b.astype(F32), hit1, hit2)
    earlier = (lax.broadcasted_iota(jnp.int32, (sub, sub), 0) > lax.broadcasted_iota(jnp.int32, (sub, sub), 1))
    inside = _each(lambda b: jnp.dot(earlier.astype(BF16), _bf(b), preferred_element_type=F32), both)
    seen = seen_ref[...]
    for s, p in enumerate(parts):
        idx1, idx2, gate1, gate2 = routed[s]
        before = inside[s] + seen
        rank1 = jnp.sum(jnp.where(hit1[s], before, 0.0), axis=-1, keepdims=True)
        rank2 = jnp.sum(jnp.where(hit2[s], before, 0.0), axis=-1, keepdims=True)
        seen = seen + jnp.sum(both[s], axis=0, keepdims=True)
        route = jnp.zeros((sub, ROUTER_LANES), F32)
        for lane_id, col in ((LANE_IDX1, idx1), (LANE_IDX2, idx2), (LANE_RANK1, rank1),
                             (LANE_RANK2, rank2), (LANE_GATE1, gate1), (LANE_GATE2, gate2)):
            route = jnp.where(lane_i == lane_id, col, route)
        route_ref[p, :] = route
        route_t_ref[:, p] = route.T[:SUBLANES, :]
        h_ref[p, :] = _bf(h[s])
        u_ref[p, :] = _pack_bf16_pairs(u[s])
    seen_ref[...] = seen
    count_ref[...] = jnp.broadcast_to(seen, count_ref.shape)


def _out_call(o, bonus, gate, yconv, x, ln_w, ln_b, w_out, g_ffn, router_w, router_b, tm, n_split):
    n_tok = x.shape[0]
    row = lambda width: pl.BlockSpec((tm, width), lambda i: (i, 0))
    full = lambda arr: pl.BlockSpec(arr.shape, lambda i: (0,) * arr.ndim, pipeline_mode=pl.Buffered(1))
    params = (ln_w, ln_b, w_out, g_ffn, router_w, router_b)
    return pl.pallas_call(
        functools.partial(_out_kernel, n_split=n_split),
        grid=(n_tok // tm,),
        in_specs=[row(D_RWKV)] * 4 + [row(D_MODEL)] + [full(p) for p in params],
        out_specs=[row(D_MODEL), row(D_MODEL // 2), row(ROUTER_LANES),
                   pl.BlockSpec((SUBLANES, tm), lambda i: (0, i)),
                   pl.BlockSpec((SUBLANES, ROUTER_LANES), lambda i: (0, 0))],
        out_shape=[jax.ShapeDtypeStruct((n_tok, D_MODEL), BF16),
                   jax.ShapeDtypeStruct((n_tok, D_MODEL // 2), jnp.int32),
                   jax.ShapeDtypeStruct((n_tok, ROUTER_LANES), F32),
                   jax.ShapeDtypeStruct((SUBLANES, n_tok), F32),
                   jax.ShapeDtypeStruct((SUBLANES, ROUTER_LANES), F32)],
        scratch_shapes=[pltpu.VMEM((1, ROUTER_LANES), F32), pltpu.VMEM(w_out.shape, BF16)],
        compiler_params=pltpu.CompilerParams(
            dimension_semantics=("arbitrary",), vmem_limit_bytes=VMEM_LIMIT),
        name="out_proj_route",
    )(o, bonus, gate, yconv, x, *params)


SC_CORES = 2
SC_SUBCORES = 16
SC_ROWS = 64


def _sc_mesh():
    return plsc.VectorSubcoreMesh(core_axis_name="c", subcore_axis_name="s",
                                  num_cores=SC_CORES, num_subcores=SC_SUBCORES)


def _sc_worker():
    return lax.axis_index("s") * SC_CORES + lax.axis_index("c")


def _sc_gather(table, idx):
    n_rows = idx.shape[0]
    width = table.shape[1]
    n_chunks = n_rows // (SC_CORES * SC_SUBCORES * SC_ROWS)

    def body(table_hbm, idx_hbm, out_hbm, idx_v, rows_v, gather_sem, write_sem):
        first = _sc_worker() * n_chunks
        pltpu.sync_copy(idx_hbm.at[pl.ds(first, n_chunks)], idx_v)
        gather = lambda j: pltpu.async_copy(table_hbm.at[idx_v.at[j]], rows_v.at[j % 2], gather_sem.at[j % 2])
        gathers = [gather(0)]
        writes = []
        for j in range(n_chunks):
            gathers[j].wait()
            if j + 1 < n_chunks:
                if j >= 1:
                    writes[j - 1].wait()
                gathers.append(gather(j + 1))
            dst = out_hbm.at[pl.ds(pl.multiple_of((first + j) * SC_ROWS, SC_ROWS), SC_ROWS)]
            writes.append(pltpu.async_copy(rows_v.at[j % 2], dst, write_sem.at[j % 2]))
        for j in range(max(n_chunks - 2, 0), n_chunks):
            writes[j].wait()

    return pl.kernel(
        body,
        out_type=jax.ShapeDtypeStruct((n_rows, width), table.dtype),
        mesh=_sc_mesh(),
        scratch_types=[pltpu.VMEM((n_chunks, SC_ROWS), jnp.int32), pltpu.VMEM((2, SC_ROWS, width), table.dtype),
                       pltpu.SemaphoreType.DMA((2,)), pltpu.SemaphoreType.DMA((2,))],
        name="sc_row_gather",
    )(table, idx.reshape(n_rows // SC_ROWS, SC_ROWS))


def _sc_scatter(rows, pos, n_out):
    n_rows, width = rows.shape
    n_slots = pos.shape[0] // n_rows
    slot_chunks = n_rows // SC_ROWS
    n_chunks = slot_chunks // (SC_CORES * SC_SUBCORES)

    def body(rows_hbm, pos_hbm, out_hbm, idx_v, rows_v, read_sem, scatter_sem):
        first = _sc_worker() * n_chunks
        for s in range(n_slots):
            pltpu.sync_copy(pos_hbm.at[pl.ds(s * slot_chunks + first, n_chunks)], idx_v.at[s])
        read = lambda j: pltpu.async_copy(
            rows_hbm.at[pl.ds(pl.multiple_of((first + j) * SC_ROWS, SC_ROWS), SC_ROWS)],
            rows_v.at[j % 2], read_sem.at[j % 2])
        reads = [read(0)]
        scatters = []
        for j in range(n_chunks):
            reads[j].wait()
            if j + 1 < n_chunks:
                if j >= 1:
                    for copy in scatters[j - 1]:
                        copy.wait()
                reads.append(read(j + 1))
            scatters.append([pltpu.async_copy(rows_v.at[j % 2], out_hbm.at[idx_v.at[s, j]], scatter_sem.at[j % 2])
                             for s in range(n_slots)])
        for j in range(max(n_chunks - 2, 0), n_chunks):
            for copy in scatters[j]:
                copy.wait()

    return pl.kernel(
        body,
        out_type=jax.ShapeDtypeStruct((n_out, width), rows.dtype),
        mesh=_sc_mesh(),
        scratch_types=[pltpu.VMEM((n_slots, n_chunks, SC_ROWS), jnp.int32),
                       pltpu.VMEM((2, SC_ROWS, width), rows.dtype),
                       pltpu.SemaphoreType.DMA((2,)), pltpu.SemaphoreType.DMA((2,))],
        name="sc_row_scatter",
    )(rows, pos.reshape(n_slots * slot_chunks, SC_ROWS))


def _expert_kernel(tile_expert_ref, n_valid_ref, x_ref, wg_ref, wu_ref, wd_ref, y_ref):
    del tile_expert_ref

    @pl.when(pl.program_id(0) < n_valid_ref[0])
    def _():
        x = _unpack_bf16_pairs(x_ref[...])
        gate = jnp.dot(x, wg_ref[0], preferred_element_type=F32)
        up = jnp.dot(x, wu_ref[0], preferred_element_type=F32)
        hid = gate * jax.nn.sigmoid(gate) * up
        y_ref[...] = _pack_bf16_pairs(jnp.dot(_bf(hid), wd_ref[0], preferred_element_type=F32))


def _expert_call(tile_expert, n_valid, x_sorted, w_gate, w_up, w_down, tm):
    n_rows = x_sorted.shape[0]
    rows = pl.BlockSpec((tm, D_MODEL // 2), lambda i, te, nv: (jnp.minimum(i, nv[0] - 1), 0))
    by_expert = lambda i, te, nv: (te[i], 0, 0)
    return pl.pallas_call(
        _expert_kernel,
        grid_spec=pltpu.PrefetchScalarGridSpec(
            num_scalar_prefetch=2,
            grid=(n_rows // tm,),
            in_specs=[rows,
                      pl.BlockSpec((1, D_MODEL, D_EXPERT), by_expert),
                      pl.BlockSpec((1, D_MODEL, D_EXPERT), by_expert),
                      pl.BlockSpec((1, D_EXPERT, D_MODEL), by_expert)],
            out_specs=rows),
        out_shape=jax.ShapeDtypeStruct((n_rows, D_MODEL // 2), jnp.int32),
        compiler_params=pltpu.CompilerParams(
            dimension_semantics=("arbitrary",), vmem_limit_bytes=VMEM_LIMIT),
        name="moe_experts",
    )(tile_expert, n_valid, x_sorted, w_gate, w_up, w_down)


def _final_kernel(h_ref, y1_ref, y2_ref, route_ref, gfin_ref, *rest):
    out_ref = rest[-1]
    route = route_ref[...]
    gate1 = route[:, LANE_GATE1:LANE_GATE1 + 1]
    gate2 = route[:, LANE_GATE2:LANE_GATE2 + 1]
    moe = gate1 * _unpack_bf16_pairs(y1_ref[...]).astype(F32) + gate2 * _unpack_bf16_pairs(y2_ref[...]).astype(F32)
    out_ref[...] = _rms_norm(h_ref[...] + moe, gfin_ref[...])


def _final_call(h, y_pairs, route, g_final, earlier, first_token, tm):
    n_tok = h.shape[0]
    n_blocks = y_pairs.shape[0] // 2 // tm
    assert n_blocks * tm * 2 == y_pairs.shape[0] and first_token % tm == 0, "chunks must be whole row blocks"
    first = first_token // tm
    in_specs = [pl.BlockSpec((tm, D_MODEL), lambda i: (first + i, 0)),
                pl.BlockSpec((tm, D_MODEL // 2), lambda i: (i, 0)),
                pl.BlockSpec((tm, D_MODEL // 2), lambda i: (i + n_blocks, 0)),
                pl.BlockSpec((tm, ROUTER_LANES), lambda i: (first + i, 0)),
                pl.BlockSpec((1, D_MODEL), lambda i: (0, 0))]
    args = [h, y_pairs, y_pairs, route, g_final]
    aliases = {}
    if earlier is not None:
        in_specs.append(pl.BlockSpec(memory_space=pl.ANY))
        args.append(earlier)
        aliases = {len(args) - 1: 0}
    return pl.pallas_call(
        _final_kernel,
        grid=(n_blocks,),
        in_specs=in_specs,
        out_specs=pl.BlockSpec((tm, D_MODEL), lambda i: (first + i, 0)),
        out_shape=jax.ShapeDtypeStruct((n_tok, D_MODEL), F32),
        input_output_aliases=aliases,
        compiler_params=pltpu.CompilerParams(
            dimension_semantics=("arbitrary",), vmem_limit_bytes=VMEM_LIMIT),
        name="moe_combine_norm",
    )(*args)


def _dispatch_plan(route_t, counts, tm):
    n_tok = route_t.shape[1]
    n_tiles = (2 * n_tok) // tm + N_EXPERTS
    counts = counts[0, :N_EXPERTS].astype(jnp.int32)
    tiles_per = (counts + tm - 1) // tm
    tile_end = jnp.cumsum(tiles_per)
    row_start = (tile_end - tiles_per) * tm
    experts = jnp.arange(N_EXPERTS, dtype=jnp.int32)

    def position(idx_lane, rank_lane):
        idx = route_t[idx_lane].astype(jnp.int32)
        start = jnp.sum(jnp.where(idx[None, :] == experts[:, None], row_start[:, None], 0), axis=0)
        return start + route_t[rank_lane].astype(jnp.int32)

    pos = jnp.concatenate([position(LANE_IDX1, LANE_RANK1), position(LANE_IDX2, LANE_RANK2)])
    n_valid = tile_end[-1:]
    tile = jnp.minimum(jnp.arange(n_tiles, dtype=jnp.int32), n_valid - 1)
    tile_expert = jnp.sum((tile_end[None, :] <= tile[:, None]).astype(jnp.int32), axis=1)
    return pos, n_tiles, tile_expert, n_valid


def _block(x, norm_mix_g, w_in, rwkv_mu, conv_w, decay_up, decay_base, aaa_up, aaa_base, gate_up,
           k_k, k_a, r_k, ln_x_w, ln_x_b, w_out, norm_ffn_g, router_group_w, router_group_b,
           router_expert_w, router_expert_b, expert_w_gate, expert_w_up, expert_w_down, norm_final_g,
           *, tm_in, tb_rwkv, tm_out, tm_expert, tm_final):
    bsz, seq, d_model = x.shape
    n_tok = bsz * seq
    row = lambda p: p.reshape(1, -1)

    yconv, r, lw, k2, v, kk, kka, gate, bonus = _in_call(
        x, row(norm_mix_g), w_in, row(rwkv_mu), conv_w, decay_up, row(decay_base), aaa_up, row(aaa_base),
        gate_up, row(k_k), row(k_a), row(r_k), tm_in)
    o, w_gate_bf, w_up_bf, w_down_bf = _rwkv_call(r, lw, k2, v, kk, kka,
                                                  (expert_w_gate, expert_w_up, expert_w_down), tb_rwkv)

    pad = ROUTER_LANES - N_EXPERTS - N_GROUPS
    router_w = jnp.concatenate([router_expert_w, router_group_w, jnp.zeros((d_model, pad), F32)], axis=1)
    router_hi = _bf(router_w)
    router_split = jnp.concatenate([router_hi, _bf(router_w - router_hi.astype(F32))], axis=1)
    router_b = jnp.concatenate([row(router_expert_b), row(router_group_b), jnp.zeros((1, pad), F32)], axis=1)
    flat = lambda t: t.reshape(n_tok, t.shape[-1])
    h, u_pairs, route, route_t, counts = _out_call(
        flat(o), flat(bonus), flat(gate), flat(yconv), flat(x),
        row(ln_x_w), row(ln_x_b), w_out, row(norm_ffn_g), router_split, router_b, tm_out, OUT_SPLIT)

    pos, n_tiles, tile_expert, n_valid = _dispatch_plan(route_t, counts, tm_expert)
    x_sorted = _sc_scatter(u_pairs, pos, n_tiles * tm_expert)
    y_sorted = _expert_call(tile_expert, n_valid, x_sorted, w_gate_bf, w_up_bf, w_down_bf, tm_expert)
    out, lo = None, 0
    for fraction in COMBINE_SPLIT:
        size = n_tok // fraction
        pos_c = jnp.concatenate([pos[lo:lo + size], pos[n_tok + lo:n_tok + lo + size]])
        out = _final_call(h, _sc_gather(y_sorted, pos_c), route, row(norm_final_g), out, lo, tm_final)
        lo += size
    assert lo == n_tok, "COMBINE_SPLIT must cover all tokens"
    return out.reshape(bsz, seq, d_model)


def kernel(x, norm_mix_g, w_in, rwkv_mu, conv_w, decay_up, decay_base, aaa_up, aaa_base, gate_up, k_k, k_a, r_k, ln_x_w, ln_x_b, w_out, norm_ffn_g, router_group_w, router_group_b, router_expert_w, router_expert_b, expert_w_gate, expert_w_up, expert_w_down, norm_final_g):
    return _block(x, norm_mix_g, w_in[0], rwkv_mu, conv_w[0], decay_up[0], decay_base,
                  aaa_up[0], aaa_base, gate_up[0], k_k, k_a, r_k, ln_x_w,
                  ln_x_b, w_out[0], norm_ffn_g, router_group_w[0], router_group_b,
                  router_expert_w[0], router_expert_b, expert_w_gate[0], expert_w_up[0],
                  expert_w_down[0], norm_final_g,
                  tm_in=512, tb_rwkv=1024, tm_out=1024, tm_expert=512, tm_final=1024)
```

```python
import functools

import jax
import jax.numpy as jnp
from jax import lax
from jax.experimental import pallas as pl
from jax.experimental.pallas import tpu as pltpu
from jax.experimental.pallas import tpu_sc as plsc

F32 = jnp.float32
BF16 = jnp.bfloat16

D_MODEL = 1024
D_CONV = 512
CONV_WIDTH = 3
N_HEADS = 8
HEAD = 64
D_RWKV = N_HEADS * HEAD
LORA_WA = 128
GATE_LORA = 128
D_RWKV_PROJ = 3 * D_RWKV + LORA_WA + GATE_LORA
D_IN = 3 * D_CONV + D_RWKV_PROJ
N_GROUPS = 4
EXPERTS_PER_GROUP = 8
N_EXPERTS = N_GROUPS * EXPERTS_PER_GROUP
D_EXPERT = D_MODEL // 4
RMS_EPS = 1e-6
LN_X_EPS = 64e-5
L2_EPS = 1e-12

SUBLANES = 8
CHUNK = 64
QUAD = 4 * HEAD
GROUP = 2 * HEAD
INV_BASE = 8
COMPACT_FROM = 64
OUT_SPLIT = 8
COMBINE_SPLIT = (4, 4, 4, 4)
ROUTER_LANES = 128

VMEM_LIMIT = 56 * 1024 * 1024


def _bf(x):
    return x.astype(BF16)


def _mm(a, b):
    return jnp.dot(_bf(a), _bf(b), preferred_element_type=F32)


def _mm_nt(a, b):
    return lax.dot_general(_bf(a), _bf(b), (((1,), (1,)), ((), ())), preferred_element_type=F32)


def _mm_exact_lhs(lhs_bf16, x, passes):
    acc = None
    rem = x
    for _ in range(passes):
        piece = _bf(rem)
        part = jnp.dot(lhs_bf16, piece, preferred_element_type=F32)
        acc = part if acc is None else acc + part
        rem = rem - piece.astype(F32)
    return acc


def _head_ones():
    return (lax.broadcasted_iota(jnp.int32, (QUAD, QUAD), 0) // HEAD
            == lax.broadcasted_iota(jnp.int32, (QUAD, QUAD), 1) // HEAD).astype(BF16)


def _head_sum(x, ones_quad):
    xb = _bf(x)
    return jnp.concatenate(
        [jnp.dot(xb[:, q * QUAD:(q + 1) * QUAD], ones_quad, preferred_element_type=F32)
         for q in range(x.shape[1] // QUAD)], axis=1)


def _rms_norm(x, g):
    return x * lax.rsqrt(jnp.mean(x * x, axis=-1, keepdims=True) + RMS_EPS) * g


def _shift_rows(cur, prev_rows, k):
    rolled = pltpu.roll(cur, k, 0)
    prev_rolled = pltpu.roll(prev_rows, k, 0)
    n = cur.shape[0]
    head = jnp.concatenate([prev_rolled, rolled[SUBLANES:]], axis=0) if n > SUBLANES else prev_rolled
    row = lax.broadcasted_iota(jnp.int32, cur.shape, 0)
    return jnp.where(row < k, head, rolled)


def _in_kernel(x_ref, g_ref, w_ref, mu_ref, convw_ref, dup_ref, dbase_ref, aup_ref, abase_ref,
               gup_ref, kk_ref, ka_ref, rk_ref,
               yconv_ref, r_ref, lw_ref, k_ref, v_ref, kkn_ref, kka_ref, gate_ref, bonus_ref,
               carry_ref, wbf_ref):
    @pl.when((pl.program_id(0) == 0) & (pl.program_id(1) == 0))
    def _():
        wbf_ref[...] = _bf(w_ref[...])

    @pl.when(pl.program_id(1) == 0)
    def _():
        carry_ref[...] = jnp.zeros_like(carry_ref)

    u = _bf(_rms_norm(x_ref[0], g_ref[...]))
    n_conv = 3 * D_CONV
    rk0 = n_conv
    bounds = dict(conv=(0, n_conv), lora=(rk0 + 3 * D_RWKV, D_IN), k=(rk0 + D_RWKV, rk0 + 2 * D_RWKV),
                  r=(rk0, rk0 + D_RWKV), v=(rk0 + 2 * D_RWKV, rk0 + 3 * D_RWKV))
    z = {name: jnp.dot(u, wbf_ref[:, lo:hi], preferred_element_type=F32) for name, (lo, hi) in bounds.items()}

    def lerp(name):
        lo, hi = bounds[name]
        cur = z[name]
        prev = carry_ref[:, D_CONV + lo - n_conv:D_CONV + hi - n_conv]
        mixed = cur + (_shift_rows(cur, prev, 1) - cur) * mu_ref[:, lo - n_conv:hi - n_conv]
        carry_ref[:, D_CONV + lo - n_conv:D_CONV + hi - n_conv] = cur[-SUBLANES:]
        return mixed

    b_gate = z["conv"][:, :D_CONV]
    ch = z["conv"][:, D_CONV:2 * D_CONV] * z["conv"][:, 2 * D_CONV:]
    prev_ch = carry_ref[:, :D_CONV]
    conv = convw_ref[CONV_WIDTH - 1:CONV_WIDTH, :] * ch
    for delay in range(1, CONV_WIDTH):
        tap = CONV_WIDTH - 1 - delay
        conv = conv + convw_ref[tap:tap + 1, :] * _shift_rows(ch, prev_ch, delay)
    yconv_ref[0] = _bf(b_gate * conv)
    carry_ref[:, :D_CONV] = ch[-SUBLANES:]

    lora = lerp("lora")
    wa_lo = lora[:, :LORA_WA]
    g_lo = lora[:, LORA_WA:]
    no_rows = jnp.zeros_like(dup_ref[...])
    decay_up = _bf(jnp.concatenate([dup_ref[...], no_rows], axis=0))
    rate_up = _bf(jnp.concatenate([no_rows, aup_ref[...]], axis=0))
    dec_in = -(dbase_ref[...] + jnp.dot(_bf(jnp.tanh(wa_lo)), decay_up, preferred_element_type=F32))
    softplus = jnp.maximum(dec_in, 0.0) + jnp.log(1.0 + jnp.exp(-jnp.abs(dec_in)))
    w = -softplus - 0.5
    lw_ref[0] = -jnp.exp(w)
    a = jax.nn.sigmoid(abase_ref[...] + jnp.dot(_bf(wa_lo), rate_up, preferred_element_type=F32))
    gate_ref[0] = _bf(jnp.dot(_bf(jax.nn.sigmoid(g_lo)), _bf(gup_ref[...]), preferred_element_type=F32))

    ones_quad = _head_ones()
    k = lerp("k")
    kk = k * kk_ref[...]
    norm = jnp.sqrt(_head_sum(kk * kk, ones_quad))
    kk = kk / jnp.maximum(norm, L2_EPS)
    k2 = k * (1.0 + (a - 1.0) * ka_ref[...])
    k_ref[0] = _bf(k2)
    kkn_ref[0] = _bf(kk)
    kka_ref[0] = _bf(kk * a)
    r = lerp("r")
    r_ref[0] = _bf(r)
    v = lerp("v")
    v_ref[0] = _bf(v)
    bonus_ref[0] = _bf(_head_sum(r * k2 * rk_ref[...], ones_quad) * v)


def _in_call(x, g, w_in, mu, conv_w, dup, dbase, aup, abase, gup, k_k, k_a, r_k, tm):
    bsz, seq, _ = x.shape
    full = lambda arr: pl.BlockSpec(arr.shape, lambda b, t: (0,) * arr.ndim, pipeline_mode=pl.Buffered(1))
    out_spec = pl.BlockSpec((1, tm, D_RWKV), lambda b, t: (b, t, 0))
    out_dtypes = (BF16, BF16, F32, BF16, BF16, BF16, BF16, BF16, BF16)
    params = (g, w_in, mu, conv_w, dup, dbase, aup, abase, gup, k_k, k_a, r_k)
    return pl.pallas_call(
        _in_kernel,
        grid=(bsz, seq // tm),
        in_specs=[pl.BlockSpec((1, tm, D_MODEL), lambda b, t: (b, t, 0))] + [full(p) for p in params],
        out_specs=[out_spec] * 9,
        out_shape=[jax.ShapeDtypeStruct((bsz, seq, D_RWKV), dt) for dt in out_dtypes],
        scratch_shapes=[pltpu.VMEM((SUBLANES, D_CONV + D_RWKV_PROJ), F32), pltpu.VMEM(w_in.shape, BF16)],
        compiler_params=pltpu.CompilerParams(
            dimension_semantics=("arbitrary", "arbitrary"), vmem_limit_bytes=VMEM_LIMIT),
        name="in_proj",
    )(x, *params)


def _block_diag(y, bd_mask):
    return jnp.where(bd_mask, jnp.concatenate([y] * (GROUP // CHUNK), axis=0), 0.0)


def _each(fn, *lists):
    return [fn(*args) for args in zip(*lists)]


def _unit_lower_inverse(a_strict, t_idx, s_idx, bd):
    bdmm = lambda xs, ys: _each(lambda x, y: _mm(x, bd(y)), xs, ys)
    eye = (t_idx == s_idx).astype(F32)
    same8 = (t_idx // INV_BASE) == (s_idx // INV_BASE)
    a8 = _each(lambda a: jnp.where(same8, a, 0.0), a_strict)
    a8_2 = bdmm(a8, a8)
    a8_34 = bdmm(_each(lambda a, b: jnp.concatenate([a, b], axis=0), a8, a8_2), a8_2)
    inv = _each(lambda a, b, c: eye + a + b + c[:CHUNK], a8, a8_2, a8_34)
    inv = _each(jnp.add, inv, bdmm(inv, _each(lambda c: c[CHUNK:], a8_34)))
    size = 2 * INV_BASE
    while size < COMPACT_FROM:
        off = ((t_idx // size) == (s_idx // size)) & ((t_idx // (size // 2)) != (s_idx // (size // 2)))
        cross = bdmm(_each(lambda a: jnp.where(off, a, 0.0), a_strict), inv)
        inv = _each(jnp.add, inv, bdmm(inv, cross))
        size *= 2
    while size <= CHUNK:
        inv = _compact_level(a_strict, inv, size)
        size *= 2
    return inv


def _compact_level(a_strict, inv, size):
    half = size // 2
    heads = GROUP // CHUNK
    blocks = CHUNK // size
    lane_starts = [h * CHUNK + b * size for h in range(heads) for b in range(blocks)]
    row_starts = [b * size for b in range(blocks)]
    rows_first = lambda x: jnp.concatenate([x[r:r + half] for r in row_starts], axis=0)
    rows_second = lambda x: jnp.concatenate([x[r + half:r + size] for r in row_starts], axis=0)
    lanes_first = lambda x: jnp.concatenate([x[:, c:c + half] for c in lane_starts], axis=1)
    lanes_second = lambda x: jnp.concatenate([x[:, c + half:c + size] for c in lane_starts], axis=1)
    n_rows, n_lanes = CHUNK // 2, GROUP // 2
    own = (lax.broadcasted_iota(jnp.int32, (n_rows, n_lanes), 0) // half
           == (lax.broadcasted_iota(jnp.int32, (n_rows, n_lanes), 1) // half) % blocks)
    diag = (lax.broadcasted_iota(jnp.int32, (n_lanes, n_lanes), 0) // half
            == lax.broadcasted_iota(jnp.int32, (n_lanes, n_lanes), 1) // half)
    bd_half = lambda y: jnp.where(diag, jnp.concatenate([y] * heads, axis=0), 0.0)
    a21 = _each(lambda a: jnp.where(own, lanes_first(rows_second(a)), 0.0), a_strict)
    t11 = _each(lambda t: lanes_first(rows_first(t)), inv)
    t22 = _each(lambda t: lanes_second(rows_second(t)), inv)
    a21_t11 = _each(lambda x, y: _mm(x, bd_half(y)), a21, t11)
    new = _each(lambda x, y: _mm(x, bd_half(y)), t22, a21_t11)

    def placed(x):
        zero_lanes = jnp.zeros((n_rows, half), F32)
        wide = jnp.concatenate(sum(([x[:, i * half:(i + 1) * half], zero_lanes] for i in range(heads * blocks)), []),
                               axis=1)
        zero_rows = jnp.zeros((half, GROUP), F32)
        return jnp.concatenate(sum(([zero_rows, wide[b * half:(b + 1) * half]] for b in range(blocks)), []), axis=0)

    return _each(lambda t, x: t + placed(x), inv, new)


def _chunk_local(r, lw, k, v, kk, kka, tri, t_idx, s_idx, bd_mask):
    bd = lambda y: _block_diag(y, bd_mask)
    bdmm = lambda xs, ys: _each(lambda x, y: _mm(x, bd(y)), xs, ys)
    bdmm2 = lambda xs, ys, zs: _each(lambda x, y, z: _mm(x, jnp.concatenate([bd(y), bd(z)], axis=1)), xs, ys, zs)
    left = lambda xs: _each(lambda x: x[:, :GROUP], xs)
    right = lambda xs: _each(lambda x: x[:, GROUP:], xs)
    top = lambda xs: _each(lambda x: x[:CHUNK], xs)
    bottom = lambda xs: _each(lambda x: x[CHUNK:], xs)

    cum = _each(lambda x: _mm_exact_lhs(tri, x, 3), lw)
    cum_last = _each(lambda c: c[CHUNK - 1:CHUNK, :], cum)
    p_incl = _each(jnp.exp, cum)
    p_excl = _each(lambda c, x: jnp.exp(c - x), cum, lw)
    p_inv = _each(lambda c: jnp.exp(-c), cum)
    to_end = _each(lambda cl, c: jnp.exp(cl - c), cum_last, cum)
    a_t = _each(lambda x, p: -x * p, kk, p_excl)
    r_t = _each(jnp.multiply, r, p_incl)
    b_t = _each(jnp.multiply, kka, p_inv)
    k_t = _each(jnp.multiply, k, p_inv)
    b_end = _each(jnp.multiply, kka, to_end)
    k_end = _each(jnp.multiply, k, to_end)

    ar = _each(lambda a, b: jnp.concatenate([a, b], axis=0), a_t, r_t)
    row2 = lax.broadcasted_iota(jnp.int32, (2 * CHUNK, 2 * GROUP), 0)
    col2 = lax.broadcasted_iota(jnp.int32, (2 * CHUNK, 2 * GROUP), 1) % CHUNK
    causal = col2 < (row2 % CHUNK) + (row2 // CHUNK)
    scores = _each(lambda x, y, z: jnp.where(causal, _mm_nt(x, jnp.concatenate([bd(y), bd(z)], axis=0)), 0.0),
                   ar, b_t, k_t)
    a_ab = top(left(scores))
    a_rb = bottom(left(scores))

    inv = _unit_lower_inverse(a_ab, t_idx, s_idx, bd)
    kv = bdmm(right(scores), v)
    wu = bdmm2(inv, a_t, top(kv))
    ro = bdmm2(a_rb, left(wu), right(wu))
    r_hat = _each(jnp.add, r_t, left(ro))
    o_loc = _each(jnp.add, right(ro), bottom(kv))

    wu_b = _each(lambda x, b: _mm(x.T, b), wu, b_end)
    trans = _each(lambda x: jnp.where(bd_mask, x[:GROUP], 0.0), wu_b)
    d_bd = _each(lambda x, y, z: jnp.where(bd_mask, x[GROUP:] + _mm(y.T, z), 0.0), wu_b, v, k_end)
    d_ls = _each(lambda d: sum(d[h * CHUNK:(h + 1) * CHUNK] for h in range(1, GROUP // CHUNK)) + d[:CHUNK], d_bd)
    p_end = _each(jnp.exp, cum_last)
    return r_hat, o_loc, trans, d_ls, p_end


def _rwkv_kernel(r_ref, lw_ref, k_ref, v_ref, kk_ref, kka_ref, wg_ref, wu_ref, wd_ref,
                 o_ref, wg_bf_ref, wu_bf_ref, wd_bf_ref, state_ref, *, n_sub):
    @pl.when(pl.program_id(1) == 0)
    def _():
        state_ref[...] = jnp.zeros_like(state_ref)

    wg_bf_ref[...] = _bf(wg_ref[...])
    wu_bf_ref[...] = _bf(wu_ref[...])
    wd_bf_ref[...] = _bf(wd_ref[...])

    t_idx = lax.broadcasted_iota(jnp.int32, (CHUNK, GROUP), 0)
    s_idx = lax.broadcasted_iota(jnp.int32, (CHUNK, GROUP), 1) % CHUNK
    bd_mask = (lax.broadcasted_iota(jnp.int32, (GROUP, GROUP), 0) // CHUNK
               == lax.broadcasted_iota(jnp.int32, (GROUP, GROUP), 1) // CHUNK)
    chunk_row = lax.broadcasted_iota(jnp.int32, (CHUNK, CHUNK), 0)
    chunk_col = lax.broadcasted_iota(jnp.int32, (CHUNK, CHUNK), 1)
    n_group = D_RWKV // GROUP
    where = [(slice(c * CHUNK, (c + 1) * CHUNK), slice(g * GROUP, (g + 1) * GROUP))
             for c in range(n_sub) for g in range(n_group)]
    load = lambda ref: [ref[0, rows, lanes].astype(F32) for rows, lanes in where]
    r_hat, o_loc, trans, d_ls, p_end = _chunk_local(
        load(r_ref), load(lw_ref), load(k_ref), load(v_ref), load(kk_ref), load(kka_ref),
        (chunk_row >= chunk_col).astype(BF16), t_idx, s_idx, bd_mask)

    state = [state_ref[g] for g in range(n_group)]
    for c in range(n_sub):
        chains = range(c * n_group, (c + 1) * n_group)
        out = [_mm_nt(r_hat[i], _block_diag(state[g], bd_mask)) + o_loc[i] for g, i in enumerate(chains)]
        for g, i in enumerate(chains):
            rows, lanes = where[i]
            o_ref[0, rows, lanes] = _bf(out[g])
        state = [state[g] * p_end[i] + _mm(state[g], trans[i]) + d_ls[i] for g, i in enumerate(chains)]
    for g in range(n_group):
        state_ref[g] = state[g]


def _rwkv_call(r, lw, k, v, kk, kka, expert_weights, tb):
    bsz, seq, _ = r.shape
    n_t = seq // tb
    per_step = N_EXPERTS // (bsz * n_t)
    assert per_step * bsz * n_t == N_EXPERTS, "grid steps must divide the expert count"
    spec = pl.BlockSpec((1, tb, D_RWKV), lambda b, t: (b, t, 0))
    w_specs = [pl.BlockSpec((per_step,) + w.shape[1:], lambda b, t: (b * n_t + t, 0, 0)) for w in expert_weights]
    return pl.pallas_call(
        functools.partial(_rwkv_kernel, n_sub=tb // CHUNK),
        grid=(bsz, n_t),
        in_specs=[spec] * 6 + w_specs,
        out_specs=[spec] + w_specs,
        out_shape=[jax.ShapeDtypeStruct((bsz, seq, D_RWKV), BF16)]
        + [jax.ShapeDtypeStruct(w.shape, BF16) for w in expert_weights],
        scratch_shapes=[pltpu.VMEM((D_RWKV // GROUP, HEAD, GROUP), F32)],
        compiler_params=pltpu.CompilerParams(
            dimension_semantics=("arbitrary", "arbitrary"), vmem_limit_bytes=VMEM_LIMIT),
        name="rwkv_chunk",
    )(r, lw, k, v, kk, kka, *expert_weights)


def _route(logits):
    lane_i = lax.broadcasted_iota(jnp.int32, logits.shape, 1)
    lane = lane_i.astype(F32)
    lane_group = (lane_i // EXPERTS_PER_GROUP).astype(F32)
    neg = -jnp.inf
    big = float(ROUTER_LANES)
    is_group = (lane_i >= N_EXPERTS) & (lane_i < N_EXPERTS + N_GROUPS)
    gl = jnp.where(is_group, logits, neg)
    ge = jnp.exp(gl - jnp.max(gl, axis=-1, keepdims=True))
    gprob = ge / jnp.sum(ge, axis=-1, keepdims=True)
    g_top = jnp.max(gprob, axis=-1, keepdims=True)
    g_idx = jnp.min(jnp.where(is_group & (gprob == g_top), lane - N_EXPERTS, big), axis=-1, keepdims=True)

    in_group = (lane_i < N_EXPERTS) & (lane_group == g_idx)
    el = jnp.where(in_group, logits, neg)
    top1 = jnp.max(el, axis=-1, keepdims=True)
    idx1 = jnp.min(jnp.where(in_group & (el == top1), lane, big), axis=-1, keepdims=True)
    el2 = jnp.where(lane == idx1, neg, el)
    top2 = jnp.max(el2, axis=-1, keepdims=True)
    idx2 = jnp.min(jnp.where(in_group & (lane != idx1) & (el2 == top2), lane, big), axis=-1, keepdims=True)
    e2 = jnp.exp(top2 - top1)
    denom = 1.0 + e2
    return idx1, idx2, g_top * (1.0 / denom), g_top * (e2 / denom)


def _pack_bf16_pairs(x):
    bits = lax.bitcast_convert_type(_bf(x).astype(F32), jnp.uint32)
    k = x.shape[1] // 2
    return lax.bitcast_convert_type((bits[:, :k] >> 16) | bits[:, k:], jnp.int32)


def _unpack_bf16_pairs(words):
    bits = lax.bitcast_convert_type(words, jnp.uint32)
    lo = lax.bitcast_convert_type(bits << 16, F32)
    hi = lax.bitcast_convert_type(bits & jnp.uint32(0xFFFF0000), F32)
    return _bf(jnp.concatenate([lo, hi], axis=1))


LANE_IDX1, LANE_IDX2, LANE_RANK1, LANE_RANK2, LANE_GATE1, LANE_GATE2 = range(6)


def _out_kernel(o_ref, bonus_ref, gate_ref, yconv_ref, x_ref, lnw_ref, lnb_ref,
                wout_ref, gffn_ref, rw_ref, rb_ref, h_ref, u_ref, route_ref, route_t_ref, count_ref,
                seen_ref, wbf_ref, *, n_split):
    @pl.when(pl.program_id(0) == 0)
    def _():
        seen_ref[...] = jnp.zeros_like(seen_ref)
        wbf_ref[...] = _bf(wout_ref[...])

    sub = o_ref.shape[0] // n_split
    parts = [slice(s * sub, (s + 1) * sub) for s in range(n_split)]
    read = lambda ref: [ref[p, :] for p in parts]
    ones_quad = _head_ones()
    inv_n = 1.0 / HEAD
    o = _each(lambda v: v.astype(F32), read(o_ref))
    mean = _each(lambda v: _head_sum(v, ones_quad) * inv_n, o)
    cen = _each(jnp.subtract, o, mean)
    var = _each(lambda c: _head_sum(c * c, ones_quad) * inv_n, cen)
    on = _each(lambda c, v: c * lax.rsqrt(v + LN_X_EPS) * lnw_ref[...] + lnb_ref[...], cen, var)
    y_rwkv = _each(lambda a, b, g: (a + b) * g, on, read(bonus_ref), read(gate_ref))
    mixed = _each(lambda yc, yr: (jnp.dot(yc, wbf_ref[:D_CONV, :], preferred_element_type=F32)
                                  + jnp.dot(_bf(yr), wbf_ref[D_CONV:, :], preferred_element_type=F32)),
                  read(yconv_ref), y_rwkv)
    h = _each(jnp.add, read(x_ref), mixed)
    u = _each(lambda v: _rms_norm(v, gffn_ref[...]), h)
    u_hi = _each(_bf, u)
    u_lo = _each(lambda a, b: _bf(a - b.astype(F32)), u, u_hi)
    by_hi = _each(lambda a: jnp.dot(a, rw_ref[...], preferred_element_type=F32), u_hi)
    by_lo = _each(lambda a: jnp.dot(a, rw_ref[:, :ROUTER_LANES], preferred_element_type=F32), u_lo)
    logits = _each(lambda a, b: a[:, :ROUTER_LANES] + a[:, ROUTER_LANES:] + b + rb_ref[...], by_hi, by_lo)
    routed = _each(_route, logits)

    lane_i = lax.broadcasted_iota(jnp.int32, (sub, ROUTER_LANES), 1)
    lane = lane_i.astype(F32)
    hit1 = _each(lambda rt: lane == rt[0], routed)
    hit2 = _each(lambda rt: lane == rt[1], routed)
    both = _each(lambda a, b: a.astype(F32) + b.astype(F32), hit1, hit2)
    earlier = (lax.broadcasted_iota(jnp.int32, (sub, sub), 0) > lax.broadcasted_iota(jnp.int32, (sub, sub), 1))
    inside = _each(lambda b: jnp.dot(earlier.astype(BF16), _bf(b), preferred_element_type=F32), both)
    seen = seen_ref[...]
    for s, p in enumerate(parts):
        idx1, idx2, gate1, gate2 = routed[s]
        before = inside[s] + seen
        rank1 = jnp.sum(jnp.where(hit1[s], before, 0.0), axis=-1, keepdims=True)
        rank2 = jnp.sum(jnp.where(hit2[s], before, 0.0), axis=-1, keepdims=True)
        seen = seen + jnp.sum(both[s], axis=0, keepdims=True)
        route = jnp.zeros((sub, ROUTER_LANES), F32)
        for lane_id, col in ((LANE_IDX1, idx1), (LANE_IDX2, idx2), (LANE_RANK1, rank1),
                             (LANE_RANK2, rank2), (LANE_GATE1, gate1), (LANE_GATE2, gate2)):
            route = jnp.where(lane_i == lane_id, col, route)
        route_ref[p, :] = route
        route_t_ref[:, p] = route.T[:SUBLANES, :]
        h_ref[p, :] = _bf(h[s])
        u_ref[p, :] = _pack_bf16_pairs(u[s])
    seen_ref[...] = seen
    count_ref[...] = jnp.broadcast_to(seen, count_ref.shape)


def _out_call(o, bonus, gate, yconv, x, ln_w, ln_b, w_out, g_ffn, router_w, router_b, tm, n_split):
    n_tok = x.shape[0]
    row = lambda width: pl.BlockSpec((tm, width), lambda i: (i, 0))
    full = lambda arr: pl.BlockSpec(arr.shape, lambda i: (0,) * arr.ndim, pipeline_mode=pl.Buffered(1))
    params = (ln_w, ln_b, w_out, g_ffn, router_w, router_b)
    return pl.pallas_call(
        functools.partial(_out_kernel, n_split=n_split),
        grid=(n_tok // tm,),
        in_specs=[row(D_RWKV)] * 4 + [row(D_MODEL)] + [full(p) for p in params],
        out_specs=[row(D_MODEL), row(D_MODEL // 2), row(ROUTER_LANES),
                   pl.BlockSpec((SUBLANES, tm), lambda i: (0, i)),
                   pl.BlockSpec((SUBLANES, ROUTER_LANES), lambda i: (0, 0))],
        out_shape=[jax.ShapeDtypeStruct((n_tok, D_MODEL), BF16),
                   jax.ShapeDtypeStruct((n_tok, D_MODEL // 2), jnp.int32),
                   jax.ShapeDtypeStruct((n_tok, ROUTER_LANES), F32),
                   jax.ShapeDtypeStruct((SUBLANES, n_tok), F32),
                   jax.ShapeDtypeStruct((SUBLANES, ROUTER_LANES), F32)],
        scratch_shapes=[pltpu.VMEM((1, ROUTER_LANES), F32), pltpu.VMEM(w_out.shape, BF16)],
        compiler_params=pltpu.CompilerParams(
            dimension_semantics=("arbitrary",), vmem_limit_bytes=VMEM_LIMIT),
        name="out_proj_route",
    )(o, bonus, gate, yconv, x, *params)


SC_CORES = 2
SC_SUBCORES = 16
SC_ROWS = 64


def _sc_mesh():
    return plsc.VectorSubcoreMesh(core_axis_name="c", subcore_axis_name="s",
                                  num_cores=SC_CORES, num_subcores=SC_SUBCORES)


def _sc_worker():
    return lax.axis_index("s") * SC_CORES + lax.axis_index("c")


def _sc_gather(table, idx):
    n_rows = idx.shape[0]
    width = table.shape[1]
    n_chunks = n_rows // (SC_CORES * SC_SUBCORES * SC_ROWS)

    def body(table_hbm, idx_hbm, out_hbm, idx_v, rows_v, gather_sem, write_sem):
        first = _sc_worker() * n_chunks
        pltpu.sync_copy(idx_hbm.at[pl.ds(first, n_chunks)], idx_v)
        gather = lambda j: pltpu.async_copy(table_hbm.at[idx_v.at[j]], rows_v.at[j % 2], gather_sem.at[j % 2])
        gathers = [gather(0)]
        writes = []
        for j in range(n_chunks):
            gathers[j].wait()
            if j + 1 < n_chunks:
                if j >= 1:
                    writes[j - 1].wait()
                gathers.append(gather(j + 1))
            dst = out_hbm.at[pl.ds(pl.multiple_of((first + j) * SC_ROWS, SC_ROWS), SC_ROWS)]
            writes.append(pltpu.async_copy(rows_v.at[j % 2], dst, write_sem.at[j % 2]))
        for j in range(max(n_chunks - 2, 0), n_chunks):
            writes[j].wait()

    return pl.kernel(
        body,
        out_type=jax.ShapeDtypeStruct((n_rows, width), table.dtype),
        mesh=_sc_mesh(),
        scratch_types=[pltpu.VMEM((n_chunks, SC_ROWS), jnp.int32), pltpu.VMEM((2, SC_ROWS, width), table.dtype),
                       pltpu.SemaphoreType.DMA((2,)), pltpu.SemaphoreType.DMA((2,))],
        name="sc_row_gather",
    )(table, idx.reshape(n_rows // SC_ROWS, SC_ROWS))


def _sc_scatter(rows, pos, n_out):
    n_rows, width = rows.shape
    n_slots = pos.shape[0] // n_rows
    slot_chunks = n_rows // SC_ROWS
    n_chunks = slot_chunks // (SC_CORES * SC_SUBCORES)

    def body(rows_hbm, pos_hbm, out_hbm, idx_v, rows_v, read_sem, scatter_sem):
        first = _sc_worker() * n_chunks
        for s in range(n_slots):
            pltpu.sync_copy(pos_hbm.at[pl.ds(s * slot_chunks + first, n_chunks)], idx_v.at[s])
        read = lambda j: pltpu.async_copy(
            rows_hbm.at[pl.ds(pl.multiple_of((first + j) * SC_ROWS, SC_ROWS), SC_ROWS)],
            rows_v.at[j % 2], read_sem.at[j % 2])
        reads = [read(0)]
        scatters = []
        for j in range(n_chunks):
            reads[j].wait()
            if j + 1 < n_chunks:
                if j >= 1:
                    for copy in scatters[j - 1]:
                        copy.wait()
                reads.append(read(j + 1))
            scatters.append([pltpu.async_copy(rows_v.at[j % 2], out_hbm.at[idx_v.at[s, j]], scatter_sem.at[j % 2])
                             for s in range(n_slots)])
        for j in range(max(n_chunks - 2, 0), n_chunks):
            for copy in scatters[j]:
                copy.wait()

    return pl.kernel(
        body,
        out_type=jax.ShapeDtypeStruct((n_out, width), rows.dtype),
        mesh=_sc_mesh(),
        scratch_types=[pltpu.VMEM((n_slots, n_chunks, SC_ROWS), jnp.int32),
                       pltpu.VMEM((2, SC_ROWS, width), rows.dtype),
                       pltpu.SemaphoreType.DMA((2,)), pltpu.SemaphoreType.DMA((2,))],
        name="sc_row_scatter",
    )(rows, pos.reshape(n_slots * slot_chunks, SC_ROWS))


def _expert_kernel(tile_expert_ref, n_valid_ref, slot_ref, next_expert_ref, x_ref, wg_hbm, wu_hbm, wd_hbm,
                   y_ref, wg_buf, wu_buf, wd_buf, sem):
    step = pl.program_id(0)
    expert = tile_expert_ref[step]
    slot = slot_ref[step]
    valid = step < n_valid_ref[0]
    first_of_expert = (step == 0) | (expert != tile_expert_ref[jnp.maximum(step - 1, 0)])

    def weight_copies(which, into):
        pairs = ((wg_hbm, wg_buf), (wu_hbm, wu_buf), (wd_hbm, wd_buf))
        return [pltpu.make_async_copy(w.at[which], buf.at[into], sem.at[into, j]) for j, (w, buf) in enumerate(pairs)]

    @pl.when(step == 0)
    def _():
        for copy in weight_copies(expert, slot):
            copy.start()

    @pl.when(valid & first_of_expert)
    def _():
        for copy in weight_copies(expert, slot):
            copy.wait()
        upcoming = next_expert_ref[step]

        @pl.when(upcoming < N_EXPERTS)
        def _():
            for copy in weight_copies(upcoming, 1 - slot):
                copy.start()

    @pl.when(valid)
    def _():
        x = _unpack_bf16_pairs(x_ref[...])
        gate = jnp.dot(x, wg_buf[slot], preferred_element_type=F32)
        up = jnp.dot(x, wu_buf[slot], preferred_element_type=F32)
        hid = gate * jax.nn.sigmoid(gate) * up
        y_ref[...] = _pack_bf16_pairs(jnp.dot(_bf(hid), wd_buf[slot], preferred_element_type=F32))


def _expert_call(tile_expert, n_valid, slot, next_expert, x_sorted, w_gate, w_up, w_down, tm):
    n_rows = x_sorted.shape[0]
    rows = pl.BlockSpec((tm, D_MODEL // 2), lambda i, te, nv, sl, nx: (jnp.minimum(i, nv[0] - 1), 0))
    in_hbm = pl.BlockSpec(memory_space=pl.ANY)
    return pl.pallas_call(
        _expert_kernel,
        grid_spec=pltpu.PrefetchScalarGridSpec(
            num_scalar_prefetch=4,
            grid=(n_rows // tm,),
            in_specs=[rows, in_hbm, in_hbm, in_hbm],
            out_specs=rows,
            scratch_shapes=[pltpu.VMEM((2,) + w_gate.shape[1:], BF16), pltpu.VMEM((2,) + w_up.shape[1:], BF16),
                            pltpu.VMEM((2,) + w_down.shape[1:], BF16), pltpu.SemaphoreType.DMA((2, 3))]),
        out_shape=jax.ShapeDtypeStruct((n_rows, D_MODEL // 2), jnp.int32),
        compiler_params=pltpu.CompilerParams(
            dimension_semantics=("arbitrary",), vmem_limit_bytes=VMEM_LIMIT),
        name="moe_experts",
    )(tile_expert, n_valid, slot, next_expert, x_sorted, w_gate, w_up, w_down)


def _final_kernel(h_ref, y1_ref, y2_ref, route_ref, gfin_ref, *rest):
    out_ref = rest[-1]
    route = route_ref[...]
    gate1 = route[:, LANE_GATE1:LANE_GATE1 + 1]
    gate2 = route[:, LANE_GATE2:LANE_GATE2 + 1]
    moe = gate1 * _unpack_bf16_pairs(y1_ref[...]).astype(F32) + gate2 * _unpack_bf16_pairs(y2_ref[...]).astype(F32)
    out_ref[...] = _rms_norm(h_ref[...] + moe, gfin_ref[...])


def _final_call(h, y_pairs, route, g_final, earlier, first_token, tm):
    n_tok = h.shape[0]
    n_blocks = y_pairs.shape[0] // 2 // tm
    assert n_blocks * tm * 2 == y_pairs.shape[0] and first_token % tm == 0, "chunks must be whole row blocks"
    first = first_token // tm
    in_specs = [pl.BlockSpec((tm, D_MODEL), lambda i: (first + i, 0)),
                pl.BlockSpec((tm, D_MODEL // 2), lambda i: (i, 0)),
                pl.BlockSpec((tm, D_MODEL // 2), lambda i: (i + n_blocks, 0)),
                pl.BlockSpec((tm, ROUTER_LANES), lambda i: (first + i, 0)),
                pl.BlockSpec((1, D_MODEL), lambda i: (0, 0))]
    args = [h, y_pairs, y_pairs, route, g_final]
    aliases = {}
    if earlier is not None:
        in_specs.append(pl.BlockSpec(memory_space=pl.ANY))
        args.append(earlier)
        aliases = {len(args) - 1: 0}
    return pl.pallas_call(
        _final_kernel,
        grid=(n_blocks,),
        in_specs=in_specs,
        out_specs=pl.BlockSpec((tm, D_MODEL), lambda i: (first + i, 0)),
        out_shape=jax.ShapeDtypeStruct((n_tok, D_MODEL), F32),
        input_output_aliases=aliases,
        compiler_params=pltpu.CompilerParams(
            dimension_semantics=("arbitrary",), vmem_limit_bytes=VMEM_LIMIT),
        name="moe_combine_norm",
    )(*args)


def _dispatch_plan(route_t, counts, tm):
    n_tok = route_t.shape[1]
    n_tiles = (2 * n_tok) // tm + N_EXPERTS
    counts = counts[0, :N_EXPERTS].astype(jnp.int32)
    tiles_per = (counts + tm - 1) // tm
    tile_end = jnp.cumsum(tiles_per)
    row_start = (tile_end - tiles_per) * tm
    experts = jnp.arange(N_EXPERTS, dtype=jnp.int32)

    def position(idx_lane, rank_lane):
        idx = route_t[idx_lane].astype(jnp.int32)
        start = jnp.sum(jnp.where(idx[None, :] == experts[:, None], row_start[:, None], 0), axis=0)
        return start + route_t[rank_lane].astype(jnp.int32)

    pos = jnp.concatenate([position(LANE_IDX1, LANE_RANK1), position(LANE_IDX2, LANE_RANK2)])
    n_valid = tile_end[-1:]
    tile = jnp.minimum(jnp.arange(n_tiles, dtype=jnp.int32), n_valid - 1)
    tile_expert = jnp.sum((tile_end[None, :] <= tile[:, None]).astype(jnp.int32), axis=1)
    present = tiles_per > 0
    later = (experts[None, :] > experts[:, None]) & present[None, :]
    next_present = jnp.min(jnp.where(later, experts[None, :], N_EXPERTS), axis=1)
    order = jnp.cumsum(present.astype(jnp.int32)) - 1
    of_tile = lambda per_expert: jnp.sum(jnp.where(tile_expert[:, None] == experts[None, :], per_expert[None, :], 0), axis=1)
    return pos, n_tiles, tile_expert, n_valid, of_tile(order) % 2, of_tile(next_present)


def _block(x, norm_mix_g, w_in, rwkv_mu, conv_w, decay_up, decay_base, aaa_up, aaa_base, gate_up,
           k_k, k_a, r_k, ln_x_w, ln_x_b, w_out, norm_ffn_g, router_group_w, router_group_b,
           router_expert_w, router_expert_b, expert_w_gate, expert_w_up, expert_w_down, norm_final_g,
           *, tm_in, tb_rwkv, tm_out, tm_expert, tm_final):
    bsz, seq, d_model = x.shape
    n_tok = bsz * seq
    row = lambda p: p.reshape(1, -1)

    yconv, r, lw, k2, v, kk, kka, gate, bonus = _in_call(
        x, row(norm_mix_g), w_in, row(rwkv_mu), conv_w, decay_up, row(decay_base), aaa_up, row(aaa_base),
        gate_up, row(k_k), row(k_a), row(r_k), tm_in)
    o, w_gate_bf, w_up_bf, w_down_bf = _rwkv_call(r, lw, k2, v, kk, kka,
                                                  (expert_w_gate, expert_w_up, expert_w_down), tb_rwkv)

    pad = ROUTER_LANES - N_EXPERTS - N_GROUPS
    router_w = jnp.concatenate([router_expert_w, router_group_w, jnp.zeros((d_model, pad), F32)], axis=1)
    router_hi = _bf(router_w)
    router_split = jnp.concatenate([router_hi, _bf(router_w - router_hi.astype(F32))], axis=1)
    router_b = jnp.concatenate([router_expert_b, router_group_b, jnp.zeros((pad,), F32)]).reshape(1, -1)
    flat = lambda t: t.reshape(n_tok, t.shape[-1])
    h, u_pairs, route, route_t, counts = _out_call(
        flat(o), flat(bonus), flat(gate), flat(yconv), flat(x),
        row(ln_x_w), row(ln_x_b), w_out, row(norm_ffn_g), router_split, router_b, tm_out, OUT_SPLIT)

    pos, n_tiles, tile_expert, n_valid, slot, next_expert = _dispatch_plan(route_t, counts, tm_expert)
    x_sorted = _sc_scatter(u_pairs, pos, n_tiles * tm_expert)
    y_sorted = _expert_call(tile_expert, n_valid, slot, next_expert, x_sorted, w_gate_bf, w_up_bf, w_down_bf,
                            tm_expert)
    out, lo = None, 0
    for fraction in COMBINE_SPLIT:
        size = n_tok // fraction
        pos_c = jnp.concatenate([pos[lo:lo + size], pos[n_tok + lo:n_tok + lo + size]])
        out = _final_call(h, _sc_gather(y_sorted, pos_c), route, row(norm_final_g), out, lo, tm_final)
        lo += size
    assert lo == n_tok, "COMBINE_SPLIT must cover all tokens"
    return out.reshape(bsz, seq, d_model)


def kernel(x, norm_mix_g, w_in, rwkv_mu, conv_w, decay_up, decay_base, aaa_up, aaa_base, gate_up, k_k, k_a, r_k, ln_x_w, ln_x_b, w_out, norm_ffn_g, router_group_w, router_group_b, router_expert_w, router_expert_b, expert_w_gate, expert_w_up, expert_w_down, norm_final_g):
    return _block(x, norm_mix_g[0], w_in[0], rwkv_mu[0], conv_w[0], decay_up[0], decay_base[0],
                  aaa_up[0], aaa_base[0], gate_up[0], k_k[0], k_a[0], r_k[0].reshape(-1), ln_x_w[0],
                  ln_x_b[0], w_out[0], norm_ffn_g[0], router_group_w[0], router_group_b[0],
                  router_expert_w[0], router_expert_b[0], expert_w_gate[0], expert_w_up[0],
                  expert_w_down[0], norm_final_g,
                  tm_in=512, tb_rwkv=1024, tm_out=1024, tm_expert=512, tm_final=1024)
```

```python
import functools

import jax
import jax.numpy as jnp
from jax import lax
from jax.experimental import pallas as pl
from jax.experimental.pallas import tpu as pltpu
from jax.experimental.pallas import tpu_sc as plsc

F32 = jnp.float32
BF16 = jnp.bfloat16

D_MODEL = 1024
D_CONV = 512
CONV_WIDTH = 3
N_HEADS = 8
HEAD = 64
D_RWKV = N_HEADS * HEAD
LORA_WA = 128
GATE_LORA = 128
D_RWKV_PROJ = 3 * D_RWKV + LORA_WA + GATE_LORA
D_IN = 3 * D_CONV + D_RWKV_PROJ
N_GROUPS = 4
EXPERTS_PER_GROUP = 8
N_EXPERTS = N_GROUPS * EXPERTS_PER_GROUP
D_EXPERT = D_MODEL // 4
RMS_EPS = 1e-6
LN_X_EPS = 64e-5
L2_EPS = 1e-12

SUBLANES = 8
CHUNK = 64
QUAD = 4 * HEAD
GROUP = 2 * HEAD
INV_BASE = 8
COMPACT_FROM = 64
OUT_SPLIT = 8
COMBINE_SPLIT = (4, 4, 4, 4)
ROUTER_LANES = 128

VMEM_LIMIT = 56 * 1024 * 1024


def _bf(x):
    return x.astype(BF16)


def _mm(a, b):
    return jnp.dot(_bf(a), _bf(b), preferred_element_type=F32)


def _mm_nt(a, b):
    return lax.dot_general(_bf(a), _bf(b), (((1,), (1,)), ((), ())), preferred_element_type=F32)


def _mm_exact_lhs(lhs_bf16, x, passes):
    acc = None
    rem = x
    for _ in range(passes):
        piece = _bf(rem)
        part = jnp.dot(lhs_bf16, piece, preferred_element_type=F32)
        acc = part if acc is None else acc + part
        rem = rem - piece.astype(F32)
    return acc


def _head_ones():
    return (lax.broadcasted_iota(jnp.int32, (QUAD, QUAD), 0) // HEAD
            == lax.broadcasted_iota(jnp.int32, (QUAD, QUAD), 1) // HEAD).astype(BF16)


def _head_sum(x, ones_quad):
    xb = _bf(x)
    return jnp.concatenate(
        [jnp.dot(xb[:, q * QUAD:(q + 1) * QUAD], ones_quad, preferred_element_type=F32)
         for q in range(x.shape[1] // QUAD)], axis=1)


def _rms_norm(x, g):
    return x * lax.rsqrt(jnp.mean(x * x, axis=-1, keepdims=True) + RMS_EPS) * g


def _shift_rows(cur, prev_rows, k):
    rolled = pltpu.roll(cur, k, 0)
    prev_rolled = pltpu.roll(prev_rows, k, 0)
    n = cur.shape[0]
    head = jnp.concatenate([prev_rolled, rolled[SUBLANES:]], axis=0) if n > SUBLANES else prev_rolled
    row = lax.broadcasted_iota(jnp.int32, cur.shape, 0)
    return jnp.where(row < k, head, rolled)


def _in_kernel(x_ref, g_ref, w_ref, mu_ref, convw_ref, dup_ref, dbase_ref, aup_ref, abase_ref,
               gup_ref, kk_ref, ka_ref, rk_ref,
               yconv_ref, r_ref, lw_ref, k_ref, v_ref, kkn_ref, kka_ref, gate_ref, bonus_ref,
               carry_ref, wbf_ref):
    @pl.when((pl.program_id(0) == 0) & (pl.program_id(1) == 0))
    def _():
        wbf_ref[...] = _bf(w_ref[...])

    @pl.when(pl.program_id(1) == 0)
    def _():
        carry_ref[...] = jnp.zeros_like(carry_ref)

    u = _bf(_rms_norm(x_ref[0], g_ref[...]))
    n_conv = 3 * D_CONV
    rk0 = n_conv
    bounds = dict(conv=(0, n_conv), lora=(rk0 + 3 * D_RWKV, D_IN), k=(rk0 + D_RWKV, rk0 + 2 * D_RWKV),
                  r=(rk0, rk0 + D_RWKV), v=(rk0 + 2 * D_RWKV, rk0 + 3 * D_RWKV))
    z = {name: jnp.dot(u, wbf_ref[:, lo:hi], preferred_element_type=F32) for name, (lo, hi) in bounds.items()}

    def lerp(name):
        lo, hi = bounds[name]
        cur = z[name]
        prev = carry_ref[:, D_CONV + lo - n_conv:D_CONV + hi - n_conv]
        mixed = cur + (_shift_rows(cur, prev, 1) - cur) * mu_ref[:, lo - n_conv:hi - n_conv]
        carry_ref[:, D_CONV + lo - n_conv:D_CONV + hi - n_conv] = cur[-SUBLANES:]
        return mixed

    b_gate = z["conv"][:, :D_CONV]
    ch = z["conv"][:, D_CONV:2 * D_CONV] * z["conv"][:, 2 * D_CONV:]
    prev_ch = carry_ref[:, :D_CONV]
    conv = convw_ref[CONV_WIDTH - 1:CONV_WIDTH, :] * ch
    for delay in range(1, CONV_WIDTH):
        tap = CONV_WIDTH - 1 - delay
        conv = conv + convw_ref[tap:tap + 1, :] * _shift_rows(ch, prev_ch, delay)
    yconv_ref[0] = _bf(b_gate * conv)
    carry_ref[:, :D_CONV] = ch[-SUBLANES:]

    lora = lerp("lora")
    wa_lo = lora[:, :LORA_WA]
    g_lo = lora[:, LORA_WA:]
    no_rows = jnp.zeros_like(dup_ref[...])
    decay_up = _bf(jnp.concatenate([dup_ref[...], no_rows], axis=0))
    rate_up = _bf(jnp.concatenate([no_rows, aup_ref[...]], axis=0))
    dec_in = -(dbase_ref[...] + jnp.dot(_bf(jnp.tanh(wa_lo)), decay_up, preferred_element_type=F32))
    softplus = jnp.maximum(dec_in, 0.0) + jnp.log(1.0 + jnp.exp(-jnp.abs(dec_in)))
    w = -softplus - 0.5
    lw_ref[0] = -jnp.exp(w)
    a = jax.nn.sigmoid(abase_ref[...] + jnp.dot(_bf(wa_lo), rate_up, preferred_element_type=F32))
    gate_ref[0] = _bf(jnp.dot(_bf(jax.nn.sigmoid(g_lo)), _bf(gup_ref[...]), preferred_element_type=F32))

    ones_quad = _head_ones()
    k = lerp("k")
    kk = k * kk_ref[...]
    norm = jnp.sqrt(_head_sum(kk * kk, ones_quad))
    kk = kk / jnp.maximum(norm, L2_EPS)
    k2 = k * (1.0 + (a - 1.0) * ka_ref[...])
    k_ref[0] = _bf(k2)
    kkn_ref[0] = _bf(kk)
    kka_ref[0] = _bf(kk * a)
    r = lerp("r")
    r_ref[0] = _bf(r)
    v = lerp("v")
    v_ref[0] = _bf(v)
    bonus_ref[0] = _bf(_head_sum(r * k2 * rk_ref[...], ones_quad) * v)


def _in_call(x, g, w_in, mu, conv_w, dup, dbase, aup, abase, gup, k_k, k_a, r_k, tm):
    bsz, seq, _ = x.shape
    full = lambda arr: pl.BlockSpec(arr.shape, lambda b, t: (0,) * arr.ndim, pipeline_mode=pl.Buffered(1))
    out_spec = pl.BlockSpec((1, tm, D_RWKV), lambda b, t: (b, t, 0))
    out_dtypes = (BF16, BF16, F32, BF16, BF16, BF16, BF16, BF16, BF16)
    params = (g, w_in, mu, conv_w, dup, dbase, aup, abase, gup, k_k, k_a, r_k)
    return pl.pallas_call(
        _in_kernel,
        grid=(bsz, seq // tm),
        in_specs=[pl.BlockSpec((1, tm, D_MODEL), lambda b, t: (b, t, 0))] + [full(p) for p in params],
        out_specs=[out_spec] * 9,
        out_shape=[jax.ShapeDtypeStruct((bsz, seq, D_RWKV), dt) for dt in out_dtypes],
        scratch_shapes=[pltpu.VMEM((SUBLANES, D_CONV + D_RWKV_PROJ), F32), pltpu.VMEM(w_in.shape, BF16)],
        compiler_params=pltpu.CompilerParams(
            dimension_semantics=("arbitrary", "arbitrary"), vmem_limit_bytes=VMEM_LIMIT),
        name="in_proj",
    )(x, *params)


def _block_diag(y, bd_mask):
    return jnp.where(bd_mask, jnp.concatenate([y] * (GROUP // CHUNK), axis=0), 0.0)


def _each(fn, *lists):
    return [fn(*args) for args in zip(*lists)]


def _unit_lower_inverse(a_strict, t_idx, s_idx, bd):
    bdmm = lambda xs, ys: _each(lambda x, y: _mm(x, bd(y)), xs, ys)
    eye = (t_idx == s_idx).astype(F32)
    same8 = (t_idx // INV_BASE) == (s_idx // INV_BASE)
    a8 = _each(lambda a: jnp.where(same8, a, 0.0), a_strict)
    a8_2 = bdmm(a8, a8)
    a8_34 = bdmm(_each(lambda a, b: jnp.concatenate([a, b], axis=0), a8, a8_2), a8_2)
    inv = _each(lambda a, b, c: eye + a + b + c[:CHUNK], a8, a8_2, a8_34)
    inv = _each(jnp.add, inv, bdmm(inv, _each(lambda c: c[CHUNK:], a8_34)))
    size = 2 * INV_BASE
    while size < COMPACT_FROM:
        off = ((t_idx // size) == (s_idx // size)) & ((t_idx // (size // 2)) != (s_idx // (size // 2)))
        cross = bdmm(_each(lambda a: jnp.where(off, a, 0.0), a_strict), inv)
        inv = _each(jnp.add, inv, bdmm(inv, cross))
        size *= 2
    while size <= CHUNK:
        inv = _compact_level(a_strict, inv, size)
        size *= 2
    return inv


def _compact_level(a_strict, inv, size):
    half = size // 2
    heads = GROUP // CHUNK
    blocks = CHUNK // size
    lane_starts = [h * CHUNK + b * size for h in range(heads) for b in range(blocks)]
    row_starts = [b * size for b in range(blocks)]
    rows_first = lambda x: jnp.concatenate([x[r:r + half] for r in row_starts], axis=0)
    rows_second = lambda x: jnp.concatenate([x[r + half:r + size] for r in row_starts], axis=0)
    lanes_first = lambda x: jnp.concatenate([x[:, c:c + half] for c in lane_starts], axis=1)
    lanes_second = lambda x: jnp.concatenate([x[:, c + half:c + size] for c in lane_starts], axis=1)
    n_rows, n_lanes = CHUNK // 2, GROUP // 2
    own = (lax.broadcasted_iota(jnp.int32, (n_rows, n_lanes), 0) // half
           == (lax.broadcasted_iota(jnp.int32, (n_rows, n_lanes), 1) // half) % blocks)
    diag = (lax.broadcasted_iota(jnp.int32, (n_lanes, n_lanes), 0) // half
            == lax.broadcasted_iota(jnp.int32, (n_lanes, n_lanes), 1) // half)
    bd_half = lambda y: jnp.where(diag, jnp.concatenate([y] * heads, axis=0), 0.0)
    a21 = _each(lambda a: jnp.where(own, lanes_first(rows_second(a)), 0.0), a_strict)
    t11 = _each(lambda t: lanes_first(rows_first(t)), inv)
    t22 = _each(lambda t: lanes_second(rows_second(t)), inv)
    a21_t11 = _each(lambda x, y: _mm(x, bd_half(y)), a21, t11)
    new = _each(lambda x, y: _mm(x, bd_half(y)), t22, a21_t11)

    def placed(x):
        zero_lanes = jnp.zeros((n_rows, half), F32)
        wide = jnp.concatenate(sum(([x[:, i * half:(i + 1) * half], zero_lanes] for i in range(heads * blocks)), []),
                               axis=1)
        zero_rows = jnp.zeros((half, GROUP), F32)
        return jnp.concatenate(sum(([zero_rows, wide[b * half:(b + 1) * half]] for b in range(blocks)), []), axis=0)

    return _each(lambda t, x: t + placed(x), inv, new)


def _chunk_local(r, lw, k, v, kk, kka, tri, t_idx, s_idx, bd_mask):
    bd = lambda y: _block_diag(y, bd_mask)
    bdmm = lambda xs, ys: _each(lambda x, y: _mm(x, bd(y)), xs, ys)
    bdmm2 = lambda xs, ys, zs: _each(lambda x, y, z: _mm(x, jnp.concatenate([bd(y), bd(z)], axis=1)), xs, ys, zs)
    left = lambda xs: _each(lambda x: x[:, :GROUP], xs)
    right = lambda xs: _each(lambda x: x[:, GROUP:], xs)
    top = lambda xs: _each(lambda x: x[:CHUNK], xs)
    bottom = lambda xs: _each(lambda x: x[CHUNK:], xs)

    cum = _each(lambda x: _mm_exact_lhs(tri, x, 3), lw)
    cum_last = _each(lambda c: c[CHUNK - 1:CHUNK, :], cum)
    p_incl = _each(jnp.exp, cum)
    p_excl = _each(lambda c, x: jnp.exp(c - x), cum, lw)
    p_inv = _each(lambda c: jnp.exp(-c), cum)
    to_end = _each(lambda cl, c: jnp.exp(cl - c), cum_last, cum)
    a_t = _each(lambda x, p: -x * p, kk, p_excl)
    r_t = _each(jnp.multiply, r, p_incl)
    b_t = _each(jnp.multiply, kka, p_inv)
    k_t = _each(jnp.multiply, k, p_inv)
    b_end = _each(jnp.multiply, kka, to_end)
    k_end = _each(jnp.multiply, k, to_end)

    ar = _each(lambda a, b: jnp.concatenate([a, b], axis=0), a_t, r_t)
    row2 = lax.broadcasted_iota(jnp.int32, (2 * CHUNK, 2 * GROUP), 0)
    col2 = lax.broadcasted_iota(jnp.int32, (2 * CHUNK, 2 * GROUP), 1) % CHUNK
    causal = col2 < (row2 % CHUNK) + (row2 // CHUNK)
    scores = _each(lambda x, y, z: jnp.where(causal, _mm_nt(x, jnp.concatenate([bd(y), bd(z)], axis=0)), 0.0),
                   ar, b_t, k_t)
    a_ab = top(left(scores))
    a_rb = bottom(left(scores))

    inv = _unit_lower_inverse(a_ab, t_idx, s_idx, bd)
    kv = bdmm(right(scores), v)
    wu = bdmm2(inv, a_t, top(kv))
    ro = bdmm2(a_rb, left(wu), right(wu))
    r_hat = _each(jnp.add, r_t, left(ro))
    o_loc = _each(jnp.add, right(ro), bottom(kv))

    wu_b = _each(lambda x, b: _mm(x.T, b), wu, b_end)
    trans = _each(lambda x: jnp.where(bd_mask, x[:GROUP], 0.0), wu_b)
    d_bd = _each(lambda x, y, z: jnp.where(bd_mask, x[GROUP:] + _mm(y.T, z), 0.0), wu_b, v, k_end)
    d_ls = _each(lambda d: sum(d[h * CHUNK:(h + 1) * CHUNK] for h in range(1, GROUP // CHUNK)) + d[:CHUNK], d_bd)
    p_end = _each(jnp.exp, cum_last)
    return r_hat, o_loc, trans, d_ls, p_end


def _rwkv_kernel(r_ref, lw_ref, k_ref, v_ref, kk_ref, kka_ref, wg_ref, wu_ref, wd_ref,
                 o_ref, wg_bf_ref, wu_bf_ref, wd_bf_ref, state_ref, *, n_sub):
    @pl.when(pl.program_id(1) == 0)
    def _():
        state_ref[...] = jnp.zeros_like(state_ref)

    wg_bf_ref[...] = _bf(wg_ref[...])
    wu_bf_ref[...] = _bf(wu_ref[...])
    wd_bf_ref[...] = _bf(wd_ref[...])

    t_idx = lax.broadcasted_iota(jnp.int32, (CHUNK, GROUP), 0)
    s_idx = lax.broadcasted_iota(jnp.int32, (CHUNK, GROUP), 1) % CHUNK
    bd_mask = (lax.broadcasted_iota(jnp.int32, (GROUP, GROUP), 0) // CHUNK
               == lax.broadcasted_iota(jnp.int32, (GROUP, GROUP), 1) // CHUNK)
    chunk_row = lax.broadcasted_iota(jnp.int32, (CHUNK, CHUNK), 0)
    chunk_col = lax.broadcasted_iota(jnp.int32, (CHUNK, CHUNK), 1)
    n_group = D_RWKV // GROUP
    where = [(slice(c * CHUNK, (c + 1) * CHUNK), slice(g * GROUP, (g + 1) * GROUP))
             for c in range(n_sub) for g in range(n_group)]
    load = lambda ref: [ref[0, rows, lanes].astype(F32) for rows, lanes in where]
    r_hat, o_loc, trans, d_ls, p_end = _chunk_local(
        load(r_ref), load(lw_ref), load(k_ref), load(v_ref), load(kk_ref), load(kka_ref),
        (chunk_row >= chunk_col).astype(BF16), t_idx, s_idx, bd_mask)

    state = [state_ref[g] for g in range(n_group)]
    for c in range(n_sub):
        chains = range(c * n_group, (c + 1) * n_group)
        out = [_mm_nt(r_hat[i], _block_diag(state[g], bd_mask)) + o_loc[i] for g, i in enumerate(chains)]
        for g, i in enumerate(chains):
            rows, lanes = where[i]
            o_ref[0, rows, lanes] = _bf(out[g])
        state = [state[g] * p_end[i] + _mm(state[g], trans[i]) + d_ls[i] for g, i in enumerate(chains)]
    for g in range(n_group):
        state_ref[g] = state[g]


def _rwkv_call(r, lw, k, v, kk, kka, expert_weights, tb):
    bsz, seq, _ = r.shape
    n_t = seq // tb
    per_step = N_EXPERTS // (bsz * n_t)
    assert per_step * bsz * n_t == N_EXPERTS, "grid steps must divide the expert count"
    spec = pl.BlockSpec((1, tb, D_RWKV), lambda b, t: (b, t, 0))
    w_specs = [pl.BlockSpec((per_step,) + w.shape[1:], lambda b, t: (b * n_t + t, 0, 0)) for w in expert_weights]
    return pl.pallas_call(
        functools.partial(_rwkv_kernel, n_sub=tb // CHUNK),
        grid=(bsz, n_t),
        in_specs=[spec] * 6 + w_specs,
        out_specs=[spec] + w_specs,
        out_shape=[jax.ShapeDtypeStruct((bsz, seq, D_RWKV), BF16)]
        + [jax.ShapeDtypeStruct(w.shape, BF16) for w in expert_weights],
        scratch_shapes=[pltpu.VMEM((D_RWKV // GROUP, HEAD, GROUP), F32)],
        compiler_params=pltpu.CompilerParams(
            dimension_semantics=("arbitrary", "arbitrary"), vmem_limit_bytes=VMEM_LIMIT),
        name="rwkv_chunk",
    )(r, lw, k, v, kk, kka, *expert_weights)


def _route(logits):
    lane_i = lax.broadcasted_iota(jnp.int32, logits.shape, 1)
    lane = lane_i.astype(F32)
    lane_group = (lane_i // EXPERTS_PER_GROUP).astype(F32)
    neg = -jnp.inf
    big = float(ROUTER_LANES)
    is_group = (lane_i >= N_EXPERTS) & (lane_i < N_EXPERTS + N_GROUPS)
    gl = jnp.where(is_group, logits, neg)
    ge = jnp.exp(gl - jnp.max(gl, axis=-1, keepdims=True))
    gprob = ge / jnp.sum(ge, axis=-1, keepdims=True)
    g_top = jnp.max(gprob, axis=-1, keepdims=True)
    g_idx = jnp.min(jnp.where(is_group & (gprob == g_top), lane - N_EXPERTS, big), axis=-1, keepdims=True)

    in_group = (lane_i < N_EXPERTS) & (lane_group == g_idx)
    el = jnp.where(in_group, logits, neg)
    top1 = jnp.max(el, axis=-1, keepdims=True)
    idx1 = jnp.min(jnp.where(in_group & (el == top1), lane, big), axis=-1, keepdims=True)
    el2 = jnp.where(lane == idx1, neg, el)
    top2 = jnp.max(el2, axis=-1, keepdims=True)
    idx2 = jnp.min(jnp.where(in_group & (lane != idx1) & (el2 == top2), lane, big), axis=-1, keepdims=True)
    e2 = jnp.exp(top2 - top1)
    denom = 1.0 + e2
    return idx1, idx2, g_top * (1.0 / denom), g_top * (e2 / denom)


def _pack_bf16_pairs(x):
    bits = lax.bitcast_convert_type(_bf(x).astype(F32), jnp.uint32)
    k = x.shape[1] // 2
    return lax.bitcast_convert_type((bits[:, :k] >> 16) | bits[:, k:], jnp.int32)


def _unpack_bf16_pairs(words):
    bits = lax.bitcast_convert_type(words, jnp.uint32)
    lo = lax.bitcast_convert_type(bits << 16, F32)
    hi = lax.bitcast_convert_type(bits & jnp.uint32(0xFFFF0000), F32)
    return _bf(jnp.concatenate([lo, hi], axis=1))


LANE_IDX1, LANE_IDX2, LANE_RANK1, LANE_RANK2, LANE_GATE1, LANE_GATE2 = range(6)


def _out_kernel(o_ref, bonus_ref, gate_ref, yconv_ref, x_ref, lnw_ref, lnb_ref,
                wout_ref, gffn_ref, rw_ref, rb_ref, h_ref, u_ref, route_t_ref, count_ref,
                seen_ref, wbf_ref, *, n_split):
    @pl.when(pl.program_id(0) == 0)
    def _():
        seen_ref[...] = jnp.zeros_like(seen_ref)
        wbf_ref[...] = _bf(wout_ref[...])

    sub = o_ref.shape[0] // n_split
    parts = [slice(s * sub, (s + 1) * sub) for s in range(n_split)]
    read = lambda ref: [ref[p, :] for p in parts]
    ones_quad = _head_ones()
    inv_n = 1.0 / HEAD
    o = _each(lambda v: v.astype(F32), read(o_ref))
    mean = _each(lambda v: _head_sum(v, ones_quad) * inv_n, o)
    cen = _each(jnp.subtract, o, mean)
    var = _each(lambda c: _head_sum(c * c, ones_quad) * inv_n, cen)
    on = _each(lambda c, v: c * lax.rsqrt(v + LN_X_EPS) * lnw_ref[...] + lnb_ref[...], cen, var)
    y_rwkv = _each(lambda a, b, g: (a + b) * g, on, read(bonus_ref), read(gate_ref))
    mixed = _each(lambda yc, yr: (jnp.dot(yc, wbf_ref[:D_CONV, :], preferred_element_type=F32)
                                  + jnp.dot(_bf(yr), wbf_ref[D_CONV:, :], preferred_element_type=F32)),
                  read(yconv_ref), y_rwkv)
    h = _each(jnp.add, read(x_ref), mixed)
    u = _each(lambda v: _rms_norm(v, gffn_ref[...]), h)
    u_hi = _each(_bf, u)
    u_lo = _each(lambda a, b: _bf(a - b.astype(F32)), u, u_hi)
    by_hi = _each(lambda a: jnp.dot(a, rw_ref[...], preferred_element_type=F32), u_hi)
    by_lo = _each(lambda a: jnp.dot(a, rw_ref[:, :ROUTER_LANES], preferred_element_type=F32), u_lo)
    logits = _each(lambda a, b: a[:, :ROUTER_LANES] + a[:, ROUTER_LANES:] + b + rb_ref[...], by_hi, by_lo)
    routed = _each(_route, logits)

    lane_i = lax.broadcasted_iota(jnp.int32, (sub, ROUTER_LANES), 1)
    lane = lane_i.astype(F32)
    hit1 = _each(lambda rt: lane == rt[0], routed)
    hit2 = _each(lambda rt: lane == rt[1], routed)
    both = _each(lambda a, b: a.astype(F32) + b.astype(F32), hit1, hit2)
    earlier = (lax.broadcasted_iota(jnp.int32, (sub, sub), 0) > lax.broadcasted_iota(jnp.int32, (sub, sub), 1))
    inside = _each(lambda b: jnp.dot(earlier.astype(BF16), _bf(b), preferred_element_type=F32), both)
    seen = seen_ref[...]
    for s, p in enumerate(parts):
        idx1, idx2, gate1, gate2 = routed[s]
        before = inside[s] + seen
        rank1 = jnp.sum(jnp.where(hit1[s], before, 0.0), axis=-1, keepdims=True)
        rank2 = jnp.sum(jnp.where(hit2[s], before, 0.0), axis=-1, keepdims=True)
        seen = seen + jnp.sum(both[s], axis=0, keepdims=True)
        route = jnp.zeros((sub, ROUTER_LANES), F32)
        for lane_id, col in ((LANE_IDX1, idx1), (LANE_IDX2, idx2), (LANE_RANK1, rank1),
                             (LANE_RANK2, rank2), (LANE_GATE1, gate1), (LANE_GATE2, gate2)):
            route = jnp.where(lane_i == lane_id, col, route)
        route_t_ref[:, p] = route.T[:SUBLANES, :]
        h_ref[p, :] = _bf(h[s])
        u_ref[p, :] = _pack_bf16_pairs(u[s])
    seen_ref[...] = seen
    count_ref[...] = jnp.broadcast_to(seen, count_ref.shape)


def _out_call(o, bonus, gate, yconv, x, ln_w, ln_b, w_out, g_ffn, router_w, router_b, tm, n_split):
    n_tok = x.shape[0]
    row = lambda width: pl.BlockSpec((tm, width), lambda i: (i, 0))
    full = lambda arr: pl.BlockSpec(arr.shape, lambda i: (0,) * arr.ndim, pipeline_mode=pl.Buffered(1))
    params = (ln_w, ln_b, w_out, g_ffn, router_w, router_b)
    return pl.pallas_call(
        functools.partial(_out_kernel, n_split=n_split),
        grid=(n_tok // tm,),
        in_specs=[row(D_RWKV)] * 4 + [row(D_MODEL)] + [full(p) for p in params],
        out_specs=[row(D_MODEL), row(D_MODEL // 2),
                   pl.BlockSpec((SUBLANES, tm), lambda i: (0, i)),
                   pl.BlockSpec((SUBLANES, ROUTER_LANES), lambda i: (0, 0))],
        out_shape=[jax.ShapeDtypeStruct((n_tok, D_MODEL), BF16),
                   jax.ShapeDtypeStruct((n_tok, D_MODEL // 2), jnp.int32),
                   jax.ShapeDtypeStruct((SUBLANES, n_tok), F32),
                   jax.ShapeDtypeStruct((SUBLANES, ROUTER_LANES), F32)],
        scratch_shapes=[pltpu.VMEM((1, ROUTER_LANES), F32), pltpu.VMEM(w_out.shape, BF16)],
        compiler_params=pltpu.CompilerParams(
            dimension_semantics=("arbitrary",), vmem_limit_bytes=VMEM_LIMIT),
        name="out_proj_route",
    )(o, bonus, gate, yconv, x, *params)


SC_CORES = 2
SC_SUBCORES = 16
SC_ROWS = 64


def _sc_mesh():
    return plsc.VectorSubcoreMesh(core_axis_name="c", subcore_axis_name="s",
                                  num_cores=SC_CORES, num_subcores=SC_SUBCORES)


def _sc_worker():
    return lax.axis_index("s") * SC_CORES + lax.axis_index("c")


def _sc_gather(table, idx):
    n_rows = idx.shape[0]
    width = table.shape[1]
    n_chunks = n_rows // (SC_CORES * SC_SUBCORES * SC_ROWS)

    def body(table_hbm, idx_hbm, out_hbm, idx_v, rows_v, gather_sem, write_sem):
        first = _sc_worker() * n_chunks
        pltpu.sync_copy(idx_hbm.at[pl.ds(first, n_chunks)], idx_v)
        gather = lambda j: pltpu.async_copy(table_hbm.at[idx_v.at[j]], rows_v.at[j % 2], gather_sem.at[j % 2])
        gathers = [gather(0)]
        writes = []
        for j in range(n_chunks):
            gathers[j].wait()
            if j + 1 < n_chunks:
                if j >= 1:
                    writes[j - 1].wait()
                gathers.append(gather(j + 1))
            dst = out_hbm.at[pl.ds(pl.multiple_of((first + j) * SC_ROWS, SC_ROWS), SC_ROWS)]
            writes.append(pltpu.async_copy(rows_v.at[j % 2], dst, write_sem.at[j % 2]))
        for j in range(max(n_chunks - 2, 0), n_chunks):
            writes[j].wait()

    return pl.kernel(
        body,
        out_type=jax.ShapeDtypeStruct((n_rows, width), table.dtype),
        mesh=_sc_mesh(),
        scratch_types=[pltpu.VMEM((n_chunks, SC_ROWS), jnp.int32), pltpu.VMEM((2, SC_ROWS, width), table.dtype),
                       pltpu.SemaphoreType.DMA((2,)), pltpu.SemaphoreType.DMA((2,))],
        name="sc_row_gather",
    )(table, idx.reshape(n_rows // SC_ROWS, SC_ROWS))


def _sc_scatter(rows, pos, n_out):
    n_rows, width = rows.shape
    n_slots = pos.shape[0] // n_rows
    slot_chunks = n_rows // SC_ROWS
    n_chunks = slot_chunks // (SC_CORES * SC_SUBCORES)

    def body(rows_hbm, pos_hbm, out_hbm, idx_v, rows_v, read_sem, scatter_sem):
        first = _sc_worker() * n_chunks
        for s in range(n_slots):
            pltpu.sync_copy(pos_hbm.at[pl.ds(s * slot_chunks + first, n_chunks)], idx_v.at[s])
        read = lambda j: pltpu.async_copy(
            rows_hbm.at[pl.ds(pl.multiple_of((first + j) * SC_ROWS, SC_ROWS), SC_ROWS)],
            rows_v.at[j % 2], read_sem.at[j % 2])
        reads = [read(0)]
        scatters = []
        for j in range(n_chunks):
            reads[j].wait()
            if j + 1 < n_chunks:
                if j >= 1:
                    for copy in scatters[j - 1]:
                        copy.wait()
                reads.append(read(j + 1))
            scatters.append([pltpu.async_copy(rows_v.at[j % 2], out_hbm.at[idx_v.at[s, j]], scatter_sem.at[j % 2])
                             for s in range(n_slots)])
        for j in range(max(n_chunks - 2, 0), n_chunks):
            for copy in scatters[j]:
                copy.wait()

    return pl.kernel(
        body,
        out_type=jax.ShapeDtypeStruct((n_out, width), rows.dtype),
        mesh=_sc_mesh(),
        scratch_types=[pltpu.VMEM((n_slots, n_chunks, SC_ROWS), jnp.int32),
                       pltpu.VMEM((2, SC_ROWS, width), rows.dtype),
                       pltpu.SemaphoreType.DMA((2,)), pltpu.SemaphoreType.DMA((2,))],
        name="sc_row_scatter",
    )(rows, pos.reshape(n_slots * slot_chunks, SC_ROWS))


def _expert_kernel(tile_expert_ref, n_valid_ref, slot_ref, next_expert_ref, x_ref, wg_hbm, wu_hbm, wd_hbm,
                   y_ref, wg_buf, wu_buf, wd_buf, sem):
    step = pl.program_id(0)
    expert = tile_expert_ref[step]
    slot = slot_ref[step]
    valid = step < n_valid_ref[0]
    first_of_expert = (step == 0) | (expert != tile_expert_ref[jnp.maximum(step - 1, 0)])

    def weight_copies(which, into):
        pairs = ((wg_hbm, wg_buf), (wu_hbm, wu_buf), (wd_hbm, wd_buf))
        return [pltpu.make_async_copy(w.at[which], buf.at[into], sem.at[into, j]) for j, (w, buf) in enumerate(pairs)]

    @pl.when(step == 0)
    def _():
        for copy in weight_copies(expert, slot):
            copy.start()

    @pl.when(valid & first_of_expert)
    def _():
        for copy in weight_copies(expert, slot):
            copy.wait()
        upcoming = next_expert_ref[step]

        @pl.when(upcoming < N_EXPERTS)
        def _():
            for copy in weight_copies(upcoming, 1 - slot):
                copy.start()

    @pl.when(valid)
    def _():
        x = _unpack_bf16_pairs(x_ref[...])
        gate = jnp.dot(x, wg_buf[slot], preferred_element_type=F32)
        up = jnp.dot(x, wu_buf[slot], preferred_element_type=F32)
        hid = gate * jax.nn.sigmoid(gate) * up
        y_ref[...] = _pack_bf16_pairs(jnp.dot(_bf(hid), wd_buf[slot], preferred_element_type=F32))


def _expert_call(tile_expert, n_valid, slot, next_expert, x_sorted, w_gate, w_up, w_down, tm):
    n_rows = x_sorted.shape[0]
    rows = pl.BlockSpec((tm, D_MODEL // 2), lambda i, te, nv, sl, nx: (jnp.minimum(i, nv[0] - 1), 0))
    in_hbm = pl.BlockSpec(memory_space=pl.ANY)
    return pl.pallas_call(
        _expert_kernel,
        grid_spec=pltpu.PrefetchScalarGridSpec(
            num_scalar_prefetch=4,
            grid=(n_rows // tm,),
            in_specs=[rows, in_hbm, in_hbm, in_hbm],
            out_specs=rows,
            scratch_shapes=[pltpu.VMEM((2,) + w_gate.shape[1:], BF16), pltpu.VMEM((2,) + w_up.shape[1:], BF16),
                            pltpu.VMEM((2,) + w_down.shape[1:], BF16), pltpu.SemaphoreType.DMA((2, 3))]),
        out_shape=jax.ShapeDtypeStruct((n_rows, D_MODEL // 2), jnp.int32),
        compiler_params=pltpu.CompilerParams(
            dimension_semantics=("arbitrary",), vmem_limit_bytes=VMEM_LIMIT),
        name="moe_experts",
    )(tile_expert, n_valid, slot, next_expert, x_sorted, w_gate, w_up, w_down)


def _final_kernel(h_ref, y1_ref, y2_ref, route_t_ref, gfin_ref, *rest):
    out_ref = rest[-1]
    route = route_t_ref[...].T
    gate1 = route[:, LANE_GATE1:LANE_GATE1 + 1]
    gate2 = route[:, LANE_GATE2:LANE_GATE2 + 1]
    moe = gate1 * _unpack_bf16_pairs(y1_ref[...]).astype(F32) + gate2 * _unpack_bf16_pairs(y2_ref[...]).astype(F32)
    out_ref[...] = _rms_norm(h_ref[...] + moe, gfin_ref[...])


def _final_call(h, y_pairs, route_t, g_final, earlier, first_token, tm):
    n_tok = h.shape[0]
    n_blocks = y_pairs.shape[0] // 2 // tm
    assert n_blocks * tm * 2 == y_pairs.shape[0] and first_token % tm == 0, "chunks must be whole row blocks"
    first = first_token // tm
    in_specs = [pl.BlockSpec((tm, D_MODEL), lambda i: (first + i, 0)),
                pl.BlockSpec((tm, D_MODEL // 2), lambda i: (i, 0)),
                pl.BlockSpec((tm, D_MODEL // 2), lambda i: (i + n_blocks, 0)),
                pl.BlockSpec((SUBLANES, tm), lambda i: (0, first + i)),
                pl.BlockSpec((1, D_MODEL), lambda i: (0, 0))]
    args = [h, y_pairs, y_pairs, route_t, g_final]
    aliases = {}
    if earlier is not None:
        in_specs.append(pl.BlockSpec(memory_space=pl.ANY))
        args.append(earlier)
        aliases = {len(args) - 1: 0}
    return pl.pallas_call(
        _final_kernel,
        grid=(n_blocks,),
        in_specs=in_specs,
        out_specs=pl.BlockSpec((tm, D_MODEL), lambda i: (first + i, 0)),
        out_shape=jax.ShapeDtypeStruct((n_tok, D_MODEL), F32),
        input_output_aliases=aliases,
        compiler_params=pltpu.CompilerParams(
            dimension_semantics=("arbitrary",), vmem_limit_bytes=VMEM_LIMIT),
        name="moe_combine_norm",
    )(*args)


def _dispatch_plan(route_t, counts, tm):
    n_tok = route_t.shape[1]
    n_tiles = (2 * n_tok) // tm + N_EXPERTS
    counts = counts[0, :N_EXPERTS].astype(jnp.int32)
    tiles_per = (counts + tm - 1) // tm
    tile_end = jnp.cumsum(tiles_per)
    row_start = (tile_end - tiles_per) * tm
    experts = jnp.arange(N_EXPERTS, dtype=jnp.int32)

    def position(idx_lane, rank_lane):
        idx = route_t[idx_lane].astype(jnp.int32)
        start = jnp.sum(jnp.where(idx[None, :] == experts[:, None], row_start[:, None], 0), axis=0)
        return start + route_t[rank_lane].astype(jnp.int32)

    pos = jnp.concatenate([position(LANE_IDX1, LANE_RANK1), position(LANE_IDX2, LANE_RANK2)])
    n_valid = tile_end[-1:]
    tile = jnp.minimum(jnp.arange(n_tiles, dtype=jnp.int32), n_valid - 1)
    tile_expert = jnp.sum((tile_end[None, :] <= tile[:, None]).astype(jnp.int32), axis=1)
    present = tiles_per > 0
    later = (experts[None, :] > experts[:, None]) & present[None, :]
    next_present = jnp.min(jnp.where(later, experts[None, :], N_EXPERTS), axis=1)
    order = jnp.cumsum(present.astype(jnp.int32)) - 1
    of_tile = lambda per_expert: jnp.sum(jnp.where(tile_expert[:, None] == experts[None, :], per_expert[None, :], 0), axis=1)
    return pos, n_tiles, tile_expert, n_valid, of_tile(order) % 2, of_tile(next_present)


def _block(x, norm_mix_g, w_in, rwkv_mu, conv_w, decay_up, decay_base, aaa_up, aaa_base, gate_up,
           k_k, k_a, r_k, ln_x_w, ln_x_b, w_out, norm_ffn_g, router_group_w, router_group_b,
           router_expert_w, router_expert_b, expert_w_gate, expert_w_up, expert_w_down, norm_final_g,
           *, tm_in, tb_rwkv, tm_out, tm_expert, tm_final):
    bsz, seq, d_model = x.shape
    n_tok = bsz * seq
    row = lambda p: p.reshape(1, -1)

    yconv, r, lw, k2, v, kk, kka, gate, bonus = _in_call(
        x, row(norm_mix_g), w_in, row(rwkv_mu), conv_w, decay_up, row(decay_base), aaa_up, row(aaa_base),
        gate_up, row(k_k), row(k_a), row(r_k), tm_in)
    o, w_gate_bf, w_up_bf, w_down_bf = _rwkv_call(r, lw, k2, v, kk, kka,
                                                  (expert_w_gate, expert_w_up, expert_w_down), tb_rwkv)

    pad = ROUTER_LANES - N_EXPERTS - N_GROUPS
    router_w = jnp.concatenate([router_expert_w, router_group_w, jnp.zeros((d_model, pad), F32)], axis=1)
    router_hi = _bf(router_w)
    router_split = jnp.concatenate([router_hi, _bf(router_w - router_hi.astype(F32))], axis=1)
    router_b = jnp.concatenate([router_expert_b, router_group_b, jnp.zeros((pad,), F32)]).reshape(1, -1)
    flat = lambda t: t.reshape(n_tok, t.shape[-1])
    h, u_pairs, route_t, counts = _out_call(
        flat(o), flat(bonus), flat(gate), flat(yconv), flat(x),
        row(ln_x_w), row(ln_x_b), w_out, row(norm_ffn_g), router_split, router_b, tm_out, OUT_SPLIT)

    pos, n_tiles, tile_expert, n_valid, slot, next_expert = _dispatch_plan(route_t, counts, tm_expert)
    x_sorted = _sc_scatter(u_pairs, pos, n_tiles * tm_expert)
    y_sorted = _expert_call(tile_expert, n_valid, slot, next_expert, x_sorted, w_gate_bf, w_up_bf, w_down_bf,
                            tm_expert)
    out, lo = None, 0
    for fraction in COMBINE_SPLIT:
        size = n_tok // fraction
        pos_c = jnp.concatenate([pos[lo:lo + size], pos[n_tok + lo:n_tok + lo + size]])
        out = _final_call(h, _sc_gather(y_sorted, pos_c), route_t, row(norm_final_g), out, lo, tm_final)
        lo += size
    assert lo == n_tok, "COMBINE_SPLIT must cover all tokens"
    return out.reshape(bsz, seq, d_model)


def kernel(x, norm_mix_g, w_in, rwkv_mu, conv_w, decay_up, decay_base, aaa_up, aaa_base, gate_up, k_k, k_a, r_k, ln_x_w, ln_x_b, w_out, norm_ffn_g, router_group_w, router_group_b, router_expert_w, router_expert_b, expert_w_gate, expert_w_up, expert_w_down, norm_final_g):
    return _block(x, norm_mix_g[0], w_in[0], rwkv_mu[0], conv_w[0], decay_up[0], decay_base[0],
                  aaa_up[0], aaa_base[0], gate_up[0], k_k[0], k_a[0], r_k[0].reshape(-1), ln_x_w[0],
                  ln_x_b[0], w_out[0], norm_ffn_g[0], router_group_w[0], router_group_b[0],
                  router_expert_w[0], router_expert_b[0], expert_w_gate[0], expert_w_up[0],
                  expert_w_down[0], norm_final_g,
                  tm_in=512, tb_rwkv=1024, tm_out=1024, tm_expert=512, tm_final=1024)
```

```python
import functools

import jax
import jax.numpy as jnp
from jax import lax
from jax.experimental import pallas as pl
from jax.experimental.pallas import tpu as pltpu
from jax.experimental.pallas import tpu_sc as plsc

F32 = jnp.float32
BF16 = jnp.bfloat16

D_MODEL = 1024
D_CONV = 512
CONV_WIDTH = 3
N_HEADS = 8
HEAD = 64
D_RWKV = N_HEADS * HEAD
LORA_WA = 128
GATE_LORA = 128
D_RWKV_PROJ = 3 * D_RWKV + LORA_WA + GATE_LORA
D_IN = 3 * D_CONV + D_RWKV_PROJ
N_GROUPS = 4
EXPERTS_PER_GROUP = 8
N_EXPERTS = N_GROUPS * EXPERTS_PER_GROUP
D_EXPERT = D_MODEL // 4
RMS_EPS = 1e-6
LN_X_EPS = 64e-5
L2_EPS = 1e-12

SUBLANES = 8
CHUNK = 64
QUAD = 4 * HEAD
GROUP = 2 * HEAD
INV_BASE = 8
WAVE = 4
PRODUCTS_PER_CHAIN = 15
COMPACT_FROM = 64
OUT_SPLIT = 8
COMBINE_SPLIT = (4, 4, 4, 4)
ROUTER_LANES = 128

VMEM_LIMIT = 56 * 1024 * 1024


def _bf(x):
    return x.astype(BF16)


class _Interleaver:
    def __init__(self):
        self.stages, self.every, self.calls, self.busy = None, 1, 0, False

    def start(self, stages, every):
        self.stages, self.every, self.calls = stages, every, 0

    def tick(self):
        if self.stages is None or self.busy:
            return
        self.calls += 1
        if self.calls % self.every == 0:
            self.busy = True
            next(self.stages, None)
            self.busy = False

    def drain(self):
        self.busy = True
        for _ in self.stages:
            pass
        self.stages, self.busy = None, False


_INTERLEAVE = _Interleaver()


def _mm(a, b):
    _INTERLEAVE.tick()
    return jnp.dot(_bf(a), _bf(b), preferred_element_type=F32)


def _mm_nt(a, b):
    _INTERLEAVE.tick()
    return lax.dot_general(_bf(a), _bf(b), (((1,), (1,)), ((), ())), preferred_element_type=F32)


def _mm_exact_lhs(lhs_bf16, x, passes):
    acc = None
    rem = x
    for _ in range(passes):
        piece = _bf(rem)
        part = jnp.dot(lhs_bf16, piece, preferred_element_type=F32)
        acc = part if acc is None else acc + part
        rem = rem - piece.astype(F32)
    return acc


def _head_ones():
    return (lax.broadcasted_iota(jnp.int32, (QUAD, QUAD), 0) // HEAD
            == lax.broadcasted_iota(jnp.int32, (QUAD, QUAD), 1) // HEAD).astype(BF16)


def _head_sum(x, ones_quad):
    xb = _bf(x)
    return jnp.concatenate(
        [jnp.dot(xb[:, q * QUAD:(q + 1) * QUAD], ones_quad, preferred_element_type=F32)
         for q in range(x.shape[1] // QUAD)], axis=1)


def _rms_norm(x, g):
    return x * lax.rsqrt(jnp.mean(x * x, axis=-1, keepdims=True) + RMS_EPS) * g


def _shift_rows(cur, prev_rows, k):
    rolled = pltpu.roll(cur, k, 0)
    prev_rolled = pltpu.roll(prev_rows, k, 0)
    n = cur.shape[0]
    head = jnp.concatenate([prev_rolled, rolled[SUBLANES:]], axis=0) if n > SUBLANES else prev_rolled
    row = lax.broadcasted_iota(jnp.int32, cur.shape, 0)
    return jnp.where(row < k, head, rolled)


def _in_kernel(x_ref, g_ref, w_ref, mu_ref, convw_ref, dup_ref, dbase_ref, aup_ref, abase_ref,
               gup_ref, kk_ref, ka_ref, rk_ref,
               yconv_ref, r_ref, lw_ref, k_ref, v_ref, kkn_ref, kka_ref, gate_ref, bonus_ref,
               carry_ref, wbf_ref):
    @pl.when((pl.program_id(0) == 0) & (pl.program_id(1) == 0))
    def _():
        wbf_ref[...] = _bf(w_ref[...])

    @pl.when(pl.program_id(1) == 0)
    def _():
        carry_ref[...] = jnp.zeros_like(carry_ref)

    u = _bf(_rms_norm(x_ref[0], g_ref[...]))
    n_conv = 3 * D_CONV
    rk0 = n_conv
    bounds = dict(conv=(0, n_conv), lora=(rk0 + 3 * D_RWKV, D_IN), k=(rk0 + D_RWKV, rk0 + 2 * D_RWKV),
                  r=(rk0, rk0 + D_RWKV), v=(rk0 + 2 * D_RWKV, rk0 + 3 * D_RWKV))
    z = {name: jnp.dot(u, wbf_ref[:, lo:hi], preferred_element_type=F32) for name, (lo, hi) in bounds.items()}

    def lerp(name):
        lo, hi = bounds[name]
        cur = z[name]
        prev = carry_ref[:, D_CONV + lo - n_conv:D_CONV + hi - n_conv]
        mixed = cur + (_shift_rows(cur, prev, 1) - cur) * mu_ref[:, lo - n_conv:hi - n_conv]
        carry_ref[:, D_CONV + lo - n_conv:D_CONV + hi - n_conv] = cur[-SUBLANES:]
        return mixed

    b_gate = z["conv"][:, :D_CONV]
    ch = z["conv"][:, D_CONV:2 * D_CONV] * z["conv"][:, 2 * D_CONV:]
    prev_ch = carry_ref[:, :D_CONV]
    conv = convw_ref[CONV_WIDTH - 1:CONV_WIDTH, :] * ch
    for delay in range(1, CONV_WIDTH):
        tap = CONV_WIDTH - 1 - delay
        conv = conv + convw_ref[tap:tap + 1, :] * _shift_rows(ch, prev_ch, delay)
    yconv_ref[0] = _bf(b_gate * conv)
    carry_ref[:, :D_CONV] = ch[-SUBLANES:]

    lora = lerp("lora")
    wa_lo = lora[:, :LORA_WA]
    g_lo = lora[:, LORA_WA:]
    no_rows = jnp.zeros_like(dup_ref[...])
    decay_up = _bf(jnp.concatenate([dup_ref[...], no_rows], axis=0))
    rate_up = _bf(jnp.concatenate([no_rows, aup_ref[...]], axis=0))
    dec_in = -(dbase_ref[...] + jnp.dot(_bf(jnp.tanh(wa_lo)), decay_up, preferred_element_type=F32))
    softplus = jnp.maximum(dec_in, 0.0) + jnp.log(1.0 + jnp.exp(-jnp.abs(dec_in)))
    w = -softplus - 0.5
    lw_ref[0] = -jnp.exp(w)
    a = jax.nn.sigmoid(abase_ref[...] + jnp.dot(_bf(wa_lo), rate_up, preferred_element_type=F32))
    gate_ref[0] = _bf(jnp.dot(_bf(jax.nn.sigmoid(g_lo)), _bf(gup_ref[...]), preferred_element_type=F32))

    ones_quad = _head_ones()
    k = lerp("k")
    kk = k * kk_ref[...]
    norm = jnp.sqrt(_head_sum(kk * kk, ones_quad))
    kk = kk / jnp.maximum(norm, L2_EPS)
    k2 = k * (1.0 + (a - 1.0) * ka_ref[...])
    k_ref[0] = _bf(k2)
    kkn_ref[0] = _bf(kk)
    kka_ref[0] = _bf(kk * a)
    r = lerp("r")
    r_ref[0] = _bf(r)
    v = lerp("v")
    v_ref[0] = _bf(v)
    bonus_ref[0] = _bf(_head_sum(r * k2 * rk_ref[...], ones_quad) * v)


def _in_call(x, g, w_in, mu, conv_w, dup, dbase, aup, abase, gup, k_k, k_a, r_k, tm):
    bsz, seq, _ = x.shape
    full = lambda arr: pl.BlockSpec(arr.shape, lambda b, t: (0,) * arr.ndim, pipeline_mode=pl.Buffered(1))
    out_spec = pl.BlockSpec((1, tm, D_RWKV), lambda b, t: (b, t, 0))
    out_dtypes = (BF16, BF16, F32, BF16, BF16, BF16, BF16, BF16, BF16)
    params = (g, w_in, mu, conv_w, dup, dbase, aup, abase, gup, k_k, k_a, r_k)
    return pl.pallas_call(
        _in_kernel,
        grid=(bsz, seq // tm),
        in_specs=[pl.BlockSpec((1, tm, D_MODEL), lambda b, t: (b, t, 0))] + [full(p) for p in params],
        out_specs=[out_spec] * 9,
        out_shape=[jax.ShapeDtypeStruct((bsz, seq, D_RWKV), dt) for dt in out_dtypes],
        scratch_shapes=[pltpu.VMEM((SUBLANES, D_CONV + D_RWKV_PROJ), F32), pltpu.VMEM(w_in.shape, BF16)],
        compiler_params=pltpu.CompilerParams(
            dimension_semantics=("arbitrary", "arbitrary"), vmem_limit_bytes=VMEM_LIMIT),
        name="in_proj",
    )(x, *params)


def _block_diag(y, bd_mask):
    return jnp.where(bd_mask, jnp.concatenate([y] * (GROUP // CHUNK), axis=0), 0.0)


def _each(fn, *lists):
    return [fn(*args) for args in zip(*lists)]


def _unit_lower_inverse(a_strict, t_idx, s_idx, bd):
    bdmm = lambda xs, ys: _each(lambda x, y: _mm(x, bd(y)), xs, ys)
    eye = (t_idx == s_idx).astype(F32)
    same8 = (t_idx // INV_BASE) == (s_idx // INV_BASE)
    a8 = _each(lambda a: jnp.where(same8, a, 0.0), a_strict)
    a8_2 = bdmm(a8, a8)
    a8_34 = bdmm(_each(lambda a, b: jnp.concatenate([a, b], axis=0), a8, a8_2), a8_2)
    inv = _each(lambda a, b, c: eye + a + b + c[:CHUNK], a8, a8_2, a8_34)
    inv = _each(jnp.add, inv, bdmm(inv, _each(lambda c: c[CHUNK:], a8_34)))
    size = 2 * INV_BASE
    while size < COMPACT_FROM:
        off = ((t_idx // size) == (s_idx // size)) & ((t_idx // (size // 2)) != (s_idx // (size // 2)))
        cross = bdmm(_each(lambda a: jnp.where(off, a, 0.0), a_strict), inv)
        inv = _each(jnp.add, inv, bdmm(inv, cross))
        size *= 2
    while size <= CHUNK:
        inv = _compact_level(a_strict, inv, size)
        size *= 2
    return inv


def _compact_level(a_strict, inv, size):
    half = size // 2
    heads = GROUP // CHUNK
    blocks = CHUNK // size
    lane_starts = [h * CHUNK + b * size for h in range(heads) for b in range(blocks)]
    row_starts = [b * size for b in range(blocks)]
    rows_first = lambda x: jnp.concatenate([x[r:r + half] for r in row_starts], axis=0)
    rows_second = lambda x: jnp.concatenate([x[r + half:r + size] for r in row_starts], axis=0)
    lanes_first = lambda x: jnp.concatenate([x[:, c:c + half] for c in lane_starts], axis=1)
    lanes_second = lambda x: jnp.concatenate([x[:, c + half:c + size] for c in lane_starts], axis=1)
    n_rows, n_lanes = CHUNK // 2, GROUP // 2
    own = (lax.broadcasted_iota(jnp.int32, (n_rows, n_lanes), 0) // half
           == (lax.broadcasted_iota(jnp.int32, (n_rows, n_lanes), 1) // half) % blocks)
    diag = (lax.broadcasted_iota(jnp.int32, (n_lanes, n_lanes), 0) // half
            == lax.broadcasted_iota(jnp.int32, (n_lanes, n_lanes), 1) // half)
    bd_half = lambda y: jnp.where(diag, jnp.concatenate([y] * heads, axis=0), 0.0)
    a21 = _each(lambda a: jnp.where(own, lanes_first(rows_second(a)), 0.0), a_strict)
    t11 = _each(lambda t: lanes_first(rows_first(t)), inv)
    t22 = _each(lambda t: lanes_second(rows_second(t)), inv)
    a21_t11 = _each(lambda x, y: _mm(x, bd_half(y)), a21, t11)
    new = _each(lambda x, y: _mm(x, bd_half(y)), t22, a21_t11)

    def placed(x):
        zero_lanes = jnp.zeros((n_rows, half), F32)
        wide = jnp.concatenate(sum(([x[:, i * half:(i + 1) * half], zero_lanes] for i in range(heads * blocks)), []),
                               axis=1)
        zero_rows = jnp.zeros((half, GROUP), F32)
        return jnp.concatenate(sum(([zero_rows, wide[b * half:(b + 1) * half]] for b in range(blocks)), []), axis=0)

    return _each(lambda t, x: t + placed(x), inv, new)


def _chunk_local(r, lw, k, v, kk, kka, tri, t_idx, s_idx, bd_mask):
    bd = lambda y: _block_diag(y, bd_mask)
    bdmm = lambda xs, ys: _each(lambda x, y: _mm(x, bd(y)), xs, ys)
    bdmm2 = lambda xs, ys, zs: _each(lambda x, y, z: _mm(x, jnp.concatenate([bd(y), bd(z)], axis=1)), xs, ys, zs)
    left = lambda xs: _each(lambda x: x[:, :GROUP], xs)
    right = lambda xs: _each(lambda x: x[:, GROUP:], xs)
    top = lambda xs: _each(lambda x: x[:CHUNK], xs)
    bottom = lambda xs: _each(lambda x: x[CHUNK:], xs)

    cum = _each(lambda x: _mm_exact_lhs(tri, x, 3), lw)
    cum_last = _each(lambda c: c[CHUNK - 1:CHUNK, :], cum)
    p_incl = _each(jnp.exp, cum)
    p_excl = _each(lambda c, x: jnp.exp(c - x), cum, lw)
    p_inv = _each(lambda c: jnp.exp(-c), cum)
    to_end = _each(lambda cl, c: jnp.exp(cl - c), cum_last, cum)
    a_t = _each(lambda x, p: -x * p, kk, p_excl)
    r_t = _each(jnp.multiply, r, p_incl)
    b_t = _each(jnp.multiply, kka, p_inv)
    k_t = _each(jnp.multiply, k, p_inv)
    b_end = _each(jnp.multiply, kka, to_end)
    k_end = _each(jnp.multiply, k, to_end)

    ar = _each(lambda a, b: jnp.concatenate([a, b], axis=0), a_t, r_t)
    row2 = lax.broadcasted_iota(jnp.int32, (2 * CHUNK, 2 * GROUP), 0)
    col2 = lax.broadcasted_iota(jnp.int32, (2 * CHUNK, 2 * GROUP), 1) % CHUNK
    causal = col2 < (row2 % CHUNK) + (row2 // CHUNK)
    scores = _each(lambda x, y, z: jnp.where(causal, _mm_nt(x, jnp.concatenate([bd(y), bd(z)], axis=0)), 0.0),
                   ar, b_t, k_t)
    a_ab = top(left(scores))
    a_rb = bottom(left(scores))

    inv = _unit_lower_inverse(a_ab, t_idx, s_idx, bd)
    kv = bdmm(right(scores), v)
    wu = bdmm2(inv, a_t, top(kv))
    ro = bdmm2(a_rb, left(wu), right(wu))
    r_hat = _each(jnp.add, r_t, left(ro))
    o_loc = _each(jnp.add, right(ro), bottom(kv))

    wu_b = _each(lambda x, b: _mm(x.T, b), wu, b_end)
    trans = _each(lambda x: jnp.where(bd_mask, x[:GROUP], 0.0), wu_b)
    d_bd = _each(lambda x, y, z: jnp.where(bd_mask, x[GROUP:] + _mm(y.T, z), 0.0), wu_b, v, k_end)
    d_ls = _each(lambda d: sum(d[h * CHUNK:(h + 1) * CHUNK] for h in range(1, GROUP // CHUNK)) + d[:CHUNK], d_bd)
    p_end = _each(jnp.exp, cum_last)
    return r_hat, o_loc, trans, d_ls, p_end


def _rwkv_kernel(r_ref, lw_ref, k_ref, v_ref, kk_ref, kka_ref, wg_ref, wu_ref, wd_ref,
                 o_ref, wg_bf_ref, wu_bf_ref, wd_bf_ref, state_ref, *, n_sub):
    @pl.when(pl.program_id(1) == 0)
    def _():
        state_ref[...] = jnp.zeros_like(state_ref)

    wg_bf_ref[...] = _bf(wg_ref[...])
    wu_bf_ref[...] = _bf(wu_ref[...])
    wd_bf_ref[...] = _bf(wd_ref[...])

    t_idx = lax.broadcasted_iota(jnp.int32, (CHUNK, GROUP), 0)
    s_idx = lax.broadcasted_iota(jnp.int32, (CHUNK, GROUP), 1) % CHUNK
    bd_mask = (lax.broadcasted_iota(jnp.int32, (GROUP, GROUP), 0) // CHUNK
               == lax.broadcasted_iota(jnp.int32, (GROUP, GROUP), 1) // CHUNK)
    chunk_row = lax.broadcasted_iota(jnp.int32, (CHUNK, CHUNK), 0)
    chunk_col = lax.broadcasted_iota(jnp.int32, (CHUNK, CHUNK), 1)
    n_group = D_RWKV // GROUP
    tri = (chunk_row >= chunk_col).astype(BF16)
    state = [state_ref[g] for g in range(n_group)]

    def through_state(local, where):
        r_hat, o_loc, trans, d_ls, p_end = local
        for first in range(0, len(where), n_group):
            chains = range(first, first + n_group)
            out = [_mm_nt(r_hat[i], _block_diag(state[g], bd_mask)) + o_loc[i] for g, i in enumerate(chains)]
            for g, i in enumerate(chains):
                rows, lanes = where[i]
                o_ref[0, rows, lanes] = _bf(out[g])
            yield
            state[:] = [state[g] * p_end[i] + _mm(state[g], trans[i]) + d_ls[i] for g, i in enumerate(chains)]
            yield

    pending = None
    for start in range(0, n_sub, WAVE):
        where = [(slice(c * CHUNK, (c + 1) * CHUNK), slice(g * GROUP, (g + 1) * GROUP))
                 for c in range(start, min(start + WAVE, n_sub)) for g in range(n_group)]
        load = lambda ref: [ref[0, rows, lanes].astype(F32) for rows, lanes in where]
        if pending is not None:
            _INTERLEAVE.start(pending, len(where) * PRODUCTS_PER_CHAIN // (2 * WAVE + 1))
        local = _chunk_local(load(r_ref), load(lw_ref), load(k_ref), load(v_ref), load(kk_ref), load(kka_ref),
                             tri, t_idx, s_idx, bd_mask)
        if pending is not None:
            _INTERLEAVE.drain()
        pending = through_state(local, where)
    for _ in pending:
        pass
    for g in range(n_group):
        state_ref[g] = state[g]


def _rwkv_call(r, lw, k, v, kk, kka, expert_weights, tb):
    bsz, seq, _ = r.shape
    n_t = seq // tb
    per_step = N_EXPERTS // (bsz * n_t)
    assert per_step * bsz * n_t == N_EXPERTS, "grid steps must divide the expert count"
    spec = pl.BlockSpec((1, tb, D_RWKV), lambda b, t: (b, t, 0))
    w_specs = [pl.BlockSpec((per_step,) + w.shape[1:], lambda b, t: (b * n_t + t, 0, 0)) for w in expert_weights]
    return pl.pallas_call(
        functools.partial(_rwkv_kernel, n_sub=tb // CHUNK),
        grid=(bsz, n_t),
        in_specs=[spec] * 6 + w_specs,
        out_specs=[spec] + w_specs,
        out_shape=[jax.ShapeDtypeStruct((bsz, seq, D_RWKV), BF16)]
        + [jax.ShapeDtypeStruct(w.shape, BF16) for w in expert_weights],
        scratch_shapes=[pltpu.VMEM((D_RWKV // GROUP, HEAD, GROUP), F32)],
        compiler_params=pltpu.CompilerParams(
            dimension_semantics=("arbitrary", "arbitrary"), vmem_limit_bytes=VMEM_LIMIT),
        name="rwkv_chunk",
    )(r, lw, k, v, kk, kka, *expert_weights)


def _route(logits):
    lane_i = lax.broadcasted_iota(jnp.int32, logits.shape, 1)
    lane = lane_i.astype(F32)
    lane_group = (lane_i // EXPERTS_PER_GROUP).astype(F32)
    neg = -jnp.inf
    big = float(ROUTER_LANES)
    is_group = (lane_i >= N_EXPERTS) & (lane_i < N_EXPERTS + N_GROUPS)
    gl = jnp.where(is_group, logits, neg)
    ge = jnp.exp(gl - jnp.max(gl, axis=-1, keepdims=True))
    gprob = ge / jnp.sum(ge, axis=-1, keepdims=True)
    g_top = jnp.max(gprob, axis=-1, keepdims=True)
    g_idx = jnp.min(jnp.where(is_group & (gprob == g_top), lane - N_EXPERTS, big), axis=-1, keepdims=True)

    in_group = (lane_i < N_EXPERTS) & (lane_group == g_idx)
    el = jnp.where(in_group, logits, neg)
    top1 = jnp.max(el, axis=-1, keepdims=True)
    idx1 = jnp.min(jnp.where(in_group & (el == top1), lane, big), axis=-1, keepdims=True)
    el2 = jnp.where(lane == idx1, neg, el)
    top2 = jnp.max(el2, axis=-1, keepdims=True)
    idx2 = jnp.min(jnp.where(in_group & (lane != idx1) & (el2 == top2), lane, big), axis=-1, keepdims=True)
    e2 = jnp.exp(top2 - top1)
    denom = 1.0 + e2
    return idx1, idx2, g_top * (1.0 / denom), g_top * (e2 / denom)


def _pack_bf16_pairs(x):
    bits = lax.bitcast_convert_type(_bf(x).astype(F32), jnp.uint32)
    k = x.shape[1] // 2
    return lax.bitcast_convert_type((bits[:, :k] >> 16) | bits[:, k:], jnp.int32)


def _unpack_bf16_pairs(words):
    bits = lax.bitcast_convert_type(words, jnp.uint32)
    lo = lax.bitcast_convert_type(bits << 16, F32)
    hi = lax.bitcast_convert_type(bits & jnp.uint32(0xFFFF0000), F32)
    return _bf(jnp.concatenate([lo, hi], axis=1))


LANE_IDX1, LANE_IDX2, LANE_RANK1, LANE_RANK2, LANE_GATE1, LANE_GATE2 = range(6)


def _out_kernel(o_ref, bonus_ref, gate_ref, yconv_ref, x_ref, lnw_ref, lnb_ref,
                wout_ref, gffn_ref, rw_ref, rb_ref, h_ref, u_ref, route_ref, route_t_ref, count_ref,
                seen_ref, wbf_ref, *, n_split):
    @pl.when(pl.program_id(0) == 0)
    def _():
        seen_ref[...] = jnp.zeros_like(seen_ref)
        wbf_ref[...] = _bf(wout_ref[...])

    sub = o_ref.shape[0] // n_split
    parts = [slice(s * sub, (s + 1) * sub) for s in range(n_split)]
    read = lambda ref: [ref[p, :] for p in parts]
    ones_quad = _head_ones()
    inv_n = 1.0 / HEAD
    o = _each(lambda v: v.astype(F32), read(o_ref))
    mean = _each(lambda v: _head_sum(v, ones_quad) * inv_n, o)
    cen = _each(jnp.subtract, o, mean)
    var = _each(lambda c: _head_sum(c * c, ones_quad) * inv_n, cen)
    on = _each(lambda c, v: c * lax.rsqrt(v + LN_X_EPS) * lnw_ref[...] + lnb_ref[...], cen, var)
    y_rwkv = _each(lambda a, b, g: (a + b) * g, on, read(bonus_ref), read(gate_ref))
    mixed = _each(lambda yc, yr: (jnp.dot(yc, wbf_ref[:D_CONV, :], preferred_element_type=F32)
                                  + jnp.dot(_bf(yr), wbf_ref[D_CONV:, :], preferred_element_type=F32)),
                  read(yconv_ref), y_rwkv)
    h = _each(jnp.add, read(x_ref), mixed)
    u = _each(lambda v: _rms_norm(v, gffn_ref[...]), h)
    u_hi = _each(_bf, u)
    u_lo = _each(lambda a, b: _bf(a - b.astype(F32)), u, u_hi)
    by_hi = _each(lambda a: jnp.dot(a, rw_ref[...], preferred_element_type=F32), u_hi)
    by_lo = _each(lambda a: jnp.dot(a, rw_ref[:, :ROUTER_LANES], preferred_element_type=F32), u_lo)
    logits = _each(lambda a, b: a[:, :ROUTER_LANES] + a[:, ROUTER_LANES:] + b + rb_ref[...], by_hi, by_lo)
    routed = _each(_route, logits)

    lane_i = lax.broadcasted_iota(jnp.int32, (sub, ROUTER_LANES), 1)
    lane = lane_i.astype(F32)
    hit1 = _each(lambda rt: lane == rt[0], routed)
    hit2 = _each(lambda rt: lane == rt[1], routed)
    both = _each(lambda a, b: a.astype(F32) + b.astype(F32), hit1, hit2)
    earlier = (lax.broadcasted_iota(jnp.int32, (sub, sub), 0) > lax.broadcasted_iota(jnp.int32, (sub, sub), 1))
    inside = _each(lambda b: jnp.dot(earlier.astype(BF16), _bf(b), preferred_element_type=F32), both)
    seen = seen_ref[...]
    for s, p in enumerate(parts):
        idx1, idx2, gate1, gate2 = routed[s]
        before = inside[s] + seen
        rank1 = jnp.sum(jnp.where(hit1[s], before, 0.0), axis=-1, keepdims=True)
        rank2 = jnp.sum(jnp.where(hit2[s], before, 0.0), axis=-1, keepdims=True)
        seen = seen + jnp.sum(both[s], axis=0, keepdims=True)
        route = jnp.zeros((sub, ROUTER_LANES), F32)
        for lane_id, col in ((LANE_IDX1, idx1), (LANE_IDX2, idx2), (LANE_RANK1, rank1),
                             (LANE_RANK2, rank2), (LANE_GATE1, gate1), (LANE_GATE2, gate2)):
            route = jnp.where(lane_i == lane_id, col, route)
        route_ref[p, :] = route
        route_t_ref[:, p] = route.T[:SUBLANES, :]
        h_ref[p, :] = _bf(h[s])
        u_ref[p, :] = _pack_bf16_pairs(u[s])
    seen_ref[...] = seen
    count_ref[...] = jnp.broadcast_to(seen, count_ref.shape)


def _out_call(o, bonus, gate, yconv, x, ln_w, ln_b, w_out, g_ffn, router_w, router_b, tm, n_split):
    n_tok = x.shape[0]
    row = lambda width: pl.BlockSpec((tm, width), lambda i: (i, 0))
    full = lambda arr: pl.BlockSpec(arr.shape, lambda i: (0,) * arr.ndim, pipeline_mode=pl.Buffered(1))
    params = (ln_w, ln_b, w_out, g_ffn, router_w, router_b)
    return pl.pallas_call(
        functools.partial(_out_kernel, n_split=n_split),
        grid=(n_tok // tm,),
        in_specs=[row(D_RWKV)] * 4 + [row(D_MODEL)] + [full(p) for p in params],
        out_specs=[row(D_MODEL), row(D_MODEL // 2), row(ROUTER_LANES),
                   pl.BlockSpec((SUBLANES, tm), lambda i: (0, i)),
                   pl.BlockSpec((SUBLANES, ROUTER_LANES), lambda i: (0, 0))],
        out_shape=[jax.ShapeDtypeStruct((n_tok, D_MODEL), BF16),
                   jax.ShapeDtypeStruct((n_tok, D_MODEL // 2), jnp.int32),
                   jax.ShapeDtypeStruct((n_tok, ROUTER_LANES), F32),
                   jax.ShapeDtypeStruct((SUBLANES, n_tok), F32),
                   jax.ShapeDtypeStruct((SUBLANES, ROUTER_LANES), F32)],
        scratch_shapes=[pltpu.VMEM((1, ROUTER_LANES), F32), pltpu.VMEM(w_out.shape, BF16)],
        compiler_params=pltpu.CompilerParams(
            dimension_semantics=("arbitrary",), vmem_limit_bytes=VMEM_LIMIT),
        name="out_proj_route",
    )(o, bonus, gate, yconv, x, *params)


SC_CORES = 2
SC_SUBCORES = 16
SC_ROWS = 64


def _sc_mesh():
    return plsc.VectorSubcoreMesh(core_axis_name="c", subcore_axis_name="s",
                                  num_cores=SC_CORES, num_subcores=SC_SUBCORES)


def _sc_worker():
    return lax.axis_index("s") * SC_CORES + lax.axis_index("c")


def _sc_gather(table, idx):
    n_rows = idx.shape[0]
    width = table.shape[1]
    n_chunks = n_rows // (SC_CORES * SC_SUBCORES * SC_ROWS)

    def body(table_hbm, idx_hbm, out_hbm, idx_v, rows_v, gather_sem, write_sem):
        first = _sc_worker() * n_chunks
        pltpu.sync_copy(idx_hbm.at[pl.ds(first, n_chunks)], idx_v)
        gather = lambda j: pltpu.async_copy(table_hbm.at[idx_v.at[j]], rows_v.at[j % 2], gather_sem.at[j % 2])
        gathers = [gather(0)]
        writes = []
        for j in range(n_chunks):
            gathers[j].wait()
            if j + 1 < n_chunks:
                if j >= 1:
                    writes[j - 1].wait()
                gathers.append(gather(j + 1))
            dst = out_hbm.at[pl.ds(pl.multiple_of((first + j) * SC_ROWS, SC_ROWS), SC_ROWS)]
            writes.append(pltpu.async_copy(rows_v.at[j % 2], dst, write_sem.at[j % 2]))
        for j in range(max(n_chunks - 2, 0), n_chunks):
            writes[j].wait()

    return pl.kernel(
        body,
        out_type=jax.ShapeDtypeStruct((n_rows, width), table.dtype),
        mesh=_sc_mesh(),
        scratch_types=[pltpu.VMEM((n_chunks, SC_ROWS), jnp.int32), pltpu.VMEM((2, SC_ROWS, width), table.dtype),
                       pltpu.SemaphoreType.DMA((2,)), pltpu.SemaphoreType.DMA((2,))],
        name="sc_row_gather",
    )(table, idx.reshape(n_rows // SC_ROWS, SC_ROWS))


def _sc_scatter(rows, pos, n_out):
    n_rows, width = rows.shape
    n_slots = pos.shape[0] // n_rows
    slot_chunks = n_rows // SC_ROWS
    n_chunks = slot_chunks // (SC_CORES * SC_SUBCORES)

    def body(rows_hbm, pos_hbm, out_hbm, idx_v, rows_v, read_sem, scatter_sem):
        first = _sc_worker() * n_chunks
        for s in range(n_slots):
            pltpu.sync_copy(pos_hbm.at[pl.ds(s * slot_chunks + first, n_chunks)], idx_v.at[s])
        read = lambda j: pltpu.async_copy(
            rows_hbm.at[pl.ds(pl.multiple_of((first + j) * SC_ROWS, SC_ROWS), SC_ROWS)],
            rows_v.at[j % 2], read_sem.at[j % 2])
        reads = [read(0)]
        scatters = []
        for j in range(n_chunks):
            reads[j].wait()
            if j + 1 < n_chunks:
                if j >= 1:
                    for copy in scatters[j - 1]:
                        copy.wait()
                reads.append(read(j + 1))
            scatters.append([pltpu.async_copy(rows_v.at[j % 2], out_hbm.at[idx_v.at[s, j]], scatter_sem.at[j % 2])
                             for s in range(n_slots)])
        for j in range(max(n_chunks - 2, 0), n_chunks):
            for copy in scatters[j]:
                copy.wait()

    return pl.kernel(
        body,
        out_type=jax.ShapeDtypeStruct((n_out, width), rows.dtype),
        mesh=_sc_mesh(),
        scratch_types=[pltpu.VMEM((n_slots, n_chunks, SC_ROWS), jnp.int32),
                       pltpu.VMEM((2, SC_ROWS, width), rows.dtype),
                       pltpu.SemaphoreType.DMA((2,)), pltpu.SemaphoreType.DMA((2,))],
        name="sc_row_scatter",
    )(rows, pos.reshape(n_slots * slot_chunks, SC_ROWS))


def _expert_kernel(tile_expert_ref, n_valid_ref, slot_ref, next_expert_ref, x_ref, wg_hbm, wu_hbm, wd_hbm,
                   y_ref, wg_buf, wu_buf, wd_buf, sem):
    step = pl.program_id(0)
    expert = tile_expert_ref[step]
    slot = slot_ref[step]
    valid = step < n_valid_ref[0]
    first_of_expert = (step == 0) | (expert != tile_expert_ref[jnp.maximum(step - 1, 0)])

    def weight_copies(which, into):
        pairs = ((wg_hbm, wg_buf), (wu_hbm, wu_buf), (wd_hbm, wd_buf))
        return [pltpu.make_async_copy(w.at[which], buf.at[into], sem.at[into, j]) for j, (w, buf) in enumerate(pairs)]

    @pl.when(step == 0)
    def _():
        for copy in weight_copies(expert, slot):
            copy.start()

    @pl.when(valid & first_of_expert)
    def _():
        for copy in weight_copies(expert, slot):
            copy.wait()
        upcoming = next_expert_ref[step]

        @pl.when(upcoming < N_EXPERTS)
        def _():
            for copy in weight_copies(upcoming, 1 - slot):
                copy.start()

    @pl.when(valid)
    def _():
        x = _unpack_bf16_pairs(x_ref[...])
        gate = jnp.dot(x, wg_buf[slot], preferred_element_type=F32)
        up = jnp.dot(x, wu_buf[slot], preferred_element_type=F32)
        hid = gate * jax.nn.sigmoid(gate) * up
        y_ref[...] = _pack_bf16_pairs(jnp.dot(_bf(hid), wd_buf[slot], preferred_element_type=F32))


def _expert_call(tile_expert, n_valid, slot, next_expert, x_sorted, w_gate, w_up, w_down, tm):
    n_rows = x_sorted.shape[0]
    rows = pl.BlockSpec((tm, D_MODEL // 2), lambda i, te, nv, sl, nx: (jnp.minimum(i, nv[0] - 1), 0))
    in_hbm = pl.BlockSpec(memory_space=pl.ANY)
    return pl.pallas_call(
        _expert_kernel,
        grid_spec=pltpu.PrefetchScalarGridSpec(
            num_scalar_prefetch=4,
            grid=(n_rows // tm,),
            in_specs=[rows, in_hbm, in_hbm, in_hbm],
            out_specs=rows,
            scratch_shapes=[pltpu.VMEM((2,) + w_gate.shape[1:], BF16), pltpu.VMEM((2,) + w_up.shape[1:], BF16),
                            pltpu.VMEM((2,) + w_down.shape[1:], BF16), pltpu.SemaphoreType.DMA((2, 3))]),
        out_shape=jax.ShapeDtypeStruct((n_rows, D_MODEL // 2), jnp.int32),
        compiler_params=pltpu.CompilerParams(
            dimension_semantics=("arbitrary",), vmem_limit_bytes=VMEM_LIMIT),
        name="moe_experts",
    )(tile_expert, n_valid, slot, next_expert, x_sorted, w_gate, w_up, w_down)


def _final_kernel(h_ref, y1_ref, y2_ref, route_ref, gfin_ref, *rest):
    out_ref = rest[-1]
    route = route_ref[...]
    gate1 = route[:, LANE_GATE1:LANE_GATE1 + 1]
    gate2 = route[:, LANE_GATE2:LANE_GATE2 + 1]
    moe = gate1 * _unpack_bf16_pairs(y1_ref[...]).astype(F32) + gate2 * _unpack_bf16_pairs(y2_ref[...]).astype(F32)
    out_ref[...] = _rms_norm(h_ref[...] + moe, gfin_ref[...])


def _final_call(h, y_pairs, route, g_final, earlier, first_token, tm):
    n_tok = h.shape[0]
    n_blocks = y_pairs.shape[0] // 2 // tm
    assert n_blocks * tm * 2 == y_pairs.shape[0] and first_token % tm == 0, "chunks must be whole row blocks"
    first = first_token // tm
    in_specs = [pl.BlockSpec((tm, D_MODEL), lambda i: (first + i, 0)),
                pl.BlockSpec((tm, D_MODEL // 2), lambda i: (i, 0)),
                pl.BlockSpec((tm, D_MODEL // 2), lambda i: (i + n_blocks, 0)),
                pl.BlockSpec((tm, ROUTER_LANES), lambda i: (first + i, 0)),
                pl.BlockSpec((1, D_MODEL), lambda i: (0, 0))]
    args = [h, y_pairs, y_pairs, route, g_final]
    aliases = {}
    if earlier is not None:
        in_specs.append(pl.BlockSpec(memory_space=pl.ANY))
        args.append(earlier)
        aliases = {len(args) - 1: 0}
    return pl.pallas_call(
        _final_kernel,
        grid=(n_blocks,),
        in_specs=in_specs,
        out_specs=pl.BlockSpec((tm, D_MODEL), lambda i: (first + i, 0)),
        out_shape=jax.ShapeDtypeStruct((n_tok, D_MODEL), F32),
        input_output_aliases=aliases,
        compiler_params=pltpu.CompilerParams(
            dimension_semantics=("arbitrary",), vmem_limit_bytes=VMEM_LIMIT),
        name="moe_combine_norm",
    )(*args)


def _dispatch_plan(route_t, counts, tm):
    n_tok = route_t.shape[1]
    n_tiles = (2 * n_tok) // tm + N_EXPERTS
    counts = counts[0, :N_EXPERTS].astype(jnp.int32)
    tiles_per = (counts + tm - 1) // tm
    tile_end = jnp.cumsum(tiles_per)
    row_start = (tile_end - tiles_per) * tm
    experts = jnp.arange(N_EXPERTS, dtype=jnp.int32)

    def position(idx_lane, rank_lane):
        idx = route_t[idx_lane].astype(jnp.int32)
        start = jnp.sum(jnp.where(idx[None, :] == experts[:, None], row_start[:, None], 0), axis=0)
        return start + route_t[rank_lane].astype(jnp.int32)

    pos = jnp.concatenate([position(LANE_IDX1, LANE_RANK1), position(LANE_IDX2, LANE_RANK2)])
    n_valid = tile_end[-1:]
    tile = jnp.minimum(jnp.arange(n_tiles, dtype=jnp.int32), n_valid - 1)
    tile_expert = jnp.sum((tile_end[None, :] <= tile[:, None]).astype(jnp.int32), axis=1)
    present = tiles_per > 0
    later = (experts[None, :] > experts[:, None]) & present[None, :]
    next_present = jnp.min(jnp.where(later, experts[None, :], N_EXPERTS), axis=1)
    order = jnp.cumsum(present.astype(jnp.int32)) - 1
    of_tile = lambda per_expert: jnp.sum(jnp.where(tile_expert[:, None] == experts[None, :], per_expert[None, :], 0), axis=1)
    return pos, n_tiles, tile_expert, n_valid, of_tile(order) % 2, of_tile(next_present)


def _block(x, norm_mix_g, w_in, rwkv_mu, conv_w, decay_up, decay_base, aaa_up, aaa_base, gate_up,
           k_k, k_a, r_k, ln_x_w, ln_x_b, w_out, norm_ffn_g, router_group_w, router_group_b,
           router_expert_w, router_expert_b, expert_w_gate, expert_w_up, expert_w_down, norm_final_g,
           *, tm_in, tb_rwkv, tm_out, tm_expert, tm_final):
    bsz, seq, d_model = x.shape
    n_tok = bsz * seq
    row = lambda p: p.reshape(1, -1)

    yconv, r, lw, k2, v, kk, kka, gate, bonus = _in_call(
        x, row(norm_mix_g), w_in, row(rwkv_mu), conv_w, decay_up, row(decay_base), aaa_up, row(aaa_base),
        gate_up, row(k_k), row(k_a), row(r_k), tm_in)
    o, w_gate_bf, w_up_bf, w_down_bf = _rwkv_call(r, lw, k2, v, kk, kka,
                                                  (expert_w_gate, expert_w_up, expert_w_down), tb_rwkv)

    pad = ROUTER_LANES - N_EXPERTS - N_GROUPS
    router_w = jnp.concatenate([router_expert_w, router_group_w, jnp.zeros((d_model, pad), F32)], axis=1)
    router_hi = _bf(router_w)
    router_split = jnp.concatenate([router_hi, _bf(router_w - router_hi.astype(F32))], axis=1)
    router_b = jnp.concatenate([router_expert_b, router_group_b, jnp.zeros((pad,), F32)]).reshape(1, -1)
    flat = lambda t: t.reshape(n_tok, t.shape[-1])
    h, u_pairs, route, route_t, counts = _out_call(
        flat(o), flat(bonus), flat(gate), flat(yconv), flat(x),
        row(ln_x_w), row(ln_x_b), w_out, row(norm_ffn_g), router_split, router_b, tm_out, OUT_SPLIT)

    pos, n_tiles, tile_expert, n_valid, slot, next_expert = _dispatch_plan(route_t, counts, tm_expert)
    x_sorted = _sc_scatter(u_pairs, pos, n_tiles * tm_expert)
    y_sorted = _expert_call(tile_expert, n_valid, slot, next_expert, x_sorted, w_gate_bf, w_up_bf, w_down_bf,
                            tm_expert)
    out, lo = None, 0
    for fraction in COMBINE_SPLIT:
        size = n_tok // fraction
        pos_c = jnp.concatenate([pos[lo:lo + size], pos[n_tok + lo:n_tok + lo + size]])
        out = _final_call(h, _sc_gather(y_sorted, pos_c), route, row(norm_final_g), out, lo, tm_final)
        lo += size
    assert lo == n_tok, "COMBINE_SPLIT must cover all tokens"
    return out.reshape(bsz, seq, d_model)


def kernel(x, norm_mix_g, w_in, rwkv_mu, conv_w, decay_up, decay_base, aaa_up, aaa_base, gate_up, k_k, k_a, r_k, ln_x_w, ln_x_b, w_out, norm_ffn_g, router_group_w, router_group_b, router_expert_w, router_expert_b, expert_w_gate, expert_w_up, expert_w_down, norm_final_g):
    return _block(x, norm_mix_g[0], w_in[0], rwkv_mu[0], conv_w[0], decay_up[0], decay_base[0],
                  aaa_up[0], aaa_base[0], gate_up[0], k_k[0], k_a[0], r_k[0].reshape(-1), ln_x_w[0],
                  ln_x_b[0], w_out[0], norm_ffn_g[0], router_group_w[0], router_group_b[0],
                  router_expert_w[0], router_expert_b[0], expert_w_gate[0], expert_w_up[0],
                  expert_w_down[0], norm_final_g,
                  tm_in=512, tb_rwkv=1024, tm_out=1024, tm_expert=512, tm_final=1024)
```

```python
import functools

import jax
import jax.numpy as jnp
from jax import lax
from jax.experimental import pallas as pl
from jax.experimental.pallas import tpu as pltpu
from jax.experimental.pallas import tpu_sc as plsc

F32 = jnp.float32
BF16 = jnp.bfloat16

D_MODEL = 1024
D_CONV = 512
CONV_WIDTH = 3
N_HEADS = 8
HEAD = 64
D_RWKV = N_HEADS * HEAD
LORA_WA = 128
GATE_LORA = 128
D_RWKV_PROJ = 3 * D_RWKV + LORA_WA + GATE_LORA
D_IN = 3 * D_CONV + D_RWKV_PROJ
N_GROUPS = 4
EXPERTS_PER_GROUP = 8
N_EXPERTS = N_GROUPS * EXPERTS_PER_GROUP
D_EXPERT = D_MODEL // 4
RMS_EPS = 1e-6
LN_X_EPS = 64e-5
L2_EPS = 1e-12

SUBLANES = 8
CHUNK = 64
QUAD = 4 * HEAD
GROUP = 2 * HEAD
INV_BASE = 8
WAVES = (8, 4, 4)
WAVE = 4
PRODUCTS_PER_CHAIN = 15
COMPACT_FROM = 64
OUT_SPLIT = 8
COMBINE_SPLIT = (4, 4, 4, 4)
ROUTER_LANES = 128

VMEM_LIMIT = 56 * 1024 * 1024


def _bf(x):
    return x.astype(BF16)


class _Interleaver:
    def __init__(self):
        self.stages, self.every, self.calls, self.busy = None, 1, 0, False

    def start(self, stages, every):
        self.stages, self.every, self.calls = stages, every, 0

    def tick(self):
        if self.stages is None or self.busy:
            return
        self.calls += 1
        if self.calls % self.every == 0:
            self.busy = True
            next(self.stages, None)
            self.busy = False

    def drain(self):
        self.busy = True
        for _ in self.stages:
            pass
        self.stages, self.busy = None, False


_INTERLEAVE = _Interleaver()


def _mm(a, b):
    _INTERLEAVE.tick()
    return jnp.dot(_bf(a), _bf(b), preferred_element_type=F32)


def _mm_nt(a, b):
    _INTERLEAVE.tick()
    return lax.dot_general(_bf(a), _bf(b), (((1,), (1,)), ((), ())), preferred_element_type=F32)


def _mm_exact_lhs(lhs_bf16, x, passes):
    acc = None
    rem = x
    for _ in range(passes):
        piece = _bf(rem)
        part = jnp.dot(lhs_bf16, piece, preferred_element_type=F32)
        acc = part if acc is None else acc + part
        rem = rem - piece.astype(F32)
    return acc


def _head_ones():
    return (lax.broadcasted_iota(jnp.int32, (QUAD, QUAD), 0) // HEAD
            == lax.broadcasted_iota(jnp.int32, (QUAD, QUAD), 1) // HEAD).astype(BF16)


def _head_sum(x, ones_quad):
    xb = _bf(x)
    return jnp.concatenate(
        [jnp.dot(xb[:, q * QUAD:(q + 1) * QUAD], ones_quad, preferred_element_type=F32)
         for q in range(x.shape[1] // QUAD)], axis=1)


def _rms_norm(x, g):
    return x * lax.rsqrt(jnp.mean(x * x, axis=-1, keepdims=True) + RMS_EPS) * g


def _shift_rows(cur, prev_rows, k):
    rolled = pltpu.roll(cur, k, 0)
    prev_rolled = pltpu.roll(prev_rows, k, 0)
    n = cur.shape[0]
    head = jnp.concatenate([prev_rolled, rolled[SUBLANES:]], axis=0) if n > SUBLANES else prev_rolled
    row = lax.broadcasted_iota(jnp.int32, cur.shape, 0)
    return jnp.where(row < k, head, rolled)


def _in_kernel(x_ref, g_ref, w_ref, mu_ref, convw_ref, dup_ref, dbase_ref, aup_ref, abase_ref,
               gup_ref, kk_ref, ka_ref, rk_ref,
               yconv_ref, r_ref, lw_ref, k_ref, v_ref, kkn_ref, kka_ref, gate_ref, bonus_ref,
               carry_ref, wbf_ref):
    @pl.when((pl.program_id(0) == 0) & (pl.program_id(1) == 0))
    def _():
        wbf_ref[...] = _bf(w_ref[...])

    @pl.when(pl.program_id(1) == 0)
    def _():
        carry_ref[...] = jnp.zeros_like(carry_ref)

    u = _bf(_rms_norm(x_ref[0], g_ref[...]))
    n_conv = 3 * D_CONV
    rk0 = n_conv
    bounds = dict(conv=(0, n_conv), lora=(rk0 + 3 * D_RWKV, D_IN), k=(rk0 + D_RWKV, rk0 + 2 * D_RWKV),
                  r=(rk0, rk0 + D_RWKV), v=(rk0 + 2 * D_RWKV, rk0 + 3 * D_RWKV))
    z = {name: jnp.dot(u, wbf_ref[:, lo:hi], preferred_element_type=F32) for name, (lo, hi) in bounds.items()}

    def lerp(name):
        lo, hi = bounds[name]
        cur = z[name]
        prev = carry_ref[:, D_CONV + lo - n_conv:D_CONV + hi - n_conv]
        mixed = cur + (_shift_rows(cur, prev, 1) - cur) * mu_ref[:, lo - n_conv:hi - n_conv]
        carry_ref[:, D_CONV + lo - n_conv:D_CONV + hi - n_conv] = cur[-SUBLANES:]
        return mixed

    b_gate = z["conv"][:, :D_CONV]
    ch = z["conv"][:, D_CONV:2 * D_CONV] * z["conv"][:, 2 * D_CONV:]
    prev_ch = carry_ref[:, :D_CONV]
    conv = convw_ref[CONV_WIDTH - 1:CONV_WIDTH, :] * ch
    for delay in range(1, CONV_WIDTH):
        tap = CONV_WIDTH - 1 - delay
        conv = conv + convw_ref[tap:tap + 1, :] * _shift_rows(ch, prev_ch, delay)
    yconv_ref[0] = _bf(b_gate * conv)
    carry_ref[:, :D_CONV] = ch[-SUBLANES:]

    lora = lerp("lora")
    wa_lo = lora[:, :LORA_WA]
    g_lo = lora[:, LORA_WA:]
    no_rows = jnp.zeros_like(dup_ref[...])
    decay_up = _bf(jnp.concatenate([dup_ref[...], no_rows], axis=0))
    rate_up = _bf(jnp.concatenate([no_rows, aup_ref[...]], axis=0))
    dec_in = -(dbase_ref[...] + jnp.dot(_bf(jnp.tanh(wa_lo)), decay_up, preferred_element_type=F32))
    softplus = jnp.maximum(dec_in, 0.0) + jnp.log(1.0 + jnp.exp(-jnp.abs(dec_in)))
    w = -softplus - 0.5
    lw_ref[0] = -jnp.exp(w)
    a = jax.nn.sigmoid(abase_ref[...] + jnp.dot(_bf(wa_lo), rate_up, preferred_element_type=F32))
    gate_ref[0] = _bf(jnp.dot(_bf(jax.nn.sigmoid(g_lo)), _bf(gup_ref[...]), preferred_element_type=F32))

    ones_quad = _head_ones()
    k = lerp("k")
    kk = k * kk_ref[...]
    norm = jnp.sqrt(_head_sum(kk * kk, ones_quad))
    kk = kk / jnp.maximum(norm, L2_EPS)
    k2 = k * (1.0 + (a - 1.0) * ka_ref[...])
    k_ref[0] = _bf(k2)
    kkn_ref[0] = _bf(kk)
    kka_ref[0] = _bf(kk * a)
    r = lerp("r")
    r_ref[0] = _bf(r)
    v = lerp("v")
    v_ref[0] = _bf(v)
    bonus_ref[0] = _bf(_head_sum(r * k2 * rk_ref[...], ones_quad) * v)


def _in_call(x, g, w_in, mu, conv_w, dup, dbase, aup, abase, gup, k_k, k_a, r_k, tm):
    bsz, seq, _ = x.shape
    full = lambda arr: pl.BlockSpec(arr.shape, lambda b, t: (0,) * arr.ndim, pipeline_mode=pl.Buffered(1))
    out_spec = pl.BlockSpec((1, tm, D_RWKV), lambda b, t: (b, t, 0))
    out_dtypes = (BF16, BF16, F32, BF16, BF16, BF16, BF16, BF16, BF16)
    params = (g, w_in, mu, conv_w, dup, dbase, aup, abase, gup, k_k, k_a, r_k)
    return pl.pallas_call(
        _in_kernel,
        grid=(bsz, seq // tm),
        in_specs=[pl.BlockSpec((1, tm, D_MODEL), lambda b, t: (b, t, 0))] + [full(p) for p in params],
        out_specs=[out_spec] * 9,
        out_shape=[jax.ShapeDtypeStruct((bsz, seq, D_RWKV), dt) for dt in out_dtypes],
        scratch_shapes=[pltpu.VMEM((SUBLANES, D_CONV + D_RWKV_PROJ), F32), pltpu.VMEM(w_in.shape, BF16)],
        compiler_params=pltpu.CompilerParams(
            dimension_semantics=("arbitrary", "arbitrary"), vmem_limit_bytes=VMEM_LIMIT),
        name="in_proj",
    )(x, *params)


def _block_diag(y, bd_mask):
    return jnp.where(bd_mask, jnp.concatenate([y] * (GROUP // CHUNK), axis=0), 0.0)


def _each(fn, *lists):
    return [fn(*args) for args in zip(*lists)]


def _unit_lower_inverse(a_strict, t_idx, s_idx, bd):
    bdmm = lambda xs, ys: _each(lambda x, y: _mm(x, bd(y)), xs, ys)
    eye = (t_idx == s_idx).astype(F32)
    same8 = (t_idx // INV_BASE) == (s_idx // INV_BASE)
    a8 = _each(lambda a: jnp.where(same8, a, 0.0), a_strict)
    a8_2 = bdmm(a8, a8)
    a8_34 = bdmm(_each(lambda a, b: jnp.concatenate([a, b], axis=0), a8, a8_2), a8_2)
    inv = _each(lambda a, b, c: eye + a + b + c[:CHUNK], a8, a8_2, a8_34)
    inv = _each(jnp.add, inv, bdmm(inv, _each(lambda c: c[CHUNK:], a8_34)))
    size = 2 * INV_BASE
    while size < COMPACT_FROM:
        off = ((t_idx // size) == (s_idx // size)) & ((t_idx // (size // 2)) != (s_idx // (size // 2)))
        cross = bdmm(_each(lambda a: jnp.where(off, a, 0.0), a_strict), inv)
        inv = _each(jnp.add, inv, bdmm(inv, cross))
        size *= 2
    while size <= CHUNK:
        inv = _compact_level(a_strict, inv, size)
        size *= 2
    return inv


def _compact_level(a_strict, inv, size):
    half = size // 2
    heads = GROUP // CHUNK
    blocks = CHUNK // size
    lane_starts = [h * CHUNK + b * size for h in range(heads) for b in range(blocks)]
    row_starts = [b * size for b in range(blocks)]
    rows_first = lambda x: jnp.concatenate([x[r:r + half] for r in row_starts], axis=0)
    rows_second = lambda x: jnp.concatenate([x[r + half:r + size] for r in row_starts], axis=0)
    lanes_first = lambda x: jnp.concatenate([x[:, c:c + half] for c in lane_starts], axis=1)
    lanes_second = lambda x: jnp.concatenate([x[:, c + half:c + size] for c in lane_starts], axis=1)
    n_rows, n_lanes = CHUNK // 2, GROUP // 2
    own = (lax.broadcasted_iota(jnp.int32, (n_rows, n_lanes), 0) // half
           == (lax.broadcasted_iota(jnp.int32, (n_rows, n_lanes), 1) // half) % blocks)
    diag = (lax.broadcasted_iota(jnp.int32, (n_lanes, n_lanes), 0) // half
            == lax.broadcasted_iota(jnp.int32, (n_lanes, n_lanes), 1) // half)
    bd_half = lambda y: jnp.where(diag, jnp.concatenate([y] * heads, axis=0), 0.0)
    a21 = _each(lambda a: jnp.where(own, lanes_first(rows_second(a)), 0.0), a_strict)
    t11 = _each(lambda t: lanes_first(rows_first(t)), inv)
    t22 = _each(lambda t: lanes_second(rows_second(t)), inv)
    a21_t11 = _each(lambda x, y: _mm(x, bd_half(y)), a21, t11)
    new = _each(lambda x, y: _mm(x, bd_half(y)), t22, a21_t11)

    def placed(x):
        zero_lanes = jnp.zeros((n_rows, half), F32)
        wide = jnp.concatenate(sum(([x[:, i * half:(i + 1) * half], zero_lanes] for i in range(heads * blocks)), []),
                               axis=1)
        zero_rows = jnp.zeros((half, GROUP), F32)
        return jnp.concatenate(sum(([zero_rows, wide[b * half:(b + 1) * half]] for b in range(blocks)), []), axis=0)

    return _each(lambda t, x: t + placed(x), inv, new)


def _chunk_local(r, lw, k, v, kk, kka, tri, t_idx, s_idx, bd_mask):
    bd = lambda y: _block_diag(y, bd_mask)
    bdmm = lambda xs, ys: _each(lambda x, y: _mm(x, bd(y)), xs, ys)
    bdmm2 = lambda xs, ys, zs: _each(lambda x, y, z: _mm(x, jnp.concatenate([bd(y), bd(z)], axis=1)), xs, ys, zs)
    left = lambda xs: _each(lambda x: x[:, :GROUP], xs)
    right = lambda xs: _each(lambda x: x[:, GROUP:], xs)
    top = lambda xs: _each(lambda x: x[:CHUNK], xs)
    bottom = lambda xs: _each(lambda x: x[CHUNK:], xs)

    cum = _each(lambda x: _mm_exact_lhs(tri, x, 3), lw)
    cum_last = _each(lambda c: c[CHUNK - 1:CHUNK, :], cum)
    p_incl = _each(jnp.exp, cum)
    p_excl = _each(lambda c, x: jnp.exp(c - x), cum, lw)
    p_inv = _each(lambda c: jnp.exp(-c), cum)
    to_end = _each(lambda cl, c: jnp.exp(cl - c), cum_last, cum)
    a_t = _each(lambda x, p: -x * p, kk, p_excl)
    r_t = _each(jnp.multiply, r, p_incl)
    b_t = _each(jnp.multiply, kka, p_inv)
    k_t = _each(jnp.multiply, k, p_inv)
    b_end = _each(jnp.multiply, kka, to_end)
    k_end = _each(jnp.multiply, k, to_end)

    ar = _each(lambda a, b: jnp.concatenate([a, b], axis=0), a_t, r_t)
    row2 = lax.broadcasted_iota(jnp.int32, (2 * CHUNK, 2 * GROUP), 0)
    col2 = lax.broadcasted_iota(jnp.int32, (2 * CHUNK, 2 * GROUP), 1) % CHUNK
    causal = col2 < (row2 % CHUNK) + (row2 // CHUNK)
    scores = _each(lambda x, y, z: jnp.where(causal, _mm_nt(x, jnp.concatenate([bd(y), bd(z)], axis=0)), 0.0),
                   ar, b_t, k_t)
    a_ab = top(left(scores))
    a_rb = bottom(left(scores))

    inv = _unit_lower_inverse(a_ab, t_idx, s_idx, bd)
    kv = bdmm(right(scores), v)
    wu = bdmm2(inv, a_t, top(kv))
    ro = bdmm2(a_rb, left(wu), right(wu))
    r_hat = _each(jnp.add, r_t, left(ro))
    o_loc = _each(jnp.add, right(ro), bottom(kv))

    wu_b = _each(lambda x, b: _mm(x.T, b), wu, b_end)
    trans = _each(lambda x: jnp.where(bd_mask, x[:GROUP], 0.0), wu_b)
    d_bd = _each(lambda x, y, z: jnp.where(bd_mask, x[GROUP:] + _mm(y.T, z), 0.0), wu_b, v, k_end)
    d_ls = _each(lambda d: sum(d[h * CHUNK:(h + 1) * CHUNK] for h in range(1, GROUP // CHUNK)) + d[:CHUNK], d_bd)
    p_end = _each(jnp.exp, cum_last)
    return r_hat, o_loc, trans, d_ls, p_end


def _rwkv_kernel(r_ref, lw_ref, k_ref, v_ref, kk_ref, kka_ref, wg_ref, wu_ref, wd_ref,
                 o_ref, wg_bf_ref, wu_bf_ref, wd_bf_ref, state_ref, *, n_sub):
    @pl.when(pl.program_id(1) == 0)
    def _():
        state_ref[...] = jnp.zeros_like(state_ref)

    wg_bf_ref[...] = _bf(wg_ref[...])
    wu_bf_ref[...] = _bf(wu_ref[...])
    wd_bf_ref[...] = _bf(wd_ref[...])

    t_idx = lax.broadcasted_iota(jnp.int32, (CHUNK, GROUP), 0)
    s_idx = lax.broadcasted_iota(jnp.int32, (CHUNK, GROUP), 1) % CHUNK
    bd_mask = (lax.broadcasted_iota(jnp.int32, (GROUP, GROUP), 0) // CHUNK
               == lax.broadcasted_iota(jnp.int32, (GROUP, GROUP), 1) // CHUNK)
    chunk_row = lax.broadcasted_iota(jnp.int32, (CHUNK, CHUNK), 0)
    chunk_col = lax.broadcasted_iota(jnp.int32, (CHUNK, CHUNK), 1)
    n_group = D_RWKV // GROUP
    tri = (chunk_row >= chunk_col).astype(BF16)
    state = [state_ref[g] for g in range(n_group)]

    def through_state(local, where):
        r_hat, o_loc, trans, d_ls, p_end = local
        for first in range(0, len(where), n_group):
            chains = range(first, first + n_group)
            out = [_mm_nt(r_hat[i], _block_diag(state[g], bd_mask)) + o_loc[i] for g, i in enumerate(chains)]
            for g, i in enumerate(chains):
                rows, lanes = where[i]
                o_ref[0, rows, lanes] = _bf(out[g])
            yield
            state[:] = [state[g] * p_end[i] + _mm(state[g], trans[i]) + d_ls[i] for g, i in enumerate(chains)]
            yield

    pending, start, stages = None, 0, 0
    waves = WAVES if sum(WAVES) == n_sub else [min(WAVE, n_sub - s) for s in range(0, n_sub, WAVE)]
    for wave in waves:
        where = [(slice(c * CHUNK, (c + 1) * CHUNK), slice(g * GROUP, (g + 1) * GROUP))
                 for c in range(start, start + wave) for g in range(n_group)]
        load = lambda ref: [ref[0, rows, lanes].astype(F32) for rows, lanes in where]
        if pending is not None:
            _INTERLEAVE.start(pending, len(where) * PRODUCTS_PER_CHAIN // (stages + 1))
        local = _chunk_local(load(r_ref), load(lw_ref), load(k_ref), load(v_ref), load(kk_ref), load(kka_ref),
                             tri, t_idx, s_idx, bd_mask)
        if pending is not None:
            _INTERLEAVE.drain()
        pending, start, stages = through_state(local, where), start + wave, 2 * wave
    for _ in pending:
        pass
    for g in range(n_group):
        state_ref[g] = state[g]


def _rwkv_call(r, lw, k, v, kk, kka, expert_weights, tb):
    bsz, seq, _ = r.shape
    n_t = seq // tb
    per_step = N_EXPERTS // (bsz * n_t)
    assert per_step * bsz * n_t == N_EXPERTS, "grid steps must divide the expert count"
    spec = pl.BlockSpec((1, tb, D_RWKV), lambda b, t: (b, t, 0))
    w_specs = [pl.BlockSpec((per_step,) + w.shape[1:], lambda b, t: (b * n_t + t, 0, 0)) for w in expert_weights]
    return pl.pallas_call(
        functools.partial(_rwkv_kernel, n_sub=tb // CHUNK),
        grid=(bsz, n_t),
        in_specs=[spec] * 6 + w_specs,
        out_specs=[spec] + w_specs,
        out_shape=[jax.ShapeDtypeStruct((bsz, seq, D_RWKV), BF16)]
        + [jax.ShapeDtypeStruct(w.shape, BF16) for w in expert_weights],
        scratch_shapes=[pltpu.VMEM((D_RWKV // GROUP, HEAD, GROUP), F32)],
        compiler_params=pltpu.CompilerParams(
            dimension_semantics=("arbitrary", "arbitrary"), vmem_limit_bytes=VMEM_LIMIT),
        name="rwkv_chunk",
    )(r, lw, k, v, kk, kka, *expert_weights)


def _route(logits):
    lane_i = lax.broadcasted_iota(jnp.int32, logits.shape, 1)
    lane = lane_i.astype(F32)
    lane_group = (lane_i // EXPERTS_PER_GROUP).astype(F32)
    neg = -jnp.inf
    big = float(ROUTER_LANES)
    is_group = (lane_i >= N_EXPERTS) & (lane_i < N_EXPERTS + N_GROUPS)
    gl = jnp.where(is_group, logits, neg)
    ge = jnp.exp(gl - jnp.max(gl, axis=-1, keepdims=True))
    gprob = ge / jnp.sum(ge, axis=-1, keepdims=True)
    g_top = jnp.max(gprob, axis=-1, keepdims=True)
    g_idx = jnp.min(jnp.where(is_group & (gprob == g_top), lane - N_EXPERTS, big), axis=-1, keepdims=True)

    in_group = (lane_i < N_EXPERTS) & (lane_group == g_idx)
    el = jnp.where(in_group, logits, neg)
    top1 = jnp.max(el, axis=-1, keepdims=True)
    idx1 = jnp.min(jnp.where(in_group & (el == top1), lane, big), axis=-1, keepdims=True)
    el2 = jnp.where(lane == idx1, neg, el)
    top2 = jnp.max(el2, axis=-1, keepdims=True)
    idx2 = jnp.min(jnp.where(in_group & (lane != idx1) & (el2 == top2), lane, big), axis=-1, keepdims=True)
    e2 = jnp.exp(top2 - top1)
    denom = 1.0 + e2
    return idx1, idx2, g_top * (1.0 / denom), g_top * (e2 / denom)


def _pack_bf16_pairs(x):
    bits = lax.bitcast_convert_type(_bf(x).astype(F32), jnp.uint32)
    k = x.shape[1] // 2
    return lax.bitcast_convert_type((bits[:, :k] >> 16) | bits[:, k:], jnp.int32)


def _unpack_bf16_pairs(words):
    bits = lax.bitcast_convert_type(words, jnp.uint32)
    lo = lax.bitcast_convert_type(bits << 16, F32)
    hi = lax.bitcast_convert_type(bits & jnp.uint32(0xFFFF0000), F32)
    return _bf(jnp.concatenate([lo, hi], axis=1))


LANE_IDX1, LANE_IDX2, LANE_RANK1, LANE_RANK2, LANE_GATE1, LANE_GATE2 = range(6)


def _out_kernel(o_ref, bonus_ref, gate_ref, yconv_ref, x_ref, lnw_ref, lnb_ref,
                wout_ref, gffn_ref, rw_ref, rb_ref, h_ref, u_ref, route_ref, route_t_ref, count_ref,
                seen_ref, wbf_ref, *, n_split):
    @pl.when(pl.program_id(0) == 0)
    def _():
        seen_ref[...] = jnp.zeros_like(seen_ref)
        wbf_ref[...] = _bf(wout_ref[...])

    sub = o_ref.shape[0] // n_split
    parts = [slice(s * sub, (s + 1) * sub) for s in range(n_split)]
    read = lambda ref: [ref[p, :] for p in parts]
    ones_quad = _head_ones()
    inv_n = 1.0 / HEAD
    o = _each(lambda v: v.astype(F32), read(o_ref))
    mean = _each(lambda v: _head_sum(v, ones_quad) * inv_n, o)
    cen = _each(jnp.subtract, o, mean)
    var = _each(lambda c: _head_sum(c * c, ones_quad) * inv_n, cen)
    on = _each(lambda c, v: c * lax.rsqrt(v + LN_X_EPS) * lnw_ref[...] + lnb_ref[...], cen, var)
    y_rwkv = _each(lambda a, b, g: (a + b) * g, on, read(bonus_ref), read(gate_ref))
    mixed = _each(lambda yc, yr: (jnp.dot(yc, wbf_ref[:D_CONV, :], preferred_element_type=F32)
                                  + jnp.dot(_bf(yr), wbf_ref[D_CONV:, :], preferred_element_type=F32)),
                  read(yconv_ref), y_rwkv)
    h = _each(jnp.add, read(x_ref), mixed)
    u = _each(lambda v: _rms_norm(v, gffn_ref[...]), h)
    u_hi = _each(_bf, u)
    u_lo = _each(lambda a, b: _bf(a - b.astype(F32)), u, u_hi)
    by_hi = _each(lambda a: jnp.dot(a, rw_ref[...], preferred_element_type=F32), u_hi)
    by_lo = _each(lambda a: jnp.dot(a, rw_ref[:, :ROUTER_LANES], preferred_element_type=F32), u_lo)
    logits = _each(lambda a, b: a[:, :ROUTER_LANES] + a[:, ROUTER_LANES:] + b + rb_ref[...], by_hi, by_lo)
    routed = _each(_route, logits)

    lane_i = lax.broadcasted_iota(jnp.int32, (sub, ROUTER_LANES), 1)
    lane = lane_i.astype(F32)
    hit1 = _each(lambda rt: lane == rt[0], routed)
    hit2 = _each(lambda rt: lane == rt[1], routed)
    both = _each(lambda a, b: a.astype(F32) + b.astype(F32), hit1, hit2)
    earlier = (lax.broadcasted_iota(jnp.int32, (sub, sub), 0) > lax.broadcasted_iota(jnp.int32, (sub, sub), 1))
    inside = _each(lambda b: jnp.dot(earlier.astype(BF16), _bf(b), preferred_element_type=F32), both)
    seen = seen_ref[...]
    for s, p in enumerate(parts):
        idx1, idx2, gate1, gate2 = routed[s]
        before = inside[s] + seen
        rank1 = jnp.sum(jnp.where(hit1[s], before, 0.0), axis=-1, keepdims=True)
        rank2 = jnp.sum(jnp.where(hit2[s], before, 0.0), axis=-1, keepdims=True)
        seen = seen + jnp.sum(both[s], axis=0, keepdims=True)
        route = jnp.zeros((sub, ROUTER_LANES), F32)
        for lane_id, col in ((LANE_IDX1, idx1), (LANE_IDX2, idx2), (LANE_RANK1, rank1),
                             (LANE_RANK2, rank2), (LANE_GATE1, gate1), (LANE_GATE2, gate2)):
            route = jnp.where(lane_i == lane_id, col, route)
        route_ref[p, :] = route
        route_t_ref[:, p] = route.T[:SUBLANES, :]
        h_ref[p, :] = _bf(h[s])
        u_ref[p, :] = _pack_bf16_pairs(u[s])
    seen_ref[...] = seen
    count_ref[...] = jnp.broadcast_to(seen, count_ref.shape)


def _out_call(o, bonus, gate, yconv, x, ln_w, ln_b, w_out, g_ffn, router_w, router_b, tm, n_split):
    n_tok = x.shape[0]
    row = lambda width: pl.BlockSpec((tm, width), lambda i: (i, 0))
    full = lambda arr: pl.BlockSpec(arr.shape, lambda i: (0,) * arr.ndim, pipeline_mode=pl.Buffered(1))
    params = (ln_w, ln_b, w_out, g_ffn, router_w, router_b)
    return pl.pallas_call(
        functools.partial(_out_kernel, n_split=n_split),
        grid=(n_tok // tm,),
        in_specs=[row(D_RWKV)] * 4 + [row(D_MODEL)] + [full(p) for p in params],
        out_specs=[row(D_MODEL), row(D_MODEL // 2), row(ROUTER_LANES),
                   pl.BlockSpec((SUBLANES, tm), lambda i: (0, i)),
                   pl.BlockSpec((SUBLANES, ROUTER_LANES), lambda i: (0, 0))],
        out_shape=[jax.ShapeDtypeStruct((n_tok, D_MODEL), BF16),
                   jax.ShapeDtypeStruct((n_tok, D_MODEL // 2), jnp.int32),
                   jax.ShapeDtypeStruct((n_tok, ROUTER_LANES), F32),
                   jax.ShapeDtypeStruct((SUBLANES, n_tok), F32),
                   jax.ShapeDtypeStruct((SUBLANES, ROUTER_LANES), F32)],
        scratch_shapes=[pltpu.VMEM((1, ROUTER_LANES), F32), pltpu.VMEM(w_out.shape, BF16)],
        compiler_params=pltpu.CompilerParams(
            dimension_semantics=("arbitrary",), vmem_limit_bytes=VMEM_LIMIT),
        name="out_proj_route",
    )(o, bonus, gate, yconv, x, *params)


SC_CORES = 2
SC_SUBCORES = 16
SC_ROWS = 64


def _sc_mesh():
    return plsc.VectorSubcoreMesh(core_axis_name="c", subcore_axis_name="s",
                                  num_cores=SC_CORES, num_subcores=SC_SUBCORES)


def _sc_worker():
    return lax.axis_index("s") * SC_CORES + lax.axis_index("c")


def _sc_gather(table, idx):
    n_rows = idx.shape[0]
    width = table.shape[1]
    n_chunks = n_rows // (SC_CORES * SC_SUBCORES * SC_ROWS)

    def body(table_hbm, idx_hbm, out_hbm, idx_v, rows_v, gather_sem, write_sem):
        first = _sc_worker() * n_chunks
        pltpu.sync_copy(idx_hbm.at[pl.ds(first, n_chunks)], idx_v)
        gather = lambda j: pltpu.async_copy(table_hbm.at[idx_v.at[j]], rows_v.at[j % 2], gather_sem.at[j % 2])
        gathers = [gather(0)]
        writes = []
        for j in range(n_chunks):
            gathers[j].wait()
            if j + 1 < n_chunks:
                if j >= 1:
                    writes[j - 1].wait()
                gathers.append(gather(j + 1))
            dst = out_hbm.at[pl.ds(pl.multiple_of((first + j) * SC_ROWS, SC_ROWS), SC_ROWS)]
            writes.append(pltpu.async_copy(rows_v.at[j % 2], dst, write_sem.at[j % 2]))
        for j in range(max(n_chunks - 2, 0), n_chunks):
            writes[j].wait()

    return pl.kernel(
        body,
        out_type=jax.ShapeDtypeStruct((n_rows, width), table.dtype),
        mesh=_sc_mesh(),
        scratch_types=[pltpu.VMEM((n_chunks, SC_ROWS), jnp.int32), pltpu.VMEM((2, SC_ROWS, width), table.dtype),
                       pltpu.SemaphoreType.DMA((2,)), pltpu.SemaphoreType.DMA((2,))],
        name="sc_row_gather",
    )(table, idx.reshape(n_rows // SC_ROWS, SC_ROWS))


def _sc_scatter(rows, pos, n_out):
    n_rows, width = rows.shape
    n_slots = pos.shape[0] // n_rows
    slot_chunks = n_rows // SC_ROWS
    n_chunks = slot_chunks // (SC_CORES * SC_SUBCORES)

    def body(rows_hbm, pos_hbm, out_hbm, idx_v, rows_v, read_sem, scatter_sem):
        first = _sc_worker() * n_chunks
        for s in range(n_slots):
            pltpu.sync_copy(pos_hbm.at[pl.ds(s * slot_chunks + first, n_chunks)], idx_v.at[s])
        read = lambda j: pltpu.async_copy(
            rows_hbm.at[pl.ds(pl.multiple_of((first + j) * SC_ROWS, SC_ROWS), SC_ROWS)],
            rows_v.at[j % 2], read_sem.at[j % 2])
        reads = [read(0)]
        scatters = []
        for j in range(n_chunks):
            reads[j].wait()
            if j + 1 < n_chunks:
                if j >= 1:
                    for copy in scatters[j - 1]:
                        copy.wait()
                reads.append(read(j + 1))
            scatters.append([pltpu.async_copy(rows_v.at[j % 2], out_hbm.at[idx_v.at[s, j]], scatter_sem.at[j % 2])
                             for s in range(n_slots)])
        for j in range(max(n_chunks - 2, 0), n_chunks):
            for copy in scatters[j]:
                copy.wait()

    return pl.kernel(
        body,
        out_type=jax.ShapeDtypeStruct((n_out, width), rows.dtype),
        mesh=_sc_mesh(),
        scratch_types=[pltpu.VMEM((n_slots, n_chunks, SC_ROWS), jnp.int32),
                       pltpu.VMEM((2, SC_ROWS, width), rows.dtype),
                       pltpu.SemaphoreType.DMA((2,)), pltpu.SemaphoreType.DMA((2,))],
        name="sc_row_scatter",
    )(rows, pos.reshape(n_slots * slot_chunks, SC_ROWS))


def _expert_kernel(tile_expert_ref, n_valid_ref, slot_ref, next_expert_ref, x_ref, wg_hbm, wu_hbm, wd_hbm,
                   y_ref, wg_buf, wu_buf, wd_buf, sem):
    step = pl.program_id(0)
    expert = tile_expert_ref[step]
    slot = slot_ref[step]
    valid = step < n_valid_ref[0]
    first_of_expert = (step == 0) | (expert != tile_expert_ref[jnp.maximum(step - 1, 0)])

    def weight_copies(which, into):
        pairs = ((wg_hbm, wg_buf), (wu_hbm, wu_buf), (wd_hbm, wd_buf))
        return [pltpu.make_async_copy(w.at[which], buf.at[into], sem.at[into, j]) for j, (w, buf) in enumerate(pairs)]

    @pl.when(step == 0)
    def _():
        for copy in weight_copies(expert, slot):
            copy.start()

    @pl.when(valid & first_of_expert)
    def _():
        for copy in weight_copies(expert, slot):
            copy.wait()
        upcoming = next_expert_ref[step]

        @pl.when(upcoming < N_EXPERTS)
        def _():
            for copy in weight_copies(upcoming, 1 - slot):
                copy.start()

    @pl.when(valid)
    def _():
        x = _unpack_bf16_pairs(x_ref[...])
        gate = jnp.dot(x, wg_buf[slot], preferred_element_type=F32)
        up = jnp.dot(x, wu_buf[slot], preferred_element_type=F32)
        hid = gate * jax.nn.sigmoid(gate) * up
        y_ref[...] = _pack_bf16_pairs(jnp.dot(_bf(hid), wd_buf[slot], preferred_element_type=F32))


def _expert_call(tile_expert, n_valid, slot, next_expert, x_sorted, w_gate, w_up, w_down, tm):
    n_rows = x_sorted.shape[0]
    rows = pl.BlockSpec((tm, D_MODEL // 2), lambda i, te, nv, sl, nx: (jnp.minimum(i, nv[0] - 1), 0))
    in_hbm = pl.BlockSpec(memory_space=pl.ANY)
    return pl.pallas_call(
        _expert_kernel,
        grid_spec=pltpu.PrefetchScalarGridSpec(
            num_scalar_prefetch=4,
            grid=(n_rows // tm,),
            in_specs=[rows, in_hbm, in_hbm, in_hbm],
            out_specs=rows,
            scratch_shapes=[pltpu.VMEM((2,) + w_gate.shape[1:], BF16), pltpu.VMEM((2,) + w_up.shape[1:], BF16),
                            pltpu.VMEM((2,) + w_down.shape[1:], BF16), pltpu.SemaphoreType.DMA((2, 3))]),
        out_shape=jax.ShapeDtypeStruct((n_rows, D_MODEL // 2), jnp.int32),
        compiler_params=pltpu.CompilerParams(
            dimension_semantics=("arbitrary",), vmem_limit_bytes=VMEM_LIMIT),
        name="moe_experts",
    )(tile_expert, n_valid, slot, next_expert, x_sorted, w_gate, w_up, w_down)


def _final_kernel(h_ref, y1_ref, y2_ref, route_ref, gfin_ref, *rest):
    out_ref = rest[-1]
    route = route_ref[...]
    gate1 = route[:, LANE_GATE1:LANE_GATE1 + 1]
    gate2 = route[:, LANE_GATE2:LANE_GATE2 + 1]
    moe = gate1 * _unpack_bf16_pairs(y1_ref[...]).astype(F32) + gate2 * _unpack_bf16_pairs(y2_ref[...]).astype(F32)
    out_ref[...] = _rms_norm(h_ref[...] + moe, gfin_ref[...])


def _final_call(h, y_pairs, route, g_final, earlier, first_token, tm):
    n_tok = h.shape[0]
    n_blocks = y_pairs.shape[0] // 2 // tm
    assert n_blocks * tm * 2 == y_pairs.shape[0] and first_token % tm == 0, "chunks must be whole row blocks"
    first = first_token // tm
    in_specs = [pl.BlockSpec((tm, D_MODEL), lambda i: (first + i, 0)),
                pl.BlockSpec((tm, D_MODEL // 2), lambda i: (i, 0)),
                pl.BlockSpec((tm, D_MODEL // 2), lambda i: (i + n_blocks, 0)),
                pl.BlockSpec((tm, ROUTER_LANES), lambda i: (first + i, 0)),
                pl.BlockSpec((1, D_MODEL), lambda i: (0, 0))]
    args = [h, y_pairs, y_pairs, route, g_final]
    aliases = {}
    if earlier is not None:
        in_specs.append(pl.BlockSpec(memory_space=pl.ANY))
        args.append(earlier)
        aliases = {len(args) - 1: 0}
    return pl.pallas_call(
        _final_kernel,
        grid=(n_blocks,),
        in_specs=in_specs,
        out_specs=pl.BlockSpec((tm, D_MODEL), lambda i: (first + i, 0)),
        out_shape=jax.ShapeDtypeStruct((n_tok, D_MODEL), F32),
        input_output_aliases=aliases,
        compiler_params=pltpu.CompilerParams(
            dimension_semantics=("arbitrary",), vmem_limit_bytes=VMEM_LIMIT),
        name="moe_combine_norm",
    )(*args)


def _dispatch_plan(route_t, counts, tm):
    n_tok = route_t.shape[1]
    n_tiles = (2 * n_tok) // tm + N_EXPERTS
    counts = counts[0, :N_EXPERTS].astype(jnp.int32)
    tiles_per = (counts + tm - 1) // tm
    tile_end = jnp.cumsum(tiles_per)
    row_start = (tile_end - tiles_per) * tm
    experts = jnp.arange(N_EXPERTS, dtype=jnp.int32)

    def position(idx_lane, rank_lane):
        idx = route_t[idx_lane].astype(jnp.int32)
        start = jnp.sum(jnp.where(idx[None, :] == experts[:, None], row_start[:, None], 0), axis=0)
        return start + route_t[rank_lane].astype(jnp.int32)

    pos = jnp.concatenate([position(LANE_IDX1, LANE_RANK1), position(LANE_IDX2, LANE_RANK2)])
    n_valid = tile_end[-1:]
    tile = jnp.minimum(jnp.arange(n_tiles, dtype=jnp.int32), n_valid - 1)
    tile_expert = jnp.sum((tile_end[None, :] <= tile[:, None]).astype(jnp.int32), axis=1)
    present = tiles_per > 0
    later = (experts[None, :] > experts[:, None]) & present[None, :]
    next_present = jnp.min(jnp.where(later, experts[None, :], N_EXPERTS), axis=1)
    order = jnp.cumsum(present.astype(jnp.int32)) - 1
    of_tile = lambda per_expert: jnp.sum(jnp.where(tile_expert[:, None] == experts[None, :], per_expert[None, :], 0), axis=1)
    return pos, n_tiles, tile_expert, n_valid, of_tile(order) % 2, of_tile(next_present)


def _block(x, norm_mix_g, w_in, rwkv_mu, conv_w, decay_up, decay_base, aaa_up, aaa_base, gate_up,
           k_k, k_a, r_k, ln_x_w, ln_x_b, w_out, norm_ffn_g, router_group_w, router_group_b,
           router_expert_w, router_expert_b, expert_w_gate, expert_w_up, expert_w_down, norm_final_g,
           *, tm_in, tb_rwkv, tm_out, tm_expert, tm_final):
    bsz, seq, d_model = x.shape
    n_tok = bsz * seq
    row = lambda p: p.reshape(1, -1)

    yconv, r, lw, k2, v, kk, kka, gate, bonus = _in_call(
        x, row(norm_mix_g), w_in, row(rwkv_mu), conv_w, decay_up, row(decay_base), aaa_up, row(aaa_base),
        gate_up, row(k_k), row(k_a), row(r_k), tm_in)
    o, w_gate_bf, w_up_bf, w_down_bf = _rwkv_call(r, lw, k2, v, kk, kka,
                                                  (expert_w_gate, expert_w_up, expert_w_down), tb_rwkv)

    pad = ROUTER_LANES - N_EXPERTS - N_GROUPS
    router_w = jnp.concatenate([router_expert_w, router_group_w, jnp.zeros((d_model, pad), F32)], axis=1)
    router_hi = _bf(router_w)
    router_split = jnp.concatenate([router_hi, _bf(router_w - router_hi.astype(F32))], axis=1)
    router_b = jnp.concatenate([router_expert_b, router_group_b, jnp.zeros((pad,), F32)]).reshape(1, -1)
    flat = lambda t: t.reshape(n_tok, t.shape[-1])
    h, u_pairs, route, route_t, counts = _out_call(
        flat(o), flat(bonus), flat(gate), flat(yconv), flat(x),
        row(ln_x_w), row(ln_x_b), w_out, row(norm_ffn_g), router_split, router_b, tm_out, OUT_SPLIT)

    pos, n_tiles, tile_expert, n_valid, slot, next_expert = _dispatch_plan(route_t, counts, tm_expert)
    x_sorted = _sc_scatter(u_pairs, pos, n_tiles * tm_expert)
    y_sorted = _expert_call(tile_expert, n_valid, slot, next_expert, x_sorted, w_gate_bf, w_up_bf, w_down_bf,
                            tm_expert)
    out, lo = None, 0
    for fraction in COMBINE_SPLIT:
        size = n_tok // fraction
        pos_c = jnp.concatenate([pos[lo:lo + size], pos[n_tok + lo:n_tok + lo + size]])
        out = _final_call(h, _sc_gather(y_sorted, pos_c), route, row(norm_final_g), out, lo, tm_final)
        lo += size
    assert lo == n_tok, "COMBINE_SPLIT must cover all tokens"
    return out.reshape(bsz, seq, d_model)


def kernel(x, norm_mix_g, w_in, rwkv_mu, conv_w, decay_up, decay_base, aaa_up, aaa_base, gate_up, k_k, k_a, r_k, ln_x_w, ln_x_b, w_out, norm_ffn_g, router_group_w, router_group_b, router_expert_w, router_expert_b, expert_w_gate, expert_w_up, expert_w_down, norm_final_g):
    return _block(x, norm_mix_g[0], w_in[0], rwkv_mu[0], conv_w[0], decay_up[0], decay_base[0],
                  aaa_up[0], aaa_base[0], gate_up[0], k_k[0], k_a[0], r_k[0].reshape(-1), ln_x_w[0],
                  ln_x_b[0], w_out[0], norm_ffn_g[0], router_group_w[0], router_group_b[0],
                  router_expert_w[0], router_expert_b[0], expert_w_gate[0], expert_w_up[0],
                  expert_w_down[0], norm_final_g,
                  tm_in=512, tb_rwkv=1024, tm_out=1024, tm_expert=512, tm_final=1024)
```

```python
import functools

import jax
import jax.numpy as jnp
from jax import lax
from jax.experimental import pallas as pl
from jax.experimental.pallas import tpu as pltpu
from jax.experimental.pallas import tpu_sc as plsc

F32 = jnp.float32
BF16 = jnp.bfloat16

D_MODEL = 1024
D_CONV = 512
CONV_WIDTH = 3
N_HEADS = 8
HEAD = 64
D_RWKV = N_HEADS * HEAD
LORA_WA = 128
GATE_LORA = 128
D_RWKV_PROJ = 3 * D_RWKV + LORA_WA + GATE_LORA
D_IN = 3 * D_CONV + D_RWKV_PROJ
N_GROUPS = 4
EXPERTS_PER_GROUP = 8
N_EXPERTS = N_GROUPS * EXPERTS_PER_GROUP
D_EXPERT = D_MODEL // 4
RMS_EPS = 1e-6
LN_X_EPS = 64e-5
L2_EPS = 1e-12

SUBLANES = 8
CHUNK = 64
QUAD = 4 * HEAD
GROUP = 2 * HEAD
INV_BASE = 8
WAVES = (8, 4, 4)
WAVE = 4
PRODUCTS_PER_CHAIN = 15
COMPACT_FROM = 64
OUT_SPLIT = 8
COMBINE_SPLIT = (4, 4, 4, 4)
ROUTER_LANES = 128

VMEM_LIMIT = 56 * 1024 * 1024


def _bf(x):
    return x.astype(BF16)


class _Interleaver:
    def __init__(self):
        self.stages, self.every, self.calls, self.busy = None, 1, 0, False

    def start(self, stages, every):
        self.stages, self.every, self.calls = stages, every, 0

    def tick(self):
        if self.stages is None or self.busy:
            return
        self.calls += 1
        if self.calls % self.every == 0:
            self.busy = True
            next(self.stages, None)
            self.busy = False

    def drain(self):
        self.busy = True
        for _ in self.stages:
            pass
        self.stages, self.busy = None, False


_INTERLEAVE = _Interleaver()


def _mm(a, b):
    _INTERLEAVE.tick()
    return jnp.dot(_bf(a), _bf(b), preferred_element_type=F32)


def _mm_nt(a, b):
    _INTERLEAVE.tick()
    return lax.dot_general(_bf(a), _bf(b), (((1,), (1,)), ((), ())), preferred_element_type=F32)


def _mm_exact_lhs(lhs_bf16, x, passes):
    acc = None
    rem = x
    for _ in range(passes):
        piece = _bf(rem)
        part = jnp.dot(lhs_bf16, piece, preferred_element_type=F32)
        acc = part if acc is None else acc + part
        rem = rem - piece.astype(F32)
    return acc


def _head_ones():
    return (lax.broadcasted_iota(jnp.int32, (QUAD, QUAD), 0) // HEAD
            == lax.broadcasted_iota(jnp.int32, (QUAD, QUAD), 1) // HEAD).astype(BF16)


def _head_sum(x, ones_quad):
    xb = _bf(x)
    return jnp.concatenate(
        [jnp.dot(xb[:, q * QUAD:(q + 1) * QUAD], ones_quad, preferred_element_type=F32)
         for q in range(x.shape[1] // QUAD)], axis=1)


def _rms_norm(x, g):
    return x * lax.rsqrt(jnp.mean(x * x, axis=-1, keepdims=True) + RMS_EPS) * g


def _shift_rows(cur, prev_rows, k):
    rolled = pltpu.roll(cur, k, 0)
    prev_rolled = pltpu.roll(prev_rows, k, 0)
    n = cur.shape[0]
    head = jnp.concatenate([prev_rolled, rolled[SUBLANES:]], axis=0) if n > SUBLANES else prev_rolled
    row = lax.broadcasted_iota(jnp.int32, cur.shape, 0)
    return jnp.where(row < k, head, rolled)


def _in_kernel(x_ref, g_ref, w_ref, mu_ref, convw_ref, dup_ref, dbase_ref, aup_ref, abase_ref,
               gup_ref, kk_ref, ka_ref, rk_ref,
               yconv_ref, r_ref, lw_ref, k_ref, v_ref, kkn_ref, kka_ref, gate_ref, bonus_ref,
               carry_ref, wbf_ref):
    @pl.when((pl.program_id(0) == 0) & (pl.program_id(1) == 0))
    def _():
        wbf_ref[...] = _bf(w_ref[...])

    @pl.when(pl.program_id(1) == 0)
    def _():
        carry_ref[...] = jnp.zeros_like(carry_ref)

    u = _bf(_rms_norm(x_ref[0], g_ref[...]))
    n_conv = 3 * D_CONV
    rk0 = n_conv
    bounds = dict(conv=(0, n_conv), lora=(rk0 + 3 * D_RWKV, D_IN), k=(rk0 + D_RWKV, rk0 + 2 * D_RWKV),
                  r=(rk0, rk0 + D_RWKV), v=(rk0 + 2 * D_RWKV, rk0 + 3 * D_RWKV))
    z = {name: jnp.dot(u, wbf_ref[:, lo:hi], preferred_element_type=F32) for name, (lo, hi) in bounds.items()}

    def lerp(name):
        lo, hi = bounds[name]
        cur = z[name]
        prev = carry_ref[:, D_CONV + lo - n_conv:D_CONV + hi - n_conv]
        mixed = cur + (_shift_rows(cur, prev, 1) - cur) * mu_ref[:, lo - n_conv:hi - n_conv]
        carry_ref[:, D_CONV + lo - n_conv:D_CONV + hi - n_conv] = cur[-SUBLANES:]
        return mixed

    b_gate = z["conv"][:, :D_CONV]
    ch = z["conv"][:, D_CONV:2 * D_CONV] * z["conv"][:, 2 * D_CONV:]
    prev_ch = carry_ref[:, :D_CONV]
    conv = convw_ref[CONV_WIDTH - 1:CONV_WIDTH, :] * ch
    for delay in range(1, CONV_WIDTH):
        tap = CONV_WIDTH - 1 - delay
        conv = conv + convw_ref[tap:tap + 1, :] * _shift_rows(ch, prev_ch, delay)
    yconv_ref[0] = _bf(b_gate * conv)
    carry_ref[:, :D_CONV] = ch[-SUBLANES:]

    lora = lerp("lora")
    wa_lo = lora[:, :LORA_WA]
    g_lo = lora[:, LORA_WA:]
    no_rows = jnp.zeros_like(dup_ref[...])
    decay_up = _bf(jnp.concatenate([dup_ref[...], no_rows], axis=0))
    rate_up = _bf(jnp.concatenate([no_rows, aup_ref[...]], axis=0))
    dec_in = -(dbase_ref[...] + jnp.dot(_bf(jnp.tanh(wa_lo)), decay_up, preferred_element_type=F32))
    softplus = jnp.maximum(dec_in, 0.0) + jnp.log(1.0 + jnp.exp(-jnp.abs(dec_in)))
    w = -softplus - 0.5
    lw_ref[0] = -jnp.exp(w)
    a = jax.nn.sigmoid(abase_ref[...] + jnp.dot(_bf(wa_lo), rate_up, preferred_element_type=F32))
    gate_ref[0] = _bf(jnp.dot(_bf(jax.nn.sigmoid(g_lo)), _bf(gup_ref[...]), preferred_element_type=F32))

    ones_quad = _head_ones()
    k = lerp("k")
    kk = k * kk_ref[...]
    norm = jnp.sqrt(_head_sum(kk * kk, ones_quad))
    kk = kk / jnp.maximum(norm, L2_EPS)
    k2 = k * (1.0 + (a - 1.0) * ka_ref[...])
    k_ref[0] = _bf(k2)
    kkn_ref[0] = _bf(kk)
    kka_ref[0] = _bf(kk * a)
    r = lerp("r")
    r_ref[0] = _bf(r)
    v = lerp("v")
    v_ref[0] = _bf(v)
    bonus_ref[0] = _bf(_head_sum(r * k2 * rk_ref[...], ones_quad) * v)


def _in_call(x, g, w_in, mu, conv_w, dup, dbase, aup, abase, gup, k_k, k_a, r_k, tm):
    bsz, seq, _ = x.shape
    full = lambda arr: pl.BlockSpec(arr.shape, lambda b, t: (0,) * arr.ndim, pipeline_mode=pl.Buffered(1))
    out_spec = pl.BlockSpec((1, tm, D_RWKV), lambda b, t: (b, t, 0))
    out_dtypes = (BF16, BF16, F32, BF16, BF16, BF16, BF16, BF16, BF16)
    params = (g, w_in, mu, conv_w, dup, dbase, aup, abase, gup, k_k, k_a, r_k)
    return pl.pallas_call(
        _in_kernel,
        grid=(bsz, seq // tm),
        in_specs=[pl.BlockSpec((1, tm, D_MODEL), lambda b, t: (b, t, 0))] + [full(p) for p in params],
        out_specs=[out_spec] * 9,
        out_shape=[jax.ShapeDtypeStruct((bsz, seq, D_RWKV), dt) for dt in out_dtypes],
        scratch_shapes=[pltpu.VMEM((SUBLANES, D_CONV + D_RWKV_PROJ), F32), pltpu.VMEM(w_in.shape, BF16)],
        compiler_params=pltpu.CompilerParams(
            dimension_semantics=("arbitrary", "arbitrary"), vmem_limit_bytes=VMEM_LIMIT),
        name="in_proj",
    )(x, *params)


def _block_diag(y, bd_mask):
    return jnp.where(bd_mask, jnp.concatenate([y] * (GROUP // CHUNK), axis=0), 0.0)


def _each(fn, *lists):
    return [fn(*args) for args in zip(*lists)]


def _unit_lower_inverse(a_strict, t_idx, s_idx, bd):
    bdmm = lambda xs, ys: _each(lambda x, y: _mm(x, bd(y)), xs, ys)
    eye = (t_idx == s_idx).astype(F32)
    same8 = (t_idx // INV_BASE) == (s_idx // INV_BASE)
    a8 = _each(lambda a: jnp.where(same8, a, 0.0), a_strict)
    a8_2 = bdmm(a8, a8)
    a8_34 = bdmm(_each(lambda a, b: jnp.concatenate([a, b], axis=0), a8, a8_2), a8_2)
    inv = _each(lambda a, b, c: eye + a + b + c[:CHUNK], a8, a8_2, a8_34)
    inv = _each(jnp.add, inv, bdmm(inv, _each(lambda c: c[CHUNK:], a8_34)))
    size = 2 * INV_BASE
    while size < COMPACT_FROM:
        off = ((t_idx // size) == (s_idx // size)) & ((t_idx // (size // 2)) != (s_idx // (size // 2)))
        cross = bdmm(_each(lambda a: jnp.where(off, a, 0.0), a_strict), inv)
        inv = _each(jnp.add, inv, bdmm(inv, cross))
        size *= 2
    while size <= CHUNK:
        inv = _compact_level(a_strict, inv, size)
        size *= 2
    return inv


def _compact_level(a_strict, inv, size):
    half = size // 2
    heads = GROUP // CHUNK
    blocks = CHUNK // size
    lane_starts = [h * CHUNK + b * size for h in range(heads) for b in range(blocks)]
    row_starts = [b * size for b in range(blocks)]
    rows_first = lambda x: jnp.concatenate([x[r:r + half] for r in row_starts], axis=0)
    rows_second = lambda x: jnp.concatenate([x[r + half:r + size] for r in row_starts], axis=0)
    lanes_first = lambda x: jnp.concatenate([x[:, c:c + half] for c in lane_starts], axis=1)
    lanes_second = lambda x: jnp.concatenate([x[:, c + half:c + size] for c in lane_starts], axis=1)
    n_rows, n_lanes = CHUNK // 2, GROUP // 2
    own = (lax.broadcasted_iota(jnp.int32, (n_rows, n_lanes), 0) // half
           == (lax.broadcasted_iota(jnp.int32, (n_rows, n_lanes), 1) // half) % blocks)
    diag = (lax.broadcasted_iota(jnp.int32, (n_lanes, n_lanes), 0) // half
            == lax.broadcasted_iota(jnp.int32, (n_lanes, n_lanes), 1) // half)
    bd_half = lambda y: jnp.where(diag, jnp.concatenate([y] * heads, axis=0), 0.0)
    a21 = _each(lambda a: jnp.where(own, lanes_first(rows_second(a)), 0.0), a_strict)
    t11 = _each(lambda t: lanes_first(rows_first(t)), inv)
    t22 = _each(lambda t: lanes_second(rows_second(t)), inv)
    a21_t11 = _each(lambda x, y: _mm(x, bd_half(y)), a21, t11)
    new = _each(lambda x, y: _mm(x, bd_half(y)), t22, a21_t11)

    def placed(x):
        zero_lanes = jnp.zeros((n_rows, half), F32)
        wide = jnp.concatenate(sum(([x[:, i * half:(i + 1) * half], zero_lanes] for i in range(heads * blocks)), []),
                               axis=1)
        zero_rows = jnp.zeros((half, GROUP), F32)
        return jnp.concatenate(sum(([zero_rows, wide[b * half:(b + 1) * half]] for b in range(blocks)), []), axis=0)

    return _each(lambda t, x: t + placed(x), inv, new)


def _chunk_local(r, lw, k, v, kk, kka, tri, t_idx, s_idx, bd_mask):
    bd = lambda y: _block_diag(y, bd_mask)
    bdmm = lambda xs, ys: _each(lambda x, y: _mm(x, bd(y)), xs, ys)
    bdmm2 = lambda xs, ys, zs: _each(lambda x, y, z: _mm(x, jnp.concatenate([bd(y), bd(z)], axis=1)), xs, ys, zs)
    left = lambda xs: _each(lambda x: x[:, :GROUP], xs)
    right = lambda xs: _each(lambda x: x[:, GROUP:], xs)
    top = lambda xs: _each(lambda x: x[:CHUNK], xs)
    bottom = lambda xs: _each(lambda x: x[CHUNK:], xs)

    cum = _each(lambda x: _mm_exact_lhs(tri, x, 2), lw)
    cum_last = _each(lambda c: c[CHUNK - 1:CHUNK, :], cum)
    p_incl = _each(jnp.exp, cum)
    p_excl = _each(lambda c, x: jnp.exp(c - x), cum, lw)
    p_inv = _each(lambda c: jnp.exp(-c), cum)
    to_end = _each(lambda cl, c: jnp.exp(cl - c), cum_last, cum)
    a_t = _each(lambda x, p: -x * p, kk, p_excl)
    r_t = _each(jnp.multiply, r, p_incl)
    b_t = _each(jnp.multiply, kka, p_inv)
    k_t = _each(jnp.multiply, k, p_inv)
    b_end = _each(jnp.multiply, kka, to_end)
    k_end = _each(jnp.multiply, k, to_end)

    ar = _each(lambda a, b: jnp.concatenate([a, b], axis=0), a_t, r_t)
    row2 = lax.broadcasted_iota(jnp.int32, (2 * CHUNK, 2 * GROUP), 0)
    col2 = lax.broadcasted_iota(jnp.int32, (2 * CHUNK, 2 * GROUP), 1) % CHUNK
    causal = col2 < (row2 % CHUNK) + (row2 // CHUNK)
    scores = _each(lambda x, y, z: jnp.where(causal, _mm_nt(x, jnp.concatenate([bd(y), bd(z)], axis=0)), 0.0),
                   ar, b_t, k_t)
    a_ab = top(left(scores))
    a_rb = bottom(left(scores))

    inv = _unit_lower_inverse(a_ab, t_idx, s_idx, bd)
    kv = bdmm(right(scores), v)
    wu = bdmm2(inv, a_t, top(kv))
    ro = bdmm2(a_rb, left(wu), right(wu))
    r_hat = _each(jnp.add, r_t, left(ro))
    o_loc = _each(jnp.add, right(ro), bottom(kv))

    wu_b = _each(lambda x, b: _mm(x.T, b), wu, b_end)
    trans = _each(lambda x: jnp.where(bd_mask, x[:GROUP], 0.0), wu_b)
    d_bd = _each(lambda x, y, z: jnp.where(bd_mask, x[GROUP:] + _mm(y.T, z), 0.0), wu_b, v, k_end)
    d_ls = _each(lambda d: sum(d[h * CHUNK:(h + 1) * CHUNK] for h in range(1, GROUP // CHUNK)) + d[:CHUNK], d_bd)
    p_end = _each(jnp.exp, cum_last)
    return r_hat, o_loc, trans, d_ls, p_end


def _rwkv_kernel(r_ref, lw_ref, k_ref, v_ref, kk_ref, kka_ref, wg_ref, wu_ref, wd_ref,
                 o_ref, wg_bf_ref, wu_bf_ref, wd_bf_ref, state_ref, *, n_sub):
    @pl.when(pl.program_id(1) == 0)
    def _():
        state_ref[...] = jnp.zeros_like(state_ref)

    wg_bf_ref[...] = _bf(wg_ref[...])
    wu_bf_ref[...] = _bf(wu_ref[...])
    wd_bf_ref[...] = _bf(wd_ref[...])

    t_idx = lax.broadcasted_iota(jnp.int32, (CHUNK, GROUP), 0)
    s_idx = lax.broadcasted_iota(jnp.int32, (CHUNK, GROUP), 1) % CHUNK
    bd_mask = (lax.broadcasted_iota(jnp.int32, (GROUP, GROUP), 0) // CHUNK
               == lax.broadcasted_iota(jnp.int32, (GROUP, GROUP), 1) // CHUNK)
    chunk_row = lax.broadcasted_iota(jnp.int32, (CHUNK, CHUNK), 0)
    chunk_col = lax.broadcasted_iota(jnp.int32, (CHUNK, CHUNK), 1)
    n_group = D_RWKV // GROUP
    tri = (chunk_row >= chunk_col).astype(BF16)
    state = [state_ref[g] for g in range(n_group)]

    def through_state(local, where):
        r_hat, o_loc, trans, d_ls, p_end = local
        for first in range(0, len(where), n_group):
            chains = range(first, first + n_group)
            out = [_mm_nt(r_hat[i], _block_diag(state[g], bd_mask)) + o_loc[i] for g, i in enumerate(chains)]
            for g, i in enumerate(chains):
                rows, lanes = where[i]
                o_ref[0, rows, lanes] = _bf(out[g])
            yield
            state[:] = [state[g] * p_end[i] + _mm(state[g], trans[i]) + d_ls[i] for g, i in enumerate(chains)]
            yield

    pending, start, stages = None, 0, 0
    waves = WAVES if sum(WAVES) == n_sub else [min(WAVE, n_sub - s) for s in range(0, n_sub, WAVE)]
    for wave in waves:
        where = [(slice(c * CHUNK, (c + 1) * CHUNK), slice(g * GROUP, (g + 1) * GROUP))
                 for c in range(start, start + wave) for g in range(n_group)]
        load = lambda ref: [ref[0, rows, lanes].astype(F32) for rows, lanes in where]
        if pending is not None:
            _INTERLEAVE.start(pending, len(where) * PRODUCTS_PER_CHAIN // (stages + 1))
        local = _chunk_local(load(r_ref), load(lw_ref), load(k_ref), load(v_ref), load(kk_ref), load(kka_ref),
                             tri, t_idx, s_idx, bd_mask)
        if pending is not None:
            _INTERLEAVE.drain()
        pending, start, stages = through_state(local, where), start + wave, 2 * wave
    for _ in pending:
        pass
    for g in range(n_group):
        state_ref[g] = state[g]


def _rwkv_call(r, lw, k, v, kk, kka, expert_weights, tb):
    bsz, seq, _ = r.shape
    n_t = seq // tb
    per_step = N_EXPERTS // (bsz * n_t)
    assert per_step * bsz * n_t == N_EXPERTS, "grid steps must divide the expert count"
    spec = pl.BlockSpec((1, tb, D_RWKV), lambda b, t: (b, t, 0))
    w_specs = [pl.BlockSpec((per_step,) + w.shape[1:], lambda b, t: (b * n_t + t, 0, 0)) for w in expert_weights]
    return pl.pallas_call(
        functools.partial(_rwkv_kernel, n_sub=tb // CHUNK),
        grid=(bsz, n_t),
        in_specs=[spec] * 6 + w_specs,
        out_specs=[spec] + w_specs,
        out_shape=[jax.ShapeDtypeStruct((bsz, seq, D_RWKV), BF16)]
        + [jax.ShapeDtypeStruct(w.shape, BF16) for w in expert_weights],
        scratch_shapes=[pltpu.VMEM((D_RWKV // GROUP, HEAD, GROUP), F32)],
        compiler_params=pltpu.CompilerParams(
            dimension_semantics=("arbitrary", "arbitrary"), vmem_limit_bytes=VMEM_LIMIT),
        name="rwkv_chunk",
    )(r, lw, k, v, kk, kka, *expert_weights)


def _route(logits):
    lane_i = lax.broadcasted_iota(jnp.int32, logits.shape, 1)
    lane = lane_i.astype(F32)
    lane_group = (lane_i // EXPERTS_PER_GROUP).astype(F32)
    neg = -jnp.inf
    big = float(ROUTER_LANES)
    is_group = (lane_i >= N_EXPERTS) & (lane_i < N_EXPERTS + N_GROUPS)
    gl = jnp.where(is_group, logits, neg)
    ge = jnp.exp(gl - jnp.max(gl, axis=-1, keepdims=True))
    gprob = ge / jnp.sum(ge, axis=-1, keepdims=True)
    g_top = jnp.max(gprob, axis=-1, keepdims=True)
    g_idx = jnp.min(jnp.where(is_group & (gprob == g_top), lane - N_EXPERTS, big), axis=-1, keepdims=True)

    in_group = (lane_i < N_EXPERTS) & (lane_group == g_idx)
    el = jnp.where(in_group, logits, neg)
    top1 = jnp.max(el, axis=-1, keepdims=True)
    idx1 = jnp.min(jnp.where(in_group & (el == top1), lane, big), axis=-1, keepdims=True)
    el2 = jnp.where(lane == idx1, neg, el)
    top2 = jnp.max(el2, axis=-1, keepdims=True)
    idx2 = jnp.min(jnp.where(in_group & (lane != idx1) & (el2 == top2), lane, big), axis=-1, keepdims=True)
    e2 = jnp.exp(top2 - top1)
    denom = 1.0 + e2
    return idx1, idx2, g_top * (1.0 / denom), g_top * (e2 / denom)


def _pack_bf16_pairs(x):
    bits = lax.bitcast_convert_type(_bf(x).astype(F32), jnp.uint32)
    k = x.shape[1] // 2
    return lax.bitcast_convert_type((bits[:, :k] >> 16) | bits[:, k:], jnp.int32)


def _unpack_bf16_pairs(words):
    bits = lax.bitcast_convert_type(words, jnp.uint32)
    lo = lax.bitcast_convert_type(bits << 16, F32)
    hi = lax.bitcast_convert_type(bits & jnp.uint32(0xFFFF0000), F32)
    return _bf(jnp.concatenate([lo, hi], axis=1))


LANE_IDX1, LANE_IDX2, LANE_RANK1, LANE_RANK2, LANE_GATE1, LANE_GATE2 = range(6)


def _out_kernel(o_ref, bonus_ref, gate_ref, yconv_ref, x_ref, lnw_ref, lnb_ref,
                wout_ref, gffn_ref, rw_ref, rb_ref, h_ref, u_ref, route_ref, route_t_ref, count_ref,
                seen_ref, wbf_ref, *, n_split):
    @pl.when(pl.program_id(0) == 0)
    def _():
        seen_ref[...] = jnp.zeros_like(seen_ref)
        wbf_ref[...] = _bf(wout_ref[...])

    sub = o_ref.shape[0] // n_split
    parts = [slice(s * sub, (s + 1) * sub) for s in range(n_split)]
    read = lambda ref: [ref[p, :] for p in parts]
    ones_quad = _head_ones()
    inv_n = 1.0 / HEAD
    o = _each(lambda v: v.astype(F32), read(o_ref))
    mean = _each(lambda v: _head_sum(v, ones_quad) * inv_n, o)
    cen = _each(jnp.subtract, o, mean)
    var = _each(lambda c: _head_sum(c * c, ones_quad) * inv_n, cen)
    on = _each(lambda c, v: c * lax.rsqrt(v + LN_X_EPS) * lnw_ref[...] + lnb_ref[...], cen, var)
    y_rwkv = _each(lambda a, b, g: (a + b) * g, on, read(bonus_ref), read(gate_ref))
    mixed = _each(lambda yc, yr: (jnp.dot(yc, wbf_ref[:D_CONV, :], preferred_element_type=F32)
                                  + jnp.dot(_bf(yr), wbf_ref[D_CONV:, :], preferred_element_type=F32)),
                  read(yconv_ref), y_rwkv)
    h = _each(jnp.add, read(x_ref), mixed)
    u = _each(lambda v: _rms_norm(v, gffn_ref[...]), h)
    u_hi = _each(_bf, u)
    u_lo = _each(lambda a, b: _bf(a - b.astype(F32)), u, u_hi)
    by_hi = _each(lambda a: jnp.dot(a, rw_ref[...], preferred_element_type=F32), u_hi)
    by_lo = _each(lambda a: jnp.dot(a, rw_ref[:, :ROUTER_LANES], preferred_element_type=F32), u_lo)
    logits = _each(lambda a, b: a[:, :ROUTER_LANES] + a[:, ROUTER_LANES:] + b + rb_ref[...], by_hi, by_lo)
    routed = _each(_route, logits)

    lane_i = lax.broadcasted_iota(jnp.int32, (sub, ROUTER_LANES), 1)
    lane = lane_i.astype(F32)
    hit1 = _each(lambda rt: lane == rt[0], routed)
    hit2 = _each(lambda rt: lane == rt[1], routed)
    both = _each(lambda a, b: a.astype(F32) + b.astype(F32), hit1, hit2)
    earlier = (lax.broadcasted_iota(jnp.int32, (sub, sub), 0) > lax.broadcasted_iota(jnp.int32, (sub, sub), 1))
    inside = _each(lambda b: jnp.dot(earlier.astype(BF16), _bf(b), preferred_element_type=F32), both)
    seen = seen_ref[...]
    for s, p in enumerate(parts):
        idx1, idx2, gate1, gate2 = routed[s]
        before = inside[s] + seen
        rank1 = jnp.sum(jnp.where(hit1[s], before, 0.0), axis=-1, keepdims=True)
        rank2 = jnp.sum(jnp.where(hit2[s], before, 0.0), axis=-1, keepdims=True)
        seen = seen + jnp.sum(both[s], axis=0, keepdims=True)
        route = jnp.zeros((sub, ROUTER_LANES), F32)
        for lane_id, col in ((LANE_IDX1, idx1), (LANE_IDX2, idx2), (LANE_RANK1, rank1),
                             (LANE_RANK2, rank2), (LANE_GATE1, gate1), (LANE_GATE2, gate2)):
            route = jnp.where(lane_i == lane_id, col, route)
        route_ref[p, :] = route
        route_t_ref[:, p] = route.T[:SUBLANES, :]
        h_ref[p, :] = _bf(h[s])
        u_ref[p, :] = _pack_bf16_pairs(u[s])
    seen_ref[...] = seen
    count_ref[...] = jnp.broadcast_to(seen, count_ref.shape)


def _out_call(o, bonus, gate, yconv, x, ln_w, ln_b, w_out, g_ffn, router_w, router_b, tm, n_split):
    n_tok = x.shape[0]
    row = lambda width: pl.BlockSpec((tm, width), lambda i: (i, 0))
    full = lambda arr: pl.BlockSpec(arr.shape, lambda i: (0,) * arr.ndim, pipeline_mode=pl.Buffered(1))
    params = (ln_w, ln_b, w_out, g_ffn, router_w, router_b)
    return pl.pallas_call(
        functools.partial(_out_kernel, n_split=n_split),
        grid=(n_tok // tm,),
        in_specs=[row(D_RWKV)] * 4 + [row(D_MODEL)] + [full(p) for p in params],
        out_specs=[row(D_MODEL), row(D_MODEL // 2), row(ROUTER_LANES),
                   pl.BlockSpec((SUBLANES, tm), lambda i: (0, i)),
                   pl.BlockSpec((SUBLANES, ROUTER_LANES), lambda i: (0, 0))],
        out_shape=[jax.ShapeDtypeStruct((n_tok, D_MODEL), BF16),
                   jax.ShapeDtypeStruct((n_tok, D_MODEL // 2), jnp.int32),
                   jax.ShapeDtypeStruct((n_tok, ROUTER_LANES), F32),
                   jax.ShapeDtypeStruct((SUBLANES, n_tok), F32),
                   jax.ShapeDtypeStruct((SUBLANES, ROUTER_LANES), F32)],
        scratch_shapes=[pltpu.VMEM((1, ROUTER_LANES), F32), pltpu.VMEM(w_out.shape, BF16)],
        compiler_params=pltpu.CompilerParams(
            dimension_semantics=("arbitrary",), vmem_limit_bytes=VMEM_LIMIT),
        name="out_proj_route",
    )(o, bonus, gate, yconv, x, *params)


SC_CORES = 2
SC_SUBCORES = 16
SC_ROWS = 64


def _sc_mesh():
    return plsc.VectorSubcoreMesh(core_axis_name="c", subcore_axis_name="s",
                                  num_cores=SC_CORES, num_subcores=SC_SUBCORES)


def _sc_worker():
    return lax.axis_index("s") * SC_CORES + lax.axis_index("c")


def _sc_gather(table, idx):
    n_rows = idx.shape[0]
    width = table.shape[1]
    n_chunks = n_rows // (SC_CORES * SC_SUBCORES * SC_ROWS)

    def body(table_hbm, idx_hbm, out_hbm, idx_v, rows_v, gather_sem, write_sem):
        first = _sc_worker() * n_chunks
        pltpu.sync_copy(idx_hbm.at[pl.ds(first, n_chunks)], idx_v)
        gather = lambda j: pltpu.async_copy(table_hbm.at[idx_v.at[j]], rows_v.at[j % 2], gather_sem.at[j % 2])
        gathers = [gather(0)]
        writes = []
        for j in range(n_chunks):
            gathers[j].wait()
            if j + 1 < n_chunks:
                if j >= 1:
                    writes[j - 1].wait()
                gathers.append(gather(j + 1))
            dst = out_hbm.at[pl.ds(pl.multiple_of((first + j) * SC_ROWS, SC_ROWS), SC_ROWS)]
            writes.append(pltpu.async_copy(rows_v.at[j % 2], dst, write_sem.at[j % 2]))
        for j in range(max(n_chunks - 2, 0), n_chunks):
            writes[j].wait()

    return pl.kernel(
        body,
        out_type=jax.ShapeDtypeStruct((n_rows, width), table.dtype),
        mesh=_sc_mesh(),
        scratch_types=[pltpu.VMEM((n_chunks, SC_ROWS), jnp.int32), pltpu.VMEM((2, SC_ROWS, width), table.dtype),
                       pltpu.SemaphoreType.DMA((2,)), pltpu.SemaphoreType.DMA((2,))],
        name="sc_row_gather",
    )(table, idx.reshape(n_rows // SC_ROWS, SC_ROWS))


def _sc_scatter(rows, pos, n_out):
    n_rows, width = rows.shape
    n_slots = pos.shape[0] // n_rows
    slot_chunks = n_rows // SC_ROWS
    n_chunks = slot_chunks // (SC_CORES * SC_SUBCORES)

    def body(rows_hbm, pos_hbm, out_hbm, idx_v, rows_v, read_sem, scatter_sem):
        first = _sc_worker() * n_chunks
        for s in range(n_slots):
            pltpu.sync_copy(pos_hbm.at[pl.ds(s * slot_chunks + first, n_chunks)], idx_v.at[s])
        read = lambda j: pltpu.async_copy(
            rows_hbm.at[pl.ds(pl.multiple_of((first + j) * SC_ROWS, SC_ROWS), SC_ROWS)],
            rows_v.at[j % 2], read_sem.at[j % 2])
        reads = [read(0)]
        scatters = []
        for j in range(n_chunks):
            reads[j].wait()
            if j + 1 < n_chunks:
                if j >= 1:
                    for copy in scatters[j - 1]:
                        copy.wait()
                reads.append(read(j + 1))
            scatters.append([pltpu.async_copy(rows_v.at[j % 2], out_hbm.at[idx_v.at[s, j]], scatter_sem.at[j % 2])
                             for s in range(n_slots)])
        for j in range(max(n_chunks - 2, 0), n_chunks):
            for copy in scatters[j]:
                copy.wait()

    return pl.kernel(
        body,
        out_type=jax.ShapeDtypeStruct((n_out, width), rows.dtype),
        mesh=_sc_mesh(),
        scratch_types=[pltpu.VMEM((n_slots, n_chunks, SC_ROWS), jnp.int32),
                       pltpu.VMEM((2, SC_ROWS, width), rows.dtype),
                       pltpu.SemaphoreType.DMA((2,)), pltpu.SemaphoreType.DMA((2,))],
        name="sc_row_scatter",
    )(rows, pos.reshape(n_slots * slot_chunks, SC_ROWS))


def _expert_kernel(tile_expert_ref, n_valid_ref, slot_ref, next_expert_ref, x_ref, wg_hbm, wu_hbm, wd_hbm,
                   y_ref, wg_buf, wu_buf, wd_buf, sem):
    step = pl.program_id(0)
    expert = tile_expert_ref[step]
    slot = slot_ref[step]
    valid = step < n_valid_ref[0]
    first_of_expert = (step == 0) | (expert != tile_expert_ref[jnp.maximum(step - 1, 0)])

    def weight_copies(which, into):
        pairs = ((wg_hbm, wg_buf), (wu_hbm, wu_buf), (wd_hbm, wd_buf))
        return [pltpu.make_async_copy(w.at[which], buf.at[into], sem.at[into, j]) for j, (w, buf) in enumerate(pairs)]

    @pl.when(step == 0)
    def _():
        for copy in weight_copies(expert, slot):
            copy.start()

    @pl.when(valid & first_of_expert)
    def _():
        for copy in weight_copies(expert, slot):
            copy.wait()
        upcoming = next_expert_ref[step]

        @pl.when(upcoming < N_EXPERTS)
        def _():
            for copy in weight_copies(upcoming, 1 - slot):
                copy.start()

    @pl.when(valid)
    def _():
        x = _unpack_bf16_pairs(x_ref[...])
        gate = jnp.dot(x, wg_buf[slot], preferred_element_type=F32)
        up = jnp.dot(x, wu_buf[slot], preferred_element_type=F32)
        hid = gate * jax.nn.sigmoid(gate) * up
        y_ref[...] = _pack_bf16_pairs(jnp.dot(_bf(hid), wd_buf[slot], preferred_element_type=F32))


def _expert_call(tile_expert, n_valid, slot, next_expert, x_sorted, w_gate, w_up, w_down, tm):
    n_rows = x_sorted.shape[0]
    rows = pl.BlockSpec((tm, D_MODEL // 2), lambda i, te, nv, sl, nx: (jnp.minimum(i, nv[0] - 1), 0))
    in_hbm = pl.BlockSpec(memory_space=pl.ANY)
    return pl.pallas_call(
        _expert_kernel,
        grid_spec=pltpu.PrefetchScalarGridSpec(
            num_scalar_prefetch=4,
            grid=(n_rows // tm,),
            in_specs=[rows, in_hbm, in_hbm, in_hbm],
            out_specs=rows,
            scratch_shapes=[pltpu.VMEM((2,) + w_gate.shape[1:], BF16), pltpu.VMEM((2,) + w_up.shape[1:], BF16),
                            pltpu.VMEM((2,) + w_down.shape[1:], BF16), pltpu.SemaphoreType.DMA((2, 3))]),
        out_shape=jax.ShapeDtypeStruct((n_rows, D_MODEL // 2), jnp.int32),
        compiler_params=pltpu.CompilerParams(
            dimension_semantics=("arbitrary",), vmem_limit_bytes=VMEM_LIMIT),
        name="moe_experts",
    )(tile_expert, n_valid, slot, next_expert, x_sorted, w_gate, w_up, w_down)


def _final_kernel(h_ref, y1_ref, y2_ref, route_ref, gfin_ref, *rest):
    out_ref = rest[-1]
    route = route_ref[...]
    gate1 = route[:, LANE_GATE1:LANE_GATE1 + 1]
    gate2 = route[:, LANE_GATE2:LANE_GATE2 + 1]
    moe = gate1 * _unpack_bf16_pairs(y1_ref[...]).astype(F32) + gate2 * _unpack_bf16_pairs(y2_ref[...]).astype(F32)
    out_ref[...] = _rms_norm(h_ref[...] + moe, gfin_ref[...])


def _final_call(h, y_pairs, route, g_final, earlier, first_token, tm):
    n_tok = h.shape[0]
    n_blocks = y_pairs.shape[0] // 2 // tm
    assert n_blocks * tm * 2 == y_pairs.shape[0] and first_token % tm == 0, "chunks must be whole row blocks"
    first = first_token // tm
    in_specs = [pl.BlockSpec((tm, D_MODEL), lambda i: (first + i, 0)),
                pl.BlockSpec((tm, D_MODEL // 2), lambda i: (i, 0)),
                pl.BlockSpec((tm, D_MODEL // 2), lambda i: (i + n_blocks, 0)),
                pl.BlockSpec((tm, ROUTER_LANES), lambda i: (first + i, 0)),
                pl.BlockSpec((1, D_MODEL), lambda i: (0, 0))]
    args = [h, y_pairs, y_pairs, route, g_final]
    aliases = {}
    if earlier is not None:
        in_specs.append(pl.BlockSpec(memory_space=pl.ANY))
        args.append(earlier)
        aliases = {len(args) - 1: 0}
    return pl.pallas_call(
        _final_kernel,
        grid=(n_blocks,),
        in_specs=in_specs,
        out_specs=pl.BlockSpec((tm, D_MODEL), lambda i: (first + i, 0)),
        out_shape=jax.ShapeDtypeStruct((n_tok, D_MODEL), F32),
        input_output_aliases=aliases,
        compiler_params=pltpu.CompilerParams(
            dimension_semantics=("arbitrary",), vmem_limit_bytes=VMEM_LIMIT),
        name="moe_combine_norm",
    )(*args)


def _dispatch_plan(route_t, counts, tm):
    n_tok = route_t.shape[1]
    n_tiles = (2 * n_tok) // tm + N_EXPERTS
    counts = counts[0, :N_EXPERTS].astype(jnp.int32)
    tiles_per = (counts + tm - 1) // tm
    tile_end = jnp.cumsum(tiles_per)
    row_start = (tile_end - tiles_per) * tm
    experts = jnp.arange(N_EXPERTS, dtype=jnp.int32)

    def position(idx_lane, rank_lane):
        idx = route_t[idx_lane].astype(jnp.int32)
        start = jnp.sum(jnp.where(idx[None, :] == experts[:, None], row_start[:, None], 0), axis=0)
        return start + route_t[rank_lane].astype(jnp.int32)

    pos = jnp.concatenate([position(LANE_IDX1, LANE_RANK1), position(LANE_IDX2, LANE_RANK2)])
    n_valid = tile_end[-1:]
    tile = jnp.minimum(jnp.arange(n_tiles, dtype=jnp.int32), n_valid - 1)
    tile_expert = jnp.sum((tile_end[None, :] <= tile[:, None]).astype(jnp.int32), axis=1)
    present = tiles_per > 0
    later = (experts[None, :] > experts[:, None]) & present[None, :]
    next_present = jnp.min(jnp.where(later, experts[None, :], N_EXPERTS), axis=1)
    order = jnp.cumsum(present.astype(jnp.int32)) - 1
    of_tile = lambda per_expert: jnp.sum(jnp.where(tile_expert[:, None] == experts[None, :], per_expert[None, :], 0), axis=1)
    return pos, n_tiles, tile_expert, n_valid, of_tile(order) % 2, of_tile(next_present)


def _block(x, norm_mix_g, w_in, rwkv_mu, conv_w, decay_up, decay_base, aaa_up, aaa_base, gate_up,
           k_k, k_a, r_k, ln_x_w, ln_x_b, w_out, norm_ffn_g, router_group_w, router_group_b,
           router_expert_w, router_expert_b, expert_w_gate, expert_w_up, expert_w_down, norm_final_g,
           *, tm_in, tb_rwkv, tm_out, tm_expert, tm_final):
    bsz, seq, d_model = x.shape
    n_tok = bsz * seq
    row = lambda p: p.reshape(1, -1)

    yconv, r, lw, k2, v, kk, kka, gate, bonus = _in_call(
        x, row(norm_mix_g), w_in, row(rwkv_mu), conv_w, decay_up, row(decay_base), aaa_up, row(aaa_base),
        gate_up, row(k_k), row(k_a), row(r_k), tm_in)
    o, w_gate_bf, w_up_bf, w_down_bf = _rwkv_call(r, lw, k2, v, kk, kka,
                                                  (expert_w_gate, expert_w_up, expert_w_down), tb_rwkv)

    pad = ROUTER_LANES - N_EXPERTS - N_GROUPS
    router_w = jnp.concatenate([router_expert_w, router_group_w, jnp.zeros((d_model, pad), F32)], axis=1)
    router_hi = _bf(router_w)
    router_split = jnp.concatenate([router_hi, _bf(router_w - router_hi.astype(F32))], axis=1)
    router_b = jnp.concatenate([router_expert_b, router_group_b, jnp.zeros((pad,), F32)]).reshape(1, -1)
    flat = lambda t: t.reshape(n_tok, t.shape[-1])
    h, u_pairs, route, route_t, counts = _out_call(
        flat(o), flat(bonus), flat(gate), flat(yconv), flat(x),
        row(ln_x_w), row(ln_x_b), w_out, row(norm_ffn_g), router_split, router_b, tm_out, OUT_SPLIT)

    pos, n_tiles, tile_expert, n_valid, slot, next_expert = _dispatch_plan(route_t, counts, tm_expert)
    x_sorted = _sc_scatter(u_pairs, pos, n_tiles * tm_expert)
    y_sorted = _expert_call(tile_expert, n_valid, slot, next_expert, x_sorted, w_gate_bf, w_up_bf, w_down_bf,
                            tm_expert)
    out, lo = None, 0
    for fraction in COMBINE_SPLIT:
        size = n_tok // fraction
        pos_c = jnp.concatenate([pos[lo:lo + size], pos[n_tok + lo:n_tok + lo + size]])
        out = _final_call(h, _sc_gather(y_sorted, pos_c), route, row(norm_final_g), out, lo, tm_final)
        lo += size
    assert lo == n_tok, "COMBINE_SPLIT must cover all tokens"
    return out.reshape(bsz, seq, d_model)


def kernel(x, norm_mix_g, w_in, rwkv_mu, conv_w, decay_up, decay_base, aaa_up, aaa_base, gate_up, k_k, k_a, r_k, ln_x_w, ln_x_b, w_out, norm_ffn_g, router_group_w, router_group_b, router_expert_w, router_expert_b, expert_w_gate, expert_w_up, expert_w_down, norm_final_g):
    return _block(x, norm_mix_g[0], w_in[0], rwkv_mu[0], conv_w[0], decay_up[0], decay_base[0],
                  aaa_up[0], aaa_base[0], gate_up[0], k_k[0], k_a[0], r_k[0].reshape(-1), ln_x_w[0],
                  ln_x_b[0], w_out[0], norm_ffn_g[0], router_group_w[0], router_group_b[0],
                  router_expert_w[0], router_expert_b[0], expert_w_gate[0], expert_w_up[0],
                  expert_w_down[0], norm_final_g,
                  tm_in=512, tb_rwkv=1024, tm_out=1024, tm_expert=512, tm_final=1024)
```

```python
import functools

import jax
import jax.numpy as jnp
from jax import lax
from jax.experimental import pallas as pl
from jax.experimental.pallas import tpu as pltpu
from jax.experimental.pallas import tpu_sc as plsc

F32 = jnp.float32
BF16 = jnp.bfloat16

D_MODEL = 1024
D_CONV = 512
CONV_WIDTH = 3
N_HEADS = 8
HEAD = 64
D_RWKV = N_HEADS * HEAD
LORA_WA = 128
GATE_LORA = 128
D_RWKV_PROJ = 3 * D_RWKV + LORA_WA + GATE_LORA
D_IN = 3 * D_CONV + D_RWKV_PROJ
N_GROUPS = 4
EXPERTS_PER_GROUP = 8
N_EXPERTS = N_GROUPS * EXPERTS_PER_GROUP
D_EXPERT = D_MODEL // 4
RMS_EPS = 1e-6
LN_X_EPS = 64e-5
L2_EPS = 1e-12

SUBLANES = 8
CHUNK = 64
QUAD = 4 * HEAD
GROUP = 2 * HEAD
INV_BASE = 8
WAVES = (8, 4, 4)
WAVE = 4
PRODUCTS_PER_CHAIN = 15
COMPACT_FROM = 64
OUT_SPLIT = 8
COMBINE_SPLIT = (4, 4, 4, 4)
ROUTER_LANES = 128

VMEM_LIMIT = 56 * 1024 * 1024


def _bf(x):
    return x.astype(BF16)


class _Interleaver:
    def __init__(self):
        self.stages, self.every, self.calls, self.busy = None, 1, 0, False

    def start(self, stages, every):
        self.stages, self.every, self.calls = stages, every, 0

    def tick(self):
        if self.stages is None or self.busy:
            return
        self.calls += 1
        if self.calls % self.every == 0:
            self.busy = True
            next(self.stages, None)
            self.busy = False

    def drain(self):
        self.busy = True
        for _ in self.stages:
            pass
        self.stages, self.busy = None, False


_INTERLEAVE = _Interleaver()


def _mm(a, b):
    _INTERLEAVE.tick()
    return jnp.dot(_bf(a), _bf(b), preferred_element_type=F32)


def _mm_nt(a, b):
    _INTERLEAVE.tick()
    return lax.dot_general(_bf(a), _bf(b), (((1,), (1,)), ((), ())), preferred_element_type=F32)


def _mm_exact_lhs(lhs_bf16, x, passes):
    acc = None
    rem = x
    for _ in range(passes):
        piece = _bf(rem)
        part = jnp.dot(lhs_bf16, piece, preferred_element_type=F32)
        acc = part if acc is None else acc + part
        rem = rem - piece.astype(F32)
    return acc


def _head_ones():
    return (lax.broadcasted_iota(jnp.int32, (QUAD, QUAD), 0) // HEAD
            == lax.broadcasted_iota(jnp.int32, (QUAD, QUAD), 1) // HEAD).astype(BF16)


def _head_sum(x, ones_quad):
    xb = _bf(x)
    return jnp.concatenate(
        [jnp.dot(xb[:, q * QUAD:(q + 1) * QUAD], ones_quad, preferred_element_type=F32)
         for q in range(x.shape[1] // QUAD)], axis=1)


def _rms_norm(x, g):
    return x * lax.rsqrt(jnp.mean(x * x, axis=-1, keepdims=True) + RMS_EPS) * g


def _shift_rows(cur, prev_rows, k):
    rolled = pltpu.roll(cur, k, 0)
    prev_rolled = pltpu.roll(prev_rows, k, 0)
    n = cur.shape[0]
    head = jnp.concatenate([prev_rolled, rolled[SUBLANES:]], axis=0) if n > SUBLANES else prev_rolled
    row = lax.broadcasted_iota(jnp.int32, cur.shape, 0)
    return jnp.where(row < k, head, rolled)


def _in_kernel(x_ref, g_ref, w_ref, mu_ref, convw_ref, dup_ref, dbase_ref, aup_ref, abase_ref,
               gup_ref, kk_ref, ka_ref, rk_ref,
               yconv_ref, r_ref, lw_ref, k_ref, v_ref, kkn_ref, kka_ref, gate_ref, bonus_ref,
               carry_ref, wbf_ref):
    @pl.when((pl.program_id(0) == 0) & (pl.program_id(1) == 0))
    def _():
        wbf_ref[...] = _bf(w_ref[...])

    @pl.when(pl.program_id(1) == 0)
    def _():
        carry_ref[...] = jnp.zeros_like(carry_ref)

    u = _bf(_rms_norm(x_ref[0], g_ref[...]))
    n_conv = 3 * D_CONV
    rk0 = n_conv
    bounds = dict(conv=(0, n_conv), lora=(rk0 + 3 * D_RWKV, D_IN), k=(rk0 + D_RWKV, rk0 + 2 * D_RWKV),
                  r=(rk0, rk0 + D_RWKV), v=(rk0 + 2 * D_RWKV, rk0 + 3 * D_RWKV))
    z = {name: jnp.dot(u, wbf_ref[:, lo:hi], preferred_element_type=F32) for name, (lo, hi) in bounds.items()}

    def lerp(name):
        lo, hi = bounds[name]
        cur = z[name]
        prev = carry_ref[:, D_CONV + lo - n_conv:D_CONV + hi - n_conv]
        mixed = cur + (_shift_rows(cur, prev, 1) - cur) * mu_ref[:, lo - n_conv:hi - n_conv]
        carry_ref[:, D_CONV + lo - n_conv:D_CONV + hi - n_conv] = cur[-SUBLANES:]
        return mixed

    b_gate = z["conv"][:, :D_CONV]
    ch = z["conv"][:, D_CONV:2 * D_CONV] * z["conv"][:, 2 * D_CONV:]
    prev_ch = carry_ref[:, :D_CONV]
    conv = convw_ref[CONV_WIDTH - 1:CONV_WIDTH, :] * ch
    for delay in range(1, CONV_WIDTH):
        tap = CONV_WIDTH - 1 - delay
        conv = conv + convw_ref[tap:tap + 1, :] * _shift_rows(ch, prev_ch, delay)
    yconv_ref[0] = _bf(b_gate * conv)
    carry_ref[:, :D_CONV] = ch[-SUBLANES:]

    lora = lerp("lora")
    wa_lo = lora[:, :LORA_WA]
    g_lo = lora[:, LORA_WA:]
    no_rows = jnp.zeros_like(dup_ref[...])
    decay_up = _bf(jnp.concatenate([dup_ref[...], no_rows], axis=0))
    rate_up = _bf(jnp.concatenate([no_rows, aup_ref[...]], axis=0))
    dec_in = -(dbase_ref[...] + jnp.dot(_bf(jnp.tanh(wa_lo)), decay_up, preferred_element_type=F32))
    softplus = jnp.maximum(dec_in, 0.0) + jnp.log(1.0 + jnp.exp(-jnp.abs(dec_in)))
    w = -softplus - 0.5
    lw_ref[0] = -jnp.exp(w)
    a = jax.nn.sigmoid(abase_ref[...] + jnp.dot(_bf(wa_lo), rate_up, preferred_element_type=F32))
    gate_ref[0] = _bf(jnp.dot(_bf(jax.nn.sigmoid(g_lo)), _bf(gup_ref[...]), preferred_element_type=F32))

    ones_quad = _head_ones()
    k = lerp("k")
    kk = k * kk_ref[...]
    norm = jnp.sqrt(_head_sum(kk * kk, ones_quad))
    kk = kk / jnp.maximum(norm, L2_EPS)
    k2 = k * (1.0 + (a - 1.0) * ka_ref[...])
    k_ref[0] = _bf(k2)
    kkn_ref[0] = _bf(kk)
    kka_ref[0] = _bf(kk * a)
    r = lerp("r")
    r_ref[0] = _bf(r)
    v = lerp("v")
    v_ref[0] = _bf(v)
    bonus_ref[0] = _bf(_head_sum(r * k2 * rk_ref[...], ones_quad) * v)


def _in_call(x, g, w_in, mu, conv_w, dup, dbase, aup, abase, gup, k_k, k_a, r_k, tm):
    bsz, seq, _ = x.shape
    full = lambda arr: pl.BlockSpec(arr.shape, lambda b, t: (0,) * arr.ndim, pipeline_mode=pl.Buffered(1))
    out_spec = pl.BlockSpec((1, tm, D_RWKV), lambda b, t: (b, t, 0))
    out_dtypes = (BF16, BF16, F32, BF16, BF16, BF16, BF16, BF16, BF16)
    params = (g, w_in, mu, conv_w, dup, dbase, aup, abase, gup, k_k, k_a, r_k)
    return pl.pallas_call(
        _in_kernel,
        grid=(bsz, seq // tm),
        in_specs=[pl.BlockSpec((1, tm, D_MODEL), lambda b, t: (b, t, 0))] + [full(p) for p in params],
        out_specs=[out_spec] * 9,
        out_shape=[jax.ShapeDtypeStruct((bsz, seq, D_RWKV), dt) for dt in out_dtypes],
        scratch_shapes=[pltpu.VMEM((SUBLANES, D_CONV + D_RWKV_PROJ), F32), pltpu.VMEM(w_in.shape, BF16)],
        compiler_params=pltpu.CompilerParams(
            dimension_semantics=("arbitrary", "arbitrary"), vmem_limit_bytes=VMEM_LIMIT),
        name="in_proj",
    )(x, *params)


def _block_diag(y, bd_mask):
    return jnp.where(bd_mask, jnp.concatenate([y] * (GROUP // CHUNK), axis=0), 0.0)


def _each(fn, *lists):
    return [fn(*args) for args in zip(*lists)]


def _unit_lower_inverse(a_strict, t_idx, s_idx, bd):
    bdmm = lambda xs, ys: _each(lambda x, y: _mm(x, bd(y)), xs, ys)
    eye = (t_idx == s_idx).astype(F32)
    same8 = (t_idx // INV_BASE) == (s_idx // INV_BASE)
    a8 = _each(lambda a: jnp.where(same8, a, 0.0), a_strict)
    a8_2 = bdmm(a8, a8)
    a8_34 = bdmm(_each(lambda a, b: jnp.concatenate([a, b], axis=0), a8, a8_2), a8_2)
    inv = _each(lambda a, b, c: eye + a + b + c[:CHUNK], a8, a8_2, a8_34)
    inv = _each(jnp.add, inv, bdmm(inv, _each(lambda c: c[CHUNK:], a8_34)))
    size = 2 * INV_BASE
    while size < COMPACT_FROM:
        off = ((t_idx // size) == (s_idx // size)) & ((t_idx // (size // 2)) != (s_idx // (size // 2)))
        cross = bdmm(_each(lambda a: jnp.where(off, a, 0.0), a_strict), inv)
        inv = _each(jnp.add, inv, bdmm(inv, cross))
        size *= 2
    while size <= CHUNK:
        inv = _compact_level(a_strict, inv, size)
        size *= 2
    return inv


def _compact_level(a_strict, inv, size):
    half = size // 2
    heads = GROUP // CHUNK
    blocks = CHUNK // size
    lane_starts = [h * CHUNK + b * size for h in range(heads) for b in range(blocks)]
    row_starts = [b * size for b in range(blocks)]
    rows_first = lambda x: jnp.concatenate([x[r:r + half] for r in row_starts], axis=0)
    rows_second = lambda x: jnp.concatenate([x[r + half:r + size] for r in row_starts], axis=0)
    lanes_first = lambda x: jnp.concatenate([x[:, c:c + half] for c in lane_starts], axis=1)
    lanes_second = lambda x: jnp.concatenate([x[:, c + half:c + size] for c in lane_starts], axis=1)
    n_rows, n_lanes = CHUNK // 2, GROUP // 2
    own = (lax.broadcasted_iota(jnp.int32, (n_rows, n_lanes), 0) // half
           == (lax.broadcasted_iota(jnp.int32, (n_rows, n_lanes), 1) // half) % blocks)
    diag = (lax.broadcasted_iota(jnp.int32, (n_lanes, n_lanes), 0) // half
            == lax.broadcasted_iota(jnp.int32, (n_lanes, n_lanes), 1) // half)
    bd_half = lambda y: jnp.where(diag, jnp.concatenate([y] * heads, axis=0), 0.0)
    a21 = _each(lambda a: jnp.where(own, lanes_first(rows_second(a)), 0.0), a_strict)
    t11 = _each(lambda t: lanes_first(rows_first(t)), inv)
    t22 = _each(lambda t: lanes_second(rows_second(t)), inv)
    a21_t11 = _each(lambda x, y: _mm(x, bd_half(y)), a21, t11)
    new = _each(lambda x, y: _mm(x, bd_half(y)), t22, a21_t11)

    def placed(x):
        zero_lanes = jnp.zeros((n_rows, half), F32)
        wide = jnp.concatenate(sum(([x[:, i * half:(i + 1) * half], zero_lanes] for i in range(heads * blocks)), []),
                               axis=1)
        zero_rows = jnp.zeros((half, GROUP), F32)
        return jnp.concatenate(sum(([zero_rows, wide[b * half:(b + 1) * half]] for b in range(blocks)), []), axis=0)

    return _each(lambda t, x: t + placed(x), inv, new)


def _chunk_local(r, lw, k, v, kk, kka, tri, t_idx, s_idx, bd_mask):
    bd = lambda y: _block_diag(y, bd_mask)
    bdmm = lambda xs, ys: _each(lambda x, y: _mm(x, bd(y)), xs, ys)
    bdmm2 = lambda xs, ys, zs: _each(lambda x, y, z: _mm(x, jnp.concatenate([bd(y), bd(z)], axis=1)), xs, ys, zs)
    left = lambda xs: _each(lambda x: x[:, :GROUP], xs)
    right = lambda xs: _each(lambda x: x[:, GROUP:], xs)
    top = lambda xs: _each(lambda x: x[:CHUNK], xs)
    bottom = lambda xs: _each(lambda x: x[CHUNK:], xs)

    cum = _each(lambda x: _mm_exact_lhs(tri, x, 3), lw)
    cum_last = _each(lambda c: c[CHUNK - 1:CHUNK, :], cum)
    p_incl = _each(jnp.exp, cum)
    p_excl = _each(lambda c, x: jnp.exp(c - x), cum, lw)
    p_inv = _each(lambda c: jnp.exp(-c), cum)
    to_end = _each(lambda cl, c: jnp.exp(cl - c), cum_last, cum)
    a_t = _each(lambda x, p: -x * p, kk, p_excl)
    r_t = _each(jnp.multiply, r, p_incl)
    b_t = _each(jnp.multiply, kka, p_inv)
    k_t = _each(jnp.multiply, k, p_inv)
    b_end = _each(jnp.multiply, kka, to_end)
    k_end = _each(jnp.multiply, k, to_end)

    ar = _each(lambda a, b: jnp.concatenate([a, b], axis=0), a_t, r_t)
    row2 = lax.broadcasted_iota(jnp.int32, (2 * CHUNK, 2 * GROUP), 0)
    col2 = lax.broadcasted_iota(jnp.int32, (2 * CHUNK, 2 * GROUP), 1) % CHUNK
    causal = col2 < (row2 % CHUNK) + (row2 // CHUNK)
    scores = _each(lambda x, y, z: jnp.where(causal, _mm_nt(x, jnp.concatenate([bd(y), bd(z)], axis=0)), 0.0),
                   ar, b_t, k_t)
    a_ab = top(left(scores))
    a_rb = bottom(left(scores))

    inv = _unit_lower_inverse(a_ab, t_idx, s_idx, bd)
    kv = bdmm(right(scores), v)
    wu = bdmm2(inv, a_t, top(kv))
    ro = bdmm2(a_rb, left(wu), right(wu))
    r_hat = _each(jnp.add, r_t, left(ro))
    o_loc = _each(jnp.add, right(ro), bottom(kv))

    wu_b = _each(lambda x, b: _mm(x.T, b), wu, b_end)
    trans = _each(lambda x: jnp.where(bd_mask, x[:GROUP], 0.0), wu_b)
    d_bd = _each(lambda x, y, z: jnp.where(bd_mask, x[GROUP:] + _mm(y.T, z), 0.0), wu_b, v, k_end)
    d_ls = _each(lambda d: sum(d[h * CHUNK:(h + 1) * CHUNK] for h in range(1, GROUP // CHUNK)) + d[:CHUNK], d_bd)
    p_end = _each(jnp.exp, cum_last)
    return r_hat, o_loc, trans, d_ls, p_end


def _rwkv_kernel(r_ref, lw_ref, k_ref, v_ref, kk_ref, kka_ref, wg_ref, wu_ref, wd_ref,
                 o_ref, wg_bf_ref, wu_bf_ref, wd_bf_ref, state_ref, *, n_sub):
    @pl.when(pl.program_id(1) == 0)
    def _():
        state_ref[...] = jnp.zeros_like(state_ref)

    wg_bf_ref[...] = _bf(wg_ref[...])
    wu_bf_ref[...] = _bf(wu_ref[...])
    wd_bf_ref[...] = _bf(wd_ref[...])

    t_idx = lax.broadcasted_iota(jnp.int32, (CHUNK, GROUP), 0)
    s_idx = lax.broadcasted_iota(jnp.int32, (CHUNK, GROUP), 1) % CHUNK
    bd_mask = (lax.broadcasted_iota(jnp.int32, (GROUP, GROUP), 0) // CHUNK
               == lax.broadcasted_iota(jnp.int32, (GROUP, GROUP), 1) // CHUNK)
    chunk_row = lax.broadcasted_iota(jnp.int32, (CHUNK, CHUNK), 0)
    chunk_col = lax.broadcasted_iota(jnp.int32, (CHUNK, CHUNK), 1)
    n_group = D_RWKV // GROUP
    tri = (chunk_row >= chunk_col).astype(BF16)
    state = [state_ref[g] for g in range(n_group)]

    def through_state(local, where):
        r_hat, o_loc, trans, d_ls, p_end = local
        for first in range(0, len(where), n_group):
            chains = range(first, first + n_group)
            out = [_mm_nt(r_hat[i], _block_diag(state[g], bd_mask)) + o_loc[i] for g, i in enumerate(chains)]
            for g, i in enumerate(chains):
                rows, lanes = where[i]
                o_ref[0, rows, lanes] = _bf(out[g])
            yield
            state[:] = [state[g] * p_end[i] + _mm(state[g], trans[i]) + d_ls[i] for g, i in enumerate(chains)]
            yield

    pending, start, stages = None, 0, 0
    waves = WAVES if sum(WAVES) == n_sub else [min(WAVE, n_sub - s) for s in range(0, n_sub, WAVE)]
    for wave in waves:
        where = [(slice(c * CHUNK, (c + 1) * CHUNK), slice(g * GROUP, (g + 1) * GROUP))
                 for c in range(start, start + wave) for g in range(n_group)]
        load = lambda ref: [ref[0, rows, lanes].astype(F32) for rows, lanes in where]
        if pending is not None:
            _INTERLEAVE.start(pending, len(where) * PRODUCTS_PER_CHAIN // (stages + 1))
        local = _chunk_local(load(r_ref), load(lw_ref), load(k_ref), load(v_ref), load(kk_ref), load(kka_ref),
                             tri, t_idx, s_idx, bd_mask)
        if pending is not None:
            _INTERLEAVE.drain()
        pending, start, stages = through_state(local, where), start + wave, 2 * wave
    for _ in pending:
        pass
    for g in range(n_group):
        state_ref[g] = state[g]


def _rwkv_call(r, lw, k, v, kk, kka, expert_weights, tb):
    bsz, seq, _ = r.shape
    n_t = seq // tb
    per_step = N_EXPERTS // (bsz * n_t)
    assert per_step * bsz * n_t == N_EXPERTS, "grid steps must divide the expert count"
    spec = pl.BlockSpec((1, tb, D_RWKV), lambda b, t: (b, t, 0))
    w_specs = [pl.BlockSpec((per_step,) + w.shape[1:], lambda b, t: (b * n_t + t, 0, 0)) for w in expert_weights]
    return pl.pallas_call(
        functools.partial(_rwkv_kernel, n_sub=tb // CHUNK),
        grid=(bsz, n_t),
        in_specs=[spec] * 6 + w_specs,
        out_specs=[spec] + w_specs,
        out_shape=[jax.ShapeDtypeStruct((bsz, seq, D_RWKV), BF16)]
        + [jax.ShapeDtypeStruct(w.shape, BF16) for w in expert_weights],
        scratch_shapes=[pltpu.VMEM((D_RWKV // GROUP, HEAD, GROUP), F32)],
        compiler_params=pltpu.CompilerParams(
            dimension_semantics=("arbitrary", "arbitrary"), vmem_limit_bytes=VMEM_LIMIT),
        name="rwkv_chunk",
    )(r, lw, k, v, kk, kka, *expert_weights)


def _route(logits):
    lane_i = lax.broadcasted_iota(jnp.int32, logits.shape, 1)
    lane = lane_i.astype(F32)
    lane_group = (lane_i // EXPERTS_PER_GROUP).astype(F32)
    neg = -jnp.inf
    big = float(ROUTER_LANES)
    is_group = (lane_i >= N_EXPERTS) & (lane_i < N_EXPERTS + N_GROUPS)
    gl = jnp.where(is_group, logits, neg)
    ge = jnp.exp(gl - jnp.max(gl, axis=-1, keepdims=True))
    gprob = ge / jnp.sum(ge, axis=-1, keepdims=True)
    g_top = jnp.max(gprob, axis=-1, keepdims=True)
    g_idx = jnp.min(jnp.where(is_group & (gprob == g_top), lane - N_EXPERTS, big), axis=-1, keepdims=True)

    in_group = (lane_i < N_EXPERTS) & (lane_group == g_idx)
    el = jnp.where(in_group, logits, neg)
    top1 = jnp.max(el, axis=-1, keepdims=True)
    idx1 = jnp.min(jnp.where(in_group & (el == top1), lane, big), axis=-1, keepdims=True)
    el2 = jnp.where(lane == idx1, neg, el)
    top2 = jnp.max(el2, axis=-1, keepdims=True)
    idx2 = jnp.min(jnp.where(in_group & (lane != idx1) & (el2 == top2), lane, big), axis=-1, keepdims=True)
    e2 = jnp.exp(top2 - top1)
    denom = 1.0 + e2
    return idx1, idx2, g_top * (1.0 / denom), g_top * (e2 / denom)


def _pack_bf16_pairs(x):
    bits = lax.bitcast_convert_type(_bf(x).astype(F32), jnp.uint32)
    k = x.shape[1] // 2
    return lax.bitcast_convert_type((bits[:, :k] >> 16) | bits[:, k:], jnp.int32)


def _unpack_bf16_pairs(words):
    bits = lax.bitcast_convert_type(words, jnp.uint32)
    lo = lax.bitcast_convert_type(bits << 16, F32)
    hi = lax.bitcast_convert_type(bits & jnp.uint32(0xFFFF0000), F32)
    return _bf(jnp.concatenate([lo, hi], axis=1))


LANE_IDX1, LANE_IDX2, LANE_RANK1, LANE_RANK2, LANE_GATE1, LANE_GATE2 = range(6)


def _out_kernel(o_ref, bonus_ref, gate_ref, yconv_ref, x_ref, lnw_ref, lnb_ref,
                wout_ref, gffn_ref, rw_ref, rb_ref, h_ref, u_ref, route_ref, route_t_ref, count_ref,
                seen_ref, wbf_ref, *, n_split):
    @pl.when(pl.program_id(0) == 0)
    def _():
        seen_ref[...] = jnp.zeros_like(seen_ref)
        wbf_ref[...] = _bf(wout_ref[...])

    sub = o_ref.shape[0] // n_split
    parts = [slice(s * sub, (s + 1) * sub) for s in range(n_split)]
    read = lambda ref: [ref[p, :] for p in parts]
    ones_quad = _head_ones()
    inv_n = 1.0 / HEAD
    o = _each(lambda v: v.astype(F32), read(o_ref))
    mean = _each(lambda v: _head_sum(v, ones_quad) * inv_n, o)
    cen = _each(jnp.subtract, o, mean)
    var = _each(lambda c: _head_sum(c * c, ones_quad) * inv_n, cen)
    on = _each(lambda c, v: c * lax.rsqrt(v + LN_X_EPS) * lnw_ref[...] + lnb_ref[...], cen, var)
    y_rwkv = _each(lambda a, b, g: (a + b) * g, on, read(bonus_ref), read(gate_ref))
    mixed = _each(lambda yc, yr: (jnp.dot(yc, wbf_ref[:D_CONV, :], preferred_element_type=F32)
                                  + jnp.dot(_bf(yr), wbf_ref[D_CONV:, :], preferred_element_type=F32)),
                  read(yconv_ref), y_rwkv)
    h = _each(jnp.add, read(x_ref), mixed)
    u = _each(lambda v: _rms_norm(v, gffn_ref[...]), h)
    u_hi = _each(_bf, u)
    u_lo = _each(lambda a, b: _bf(a - b.astype(F32)), u, u_hi)
    by_hi = _each(lambda a: jnp.dot(a, rw_ref[...], preferred_element_type=F32), u_hi)
    by_lo = _each(lambda a: jnp.dot(a, rw_ref[:, :ROUTER_LANES], preferred_element_type=F32), u_lo)
    logits = _each(lambda a, b: a[:, :ROUTER_LANES] + a[:, ROUTER_LANES:] + b + rb_ref[...], by_hi, by_lo)
    routed = _each(_route, logits)

    lane_i = lax.broadcasted_iota(jnp.int32, (sub, ROUTER_LANES), 1)
    lane = lane_i.astype(F32)
    hit1 = _each(lambda rt: lane == rt[0], routed)
    hit2 = _each(lambda rt: lane == rt[1], routed)
    both = _each(lambda a, b: a.astype(F32) + b.astype(F32), hit1, hit2)
    earlier = (lax.broadcasted_iota(jnp.int32, (sub, sub), 0) > lax.broadcasted_iota(jnp.int32, (sub, sub), 1))
    inside = _each(lambda b: jnp.dot(earlier.astype(BF16), _bf(b), preferred_element_type=F32), both)
    seen = seen_ref[...]
    for s, p in enumerate(parts):
        idx1, idx2, gate1, gate2 = routed[s]
        before = inside[s] + seen
        rank1 = jnp.sum(jnp.where(hit1[s], before, 0.0), axis=-1, keepdims=True)
        rank2 = jnp.sum(jnp.where(hit2[s], before, 0.0), axis=-1, keepdims=True)
        seen = seen + jnp.sum(both[s], axis=0, keepdims=True)
        route = jnp.zeros((sub, ROUTER_LANES), F32)
        for lane_id, col in ((LANE_IDX1, idx1), (LANE_IDX2, idx2), (LANE_RANK1, rank1),
                             (LANE_RANK2, rank2), (LANE_GATE1, gate1), (LANE_GATE2, gate2)):
            route = jnp.where(lane_i == lane_id, col, route)
        route_ref[p, :] = route
        route_t_ref[:, p] = route.T[:SUBLANES, :]
        h_ref[p, :] = _bf(h[s])
        u_ref[p, :] = _pack_bf16_pairs(u[s])
    seen_ref[...] = seen
    count_ref[...] = jnp.broadcast_to(seen, count_ref.shape)


def _out_call(o, bonus, gate, yconv, x, ln_w, ln_b, w_out, g_ffn, router_w, router_b, tm, n_split):
    n_tok = x.shape[0]
    row = lambda width: pl.BlockSpec((tm, width), lambda i: (i, 0))
    full = lambda arr: pl.BlockSpec(arr.shape, lambda i: (0,) * arr.ndim, pipeline_mode=pl.Buffered(1))
    params = (ln_w, ln_b, w_out, g_ffn, router_w, router_b)
    return pl.pallas_call(
        functools.partial(_out_kernel, n_split=n_split),
        grid=(n_tok // tm,),
        in_specs=[row(D_RWKV)] * 4 + [row(D_MODEL)] + [full(p) for p in params],
        out_specs=[row(D_MODEL), row(D_MODEL // 2), row(ROUTER_LANES),
                   pl.BlockSpec((SUBLANES, tm), lambda i: (0, i)),
                   pl.BlockSpec((SUBLANES, ROUTER_LANES), lambda i: (0, 0))],
        out_shape=[jax.ShapeDtypeStruct((n_tok, D_MODEL), BF16),
                   jax.ShapeDtypeStruct((n_tok, D_MODEL // 2), jnp.int32),
                   jax.ShapeDtypeStruct((n_tok, ROUTER_LANES), F32),
                   jax.ShapeDtypeStruct((SUBLANES, n_tok), F32),
                   jax.ShapeDtypeStruct((SUBLANES, ROUTER_LANES), F32)],
        scratch_shapes=[pltpu.VMEM((1, ROUTER_LANES), F32), pltpu.VMEM(w_out.shape, BF16)],
        compiler_params=pltpu.CompilerParams(
            dimension_semantics=("arbitrary",), vmem_limit_bytes=VMEM_LIMIT),
        name="out_proj_route",
    )(o, bonus, gate, yconv, x, *params)


SC_CORES = 2
SC_SUBCORES = 16
SC_ROWS = 64


def _sc_mesh():
    return plsc.VectorSubcoreMesh(core_axis_name="c", subcore_axis_name="s",
                                  num_cores=SC_CORES, num_subcores=SC_SUBCORES)


def _sc_worker():
    return lax.axis_index("s") * SC_CORES + lax.axis_index("c")


def _sc_gather(table, idx):
    n_rows = idx.shape[0]
    width = table.shape[1]
    n_chunks = n_rows // (SC_CORES * SC_SUBCORES * SC_ROWS)

    def body(table_hbm, idx_hbm, out_hbm, idx_v, rows_v, gather_sem, write_sem):
        first = _sc_worker() * n_chunks
        pltpu.sync_copy(idx_hbm.at[pl.ds(first, n_chunks)], idx_v)
        gather = lambda j: pltpu.async_copy(table_hbm.at[idx_v.at[j]], rows_v.at[j % 2], gather_sem.at[j % 2])
        gathers = [gather(0)]
        writes = []
        for j in range(n_chunks):
            gathers[j].wait()
            if j + 1 < n_chunks:
                if j >= 1:
                    writes[j - 1].wait()
                gathers.append(gather(j + 1))
            dst = out_hbm.at[pl.ds(pl.multiple_of((first + j) * SC_ROWS, SC_ROWS), SC_ROWS)]
            writes.append(pltpu.async_copy(rows_v.at[j % 2], dst, write_sem.at[j % 2]))
        for j in range(max(n_chunks - 2, 0), n_chunks):
            writes[j].wait()

    return pl.kernel(
        body,
        out_type=jax.ShapeDtypeStruct((n_rows, width), table.dtype),
        mesh=_sc_mesh(),
        scratch_types=[pltpu.VMEM((n_chunks, SC_ROWS), jnp.int32), pltpu.VMEM((2, SC_ROWS, width), table.dtype),
                       pltpu.SemaphoreType.DMA((2,)), pltpu.SemaphoreType.DMA((2,))],
        name="sc_row_gather",
    )(table, idx.reshape(n_rows // SC_ROWS, SC_ROWS))


def _sc_scatter(rows, pos, n_out):
    n_rows, width = rows.shape
    n_slots = pos.shape[0] // n_rows
    slot_chunks = n_rows // SC_ROWS
    n_chunks = slot_chunks // (SC_CORES * SC_SUBCORES)

    def body(rows_hbm, pos_hbm, out_hbm, idx_v, rows_v, read_sem, scatter_sem):
        first = _sc_worker() * n_chunks
        for s in range(n_slots):
            pltpu.sync_copy(pos_hbm.at[pl.ds(s * slot_chunks + first, n_chunks)], idx_v.at[s])
        read = lambda j: pltpu.async_copy(
            rows_hbm.at[pl.ds(pl.multiple_of((first + j) * SC_ROWS, SC_ROWS), SC_ROWS)],
            rows_v.at[j % 2], read_sem.at[j % 2])
        reads = [read(0)]
        scatters = []
        for j in range(n_chunks):
            reads[j].wait()
            if j + 1 < n_chunks:
                if j >= 1:
                    for copy in scatters[j - 1]:
                        copy.wait()
                reads.append(read(j + 1))
            scatters.append([pltpu.async_copy(rows_v.at[j % 2], out_hbm.at[idx_v.at[s, j]], scatter_sem.at[j % 2])
                             for s in range(n_slots)])
        for j in range(max(n_chunks - 2, 0), n_chunks):
            for copy in scatters[j]:
                copy.wait()

    return pl.kernel(
        body,
        out_type=jax.ShapeDtypeStruct((n_out, width), rows.dtype),
        mesh=_sc_mesh(),
        scratch_types=[pltpu.VMEM((n_slots, n_chunks, SC_ROWS), jnp.int32),
                       pltpu.VMEM((2, SC_ROWS, width), rows.dtype),
                       pltpu.SemaphoreType.DMA((2,)), pltpu.SemaphoreType.DMA((2,))],
        name="sc_row_scatter",
    )(rows, pos.reshape(n_slots * slot_chunks, SC_ROWS))


def _expert_kernel(tile_expert_ref, n_valid_ref, slot_ref, next_expert_ref, x_ref, wg_hbm, wu_hbm, wd_hbm,
                   y_ref, wg_buf, wu_buf, wd_buf, sem):
    step = pl.program_id(0)
    expert = tile_expert_ref[step]
    slot = slot_ref[step]
    valid = step < n_valid_ref[0]
    first_of_expert = (step == 0) | (expert != tile_expert_ref[jnp.maximum(step - 1, 0)])

    def weight_copies(which, into):
        pairs = ((wg_hbm, wg_buf), (wu_hbm, wu_buf), (wd_hbm, wd_buf))
        return [pltpu.make_async_copy(w.at[which], buf.at[into], sem.at[into, j]) for j, (w, buf) in enumerate(pairs)]

    @pl.when(step == 0)
    def _():
        for copy in weight_copies(expert, slot):
            copy.start()

    @pl.when(valid & first_of_expert)
    def _():
        for copy in weight_copies(expert, slot):
            copy.wait()
        upcoming = next_expert_ref[step]

        @pl.when(upcoming < N_EXPERTS)
        def _():
            for copy in weight_copies(upcoming, 1 - slot):
                copy.start(priority=1)

    @pl.when(valid)
    def _():
        x = _unpack_bf16_pairs(x_ref[...])
        gate = jnp.dot(x, wg_buf[slot], preferred_element_type=F32)
        up = jnp.dot(x, wu_buf[slot], preferred_element_type=F32)
        hid = gate * jax.nn.sigmoid(gate) * up
        y_ref[...] = _pack_bf16_pairs(jnp.dot(_bf(hid), wd_buf[slot], preferred_element_type=F32))


def _expert_call(tile_expert, n_valid, slot, next_expert, x_sorted, w_gate, w_up, w_down, tm):
    n_rows = x_sorted.shape[0]
    rows = pl.BlockSpec((tm, D_MODEL // 2), lambda i, te, nv, sl, nx: (jnp.minimum(i, nv[0] - 1), 0))
    in_hbm = pl.BlockSpec(memory_space=pl.ANY)
    return pl.pallas_call(
        _expert_kernel,
        grid_spec=pltpu.PrefetchScalarGridSpec(
            num_scalar_prefetch=4,
            grid=(n_rows // tm,),
            in_specs=[rows, in_hbm, in_hbm, in_hbm],
            out_specs=rows,
            scratch_shapes=[pltpu.VMEM((2,) + w_gate.shape[1:], BF16), pltpu.VMEM((2,) + w_up.shape[1:], BF16),
                            pltpu.VMEM((2,) + w_down.shape[1:], BF16), pltpu.SemaphoreType.DMA((2, 3))]),
        out_shape=jax.ShapeDtypeStruct((n_rows, D_MODEL // 2), jnp.int32),
        compiler_params=pltpu.CompilerParams(
            dimension_semantics=("arbitrary",), vmem_limit_bytes=VMEM_LIMIT),
        name="moe_experts",
    )(tile_expert, n_valid, slot, next_expert, x_sorted, w_gate, w_up, w_down)


def _final_kernel(h_ref, y1_ref, y2_ref, route_ref, gfin_ref, *rest):
    out_ref = rest[-1]
    route = route_ref[...]
    gate1 = route[:, LANE_GATE1:LANE_GATE1 + 1]
    gate2 = route[:, LANE_GATE2:LANE_GATE2 + 1]
    moe = gate1 * _unpack_bf16_pairs(y1_ref[...]).astype(F32) + gate2 * _unpack_bf16_pairs(y2_ref[...]).astype(F32)
    out_ref[...] = _rms_norm(h_ref[...] + moe, gfin_ref[...])


def _final_call(h, y_pairs, route, g_final, earlier, first_token, tm):
    n_tok = h.shape[0]
    n_blocks = y_pairs.shape[0] // 2 // tm
    assert n_blocks * tm * 2 == y_pairs.shape[0] and first_token % tm == 0, "chunks must be whole row blocks"
    first = first_token // tm
    in_specs = [pl.BlockSpec((tm, D_MODEL), lambda i: (first + i, 0)),
                pl.BlockSpec((tm, D_MODEL // 2), lambda i: (i, 0)),
                pl.BlockSpec((tm, D_MODEL // 2), lambda i: (i + n_blocks, 0)),
                pl.BlockSpec((tm, ROUTER_LANES), lambda i: (first + i, 0)),
                pl.BlockSpec((1, D_MODEL), lambda i: (0, 0))]
    args = [h, y_pairs, y_pairs, route, g_final]
    aliases = {}
    if earlier is not None:
        in_specs.append(pl.BlockSpec(memory_space=pl.ANY))
        args.append(earlier)
        aliases = {len(args) - 1: 0}
    return pl.pallas_call(
        _final_kernel,
        grid=(n_blocks,),
        in_specs=in_specs,
        out_specs=pl.BlockSpec((tm, D_MODEL), lambda i: (first + i, 0)),
        out_shape=jax.ShapeDtypeStruct((n_tok, D_MODEL), F32),
        input_output_aliases=aliases,
        compiler_params=pltpu.CompilerParams(
            dimension_semantics=("arbitrary",), vmem_limit_bytes=VMEM_LIMIT),
        name="moe_combine_norm",
    )(*args)


def _dispatch_plan(route_t, counts, tm):
    n_tok = route_t.shape[1]
    n_tiles = (2 * n_tok) // tm + N_EXPERTS
    counts = counts[0, :N_EXPERTS].astype(jnp.int32)
    tiles_per = (counts + tm - 1) // tm
    tile_end = jnp.cumsum(tiles_per)
    row_start = (tile_end - tiles_per) * tm
    experts = jnp.arange(N_EXPERTS, dtype=jnp.int32)

    def position(idx_lane, rank_lane):
        idx = route_t[idx_lane].astype(jnp.int32)
        start = jnp.sum(jnp.where(idx[None, :] == experts[:, None], row_start[:, None], 0), axis=0)
        return start + route_t[rank_lane].astype(jnp.int32)

    pos = jnp.concatenate([position(LANE_IDX1, LANE_RANK1), position(LANE_IDX2, LANE_RANK2)])
    n_valid = tile_end[-1:]
    tile = jnp.minimum(jnp.arange(n_tiles, dtype=jnp.int32), n_valid - 1)
    tile_expert = jnp.sum((tile_end[None, :] <= tile[:, None]).astype(jnp.int32), axis=1)
    present = tiles_per > 0
    later = (experts[None, :] > experts[:, None]) & present[None, :]
    next_present = jnp.min(jnp.where(later, experts[None, :], N_EXPERTS), axis=1)
    order = jnp.cumsum(present.astype(jnp.int32)) - 1
    of_tile = lambda per_expert: jnp.sum(jnp.where(tile_expert[:, None] == experts[None, :], per_expert[None, :], 0), axis=1)
    return pos, n_tiles, tile_expert, n_valid, of_tile(order) % 2, of_tile(next_present)


def _block(x, norm_mix_g, w_in, rwkv_mu, conv_w, decay_up, decay_base, aaa_up, aaa_base, gate_up,
           k_k, k_a, r_k, ln_x_w, ln_x_b, w_out, norm_ffn_g, router_group_w, router_group_b,
           router_expert_w, router_expert_b, expert_w_gate, expert_w_up, expert_w_down, norm_final_g,
           *, tm_in, tb_rwkv, tm_out, tm_expert, tm_final):
    bsz, seq, d_model = x.shape
    n_tok = bsz * seq
    row = lambda p: p.reshape(1, -1)

    yconv, r, lw, k2, v, kk, kka, gate, bonus = _in_call(
        x, row(norm_mix_g), w_in, row(rwkv_mu), conv_w, decay_up, row(decay_base), aaa_up, row(aaa_base),
        gate_up, row(k_k), row(k_a), row(r_k), tm_in)
    o, w_gate_bf, w_up_bf, w_down_bf = _rwkv_call(r, lw, k2, v, kk, kka,
                                                  (expert_w_gate, expert_w_up, expert_w_down), tb_rwkv)

    pad = ROUTER_LANES - N_EXPERTS - N_GROUPS
    router_w = jnp.concatenate([router_expert_w, router_group_w, jnp.zeros((d_model, pad), F32)], axis=1)
    router_hi = _bf(router_w)
    router_split = jnp.concatenate([router_hi, _bf(router_w - router_hi.astype(F32))], axis=1)
    router_b = jnp.concatenate([router_expert_b, router_group_b, jnp.zeros((pad,), F32)]).reshape(1, -1)
    flat = lambda t: t.reshape(n_tok, t.shape[-1])
    h, u_pairs, route, route_t, counts = _out_call(
        flat(o), flat(bonus), flat(gate), flat(yconv), flat(x),
        row(ln_x_w), row(ln_x_b), w_out, row(norm_ffn_g), router_split, router_b, tm_out, OUT_SPLIT)

    pos, n_tiles, tile_expert, n_valid, slot, next_expert = _dispatch_plan(route_t, counts, tm_expert)
    x_sorted = _sc_scatter(u_pairs, pos, n_tiles * tm_expert)
    y_sorted = _expert_call(tile_expert, n_valid, slot, next_expert, x_sorted, w_gate_bf, w_up_bf, w_down_bf,
                            tm_expert)
    out, lo = None, 0
    for fraction in COMBINE_SPLIT:
        size = n_tok // fraction
        pos_c = jnp.concatenate([pos[lo:lo + size], pos[n_tok + lo:n_tok + lo + size]])
        out = _final_call(h, _sc_gather(y_sorted, pos_c), route, row(norm_final_g), out, lo, tm_final)
        lo += size
    assert lo == n_tok, "COMBINE_SPLIT must cover all tokens"
    return out.reshape(bsz, seq, d_model)


def kernel(x, norm_mix_g, w_in, rwkv_mu, conv_w, decay_up, decay_base, aaa_up, aaa_base, gate_up, k_k, k_a, r_k, ln_x_w, ln_x_b, w_out, norm_ffn_g, router_group_w, router_group_b, router_expert_w, router_expert_b, expert_w_gate, expert_w_up, expert_w_down, norm_final_g):
    return _block(x, norm_mix_g[0], w_in[0], rwkv_mu[0], conv_w[0], decay_up[0], decay_base[0],
                  aaa_up[0], aaa_base[0], gate_up[0], k_k[0], k_a[0], r_k[0].reshape(-1), ln_x_w[0],
                  ln_x_b[0], w_out[0], norm_ffn_g[0], router_group_w[0], router_group_b[0],
                  router_expert_w[0], router_expert_b[0], expert_w_gate[0], expert_w_up[0],
                  expert_w_down[0], norm_final_g,
                  tm_in=512, tb_rwkv=1024, tm_out=1024, tm_expert=512, tm_final=1024)
```
